```python
import jax, jax.numpy as jnp
from jax import lax
import numpy as np

D_MODEL = 1024
BATCH = 8
SEQ = 4096
DEPTH = 2

N_META = 16
BLK = 128
PAD = BLK - N_META
ROPE_THETA = 10000.0
EPS = 1e-6
NEG = -1e30

FOX_HEADS = 8
FOX_DH = 64

MLA_HEADS = 8
MLA_NOPE = 64
MLA_ROPE = 32
MLA_V = 64
MLA_QLORA = 384
MLA_KVLORA = 256

SWA_HEADS = 8
SWA_KV_HEADS = 2
SWA_DH = 64
WINDOW = 128

BRANCH_W = 512
N_BRANCH = 3

SPLIT_SIZES = (
    FOX_HEADS * FOX_DH, FOX_HEADS * FOX_DH, FOX_HEADS * FOX_DH, FOX_HEADS, BRANCH_W,
    MLA_QLORA, MLA_KVLORA, MLA_ROPE, BRANCH_W,
    SWA_HEADS * SWA_DH, SWA_KV_HEADS * SWA_DH, SWA_KV_HEADS * SWA_DH, BRANCH_W,
    N_BRANCH * D_MODEL,
)
N_IN = sum(SPLIT_SIZES)

kernel_name = "hybrid_fox_mla_swa_gated_branches"


def rmsnorm(x, g):
    xf = x.astype(jnp.float32)
    y = xf * lax.rsqrt(jnp.mean(xf * xf, axis=-1, keepdims=True) + EPS)
    return (y * g.astype(jnp.float32)).astype(x.dtype)


def rope(x, pos):
    half = x.shape[-1] // 2
    inv = ROPE_THETA ** (-jnp.arange(half, dtype=jnp.float32) / half)
    ang = pos.astype(jnp.float32)[:, None] * inv[None, :]
    cos = jnp.cos(ang)[None, :, None, :]
    sin = jnp.sin(ang)[None, :, None, :]
    xf = x.astype(jnp.float32)
    x1, x2 = xf[..., :half], xf[..., half:]
    return jnp.concatenate([x1 * cos - x2 * sin, x2 * cos + x1 * sin], axis=-1).astype(x.dtype)


def causal_block_attention(q, k, v, scale, log_cum=None):
    B, L, H, _ = q.shape
    nb = L // BLK
    kpos = jnp.arange(L)
    key_ok = kpos >= PAD
    qb = jnp.moveaxis(q.reshape(B, nb, BLK, H, q.shape[-1]), 1, 0)
    if log_cum is None:
        xs = (jnp.arange(nb), qb)
        ck = None
    else:
        cb = jnp.moveaxis(log_cum.reshape(B, nb, BLK, H), 1, 0)
        xs = (jnp.arange(nb), qb, cb)
        ck = jnp.swapaxes(log_cum, 1, 2)

    def block(args):
        i, qi = args[0], args[1]
        s = jnp.einsum('bqhd,bkhd->bhqk', qi, k).astype(jnp.float32) * scale
        if log_cum is not None:
            ci = jnp.swapaxes(args[2], 1, 2)
            s = s + (ci[..., :, None] - ck[..., None, :])
        qpos = i * BLK + jnp.arange(BLK)
        mask = (kpos[None, :] <= qpos[:, None]) & key_ok[None, :]
        s = jnp.where(mask, s, NEG)
        p = jax.nn.softmax(s, axis=-1)
        return jnp.einsum('bhqk,bkhd->bqhd', p.astype(v.dtype), v)

    o = lax.map(block, xs)
    return jnp.moveaxis(o, 0, 1).reshape(B, L, H, v.shape[-1])


def sliding_window_sink_attention(q, k, v, sinks):
    B, L, H, D = q.shape
    Hkv = k.shape[2]
    G = H // Hkv
    nb = L // BLK
    qb = q.reshape(B, nb, BLK, Hkv, G, D)

    def with_prev(t):
        prev = jnp.concatenate([jnp.zeros_like(t[:, :1]), t[:, :-1]], axis=1)
        return jnp.concatenate([prev, t], axis=2)

    kx = with_prev(k.reshape(B, nb, BLK, Hkv, D))
    vx = with_prev(v.reshape(B, nb, BLK, Hkv, D))
    s = jnp.einsum('bnqhgd,bnkhd->bnhgqk', qb, kx).astype(jnp.float32) * (D ** -0.5)
    blocks = jnp.arange(nb)[:, None]
    qpos = blocks * BLK + jnp.arange(BLK)[None, :]
    kpos = (blocks - 1) * BLK + jnp.arange(2 * BLK)[None, :]
    rel = qpos[:, :, None] - kpos[:, None, :]
    mask = (rel >= 0) & (rel < WINDOW) & (kpos >= PAD)[:, None, :]
    s = jnp.where(mask[None, :, None, None], s, NEG)
    sink = jnp.broadcast_to(sinks.astype(jnp.float32).reshape(1, 1, Hkv, G, 1, 1), s.shape[:-1] + (1,))
    p = jax.nn.softmax(jnp.concatenate([s, sink], axis=-1), axis=-1)[..., :-1]
    o = jnp.einsum('bnhgqk,bnkhd->bnqhgd', p.astype(v.dtype), vx)
    return o.reshape(B, L, H, D)


def hybrid_layer(x, pos, norm_g, w_in, b_f, g_cq, g_ckv, w_uq, w_ukv, sinks, w_branch, w_out):
    B, L, _ = x.shape
    h = rmsnorm(x, norm_g)
    proj = h @ w_in
    (a_q, a_k, a_v, a_f, a_z,
     b_cq, b_ckv, b_kr, b_z,
     c_q, c_k, c_v, c_z, gates) = jnp.split(proj, np.cumsum(SPLIT_SIZES)[:-1], axis=-1)

    log_f = jax.nn.log_sigmoid((a_f + b_f).astype(jnp.float32))
    log_cum = jnp.cumsum(log_f, axis=1)
    y_a = causal_block_attention(a_q.reshape(B, L, FOX_HEADS, FOX_DH),
                                 a_k.reshape(B, L, FOX_HEADS, FOX_DH),
                                 a_v.reshape(B, L, FOX_HEADS, FOX_DH),
                                 FOX_DH ** -0.5, log_cum)

    cq = rmsnorm(b_cq, g_cq)
    ckv = rmsnorm(b_ckv, g_ckv)
    qB = (cq @ w_uq).reshape(B, L, MLA_HEADS, MLA_NOPE + MLA_ROPE)
    q_b = jnp.concatenate([qB[..., :MLA_NOPE], rope(qB[..., MLA_NOPE:], pos)], axis=-1)
    kvB = (ckv @ w_ukv).reshape(B, L, MLA_HEADS, MLA_NOPE + MLA_V)
    k_rope = rope(b_kr.reshape(B, L, 1, MLA_ROPE), pos)
    k_b = jnp.concatenate([kvB[..., :MLA_NOPE],
                           jnp.broadcast_to(k_rope, (B, L, MLA_HEADS, MLA_ROPE))], axis=-1)
    v_b = kvB[..., MLA_NOPE:]
    y_b = causal_block_attention(q_b, k_b, v_b, (MLA_NOPE + MLA_ROPE) ** -0.5)

    qc = rope(c_q.reshape(B, L, SWA_HEADS, SWA_DH), pos)
    kc = rope(c_k.reshape(B, L, SWA_KV_HEADS, SWA_DH), pos)
    vc = c_v.reshape(B, L, SWA_KV_HEADS, SWA_DH)
    y_c = sliding_window_sink_attention(qc, kc, vc, sinks)

    branches = jnp.stack([y_a.reshape(B, L, BRANCH_W) * jax.nn.silu(a_z),
                          y_b.reshape(B, L, BRANCH_W) * jax.nn.silu(b_z),
                          y_c.reshape(B, L, BRANCH_W) * jax.nn.silu(c_z)], axis=2)
    proj_br = jnp.einsum('blnw,nwd->blnd', branches, w_branch)
    g = jax.nn.sigmoid(gates.reshape(B, L, N_BRANCH, D_MODEL))
    merged = jnp.sum(g * proj_br, axis=2)
    return x + merged @ w_out


def _fwd_setup_inputs(seed: int = 0) -> dict:
    key = jax.random.key(seed)
    ks = jax.random.split(key, 14)
    f32 = jnp.float32
    nrm = lambda k, shape, scale: jax.random.normal(k, shape, f32) * scale
    return {
        "x": nrm(ks[0], (BATCH, SEQ, D_MODEL), 1.0),
        "meta_tokens": nrm(ks[1], (N_META, D_MODEL), 1.0),
        "norm_g": 1.0 + nrm(ks[2], (DEPTH, D_MODEL), 0.02),
        "w_in": nrm(ks[3], (DEPTH, D_MODEL, N_IN), D_MODEL ** -0.5),
        "b_f": jax.random.uniform(ks[4], (DEPTH, FOX_HEADS), f32, 1.0, 4.0),
        "g_cq": 1.0 + nrm(ks[5], (DEPTH, MLA_QLORA), 0.02),
        "g_ckv": 1.0 + nrm(ks[6], (DEPTH, MLA_KVLORA), 0.02),
        "w_uq": nrm(ks[7], (DEPTH, MLA_QLORA, MLA_HEADS * (MLA_NOPE + MLA_ROPE)), MLA_QLORA ** -0.5),
        "w_ukv": nrm(ks[8], (DEPTH, MLA_KVLORA, MLA_HEADS * (MLA_NOPE + MLA_V)), MLA_KVLORA ** -0.5),
        "sinks": nrm(ks[9], (DEPTH, SWA_HEADS), 0.5),
        "w_branch": nrm(ks[10], (DEPTH, N_BRANCH, BRANCH_W, D_MODEL), BRANCH_W ** -0.5),
        "w_out": nrm(ks[11], (DEPTH, D_MODEL, D_MODEL), D_MODEL ** -0.5),
        "final_g": 1.0 + nrm(ks[12], (D_MODEL,), 0.02),
    }


def _fwd_reference(x, meta_tokens, norm_g, w_in, b_f, g_cq, g_ckv, w_uq, w_ukv, sinks, w_branch, w_out, final_g):
    B = x.shape[0]
    pad = jnp.zeros((B, PAD, D_MODEL), x.dtype)
    meta = jnp.broadcast_to(meta_tokens.astype(x.dtype)[None], (B, N_META, D_MODEL))
    h = jnp.concatenate([pad, meta, x], axis=1)
    pos = jnp.arange(h.shape[1]) - PAD
    for l in range(DEPTH):
        h = hybrid_layer(h, pos, norm_g[l], w_in[l], b_f[l], g_cq[l], g_ckv[l],
                         w_uq[l], w_ukv[l], sinks[l], w_branch[l], w_out[l])
    h = rmsnorm(h, final_g)
    return h[:, BLK:]


import jax as _jax
import jax.numpy as _jnp

TWIN_FORMAT = 'train_step'
FWD_PARAMS = ['x', 'meta_tokens', 'norm_g', 'w_in', 'b_f', 'g_cq', 'g_ckv', 'w_uq', 'w_ukv', 'sinks', 'w_branch', 'w_out', 'final_g']
TWIN_WEIGHTS = ['meta_tokens', 'norm_g', 'w_in', 'b_f', 'g_cq', 'g_ckv', 'w_uq', 'w_ukv', 'sinks', 'w_branch', 'w_out', 'final_g']
TWIN_DIFF_INPUT = 'x'
TWIN_INPUTS = ['x', 'meta_tokens', 'norm_g', 'w_in', 'b_f', 'g_cq', 'g_ckv', 'w_uq', 'w_ukv', 'sinks', 'w_branch', 'w_out', 'final_g', 'loss_target', 'm_meta_tokens', 'm_norm_g', 'm_w_in', 'm_b_f', 'm_g_cq', 'm_g_ckv', 'm_w_uq', 'm_w_ukv', 'm_sinks', 'm_w_branch', 'm_w_out', 'm_final_g', 'v_meta_tokens', 'v_norm_g', 'v_w_in', 'v_b_f', 'v_g_cq', 'v_g_ckv', 'v_w_uq', 'v_w_ukv', 'v_sinks', 'v_w_branch', 'v_w_out', 'v_final_g']
TWIN_OUTPUTS = ['loss', 'grad_x', 'grad_meta_tokens', 'grad_norm_g', 'grad_w_in', 'grad_b_f', 'grad_g_cq', 'grad_g_ckv', 'grad_w_uq', 'grad_w_ukv', 'grad_sinks', 'grad_w_branch', 'grad_w_out', 'grad_final_g', 'delta_meta_tokens', 'delta_norm_g', 'delta_w_in', 'delta_b_f', 'delta_g_cq', 'delta_g_ckv', 'delta_w_uq', 'delta_w_ukv', 'delta_sinks', 'delta_w_branch', 'delta_w_out', 'delta_final_g', 'new_m_meta_tokens', 'new_m_norm_g', 'new_m_w_in', 'new_m_b_f', 'new_m_g_cq', 'new_m_g_ckv', 'new_m_w_uq', 'new_m_w_ukv', 'new_m_sinks', 'new_m_w_branch', 'new_m_w_out', 'new_m_final_g', 'new_v_meta_tokens', 'new_v_norm_g', 'new_v_w_in', 'new_v_b_f', 'new_v_g_cq', 'new_v_g_ckv', 'new_v_w_uq', 'new_v_w_ukv', 'new_v_sinks', 'new_v_w_branch', 'new_v_w_out', 'new_v_final_g']
TWIN_LEAF_KINDS = {'loss': 'loss', 'grad_x': 'grad_x', 'grad_meta_tokens': 'grad_w', 'grad_norm_g': 'grad_w', 'grad_w_in': 'grad_w', 'grad_b_f': 'grad_w', 'grad_g_cq': 'grad_w', 'grad_g_ckv': 'grad_w', 'grad_w_uq': 'grad_w', 'grad_w_ukv': 'grad_w', 'grad_sinks': 'grad_w', 'grad_w_branch': 'grad_w', 'grad_w_out': 'grad_w', 'grad_final_g': 'grad_w', 'delta_meta_tokens': 'delta_w', 'delta_norm_g': 'delta_w', 'delta_w_in': 'delta_w', 'delta_b_f': 'delta_w', 'delta_g_cq': 'delta_w', 'delta_g_ckv': 'delta_w', 'delta_w_uq': 'delta_w', 'delta_w_ukv': 'delta_w', 'delta_sinks': 'delta_w', 'delta_w_branch': 'delta_w', 'delta_w_out': 'delta_w', 'delta_final_g': 'delta_w', 'new_m_meta_tokens': 'new_m', 'new_m_norm_g': 'new_m', 'new_m_w_in': 'new_m', 'new_m_b_f': 'new_m', 'new_m_g_cq': 'new_m', 'new_m_g_ckv': 'new_m', 'new_m_w_uq': 'new_m', 'new_m_w_ukv': 'new_m', 'new_m_sinks': 'new_m', 'new_m_w_branch': 'new_m', 'new_m_w_out': 'new_m', 'new_m_final_g': 'new_m', 'new_v_meta_tokens': 'new_v', 'new_v_norm_g': 'new_v', 'new_v_w_in': 'new_v', 'new_v_b_f': 'new_v', 'new_v_g_cq': 'new_v', 'new_v_g_ckv': 'new_v', 'new_v_w_uq': 'new_v', 'new_v_w_ukv': 'new_v', 'new_v_sinks': 'new_v', 'new_v_w_branch': 'new_v', 'new_v_w_out': 'new_v', 'new_v_final_g': 'new_v'}


def _forward(args):
    return _fwd_reference(*[args[k] for k in FWD_PARAMS])


def _output_shape():
    def fwd():
        inp = _fwd_setup_inputs(0)
        return _fwd_reference(*[inp[k] for k in FWD_PARAMS])
    out = _jax.eval_shape(fwd)
    return out.shape, out.dtype

N_MICROBATCH = 1
ADAM_LR = 0.001
ADAM_B1 = 0.9
ADAM_B2 = 0.999
ADAM_EPS = 1e-08
ADAM_WD = 0.01
ADAM_STEP = 10
PER_EXAMPLE_BATCH_AXIS = {'x': 0, 'loss_target': 0}
SHARED_INPUTS = []
_WEIGHT_DTYPES = {'meta_tokens': _jnp.float32, 'norm_g': _jnp.float32, 'w_in': _jnp.float32, 'b_f': _jnp.float32, 'g_cq': _jnp.float32, 'g_ckv': _jnp.float32, 'w_uq': _jnp.float32, 'w_ukv': _jnp.float32, 'sinks': _jnp.float32, 'w_branch': _jnp.float32, 'w_out': _jnp.float32, 'final_g': _jnp.float32}
MOMENT_SCALE = {'meta_tokens': 3.780864e-03, 'norm_g': 5.011784e-02, 'w_in': 1.823499e-02, 'b_f': 1.564654e-01, 'g_cq': 1.461101e-02, 'g_ckv': 2.604092e-02, 'w_uq': 1.067025e-02, 'w_ukv': 1.271570e-02, 'sinks': 2.059814e-03, 'w_branch': 1.520693e-02, 'w_out': 2.633115e-02, 'final_g': 3.203101e+01}


def _to_microbatches(a, axis):
    t = _jnp.moveaxis(a, axis, 0)
    t = t.reshape((N_MICROBATCH, t.shape[0] // N_MICROBATCH) + t.shape[1:])
    return _jnp.moveaxis(t, 1, axis + 1)


def setup_inputs(seed: int = 0) -> dict:
    inp = _fwd_setup_inputs(seed)
    key = _jax.random.fold_in(_jax.random.key(seed), 7919)
    shape, _ = _output_shape()
    out = dict(inp)
    out["loss_target"] = _jax.random.normal(_jax.random.fold_in(key, 0), shape, _jnp.float32)
    for i, name in enumerate(TWIN_WEIGHTS):
        w = inp[name].astype(_jnp.float32)
        if MOMENT_SCALE is None:
            s = _jnp.sqrt(_jnp.mean(_jnp.square(w)) + 1e-30)
        else:
            s = MOMENT_SCALE[name]
        km, kv = _jax.random.split(_jax.random.fold_in(key, i + 1))
        out[name] = w
        out["m_" + name] = s * _jax.random.normal(km, w.shape, _jnp.float32)
        out["v_" + name] = (s * s) * _jax.random.uniform(kv, w.shape, _jnp.float32, 0.5, 1.5)
    if N_MICROBATCH > 1:
        for name, axis in PER_EXAMPLE_BATCH_AXIS.items():
            out[name] = _to_microbatches(out[name], axis)
    return {'x': out['x'], 'meta_tokens': out['meta_tokens'], 'norm_g': out['norm_g'], 'w_in': out['w_in'], 'b_f': out['b_f'], 'g_cq': out['g_cq'], 'g_ckv': out['g_ckv'], 'w_uq': out['w_uq'], 'w_ukv': out['w_ukv'], 'sinks': out['sinks'], 'w_branch': out['w_branch'], 'w_out': out['w_out'], 'final_g': out['final_g'], 'loss_target': out['loss_target'], 'm_meta_tokens': out['m_meta_tokens'], 'm_norm_g': out['m_norm_g'], 'm_w_in': out['m_w_in'], 'm_b_f': out['m_b_f'], 'm_g_cq': out['m_g_cq'], 'm_g_ckv': out['m_g_ckv'], 'm_w_uq': out['m_w_uq'], 'm_w_ukv': out['m_w_ukv'], 'm_sinks': out['m_sinks'], 'm_w_branch': out['m_w_branch'], 'm_w_out': out['m_w_out'], 'm_final_g': out['m_final_g'], 'v_meta_tokens': out['v_meta_tokens'], 'v_norm_g': out['v_norm_g'], 'v_w_in': out['v_w_in'], 'v_b_f': out['v_b_f'], 'v_g_cq': out['v_g_cq'], 'v_g_ckv': out['v_g_ckv'], 'v_w_uq': out['v_w_uq'], 'v_w_ukv': out['v_w_ukv'], 'v_sinks': out['v_sinks'], 'v_w_branch': out['v_w_branch'], 'v_w_out': out['v_w_out'], 'v_final_g': out['v_final_g']}


def _loss(weights, diff, rest, loss_target):
    with _jax.named_scope("forward"):
        args = {**rest, TWIN_DIFF_INPUT: diff, **{k: w.astype(_WEIGHT_DTYPES[k]) for k, w in weights.items()}}
        y = _forward(args)
    with _jax.named_scope("loss_head"):
        err = _jnp.square(y.astype(_jnp.float32) - loss_target)
        return 0.5 * _jnp.sum(_jnp.mean(err, axis=-1)) if err.ndim else 0.5 * err


def _adamw(w, g, m, v):
    m = ADAM_B1 * m + (1.0 - ADAM_B1) * g
    v = ADAM_B2 * v + (1.0 - ADAM_B2) * _jnp.square(g)
    m_hat = m / (1.0 - ADAM_B1 ** ADAM_STEP)
    v_hat = v / (1.0 - ADAM_B2 ** ADAM_STEP)
    delta = -ADAM_LR * (m_hat / (_jnp.sqrt(v_hat) + ADAM_EPS) + ADAM_WD * w)
    return delta, m, v


def reference(x, meta_tokens, norm_g, w_in, b_f, g_cq, g_ckv, w_uq, w_ukv, sinks, w_branch, w_out, final_g, loss_target, m_meta_tokens, m_norm_g, m_w_in, m_b_f, m_g_cq, m_g_ckv, m_w_uq, m_w_ukv, m_sinks, m_w_branch, m_w_out, m_final_g, v_meta_tokens, v_norm_g, v_w_in, v_b_f, v_g_cq, v_g_ckv, v_w_uq, v_w_ukv, v_sinks, v_w_branch, v_w_out, v_final_g):
    given = dict(x=x, meta_tokens=meta_tokens, norm_g=norm_g, w_in=w_in, b_f=b_f, g_cq=g_cq, g_ckv=g_ckv, w_uq=w_uq, w_ukv=w_ukv, sinks=sinks, w_branch=w_branch, w_out=w_out, final_g=final_g, loss_target=loss_target, m_meta_tokens=m_meta_tokens, m_norm_g=m_norm_g, m_w_in=m_w_in, m_b_f=m_b_f, m_g_cq=m_g_cq, m_g_ckv=m_g_ckv, m_w_uq=m_w_uq, m_w_ukv=m_w_ukv, m_sinks=m_sinks, m_w_branch=m_w_branch, m_w_out=m_w_out, m_final_g=m_final_g, v_meta_tokens=v_meta_tokens, v_norm_g=v_norm_g, v_w_in=v_w_in, v_b_f=v_b_f, v_g_cq=v_g_cq, v_g_ckv=v_g_ckv, v_w_uq=v_w_uq, v_w_ukv=v_w_ukv, v_sinks=v_sinks, v_w_branch=v_w_branch, v_w_out=v_w_out, v_final_g=v_final_g)
    weights = {n: given[n] for n in TWIN_WEIGHTS}
    shared = {n: given[n] for n in SHARED_INPUTS}
    per_example = {n: given[n] for n in ['x']}
    grad_fn = _jax.value_and_grad(_loss, argnums=(0, 1))

    def one_microbatch(ex, loss_target):
        ex = dict(ex)
        diff = ex.pop(TWIN_DIFF_INPUT)
        return grad_fn(weights, diff, {**shared, **ex}, loss_target)

    if N_MICROBATCH == 1:
        loss, (grad_w, grad_x) = one_microbatch(per_example, given["loss_target"])
    else:
        def body(carry, xs):
            loss_sum, grad_sum = carry
            l_k, (gw_k, gx_k) = one_microbatch(xs[0], xs[1])
            with _jax.named_scope("update"):
                return (loss_sum + l_k, _jax.tree.map(_jnp.add, grad_sum, gw_k)), gx_k

        init = (_jnp.zeros((), _jnp.float32), _jax.tree.map(_jnp.zeros_like, weights))
        (loss, grad_w), grad_x = _jax.lax.scan(body, init, (per_example, given["loss_target"]))
    with _jax.named_scope("update"):
        delta_w, new_m, new_v = {}, {}, {}
        for n in TWIN_WEIGHTS:
            delta_w[n], new_m[n], new_v[n] = _adamw(weights[n], grad_w[n], given["m_" + n], given["v_" + n])
    return (loss, grad_x, *[grad_w[n] for n in TWIN_WEIGHTS], *[delta_w[n] for n in TWIN_WEIGHTS],
            *[new_m[n] for n in TWIN_WEIGHTS], *[new_v[n] for n in TWIN_WEIGHTS])
```

```python
import functools
import math

import jax
import jax.numpy as jnp
from jax import lax
from jax.experimental import pallas as pl
from jax.experimental.pallas import tpu as pltpu

F32 = jnp.float32
BF16 = jnp.bfloat16

D_MODEL = 1024
DEPTH = 2
N_META = 16
BLK = 128
PAD = BLK - N_META
ROPE_THETA = 10000.0
EPS = 1e-6
NEG = -1e30
HEADS = 8
DH = 64
MLA_NOPE = 64
MLA_ROPE = 32
MLA_QLORA = 384
MLA_KVLORA = 256
SWA_KV_HEADS = 2
WINDOW = 128
BRANCH_W = 512
N_BRANCH = 3
N_IN = 7592

ADAM_LR = 0.001
ADAM_B1 = 0.9
ADAM_B2 = 0.999
ADAM_EPS = 1e-08
ADAM_WD = 0.01
ADAM_STEP = 10

N_DEV = 8
MESH = pl.DeviceIdType.MESH

_RUNS = ((0, 1544, 0), (1544, 2056, 1792), (2056, 2728, 2304), (2728, 3240, 3072), (3240, 4520, 3584), (4520, 7592, 4864))
NP_IN = 8192
A_LO, A_HI = 0, 2304
B_LO, B_HI = 2304, 3584
C_LO, C_HI = 3584, 4864
G_LO, G_HI = 4864, 7936

VMEM_LIMIT = 48 * 1024 * 1024
ATT_T = 384


def _pick(dim, prefs):
    for p in prefs:
        if dim % p == 0:
            return p
    return dim


def _cparams(sem):
    return pltpu.CompilerParams(dimension_semantics=sem, vmem_limit_bytes=VMEM_LIMIT)


def _mm(a, b, *, ta=False, tb=False, out_dtype=F32, name):
    M = a.shape[1] if ta else a.shape[0]
    K = a.shape[0] if ta else a.shape[1]
    N = b.shape[0] if tb else b.shape[1]
    assert K == (b.shape[1] if tb else b.shape[0])
    tm = _pick(M, (704, 1024, 512, 384, 256, 128))
    tn = _pick(N, (1024, 768, 512, 384, 256, 128))
    tk = _pick(K, (1024, 768, 512, 384, 256, 128))
    nk = K // tk
    dims = (((0 if ta else 1,), (1 if tb else 0,)), ((), ()))

    def body(a_ref, b_ref, o_ref, *scratch):
        r = lax.dot_general(a_ref[...].astype(BF16), b_ref[...].astype(BF16), dims, preferred_element_type=F32)
        if nk == 1:
            o_ref[...] = r.astype(out_dtype)
        else:
            acc = scratch[0]
            k = pl.program_id(2)

            @pl.when(k == 0)
            def _():
                acc[...] = r

            @pl.when(k > 0)
            def _():
                acc[...] += r

            @pl.when(k == nk - 1)
            def _():
                o_ref[...] = acc[...].astype(out_dtype)

    a_spec = pl.BlockSpec((tk, tm), lambda i, j, k: (k, i)) if ta else pl.BlockSpec((tm, tk), lambda i, j, k: (i, k))
    b_spec = pl.BlockSpec((tn, tk), lambda i, j, k: (j, k)) if tb else pl.BlockSpec((tk, tn), lambda i, j, k: (k, j))
    return pl.pallas_call(
        body, name=name,
        grid=(M // tm, N // tn, nk),
        in_specs=[a_spec, b_spec],
        out_specs=pl.BlockSpec((tm, tn), lambda i, j, k: (i, j)),
        out_shape=jax.ShapeDtypeStruct((M, N), out_dtype),
        scratch_shapes=[pltpu.VMEM((tm, tn), F32)] if nk > 1 else [],
        compiler_params=_cparams(("parallel", "parallel", "arbitrary")),
    )(a, b)


def _rms_fwd(x, g, *, name):
    L, D = x.shape
    tm = _pick(L, (384, 256, 128))

    def body(x_ref, g_ref, y_ref):
        xv = x_ref[...]
        rstd = lax.rsqrt(jnp.mean(xv * xv, axis=-1, keepdims=True) + EPS)
        y_ref[...] = xv * rstd * g_ref[...]

    return pl.pallas_call(
        body, name=name, grid=(L // tm,),
        in_specs=[pl.BlockSpec((tm, D), lambda i: (i, 0)), pl.BlockSpec((1, D), lambda i: (0, 0))],
        out_specs=pl.BlockSpec((tm, D), lambda i: (i, 0)),
        out_shape=jax.ShapeDtypeStruct((L, D), F32),
        compiler_params=_cparams(("parallel",)),
    )(x, g.reshape(1, D))


def _rms_bwd(dy, x, g, *, name):
    L, D = x.shape
    tm = _pick(L, (384, 256, 128))

    def body(dy_ref, x_ref, g_ref, dx_ref, dg_ref):
        i = pl.program_id(0)
        xv = x_ref[...]
        dyv = dy_ref[...]
        rstd = lax.rsqrt(jnp.mean(xv * xv, axis=-1, keepdims=True) + EPS)
        xhat = xv * rstd
        part = jnp.sum(dyv * xhat, axis=0, keepdims=True)

        @pl.when(i == 0)
        def _():
            dg_ref[...] = part

        @pl.when(i > 0)
        def _():
            dg_ref[...] += part

        dxh = dyv * g_ref[...]
        dx_ref[...] = rstd * (dxh - xhat * jnp.mean(dxh * xhat, axis=-1, keepdims=True))

    dx, dg = pl.pallas_call(
        body, name=name, grid=(L // tm,),
        in_specs=[pl.BlockSpec((tm, D), lambda i: (i, 0)), pl.BlockSpec((tm, D), lambda i: (i, 0)),
                  pl.BlockSpec((1, D), lambda i: (0, 0))],
        out_specs=[pl.BlockSpec((tm, D), lambda i: (i, 0)), pl.BlockSpec((1, D), lambda i: (0, 0))],
        out_shape=[jax.ShapeDtypeStruct((L, D), F32), jax.ShapeDtypeStruct((1, D), F32)],
        compiler_params=_cparams(("arbitrary",)),
    )(dy, x, g.reshape(1, D))
    return dx, dg.reshape(D)


def _loss_head(h, g, target, *, name):
    L, D = h.shape
    nb = L // BLK

    def body(h_ref, g_ref, t_ref, loss_ref, dh_ref, dg_ref):
        i = pl.program_id(0)

        @pl.when(i == 0)
        def _():
            loss_ref[...] = jnp.zeros_like(loss_ref)
            dg_ref[...] = jnp.zeros_like(dg_ref)
            dh_ref[...] = jnp.zeros_like(dh_ref)

        @pl.when(i > 0)
        def _():
            xv = h_ref[...]
            gv = g_ref[...]
            rstd = lax.rsqrt(jnp.mean(xv * xv, axis=-1, keepdims=True) + EPS)
            xhat = xv * rstd
            err = xhat * gv - t_ref[...]
            row = jnp.mean(err * err, axis=-1, keepdims=True)
            loss_ref[...] += 0.5 * jnp.sum(row, axis=0, keepdims=True)
            dy = err * (1.0 / D)
            dg_ref[...] += jnp.sum(dy * xhat, axis=0, keepdims=True)
            dxh = dy * gv
            dh_ref[...] = rstd * (dxh - xhat * jnp.mean(dxh * xhat, axis=-1, keepdims=True))

    loss, dh, dg = pl.pallas_call(
        body, name=name, grid=(nb,),
        in_specs=[pl.BlockSpec((BLK, D), lambda i: (i, 0)), pl.BlockSpec((1, D), lambda i: (0, 0)),
                  pl.BlockSpec((BLK, D), lambda i: (jnp.maximum(i - 1, 0), 0))],
        out_specs=[pl.BlockSpec((1, 128), lambda i: (0, 0)), pl.BlockSpec((BLK, D), lambda i: (i, 0)),
                   pl.BlockSpec((1, D), lambda i: (0, 0))],
        out_shape=[jax.ShapeDtypeStruct((1, 128), F32), jax.ShapeDtypeStruct((L, D), F32),
                   jax.ShapeDtypeStruct((1, D), F32)],
        compiler_params=_cparams(("arbitrary",)),
    )(h, g.reshape(1, D), target)
    return loss, dh, dg.reshape(D)


def _attn_mask(i, j, T, window):
    qpos = i * T + lax.broadcasted_iota(jnp.int32, (T, T), 0)
    kpos = j * T + lax.broadcasted_iota(jnp.int32, (T, T), 1)
    ok = (kpos <= qpos) & (kpos >= PAD)
    if window is not None:
        ok = ok & (qpos - kpos < window)
    return ok


def _attn_back(T, window):
    return None if window is None else (T + window - 2) // T


def _attn_fwd(q, k, v, *, scale, cq=None, ck=None, sink=None, window=None, name):
    H, L, dk = q.shape
    Hkv = k.shape[0]
    dv = v.shape[2]
    G = H // Hkv
    T = ATT_T
    nt = L // T
    back = _attn_back(T, window)
    bias = cq is not None

    def body(*refs):
        q_ref, k_ref, v_ref = refs[:3]
        n = 3
        if bias:
            cq_ref, ck_ref = refs[n:n + 2]
            n += 2
        if sink is not None:
            sk_ref = refs[n]
            n += 1
        o_ref, lse_ref, m_sc, l_sc, acc_sc = refs[n:]
        i = pl.program_id(1)
        j = pl.program_id(2)

        @pl.when(j == 0)
        def _():
            if sink is not None:
                m_sc[...] = jnp.broadcast_to(sk_ref[0, :, 0:1], (T, 1))
                l_sc[...] = jnp.ones_like(l_sc)
            else:
                m_sc[...] = jnp.full_like(m_sc, NEG)
                l_sc[...] = jnp.zeros_like(l_sc)
            acc_sc[...] = jnp.zeros_like(acc_sc)

        needed = j <= i
        if back is not None:
            needed = needed & (j >= i - back)

        @pl.when(needed)
        def _():
            s = lax.dot_general(q_ref[0].astype(BF16), k_ref[0].astype(BF16), (((1,), (1,)), ((), ())),
                                preferred_element_type=F32) * scale
            if bias:
                s = s + (cq_ref[0] - ck_ref[0])
            s = jnp.where(_attn_mask(i, j, T, window), s, NEG)
            m_prev = m_sc[...]
            m_new = jnp.maximum(m_prev, jnp.max(s, axis=-1, keepdims=True))
            alpha = jnp.exp(m_prev - m_new)
            p = jnp.exp(s - m_new)
            l_sc[...] = alpha * l_sc[...] + jnp.sum(p, axis=-1, keepdims=True)
            acc_sc[...] = alpha * acc_sc[...] + jnp.dot(p.astype(BF16), v_ref[0].astype(BF16),
                                                        preferred_element_type=F32)
            m_sc[...] = m_new

        @pl.when(j == nt - 1)
        def _():
            o_ref[0] = acc_sc[...] / l_sc[...]
            lse_ref[0] = m_sc[...] + jnp.log(l_sc[...])

    def kj(i, j):
        lo = 0 if back is None else jnp.maximum(i - back, 0)
        return jnp.clip(j, lo, i)

    in_specs = [pl.BlockSpec((1, T, dk), lambda h, i, j: (h, i, 0)),
                pl.BlockSpec((1, T, dk), lambda h, i, j: (h // G, kj(i, j), 0)),
                pl.BlockSpec((1, T, dv), lambda h, i, j: (h // G, kj(i, j), 0))]
    args = [q, k, v]
    if bias:
        in_specs += [pl.BlockSpec((1, T, 1), lambda h, i, j: (h, i, 0)),
                     pl.BlockSpec((1, 1, T), lambda h, i, j: (h, 0, kj(i, j)))]
        args += [cq, ck]
    if sink is not None:
        in_specs += [pl.BlockSpec((1, 1, 128), lambda h, i, j: (h, 0, 0))]
        args += [sink]
    o, lse = pl.pallas_call(
        body, name=name, grid=(H, nt, nt),
        in_specs=in_specs,
        out_specs=[pl.BlockSpec((1, T, dv), lambda h, i, j: (h, i, 0)),
                   pl.BlockSpec((1, T, 1), lambda h, i, j: (h, i, 0))],
        out_shape=[jax.ShapeDtypeStruct((H, L, dv), F32), jax.ShapeDtypeStruct((H, L, 1), F32)],
        scratch_shapes=[pltpu.VMEM((T, 1), F32), pltpu.VMEM((T, 1), F32), pltpu.VMEM((T, dv), F32)],
        compiler_params=_cparams(("parallel", "parallel", "arbitrary")),
    )(*args)
    return o, lse


def _attn_bwd(q, k, v, o, lse, do, *, scale, cq=None, ck=None, window=None, name):
    H, L, dk = q.shape
    Hkv = k.shape[0]
    dv = v.shape[2]
    G = H // Hkv
    T = ATT_T
    nt = L // T
    back = _attn_back(T, window)
    bias = cq is not None

    def body(*refs):
        q_ref, k_ref, v_ref, o_ref, lse_ref, do_ref = refs[:6]
        n = 6
        if bias:
            cq_ref, ck_ref = refs[n:n + 2]
            n += 2
        dq_ref, dk_ref, dv_ref = refs[n:n + 3]
        n += 3
        if bias:
            dcq_ref, dck_ref = refs[n:n + 2]
        j = pl.program_id(1)
        i = pl.program_id(2)

        @pl.when((j == 0) & (i == 0))
        def _():
            dq_ref[...] = jnp.zeros_like(dq_ref)
            if bias:
                dcq_ref[...] = jnp.zeros_like(dcq_ref)

        @pl.when(i == 0)
        def _():
            dk_ref[...] = jnp.zeros_like(dk_ref)
            dv_ref[...] = jnp.zeros_like(dv_ref)
            if bias:
                dck_ref[...] = jnp.zeros_like(dck_ref)

        needed = i >= j
        if back is not None:
            needed = needed & (i <= j + back)

        @pl.when(needed)
        def _():
            qb = q_ref[0].astype(BF16)
            kb = k_ref[0].astype(BF16)
            vb = v_ref[0].astype(BF16)
            dob = do_ref[0].astype(BF16)
            s = lax.dot_general(qb, kb, (((1,), (1,)), ((), ())), preferred_element_type=F32) * scale
            if bias:
                s = s + (cq_ref[0] - ck_ref[0])
            s = jnp.where(_attn_mask(i, j, T, window), s, NEG)
            p = jnp.exp(s - lse_ref[0])
            delta = jnp.sum(do_ref[0] * o_ref[0], axis=-1, keepdims=True)
            dp = lax.dot_general(dob, vb, (((1,), (1,)), ((), ())), preferred_element_type=F32)
            ds = p * (dp - delta)
            dsb = ds.astype(BF16)
            dv_ref[0] += lax.dot_general(p.astype(BF16), dob, (((0,), (0,)), ((), ())), preferred_element_type=F32)
            dk_ref[0] += lax.dot_general(dsb, qb, (((0,), (0,)), ((), ())), preferred_element_type=F32) * scale
            rows = pl.ds(pl.multiple_of(i * T, T), T)
            dq_ref[0, rows, :] += jnp.dot(dsb, kb, preferred_element_type=F32) * scale
            if bias:
                dcq_ref[0, rows, :] += jnp.sum(ds, axis=1, keepdims=True)
                dck_ref[0] += -jnp.sum(ds, axis=0, keepdims=True)

    def qi(j, i):
        hi = nt - 1 if back is None else jnp.minimum(j + back, nt - 1)
        return jnp.clip(i, j, hi)

    in_specs = [pl.BlockSpec((1, T, dk), lambda h, j, i: (h, qi(j, i), 0)),
                pl.BlockSpec((1, T, dk), lambda h, j, i: (h // G, j, 0)),
                pl.BlockSpec((1, T, dv), lambda h, j, i: (h // G, j, 0)),
                pl.BlockSpec((1, T, dv), lambda h, j, i: (h, qi(j, i), 0)),
                pl.BlockSpec((1, T, 1), lambda h, j, i: (h, qi(j, i), 0)),
                pl.BlockSpec((1, T, dv), lambda h, j, i: (h, qi(j, i), 0))]
    args = [q, k, v, o, lse, do]
    out_specs = [pl.BlockSpec((1, L, dk), lambda h, j, i: (h, 0, 0)),
                 pl.BlockSpec((1, T, dk), lambda h, j, i: (h, j, 0)),
                 pl.BlockSpec((1, T, dv), lambda h, j, i: (h, j, 0))]
    out_shape = [jax.ShapeDtypeStruct((H, L, dk), F32), jax.ShapeDtypeStruct((H, L, dk), F32),
                 jax.ShapeDtypeStruct((H, L, dv), F32)]
    if bias:
        in_specs += [pl.BlockSpec((1, T, 1), lambda h, j, i: (h, qi(j, i), 0)),
                     pl.BlockSpec((1, 1, T), lambda h, j, i: (h, 0, j))]
        args += [cq, ck]
        out_specs += [pl.BlockSpec((1, L, 1), lambda h, j, i: (h, 0, 0)),
                      pl.BlockSpec((1, 1, T), lambda h, j, i: (h, 0, j))]
        out_shape += [jax.ShapeDtypeStruct((H, L, 1), F32), jax.ShapeDtypeStruct((H, 1, L), F32)]
    outs = pl.pallas_call(
        body, name=name, grid=(H, nt, nt),
        in_specs=in_specs, out_specs=out_specs, out_shape=out_shape,
        compiler_params=_cparams(("parallel", "arbitrary", "arbitrary")),
    )(*args)
    return outs if bias else (*outs, None, None)


def _adamw(w, gparts, m, v, *, name):
    n, R, C = gparts.shape
    tr = _pick(R, (256, 128, 64, 32, 16, 8))
    c1 = 1.0 - ADAM_B1 ** ADAM_STEP
    c2 = 1.0 - ADAM_B2 ** ADAM_STEP

    def body(w_ref, g_ref, m_ref, v_ref, go_ref, d_ref, mo_ref, vo_ref):
        g = g_ref[0]
        for t in range(1, n):
            g = g + g_ref[t]
        mn = ADAM_B1 * m_ref[...] + (1.0 - ADAM_B1) * g
        vn = ADAM_B2 * v_ref[...] + (1.0 - ADAM_B2) * (g * g)
        go_ref[...] = g
        mo_ref[...] = mn
        vo_ref[...] = vn
        d_ref[...] = -ADAM_LR * ((mn / c1) / (jnp.sqrt(vn / c2) + ADAM_EPS) + ADAM_WD * w_ref[...])

    spec = pl.BlockSpec((tr, C), lambda i: (i, 0))
    return pl.pallas_call(
        body, name=name, grid=(R // tr,),
        in_specs=[spec, pl.BlockSpec((n, tr, C), lambda i: (0, i, 0)), spec, spec],
        out_specs=[spec] * 4,
        out_shape=[jax.ShapeDtypeStruct((R, C), F32)] * 4,
        compiler_params=_cparams(("parallel",)),
    )(w, gparts, m, v)


def _me():
    return lax.axis_index("x"), lax.axis_index("y"), lax.axis_index("c")


def _all_gather(arrs, *, name):
    n = len(arrs)

    def body(*refs):
        ins, outs = refs[:n], refs[n:2 * n]
        send_sems, recv_sems, local_sems = refs[2 * n:]
        x, y, c = _me()
        me, sibling = (x, y, c), (x, y, 1 - c)
        chips = [(1 - x, y), (x, 1 - y), (1 - x, 1 - y)]

        def idx(p):
            return 4 * p[0] + 2 * p[1] + p[2]

        def copy(t, k, block, to, src=None):
            dst = outs[t].at[idx(block)]
            return pltpu.make_async_remote_copy(
                src_ref=dst if src is None else src, dst_ref=dst,
                send_sem=send_sems.at[t, k], recv_sem=recv_sems.at[t, k],
                device_id=to, device_id_type=MESH)

        mine = [pltpu.make_async_copy(ins[t], outs[t].at[idx(me)], local_sems.at[t]) for t in range(n)]
        for cp in mine:
            cp.start()
        first = []
        for t in range(n):
            first.append(copy(t, 0, me, sibling, src=ins[t]))
            first += [copy(t, 1 + j, me, (*chip, c), src=ins[t]) for j, chip in enumerate(chips)]
        for cp in first:
            cp.start()
        passed = [[copy(t, 4 + j, (*chip, c), sibling) for j, chip in enumerate(chips)] for t in range(n)]
        for j, chip in enumerate(chips):
            for t in range(n):
                copy(t, 1 + j, (*chip, c), me).wait_recv()
                passed[t][j].start()
        for t in range(n):
            copy(t, 0, sibling, me).wait_recv()
            for j, chip in enumerate(chips):
                copy(t, 4 + j, (*chip, 1 - c), me).wait_recv()
        for cp in first:
            cp.wait_send()
        for t in range(n):
            for cp in passed[t]:
                cp.wait_send()
        for cp in mine:
            cp.wait()

    any_spec = pl.BlockSpec(memory_space=pl.ANY)
    return pl.pallas_call(
        body, name=name,
        in_specs=[any_spec] * n, out_specs=[any_spec] * n,
        out_shape=[jax.ShapeDtypeStruct((N_DEV, *a.shape), a.dtype) for a in arrs],
        scratch_shapes=[pltpu.SemaphoreType.DMA((n, 7)), pltpu.SemaphoreType.DMA((n, 7)),
                        pltpu.SemaphoreType.DMA((n,))],
        compiler_params=pltpu.CompilerParams(has_side_effects=True),
    )(*arrs)


def _exchange(arrs, *, name):
    n = len(arrs)

    def body(*refs):
        ins, outs = refs[:n], refs[n:2 * n]
        send_sems, recv_sems, local_sems = refs[2 * n:]
        x, y, c = _me()
        my_idx = 4 * x + 2 * y + c
        mine = [pltpu.make_async_copy(ins[t].at[my_idx], outs[t].at[my_idx], local_sems.at[t]) for t in range(n)]
        for cp in mine:
            cp.start()
        copies = []
        for t in range(n):
            for k in range(1, N_DEV):
                bx, by, bc = (k >> 2) & 1, (k >> 1) & 1, k & 1
                peer = (x ^ bx, y ^ by, c ^ bc)
                peer_idx = 4 * peer[0] + 2 * peer[1] + peer[2]
                copies.append((pltpu.make_async_remote_copy(
                    src_ref=ins[t].at[peer_idx], dst_ref=outs[t].at[my_idx],
                    send_sem=send_sems.at[t, k - 1], recv_sem=recv_sems.at[t, k - 1],
                    device_id=peer, device_id_type=MESH),
                    pltpu.make_async_remote_copy(
                    src_ref=ins[t].at[peer_idx], dst_ref=outs[t].at[peer_idx],
                    send_sem=send_sems.at[t, k - 1], recv_sem=recv_sems.at[t, k - 1],
                    device_id=peer, device_id_type=MESH)))
        for snd, _ in copies:
            snd.start()
        for _, rcv in copies:
            rcv.wait_recv()
        for snd, _ in copies:
            snd.wait_send()
        for cp in mine:
            cp.wait()

    any_spec = pl.BlockSpec(memory_space=pl.ANY)
    return pl.pallas_call(
        body, name=name,
        in_specs=[any_spec] * n, out_specs=[any_spec] * n,
        out_shape=[jax.ShapeDtypeStruct(a.shape, a.dtype) for a in arrs],
        scratch_shapes=[pltpu.SemaphoreType.DMA((n, 7)), pltpu.SemaphoreType.DMA((n, 7)),
                        pltpu.SemaphoreType.DMA((n,))],
        compiler_params=pltpu.CompilerParams(has_side_effects=True),
    )(*arrs)


def _rope_tables(L, half):
    pos = (jnp.arange(L) - PAD).astype(F32)
    inv = ROPE_THETA ** (-jnp.arange(half, dtype=F32) / half)
    ang = pos[:, None] * inv[None, :]
    return jnp.cos(ang)[:, None, :], jnp.sin(ang)[:, None, :]


def _rope(x):
    half = x.shape[-1] // 2
    cos, sin = _rope_tables(x.shape[0], half)
    x1, x2 = x[..., :half], x[..., half:]
    return jnp.concatenate([x1 * cos - x2 * sin, x2 * cos + x1 * sin], axis=-1)


def _heads(t, H):
    L = t.shape[0]
    return t.reshape(L, H, t.shape[1] // H).transpose(1, 0, 2)


def _glue_a(pa, b_f):
    q = _heads(pa[:, 0:512], HEADS)
    k = _heads(pa[:, 512:1024], HEADS)
    v = _heads(pa[:, 1024:1536], HEADS)
    log_f = jax.nn.log_sigmoid(pa[:, 1536:1536 + HEADS] + b_f)
    ct = jnp.cumsum(log_f, axis=0).T
    return q, k, v, ct[:, :, None], ct[:, None, :], pa[:, 1792:2304]


def _glue_b1(pb):
    return pb[:, 0:384], pb[:, 384:640], pb[:, 640:672], pb[:, 768:1280]


def _glue_b2(qb, kvb, kr):
    L = qb.shape[0]
    q3 = qb.reshape(L, HEADS, MLA_NOPE + MLA_ROPE)
    q = jnp.concatenate([q3[..., :MLA_NOPE], _rope(q3[..., MLA_NOPE:])], axis=-1).transpose(1, 0, 2)
    kv3 = kvb.reshape(L, HEADS, MLA_NOPE + DH)
    krr = jnp.broadcast_to(_rope(kr.reshape(L, 1, MLA_ROPE)), (L, HEADS, MLA_ROPE))
    k = jnp.concatenate([kv3[..., :MLA_NOPE], krr], axis=-1).transpose(1, 0, 2)
    v = kv3[..., MLA_NOPE:].transpose(1, 0, 2)
    return q, k, v


def _glue_c(pc):
    L = pc.shape[0]
    q = _rope(pc[:, 0:512].reshape(L, HEADS, DH)).transpose(1, 0, 2)
    k = _rope(pc[:, 512:640].reshape(L, SWA_KV_HEADS, DH)).transpose(1, 0, 2)
    v = pc[:, 640:768].reshape(L, SWA_KV_HEADS, DH).transpose(1, 0, 2)
    return q, k, v, pc[:, 768:1280]


def _glue_gate(y, z):
    L = z.shape[0]
    return y.transpose(1, 0, 2).reshape(L, BRANCH_W) * jax.nn.silu(z)


def _glue_merge(gates, pa, pb, pc):
    g = jax.nn.sigmoid(gates)
    return g[:, 0:1024] * pa + g[:, 1024:2048] * pb + g[:, 2048:3072] * pc


def _kv_group_sum(d):
    H, L, dd = d.shape
    return d.reshape(SWA_KV_HEADS, H // SWA_KV_HEADS, L, dd).sum(axis=1)


def _layer_fwd(h, w, l):
    tag = f"l{l}"
    hn = _rms_fwd(h, w["norm_g"], name=f"{tag}_rms_in")
    proj = _mm(hn, w["w_in"], name=f"{tag}_mm_in")
    (qa, ka, va, cqa, cka, za), vjp_a = jax.vjp(_glue_a, proj[:, A_LO:A_HI], w["b_f"])
    ya, lsea = _attn_fwd(qa, ka, va, scale=DH ** -0.5, cq=cqa, ck=cka, name=f"{tag}_attn_a")
    (bcq, bckv, bkr, zb), vjp_b1 = jax.vjp(_glue_b1, proj[:, B_LO:B_HI])
    cqn = _rms_fwd(bcq, w["g_cq"], name=f"{tag}_rms_cq")
    ckvn = _rms_fwd(bckv, w["g_ckv"], name=f"{tag}_rms_ckv")
    qbm = _mm(cqn, w["w_uq"], name=f"{tag}_mm_uq")
    kvbm = _mm(ckvn, w["w_ukv"], name=f"{tag}_mm_ukv")
    (qb, kb, vb), vjp_b2 = jax.vjp(_glue_b2, qbm, kvbm, bkr)
    yb, lseb = _attn_fwd(qb, kb, vb, scale=(MLA_NOPE + MLA_ROPE) ** -0.5, name=f"{tag}_attn_b")
    (qc, kc, vc, zc), vjp_c = jax.vjp(_glue_c, proj[:, C_LO:C_HI])
    sink = jnp.broadcast_to(w["sinks"][:, None, None], (HEADS, 1, 128))
    yc, lsec = _attn_fwd(qc, kc, vc, scale=DH ** -0.5, sink=sink, window=WINDOW, name=f"{tag}_attn_c")
    us, vjp_gates = [], []
    for y, z in ((ya, za), (yb, zb), (yc, zc)):
        u, vj = jax.vjp(_glue_gate, y, z)
        us.append(u)
        vjp_gates.append(vj)
    pbr = [_mm(us[n], w["w_branch"][n], name=f"{tag}_mm_br{n}") for n in range(N_BRANCH)]
    merged, vjp_merge = jax.vjp(_glue_merge, proj[:, G_LO:G_HI], *pbr)
    out = h + _mm(merged, w["w_out"], name=f"{tag}_mm_out")
    saved = dict(h=h, hn=hn, vjp_a=vjp_a, vjp_b1=vjp_b1, vjp_b2=vjp_b2, vjp_c=vjp_c, vjp_gates=vjp_gates,
                 vjp_merge=vjp_merge, qa=qa, ka=ka, va=va, cqa=cqa, cka=cka, ya=ya, lsea=lsea,
                 bcq=bcq, bckv=bckv, cqn=cqn, ckvn=ckvn, qb=qb, kb=kb, vb=vb, yb=yb, lseb=lseb,
                 qc=qc, kc=kc, vc=vc, yc=yc, lsec=lsec, us=us, merged=merged)
    return out, saved


def _layer_bwd(dout, s, w, l):
    tag = f"l{l}"
    L = dout.shape[0]
    g = {}
    dmerged = _mm(dout, w["w_out"], tb=True, name=f"{tag}_mm_out_dx")
    g["w_out"] = _mm(s["merged"], dout, ta=True, name=f"{tag}_mm_out_dw")
    dgates, *dpbr = s["vjp_merge"](dmerged)
    dus = [_mm(dpbr[n], w["w_branch"][n], tb=True, name=f"{tag}_mm_br{n}_dx") for n in range(N_BRANCH)]
    g["w_branch"] = jnp.stack([_mm(s["us"][n], dpbr[n], ta=True, name=f"{tag}_mm_br{n}_dw") for n in range(N_BRANCH)])
    (dya, dza), (dyb, dzb), (dyc, dzc) = [s["vjp_gates"][n](dus[n]) for n in range(N_BRANCH)]
    dqa, dka, dva, dcq, dck = _attn_bwd(s["qa"], s["ka"], s["va"], s["ya"], s["lsea"], dya, scale=DH ** -0.5,
                                        cq=s["cqa"], ck=s["cka"], name=f"{tag}_attn_a_bwd")
    dpa, g["b_f"] = s["vjp_a"]((dqa, dka, dva, dcq, dck, dza))
    dqb, dkb, dvb, _, _ = _attn_bwd(s["qb"], s["kb"], s["vb"], s["yb"], s["lseb"], dyb,
                                 scale=(MLA_NOPE + MLA_ROPE) ** -0.5, name=f"{tag}_attn_b_bwd")
    dqbm, dkvbm, dbkr = s["vjp_b2"]((dqb, dkb, dvb))
    dcqn = _mm(dqbm, w["w_uq"], tb=True, name=f"{tag}_mm_uq_dx")
    g["w_uq"] = _mm(s["cqn"], dqbm, ta=True, name=f"{tag}_mm_uq_dw")
    dckvn = _mm(dkvbm, w["w_ukv"], tb=True, name=f"{tag}_mm_ukv_dx")
    g["w_ukv"] = _mm(s["ckvn"], dkvbm, ta=True, name=f"{tag}_mm_ukv_dw")
    dbcq, g["g_cq"] = _rms_bwd(dcqn, s["bcq"], w["g_cq"], name=f"{tag}_rms_cq_bwd")
    dbckv, g["g_ckv"] = _rms_bwd(dckvn, s["bckv"], w["g_ckv"], name=f"{tag}_rms_ckv_bwd")
    (dpb,) = s["vjp_b1"]((dbcq, dbckv, dbkr, dzb))
    dqc, dkc, dvc, _, _ = _attn_bwd(s["qc"], s["kc"], s["vc"], s["yc"], s["lsec"], dyc, scale=DH ** -0.5,
                                 window=WINDOW, name=f"{tag}_attn_c_bwd")
    delta_c = jnp.sum(dyc * s["yc"], axis=-1)
    g["sinks"] = -jnp.sum(jnp.exp(w["sinks"][:, None] - s["lsec"][:, :, 0]) * delta_c, axis=1)
    (dpc,) = s["vjp_c"]((dqc, _kv_group_sum(dkc), _kv_group_sum(dvc), dzc))
    dproj = jnp.concatenate([dpa, dpb, dpc, dgates, jnp.zeros((L, NP_IN - G_HI), F32)], axis=1)
    dhn = _mm(dproj, w["w_in"], tb=True, name=f"{tag}_mm_in_dx")
    g["w_in"] = _mm(s["hn"], dproj, ta=True, name=f"{tag}_mm_in_dw")
    dh, g["norm_g"] = _rms_bwd(dhn, s["h"], w["norm_g"], name=f"{tag}_rms_in_bwd")
    return dout + dh, g


def _cols_from_shards(g):
    return jnp.moveaxis(g, 0, 1).reshape(g.shape[1], N_DEV * g.shape[2])


def _cols_to_shards(w):
    R = w.shape[0]
    return jnp.moveaxis(w.reshape(R, N_DEV, w.shape[1] // N_DEV), 1, 0)


def _pad_in(w):
    parts, at = [], 0
    for lo, hi, dst in _RUNS:
        if dst > at:
            parts.append(jnp.zeros((w.shape[0], dst - at), w.dtype))
        parts.append(w[:, lo:hi])
        at = dst + hi - lo
    parts.append(jnp.zeros((w.shape[0], NP_IN - at), w.dtype))
    return jnp.concatenate(parts, axis=1)


def _unpad_in(w):
    return jnp.concatenate([w[:, dst:dst + hi - lo] for lo, hi, dst in _RUNS], axis=1)


_SMALL = (("norm_g", DEPTH * D_MODEL), ("b_f", DEPTH * HEADS), ("g_cq", DEPTH * MLA_QLORA),
          ("g_ckv", DEPTH * MLA_KVLORA), ("sinks", DEPTH * HEADS), ("final_g", D_MODEL), ("loss", 1),
          ("meta", N_META * D_MODEL))
SMALL_ROWS = 168


def _pack_small(d):
    parts = []
    for name, size in _SMALL:
        padded = -(-size // 128) * 128
        v = d[name].reshape(-1).astype(F32) if name in d else jnp.zeros((size,), F32)
        parts.append(jnp.pad(v, (0, padded - size)))
    flat = jnp.concatenate(parts)
    return jnp.pad(flat, (0, SMALL_ROWS * 128 - flat.shape[0])).reshape(SMALL_ROWS, 128)


def _unpack_small(p, shapes):
    flat = p.reshape(-1)
    out, at = {}, 0
    for name, size in _SMALL:
        if name in shapes:
            out[name] = flat[at:at + size].reshape(shapes[name])
        at += -(-size // 128) * 128
    return out


def kernel(x, meta_tokens, norm_g, w_in, b_f, g_cq, g_ckv, w_uq, w_ukv, sinks, w_branch, w_out, final_g, loss_target, m_meta_tokens, m_norm_g, m_w_in, m_b_f, m_g_cq, m_g_ckv, m_w_uq, m_w_ukv, m_sinks, m_w_branch, m_w_out, m_final_g, v_meta_tokens, v_norm_g, v_w_in, v_b_f, v_g_cq, v_g_ckv, v_w_uq, v_w_ukv, v_sinks, v_w_branch, v_w_out, v_final_g):
    S = x.shape[1]
    L = BLK + S
    cx, cy, cc = _me()
    my_idx = 4 * cx + 2 * cy + cc

    gw_in, gw_uq, gw_ukv, gw_br, gw_out, g_meta = _all_gather(
        [w_in.astype(BF16), w_uq.astype(BF16), w_ukv.astype(BF16), w_branch.astype(BF16), w_out.astype(BF16),
         meta_tokens], name="gather_weights")
    layers = []
    for l in range(DEPTH):
        layers.append(dict(
            norm_g=norm_g[l], b_f=b_f[l], g_cq=g_cq[l], g_ckv=g_ckv[l], sinks=sinks[l],
            w_in=_pad_in(_cols_from_shards(gw_in[:, l])),
            w_uq=_cols_from_shards(gw_uq[:, l]),
            w_ukv=_cols_from_shards(gw_ukv[:, l]),
            w_branch=jnp.stack([_cols_from_shards(gw_br[:, l, n]) for n in range(N_BRANCH)]),
            w_out=gw_out[:, l].reshape(D_MODEL, D_MODEL)))
    meta_full = _cols_from_shards(g_meta)

    h = jnp.concatenate([jnp.zeros((PAD, D_MODEL), F32), meta_full, x[0]], axis=0)
    saved = []
    for l in range(DEPTH):
        h, s = _layer_fwd(h, layers[l], l)
        saved.append(s)
    loss_vec, dh, g_final = _loss_head(h, final_g, loss_target[0], name="loss_head")

    grads = [None] * DEPTH
    for l in reversed(range(DEPTH)):
        dh, grads[l] = _layer_bwd(dh, saved[l], layers[l], l)

    def stack(name):
        return jnp.stack([grads[l][name] for l in range(DEPTH)])

    s_in = jnp.stack([_cols_to_shards(_unpad_in(grads[l]["w_in"])) for l in range(DEPTH)], axis=1)
    s_uq = jnp.stack([_cols_to_shards(grads[l]["w_uq"]) for l in range(DEPTH)], axis=1)
    s_ukv = jnp.stack([_cols_to_shards(grads[l]["w_ukv"]) for l in range(DEPTH)], axis=1)
    s_br = jnp.stack([jnp.stack([_cols_to_shards(grads[l]["w_branch"][n]) for n in range(N_BRANCH)], axis=1)
                      for l in range(DEPTH)], axis=1)
    s_out = jnp.stack([grads[l]["w_out"].reshape(N_DEV, D_MODEL // N_DEV, D_MODEL) for l in range(DEPTH)], axis=1)
    r_in, r_uq, r_ukv, r_br, r_out = _exchange([s_in, s_uq, s_ukv, s_br, s_out], name="exchange_grads")

    small = _pack_small(dict(norm_g=stack("norm_g"), b_f=stack("b_f"), g_cq=stack("g_cq"), g_ckv=stack("g_ckv"),
                             sinks=stack("sinks"), final_g=g_final, loss=loss_vec[0, 0:1],
                             meta=dh[PAD:BLK]))
    (g_small,) = _all_gather([small], name="gather_small")

    def adam_big(w_, parts, m_, v_, name):
        shape = w_.shape
        C = shape[-1]
        R = math.prod(shape[:-1])
        outs = _adamw(w_.reshape(R, C), parts.reshape(parts.shape[0], R, C), m_.reshape(R, C), v_.reshape(R, C),
                      name=name)
        return [o.reshape(shape) for o in outs]

    res = {}
    res["w_in"] = adam_big(w_in, r_in, m_w_in, v_w_in, "adam_w_in")
    res["w_uq"] = adam_big(w_uq, r_uq, m_w_uq, v_w_uq, "adam_w_uq")
    res["w_ukv"] = adam_big(w_ukv, r_ukv, m_w_ukv, v_w_ukv, "adam_w_ukv")
    res["w_branch"] = adam_big(w_branch, r_br, m_w_branch, v_w_branch, "adam_w_branch")
    res["w_out"] = adam_big(w_out, r_out, m_w_out, v_w_out, "adam_w_out")

    small_w = dict(norm_g=norm_g, b_f=b_f, g_cq=g_cq, g_ckv=g_ckv, sinks=sinks, final_g=final_g)
    small_m = dict(norm_g=m_norm_g, b_f=m_b_f, g_cq=m_g_cq, g_ckv=m_g_ckv, sinks=m_sinks, final_g=m_final_g)
    small_v = dict(norm_g=v_norm_g, b_f=v_b_f, g_cq=v_g_cq, g_ckv=v_g_ckv, sinks=v_sinks, final_g=v_final_g)
    sm = _adamw(_pack_small(small_w), g_small, _pack_small(small_m), _pack_small(small_v), name="adam_small")
    shapes = {k: a.shape for k, a in small_w.items()}
    shapes_all = dict(shapes, loss=(), meta=(N_META, D_MODEL))
    sm_g = _unpack_small(sm[0], shapes_all)
    sm_d, sm_m, sm_v = (_unpack_small(t, shapes) for t in sm[1:])
    for k in shapes:
        res[k] = [sm_g[k], sm_d[k], sm_m[k], sm_v[k]]
    g_meta_mine = lax.dynamic_slice(sm_g["meta"], (0, my_idx * 128), (N_META, 128))
    res["meta_tokens"] = _adamw(meta_tokens, g_meta_mine[None], m_meta_tokens, v_meta_tokens, name="adam_meta")

    order = ["meta_tokens", "norm_g", "w_in", "b_f", "g_cq", "g_ckv", "w_uq", "w_ukv", "sinks", "w_branch", "w_out",
             "final_g"]
    grad_x = dh[BLK:][None]
    return (sm_g["loss"], grad_x, *[res[k][0] for k in order], *[res[k][1] for k in order],
            *[res[k][2] for k in order], *[res[k][3] for k in order])
```

```python
import functools
import math

import jax
import jax.numpy as jnp
from jax import lax
from jax.experimental import pallas as pl
from jax.experimental.pallas import tpu as pltpu

F32 = jnp.float32
BF16 = jnp.bfloat16

D_MODEL = 1024
DEPTH = 2
N_META = 16
BLK = 128
PAD = BLK - N_META
ROPE_THETA = 10000.0
EPS = 1e-6
NEG = -1e30
BIG = 1e30
HEADS = 8
DH = 64
MLA_NOPE = 64
MLA_ROPE = 32
MLA_QLORA = 384
MLA_KVLORA = 256
SWA_KV_HEADS = 2
WINDOW = 128
BRANCH_W = 512
N_BRANCH = 3
N_IN = 7592

ADAM_LR = 0.001
ADAM_B1 = 0.9
ADAM_B2 = 0.999
ADAM_EPS = 1e-08
ADAM_WD = 0.01
ADAM_STEP = 10

N_DEV = 8
MESH = pl.DeviceIdType.MESH

_RUNS = ((0, 1544, 0), (1544, 2056, 1792), (2056, 2728, 2304), (2728, 3240, 3072), (3240, 4520, 3584), (4520, 7592, 4864))
NP_IN = 8192
A_LO, A_HI = 0, 2304
B_LO, B_HI = 2304, 3584
C_LO, C_HI = 3584, 4864
G_LO, G_HI = 4864, 7936

VMEM_LIMIT = 48 * 1024 * 1024
ATT_T = 384
LANES = 128


def _pick(dim, prefs):
    for p in prefs:
        if dim % p == 0:
            return p
    return dim


def _cparams(sem):
    return pltpu.CompilerParams(dimension_semantics=sem, vmem_limit_bytes=VMEM_LIMIT)


def _mm(a, b, *, ta=False, tb=False, out_dtype=F32, name):
    M = a.shape[1] if ta else a.shape[0]
    K = a.shape[0] if ta else a.shape[1]
    N = b.shape[0] if tb else b.shape[1]
    assert K == (b.shape[1] if tb else b.shape[0])
    tm = _pick(M, (704, 1024, 512, 384, 256, 128))
    tn = _pick(N, (1024, 768, 512, 384, 256, 128))
    tk = _pick(K, (1024, 768, 512, 384, 256, 128))
    nk = K // tk
    dims = (((0 if ta else 1,), (1 if tb else 0,)), ((), ()))

    def body(a_ref, b_ref, o_ref, *scratch):
        r = lax.dot_general(a_ref[...].astype(BF16), b_ref[...].astype(BF16), dims, preferred_element_type=F32)
        if nk == 1:
            o_ref[...] = r.astype(out_dtype)
        else:
            acc = scratch[0]
            k = pl.program_id(2)

            @pl.when(k == 0)
            def _():
                acc[...] = r

            @pl.when(k > 0)
            def _():
                acc[...] += r

            @pl.when(k == nk - 1)
            def _():
                o_ref[...] = acc[...].astype(out_dtype)

    a_spec = pl.BlockSpec((tk, tm), lambda i, j, k: (k, i)) if ta else pl.BlockSpec((tm, tk), lambda i, j, k: (i, k))
    b_spec = pl.BlockSpec((tn, tk), lambda i, j, k: (j, k)) if tb else pl.BlockSpec((tk, tn), lambda i, j, k: (k, j))
    return pl.pallas_call(
        body, name=name,
        grid=(M // tm, N // tn, nk),
        in_specs=[a_spec, b_spec],
        out_specs=pl.BlockSpec((tm, tn), lambda i, j, k: (i, j)),
        out_shape=jax.ShapeDtypeStruct((M, N), out_dtype),
        scratch_shapes=[pltpu.VMEM((tm, tn), F32)] if nk > 1 else [],
        compiler_params=_cparams(("parallel", "parallel", "arbitrary")),
    )(a, b)


def _rms_fwd(x, g, *, name):
    L, D = x.shape
    tm = _pick(L, (384, 256, 128))

    def body(x_ref, g_ref, y_ref):
        xv = x_ref[...]
        rstd = lax.rsqrt(jnp.mean(xv * xv, axis=-1, keepdims=True) + EPS)
        y_ref[...] = xv * rstd * g_ref[...]

    return pl.pallas_call(
        body, name=name, grid=(L // tm,),
        in_specs=[pl.BlockSpec((tm, D), lambda i: (i, 0)), pl.BlockSpec((1, D), lambda i: (0, 0))],
        out_specs=pl.BlockSpec((tm, D), lambda i: (i, 0)),
        out_shape=jax.ShapeDtypeStruct((L, D), F32),
        compiler_params=_cparams(("parallel",)),
    )(x, g.reshape(1, D))


def _rms_bwd(dy, x, g, *, name):
    L, D = x.shape
    tm = _pick(L, (384, 256, 128))

    def body(dy_ref, x_ref, g_ref, dx_ref, dg_ref):
        i = pl.program_id(0)
        xv = x_ref[...]
        dyv = dy_ref[...]
        rstd = lax.rsqrt(jnp.mean(xv * xv, axis=-1, keepdims=True) + EPS)
        xhat = xv * rstd
        part = jnp.sum(dyv * xhat, axis=0, keepdims=True)

        @pl.when(i == 0)
        def _():
            dg_ref[...] = part

        @pl.when(i > 0)
        def _():
            dg_ref[...] += part

        dxh = dyv * g_ref[...]
        dx_ref[...] = rstd * (dxh - xhat * jnp.mean(dxh * xhat, axis=-1, keepdims=True))

    dx, dg = pl.pallas_call(
        body, name=name, grid=(L // tm,),
        in_specs=[pl.BlockSpec((tm, D), lambda i: (i, 0)), pl.BlockSpec((tm, D), lambda i: (i, 0)),
                  pl.BlockSpec((1, D), lambda i: (0, 0))],
        out_specs=[pl.BlockSpec((tm, D), lambda i: (i, 0)), pl.BlockSpec((1, D), lambda i: (0, 0))],
        out_shape=[jax.ShapeDtypeStruct((L, D), F32), jax.ShapeDtypeStruct((1, D), F32)],
        compiler_params=_cparams(("arbitrary",)),
    )(dy, x, g.reshape(1, D))
    return dx, dg.reshape(D)


def _loss_head(h, g, target, *, name):
    L, D = h.shape
    nb = L // BLK

    def body(h_ref, g_ref, t_ref, loss_ref, dh_ref, dg_ref):
        i = pl.program_id(0)

        @pl.when(i == 0)
        def _():
            loss_ref[...] = jnp.zeros_like(loss_ref)
            dg_ref[...] = jnp.zeros_like(dg_ref)
            dh_ref[...] = jnp.zeros_like(dh_ref)

        @pl.when(i > 0)
        def _():
            xv = h_ref[...]
            gv = g_ref[...]
            rstd = lax.rsqrt(jnp.mean(xv * xv, axis=-1, keepdims=True) + EPS)
            xhat = xv * rstd
            err = xhat * gv - t_ref[...]
            row = jnp.mean(err * err, axis=-1, keepdims=True)
            loss_ref[...] += 0.5 * jnp.sum(row, axis=0, keepdims=True)
            dy = err * (1.0 / D)
            dg_ref[...] += jnp.sum(dy * xhat, axis=0, keepdims=True)
            dxh = dy * gv
            dh_ref[...] = rstd * (dxh - xhat * jnp.mean(dxh * xhat, axis=-1, keepdims=True))

    loss, dh, dg = pl.pallas_call(
        body, name=name, grid=(nb,),
        in_specs=[pl.BlockSpec((BLK, D), lambda i: (i, 0)), pl.BlockSpec((1, D), lambda i: (0, 0)),
                  pl.BlockSpec((BLK, D), lambda i: (jnp.maximum(i - 1, 0), 0))],
        out_specs=[pl.BlockSpec((1, 128), lambda i: (0, 0)), pl.BlockSpec((BLK, D), lambda i: (i, 0)),
                   pl.BlockSpec((1, D), lambda i: (0, 0))],
        out_shape=[jax.ShapeDtypeStruct((1, 128), F32), jax.ShapeDtypeStruct((L, D), F32),
                   jax.ShapeDtypeStruct((1, D), F32)],
        compiler_params=_cparams(("arbitrary",)),
    )(h, g.reshape(1, D), target)
    return loss, dh, dg.reshape(D)


_NT = (((1,), (1,)), ((), ()))
_TN = (((0,), (0,)), ((), ()))


def _attn_fwd(q, k, v, kbias, *, T, window=False, sink=None, name):
    H, L, dk = q.shape
    Hkv = k.shape[0]
    dv = v.shape[2]
    G = H // Hkv
    nt = L // T
    Hb = kbias.shape[0]
    reps = T // LANES
    assert not window or T == WINDOW

    def body(*refs):
        q_ref, k_ref, v_ref, kb_ref = refs[:4]
        n = 4
        if sink is not None:
            sk_ref = refs[n]
            n += 1
        o_ref, lse_ref, m_sc, l_sc, acc_sc = refs[n:]
        i = pl.program_id(1)
        qv = q_ref[0]
        if sink is not None:
            m_sc[...] = jnp.broadcast_to(sk_ref[0, :, 0:1], (T, LANES))
            l_sc[...] = jnp.ones_like(l_sc)
        else:
            m_sc[...] = jnp.full_like(m_sc, NEG)
            l_sc[...] = jnp.zeros_like(l_sc)
        acc_sc[...] = jnp.zeros_like(acc_sc)
        row = lax.broadcasted_iota(jnp.int32, (T, T), 0)
        col = lax.broadcasted_iota(jnp.int32, (T, T), 1)

        def tile(j, kind):
            rows = pl.ds(pl.multiple_of(j * T, T), T)
            s = lax.dot_general(qv, k_ref[0, rows, :], _NT, preferred_element_type=F32) - kb_ref[0, j]
            if kind == "diag":
                s = jnp.where(row >= col, s, NEG)
            elif kind == "prev":
                s = jnp.where(col > row, s, NEG)
            m_prev = m_sc[...]
            m_new = jnp.maximum(m_prev, jnp.max(s, axis=1, keepdims=True))
            alpha = jnp.exp(m_prev - m_new)
            p = jnp.exp(s - jnp.tile(m_new, (1, reps)))
            l_sc[...] = alpha * l_sc[...] + jnp.sum(p, axis=1, keepdims=True)
            acc_sc[...] = alpha[:, :dv] * acc_sc[...] + jnp.dot(p.astype(BF16), v_ref[0, rows, :],
                                                                preferred_element_type=F32)
            m_sc[...] = m_new

        if window:
            @pl.when(i > 0)
            def _():
                tile(i - 1, "prev")

            tile(i, "diag")
        else:
            tile(i, "diag")

            def full(j, c):
                tile(j, "full")
                return c

            lax.fori_loop(0, i, full, 0)

        lv = l_sc[...]
        o_ref[0] = acc_sc[...] / lv[:, :dv]
        lse_ref[0] = (m_sc[...] + jnp.log(lv))[:, 0:1]

    in_specs = [pl.BlockSpec((1, T, dk), lambda h, i: (h, i, 0)),
                pl.BlockSpec((1, L, dk), lambda h, i: (h // G, 0, 0)),
                pl.BlockSpec((1, L, dv), lambda h, i: (h // G, 0, 0)),
                pl.BlockSpec((1, nt, 1, T), lambda h, i: (h if Hb > 1 else 0, 0, 0, 0))]
    args = [q, k, v, kbias]
    if sink is not None:
        in_specs += [pl.BlockSpec((1, 1, LANES), lambda h, i: (h, 0, 0))]
        args += [sink]
    o, lse = pl.pallas_call(
        body, name=name, grid=(H, nt),
        in_specs=in_specs,
        out_specs=[pl.BlockSpec((1, T, dv), lambda h, i: (h, i, 0)),
                   pl.BlockSpec((1, T, 1), lambda h, i: (h, i, 0))],
        out_shape=[jax.ShapeDtypeStruct((H, L, dv), F32), jax.ShapeDtypeStruct((H, L, 1), F32)],
        scratch_shapes=[pltpu.VMEM((T, LANES), F32), pltpu.VMEM((T, LANES), F32), pltpu.VMEM((T, dv), F32)],
        compiler_params=_cparams(("parallel", "arbitrary")),
    )(*args)
    return o, lse


def _attn_bwd(q, k, v, kbias, o, lse, do, *, T, window=False, fox=False, name):
    H, L, dk = q.shape
    Hkv = k.shape[0]
    dv = v.shape[2]
    G = H // Hkv
    nt = L // T
    Hb = kbias.shape[0]
    assert not window or T == WINDOW

    def body(*refs):
        q_ref, k_ref, v_ref, kb_ref, o_ref, lse_ref, do_ref = refs[:7]
        dq_ref, dk_ref, dv_ref = refs[7:10]
        n = 10
        if fox:
            dcq_ref, dck_ref = refs[n:n + 2]
            n += 2
        delta_sc = refs[n]
        j = pl.program_id(1)

        @pl.when(j == 0)
        def _():
            dq_ref[...] = jnp.zeros_like(dq_ref)
            if fox:
                dcq_ref[...] = jnp.zeros_like(dcq_ref)

            def dl(i, c):
                rows = pl.ds(pl.multiple_of(i * T, T), T)
                delta_sc[rows, :] = jnp.sum(do_ref[0, rows, :] * o_ref[0, rows, :], axis=1, keepdims=True)
                return c

            lax.fori_loop(0, nt, dl, 0)

        dk_ref[...] = jnp.zeros_like(dk_ref)
        dv_ref[...] = jnp.zeros_like(dv_ref)
        if fox:
            dck_ref[...] = jnp.zeros_like(dck_ref)
        kb = k_ref[0]
        vb = v_ref[0]
        kbias_j = kb_ref[0, j]
        row = lax.broadcasted_iota(jnp.int32, (T, T), 0)
        col = lax.broadcasted_iota(jnp.int32, (T, T), 1)

        def tile(i, kind):
            rows = pl.ds(pl.multiple_of(i * T, T), T)
            qb = q_ref[0, rows, :]
            dob = do_ref[0, rows, :].astype(BF16)
            s = lax.dot_general(qb, kb, _NT, preferred_element_type=F32) - kbias_j
            if kind == "diag":
                s = jnp.where(row >= col, s, NEG)
            elif kind == "prev":
                s = jnp.where(col > row, s, NEG)
            p = jnp.exp(s - lse_ref[0, rows, :])
            dp = lax.dot_general(dob, vb, _NT, preferred_element_type=F32)
            ds = p * (dp - delta_sc[rows, :])
            dsb = ds.astype(BF16)
            dv_ref[0] += lax.dot_general(p.astype(BF16), dob, _TN, preferred_element_type=F32)
            dk_ref[0] += lax.dot_general(dsb, qb, _TN, preferred_element_type=F32)
            dq_ref[0, rows, :] += jnp.dot(dsb, kb, preferred_element_type=F32)
            if fox:
                dcq_ref[0, rows, :] += jnp.sum(ds, axis=1, keepdims=True)
                dck_ref[0, 0] += -jnp.sum(ds, axis=0, keepdims=True)

        tile(j, "diag")
        if window:
            @pl.when(j + 1 < nt)
            def _():
                tile(j + 1, "prev")
        else:
            def full(i, c):
                tile(i, "full")
                return c

            lax.fori_loop(j + 1, nt, full, 0)

    in_specs = [pl.BlockSpec((1, L, dk), lambda h, j: (h, 0, 0)),
                pl.BlockSpec((1, T, dk), lambda h, j: (h // G, j, 0)),
                pl.BlockSpec((1, T, dv), lambda h, j: (h // G, j, 0)),
                pl.BlockSpec((1, nt, 1, T), lambda h, j: (h if Hb > 1 else 0, 0, 0, 0)),
                pl.BlockSpec((1, L, dv), lambda h, j: (h, 0, 0)),
                pl.BlockSpec((1, L, 1), lambda h, j: (h, 0, 0)),
                pl.BlockSpec((1, L, dv), lambda h, j: (h, 0, 0))]
    out_specs = [pl.BlockSpec((1, L, dk), lambda h, j: (h, 0, 0)),
                 pl.BlockSpec((1, T, dk), lambda h, j: (h, j, 0)),
                 pl.BlockSpec((1, T, dv), lambda h, j: (h, j, 0))]
    out_shape = [jax.ShapeDtypeStruct((H, L, dk), F32), jax.ShapeDtypeStruct((H, L, dk), F32),
                 jax.ShapeDtypeStruct((H, L, dv), F32)]
    if fox:
        out_specs += [pl.BlockSpec((1, L, 1), lambda h, j: (h, 0, 0)),
                      pl.BlockSpec((1, 1, 1, T), lambda h, j: (h, j, 0, 0))]
        out_shape += [jax.ShapeDtypeStruct((H, L, 1), F32), jax.ShapeDtypeStruct((H, nt, 1, T), F32)]
    outs = pl.pallas_call(
        body, name=name, grid=(H, nt),
        in_specs=in_specs, out_specs=out_specs, out_shape=out_shape,
        scratch_shapes=[pltpu.VMEM((L, 1), F32)],
        compiler_params=_cparams(("parallel", "arbitrary")),
    )(q, k, v, kbias, o, lse, do)
    return outs if fox else (*outs, None, None)


def _attn_operands(q, k, v, scale, T, ct=None):
    L = q.shape[1]
    padb = jnp.where(jnp.arange(L) < PAD, BIG, 0.0).astype(F32)[None]
    kb = padb if ct is None else ct + padb
    return (q * scale).astype(BF16), k.astype(BF16), v.astype(BF16), kb.reshape(kb.shape[0], L // T, 1, T)


def _adamw(w, gparts, m, v, *, name):
    n, R, C = gparts.shape
    tr = _pick(R, (128, 64, 32, 16, 8))
    c1 = 1.0 - ADAM_B1 ** ADAM_STEP
    c2 = 1.0 - ADAM_B2 ** ADAM_STEP

    def body(w_ref, g_ref, m_ref, v_ref, go_ref, d_ref, mo_ref, vo_ref):
        g = g_ref[0].astype(F32)
        for t in range(1, n):
            g = g + g_ref[t].astype(F32)
        mn = ADAM_B1 * m_ref[...] + (1.0 - ADAM_B1) * g
        vn = ADAM_B2 * v_ref[...] + (1.0 - ADAM_B2) * (g * g)
        go_ref[...] = g
        mo_ref[...] = mn
        vo_ref[...] = vn
        d_ref[...] = -ADAM_LR * ((mn / c1) / (jnp.sqrt(vn / c2) + ADAM_EPS) + ADAM_WD * w_ref[...])

    spec = pl.BlockSpec((tr, C), lambda i: (i, 0))
    return pl.pallas_call(
        body, name=name, grid=(R // tr,),
        in_specs=[spec, pl.BlockSpec((n, tr, C), lambda i: (0, i, 0)), spec, spec],
        out_specs=[spec] * 4,
        out_shape=[jax.ShapeDtypeStruct((R, C), F32)] * 4,
        compiler_params=_cparams(("parallel",)),
    )(w, gparts, m, v)


def _me():
    return lax.axis_index("x"), lax.axis_index("y"), lax.axis_index("c")


def _all_gather(arrs, *, name):
    n = len(arrs)

    def body(*refs):
        ins, outs = refs[:n], refs[n:2 * n]
        send_sems, recv_sems, local_sems = refs[2 * n:]
        x, y, c = _me()
        me, sibling = (x, y, c), (x, y, 1 - c)
        chips = [(1 - x, y), (x, 1 - y), (1 - x, 1 - y)]

        def idx(p):
            return 4 * p[0] + 2 * p[1] + p[2]

        def copy(t, k, block, to, src=None):
            dst = outs[t].at[idx(block)]
            return pltpu.make_async_remote_copy(
                src_ref=dst if src is None else src, dst_ref=dst,
                send_sem=send_sems.at[t, k], recv_sem=recv_sems.at[t, k],
                device_id=to, device_id_type=MESH)

        mine = [pltpu.make_async_copy(ins[t], outs[t].at[idx(me)], local_sems.at[t]) for t in range(n)]
        for cp in mine:
            cp.start()
        first = []
        for t in range(n):
            first.append(copy(t, 0, me, sibling, src=ins[t]))
            first += [copy(t, 1 + j, me, (*chip, c), src=ins[t]) for j, chip in enumerate(chips)]
        for cp in first:
            cp.start()
        passed = [[copy(t, 4 + j, (*chip, c), sibling) for j, chip in enumerate(chips)] for t in range(n)]
        for j, chip in enumerate(chips):
            for t in range(n):
                copy(t, 1 + j, (*chip, c), me).wait_recv()
                passed[t][j].start()
        for t in range(n):
            copy(t, 0, sibling, me).wait_recv()
            for j, chip in enumerate(chips):
                copy(t, 4 + j, (*chip, 1 - c), me).wait_recv()
        for cp in first:
            cp.wait_send()
        for t in range(n):
            for cp in passed[t]:
                cp.wait_send()
        for cp in mine:
            cp.wait()

    any_spec = pl.BlockSpec(memory_space=pl.ANY)
    return pl.pallas_call(
        body, name=name,
        in_specs=[any_spec] * n, out_specs=[any_spec] * n,
        out_shape=[jax.ShapeDtypeStruct((N_DEV, *a.shape), a.dtype) for a in arrs],
        scratch_shapes=[pltpu.SemaphoreType.DMA((n, 7)), pltpu.SemaphoreType.DMA((n, 7)),
                        pltpu.SemaphoreType.DMA((n,))],
        compiler_params=pltpu.CompilerParams(has_side_effects=True),
    )(*arrs)


def _exchange(arrs, *, name):
    n = len(arrs)

    def body(*refs):
        ins, outs = refs[:n], refs[n:2 * n]
        send_sems, recv_sems, local_sems = refs[2 * n:]
        x, y, c = _me()
        my_idx = 4 * x + 2 * y + c
        mine = [pltpu.make_async_copy(ins[t].at[my_idx], outs[t].at[my_idx], local_sems.at[t]) for t in range(n)]
        for cp in mine:
            cp.start()
        copies = []
        for t in range(n):
            for k in range(1, N_DEV):
                bx, by, bc = (k >> 2) & 1, (k >> 1) & 1, k & 1
                peer = (x ^ bx, y ^ by, c ^ bc)
                peer_idx = 4 * peer[0] + 2 * peer[1] + peer[2]
                copies.append((pltpu.make_async_remote_copy(
                    src_ref=ins[t].at[peer_idx], dst_ref=outs[t].at[my_idx],
                    send_sem=send_sems.at[t, k - 1], recv_sem=recv_sems.at[t, k - 1],
                    device_id=peer, device_id_type=MESH),
                    pltpu.make_async_remote_copy(
                    src_ref=ins[t].at[peer_idx], dst_ref=outs[t].at[peer_idx],
                    send_sem=send_sems.at[t, k - 1], recv_sem=recv_sems.at[t, k - 1],
                    device_id=peer, device_id_type=MESH)))
        for snd, _ in copies:
            snd.start()
        for _, rcv in copies:
            rcv.wait_recv()
        for snd, _ in copies:
            snd.wait_send()
        for cp in mine:
            cp.wait()

    any_spec = pl.BlockSpec(memory_space=pl.ANY)
    return pl.pallas_call(
        body, name=name,
        in_specs=[any_spec] * n, out_specs=[any_spec] * n,
        out_shape=[jax.ShapeDtypeStruct(a.shape, a.dtype) for a in arrs],
        scratch_shapes=[pltpu.SemaphoreType.DMA((n, 7)), pltpu.SemaphoreType.DMA((n, 7)),
                        pltpu.SemaphoreType.DMA((n,))],
        compiler_params=pltpu.CompilerParams(has_side_effects=True),
    )(*arrs)


def _rope_tables(L, half):
    pos = (jnp.arange(L) - PAD).astype(F32)
    inv = ROPE_THETA ** (-jnp.arange(half, dtype=F32) / half)
    ang = pos[:, None] * inv[None, :]
    return jnp.cos(ang)[:, None, :], jnp.sin(ang)[:, None, :]


def _rope(x):
    half = x.shape[-1] // 2
    cos, sin = _rope_tables(x.shape[0], half)
    x1, x2 = x[..., :half], x[..., half:]
    return jnp.concatenate([x1 * cos - x2 * sin, x2 * cos + x1 * sin], axis=-1)


def _heads(t, H):
    L = t.shape[0]
    return t.reshape(L, H, t.shape[1] // H).transpose(1, 0, 2)


def _glue_a(pa, b_f):
    q = _heads(pa[:, 0:512], HEADS)
    k = _heads(pa[:, 512:1024], HEADS)
    v = _heads(pa[:, 1024:1536], HEADS)
    log_f = jax.nn.log_sigmoid(pa[:, 1536:1536 + HEADS] + b_f)
    return q, k, v, jnp.cumsum(log_f, axis=0).T, pa[:, 1792:2304]


def _glue_b1(pb):
    return pb[:, 0:384], pb[:, 384:640], pb[:, 640:672], pb[:, 768:1280]


def _glue_b2(qb, kvb, kr):
    L = qb.shape[0]
    q3 = qb.reshape(L, HEADS, MLA_NOPE + MLA_ROPE)
    q = jnp.concatenate([q3[..., :MLA_NOPE], _rope(q3[..., MLA_NOPE:])], axis=-1).transpose(1, 0, 2)
    kv3 = kvb.reshape(L, HEADS, MLA_NOPE + DH)
    krr = jnp.broadcast_to(_rope(kr.reshape(L, 1, MLA_ROPE)), (L, HEADS, MLA_ROPE))
    k = jnp.concatenate([kv3[..., :MLA_NOPE], krr], axis=-1).transpose(1, 0, 2)
    v = kv3[..., MLA_NOPE:].transpose(1, 0, 2)
    return q, k, v


def _glue_c(pc):
    L = pc.shape[0]
    q = _rope(pc[:, 0:512].reshape(L, HEADS, DH)).transpose(1, 0, 2)
    k = _rope(pc[:, 512:640].reshape(L, SWA_KV_HEADS, DH)).transpose(1, 0, 2)
    v = pc[:, 640:768].reshape(L, SWA_KV_HEADS, DH).transpose(1, 0, 2)
    return q, k, v, pc[:, 768:1280]


def _glue_gate(y, z):
    L = z.shape[0]
    return y.transpose(1, 0, 2).reshape(L, BRANCH_W) * jax.nn.silu(z)


def _glue_merge(gates, pa, pb, pc):
    g = jax.nn.sigmoid(gates)
    return g[:, 0:1024] * pa + g[:, 1024:2048] * pb + g[:, 2048:3072] * pc


def _kv_group_sum(d):
    H, L, dd = d.shape
    return d.reshape(SWA_KV_HEADS, H // SWA_KV_HEADS, L, dd).sum(axis=1)


SCALE_A = DH ** -0.5
SCALE_B = (MLA_NOPE + MLA_ROPE) ** -0.5
SCALE_C = DH ** -0.5


def _layer_fwd(h, w, l):
    tag = f"l{l}"
    hn = _rms_fwd(h, w["norm_g"], name=f"{tag}_rms_in")
    proj = _mm(hn, w["w_in"], name=f"{tag}_mm_in")
    (qa, ka, va, ct, za), vjp_a = jax.vjp(_glue_a, proj[:, A_LO:A_HI], w["b_f"])
    ops_a = _attn_operands(qa, ka, va, SCALE_A, ATT_T, ct)
    ya, lsea = _attn_fwd(*ops_a, T=ATT_T, name=f"{tag}_attn_a")
    (bcq, bckv, bkr, zb), vjp_b1 = jax.vjp(_glue_b1, proj[:, B_LO:B_HI])
    cqn = _rms_fwd(bcq, w["g_cq"], name=f"{tag}_rms_cq")
    ckvn = _rms_fwd(bckv, w["g_ckv"], name=f"{tag}_rms_ckv")
    qbm = _mm(cqn, w["w_uq"], name=f"{tag}_mm_uq")
    kvbm = _mm(ckvn, w["w_ukv"], name=f"{tag}_mm_ukv")
    (qb, kb, vb), vjp_b2 = jax.vjp(_glue_b2, qbm, kvbm, bkr)
    ops_b = _attn_operands(qb, kb, vb, SCALE_B, ATT_T)
    yb, lseb = _attn_fwd(*ops_b, T=ATT_T, name=f"{tag}_attn_b")
    (qc, kc, vc, zc), vjp_c = jax.vjp(_glue_c, proj[:, C_LO:C_HI])
    ops_c = _attn_operands(qc, kc, vc, SCALE_C, WINDOW)
    sink = jnp.broadcast_to(w["sinks"][:, None, None], (HEADS, 1, LANES))
    yc, lsec = _attn_fwd(*ops_c, T=WINDOW, window=True, sink=sink, name=f"{tag}_attn_c")
    us, vjp_gates = [], []
    for y, z in ((ya, za), (yb, zb), (yc, zc)):
        u, vj = jax.vjp(_glue_gate, y, z)
        us.append(u)
        vjp_gates.append(vj)
    pbr = [_mm(us[n], w["w_branch"][n], name=f"{tag}_mm_br{n}") for n in range(N_BRANCH)]
    merged, vjp_merge = jax.vjp(_glue_merge, proj[:, G_LO:G_HI], *pbr)
    out = h + _mm(merged, w["w_out"], name=f"{tag}_mm_out")
    saved = dict(h=h, hn=hn, vjp_a=vjp_a, vjp_b1=vjp_b1, vjp_b2=vjp_b2, vjp_c=vjp_c, vjp_gates=vjp_gates,
                 vjp_merge=vjp_merge, ops_a=ops_a, ya=ya, lsea=lsea, bcq=bcq, bckv=bckv, cqn=cqn, ckvn=ckvn,
                 ops_b=ops_b, yb=yb, lseb=lseb, ops_c=ops_c, yc=yc, lsec=lsec, us=us, merged=merged)
    return out, saved


def _layer_bwd(dout, s, w, l):
    tag = f"l{l}"
    L = dout.shape[0]
    g = {}
    dmerged = _mm(dout, w["w_out"], tb=True, name=f"{tag}_mm_out_dx")
    g["w_out"] = _mm(s["merged"], dout, ta=True, name=f"{tag}_mm_out_dw")
    dgates, *dpbr = s["vjp_merge"](dmerged)
    dus = [_mm(dpbr[n], w["w_branch"][n], tb=True, name=f"{tag}_mm_br{n}_dx") for n in range(N_BRANCH)]
    g["w_branch"] = jnp.stack([_mm(s["us"][n], dpbr[n], ta=True, name=f"{tag}_mm_br{n}_dw") for n in range(N_BRANCH)])
    (dya, dza), (dyb, dzb), (dyc, dzc) = [s["vjp_gates"][n](dus[n]) for n in range(N_BRANCH)]
    dqa, dka, dva, dcq, dck = _attn_bwd(*s["ops_a"], s["ya"], s["lsea"], dya, T=ATT_T, fox=True,
                                        name=f"{tag}_attn_a_bwd")
    dct = dcq[:, :, 0] + dck.reshape(HEADS, L)
    dpa, g["b_f"] = s["vjp_a"]((dqa * SCALE_A, dka, dva, dct, dza))
    dqb, dkb, dvb, _, _ = _attn_bwd(*s["ops_b"], s["yb"], s["lseb"], dyb, T=ATT_T, name=f"{tag}_attn_b_bwd")
    dqbm, dkvbm, dbkr = s["vjp_b2"]((dqb * SCALE_B, dkb, dvb))
    dcqn = _mm(dqbm, w["w_uq"], tb=True, name=f"{tag}_mm_uq_dx")
    g["w_uq"] = _mm(s["cqn"], dqbm, ta=True, name=f"{tag}_mm_uq_dw")
    dckvn = _mm(dkvbm, w["w_ukv"], tb=True, name=f"{tag}_mm_ukv_dx")
    g["w_ukv"] = _mm(s["ckvn"], dkvbm, ta=True, name=f"{tag}_mm_ukv_dw")
    dbcq, g["g_cq"] = _rms_bwd(dcqn, s["bcq"], w["g_cq"], name=f"{tag}_rms_cq_bwd")
    dbckv, g["g_ckv"] = _rms_bwd(dckvn, s["bckv"], w["g_ckv"], name=f"{tag}_rms_ckv_bwd")
    (dpb,) = s["vjp_b1"]((dbcq, dbckv, dbkr, dzb))
    dqc, dkc, dvc, _, _ = _attn_bwd(*s["ops_c"], s["yc"], s["lsec"], dyc, T=WINDOW, window=True,
                                    name=f"{tag}_attn_c_bwd")
    delta_c = jnp.sum(dyc * s["yc"], axis=-1)
    g["sinks"] = -jnp.sum(jnp.exp(w["sinks"][:, None] - s["lsec"][:, :, 0]) * delta_c, axis=1)
    (dpc,) = s["vjp_c"]((dqc * SCALE_C, _kv_group_sum(dkc), _kv_group_sum(dvc), dzc))
    dproj = jnp.concatenate([dpa, dpb, dpc, dgates, jnp.zeros((L, NP_IN - G_HI), F32)], axis=1)
    dhn = _mm(dproj, w["w_in"], tb=True, name=f"{tag}_mm_in_dx")
    g["w_in"] = _mm(s["hn"], dproj, ta=True, name=f"{tag}_mm_in_dw")
    dh, g["norm_g"] = _rms_bwd(dhn, s["h"], w["norm_g"], name=f"{tag}_rms_in_bwd")
    return dout + dh, g


def _cols_from_shards(g):
    return jnp.moveaxis(g, 0, 1).reshape(g.shape[1], N_DEV * g.shape[2])


def _cols_to_shards(w):
    R = w.shape[0]
    return jnp.moveaxis(w.reshape(R, N_DEV, w.shape[1] // N_DEV), 1, 0)


def _pad_in(w):
    parts, at = [], 0
    for lo, hi, dst in _RUNS:
        if dst > at:
            parts.append(jnp.zeros((w.shape[0], dst - at), w.dtype))
        parts.append(w[:, lo:hi])
        at = dst + hi - lo
    parts.append(jnp.zeros((w.shape[0], NP_IN - at), w.dtype))
    return jnp.concatenate(parts, axis=1)


def _unpad_in(w):
    return jnp.concatenate([w[:, dst:dst + hi - lo] for lo, hi, dst in _RUNS], axis=1)


_SMALL = (("norm_g", DEPTH * D_MODEL), ("b_f", DEPTH * HEADS), ("g_cq", DEPTH * MLA_QLORA),
          ("g_ckv", DEPTH * MLA_KVLORA), ("sinks", DEPTH * HEADS), ("final_g", D_MODEL), ("loss", 1),
          ("meta", N_META * D_MODEL))
SMALL_ROWS = 168


def _pack_small(d):
    parts = []
    for name, size in _SMALL:
        padded = -(-size // 128) * 128
        v = d[name].reshape(-1).astype(F32) if name in d else jnp.zeros((size,), F32)
        parts.append(jnp.pad(v, (0, padded - size)))
    flat = jnp.concatenate(parts)
    return jnp.pad(flat, (0, SMALL_ROWS * 128 - flat.shape[0])).reshape(SMALL_ROWS, 128)


def _unpack_small(p, shapes):
    flat = p.reshape(-1)
    out, at = {}, 0
    for name, size in _SMALL:
        if name in shapes:
            out[name] = flat[at:at + size].reshape(shapes[name])
        at += -(-size // 128) * 128
    return out


def kernel(x, meta_tokens, norm_g, w_in, b_f, g_cq, g_ckv, w_uq, w_ukv, sinks, w_branch, w_out, final_g, loss_target, m_meta_tokens, m_norm_g, m_w_in, m_b_f, m_g_cq, m_g_ckv, m_w_uq, m_w_ukv, m_sinks, m_w_branch, m_w_out, m_final_g, v_meta_tokens, v_norm_g, v_w_in, v_b_f, v_g_cq, v_g_ckv, v_w_uq, v_w_ukv, v_sinks, v_w_branch, v_w_out, v_final_g):
    S = x.shape[1]
    L = BLK + S
    cx, cy, cc = _me()
    my_idx = 4 * cx + 2 * cy + cc

    gw_in, gw_uq, gw_ukv, gw_br, gw_out, g_meta = _all_gather(
        [w_in.astype(BF16), w_uq.astype(BF16), w_ukv.astype(BF16), w_branch.astype(BF16), w_out.astype(BF16),
         meta_tokens], name="gather_weights")
    layers = []
    for l in range(DEPTH):
        layers.append(dict(
            norm_g=norm_g[l], b_f=b_f[l], g_cq=g_cq[l], g_ckv=g_ckv[l], sinks=sinks[l],
            w_in=_pad_in(_cols_from_shards(gw_in[:, l])),
            w_uq=_cols_from_shards(gw_uq[:, l]),
            w_ukv=_cols_from_shards(gw_ukv[:, l]),
            w_branch=jnp.stack([_cols_from_shards(gw_br[:, l, n]) for n in range(N_BRANCH)]),
            w_out=gw_out[:, l].reshape(D_MODEL, D_MODEL)))
    meta_full = _cols_from_shards(g_meta)

    h = jnp.concatenate([jnp.zeros((PAD, D_MODEL), F32), meta_full, x[0]], axis=0)
    saved = []
    for l in range(DEPTH):
        h, s = _layer_fwd(h, layers[l], l)
        saved.append(s)
    loss_vec, dh, g_final = _loss_head(h, final_g, loss_target[0], name="loss_head")

    grads = [None] * DEPTH
    for l in reversed(range(DEPTH)):
        dh, grads[l] = _layer_bwd(dh, saved[l], layers[l], l)

    def stack(name):
        return jnp.stack([grads[l][name] for l in range(DEPTH)])

    s_in = jnp.stack([_cols_to_shards(_unpad_in(grads[l]["w_in"])) for l in range(DEPTH)], axis=1)
    s_uq = jnp.stack([_cols_to_shards(grads[l]["w_uq"]) for l in range(DEPTH)], axis=1)
    s_ukv = jnp.stack([_cols_to_shards(grads[l]["w_ukv"]) for l in range(DEPTH)], axis=1)
    s_br = jnp.stack([jnp.stack([_cols_to_shards(grads[l]["w_branch"][n]) for n in range(N_BRANCH)], axis=1)
                      for l in range(DEPTH)], axis=1)
    s_out = jnp.stack([grads[l]["w_out"].reshape(N_DEV, D_MODEL // N_DEV, D_MODEL) for l in range(DEPTH)], axis=1)
    r_in, r_uq, r_ukv, r_br, r_out = _exchange([t.astype(BF16) for t in (s_in, s_uq, s_ukv, s_br, s_out)],
                                               name="exchange_grads")

    small = _pack_small(dict(norm_g=stack("norm_g"), b_f=stack("b_f"), g_cq=stack("g_cq"), g_ckv=stack("g_ckv"),
                             sinks=stack("sinks"), final_g=g_final, loss=loss_vec[0, 0:1],
                             meta=dh[PAD:BLK]))
    (g_small,) = _all_gather([small], name="gather_small")

    def adam_big(w_, parts, m_, v_, name):
        shape = w_.shape
        C = shape[-1]
        R = math.prod(shape[:-1])
        outs = _adamw(w_.reshape(R, C), parts.reshape(parts.shape[0], R, C), m_.reshape(R, C), v_.reshape(R, C),
                      name=name)
        return [o.reshape(shape) for o in outs]

    res = {}
    res["w_in"] = adam_big(w_in, r_in, m_w_in, v_w_in, "adam_w_in")
    res["w_uq"] = adam_big(w_uq, r_uq, m_w_uq, v_w_uq, "adam_w_uq")
    res["w_ukv"] = adam_big(w_ukv, r_ukv, m_w_ukv, v_w_ukv, "adam_w_ukv")
    res["w_branch"] = adam_big(w_branch, r_br, m_w_branch, v_w_branch, "adam_w_branch")
    res["w_out"] = adam_big(w_out, r_out, m_w_out, v_w_out, "adam_w_out")

    small_w = dict(norm_g=norm_g, b_f=b_f, g_cq=g_cq, g_ckv=g_ckv, sinks=sinks, final_g=final_g)
    small_m = dict(norm_g=m_norm_g, b_f=m_b_f, g_cq=m_g_cq, g_ckv=m_g_ckv, sinks=m_sinks, final_g=m_final_g)
    small_v = dict(norm_g=v_norm_g, b_f=v_b_f, g_cq=v_g_cq, g_ckv=v_g_ckv, sinks=v_sinks, final_g=v_final_g)
    sm = _adamw(_pack_small(small_w), g_small, _pack_small(small_m), _pack_small(small_v), name="adam_small")
    shapes = {k: a.shape for k, a in small_w.items()}
    shapes_all = dict(shapes, loss=(), meta=(N_META, D_MODEL))
    sm_g = _unpack_small(sm[0], shapes_all)
    sm_d, sm_m, sm_v = (_unpack_small(t, shapes) for t in sm[1:])
    for k in shapes:
        res[k] = [sm_g[k], sm_d[k], sm_m[k], sm_v[k]]
    g_meta_mine = lax.dynamic_slice(sm_g["meta"], (0, my_idx * 128), (N_META, 128))
    res["meta_tokens"] = _adamw(meta_tokens, g_meta_mine[None], m_meta_tokens, v_meta_tokens, name="adam_meta")

    order = ["meta_tokens", "norm_g", "w_in", "b_f", "g_cq", "g_ckv", "w_uq", "w_ukv", "sinks", "w_branch", "w_out",
             "final_g"]
    grad_x = dh[BLK:][None]
    return (sm_g["loss"], grad_x, *[res[k][0] for k in order], *[res[k][1] for k in order],
            *[res[k][2] for k in order], *[res[k][3] for k in order])
```

```python
import functools
import math

import jax
import jax.numpy as jnp
from jax import lax
from jax.experimental import pallas as pl
from jax.experimental.pallas import tpu as pltpu

F32 = jnp.float32
BF16 = jnp.bfloat16

D_MODEL = 1024
DEPTH = 2
N_META = 16
BLK = 128
PAD = BLK - N_META
ROPE_THETA = 10000.0
EPS = 1e-6
NEG = -1e30
BIG = 1e30
HEADS = 8
DH = 64
MLA_NOPE = 64
MLA_ROPE = 32
MLA_QLORA = 384
MLA_KVLORA = 256
SWA_KV_HEADS = 2
WINDOW = 128
BRANCH_W = 512
N_BRANCH = 3
N_IN = 7592

ADAM_LR = 0.001
ADAM_B1 = 0.9
ADAM_B2 = 0.999
ADAM_EPS = 1e-08
ADAM_WD = 0.01
ADAM_STEP = 10

N_DEV = 8
MESH = pl.DeviceIdType.MESH

NP_IN = 8192
QKVA, ZA, FA = 0, 1536, 2048
CKV, KR, CQ = 2304, 2560, 2688
GATES = 3072
ZB = 6144
QKVC, ZC = 6912, 7680
_RUNS = ((0, 1536, QKVA), (1536, 1544, FA), (1544, 2056, ZA), (2056, 2440, CQ), (2440, 2696, CKV), (2696, 2728, KR),
         (2728, 3240, ZB), (3240, 4008, QKVC), (4008, 4520, ZC), (4520, 7592, GATES))

VMEM_LIMIT = 48 * 1024 * 1024
ATT_T = 384
ROW_T = 384
LANES = 128


def _pick(dim, prefs):
    for p in prefs:
        if dim % p == 0:
            return p
    return dim


def _cparams(sem):
    return pltpu.CompilerParams(dimension_semantics=sem, vmem_limit_bytes=VMEM_LIMIT)


def _mm(a, b, *, ta=False, tb=False, add=None, out_dtype=F32, name):
    M = a.shape[1] if ta else a.shape[0]
    K = a.shape[0] if ta else a.shape[1]
    N = b.shape[0] if tb else b.shape[1]
    assert K == (b.shape[1] if tb else b.shape[0])
    tm = _pick(M, (704, 1024, 512, 384, 256, 128))
    tn = _pick(N, (1024, 768, 512, 384, 256, 128))
    tk = _pick(K, (1024, 768, 512, 384, 256, 128))
    nk = K // tk
    dims = (((0 if ta else 1,), (1 if tb else 0,)), ((), ()))

    def body(*refs):
        a_ref, b_ref = refs[:2]
        c_ref = refs[2] if add is not None else None
        o_ref = refs[3] if add is not None else refs[2]
        r = lax.dot_general(a_ref[...].astype(BF16), b_ref[...].astype(BF16), dims, preferred_element_type=F32)

        def finish(total):
            if c_ref is not None:
                total = total + c_ref[...]
            o_ref[...] = total.astype(out_dtype)

        if nk == 1:
            finish(r)
        else:
            acc = refs[-1]
            k = pl.program_id(2)

            @pl.when(k == 0)
            def _():
                acc[...] = r

            @pl.when(k > 0)
            def _():
                acc[...] += r

            @pl.when(k == nk - 1)
            def _():
                finish(acc[...])

    a_spec = pl.BlockSpec((tk, tm), lambda i, j, k: (k, i)) if ta else pl.BlockSpec((tm, tk), lambda i, j, k: (i, k))
    b_spec = pl.BlockSpec((tn, tk), lambda i, j, k: (j, k)) if tb else pl.BlockSpec((tk, tn), lambda i, j, k: (k, j))
    o_spec = pl.BlockSpec((tm, tn), lambda i, j, k: (i, j))
    return pl.pallas_call(
        body, name=name,
        grid=(M // tm, N // tn, nk),
        in_specs=[a_spec, b_spec] + ([o_spec] if add is not None else []),
        out_specs=o_spec,
        out_shape=jax.ShapeDtypeStruct((M, N), out_dtype),
        scratch_shapes=[pltpu.VMEM((tm, tn), F32)] if nk > 1 else [],
        compiler_params=_cparams(("parallel", "parallel", "arbitrary")),
    )(*((a, b) if add is None else (a, b, add)))


def _col_spec(tm, width, col):
    assert col % width == 0
    return pl.BlockSpec((tm, width), lambda i, _c=col // width: (i, _c))


def _rms_fwd(x, g, *, col=0, name):
    L = x.shape[0]
    D = g.shape[0]
    tm = ROW_T

    def body(x_ref, g_ref, y_ref):
        xv = x_ref[...]
        rstd = lax.rsqrt(jnp.mean(xv * xv, axis=-1, keepdims=True) + EPS)
        y_ref[...] = (xv * rstd * g_ref[...]).astype(BF16)

    return pl.pallas_call(
        body, name=name, grid=(L // tm,),
        in_specs=[_col_spec(tm, D, col), pl.BlockSpec((1, D), lambda i: (0, 0))],
        out_specs=pl.BlockSpec((tm, D), lambda i: (i, 0)),
        out_shape=jax.ShapeDtypeStruct((L, D), BF16),
        compiler_params=_cparams(("parallel",)),
    )(x, g.reshape(1, D))


def _rms_bwd(dy, x, g, *, col=0, add=None, into=None, name):
    L = x.shape[0]
    D = g.shape[0]
    tm = ROW_T

    def body(*refs):
        dy_ref, x_ref, g_ref = refs[:3]
        add_ref = refs[3] if add is not None else None
        dx_ref, dg_ref = refs[-2:]
        i = pl.program_id(0)
        xv = x_ref[...]
        dyv = dy_ref[...]
        rstd = lax.rsqrt(jnp.mean(xv * xv, axis=-1, keepdims=True) + EPS)
        xhat = xv * rstd
        part = jnp.sum(dyv * xhat, axis=0, keepdims=True)

        @pl.when(i == 0)
        def _():
            dg_ref[...] = part

        @pl.when(i > 0)
        def _():
            dg_ref[...] += part

        dxh = dyv * g_ref[...]
        dx = rstd * (dxh - xhat * jnp.mean(dxh * xhat, axis=-1, keepdims=True))
        if add_ref is not None:
            dx = dx + add_ref[...]
        dx_ref[...] = dx.astype(dx_ref.dtype)

    row = pl.BlockSpec((tm, D), lambda i: (i, 0))
    in_specs = [row, _col_spec(tm, D, col), pl.BlockSpec((1, D), lambda i: (0, 0))]
    args = [dy, x, g.reshape(1, D)]
    aliases = {}
    if add is not None:
        in_specs.append(row)
        args.append(add)
    if into is not None:
        in_specs.append(pl.BlockSpec(memory_space=pl.ANY))
        args.append(into)
        aliases = {len(args) - 1: 0}
        dx_spec, dx_shape = _col_spec(tm, D, col), jax.ShapeDtypeStruct(into.shape, into.dtype)
    else:
        dx_spec, dx_shape = row, jax.ShapeDtypeStruct((L, D), F32)
    dx, dg = pl.pallas_call(
        body, name=name, grid=(L // tm,),
        in_specs=in_specs,
        out_specs=[dx_spec, pl.BlockSpec((1, D), lambda i: (0, 0))],
        out_shape=[dx_shape, jax.ShapeDtypeStruct((1, D), F32)],
        input_output_aliases=aliases,
        compiler_params=_cparams(("arbitrary",)),
    )(*args)
    return dx, dg.reshape(D)


def _loss_head(h, g, target, *, name):
    L, D = h.shape
    nb = L // BLK

    def body(h_ref, g_ref, t_ref, loss_ref, dh_ref, dg_ref):
        i = pl.program_id(0)

        @pl.when(i == 0)
        def _():
            loss_ref[...] = jnp.zeros_like(loss_ref)
            dg_ref[...] = jnp.zeros_like(dg_ref)
            dh_ref[...] = jnp.zeros_like(dh_ref)

        @pl.when(i > 0)
        def _():
            xv = h_ref[...]
            gv = g_ref[...]
            rstd = lax.rsqrt(jnp.mean(xv * xv, axis=-1, keepdims=True) + EPS)
            xhat = xv * rstd
            err = xhat * gv - t_ref[...]
            row = jnp.mean(err * err, axis=-1, keepdims=True)
            loss_ref[...] += 0.5 * jnp.sum(row, axis=0, keepdims=True)
            dy = err * (1.0 / D)
            dg_ref[...] += jnp.sum(dy * xhat, axis=0, keepdims=True)
            dxh = dy * gv
            dh_ref[...] = rstd * (dxh - xhat * jnp.mean(dxh * xhat, axis=-1, keepdims=True))

    loss, dh, dg = pl.pallas_call(
        body, name=name, grid=(nb,),
        in_specs=[pl.BlockSpec((BLK, D), lambda i: (i, 0)), pl.BlockSpec((1, D), lambda i: (0, 0)),
                  pl.BlockSpec((BLK, D), lambda i: (jnp.maximum(i - 1, 0), 0))],
        out_specs=[pl.BlockSpec((1, 128), lambda i: (0, 0)), pl.BlockSpec((BLK, D), lambda i: (i, 0)),
                   pl.BlockSpec((1, D), lambda i: (0, 0))],
        out_shape=[jax.ShapeDtypeStruct((1, 128), F32), jax.ShapeDtypeStruct((L, D), F32),
                   jax.ShapeDtypeStruct((1, D), F32)],
        compiler_params=_cparams(("arbitrary",)),
    )(h, g.reshape(1, D), target)
    return loss, dh, dg.reshape(D)


_NT = (((1,), (1,)), ((), ()))
_TN = (((0,), (0,)), ((), ()))


def _attn_fwd(q, k, v, kbias, *, T, window=False, sink=None, name):
    H, L, dk = q.shape
    Hkv = k.shape[0]
    dv = v.shape[2]
    G = H // Hkv
    nt = L // T
    Hb = kbias.shape[0]
    reps = T // LANES
    assert not window or T == WINDOW

    def body(*refs):
        q_ref, k_ref, v_ref, kb_ref = refs[:4]
        n = 4
        if sink is not None:
            sk_ref = refs[n]
            n += 1
        o_ref, lse_ref, m_sc, l_sc, acc_sc = refs[n:]
        i = pl.program_id(1)
        qv = q_ref[0]
        if sink is not None:
            m_sc[...] = jnp.broadcast_to(sk_ref[0, :, 0:1], (T, LANES))
            l_sc[...] = jnp.ones_like(l_sc)
        else:
            m_sc[...] = jnp.full_like(m_sc, NEG)
            l_sc[...] = jnp.zeros_like(l_sc)
        acc_sc[...] = jnp.zeros_like(acc_sc)
        row = lax.broadcasted_iota(jnp.int32, (T, T), 0)
        col = lax.broadcasted_iota(jnp.int32, (T, T), 1)

        def tile(j, kind):
            rows = pl.ds(pl.multiple_of(j * T, T), T)
            s = lax.dot_general(qv, k_ref[0, rows, :], _NT, preferred_element_type=F32) - kb_ref[0, j]
            if kind == "diag":
                s = jnp.where(row >= col, s, NEG)
            elif kind == "prev":
                s = jnp.where(col > row, s, NEG)
            m_prev = m_sc[...]
            m_new = jnp.maximum(m_prev, jnp.max(s, axis=1, keepdims=True))
            alpha = jnp.exp(m_prev - m_new)
            p = jnp.exp(s - jnp.tile(m_new, (1, reps)))
            l_sc[...] = alpha * l_sc[...] + jnp.sum(p, axis=1, keepdims=True)
            acc_sc[...] = alpha[:, :dv] * acc_sc[...] + jnp.dot(p.astype(BF16), v_ref[0, rows, :],
                                                                preferred_element_type=F32)
            m_sc[...] = m_new

        if window:
            @pl.when(i > 0)
            def _():
                tile(i - 1, "prev")

            tile(i, "diag")
        else:
            tile(i, "diag")

            def full(j, c):
                tile(j, "full")
                return c

            lax.fori_loop(0, i, full, 0)

        lv = l_sc[...]
        o_ref[0] = acc_sc[...] / lv[:, :dv]
        lse_ref[0] = (m_sc[...] + jnp.log(lv))[:, 0:1]

    in_specs = [pl.BlockSpec((1, T, dk), lambda h, i: (h, i, 0)),
                pl.BlockSpec((1, L, dk), lambda h, i: (h // G, 0, 0)),
                pl.BlockSpec((1, L, dv), lambda h, i: (h // G, 0, 0)),
                pl.BlockSpec((1, nt, 1, T), lambda h, i: (h if Hb > 1 else 0, 0, 0, 0))]
    args = [q, k, v, kbias]
    if sink is not None:
        in_specs += [pl.BlockSpec((1, 1, LANES), lambda h, i: (h, 0, 0))]
        args += [sink]
    o, lse = pl.pallas_call(
        body, name=name, grid=(H, nt),
        in_specs=in_specs,
        out_specs=[pl.BlockSpec((1, T, dv), lambda h, i: (h, i, 0)),
                   pl.BlockSpec((1, T, 1), lambda h, i: (h, i, 0))],
        out_shape=[jax.ShapeDtypeStruct((H, L, dv), F32), jax.ShapeDtypeStruct((H, L, 1), F32)],
        scratch_shapes=[pltpu.VMEM((T, LANES), F32), pltpu.VMEM((T, LANES), F32), pltpu.VMEM((T, dv), F32)],
        compiler_params=_cparams(("parallel", "arbitrary")),
    )(*args)
    return o, lse


def _attn_bwd(q, k, v, kbias, o, lse, do, *, T, window=False, fox=False, name):
    H, L, dk = q.shape
    Hkv = k.shape[0]
    dv = v.shape[2]
    G = H // Hkv
    nt = L // T
    Hb = kbias.shape[0]
    assert not window or T == WINDOW

    def body(*refs):
        q_ref, k_ref, v_ref, kb_ref, o_ref, lse_ref, do_ref = refs[:7]
        dq_ref, dk_ref, dv_ref = refs[7:10]
        n = 10
        if fox:
            dcq_ref, dck_ref = refs[n:n + 2]
            n += 2
        delta_sc = refs[n]
        j = pl.program_id(1)

        @pl.when(j == 0)
        def _():
            dq_ref[...] = jnp.zeros_like(dq_ref)
            if fox:
                dcq_ref[...] = jnp.zeros_like(dcq_ref)

            def dl(i, c):
                rows = pl.ds(pl.multiple_of(i * T, T), T)
                delta_sc[rows, :] = jnp.sum(do_ref[0, rows, :] * o_ref[0, rows, :], axis=1, keepdims=True)
                return c

            lax.fori_loop(0, nt, dl, 0)

        dk_ref[...] = jnp.zeros_like(dk_ref)
        dv_ref[...] = jnp.zeros_like(dv_ref)
        if fox:
            dck_ref[...] = jnp.zeros_like(dck_ref)
        kb = k_ref[0]
        vb = v_ref[0]
        kbias_j = kb_ref[0, j]
        row = lax.broadcasted_iota(jnp.int32, (T, T), 0)
        col = lax.broadcasted_iota(jnp.int32, (T, T), 1)

        def tile(i, kind):
            rows = pl.ds(pl.multiple_of(i * T, T), T)
            qb = q_ref[0, rows, :]
            dob = do_ref[0, rows, :].astype(BF16)
            s = lax.dot_general(qb, kb, _NT, preferred_element_type=F32) - kbias_j
            if kind == "diag":
                s = jnp.where(row >= col, s, NEG)
            elif kind == "prev":
                s = jnp.where(col > row, s, NEG)
            p = jnp.exp(s - lse_ref[0, rows, :])
            dp = lax.dot_general(dob, vb, _NT, preferred_element_type=F32)
            ds = p * (dp - delta_sc[rows, :])
            dsb = ds.astype(BF16)
            dv_ref[0] += lax.dot_general(p.astype(BF16), dob, _TN, preferred_element_type=F32)
            dk_ref[0] += lax.dot_general(dsb, qb, _TN, preferred_element_type=F32)
            dq_ref[0, rows, :] += jnp.dot(dsb, kb, preferred_element_type=F32)
            if fox:
                dcq_ref[0, rows, :] += jnp.sum(ds, axis=1, keepdims=True)
                dck_ref[0, 0] += -jnp.sum(ds, axis=0, keepdims=True)

        tile(j, "diag")
        if window:
            @pl.when(j + 1 < nt)
            def _():
                tile(j + 1, "prev")
        else:
            def full(i, c):
                tile(i, "full")
                return c

            lax.fori_loop(j + 1, nt, full, 0)

    in_specs = [pl.BlockSpec((1, L, dk), lambda h, j: (h, 0, 0)),
                pl.BlockSpec((1, T, dk), lambda h, j: (h // G, j, 0)),
                pl.BlockSpec((1, T, dv), lambda h, j: (h // G, j, 0)),
                pl.BlockSpec((1, nt, 1, T), lambda h, j: (h if Hb > 1 else 0, 0, 0, 0)),
                pl.BlockSpec((1, L, dv), lambda h, j: (h, 0, 0)),
                pl.BlockSpec((1, L, 1), lambda h, j: (h, 0, 0)),
                pl.BlockSpec((1, L, dv), lambda h, j: (h, 0, 0))]
    out_specs = [pl.BlockSpec((1, L, dk), lambda h, j: (h, 0, 0)),
                 pl.BlockSpec((1, T, dk), lambda h, j: (h, j, 0)),
                 pl.BlockSpec((1, T, dv), lambda h, j: (h, j, 0))]
    out_shape = [jax.ShapeDtypeStruct((H, L, dk), F32), jax.ShapeDtypeStruct((H, L, dk), F32),
                 jax.ShapeDtypeStruct((H, L, dv), F32)]
    if fox:
        out_specs += [pl.BlockSpec((1, L, 1), lambda h, j: (h, 0, 0)),
                      pl.BlockSpec((1, 1, 1, T), lambda h, j: (h, j, 0, 0))]
        out_shape += [jax.ShapeDtypeStruct((H, L, 1), F32), jax.ShapeDtypeStruct((H, nt, 1, T), F32)]
    outs = pl.pallas_call(
        body, name=name, grid=(H, nt),
        in_specs=in_specs, out_specs=out_specs, out_shape=out_shape,
        scratch_shapes=[pltpu.VMEM((L, 1), F32)],
        compiler_params=_cparams(("parallel", "arbitrary")),
    )(q, k, v, kbias, o, lse, do)
    return outs if fox else (*outs, None, None)


def _adamw(w, gparts, m, v, *, name):
    n, R, C = gparts.shape
    tr = _pick(R, (128, 64, 32, 16, 8))
    c1 = 1.0 - ADAM_B1 ** ADAM_STEP
    c2 = 1.0 - ADAM_B2 ** ADAM_STEP

    def body(w_ref, g_ref, m_ref, v_ref, go_ref, d_ref, mo_ref, vo_ref):
        g = g_ref[0].astype(F32)
        for t in range(1, n):
            g = g + g_ref[t].astype(F32)
        mn = ADAM_B1 * m_ref[...] + (1.0 - ADAM_B1) * g
        vn = ADAM_B2 * v_ref[...] + (1.0 - ADAM_B2) * (g * g)
        go_ref[...] = g
        mo_ref[...] = mn
        vo_ref[...] = vn
        d_ref[...] = -ADAM_LR * ((mn / c1) / (jnp.sqrt(vn / c2) + ADAM_EPS) + ADAM_WD * w_ref[...])

    spec = pl.BlockSpec((tr, C), lambda i: (i, 0))
    return pl.pallas_call(
        body, name=name, grid=(R // tr,),
        in_specs=[spec, pl.BlockSpec((n, tr, C), lambda i: (0, i, 0)), spec, spec],
        out_specs=[spec] * 4,
        out_shape=[jax.ShapeDtypeStruct((R, C), F32)] * 4,
        compiler_params=_cparams(("parallel",)),
    )(w, gparts, m, v)


def _me():
    return lax.axis_index("x"), lax.axis_index("y"), lax.axis_index("c")


def _all_gather(arrs, *, name):
    n = len(arrs)

    def body(*refs):
        ins, outs = refs[:n], refs[n:2 * n]
        send_sems, recv_sems, local_sems = refs[2 * n:]
        x, y, c = _me()
        me, sibling = (x, y, c), (x, y, 1 - c)
        chips = [(1 - x, y), (x, 1 - y), (1 - x, 1 - y)]

        def idx(p):
            return 4 * p[0] + 2 * p[1] + p[2]

        def copy(t, k, block, to, src=None):
            dst = outs[t].at[idx(block)]
            return pltpu.make_async_remote_copy(
                src_ref=dst if src is None else src, dst_ref=dst,
                send_sem=send_sems.at[t, k], recv_sem=recv_sems.at[t, k],
                device_id=to, device_id_type=MESH)

        mine = [pltpu.make_async_copy(ins[t], outs[t].at[idx(me)], local_sems.at[t]) for t in range(n)]
        for cp in mine:
            cp.start()
        first = []
        for t in range(n):
            first.append(copy(t, 0, me, sibling, src=ins[t]))
            first += [copy(t, 1 + j, me, (*chip, c), src=ins[t]) for j, chip in enumerate(chips)]
        for cp in first:
            cp.start()
        passed = [[copy(t, 4 + j, (*chip, c), sibling) for j, chip in enumerate(chips)] for t in range(n)]
        for j, chip in enumerate(chips):
            for t in range(n):
                copy(t, 1 + j, (*chip, c), me).wait_recv()
                passed[t][j].start()
        for t in range(n):
            copy(t, 0, sibling, me).wait_recv()
            for j, chip in enumerate(chips):
                copy(t, 4 + j, (*chip, 1 - c), me).wait_recv()
        for cp in first:
            cp.wait_send()
        for t in range(n):
            for cp in passed[t]:
                cp.wait_send()
        for cp in mine:
            cp.wait()

    any_spec = pl.BlockSpec(memory_space=pl.ANY)
    return pl.pallas_call(
        body, name=name,
        in_specs=[any_spec] * n, out_specs=[any_spec] * n,
        out_shape=[jax.ShapeDtypeStruct((N_DEV, *a.shape), a.dtype) for a in arrs],
        scratch_shapes=[pltpu.SemaphoreType.DMA((n, 7)), pltpu.SemaphoreType.DMA((n, 7)),
                        pltpu.SemaphoreType.DMA((n,))],
        compiler_params=pltpu.CompilerParams(has_side_effects=True),
    )(*arrs)


def _exchange(arrs, *, name):
    n = len(arrs)

    def body(*refs):
        ins, outs = refs[:n], refs[n:2 * n]
        send_sems, recv_sems, local_sems = refs[2 * n:]
        x, y, c = _me()
        my_idx = 4 * x + 2 * y + c
        mine = [pltpu.make_async_copy(ins[t].at[my_idx], outs[t].at[my_idx], local_sems.at[t]) for t in range(n)]
        for cp in mine:
            cp.start()
        copies = []
        for t in range(n):
            for k in range(1, N_DEV):
                bx, by, bc = (k >> 2) & 1, (k >> 1) & 1, k & 1
                peer = (x ^ bx, y ^ by, c ^ bc)
                peer_idx = 4 * peer[0] + 2 * peer[1] + peer[2]
                copies.append((pltpu.make_async_remote_copy(
                    src_ref=ins[t].at[peer_idx], dst_ref=outs[t].at[my_idx],
                    send_sem=send_sems.at[t, k - 1], recv_sem=recv_sems.at[t, k - 1],
                    device_id=peer, device_id_type=MESH),
                    pltpu.make_async_remote_copy(
                    src_ref=ins[t].at[peer_idx], dst_ref=outs[t].at[peer_idx],
                    send_sem=send_sems.at[t, k - 1], recv_sem=recv_sems.at[t, k - 1],
                    device_id=peer, device_id_type=MESH)))
        for snd, _ in copies:
            snd.start()
        for _, rcv in copies:
            rcv.wait_recv()
        for snd, _ in copies:
            snd.wait_send()
        for cp in mine:
            cp.wait()

    any_spec = pl.BlockSpec(memory_space=pl.ANY)
    return pl.pallas_call(
        body, name=name,
        in_specs=[any_spec] * n, out_specs=[any_spec] * n,
        out_shape=[jax.ShapeDtypeStruct(a.shape, a.dtype) for a in arrs],
        scratch_shapes=[pltpu.SemaphoreType.DMA((n, 7)), pltpu.SemaphoreType.DMA((n, 7)),
                        pltpu.SemaphoreType.DMA((n,))],
        compiler_params=pltpu.CompilerParams(has_side_effects=True),
    )(*arrs)


SCALE_A = DH ** -0.5
SCALE_B = (MLA_NOPE + MLA_ROPE) ** -0.5
SCALE_C = DH ** -0.5
DQK_B = MLA_NOPE + MLA_ROPE


def _rope_tables(L, half, width, rope_lane=None):
    pos = (jnp.arange(L) - PAD).astype(F32)
    inv = ROPE_THETA ** (-jnp.arange(half, dtype=F32) / half)
    lane = jnp.arange(width)
    ang = pos[:, None] * inv[lane % half][None, :]
    sign = jnp.where(lane % (2 * half) < half, -1.0, 1.0).astype(F32)
    cos, sin = jnp.cos(ang), jnp.sin(ang) * sign[None, :]
    if rope_lane is not None:
        cos = jnp.where(rope_lane[None, :], cos, 1.0)
        sin = jnp.where(rope_lane[None, :], sin, 0.0)
    return cos, sin


def _rope_lanes(x, cos, sin, half):
    W = x.shape[1]
    lane = lax.broadcasted_iota(jnp.int32, x.shape, 1)
    first = (lane & (2 * half - 1)) < half
    partner = jnp.where(first, pltpu.roll(x, W - half, 1), pltpu.roll(x, half, 1))
    return x * cos + partner * sin


def _tile_lanes(t, width):
    return t if t.shape[1] == width else jnp.tile(t, (1, width // t.shape[1]))


def _split(x, H, d, dst):
    for h in range(H):
        dst[h] = x[:, d * h:d * (h + 1)].astype(dst.dtype)


def _join(src, H):
    return jnp.concatenate([src[h] for h in range(H)], axis=1)


def _head_spec(H, tm, d):
    return pl.BlockSpec((H, tm, d), lambda i: (0, i, 0))


def _prep_a(proj, *, name):
    L = proj.shape[0]
    tm = ROW_T

    def body(x_ref, qo, ko, vo):
        _split(x_ref[:, 0:512] * SCALE_A, HEADS, DH, qo)
        _split(x_ref[:, 512:1024], HEADS, DH, ko)
        _split(x_ref[:, 1024:1536], HEADS, DH, vo)

    return pl.pallas_call(
        body, name=name, grid=(L // tm,),
        in_specs=[_col_spec(tm, 1536, QKVA)],
        out_specs=[_head_spec(HEADS, tm, DH)] * 3,
        out_shape=[jax.ShapeDtypeStruct((HEADS, L, DH), BF16)] * 3,
        compiler_params=_cparams(("parallel",)),
    )(proj)


def _unprep_a(dq, dk, dv, dproj, *, name):
    L = dq.shape[1]
    tm = ROW_T

    def body(dq_ref, dk_ref, dv_ref, _, dp_ref):
        dp_ref[:, 0:512] = (_join(dq_ref, HEADS) * SCALE_A).astype(BF16)
        dp_ref[:, 512:1024] = _join(dk_ref, HEADS).astype(BF16)
        dp_ref[:, 1024:1536] = _join(dv_ref, HEADS).astype(BF16)

    hs = _head_spec(HEADS, tm, DH)
    return pl.pallas_call(
        body, name=name, grid=(L // tm,),
        in_specs=[hs, hs, hs, pl.BlockSpec(memory_space=pl.ANY)],
        out_specs=_col_spec(tm, 1536, QKVA),
        out_shape=jax.ShapeDtypeStruct(dproj.shape, dproj.dtype),
        input_output_aliases={3: 0},
        compiler_params=_cparams(("parallel",)),
    )(dq, dk, dv, dproj)


def _prep_c(proj, tab, *, name):
    L = proj.shape[0]
    tm = ROW_T

    def body(x_ref, cos_ref, sin_ref, qo, ko, vo):
        cos, sin = cos_ref[...], sin_ref[...]
        q = _rope_lanes(x_ref[:, 0:512], _tile_lanes(cos, 512), _tile_lanes(sin, 512), DH // 2)
        _split(q * SCALE_C, HEADS, DH, qo)
        _split(_rope_lanes(x_ref[:, 512:640], cos, sin, DH // 2), SWA_KV_HEADS, DH, ko)
        _split(x_ref[:, 640:768], SWA_KV_HEADS, DH, vo)

    t128 = pl.BlockSpec((tm, LANES), lambda i: (i, 0))
    return pl.pallas_call(
        body, name=name, grid=(L // tm,),
        in_specs=[_col_spec(tm, 768, QKVC), t128, t128],
        out_specs=[_head_spec(HEADS, tm, DH), _head_spec(SWA_KV_HEADS, tm, DH), _head_spec(SWA_KV_HEADS, tm, DH)],
        out_shape=[jax.ShapeDtypeStruct((HEADS, L, DH), BF16), jax.ShapeDtypeStruct((SWA_KV_HEADS, L, DH), BF16),
                   jax.ShapeDtypeStruct((SWA_KV_HEADS, L, DH), BF16)],
        compiler_params=_cparams(("parallel",)),
    )(proj, *tab)


def _unprep_c(dq, dk, dv, tab, dproj, *, name):
    L = dq.shape[1]
    tm = ROW_T
    G = HEADS // SWA_KV_HEADS

    def group_sum(ref):
        parts = []
        for kvh in range(SWA_KV_HEADS):
            t = ref[kvh * G]
            for g in range(1, G):
                t = t + ref[kvh * G + g]
            parts.append(t)
        return jnp.concatenate(parts, axis=1)

    def body(dq_ref, dk_ref, dv_ref, cos_ref, sin_ref, _, dp_ref):
        cos, nsin = cos_ref[...], -sin_ref[...]
        dqv = _join(dq_ref, HEADS) * SCALE_C
        dp_ref[:, 0:512] = _rope_lanes(dqv, _tile_lanes(cos, 512), _tile_lanes(nsin, 512), DH // 2).astype(BF16)
        dp_ref[:, 512:640] = _rope_lanes(group_sum(dk_ref), cos, nsin, DH // 2).astype(BF16)
        dp_ref[:, 640:768] = group_sum(dv_ref).astype(BF16)

    hs = _head_spec(HEADS, tm, DH)
    t128 = pl.BlockSpec((tm, LANES), lambda i: (i, 0))
    return pl.pallas_call(
        body, name=name, grid=(L // tm,),
        in_specs=[hs, hs, hs, t128, t128, pl.BlockSpec(memory_space=pl.ANY)],
        out_specs=_col_spec(tm, 768, QKVC),
        out_shape=jax.ShapeDtypeStruct(dproj.shape, dproj.dtype),
        input_output_aliases={5: 0},
        compiler_params=_cparams(("parallel",)),
    )(dq, dk, dv, *tab, dproj)


def _prep_b(qbm, kvbm, proj, tab_q, tab_k, *, name):
    L = proj.shape[0]
    tm = ROW_T

    def body(q_ref, kv_ref, kr_ref, cq_ref, sq_ref, ck_ref, sk_ref, qo, ko, vo):
        q = _rope_lanes(q_ref[...], cq_ref[...], sq_ref[...], MLA_ROPE // 2) * SCALE_B
        _split(q, HEADS, DQK_B, qo)
        kr = _rope_lanes(kr_ref[...], ck_ref[...], sk_ref[...], MLA_ROPE // 2)[:, :MLA_ROPE].astype(BF16)
        kv = kv_ref[...]
        for h in range(HEADS):
            ko[h] = jnp.concatenate([kv[:, 128 * h:128 * h + MLA_NOPE].astype(BF16), kr], axis=1)
            vo[h] = kv[:, 128 * h + MLA_NOPE:128 * (h + 1)].astype(BF16)

    t128 = pl.BlockSpec((tm, LANES), lambda i: (i, 0))
    t768 = pl.BlockSpec((tm, 768), lambda i: (i, 0))
    return pl.pallas_call(
        body, name=name, grid=(L // tm,),
        in_specs=[t768, pl.BlockSpec((tm, 1024), lambda i: (i, 0)), _col_spec(tm, 128, KR), t768, t768, t128, t128],
        out_specs=[_head_spec(HEADS, tm, DQK_B), _head_spec(HEADS, tm, DQK_B), _head_spec(HEADS, tm, DH)],
        out_shape=[jax.ShapeDtypeStruct((HEADS, L, DQK_B), BF16), jax.ShapeDtypeStruct((HEADS, L, DQK_B), BF16),
                   jax.ShapeDtypeStruct((HEADS, L, DH), BF16)],
        compiler_params=_cparams(("parallel",)),
    )(qbm, kvbm, proj, *tab_q, *tab_k)


def _unprep_b(dq, dk, dv, tab_q, tab_k, dproj, *, name):
    L = dq.shape[1]
    tm = ROW_T

    def body(dq_ref, dk_ref, dv_ref, cq_ref, sq_ref, ck_ref, sk_ref, _, dp_kr, dqo, dkvo):
        dqv = _join(dq_ref, HEADS) * SCALE_B
        dqo[...] = _rope_lanes(dqv, cq_ref[...], -sq_ref[...], MLA_ROPE // 2).astype(BF16)
        parts = []
        dkr = None
        for h in range(HEADS):
            dkh = dk_ref[h]
            parts += [dkh[:, :MLA_NOPE], dv_ref[h]]
            r = dkh[:, MLA_NOPE:]
            dkr = r if dkr is None else dkr + r
        dkvo[...] = jnp.concatenate(parts, axis=1).astype(BF16)
        dkr = jnp.concatenate([dkr, jnp.zeros((tm, LANES - MLA_ROPE), F32)], axis=1)
        dp_kr[...] = _rope_lanes(dkr, ck_ref[...], -sk_ref[...], MLA_ROPE // 2).astype(BF16)

    t128 = pl.BlockSpec((tm, LANES), lambda i: (i, 0))
    t768 = pl.BlockSpec((tm, 768), lambda i: (i, 0))
    hq = _head_spec(HEADS, tm, DQK_B)
    return pl.pallas_call(
        body, name=name, grid=(L // tm,),
        in_specs=[hq, hq, _head_spec(HEADS, tm, DH), t768, t768, t128, t128, pl.BlockSpec(memory_space=pl.ANY)],
        out_specs=[_col_spec(tm, 128, KR), t768, pl.BlockSpec((tm, 1024), lambda i: (i, 0))],
        out_shape=[jax.ShapeDtypeStruct(dproj.shape, dproj.dtype), jax.ShapeDtypeStruct((L, 768), BF16),
                   jax.ShapeDtypeStruct((L, 1024), BF16)],
        input_output_aliases={7: 0},
        compiler_params=_cparams(("parallel",)),
    )(dq, dk, dv, *tab_q, *tab_k, dproj)


def _gate(y, proj, zcol, *, name):
    L = proj.shape[0]
    tm = ROW_T

    def body(y_ref, z_ref, u_ref):
        z = z_ref[...]
        u_ref[...] = (_join(y_ref, HEADS) * (z * jax.nn.sigmoid(z))).astype(BF16)

    return pl.pallas_call(
        body, name=name, grid=(L // tm,),
        in_specs=[_head_spec(HEADS, tm, DH), _col_spec(tm, 512, zcol)],
        out_specs=pl.BlockSpec((tm, 512), lambda i: (i, 0)),
        out_shape=jax.ShapeDtypeStruct((L, 512), BF16),
        compiler_params=_cparams(("parallel",)),
    )(y, proj)


def _gate_bwd(du, y, proj, zcol, dproj, *, name):
    L = proj.shape[0]
    tm = ROW_T

    def body(du_ref, y_ref, z_ref, _, dz_ref, dy_ref):
        z = z_ref[...]
        duv = du_ref[...]
        sg = jax.nn.sigmoid(z)
        _split(duv * (z * sg), HEADS, DH, dy_ref)
        dz_ref[...] = (duv * _join(y_ref, HEADS) * (sg * (1.0 + z * (1.0 - sg)))).astype(BF16)

    hs = _head_spec(HEADS, tm, DH)
    return pl.pallas_call(
        body, name=name, grid=(L // tm,),
        in_specs=[pl.BlockSpec((tm, 512), lambda i: (i, 0)), hs, _col_spec(tm, 512, zcol),
                  pl.BlockSpec(memory_space=pl.ANY)],
        out_specs=[_col_spec(tm, 512, zcol), hs],
        out_shape=[jax.ShapeDtypeStruct(dproj.shape, dproj.dtype), jax.ShapeDtypeStruct((HEADS, L, DH), F32)],
        input_output_aliases={3: 0},
        compiler_params=_cparams(("parallel",)),
    )(du, y, proj, dproj)


MERGE_T = 192


def _merge(proj, pbs, *, name):
    L = proj.shape[0]
    tm = MERGE_T

    def body(g0, g1, g2, p0, p1, p2, o_ref):
        acc = None
        for g_ref, p_ref in ((g0, p0), (g1, p1), (g2, p2)):
            t = jax.nn.sigmoid(g_ref[...]) * p_ref[...]
            acc = t if acc is None else acc + t
        o_ref[...] = acc.astype(BF16)

    row = pl.BlockSpec((tm, D_MODEL), lambda i: (i, 0))
    return pl.pallas_call(
        body, name=name, grid=(L // tm,),
        in_specs=[_col_spec(tm, D_MODEL, GATES + n * D_MODEL) for n in range(N_BRANCH)] + [row] * N_BRANCH,
        out_specs=row, out_shape=jax.ShapeDtypeStruct((L, D_MODEL), BF16),
        compiler_params=_cparams(("parallel",)),
    )(proj, proj, proj, *pbs)


def _merge_bwd(dmerged, proj, pbs, dproj, *, name):
    L = proj.shape[0]
    tm = MERGE_T

    def body(dm_ref, g_ref, p0, p1, p2, _, dg_ref, dp0, dp1, dp2):
        dm = dm_ref[...]
        for n, (p_ref, dp_ref) in enumerate(((p0, dp0), (p1, dp1), (p2, dp2))):
            cols = slice(n * D_MODEL, (n + 1) * D_MODEL)
            sg = jax.nn.sigmoid(g_ref[:, cols])
            dp_ref[...] = (dm * sg).astype(BF16)
            dg_ref[:, cols] = (dm * p_ref[...] * (sg * (1.0 - sg))).astype(BF16)

    row = pl.BlockSpec((tm, D_MODEL), lambda i: (i, 0))
    gates = _col_spec(tm, N_BRANCH * D_MODEL, GATES)
    outs = pl.pallas_call(
        body, name=name, grid=(L // tm,),
        in_specs=[row, gates] + [row] * N_BRANCH + [pl.BlockSpec(memory_space=pl.ANY)],
        out_specs=[gates] + [row] * N_BRANCH,
        out_shape=[jax.ShapeDtypeStruct(dproj.shape, dproj.dtype)]
        + [jax.ShapeDtypeStruct((L, D_MODEL), BF16)] * N_BRANCH,
        input_output_aliases={5: 0},
        compiler_params=_cparams(("parallel",)),
    )(dmerged, proj, *pbs, dproj)
    return outs[0], outs[1:]


def _forget_bias(af, b_f):
    return jnp.cumsum(jax.nn.log_sigmoid(af + b_f), axis=0).T


def _key_bias(L, T, ct=None):
    padb = jnp.where(jnp.arange(L) < PAD, BIG, 0.0).astype(F32)[None]
    kb = padb if ct is None else ct + padb
    return kb.reshape(kb.shape[0], L // T, 1, T)


def _layer_fwd(h, w, tabs, l):
    tag = f"l{l}"
    L = h.shape[0]
    hn = _rms_fwd(h, w["norm_g"], name=f"{tag}_rms_in")
    proj = _mm(hn, w["w_in"], name=f"{tag}_mm_in")
    ct, vjp_f = jax.vjp(_forget_bias, proj[:, FA:FA + HEADS], w["b_f"])
    ops_a = (*_prep_a(proj, name=f"{tag}_prep_a"), _key_bias(L, ATT_T, ct))
    ya, lsea = _attn_fwd(*ops_a, T=ATT_T, name=f"{tag}_attn_a")
    cqn = _rms_fwd(proj, w["g_cq"], col=CQ, name=f"{tag}_rms_cq")
    ckvn = _rms_fwd(proj, w["g_ckv"], col=CKV, name=f"{tag}_rms_ckv")
    qbm = _mm(cqn, w["w_uq"], name=f"{tag}_mm_uq")
    kvbm = _mm(ckvn, w["w_ukv"], name=f"{tag}_mm_ukv")
    ops_b = (*_prep_b(qbm, kvbm, proj, tabs["bq"], tabs["bk"], name=f"{tag}_prep_b"), _key_bias(L, ATT_T))
    yb, lseb = _attn_fwd(*ops_b, T=ATT_T, name=f"{tag}_attn_b")
    ops_c = (*_prep_c(proj, tabs["c"], name=f"{tag}_prep_c"), _key_bias(L, WINDOW))
    sink = jnp.broadcast_to(w["sinks"][:, None, None], (HEADS, 1, LANES))
    yc, lsec = _attn_fwd(*ops_c, T=WINDOW, window=True, sink=sink, name=f"{tag}_attn_c")
    us = [_gate(y, proj, zcol, name=f"{tag}_gate{n}") for n, (y, zcol) in enumerate(((ya, ZA), (yb, ZB), (yc, ZC)))]
    pbr = [_mm(us[n], w["w_branch"][n], name=f"{tag}_mm_br{n}") for n in range(N_BRANCH)]
    merged = _merge(proj, pbr, name=f"{tag}_merge")
    out = _mm(merged, w["w_out"], add=h, name=f"{tag}_mm_out")
    saved = dict(h=h, hn=hn, proj=proj, vjp_f=vjp_f, ops_a=ops_a, ya=ya, lsea=lsea, cqn=cqn, ckvn=ckvn,
                 ops_b=ops_b, yb=yb, lseb=lseb, ops_c=ops_c, yc=yc, lsec=lsec, us=us, pbr=pbr, merged=merged)
    return out, saved


def _layer_bwd(dout, s, w, tabs, l):
    tag = f"l{l}"
    L = dout.shape[0]
    proj = s["proj"]
    g = {}
    dproj = jnp.zeros((L, NP_IN), BF16)
    dmerged = _mm(dout, w["w_out"], tb=True, name=f"{tag}_mm_out_dx")
    g["w_out"] = _mm(s["merged"], dout, ta=True, out_dtype=BF16, name=f"{tag}_mm_out_dw")
    dproj, dpbr = _merge_bwd(dmerged, proj, s["pbr"], dproj, name=f"{tag}_merge_bwd")
    dus = [_mm(dpbr[n], w["w_branch"][n], tb=True, name=f"{tag}_mm_br{n}_dx") for n in range(N_BRANCH)]
    g["w_branch"] = jnp.stack([_mm(s["us"][n], dpbr[n], ta=True, out_dtype=BF16, name=f"{tag}_mm_br{n}_dw")
                               for n in range(N_BRANCH)])
    dys = []
    for n, (y, zcol) in enumerate(((s["ya"], ZA), (s["yb"], ZB), (s["yc"], ZC))):
        dproj, dy = _gate_bwd(dus[n], y, proj, zcol, dproj, name=f"{tag}_gate{n}_bwd")
        dys.append(dy)
    dya, dyb, dyc = dys
    dqa, dka, dva, dcq, dck = _attn_bwd(*s["ops_a"], s["ya"], s["lsea"], dya, T=ATT_T, fox=True,
                                        name=f"{tag}_attn_a_bwd")
    dproj = _unprep_a(dqa, dka, dva, dproj, name=f"{tag}_unprep_a")
    daf, g["b_f"] = s["vjp_f"](dcq[:, :, 0] + dck.reshape(HEADS, L))
    dproj = lax.dynamic_update_slice(dproj, daf.astype(BF16), (0, FA))
    dqb, dkb, dvb, _, _ = _attn_bwd(*s["ops_b"], s["yb"], s["lseb"], dyb, T=ATT_T, name=f"{tag}_attn_b_bwd")
    dproj, dqbm, dkvbm = _unprep_b(dqb, dkb, dvb, tabs["bq"], tabs["bk"], dproj, name=f"{tag}_unprep_b")
    dcqn = _mm(dqbm, w["w_uq"], tb=True, name=f"{tag}_mm_uq_dx")
    g["w_uq"] = _mm(s["cqn"], dqbm, ta=True, out_dtype=BF16, name=f"{tag}_mm_uq_dw")
    dckvn = _mm(dkvbm, w["w_ukv"], tb=True, name=f"{tag}_mm_ukv_dx")
    g["w_ukv"] = _mm(s["ckvn"], dkvbm, ta=True, out_dtype=BF16, name=f"{tag}_mm_ukv_dw")
    dproj, g["g_cq"] = _rms_bwd(dcqn, proj, w["g_cq"], col=CQ, into=dproj, name=f"{tag}_rms_cq_bwd")
    dproj, g["g_ckv"] = _rms_bwd(dckvn, proj, w["g_ckv"], col=CKV, into=dproj, name=f"{tag}_rms_ckv_bwd")
    dqc, dkc, dvc, _, _ = _attn_bwd(*s["ops_c"], s["yc"], s["lsec"], dyc, T=WINDOW, window=True,
                                    name=f"{tag}_attn_c_bwd")
    dproj = _unprep_c(dqc, dkc, dvc, tabs["c"], dproj, name=f"{tag}_unprep_c")
    delta_c = jnp.sum(dyc * s["yc"], axis=-1)
    g["sinks"] = -jnp.sum(jnp.exp(w["sinks"][:, None] - s["lsec"][:, :, 0]) * delta_c, axis=1)
    dhn = _mm(dproj, w["w_in"], tb=True, name=f"{tag}_mm_in_dx")
    g["w_in"] = _mm(s["hn"], dproj, ta=True, out_dtype=BF16, name=f"{tag}_mm_in_dw")
    dh, g["norm_g"] = _rms_bwd(dhn, s["h"], w["norm_g"], add=dout, name=f"{tag}_rms_in_bwd")
    return dh, g


def _cols_from_shards(g):
    return jnp.moveaxis(g, 0, 1).reshape(g.shape[1], N_DEV * g.shape[2])


def _cols_to_shards(w):
    R = w.shape[0]
    return jnp.moveaxis(w.reshape(R, N_DEV, w.shape[1] // N_DEV), 1, 0)


def _pad_in(w):
    parts, at = [], 0
    for lo, hi, dst in sorted(_RUNS, key=lambda r: r[2]):
        if dst > at:
            parts.append(jnp.zeros((w.shape[0], dst - at), w.dtype))
        parts.append(w[:, lo:hi])
        at = dst + hi - lo
    parts.append(jnp.zeros((w.shape[0], NP_IN - at), w.dtype))
    return jnp.concatenate(parts, axis=1)


def _unpad_in(w):
    return jnp.concatenate([w[:, dst:dst + hi - lo] for lo, hi, dst in _RUNS], axis=1)


_SMALL = (("norm_g", DEPTH * D_MODEL), ("b_f", DEPTH * HEADS), ("g_cq", DEPTH * MLA_QLORA),
          ("g_ckv", DEPTH * MLA_KVLORA), ("sinks", DEPTH * HEADS), ("final_g", D_MODEL), ("loss", 1),
          ("meta", N_META * D_MODEL))
SMALL_ROWS = 168


def _pack_small(d):
    parts = []
    for name, size in _SMALL:
        padded = -(-size // 128) * 128
        v = d[name].reshape(-1).astype(F32) if name in d else jnp.zeros((size,), F32)
        parts.append(jnp.pad(v, (0, padded - size)))
    flat = jnp.concatenate(parts)
    return jnp.pad(flat, (0, SMALL_ROWS * 128 - flat.shape[0])).reshape(SMALL_ROWS, 128)


def _unpack_small(p, shapes):
    flat = p.reshape(-1)
    out, at = {}, 0
    for name, size in _SMALL:
        if name in shapes:
            out[name] = flat[at:at + size].reshape(shapes[name])
        at += -(-size // 128) * 128
    return out


def kernel(x, meta_tokens, norm_g, w_in, b_f, g_cq, g_ckv, w_uq, w_ukv, sinks, w_branch, w_out, final_g, loss_target, m_meta_tokens, m_norm_g, m_w_in, m_b_f, m_g_cq, m_g_ckv, m_w_uq, m_w_ukv, m_sinks, m_w_branch, m_w_out, m_final_g, v_meta_tokens, v_norm_g, v_w_in, v_b_f, v_g_cq, v_g_ckv, v_w_uq, v_w_ukv, v_sinks, v_w_branch, v_w_out, v_final_g):
    S = x.shape[1]
    L = BLK + S
    cx, cy, cc = _me()
    my_idx = 4 * cx + 2 * cy + cc

    gw_in, gw_uq, gw_ukv, gw_br, gw_out, g_meta = _all_gather(
        [w_in.astype(BF16), w_uq.astype(BF16), w_ukv.astype(BF16), w_branch.astype(BF16), w_out.astype(BF16),
         meta_tokens], name="gather_weights")
    layers = []
    for l in range(DEPTH):
        layers.append(dict(
            norm_g=norm_g[l], b_f=b_f[l], g_cq=g_cq[l], g_ckv=g_ckv[l], sinks=sinks[l],
            w_in=_pad_in(_cols_from_shards(gw_in[:, l])),
            w_uq=_cols_from_shards(gw_uq[:, l]),
            w_ukv=_cols_from_shards(gw_ukv[:, l]),
            w_branch=jnp.stack([_cols_from_shards(gw_br[:, l, n]) for n in range(N_BRANCH)]),
            w_out=gw_out[:, l].reshape(D_MODEL, D_MODEL)))
    meta_full = _cols_from_shards(g_meta)

    h = jnp.concatenate([jnp.zeros((PAD, D_MODEL), F32), meta_full, x[0]], axis=0)
    q_lanes = jnp.arange(HEADS * DQK_B)
    tabs = dict(c=_rope_tables(L, DH // 2, LANES), bk=_rope_tables(L, MLA_ROPE // 2, LANES),
                bq=_rope_tables(L, MLA_ROPE // 2, HEADS * DQK_B, rope_lane=(q_lanes % DQK_B) >= MLA_NOPE))
    saved = []
    for l in range(DEPTH):
        h, s = _layer_fwd(h, layers[l], tabs, l)
        saved.append(s)
    loss_vec, dh, g_final = _loss_head(h, final_g, loss_target[0], name="loss_head")

    grads = [None] * DEPTH
    for l in reversed(range(DEPTH)):
        dh, grads[l] = _layer_bwd(dh, saved[l], layers[l], tabs, l)

    def stack(name):
        return jnp.stack([grads[l][name] for l in range(DEPTH)])

    s_in = jnp.stack([_cols_to_shards(_unpad_in(grads[l]["w_in"])) for l in range(DEPTH)], axis=1)
    s_uq = jnp.stack([_cols_to_shards(grads[l]["w_uq"]) for l in range(DEPTH)], axis=1)
    s_ukv = jnp.stack([_cols_to_shards(grads[l]["w_ukv"]) for l in range(DEPTH)], axis=1)
    s_br = jnp.stack([jnp.stack([_cols_to_shards(grads[l]["w_branch"][n]) for n in range(N_BRANCH)], axis=1)
                      for l in range(DEPTH)], axis=1)
    s_out = jnp.stack([grads[l]["w_out"].reshape(N_DEV, D_MODEL // N_DEV, D_MODEL) for l in range(DEPTH)], axis=1)
    r_in, r_uq, r_ukv, r_br, r_out = _exchange([t.astype(BF16) for t in (s_in, s_uq, s_ukv, s_br, s_out)],
                                               name="exchange_grads")

    small = _pack_small(dict(norm_g=stack("norm_g"), b_f=stack("b_f"), g_cq=stack("g_cq"), g_ckv=stack("g_ckv"),
                             sinks=stack("sinks"), final_g=g_final, loss=loss_vec[0, 0:1],
                             meta=dh[PAD:BLK]))
    (g_small,) = _all_gather([small], name="gather_small")

    def adam_big(w_, parts, m_, v_, name):
        shape = w_.shape
        C = shape[-1]
        R = math.prod(shape[:-1])
        outs = _adamw(w_.reshape(R, C), parts.reshape(parts.shape[0], R, C), m_.reshape(R, C), v_.reshape(R, C),
                      name=name)
        return [o.reshape(shape) for o in outs]

    res = {}
    res["w_in"] = adam_big(w_in, r_in, m_w_in, v_w_in, "adam_w_in")
    res["w_uq"] = adam_big(w_uq, r_uq, m_w_uq, v_w_uq, "adam_w_uq")
    res["w_ukv"] = adam_big(w_ukv, r_ukv, m_w_ukv, v_w_ukv, "adam_w_ukv")
    res["w_branch"] = adam_big(w_branch, r_br, m_w_branch, v_w_branch, "adam_w_branch")
    res["w_out"] = adam_big(w_out, r_out, m_w_out, v_w_out, "adam_w_out")

    small_w = dict(norm_g=norm_g, b_f=b_f, g_cq=g_cq, g_ckv=g_ckv, sinks=sinks, final_g=final_g)
    small_m = dict(norm_g=m_norm_g, b_f=m_b_f, g_cq=m_g_cq, g_ckv=m_g_ckv, sinks=m_sinks, final_g=m_final_g)
    small_v = dict(norm_g=v_norm_g, b_f=v_b_f, g_cq=v_g_cq, g_ckv=v_g_ckv, sinks=v_sinks, final_g=v_final_g)
    sm = _adamw(_pack_small(small_w), g_small, _pack_small(small_m), _pack_small(small_v), name="adam_small")
    shapes = {k: a.shape for k, a in small_w.items()}
    shapes_all = dict(shapes, loss=(), meta=(N_META, D_MODEL))
    sm_g = _unpack_small(sm[0], shapes_all)
    sm_d, sm_m, sm_v = (_unpack_small(t, shapes) for t in sm[1:])
    for k in shapes:
        res[k] = [sm_g[k], sm_d[k], sm_m[k], sm_v[k]]
    g_meta_mine = lax.dynamic_slice(sm_g["meta"], (0, my_idx * 128), (N_META, 128))
    res["meta_tokens"] = _adamw(meta_tokens, g_meta_mine[None], m_meta_tokens, v_meta_tokens, name="adam_meta")

    order = ["meta_tokens", "norm_g", "w_in", "b_f", "g_cq", "g_ckv", "w_uq", "w_ukv", "sinks", "w_branch", "w_out",
             "final_g"]
    grad_x = dh[BLK:][None]
    return (sm_g["loss"], grad_x, *[res[k][0] for k in order], *[res[k][1] for k in order],
            *[res[k][2] for k in order], *[res[k][3] for k in order])
```

```python
import functools
import math

import jax
import jax.numpy as jnp
from jax import lax
from jax.experimental import pallas as pl
from jax.experimental.pallas import tpu as pltpu

F32 = jnp.float32
BF16 = jnp.bfloat16

D_MODEL = 1024
DEPTH = 2
N_META = 16
BLK = 128
PAD = BLK - N_META
ROPE_THETA = 10000.0
EPS = 1e-6
NEG = -1e30
BIG = 1e30
HEADS = 8
DH = 64
MLA_NOPE = 64
MLA_ROPE = 32
MLA_QLORA = 384
MLA_KVLORA = 256
SWA_KV_HEADS = 2
WINDOW = 128
BRANCH_W = 512
N_BRANCH = 3
N_IN = 7592

ADAM_LR = 0.001
ADAM_B1 = 0.9
ADAM_B2 = 0.999
ADAM_EPS = 1e-08
ADAM_WD = 0.01
ADAM_STEP = 10

N_DEV = 8
MESH = pl.DeviceIdType.MESH

NP_IN = 8192
QKVA, ZA, FA = 0, 1536, 2048
CKV, KR, CQ = 2304, 2560, 2688
GATES = 3072
ZB = 6144
QKVC, ZC = 6912, 7680
_RUNS = ((0, 1536, QKVA), (1536, 1544, FA), (1544, 2056, ZA), (2056, 2440, CQ), (2440, 2696, CKV), (2696, 2728, KR),
         (2728, 3240, ZB), (3240, 4008, QKVC), (4008, 4520, ZC), (4520, 7592, GATES))

VMEM_LIMIT = 48 * 1024 * 1024
ATT_T = 384
ATT_RB = 128
ROW_T = 384
LANES = 128


def _pick(dim, prefs):
    for p in prefs:
        if dim % p == 0:
            return p
    return dim


def _cparams(sem):
    return pltpu.CompilerParams(dimension_semantics=sem, vmem_limit_bytes=VMEM_LIMIT)


def _mm(a, b, *, ta=False, tb=False, add=None, out_dtype=F32, name):
    M = a.shape[1] if ta else a.shape[0]
    K = a.shape[0] if ta else a.shape[1]
    N = b.shape[0] if tb else b.shape[1]
    assert K == (b.shape[1] if tb else b.shape[0])
    tm = _pick(M, (704, 1024, 512, 384, 256, 128))
    tn = _pick(N, (1024, 768, 512, 384, 256, 128))
    tk = _pick(K, (2048, 1408, 1024, 768, 512, 384, 256, 128))
    nk = K // tk
    dims = (((0 if ta else 1,), (1 if tb else 0,)), ((), ()))

    def body(*refs):
        a_ref, b_ref = refs[:2]
        c_ref = refs[2] if add is not None else None
        o_ref = refs[3] if add is not None else refs[2]
        r = lax.dot_general(a_ref[...].astype(BF16), b_ref[...].astype(BF16), dims, preferred_element_type=F32)

        def finish(total):
            if c_ref is not None:
                total = total + c_ref[...]
            o_ref[...] = total.astype(out_dtype)

        if nk == 1:
            finish(r)
        else:
            acc = refs[-1]
            k = pl.program_id(2)

            @pl.when(k == 0)
            def _():
                acc[...] = r

            @pl.when(k > 0)
            def _():
                acc[...] += r

            @pl.when(k == nk - 1)
            def _():
                finish(acc[...])

    a_spec = pl.BlockSpec((tk, tm), lambda i, j, k: (k, i)) if ta else pl.BlockSpec((tm, tk), lambda i, j, k: (i, k))
    b_spec = pl.BlockSpec((tn, tk), lambda i, j, k: (j, k)) if tb else pl.BlockSpec((tk, tn), lambda i, j, k: (k, j))
    o_spec = pl.BlockSpec((tm, tn), lambda i, j, k: (i, j))
    return pl.pallas_call(
        body, name=name,
        grid=(M // tm, N // tn, nk),
        in_specs=[a_spec, b_spec] + ([o_spec] if add is not None else []),
        out_specs=o_spec,
        out_shape=jax.ShapeDtypeStruct((M, N), out_dtype),
        scratch_shapes=[pltpu.VMEM((tm, tn), F32)] if nk > 1 else [],
        compiler_params=_cparams(("parallel", "parallel", "arbitrary")),
    )(*((a, b) if add is None else (a, b, add)))


def _col_spec(tm, width, col):
    assert col % width == 0
    return pl.BlockSpec((tm, width), lambda i, _c=col // width: (i, _c))


def _rms_fwd(x, g, *, col=0, name):
    L = x.shape[0]
    D = g.shape[0]
    tm = ROW_T

    def body(x_ref, g_ref, y_ref):
        xv = x_ref[...]
        rstd = lax.rsqrt(jnp.mean(xv * xv, axis=-1, keepdims=True) + EPS)
        y_ref[...] = (xv * rstd * g_ref[...]).astype(BF16)

    return pl.pallas_call(
        body, name=name, grid=(L // tm,),
        in_specs=[_col_spec(tm, D, col), pl.BlockSpec((1, D), lambda i: (0, 0))],
        out_specs=pl.BlockSpec((tm, D), lambda i: (i, 0)),
        out_shape=jax.ShapeDtypeStruct((L, D), BF16),
        compiler_params=_cparams(("parallel",)),
    )(x, g.reshape(1, D))


def _rms_bwd(dy, x, g, *, col=0, add=None, into=None, name):
    L = x.shape[0]
    D = g.shape[0]
    tm = ROW_T

    def body(*refs):
        dy_ref, x_ref, g_ref = refs[:3]
        add_ref = refs[3] if add is not None else None
        dx_ref, dg_ref = refs[-2:]
        i = pl.program_id(0)
        xv = x_ref[...]
        dyv = dy_ref[...]
        rstd = lax.rsqrt(jnp.mean(xv * xv, axis=-1, keepdims=True) + EPS)
        xhat = xv * rstd
        part = jnp.sum(dyv * xhat, axis=0, keepdims=True)

        @pl.when(i == 0)
        def _():
            dg_ref[...] = part

        @pl.when(i > 0)
        def _():
            dg_ref[...] += part

        dxh = dyv * g_ref[...]
        dx = rstd * (dxh - xhat * jnp.mean(dxh * xhat, axis=-1, keepdims=True))
        if add_ref is not None:
            dx = dx + add_ref[...]
        dx_ref[...] = dx.astype(dx_ref.dtype)

    row = pl.BlockSpec((tm, D), lambda i: (i, 0))
    in_specs = [row, _col_spec(tm, D, col), pl.BlockSpec((1, D), lambda i: (0, 0))]
    args = [dy, x, g.reshape(1, D)]
    aliases = {}
    if add is not None:
        in_specs.append(row)
        args.append(add)
    if into is not None:
        in_specs.append(pl.BlockSpec(memory_space=pl.ANY))
        args.append(into)
        aliases = {len(args) - 1: 0}
        dx_spec, dx_shape = _col_spec(tm, D, col), jax.ShapeDtypeStruct(into.shape, into.dtype)
    else:
        dx_spec, dx_shape = row, jax.ShapeDtypeStruct((L, D), F32)
    dx, dg = pl.pallas_call(
        body, name=name, grid=(L // tm,),
        in_specs=in_specs,
        out_specs=[dx_spec, pl.BlockSpec((1, D), lambda i: (0, 0))],
        out_shape=[dx_shape, jax.ShapeDtypeStruct((1, D), F32)],
        input_output_aliases=aliases,
        compiler_params=_cparams(("arbitrary",)),
    )(*args)
    return dx, dg.reshape(D)


def _loss_head(h, g, target, *, name):
    L, D = h.shape
    nb = L // BLK

    def body(h_ref, g_ref, t_ref, loss_ref, dh_ref, dg_ref):
        i = pl.program_id(0)

        @pl.when(i == 0)
        def _():
            loss_ref[...] = jnp.zeros_like(loss_ref)
            dg_ref[...] = jnp.zeros_like(dg_ref)
            dh_ref[...] = jnp.zeros_like(dh_ref)

        @pl.when(i > 0)
        def _():
            xv = h_ref[...]
            gv = g_ref[...]
            rstd = lax.rsqrt(jnp.mean(xv * xv, axis=-1, keepdims=True) + EPS)
            xhat = xv * rstd
            err = xhat * gv - t_ref[...]
            row = jnp.mean(err * err, axis=-1, keepdims=True)
            loss_ref[...] += 0.5 * jnp.sum(row, axis=0, keepdims=True)
            dy = err * (1.0 / D)
            dg_ref[...] += jnp.sum(dy * xhat, axis=0, keepdims=True)
            dxh = dy * gv
            dh_ref[...] = rstd * (dxh - xhat * jnp.mean(dxh * xhat, axis=-1, keepdims=True))

    loss, dh, dg = pl.pallas_call(
        body, name=name, grid=(nb,),
        in_specs=[pl.BlockSpec((BLK, D), lambda i: (i, 0)), pl.BlockSpec((1, D), lambda i: (0, 0)),
                  pl.BlockSpec((BLK, D), lambda i: (jnp.maximum(i - 1, 0), 0))],
        out_specs=[pl.BlockSpec((1, 128), lambda i: (0, 0)), pl.BlockSpec((BLK, D), lambda i: (i, 0)),
                   pl.BlockSpec((1, D), lambda i: (0, 0))],
        out_shape=[jax.ShapeDtypeStruct((1, 128), F32), jax.ShapeDtypeStruct((L, D), F32),
                   jax.ShapeDtypeStruct((1, D), F32)],
        compiler_params=_cparams(("arbitrary",)),
    )(h, g.reshape(1, D), target)
    return loss, dh, dg.reshape(D)


_NT = (((1,), (1,)), ((), ()))
_TN = (((0,), (0,)), ((), ()))


def _attn_fwd(q, k, v, kbias, *, T, window=False, sink=None, name):
    H, L, dk = q.shape
    Hkv = k.shape[0]
    dv = v.shape[2]
    G = H // Hkv
    nt = L // T
    Hb = kbias.shape[0]
    reps = T // LANES
    assert not window or T == WINDOW
    HP = G if G > 1 else 2
    NS = 1 if G > 1 else HP
    R = HP * T // NS
    HKV = HP // G
    HB = HP if Hb > 1 else 1
    assert G == 1 or Hb == 1

    def body(*refs):
        q_ref, k_ref, v_ref, kb_ref = refs[:4]
        n = 4
        if sink is not None:
            sk_ref = refs[n]
            n += 1
        o_ref, lse_ref = refs[n:n + 2]
        m_scs, l_scs, acc_scs = (refs[n + 2 + t * NS:n + 2 + (t + 1) * NS] for t in range(3))
        i = pl.program_id(1)
        for a in range(NS):
            if sink is not None:
                sk = [jnp.broadcast_to(sk_ref[b, :, 0:1], (T, LANES)) for b in range(HP)]
                m_scs[a][...] = jnp.concatenate(sk, axis=0) if G > 1 else sk[a]
                l_scs[a][...] = jnp.ones((R, LANES), F32)
            else:
                m_scs[a][...] = jnp.full((R, LANES), NEG, F32)
                l_scs[a][...] = jnp.zeros((R, LANES), F32)
            acc_scs[a][...] = jnp.zeros((R, dv), F32)
        row = lax.broadcasted_iota(jnp.int32, (R, T), 0) & (T - 1) if G > 1 else \
            lax.broadcasted_iota(jnp.int32, (R, T), 0)
        col = lax.broadcasted_iota(jnp.int32, (R, T), 1)

        def tile(j, kind):
            rows = pl.ds(pl.multiple_of(j * T, T), T)
            for a in range(NS):
                m_sc, l_sc, acc_sc = m_scs[a], l_scs[a], acc_scs[a]
                qv = q_ref[...].reshape(R, dk) if G > 1 else q_ref[a]
                s = lax.dot_general(qv, k_ref[a, rows, :], _NT, preferred_element_type=F32)
                s = s - kb_ref[a if HB > 1 else 0, j]
                if kind == "diag":
                    s = jnp.where(row >= col, s, NEG)
                elif kind == "prev":
                    s = jnp.where(col > row, s, NEG)
                m_prev = m_sc[...]
                m_new = jnp.maximum(m_prev, jnp.max(s, axis=1, keepdims=True))
                alpha = jnp.exp(m_prev - m_new)
                p = jnp.exp(s - jnp.tile(m_new, (1, reps)))
                l_sc[...] = alpha * l_sc[...] + jnp.sum(p, axis=1, keepdims=True)
                acc_sc[...] = alpha[:, :dv] * acc_sc[...] + jnp.dot(p.astype(BF16), v_ref[a, rows, :],
                                                                    preferred_element_type=F32)
                m_sc[...] = m_new

        if window:
            @pl.when(i > 0)
            def _():
                tile(i - 1, "prev")

            tile(i, "diag")
        else:
            tile(i, "diag")

            def full(j, c):
                tile(j, "full")
                return c

            lax.fori_loop(0, i, full, 0)

        for a in range(NS):
            lv = l_scs[a][...]
            ov = acc_scs[a][...] / lv[:, :dv]
            lsev = (m_scs[a][...] + jnp.log(lv))[:, 0:1]
            if G > 1:
                o_ref[...] = ov.reshape(HP, T, dv)
                lse_ref[...] = lsev.reshape(HP, T, 1)
            else:
                o_ref[a] = ov
                lse_ref[a] = lsev

    in_specs = [pl.BlockSpec((HP, T, dk), lambda h, i: (h, i, 0)),
                pl.BlockSpec((HKV, L, dk), lambda h, i: (h, 0, 0)),
                pl.BlockSpec((HKV, L, dv), lambda h, i: (h, 0, 0)),
                pl.BlockSpec((HB, nt, 1, T), lambda h, i: (h if Hb > 1 else 0, 0, 0, 0))]
    args = [q, k, v, kbias]
    if sink is not None:
        in_specs += [pl.BlockSpec((HP, 1, LANES), lambda h, i: (h, 0, 0))]
        args += [sink]
    o, lse = pl.pallas_call(
        body, name=name, grid=(H // HP, nt),
        in_specs=in_specs,
        out_specs=[pl.BlockSpec((HP, T, dv), lambda h, i: (h, i, 0)),
                   pl.BlockSpec((HP, T, 1), lambda h, i: (h, i, 0))],
        out_shape=[jax.ShapeDtypeStruct((H, L, dv), F32), jax.ShapeDtypeStruct((H, L, 1), F32)],
        scratch_shapes=[pltpu.VMEM((R, LANES), F32)] * (2 * NS) + [pltpu.VMEM((R, dv), F32)] * NS,
        compiler_params=_cparams(("parallel", "arbitrary")),
    )(*args)
    return o, lse


def _attn_bwd(q, k, v, kbias, o, lse, do, *, T, window=False, fox=False, name):
    H, L, dk = q.shape
    Hkv = k.shape[0]
    dv = v.shape[2]
    G = H // Hkv
    nt = L // T
    Hb = kbias.shape[0]
    assert not window or T == WINDOW

    def body(*refs):
        q_ref, k_ref, v_ref, kb_ref, o_ref, lse_ref, do_ref = refs[:7]
        dq_ref, dk_ref, dv_ref = refs[7:10]
        n = 10
        if fox:
            dcq_ref, dck_ref = refs[n:n + 2]
            n += 2
        delta_sc = refs[n]
        j = pl.program_id(1)

        @pl.when(j == 0)
        def _():
            dq_ref[...] = jnp.zeros_like(dq_ref)
            if fox:
                dcq_ref[...] = jnp.zeros_like(dcq_ref)

            def dl(i, c):
                rows = pl.ds(pl.multiple_of(i * T, T), T)
                delta_sc[rows, :] = jnp.sum(do_ref[0, rows, :] * o_ref[0, rows, :], axis=1, keepdims=True)
                return c

            lax.fori_loop(0, nt, dl, 0)

        dk_ref[...] = jnp.zeros_like(dk_ref)
        dv_ref[...] = jnp.zeros_like(dv_ref)
        if fox:
            dck_ref[...] = jnp.zeros_like(dck_ref)
        kb = k_ref[0]
        vb = v_ref[0]
        kbias_j = kb_ref[0, j]
        row = lax.broadcasted_iota(jnp.int32, (T, T), 0)
        col = lax.broadcasted_iota(jnp.int32, (T, T), 1)

        def tile(i, kind):
            rows = pl.ds(pl.multiple_of(i * T, T), T)
            qb = q_ref[0, rows, :]
            dob = do_ref[0, rows, :].astype(BF16)
            s = lax.dot_general(qb, kb, _NT, preferred_element_type=F32) - kbias_j
            if kind == "diag":
                s = jnp.where(row >= col, s, NEG)
            elif kind == "prev":
                s = jnp.where(col > row, s, NEG)
            p = jnp.exp(s - lse_ref[0, rows, :])
            dp = lax.dot_general(dob, vb, _NT, preferred_element_type=F32)
            ds = p * (dp - delta_sc[rows, :])
            dsb = ds.astype(BF16)
            dv_ref[0] += lax.dot_general(p.astype(BF16), dob, _TN, preferred_element_type=F32)
            dk_ref[0] += lax.dot_general(dsb, qb, _TN, preferred_element_type=F32)
            dq_ref[0, rows, :] += jnp.dot(dsb, kb, preferred_element_type=F32)
            if fox:
                dcq_ref[0, rows, :] += jnp.sum(ds, axis=1, keepdims=True)
                dck_ref[0, 0] += -jnp.sum(ds, axis=0, keepdims=True)

        tile(j, "diag")
        if window:
            @pl.when(j + 1 < nt)
            def _():
                tile(j + 1, "prev")
        else:
            def full(i, c):
                tile(i, "full")
                return c

            lax.fori_loop(j + 1, nt, full, 0)

    in_specs = [pl.BlockSpec((1, L, dk), lambda h, j: (h, 0, 0)),
                pl.BlockSpec((1, T, dk), lambda h, j: (h // G, j, 0)),
                pl.BlockSpec((1, T, dv), lambda h, j: (h // G, j, 0)),
                pl.BlockSpec((1, nt, 1, T), lambda h, j: (h if Hb > 1 else 0, 0, 0, 0)),
                pl.BlockSpec((1, L, dv), lambda h, j: (h, 0, 0)),
                pl.BlockSpec((1, L, 1), lambda h, j: (h, 0, 0)),
                pl.BlockSpec((1, L, dv), lambda h, j: (h, 0, 0))]
    out_specs = [pl.BlockSpec((1, L, dk), lambda h, j: (h, 0, 0)),
                 pl.BlockSpec((1, T, dk), lambda h, j: (h, j, 0)),
                 pl.BlockSpec((1, T, dv), lambda h, j: (h, j, 0))]
    out_shape = [jax.ShapeDtypeStruct((H, L, dk), F32), jax.ShapeDtypeStruct((H, L, dk), F32),
                 jax.ShapeDtypeStruct((H, L, dv), F32)]
    if fox:
        out_specs += [pl.BlockSpec((1, L, 1), lambda h, j: (h, 0, 0)),
                      pl.BlockSpec((1, 1, 1, T), lambda h, j: (h, j, 0, 0))]
        out_shape += [jax.ShapeDtypeStruct((H, L, 1), F32), jax.ShapeDtypeStruct((H, nt, 1, T), F32)]
    outs = pl.pallas_call(
        body, name=name, grid=(H, nt),
        in_specs=in_specs, out_specs=out_specs, out_shape=out_shape,
        scratch_shapes=[pltpu.VMEM((L, 1), F32)],
        compiler_params=_cparams(("parallel", "arbitrary")),
    )(q, k, v, kbias, o, lse, do)
    return outs if fox else (*outs, None, None)


def _attn_bwd_window(q, k, v, kbias, o, lse, do, *, name):
    H, L, dk = q.shape
    Hkv = k.shape[0]
    dv = v.shape[2]
    G = H // Hkv
    T = WINDOW
    nt = L // T
    R = G * T

    def body(q_ref, kc_ref, kp_ref, vc_ref, vp_ref, kb_ref, o_ref, lse_ref, do_ref, dq_ref, dk_ref, dv_ref,
             dk_sc, dv_sc):
        i = pl.program_id(1)
        row = lax.broadcasted_iota(jnp.int32, (R, T), 0) & (T - 1)
        col = lax.broadcasted_iota(jnp.int32, (R, T), 1)

        @pl.when(i == 0)
        def _():
            dk_sc[...] = jnp.zeros_like(dk_sc)
            dv_sc[...] = jnp.zeros_like(dv_sc)

        @pl.when(i == nt)
        def _():
            dk_ref[0] = dk_sc[...]
            dv_ref[0] = dv_sc[...]

        @pl.when(i < nt)
        def _():
            qb = q_ref[...].reshape(R, dk)
            dof = do_ref[...].reshape(R, dv)
            dob = dof.astype(BF16)
            lse_c = lse_ref[...].reshape(R, 1)
            delta = jnp.sum(dof * o_ref[...].reshape(R, dv), axis=1, keepdims=True)

            def grads(kt, vt, kbias_j, mask):
                s = lax.dot_general(qb, kt, _NT, preferred_element_type=F32) - kbias_j
                p = jnp.exp(jnp.where(mask, s, NEG) - lse_c)
                dp = lax.dot_general(dob, vt, _NT, preferred_element_type=F32)
                ds = (p * (dp - delta)).astype(BF16)
                return (jnp.dot(ds, kt, preferred_element_type=F32),
                        lax.dot_general(ds, qb, _TN, preferred_element_type=F32),
                        lax.dot_general(p.astype(BF16), dob, _TN, preferred_element_type=F32))

            ip = jnp.maximum(i - 1, 0)
            dq_p, dk_p, dv_p = grads(kp_ref[0], vp_ref[0], kb_ref[0, ip], (col > row) & (i > 0))
            dq_c, dk_c, dv_c = grads(kc_ref[0], vc_ref[0], kb_ref[0, i], row >= col)
            dq_ref[...] = (dq_p + dq_c).reshape(G, T, dk)
            dk_ref[0] = dk_sc[...] + dk_p
            dv_ref[0] = dv_sc[...] + dv_p
            dk_sc[...] = dk_c
            dv_sc[...] = dv_c

    def cur(i):
        return jnp.minimum(i, nt - 1)

    def prev(i):
        return jnp.maximum(jnp.minimum(i, nt - 1) - 1, 0)

    def written(i):
        return jnp.maximum(i - 1, 0)

    qs = lambda d: pl.BlockSpec((G, T, d), lambda h, i: (h, cur(i), 0))
    return pl.pallas_call(
        body, name=name, grid=(Hkv, nt + 1),
        in_specs=[qs(dk),
                  pl.BlockSpec((1, T, dk), lambda h, i: (h, cur(i), 0)),
                  pl.BlockSpec((1, T, dk), lambda h, i: (h, prev(i), 0)),
                  pl.BlockSpec((1, T, dv), lambda h, i: (h, cur(i), 0)),
                  pl.BlockSpec((1, T, dv), lambda h, i: (h, prev(i), 0)),
                  pl.BlockSpec((1, nt, 1, T), lambda h, i: (0, 0, 0, 0)),
                  qs(dv), qs(1), qs(dv)],
        out_specs=[qs(dk),
                   pl.BlockSpec((1, T, dk), lambda h, i: (h, written(i), 0)),
                   pl.BlockSpec((1, T, dv), lambda h, i: (h, written(i), 0))],
        out_shape=[jax.ShapeDtypeStruct((H, L, dk), F32), jax.ShapeDtypeStruct((Hkv, L, dk), F32),
                   jax.ShapeDtypeStruct((Hkv, L, dv), F32)],
        scratch_shapes=[pltpu.VMEM((T, dk), F32), pltpu.VMEM((T, dv), F32)],
        compiler_params=_cparams(("parallel", "arbitrary")),
    )(q, k, k, v, v, kbias, o, lse, do)


def _adamw(w, gparts, m, v, *, name):
    n, R, C = gparts.shape
    tr = _pick(R, (128, 64, 32, 16, 8))
    c1 = 1.0 - ADAM_B1 ** ADAM_STEP
    c2 = 1.0 - ADAM_B2 ** ADAM_STEP

    def body(w_ref, g_ref, m_ref, v_ref, go_ref, d_ref, mo_ref, vo_ref):
        g = g_ref[0].astype(F32)
        for t in range(1, n):
            g = g + g_ref[t].astype(F32)
        mn = ADAM_B1 * m_ref[...] + (1.0 - ADAM_B1) * g
        vn = ADAM_B2 * v_ref[...] + (1.0 - ADAM_B2) * (g * g)
        go_ref[...] = g
        mo_ref[...] = mn
        vo_ref[...] = vn
        d_ref[...] = -ADAM_LR * ((mn / c1) / (jnp.sqrt(vn / c2) + ADAM_EPS) + ADAM_WD * w_ref[...])

    spec = pl.BlockSpec((tr, C), lambda i: (i, 0))
    return pl.pallas_call(
        body, name=name, grid=(R // tr,),
        in_specs=[spec, pl.BlockSpec((n, tr, C), lambda i: (0, i, 0)), spec, spec],
        out_specs=[spec] * 4,
        out_shape=[jax.ShapeDtypeStruct((R, C), F32)] * 4,
        compiler_params=_cparams(("parallel",)),
    )(w, gparts, m, v)


def _me():
    return lax.axis_index("x"), lax.axis_index("y"), lax.axis_index("c")


def _all_gather(arrs, *, name):
    n = len(arrs)

    def body(*refs):
        ins, outs = refs[:n], refs[n:2 * n]
        send_sems, recv_sems, local_sems = refs[2 * n:]
        x, y, c = _me()
        me, sibling = (x, y, c), (x, y, 1 - c)
        chips = [(1 - x, y), (x, 1 - y), (1 - x, 1 - y)]

        def idx(p):
            return 4 * p[0] + 2 * p[1] + p[2]

        def copy(t, k, block, to, src=None):
            dst = outs[t].at[idx(block)]
            return pltpu.make_async_remote_copy(
                src_ref=dst if src is None else src, dst_ref=dst,
                send_sem=send_sems.at[t, k], recv_sem=recv_sems.at[t, k],
                device_id=to, device_id_type=MESH)

        mine = [pltpu.make_async_copy(ins[t], outs[t].at[idx(me)], local_sems.at[t]) for t in range(n)]
        for cp in mine:
            cp.start()
        first = []
        for t in range(n):
            first.append(copy(t, 0, me, sibling, src=ins[t]))
            first += [copy(t, 1 + j, me, (*chip, c), src=ins[t]) for j, chip in enumerate(chips)]
        for cp in first:
            cp.start()
        passed = [[copy(t, 4 + j, (*chip, c), sibling) for j, chip in enumerate(chips)] for t in range(n)]
        for j, chip in enumerate(chips):
            for t in range(n):
                copy(t, 1 + j, (*chip, c), me).wait_recv()
                passed[t][j].start()
        for t in range(n):
            copy(t, 0, sibling, me).wait_recv()
            for j, chip in enumerate(chips):
                copy(t, 4 + j, (*chip, 1 - c), me).wait_recv()
        for cp in first:
            cp.wait_send()
        for t in range(n):
            for cp in passed[t]:
                cp.wait_send()
        for cp in mine:
            cp.wait()

    any_spec = pl.BlockSpec(memory_space=pl.ANY)
    return pl.pallas_call(
        body, name=name,
        in_specs=[any_spec] * n, out_specs=[any_spec] * n,
        out_shape=[jax.ShapeDtypeStruct((N_DEV, *a.shape), a.dtype) for a in arrs],
        scratch_shapes=[pltpu.SemaphoreType.DMA((n, 7)), pltpu.SemaphoreType.DMA((n, 7)),
                        pltpu.SemaphoreType.DMA((n,))],
        compiler_params=pltpu.CompilerParams(has_side_effects=True),
    )(*arrs)


def _exchange(arrs, *, name):
    n = len(arrs)

    def body(*refs):
        ins, outs = refs[:n], refs[n:2 * n]
        send_sems, recv_sems, local_sems = refs[2 * n:]
        x, y, c = _me()
        my_idx = 4 * x + 2 * y + c
        mine = [pltpu.make_async_copy(ins[t].at[my_idx], outs[t].at[my_idx], local_sems.at[t]) for t in range(n)]
        for cp in mine:
            cp.start()
        copies = []
        for t in range(n):
            for k in range(1, N_DEV):
                bx, by, bc = (k >> 2) & 1, (k >> 1) & 1, k & 1
                peer = (x ^ bx, y ^ by, c ^ bc)
                peer_idx = 4 * peer[0] + 2 * peer[1] + peer[2]
                copies.append((pltpu.make_async_remote_copy(
                    src_ref=ins[t].at[peer_idx], dst_ref=outs[t].at[my_idx],
                    send_sem=send_sems.at[t, k - 1], recv_sem=recv_sems.at[t, k - 1],
                    device_id=peer, device_id_type=MESH),
                    pltpu.make_async_remote_copy(
                    src_ref=ins[t].at[peer_idx], dst_ref=outs[t].at[peer_idx],
                    send_sem=send_sems.at[t, k - 1], recv_sem=recv_sems.at[t, k - 1],
                    device_id=peer, device_id_type=MESH)))
        for snd, _ in copies:
            snd.start()
        for _, rcv in copies:
            rcv.wait_recv()
        for snd, _ in copies:
            snd.wait_send()
        for cp in mine:
            cp.wait()

    any_spec = pl.BlockSpec(memory_space=pl.ANY)
    return pl.pallas_call(
        body, name=name,
        in_specs=[any_spec] * n, out_specs=[any_spec] * n,
        out_shape=[jax.ShapeDtypeStruct(a.shape, a.dtype) for a in arrs],
        scratch_shapes=[pltpu.SemaphoreType.DMA((n, 7)), pltpu.SemaphoreType.DMA((n, 7)),
                        pltpu.SemaphoreType.DMA((n,))],
        compiler_params=pltpu.CompilerParams(has_side_effects=True),
    )(*arrs)


SCALE_A = DH ** -0.5
SCALE_B = (MLA_NOPE + MLA_ROPE) ** -0.5
SCALE_C = DH ** -0.5
DQK_B = MLA_NOPE + MLA_ROPE


def _rope_tables(L, half, width, rope_lane=None):
    pos = (jnp.arange(L) - PAD).astype(F32)
    inv = ROPE_THETA ** (-jnp.arange(half, dtype=F32) / half)
    lane = jnp.arange(width)
    ang = pos[:, None] * inv[lane % half][None, :]
    sign = jnp.where(lane % (2 * half) < half, -1.0, 1.0).astype(F32)
    cos, sin = jnp.cos(ang), jnp.sin(ang) * sign[None, :]
    if rope_lane is not None:
        cos = jnp.where(rope_lane[None, :], cos, 1.0)
        sin = jnp.where(rope_lane[None, :], sin, 0.0)
    return cos, sin


def _rope_lanes(x, cos, sin, half):
    W = x.shape[1]
    lane = lax.broadcasted_iota(jnp.int32, x.shape, 1)
    first = (lane & (2 * half - 1)) < half
    partner = jnp.where(first, pltpu.roll(x, W - half, 1), pltpu.roll(x, half, 1))
    return x * cos + partner * sin


def _tile_lanes(t, width):
    return t if t.shape[1] == width else jnp.tile(t, (1, width // t.shape[1]))


def _split(x, H, d, dst):
    for h in range(H):
        dst[h] = x[:, d * h:d * (h + 1)].astype(dst.dtype)


def _join(src, H):
    return jnp.concatenate([src[h] for h in range(H)], axis=1)


def _head_spec(H, tm, d):
    return pl.BlockSpec((H, tm, d), lambda i: (0, i, 0))


def _prep_a(proj, *, name):
    L = proj.shape[0]
    tm = ROW_T

    def body(x_ref, qo, ko, vo):
        _split(x_ref[:, 0:512] * SCALE_A, HEADS, DH, qo)
        _split(x_ref[:, 512:1024], HEADS, DH, ko)
        _split(x_ref[:, 1024:1536], HEADS, DH, vo)

    return pl.pallas_call(
        body, name=name, grid=(L // tm,),
        in_specs=[_col_spec(tm, 1536, QKVA)],
        out_specs=[_head_spec(HEADS, tm, DH)] * 3,
        out_shape=[jax.ShapeDtypeStruct((HEADS, L, DH), BF16)] * 3,
        compiler_params=_cparams(("parallel",)),
    )(proj)


def _unprep_a(dq, dk, dv, dproj, *, name):
    L = dq.shape[1]
    tm = ROW_T

    def body(dq_ref, dk_ref, dv_ref, _, dp_ref):
        dp_ref[:, 0:512] = (_join(dq_ref, HEADS) * SCALE_A).astype(BF16)
        dp_ref[:, 512:1024] = _join(dk_ref, HEADS).astype(BF16)
        dp_ref[:, 1024:1536] = _join(dv_ref, HEADS).astype(BF16)

    hs = _head_spec(HEADS, tm, DH)
    return pl.pallas_call(
        body, name=name, grid=(L // tm,),
        in_specs=[hs, hs, hs, pl.BlockSpec(memory_space=pl.ANY)],
        out_specs=_col_spec(tm, 1536, QKVA),
        out_shape=jax.ShapeDtypeStruct(dproj.shape, dproj.dtype),
        input_output_aliases={3: 0},
        compiler_params=_cparams(("parallel",)),
    )(dq, dk, dv, dproj)


def _prep_c(proj, tab, *, name):
    L = proj.shape[0]
    tm = ROW_T

    def body(x_ref, cos_ref, sin_ref, qo, ko, vo):
        cos, sin = cos_ref[...], sin_ref[...]
        q = _rope_lanes(x_ref[:, 0:512], _tile_lanes(cos, 512), _tile_lanes(sin, 512), DH // 2)
        _split(q * SCALE_C, HEADS, DH, qo)
        _split(_rope_lanes(x_ref[:, 512:640], cos, sin, DH // 2), SWA_KV_HEADS, DH, ko)
        _split(x_ref[:, 640:768], SWA_KV_HEADS, DH, vo)

    t128 = pl.BlockSpec((tm, LANES), lambda i: (i, 0))
    return pl.pallas_call(
        body, name=name, grid=(L // tm,),
        in_specs=[_col_spec(tm, 768, QKVC), t128, t128],
        out_specs=[_head_spec(HEADS, tm, DH), _head_spec(SWA_KV_HEADS, tm, DH), _head_spec(SWA_KV_HEADS, tm, DH)],
        out_shape=[jax.ShapeDtypeStruct((HEADS, L, DH), BF16), jax.ShapeDtypeStruct((SWA_KV_HEADS, L, DH), BF16),
                   jax.ShapeDtypeStruct((SWA_KV_HEADS, L, DH), BF16)],
        compiler_params=_cparams(("parallel",)),
    )(proj, *tab)


def _unprep_c(dq, dk, dv, tab, dproj, *, name):
    L = dq.shape[1]
    tm = ROW_T

    def body(dq_ref, dk_ref, dv_ref, cos_ref, sin_ref, _, dp_ref):
        cos, nsin = cos_ref[...], -sin_ref[...]
        dqv = _join(dq_ref, HEADS) * SCALE_C
        dp_ref[:, 0:512] = _rope_lanes(dqv, _tile_lanes(cos, 512), _tile_lanes(nsin, 512), DH // 2).astype(BF16)
        dp_ref[:, 512:640] = _rope_lanes(_join(dk_ref, SWA_KV_HEADS), cos, nsin, DH // 2).astype(BF16)
        dp_ref[:, 640:768] = _join(dv_ref, SWA_KV_HEADS).astype(BF16)

    hs = _head_spec(HEADS, tm, DH)
    hkv = _head_spec(SWA_KV_HEADS, tm, DH)
    t128 = pl.BlockSpec((tm, LANES), lambda i: (i, 0))
    return pl.pallas_call(
        body, name=name, grid=(L // tm,),
        in_specs=[hs, hkv, hkv, t128, t128, pl.BlockSpec(memory_space=pl.ANY)],
        out_specs=_col_spec(tm, 768, QKVC),
        out_shape=jax.ShapeDtypeStruct(dproj.shape, dproj.dtype),
        input_output_aliases={5: 0},
        compiler_params=_cparams(("parallel",)),
    )(dq, dk, dv, *tab, dproj)


def _prep_b(qbm, kvbm, proj, tab_q, tab_k, *, name):
    L = proj.shape[0]
    tm = ROW_T

    def body(q_ref, kv_ref, kr_ref, cq_ref, sq_ref, ck_ref, sk_ref, qo, ko, vo):
        q = _rope_lanes(q_ref[...], cq_ref[...], sq_ref[...], MLA_ROPE // 2) * SCALE_B
        _split(q, HEADS, DQK_B, qo)
        kr = _rope_lanes(kr_ref[...], ck_ref[...], sk_ref[...], MLA_ROPE // 2)[:, :MLA_ROPE].astype(BF16)
        kv = kv_ref[...]
        for h in range(HEADS):
            ko[h] = jnp.concatenate([kv[:, 128 * h:128 * h + MLA_NOPE].astype(BF16), kr], axis=1)
            vo[h] = kv[:, 128 * h + MLA_NOPE:128 * (h + 1)].astype(BF16)

    t128 = pl.BlockSpec((tm, LANES), lambda i: (i, 0))
    t768 = pl.BlockSpec((tm, 768), lambda i: (i, 0))
    return pl.pallas_call(
        body, name=name, grid=(L // tm,),
        in_specs=[t768, pl.BlockSpec((tm, 1024), lambda i: (i, 0)), _col_spec(tm, 128, KR), t768, t768, t128, t128],
        out_specs=[_head_spec(HEADS, tm, DQK_B), _head_spec(HEADS, tm, DQK_B), _head_spec(HEADS, tm, DH)],
        out_shape=[jax.ShapeDtypeStruct((HEADS, L, DQK_B), BF16), jax.ShapeDtypeStruct((HEADS, L, DQK_B), BF16),
                   jax.ShapeDtypeStruct((HEADS, L, DH), BF16)],
        compiler_params=_cparams(("parallel",)),
    )(qbm, kvbm, proj, *tab_q, *tab_k)


def _unprep_b(dq, dk, dv, tab_q, tab_k, dproj, *, name):
    L = dq.shape[1]
    tm = ROW_T

    def body(dq_ref, dk_ref, dv_ref, cq_ref, sq_ref, ck_ref, sk_ref, _, dp_kr, dqo, dkvo):
        dqv = _join(dq_ref, HEADS) * SCALE_B
        dqo[...] = _rope_lanes(dqv, cq_ref[...], -sq_ref[...], MLA_ROPE // 2).astype(BF16)
        parts = []
        dkr = None
        for h in range(HEADS):
            dkh = dk_ref[h]
            parts += [dkh[:, :MLA_NOPE], dv_ref[h]]
            r = dkh[:, MLA_NOPE:]
            dkr = r if dkr is None else dkr + r
        dkvo[...] = jnp.concatenate(parts, axis=1).astype(BF16)
        dkr = jnp.concatenate([dkr, jnp.zeros((tm, LANES - MLA_ROPE), F32)], axis=1)
        dp_kr[...] = _rope_lanes(dkr, ck_ref[...], -sk_ref[...], MLA_ROPE // 2).astype(BF16)

    t128 = pl.BlockSpec((tm, LANES), lambda i: (i, 0))
    t768 = pl.BlockSpec((tm, 768), lambda i: (i, 0))
    hq = _head_spec(HEADS, tm, DQK_B)
    return pl.pallas_call(
        body, name=name, grid=(L // tm,),
        in_specs=[hq, hq, _head_spec(HEADS, tm, DH), t768, t768, t128, t128, pl.BlockSpec(memory_space=pl.ANY)],
        out_specs=[_col_spec(tm, 128, KR), t768, pl.BlockSpec((tm, 1024), lambda i: (i, 0))],
        out_shape=[jax.ShapeDtypeStruct(dproj.shape, dproj.dtype), jax.ShapeDtypeStruct((L, 768), BF16),
                   jax.ShapeDtypeStruct((L, 1024), BF16)],
        input_output_aliases={7: 0},
        compiler_params=_cparams(("parallel",)),
    )(dq, dk, dv, *tab_q, *tab_k, dproj)


def _gate(y, proj, zcol, *, name):
    L = proj.shape[0]
    tm = ROW_T

    def body(y_ref, z_ref, u_ref):
        z = z_ref[...]
        u_ref[...] = (_join(y_ref, HEADS) * (z * jax.nn.sigmoid(z))).astype(BF16)

    return pl.pallas_call(
        body, name=name, grid=(L // tm,),
        in_specs=[_head_spec(HEADS, tm, DH), _col_spec(tm, 512, zcol)],
        out_specs=pl.BlockSpec((tm, 512), lambda i: (i, 0)),
        out_shape=jax.ShapeDtypeStruct((L, 512), BF16),
        compiler_params=_cparams(("parallel",)),
    )(y, proj)


def _gate_bwd(du, y, proj, zcol, dproj, *, name):
    L = proj.shape[0]
    tm = ROW_T

    def body(du_ref, y_ref, z_ref, _, dz_ref, dy_ref):
        z = z_ref[...]
        duv = du_ref[...]
        sg = jax.nn.sigmoid(z)
        _split(duv * (z * sg), HEADS, DH, dy_ref)
        dz_ref[...] = (duv * _join(y_ref, HEADS) * (sg * (1.0 + z * (1.0 - sg)))).astype(BF16)

    hs = _head_spec(HEADS, tm, DH)
    return pl.pallas_call(
        body, name=name, grid=(L // tm,),
        in_specs=[pl.BlockSpec((tm, 512), lambda i: (i, 0)), hs, _col_spec(tm, 512, zcol),
                  pl.BlockSpec(memory_space=pl.ANY)],
        out_specs=[_col_spec(tm, 512, zcol), hs],
        out_shape=[jax.ShapeDtypeStruct(dproj.shape, dproj.dtype), jax.ShapeDtypeStruct((HEADS, L, DH), F32)],
        input_output_aliases={3: 0},
        compiler_params=_cparams(("parallel",)),
    )(du, y, proj, dproj)


MERGE_T = 192


def _merge(proj, pbs, *, name):
    L = proj.shape[0]
    tm = MERGE_T

    def body(g0, g1, g2, p0, p1, p2, o_ref):
        acc = None
        for g_ref, p_ref in ((g0, p0), (g1, p1), (g2, p2)):
            t = jax.nn.sigmoid(g_ref[...]) * p_ref[...]
            acc = t if acc is None else acc + t
        o_ref[...] = acc.astype(BF16)

    row = pl.BlockSpec((tm, D_MODEL), lambda i: (i, 0))
    return pl.pallas_call(
        body, name=name, grid=(L // tm,),
        in_specs=[_col_spec(tm, D_MODEL, GATES + n * D_MODEL) for n in range(N_BRANCH)] + [row] * N_BRANCH,
        out_specs=row, out_shape=jax.ShapeDtypeStruct((L, D_MODEL), BF16),
        compiler_params=_cparams(("parallel",)),
    )(proj, proj, proj, *pbs)


def _merge_bwd(dmerged, proj, pbs, dproj, *, name):
    L = proj.shape[0]
    tm = MERGE_T

    def body(dm_ref, g_ref, p0, p1, p2, _, dg_ref, dp0, dp1, dp2):
        dm = dm_ref[...]
        for n, (p_ref, dp_ref) in enumerate(((p0, dp0), (p1, dp1), (p2, dp2))):
            cols = slice(n * D_MODEL, (n + 1) * D_MODEL)
            sg = jax.nn.sigmoid(g_ref[:, cols])
            dp_ref[...] = (dm * sg).astype(BF16)
            dg_ref[:, cols] = (dm * p_ref[...] * (sg * (1.0 - sg))).astype(BF16)

    row = pl.BlockSpec((tm, D_MODEL), lambda i: (i, 0))
    gates = _col_spec(tm, N_BRANCH * D_MODEL, GATES)
    outs = pl.pallas_call(
        body, name=name, grid=(L // tm,),
        in_specs=[row, gates] + [row] * N_BRANCH + [pl.BlockSpec(memory_space=pl.ANY)],
        out_specs=[gates] + [row] * N_BRANCH,
        out_shape=[jax.ShapeDtypeStruct(dproj.shape, dproj.dtype)]
        + [jax.ShapeDtypeStruct((L, D_MODEL), BF16)] * N_BRANCH,
        input_output_aliases={5: 0},
        compiler_params=_cparams(("parallel",)),
    )(dmerged, proj, *pbs, dproj)
    return outs[0], outs[1:]


def _forget_bias(af, b_f):
    return jnp.cumsum(jax.nn.log_sigmoid(af + b_f), axis=0).T


def _key_bias(L, T, ct=None):
    padb = jnp.where(jnp.arange(L) < PAD, BIG, 0.0).astype(F32)[None]
    kb = padb if ct is None else ct + padb
    return kb.reshape(kb.shape[0], L // T, 1, T)


def _layer_fwd(h, w, tabs, l):
    tag = f"l{l}"
    L = h.shape[0]
    hn = _rms_fwd(h, w["norm_g"], name=f"{tag}_rms_in")
    proj = _mm(hn, w["w_in"], name=f"{tag}_mm_in")
    ct, vjp_f = jax.vjp(_forget_bias, proj[:, FA:FA + HEADS], w["b_f"])
    ops_a = (*_prep_a(proj, name=f"{tag}_prep_a"), _key_bias(L, ATT_T, ct))
    ya, lsea = _attn_fwd(*ops_a, T=ATT_T, name=f"{tag}_attn_a")
    cqn = _rms_fwd(proj, w["g_cq"], col=CQ, name=f"{tag}_rms_cq")
    ckvn = _rms_fwd(proj, w["g_ckv"], col=CKV, name=f"{tag}_rms_ckv")
    qbm = _mm(cqn, w["w_uq"], name=f"{tag}_mm_uq")
    kvbm = _mm(ckvn, w["w_ukv"], name=f"{tag}_mm_ukv")
    ops_b = (*_prep_b(qbm, kvbm, proj, tabs["bq"], tabs["bk"], name=f"{tag}_prep_b"), _key_bias(L, ATT_T))
    yb, lseb = _attn_fwd(*ops_b, T=ATT_T, name=f"{tag}_attn_b")
    ops_c = (*_prep_c(proj, tabs["c"], name=f"{tag}_prep_c"), _key_bias(L, WINDOW))
    sink = jnp.broadcast_to(w["sinks"][:, None, None], (HEADS, 1, LANES))
    yc, lsec = _attn_fwd(*ops_c, T=WINDOW, window=True, sink=sink, name=f"{tag}_attn_c")
    us = [_gate(y, proj, zcol, name=f"{tag}_gate{n}") for n, (y, zcol) in enumerate(((ya, ZA), (yb, ZB), (yc, ZC)))]
    pbr = [_mm(us[n], w["w_branch"][n], name=f"{tag}_mm_br{n}") for n in range(N_BRANCH)]
    merged = _merge(proj, pbr, name=f"{tag}_merge")
    out = _mm(merged, w["w_out"], add=h, name=f"{tag}_mm_out")
    saved = dict(h=h, hn=hn, proj=proj, vjp_f=vjp_f, ops_a=ops_a, ya=ya, lsea=lsea, cqn=cqn, ckvn=ckvn,
                 ops_b=ops_b, yb=yb, lseb=lseb, ops_c=ops_c, yc=yc, lsec=lsec, us=us, pbr=pbr, merged=merged)
    return out, saved


def _layer_bwd(dout, s, w, tabs, l):
    tag = f"l{l}"
    L = dout.shape[0]
    proj = s["proj"]
    g = {}
    dproj = jnp.zeros((L, NP_IN), BF16)
    dmerged = _mm(dout, w["w_out"], tb=True, name=f"{tag}_mm_out_dx")
    g["w_out"] = _mm(s["merged"], dout, ta=True, out_dtype=BF16, name=f"{tag}_mm_out_dw")
    dproj, dpbr = _merge_bwd(dmerged, proj, s["pbr"], dproj, name=f"{tag}_merge_bwd")
    dus = [_mm(dpbr[n], w["w_branch"][n], tb=True, name=f"{tag}_mm_br{n}_dx") for n in range(N_BRANCH)]
    g["w_branch"] = jnp.stack([_mm(s["us"][n], dpbr[n], ta=True, out_dtype=BF16, name=f"{tag}_mm_br{n}_dw")
                               for n in range(N_BRANCH)])
    dys = []
    for n, (y, zcol) in enumerate(((s["ya"], ZA), (s["yb"], ZB), (s["yc"], ZC))):
        dproj, dy = _gate_bwd(dus[n], y, proj, zcol, dproj, name=f"{tag}_gate{n}_bwd")
        dys.append(dy)
    dya, dyb, dyc = dys
    dqa, dka, dva, dcq, dck = _attn_bwd(*s["ops_a"], s["ya"], s["lsea"], dya, T=ATT_T, fox=True,
                                        name=f"{tag}_attn_a_bwd")
    dproj = _unprep_a(dqa, dka, dva, dproj, name=f"{tag}_unprep_a")
    daf, g["b_f"] = s["vjp_f"](dcq[:, :, 0] + dck.reshape(HEADS, L))
    dproj = lax.dynamic_update_slice(dproj, daf.astype(BF16), (0, FA))
    dqb, dkb, dvb, _, _ = _attn_bwd(*s["ops_b"], s["yb"], s["lseb"], dyb, T=ATT_T, name=f"{tag}_attn_b_bwd")
    dproj, dqbm, dkvbm = _unprep_b(dqb, dkb, dvb, tabs["bq"], tabs["bk"], dproj, name=f"{tag}_unprep_b")
    dcqn = _mm(dqbm, w["w_uq"], tb=True, name=f"{tag}_mm_uq_dx")
    g["w_uq"] = _mm(s["cqn"], dqbm, ta=True, out_dtype=BF16, name=f"{tag}_mm_uq_dw")
    dckvn = _mm(dkvbm, w["w_ukv"], tb=True, name=f"{tag}_mm_ukv_dx")
    g["w_ukv"] = _mm(s["ckvn"], dkvbm, ta=True, out_dtype=BF16, name=f"{tag}_mm_ukv_dw")
    dproj, g["g_cq"] = _rms_bwd(dcqn, proj, w["g_cq"], col=CQ, into=dproj, name=f"{tag}_rms_cq_bwd")
    dproj, g["g_ckv"] = _rms_bwd(dckvn, proj, w["g_ckv"], col=CKV, into=dproj, name=f"{tag}_rms_ckv_bwd")
    dqc, dkc, dvc = _attn_bwd_window(*s["ops_c"], s["yc"], s["lsec"], dyc, name=f"{tag}_attn_c_bwd")
    dproj = _unprep_c(dqc, dkc, dvc, tabs["c"], dproj, name=f"{tag}_unprep_c")
    delta_c = jnp.sum(dyc * s["yc"], axis=-1)
    g["sinks"] = -jnp.sum(jnp.exp(w["sinks"][:, None] - s["lsec"][:, :, 0]) * delta_c, axis=1)
    dhn = _mm(dproj, w["w_in"], tb=True, name=f"{tag}_mm_in_dx")
    g["w_in"] = _mm(s["hn"], dproj, ta=True, out_dtype=BF16, name=f"{tag}_mm_in_dw")
    dh, g["norm_g"] = _rms_bwd(dhn, s["h"], w["norm_g"], add=dout, name=f"{tag}_rms_in_bwd")
    return dh, g


def _cols_from_shards(g):
    return jnp.moveaxis(g, 0, 1).reshape(g.shape[1], N_DEV * g.shape[2])


def _cols_to_shards(w):
    R = w.shape[0]
    return jnp.moveaxis(w.reshape(R, N_DEV, w.shape[1] // N_DEV), 1, 0)


def _pad_in(w):
    parts, at = [], 0
    for lo, hi, dst in sorted(_RUNS, key=lambda r: r[2]):
        if dst > at:
            parts.append(jnp.zeros((w.shape[0], dst - at), w.dtype))
        parts.append(w[:, lo:hi])
        at = dst + hi - lo
    parts.append(jnp.zeros((w.shape[0], NP_IN - at), w.dtype))
    return jnp.concatenate(parts, axis=1)


def _unpad_in(w):
    return jnp.concatenate([w[:, dst:dst + hi - lo] for lo, hi, dst in _RUNS], axis=1)


_SMALL = (("norm_g", DEPTH * D_MODEL), ("b_f", DEPTH * HEADS), ("g_cq", DEPTH * MLA_QLORA),
          ("g_ckv", DEPTH * MLA_KVLORA), ("sinks", DEPTH * HEADS), ("final_g", D_MODEL), ("loss", 1),
          ("meta", N_META * D_MODEL))
SMALL_ROWS = 168


def _pack_small(d):
    parts = []
    for name, size in _SMALL:
        padded = -(-size // 128) * 128
        v = d[name].reshape(-1).astype(F32) if name in d else jnp.zeros((size,), F32)
        parts.append(jnp.pad(v, (0, padded - size)))
    flat = jnp.concatenate(parts)
    return jnp.pad(flat, (0, SMALL_ROWS * 128 - flat.shape[0])).reshape(SMALL_ROWS, 128)


def _unpack_small(p, shapes):
    flat = p.reshape(-1)
    out, at = {}, 0
    for name, size in _SMALL:
        if name in shapes:
            out[name] = flat[at:at + size].reshape(shapes[name])
        at += -(-size // 128) * 128
    return out


def kernel(x, meta_tokens, norm_g, w_in, b_f, g_cq, g_ckv, w_uq, w_ukv, sinks, w_branch, w_out, final_g, loss_target, m_meta_tokens, m_norm_g, m_w_in, m_b_f, m_g_cq, m_g_ckv, m_w_uq, m_w_ukv, m_sinks, m_w_branch, m_w_out, m_final_g, v_meta_tokens, v_norm_g, v_w_in, v_b_f, v_g_cq, v_g_ckv, v_w_uq, v_w_ukv, v_sinks, v_w_branch, v_w_out, v_final_g):
    S = x.shape[1]
    L = BLK + S
    cx, cy, cc = _me()
    my_idx = 4 * cx + 2 * cy + cc

    gw_in, gw_uq, gw_ukv, gw_br, gw_out, g_meta = _all_gather(
        [w_in.astype(BF16), w_uq.astype(BF16), w_ukv.astype(BF16), w_branch.astype(BF16), w_out.astype(BF16),
         meta_tokens], name="gather_weights")
    layers = []
    for l in range(DEPTH):
        layers.append(dict(
            norm_g=norm_g[l], b_f=b_f[l], g_cq=g_cq[l], g_ckv=g_ckv[l], sinks=sinks[l],
            w_in=_pad_in(_cols_from_shards(gw_in[:, l])),
            w_uq=_cols_from_shards(gw_uq[:, l]),
            w_ukv=_cols_from_shards(gw_ukv[:, l]),
            w_branch=jnp.stack([_cols_from_shards(gw_br[:, l, n]) for n in range(N_BRANCH)]),
            w_out=gw_out[:, l].reshape(D_MODEL, D_MODEL)))
    meta_full = _cols_from_shards(g_meta)

    h = jnp.concatenate([jnp.zeros((PAD, D_MODEL), F32), meta_full, x[0]], axis=0)
    q_lanes = jnp.arange(HEADS * DQK_B)
    tabs = dict(c=_rope_tables(L, DH // 2, LANES), bk=_rope_tables(L, MLA_ROPE // 2, LANES),
                bq=_rope_tables(L, MLA_ROPE // 2, HEADS * DQK_B, rope_lane=(q_lanes % DQK_B) >= MLA_NOPE))
    saved = []
    for l in range(DEPTH):
        h, s = _layer_fwd(h, layers[l], tabs, l)
        saved.append(s)
    loss_vec, dh, g_final = _loss_head(h, final_g, loss_target[0], name="loss_head")

    grads = [None] * DEPTH
    for l in reversed(range(DEPTH)):
        dh, grads[l] = _layer_bwd(dh, saved[l], layers[l], tabs, l)

    def stack(name):
        return jnp.stack([grads[l][name] for l in range(DEPTH)])

    s_in = jnp.stack([_cols_to_shards(_unpad_in(grads[l]["w_in"])) for l in range(DEPTH)], axis=1)
    s_uq = jnp.stack([_cols_to_shards(grads[l]["w_uq"]) for l in range(DEPTH)], axis=1)
    s_ukv = jnp.stack([_cols_to_shards(grads[l]["w_ukv"]) for l in range(DEPTH)], axis=1)
    s_br = jnp.stack([jnp.stack([_cols_to_shards(grads[l]["w_branch"][n]) for n in range(N_BRANCH)], axis=1)
                      for l in range(DEPTH)], axis=1)
    s_out = jnp.stack([grads[l]["w_out"].reshape(N_DEV, D_MODEL // N_DEV, D_MODEL) for l in range(DEPTH)], axis=1)
    r_in, r_uq, r_ukv, r_br, r_out = _exchange([t.astype(BF16) for t in (s_in, s_uq, s_ukv, s_br, s_out)],
                                               name="exchange_grads")

    small = _pack_small(dict(norm_g=stack("norm_g"), b_f=stack("b_f"), g_cq=stack("g_cq"), g_ckv=stack("g_ckv"),
                             sinks=stack("sinks"), final_g=g_final, loss=loss_vec[0, 0:1],
                             meta=dh[PAD:BLK]))
    (g_small,) = _all_gather([small], name="gather_small")

    def adam_big(w_, parts, m_, v_, name):
        shape = w_.shape
        C = shape[-1]
        R = math.prod(shape[:-1])
        outs = _adamw(w_.reshape(R, C), parts.reshape(parts.shape[0], R, C), m_.reshape(R, C), v_.reshape(R, C),
                      name=name)
        return [o.reshape(shape) for o in outs]

    res = {}
    res["w_in"] = adam_big(w_in, r_in, m_w_in, v_w_in, "adam_w_in")
    res["w_uq"] = adam_big(w_uq, r_uq, m_w_uq, v_w_uq, "adam_w_uq")
    res["w_ukv"] = adam_big(w_ukv, r_ukv, m_w_ukv, v_w_ukv, "adam_w_ukv")
    res["w_branch"] = adam_big(w_branch, r_br, m_w_branch, v_w_branch, "adam_w_branch")
    res["w_out"] = adam_big(w_out, r_out, m_w_out, v_w_out, "adam_w_out")

    small_w = dict(norm_g=norm_g, b_f=b_f, g_cq=g_cq, g_ckv=g_ckv, sinks=sinks, final_g=final_g)
    small_m = dict(norm_g=m_norm_g, b_f=m_b_f, g_cq=m_g_cq, g_ckv=m_g_ckv, sinks=m_sinks, final_g=m_final_g)
    small_v = dict(norm_g=v_norm_g, b_f=v_b_f, g_cq=v_g_cq, g_ckv=v_g_ckv, sinks=v_sinks, final_g=v_final_g)
    sm = _adamw(_pack_small(small_w), g_small, _pack_small(small_m), _pack_small(small_v), name="adam_small")
    shapes = {k: a.shape for k, a in small_w.items()}
    shapes_all = dict(shapes, loss=(), meta=(N_META, D_MODEL))
    sm_g = _unpack_small(sm[0], shapes_all)
    sm_d, sm_m, sm_v = (_unpack_small(t, shapes) for t in sm[1:])
    for k in shapes:
        res[k] = [sm_g[k], sm_d[k], sm_m[k], sm_v[k]]
    g_meta_mine = lax.dynamic_slice(sm_g["meta"], (0, my_idx * 128), (N_META, 128))
    res["meta_tokens"] = _adamw(meta_tokens, g_meta_mine[None], m_meta_tokens, v_meta_tokens, name="adam_meta")

    order = ["meta_tokens", "norm_g", "w_in", "b_f", "g_cq", "g_ckv", "w_uq", "w_ukv", "sinks", "w_branch", "w_out",
             "final_g"]
    grad_x = dh[BLK:][None]
    return (sm_g["loss"], grad_x, *[res[k][0] for k in order], *[res[k][1] for k in order],
            *[res[k][2] for k in order], *[res[k][3] for k in order])
```

```python
import functools
import math

import jax
import jax.numpy as jnp
from jax import lax
from jax.experimental import pallas as pl
from jax.experimental.pallas import tpu as pltpu

F32 = jnp.float32
BF16 = jnp.bfloat16

D_MODEL = 1024
DEPTH = 2
N_META = 16
BLK = 128
PAD = BLK - N_META
ROPE_THETA = 10000.0
EPS = 1e-6
NEG = -1e30
BIG = 1e30
HEADS = 8
DH = 64
MLA_NOPE = 64
MLA_ROPE = 32
MLA_QLORA = 384
MLA_KVLORA = 256
SWA_KV_HEADS = 2
WINDOW = 128
BRANCH_W = 512
N_BRANCH = 3
N_IN = 7592

ADAM_LR = 0.001
ADAM_B1 = 0.9
ADAM_B2 = 0.999
ADAM_EPS = 1e-08
ADAM_WD = 0.01
ADAM_STEP = 10

N_DEV = 8
MESH = pl.DeviceIdType.MESH

NP_IN = 8192
QKVA, ZA, FA = 0, 1536, 2048
CKV, KR, CQ = 2304, 2560, 2688
GATES = 3072
ZB = 6144
QKVC, ZC = 6912, 7680
_RUNS = ((0, 1536, QKVA), (1536, 1544, FA), (1544, 2056, ZA), (2056, 2440, CQ), (2440, 2696, CKV), (2696, 2728, KR),
         (2728, 3240, ZB), (3240, 4008, QKVC), (4008, 4520, ZC), (4520, 7592, GATES))

VMEM_LIMIT = 48 * 1024 * 1024
ATT_T = 384
ATT_RB = 128
ROW_T = 384
LANES = 128


def _pick(dim, prefs):
    for p in prefs:
        if dim % p == 0:
            return p
    return dim


def _cparams(sem):
    return pltpu.CompilerParams(dimension_semantics=sem, vmem_limit_bytes=VMEM_LIMIT)


def _side_cparams(side, sem):
    if side is None:
        return _cparams(sem)
    return pltpu.CompilerParams(dimension_semantics=("arbitrary",) * len(sem), vmem_limit_bytes=VMEM_LIMIT,
                                has_side_effects=True)


def _mm(a, b, *, ta=False, tb=False, add=None, out_dtype=F32, name):
    M = a.shape[1] if ta else a.shape[0]
    K = a.shape[0] if ta else a.shape[1]
    N = b.shape[0] if tb else b.shape[1]
    assert K == (b.shape[1] if tb else b.shape[0])
    tm = _pick(M, (704, 1024, 512, 384, 256, 128))
    tn = _pick(N, (1024, 768, 512, 384, 256, 128))
    tk = _pick(K, (2048, 1408, 1024, 768, 512, 384, 256, 128))
    nk = K // tk
    dims = (((0 if ta else 1,), (1 if tb else 0,)), ((), ()))

    def body(*refs):
        a_ref, b_ref = refs[:2]
        c_ref = refs[2] if add is not None else None
        o_ref = refs[3] if add is not None else refs[2]
        r = lax.dot_general(a_ref[...].astype(BF16), b_ref[...].astype(BF16), dims, preferred_element_type=F32)

        def finish(total):
            if c_ref is not None:
                total = total + c_ref[...]
            o_ref[...] = total.astype(out_dtype)

        if nk == 1:
            finish(r)
        else:
            acc = refs[-1]
            k = pl.program_id(2)

            @pl.when(k == 0)
            def _():
                acc[...] = r

            @pl.when(k > 0)
            def _():
                acc[...] += r

            @pl.when(k == nk - 1)
            def _():
                finish(acc[...])

    a_spec = pl.BlockSpec((tk, tm), lambda i, j, k: (k, i)) if ta else pl.BlockSpec((tm, tk), lambda i, j, k: (i, k))
    b_spec = pl.BlockSpec((tn, tk), lambda i, j, k: (j, k)) if tb else pl.BlockSpec((tk, tn), lambda i, j, k: (k, j))
    o_spec = pl.BlockSpec((tm, tn), lambda i, j, k: (i, j))
    return pl.pallas_call(
        body, name=name,
        grid=(M // tm, N // tn, nk),
        in_specs=[a_spec, b_spec] + ([o_spec] if add is not None else []),
        out_specs=o_spec,
        out_shape=jax.ShapeDtypeStruct((M, N), out_dtype),
        scratch_shapes=[pltpu.VMEM((tm, tn), F32)] if nk > 1 else [],
        compiler_params=_cparams(("parallel", "parallel", "arbitrary")),
    )(*((a, b) if add is None else (a, b, add)))


def _col_spec(tm, width, col):
    assert col % width == 0
    return pl.BlockSpec((tm, width), lambda i, _c=col // width: (i, _c))


def _rms_fwd(x, g, *, col=0, name):
    L = x.shape[0]
    D = g.shape[0]
    tm = ROW_T

    def body(x_ref, g_ref, y_ref):
        xv = x_ref[...]
        rstd = lax.rsqrt(jnp.mean(xv * xv, axis=-1, keepdims=True) + EPS)
        y_ref[...] = (xv * rstd * g_ref[...]).astype(BF16)

    return pl.pallas_call(
        body, name=name, grid=(L // tm,),
        in_specs=[_col_spec(tm, D, col), pl.BlockSpec((1, D), lambda i: (0, 0))],
        out_specs=pl.BlockSpec((tm, D), lambda i: (i, 0)),
        out_shape=jax.ShapeDtypeStruct((L, D), BF16),
        compiler_params=_cparams(("parallel",)),
    )(x, g.reshape(1, D))


def _rms_bwd(dy, x, g, *, col=0, add=None, into=None, name):
    L = x.shape[0]
    D = g.shape[0]
    tm = ROW_T

    def body(*refs):
        dy_ref, x_ref, g_ref = refs[:3]
        add_ref = refs[3] if add is not None else None
        dx_ref, dg_ref = refs[-2:]
        i = pl.program_id(0)
        xv = x_ref[...]
        dyv = dy_ref[...]
        rstd = lax.rsqrt(jnp.mean(xv * xv, axis=-1, keepdims=True) + EPS)
        xhat = xv * rstd
        part = jnp.sum(dyv * xhat, axis=0, keepdims=True)

        @pl.when(i == 0)
        def _():
            dg_ref[...] = part

        @pl.when(i > 0)
        def _():
            dg_ref[...] += part

        dxh = dyv * g_ref[...]
        dx = rstd * (dxh - xhat * jnp.mean(dxh * xhat, axis=-1, keepdims=True))
        if add_ref is not None:
            dx = dx + add_ref[...]
        dx_ref[...] = dx.astype(dx_ref.dtype)

    row = pl.BlockSpec((tm, D), lambda i: (i, 0))
    in_specs = [row, _col_spec(tm, D, col), pl.BlockSpec((1, D), lambda i: (0, 0))]
    args = [dy, x, g.reshape(1, D)]
    aliases = {}
    if add is not None:
        in_specs.append(row)
        args.append(add)
    if into is not None:
        in_specs.append(pl.BlockSpec(memory_space=pl.ANY))
        args.append(into)
        aliases = {len(args) - 1: 0}
        dx_spec, dx_shape = _col_spec(tm, D, col), jax.ShapeDtypeStruct(into.shape, into.dtype)
    else:
        dx_spec, dx_shape = row, jax.ShapeDtypeStruct((L, D), F32)
    dx, dg = pl.pallas_call(
        body, name=name, grid=(L // tm,),
        in_specs=in_specs,
        out_specs=[dx_spec, pl.BlockSpec((1, D), lambda i: (0, 0))],
        out_shape=[dx_shape, jax.ShapeDtypeStruct((1, D), F32)],
        input_output_aliases=aliases,
        compiler_params=_cparams(("arbitrary",)),
    )(*args)
    return dx, dg.reshape(D)


def _loss_head(h, g, target, *, name):
    L, D = h.shape
    nb = L // BLK

    def body(h_ref, g_ref, t_ref, loss_ref, dh_ref, dg_ref):
        i = pl.program_id(0)

        @pl.when(i == 0)
        def _():
            loss_ref[...] = jnp.zeros_like(loss_ref)
            dg_ref[...] = jnp.zeros_like(dg_ref)
            dh_ref[...] = jnp.zeros_like(dh_ref)

        @pl.when(i > 0)
        def _():
            xv = h_ref[...]
            gv = g_ref[...]
            rstd = lax.rsqrt(jnp.mean(xv * xv, axis=-1, keepdims=True) + EPS)
            xhat = xv * rstd
            err = xhat * gv - t_ref[...]
            row = jnp.mean(err * err, axis=-1, keepdims=True)
            loss_ref[...] += 0.5 * jnp.sum(row, axis=0, keepdims=True)
            dy = err * (1.0 / D)
            dg_ref[...] += jnp.sum(dy * xhat, axis=0, keepdims=True)
            dxh = dy * gv
            dh_ref[...] = rstd * (dxh - xhat * jnp.mean(dxh * xhat, axis=-1, keepdims=True))

    loss, dh, dg = pl.pallas_call(
        body, name=name, grid=(nb,),
        in_specs=[pl.BlockSpec((BLK, D), lambda i: (i, 0)), pl.BlockSpec((1, D), lambda i: (0, 0)),
                  pl.BlockSpec((BLK, D), lambda i: (jnp.maximum(i - 1, 0), 0))],
        out_specs=[pl.BlockSpec((1, 128), lambda i: (0, 0)), pl.BlockSpec((BLK, D), lambda i: (i, 0)),
                   pl.BlockSpec((1, D), lambda i: (0, 0))],
        out_shape=[jax.ShapeDtypeStruct((1, 128), F32), jax.ShapeDtypeStruct((L, D), F32),
                   jax.ShapeDtypeStruct((1, D), F32)],
        compiler_params=_cparams(("arbitrary",)),
    )(h, g.reshape(1, D), target)
    return loss, dh, dg.reshape(D)


_NT = (((1,), (1,)), ((), ()))
_TN = (((0,), (0,)), ((), ()))


def _attn_fwd(q, k, v, kbias, *, T, window=False, sink=None, side=None, name):
    sn = 0 if side is None else side.n
    H, L, dk = q.shape
    Hkv = k.shape[0]
    dv = v.shape[2]
    G = H // Hkv
    nt = L // T
    Hb = kbias.shape[0]
    reps = T // LANES
    assert not window or T == WINDOW
    HP = G if G > 1 else 2
    NS = 1 if G > 1 else HP
    R = HP * T // NS
    HKV = HP // G
    HB = HP if Hb > 1 else 1
    assert G == 1 or Hb == 1

    def body(*refs):
        q_ref, k_ref, v_ref, kb_ref = refs[:4]
        n = 4
        if sink is not None:
            sk_ref = refs[n]
            n += 1
        side_in = refs[n:n + sn]
        n += sn
        o_ref, lse_ref = refs[n:n + 2]
        side_out = refs[n + 2:n + 2 + sn]
        n += 2 + sn
        m_scs, l_scs, acc_scs = (refs[n + t * NS:n + (t + 1) * NS] for t in range(3))
        i = pl.program_id(1)
        if side is not None:
            _ride(side, [*side_in, *side_out, *refs[n + 3 * NS:]], pl.program_id(0) * nt + i, (H // HP) * nt)
        for a in range(NS):
            if sink is not None:
                sk = [jnp.broadcast_to(sk_ref[b, :, 0:1], (T, LANES)) for b in range(HP)]
                m_scs[a][...] = jnp.concatenate(sk, axis=0) if G > 1 else sk[a]
                l_scs[a][...] = jnp.ones((R, LANES), F32)
            else:
                m_scs[a][...] = jnp.full((R, LANES), NEG, F32)
                l_scs[a][...] = jnp.zeros((R, LANES), F32)
            acc_scs[a][...] = jnp.zeros((R, dv), F32)
        row = lax.broadcasted_iota(jnp.int32, (R, T), 0) & (T - 1) if G > 1 else \
            lax.broadcasted_iota(jnp.int32, (R, T), 0)
        col = lax.broadcasted_iota(jnp.int32, (R, T), 1)

        def tile(j, kind):
            rows = pl.ds(pl.multiple_of(j * T, T), T)
            for a in range(NS):
                m_sc, l_sc, acc_sc = m_scs[a], l_scs[a], acc_scs[a]
                qv = q_ref[...].reshape(R, dk) if G > 1 else q_ref[a]
                s = lax.dot_general(qv, k_ref[a, rows, :], _NT, preferred_element_type=F32)
                s = s - kb_ref[a if HB > 1 else 0, j]
                if kind == "diag":
                    s = jnp.where(row >= col, s, NEG)
                elif kind == "prev":
                    s = jnp.where(col > row, s, NEG)
                m_prev = m_sc[...]
                m_new = jnp.maximum(m_prev, jnp.max(s, axis=1, keepdims=True))
                alpha = jnp.exp(m_prev - m_new)
                p = jnp.exp(s - jnp.tile(m_new, (1, reps)))
                l_sc[...] = alpha * l_sc[...] + jnp.sum(p, axis=1, keepdims=True)
                acc_sc[...] = alpha[:, :dv] * acc_sc[...] + jnp.dot(p.astype(BF16), v_ref[a, rows, :],
                                                                    preferred_element_type=F32)
                m_sc[...] = m_new

        if window:
            @pl.when(i > 0)
            def _():
                tile(i - 1, "prev")

            tile(i, "diag")
        else:
            tile(i, "diag")

            def full(j, c):
                tile(j, "full")
                return c

            lax.fori_loop(0, i, full, 0)

        for a in range(NS):
            lv = l_scs[a][...]
            ov = acc_scs[a][...] / lv[:, :dv]
            lsev = (m_scs[a][...] + jnp.log(lv))[:, 0:1]
            if G > 1:
                o_ref[...] = ov.reshape(HP, T, dv)
                lse_ref[...] = lsev.reshape(HP, T, 1)
            else:
                o_ref[a] = ov
                lse_ref[a] = lsev

    in_specs = [pl.BlockSpec((HP, T, dk), lambda h, i: (h, i, 0)),
                pl.BlockSpec((HKV, L, dk), lambda h, i: (h, 0, 0)),
                pl.BlockSpec((HKV, L, dv), lambda h, i: (h, 0, 0)),
                pl.BlockSpec((HB, nt, 1, T), lambda h, i: (h if Hb > 1 else 0, 0, 0, 0))]
    args = [q, k, v, kbias]
    if sink is not None:
        in_specs += [pl.BlockSpec((HP, 1, LANES), lambda h, i: (h, 0, 0))]
        args += [sink]
    any_spec = pl.BlockSpec(memory_space=pl.ANY)
    return pl.pallas_call(
        body, name=name, grid=(H // HP, nt),
        in_specs=in_specs + [any_spec] * sn,
        out_specs=[pl.BlockSpec((HP, T, dv), lambda h, i: (h, i, 0)),
                   pl.BlockSpec((HP, T, 1), lambda h, i: (h, i, 0))] + [any_spec] * sn,
        out_shape=[jax.ShapeDtypeStruct((H, L, dv), F32), jax.ShapeDtypeStruct((H, L, 1), F32)]
        + ([] if side is None else side.out_shape),
        scratch_shapes=[pltpu.VMEM((R, LANES), F32)] * (2 * NS) + [pltpu.VMEM((R, dv), F32)] * NS
        + ([] if side is None else side.scratch),
        compiler_params=_side_cparams(side, ("parallel", "arbitrary")),
    )(*args, *([] if side is None else side.arrs))


def _attn_bwd(q, k, v, kbias, o, lse, do, *, T, window=False, fox=False, side=None, name):
    sn = 0 if side is None else side.n
    H, L, dk = q.shape
    Hkv = k.shape[0]
    dv = v.shape[2]
    G = H // Hkv
    nt = L // T
    Hb = kbias.shape[0]
    assert not window or T == WINDOW

    def body(*refs):
        q_ref, k_ref, v_ref, kb_ref, o_ref, lse_ref, do_ref = refs[:7]
        side_in = refs[7:7 + sn]
        n = 7 + sn
        dq_ref, dk_ref, dv_ref = refs[n:n + 3]
        n += 3
        if fox:
            dcq_ref, dck_ref = refs[n:n + 2]
            n += 2
        side_out = refs[n:n + sn]
        n += sn
        delta_sc = refs[n]
        j = pl.program_id(1)
        if side is not None:
            _ride(side, [*side_in, *side_out, *refs[n + 1:]], pl.program_id(0) * nt + j, H * nt)

        @pl.when(j == 0)
        def _():
            dq_ref[...] = jnp.zeros_like(dq_ref)
            if fox:
                dcq_ref[...] = jnp.zeros_like(dcq_ref)

            def dl(i, c):
                rows = pl.ds(pl.multiple_of(i * T, T), T)
                delta_sc[rows, :] = jnp.sum(do_ref[0, rows, :] * o_ref[0, rows, :], axis=1, keepdims=True)
                return c

            lax.fori_loop(0, nt, dl, 0)

        dk_ref[...] = jnp.zeros_like(dk_ref)
        dv_ref[...] = jnp.zeros_like(dv_ref)
        if fox:
            dck_ref[...] = jnp.zeros_like(dck_ref)
        kb = k_ref[0]
        vb = v_ref[0]
        kbias_j = kb_ref[0, j]
        row = lax.broadcasted_iota(jnp.int32, (T, T), 0)
        col = lax.broadcasted_iota(jnp.int32, (T, T), 1)

        def tile(i, kind):
            rows = pl.ds(pl.multiple_of(i * T, T), T)
            qb = q_ref[0, rows, :]
            dob = do_ref[0, rows, :].astype(BF16)
            s = lax.dot_general(qb, kb, _NT, preferred_element_type=F32) - kbias_j
            if kind == "diag":
                s = jnp.where(row >= col, s, NEG)
            elif kind == "prev":
                s = jnp.where(col > row, s, NEG)
            p = jnp.exp(s - lse_ref[0, rows, :])
            dp = lax.dot_general(dob, vb, _NT, preferred_element_type=F32)
            ds = p * (dp - delta_sc[rows, :])
            dsb = ds.astype(BF16)
            dv_ref[0] += lax.dot_general(p.astype(BF16), dob, _TN, preferred_element_type=F32)
            dk_ref[0] += lax.dot_general(dsb, qb, _TN, preferred_element_type=F32)
            dq_ref[0, rows, :] += jnp.dot(dsb, kb, preferred_element_type=F32)
            if fox:
                dcq_ref[0, rows, :] += jnp.sum(ds, axis=1, keepdims=True)
                dck_ref[0, 0] += -jnp.sum(ds, axis=0, keepdims=True)

        tile(j, "diag")
        if window:
            @pl.when(j + 1 < nt)
            def _():
                tile(j + 1, "prev")
        else:
            def full(i, c):
                tile(i, "full")
                return c

            lax.fori_loop(j + 1, nt, full, 0)

    in_specs = [pl.BlockSpec((1, L, dk), lambda h, j: (h, 0, 0)),
                pl.BlockSpec((1, T, dk), lambda h, j: (h // G, j, 0)),
                pl.BlockSpec((1, T, dv), lambda h, j: (h // G, j, 0)),
                pl.BlockSpec((1, nt, 1, T), lambda h, j: (h if Hb > 1 else 0, 0, 0, 0)),
                pl.BlockSpec((1, L, dv), lambda h, j: (h, 0, 0)),
                pl.BlockSpec((1, L, 1), lambda h, j: (h, 0, 0)),
                pl.BlockSpec((1, L, dv), lambda h, j: (h, 0, 0))]
    out_specs = [pl.BlockSpec((1, L, dk), lambda h, j: (h, 0, 0)),
                 pl.BlockSpec((1, T, dk), lambda h, j: (h, j, 0)),
                 pl.BlockSpec((1, T, dv), lambda h, j: (h, j, 0))]
    out_shape = [jax.ShapeDtypeStruct((H, L, dk), F32), jax.ShapeDtypeStruct((H, L, dk), F32),
                 jax.ShapeDtypeStruct((H, L, dv), F32)]
    if fox:
        out_specs += [pl.BlockSpec((1, L, 1), lambda h, j: (h, 0, 0)),
                      pl.BlockSpec((1, 1, 1, T), lambda h, j: (h, j, 0, 0))]
        out_shape += [jax.ShapeDtypeStruct((H, L, 1), F32), jax.ShapeDtypeStruct((H, nt, 1, T), F32)]
    any_spec = pl.BlockSpec(memory_space=pl.ANY)
    outs = pl.pallas_call(
        body, name=name, grid=(H, nt),
        in_specs=in_specs + [any_spec] * sn, out_specs=out_specs + [any_spec] * sn,
        out_shape=out_shape + ([] if side is None else side.out_shape),
        scratch_shapes=[pltpu.VMEM((L, 1), F32)] + ([] if side is None else side.scratch),
        compiler_params=_side_cparams(side, ("parallel", "arbitrary")),
    )(q, k, v, kbias, o, lse, do, *([] if side is None else side.arrs))
    main, rest = outs[:len(outs) - sn], outs[len(outs) - sn:]
    return (*(main if fox else (*main, None, None)), *rest)


def _attn_bwd_window(q, k, v, kbias, o, lse, do, *, name):
    H, L, dk = q.shape
    Hkv = k.shape[0]
    dv = v.shape[2]
    G = H // Hkv
    T = WINDOW
    nt = L // T
    R = G * T

    def body(q_ref, kc_ref, kp_ref, vc_ref, vp_ref, kb_ref, o_ref, lse_ref, do_ref, dq_ref, dk_ref, dv_ref,
             dk_sc, dv_sc):
        i = pl.program_id(1)
        row = lax.broadcasted_iota(jnp.int32, (R, T), 0) & (T - 1)
        col = lax.broadcasted_iota(jnp.int32, (R, T), 1)

        @pl.when(i == 0)
        def _():
            dk_sc[...] = jnp.zeros_like(dk_sc)
            dv_sc[...] = jnp.zeros_like(dv_sc)

        @pl.when(i == nt)
        def _():
            dk_ref[0] = dk_sc[...]
            dv_ref[0] = dv_sc[...]

        @pl.when(i < nt)
        def _():
            qb = q_ref[...].reshape(R, dk)
            dof = do_ref[...].reshape(R, dv)
            dob = dof.astype(BF16)
            lse_c = lse_ref[...].reshape(R, 1)
            delta = jnp.sum(dof * o_ref[...].reshape(R, dv), axis=1, keepdims=True)

            def grads(kt, vt, kbias_j, mask):
                s = lax.dot_general(qb, kt, _NT, preferred_element_type=F32) - kbias_j
                p = jnp.exp(jnp.where(mask, s, NEG) - lse_c)
                dp = lax.dot_general(dob, vt, _NT, preferred_element_type=F32)
                ds = (p * (dp - delta)).astype(BF16)
                return (jnp.dot(ds, kt, preferred_element_type=F32),
                        lax.dot_general(ds, qb, _TN, preferred_element_type=F32),
                        lax.dot_general(p.astype(BF16), dob, _TN, preferred_element_type=F32))

            ip = jnp.maximum(i - 1, 0)
            dq_p, dk_p, dv_p = grads(kp_ref[0], vp_ref[0], kb_ref[0, ip], (col > row) & (i > 0))
            dq_c, dk_c, dv_c = grads(kc_ref[0], vc_ref[0], kb_ref[0, i], row >= col)
            dq_ref[...] = (dq_p + dq_c).reshape(G, T, dk)
            dk_ref[0] = dk_sc[...] + dk_p
            dv_ref[0] = dv_sc[...] + dv_p
            dk_sc[...] = dk_c
            dv_sc[...] = dv_c

    def cur(i):
        return jnp.minimum(i, nt - 1)

    def prev(i):
        return jnp.maximum(jnp.minimum(i, nt - 1) - 1, 0)

    def written(i):
        return jnp.maximum(i - 1, 0)

    qs = lambda d: pl.BlockSpec((G, T, d), lambda h, i: (h, cur(i), 0))
    return pl.pallas_call(
        body, name=name, grid=(Hkv, nt + 1),
        in_specs=[qs(dk),
                  pl.BlockSpec((1, T, dk), lambda h, i: (h, cur(i), 0)),
                  pl.BlockSpec((1, T, dk), lambda h, i: (h, prev(i), 0)),
                  pl.BlockSpec((1, T, dv), lambda h, i: (h, cur(i), 0)),
                  pl.BlockSpec((1, T, dv), lambda h, i: (h, prev(i), 0)),
                  pl.BlockSpec((1, nt, 1, T), lambda h, i: (0, 0, 0, 0)),
                  qs(dv), qs(1), qs(dv)],
        out_specs=[qs(dk),
                   pl.BlockSpec((1, T, dk), lambda h, i: (h, written(i), 0)),
                   pl.BlockSpec((1, T, dv), lambda h, i: (h, written(i), 0))],
        out_shape=[jax.ShapeDtypeStruct((H, L, dk), F32), jax.ShapeDtypeStruct((Hkv, L, dk), F32),
                   jax.ShapeDtypeStruct((Hkv, L, dv), F32)],
        scratch_shapes=[pltpu.VMEM((T, dk), F32), pltpu.VMEM((T, dv), F32)],
        compiler_params=_cparams(("parallel", "arbitrary")),
    )(q, k, k, v, v, kbias, o, lse, do)


def _adamw(w, gparts, m, v, *, name):
    n, R, C = gparts.shape
    tr = _pick(R, (128, 64, 32, 16, 8))
    c1 = 1.0 - ADAM_B1 ** ADAM_STEP
    c2 = 1.0 - ADAM_B2 ** ADAM_STEP

    def body(w_ref, g_ref, m_ref, v_ref, go_ref, d_ref, mo_ref, vo_ref):
        g = g_ref[0].astype(F32)
        for t in range(1, n):
            g = g + g_ref[t].astype(F32)
        mn = ADAM_B1 * m_ref[...] + (1.0 - ADAM_B1) * g
        vn = ADAM_B2 * v_ref[...] + (1.0 - ADAM_B2) * (g * g)
        go_ref[...] = g
        mo_ref[...] = mn
        vo_ref[...] = vn
        d_ref[...] = -ADAM_LR * ((mn / c1) / (jnp.sqrt(vn / c2) + ADAM_EPS) + ADAM_WD * w_ref[...])

    spec = pl.BlockSpec((tr, C), lambda i: (i, 0))
    return pl.pallas_call(
        body, name=name, grid=(R // tr,),
        in_specs=[spec, pl.BlockSpec((n, tr, C), lambda i: (0, i, 0)), spec, spec],
        out_specs=[spec] * 4,
        out_shape=[jax.ShapeDtypeStruct((R, C), F32)] * 4,
        compiler_params=_cparams(("parallel",)),
    )(w, gparts, m, v)


def _me():
    return lax.axis_index("x"), lax.axis_index("y"), lax.axis_index("c")


class _CommJob:
    def __init__(self, arrs):
        self.arrs = list(arrs)
        self.n = len(arrs)
        self.scratch = [pltpu.SemaphoreType.DMA((self.n, 7)), pltpu.SemaphoreType.DMA((self.n, 7)),
                        pltpu.SemaphoreType.DMA((self.n,))]

    def bind(self, refs):
        n = self.n
        self.ins, self.outs = refs[:n], refs[n:2 * n]
        self.send_sems, self.recv_sems, self.local_sems = refs[2 * n:2 * n + 3]

    def middle(self):
        pass


class _Gather(_CommJob):
    def __init__(self, arrs):
        super().__init__(arrs)
        self.out_shape = [jax.ShapeDtypeStruct((N_DEV, *a.shape), a.dtype) for a in arrs]

    def _where(self):
        x, y, c = _me()
        return (x, y, c), (x, y, 1 - c), [(1 - x, y), (x, 1 - y), (1 - x, 1 - y)], c

    def _copy(self, t, k, block, to, src=None):
        dst = self.outs[t].at[4 * block[0] + 2 * block[1] + block[2]]
        return pltpu.make_async_remote_copy(
            src_ref=dst if src is None else src, dst_ref=dst,
            send_sem=self.send_sems.at[t, k], recv_sem=self.recv_sems.at[t, k],
            device_id=to, device_id_type=MESH)

    def _mine(self, t, me):
        return pltpu.make_async_copy(self.ins[t], self.outs[t].at[4 * me[0] + 2 * me[1] + me[2]],
                                     self.local_sems.at[t])

    def _first(self, t, me, sibling, chips, c):
        return [self._copy(t, 0, me, sibling, src=self.ins[t])] + \
               [self._copy(t, 1 + j, me, (*chip, c), src=self.ins[t]) for j, chip in enumerate(chips)]

    def start(self):
        me, sibling, chips, c = self._where()
        for t in range(self.n):
            self._mine(t, me).start()
            for cp in self._first(t, me, sibling, chips, c):
                cp.start()

    def middle(self):
        me, sibling, chips, c = self._where()
        for j, chip in enumerate(chips):
            for t in range(self.n):
                self._copy(t, 1 + j, (*chip, c), me).wait_recv()
                self._copy(t, 4 + j, (*chip, c), sibling).start()

    def finish(self):
        me, sibling, chips, c = self._where()
        for t in range(self.n):
            self._copy(t, 0, sibling, me).wait_recv()
            for j, chip in enumerate(chips):
                self._copy(t, 4 + j, (*chip, 1 - c), me).wait_recv()
        for t in range(self.n):
            for cp in self._first(t, me, sibling, chips, c):
                cp.wait_send()
            for j, chip in enumerate(chips):
                self._copy(t, 4 + j, (*chip, c), sibling).wait_send()
            self._mine(t, me).wait()


class _Exchange(_CommJob):
    def __init__(self, arrs):
        super().__init__(arrs)
        self.out_shape = [jax.ShapeDtypeStruct(a.shape, a.dtype) for a in arrs]

    def _copies(self, t):
        x, y, c = _me()
        my_idx = 4 * x + 2 * y + c
        pairs = []
        for k in range(1, N_DEV):
            peer = (x ^ ((k >> 2) & 1), y ^ ((k >> 1) & 1), c ^ (k & 1))
            peer_idx = 4 * peer[0] + 2 * peer[1] + peer[2]
            sems = dict(send_sem=self.send_sems.at[t, k - 1], recv_sem=self.recv_sems.at[t, k - 1],
                        device_id=peer, device_id_type=MESH)
            pairs.append((pltpu.make_async_remote_copy(src_ref=self.ins[t].at[peer_idx],
                                                       dst_ref=self.outs[t].at[my_idx], **sems),
                          pltpu.make_async_remote_copy(src_ref=self.ins[t].at[peer_idx],
                                                       dst_ref=self.outs[t].at[peer_idx], **sems)))
        return pairs

    def _mine(self, t):
        x, y, c = _me()
        my_idx = 4 * x + 2 * y + c
        return pltpu.make_async_copy(self.ins[t].at[my_idx], self.outs[t].at[my_idx], self.local_sems.at[t])

    def start(self):
        for t in range(self.n):
            self._mine(t).start()
            for snd, _ in self._copies(t):
                snd.start()

    def finish(self):
        for t in range(self.n):
            pairs = self._copies(t)
            for _, rcv in pairs:
                rcv.wait_recv()
            for snd, _ in pairs:
                snd.wait_send()
            self._mine(t).wait()


def _comm(job, *, name):
    def body(*refs):
        job.bind(refs)
        job.start()
        job.middle()
        job.finish()

    any_spec = pl.BlockSpec(memory_space=pl.ANY)
    return pl.pallas_call(
        body, name=name,
        in_specs=[any_spec] * job.n, out_specs=[any_spec] * job.n,
        out_shape=job.out_shape, scratch_shapes=job.scratch,
        compiler_params=pltpu.CompilerParams(has_side_effects=True),
    )(*job.arrs)


def _ride(job, refs, step, total):
    job.bind(refs)

    @pl.when(step == 0)
    def _():
        job.start()

    @pl.when(step == (total * 3) // 5)
    def _():
        job.middle()

    @pl.when(step == total - 1)
    def _():
        job.finish()


SCALE_A = DH ** -0.5
SCALE_B = (MLA_NOPE + MLA_ROPE) ** -0.5
SCALE_C = DH ** -0.5
DQK_B = MLA_NOPE + MLA_ROPE


def _rope_tables(L, half, width, rope_lane=None):
    pos = (jnp.arange(L) - PAD).astype(F32)
    inv = ROPE_THETA ** (-jnp.arange(half, dtype=F32) / half)
    lane = jnp.arange(width)
    ang = pos[:, None] * inv[lane % half][None, :]
    sign = jnp.where(lane % (2 * half) < half, -1.0, 1.0).astype(F32)
    cos, sin = jnp.cos(ang), jnp.sin(ang) * sign[None, :]
    if rope_lane is not None:
        cos = jnp.where(rope_lane[None, :], cos, 1.0)
        sin = jnp.where(rope_lane[None, :], sin, 0.0)
    return cos, sin


def _rope_lanes(x, cos, sin, half):
    W = x.shape[1]
    lane = lax.broadcasted_iota(jnp.int32, x.shape, 1)
    first = (lane & (2 * half - 1)) < half
    partner = jnp.where(first, pltpu.roll(x, W - half, 1), pltpu.roll(x, half, 1))
    return x * cos + partner * sin


def _tile_lanes(t, width):
    return t if t.shape[1] == width else jnp.tile(t, (1, width // t.shape[1]))


def _split(x, H, d, dst):
    for h in range(H):
        dst[h] = x[:, d * h:d * (h + 1)].astype(dst.dtype)


def _join(src, H):
    return jnp.concatenate([src[h] for h in range(H)], axis=1)


def _head_spec(H, tm, d):
    return pl.BlockSpec((H, tm, d), lambda i: (0, i, 0))


def _prep_a(proj, *, name):
    L = proj.shape[0]
    tm = ROW_T

    def body(x_ref, qo, ko, vo):
        _split(x_ref[:, 0:512] * SCALE_A, HEADS, DH, qo)
        _split(x_ref[:, 512:1024], HEADS, DH, ko)
        _split(x_ref[:, 1024:1536], HEADS, DH, vo)

    return pl.pallas_call(
        body, name=name, grid=(L // tm,),
        in_specs=[_col_spec(tm, 1536, QKVA)],
        out_specs=[_head_spec(HEADS, tm, DH)] * 3,
        out_shape=[jax.ShapeDtypeStruct((HEADS, L, DH), BF16)] * 3,
        compiler_params=_cparams(("parallel",)),
    )(proj)


def _unprep_a(dq, dk, dv, dproj, *, name):
    L = dq.shape[1]
    tm = ROW_T

    def body(dq_ref, dk_ref, dv_ref, _, dp_ref):
        dp_ref[:, 0:512] = (_join(dq_ref, HEADS) * SCALE_A).astype(BF16)
        dp_ref[:, 512:1024] = _join(dk_ref, HEADS).astype(BF16)
        dp_ref[:, 1024:1536] = _join(dv_ref, HEADS).astype(BF16)

    hs = _head_spec(HEADS, tm, DH)
    return pl.pallas_call(
        body, name=name, grid=(L // tm,),
        in_specs=[hs, hs, hs, pl.BlockSpec(memory_space=pl.ANY)],
        out_specs=_col_spec(tm, 1536, QKVA),
        out_shape=jax.ShapeDtypeStruct(dproj.shape, dproj.dtype),
        input_output_aliases={3: 0},
        compiler_params=_cparams(("parallel",)),
    )(dq, dk, dv, dproj)


def _prep_c(proj, tab, *, name):
    L = proj.shape[0]
    tm = ROW_T

    def body(x_ref, cos_ref, sin_ref, qo, ko, vo):
        cos, sin = cos_ref[...], sin_ref[...]
        q = _rope_lanes(x_ref[:, 0:512], _tile_lanes(cos, 512), _tile_lanes(sin, 512), DH // 2)
        _split(q * SCALE_C, HEADS, DH, qo)
        _split(_rope_lanes(x_ref[:, 512:640], cos, sin, DH // 2), SWA_KV_HEADS, DH, ko)
        _split(x_ref[:, 640:768], SWA_KV_HEADS, DH, vo)

    t128 = pl.BlockSpec((tm, LANES), lambda i: (i, 0))
    return pl.pallas_call(
        body, name=name, grid=(L // tm,),
        in_specs=[_col_spec(tm, 768, QKVC), t128, t128],
        out_specs=[_head_spec(HEADS, tm, DH), _head_spec(SWA_KV_HEADS, tm, DH), _head_spec(SWA_KV_HEADS, tm, DH)],
        out_shape=[jax.ShapeDtypeStruct((HEADS, L, DH), BF16), jax.ShapeDtypeStruct((SWA_KV_HEADS, L, DH), BF16),
                   jax.ShapeDtypeStruct((SWA_KV_HEADS, L, DH), BF16)],
        compiler_params=_cparams(("parallel",)),
    )(proj, *tab)


def _unprep_c(dq, dk, dv, tab, dproj, *, name):
    L = dq.shape[1]
    tm = ROW_T

    def body(dq_ref, dk_ref, dv_ref, cos_ref, sin_ref, _, dp_ref):
        cos, nsin = cos_ref[...], -sin_ref[...]
        dqv = _join(dq_ref, HEADS) * SCALE_C
        dp_ref[:, 0:512] = _rope_lanes(dqv, _tile_lanes(cos, 512), _tile_lanes(nsin, 512), DH // 2).astype(BF16)
        dp_ref[:, 512:640] = _rope_lanes(_join(dk_ref, SWA_KV_HEADS), cos, nsin, DH // 2).astype(BF16)
        dp_ref[:, 640:768] = _join(dv_ref, SWA_KV_HEADS).astype(BF16)

    hs = _head_spec(HEADS, tm, DH)
    hkv = _head_spec(SWA_KV_HEADS, tm, DH)
    t128 = pl.BlockSpec((tm, LANES), lambda i: (i, 0))
    return pl.pallas_call(
        body, name=name, grid=(L // tm,),
        in_specs=[hs, hkv, hkv, t128, t128, pl.BlockSpec(memory_space=pl.ANY)],
        out_specs=_col_spec(tm, 768, QKVC),
        out_shape=jax.ShapeDtypeStruct(dproj.shape, dproj.dtype),
        input_output_aliases={5: 0},
        compiler_params=_cparams(("parallel",)),
    )(dq, dk, dv, *tab, dproj)


def _prep_b(qbm, kvbm, proj, tab_q, tab_k, *, name):
    L = proj.shape[0]
    tm = ROW_T

    def body(q_ref, kv_ref, kr_ref, cq_ref, sq_ref, ck_ref, sk_ref, qo, ko, vo):
        q = _rope_lanes(q_ref[...], cq_ref[...], sq_ref[...], MLA_ROPE // 2) * SCALE_B
        _split(q, HEADS, DQK_B, qo)
        kr = _rope_lanes(kr_ref[...], ck_ref[...], sk_ref[...], MLA_ROPE // 2)[:, :MLA_ROPE].astype(BF16)
        kv = kv_ref[...]
        for h in range(HEADS):
            ko[h] = jnp.concatenate([kv[:, 128 * h:128 * h + MLA_NOPE].astype(BF16), kr], axis=1)
            vo[h] = kv[:, 128 * h + MLA_NOPE:128 * (h + 1)].astype(BF16)

    t128 = pl.BlockSpec((tm, LANES), lambda i: (i, 0))
    t768 = pl.BlockSpec((tm, 768), lambda i: (i, 0))
    return pl.pallas_call(
        body, name=name, grid=(L // tm,),
        in_specs=[t768, pl.BlockSpec((tm, 1024), lambda i: (i, 0)), _col_spec(tm, 128, KR), t768, t768, t128, t128],
        out_specs=[_head_spec(HEADS, tm, DQK_B), _head_spec(HEADS, tm, DQK_B), _head_spec(HEADS, tm, DH)],
        out_shape=[jax.ShapeDtypeStruct((HEADS, L, DQK_B), BF16), jax.ShapeDtypeStruct((HEADS, L, DQK_B), BF16),
                   jax.ShapeDtypeStruct((HEADS, L, DH), BF16)],
        compiler_params=_cparams(("parallel",)),
    )(qbm, kvbm, proj, *tab_q, *tab_k)


def _unprep_b(dq, dk, dv, tab_q, tab_k, dproj, *, name):
    L = dq.shape[1]
    tm = ROW_T

    def body(dq_ref, dk_ref, dv_ref, cq_ref, sq_ref, ck_ref, sk_ref, _, dp_kr, dqo, dkvo):
        dqv = _join(dq_ref, HEADS) * SCALE_B
        dqo[...] = _rope_lanes(dqv, cq_ref[...], -sq_ref[...], MLA_ROPE // 2).astype(BF16)
        parts = []
        dkr = None
        for h in range(HEADS):
            dkh = dk_ref[h]
            parts += [dkh[:, :MLA_NOPE], dv_ref[h]]
            r = dkh[:, MLA_NOPE:]
            dkr = r if dkr is None else dkr + r
        dkvo[...] = jnp.concatenate(parts, axis=1).astype(BF16)
        dkr = jnp.concatenate([dkr, jnp.zeros((tm, LANES - MLA_ROPE), F32)], axis=1)
        dp_kr[...] = _rope_lanes(dkr, ck_ref[...], -sk_ref[...], MLA_ROPE // 2).astype(BF16)

    t128 = pl.BlockSpec((tm, LANES), lambda i: (i, 0))
    t768 = pl.BlockSpec((tm, 768), lambda i: (i, 0))
    hq = _head_spec(HEADS, tm, DQK_B)
    return pl.pallas_call(
        body, name=name, grid=(L // tm,),
        in_specs=[hq, hq, _head_spec(HEADS, tm, DH), t768, t768, t128, t128, pl.BlockSpec(memory_space=pl.ANY)],
        out_specs=[_col_spec(tm, 128, KR), t768, pl.BlockSpec((tm, 1024), lambda i: (i, 0))],
        out_shape=[jax.ShapeDtypeStruct(dproj.shape, dproj.dtype), jax.ShapeDtypeStruct((L, 768), BF16),
                   jax.ShapeDtypeStruct((L, 1024), BF16)],
        input_output_aliases={7: 0},
        compiler_params=_cparams(("parallel",)),
    )(dq, dk, dv, *tab_q, *tab_k, dproj)


def _gate(y, proj, zcol, *, name):
    L = proj.shape[0]
    tm = ROW_T

    def body(y_ref, z_ref, u_ref):
        z = z_ref[...]
        u_ref[...] = (_join(y_ref, HEADS) * (z * jax.nn.sigmoid(z))).astype(BF16)

    return pl.pallas_call(
        body, name=name, grid=(L // tm,),
        in_specs=[_head_spec(HEADS, tm, DH), _col_spec(tm, 512, zcol)],
        out_specs=pl.BlockSpec((tm, 512), lambda i: (i, 0)),
        out_shape=jax.ShapeDtypeStruct((L, 512), BF16),
        compiler_params=_cparams(("parallel",)),
    )(y, proj)


def _gate_bwd(du, y, proj, zcol, dproj, *, name):
    L = proj.shape[0]
    tm = ROW_T

    def body(du_ref, y_ref, z_ref, _, dz_ref, dy_ref):
        z = z_ref[...]
        duv = du_ref[...]
        sg = jax.nn.sigmoid(z)
        _split(duv * (z * sg), HEADS, DH, dy_ref)
        dz_ref[...] = (duv * _join(y_ref, HEADS) * (sg * (1.0 + z * (1.0 - sg)))).astype(BF16)

    hs = _head_spec(HEADS, tm, DH)
    return pl.pallas_call(
        body, name=name, grid=(L // tm,),
        in_specs=[pl.BlockSpec((tm, 512), lambda i: (i, 0)), hs, _col_spec(tm, 512, zcol),
                  pl.BlockSpec(memory_space=pl.ANY)],
        out_specs=[_col_spec(tm, 512, zcol), hs],
        out_shape=[jax.ShapeDtypeStruct(dproj.shape, dproj.dtype), jax.ShapeDtypeStruct((HEADS, L, DH), F32)],
        input_output_aliases={3: 0},
        compiler_params=_cparams(("parallel",)),
    )(du, y, proj, dproj)


MERGE_T = 192


def _merge(proj, pbs, *, name):
    L = proj.shape[0]
    tm = MERGE_T

    def body(g0, g1, g2, p0, p1, p2, o_ref):
        acc = None
        for g_ref, p_ref in ((g0, p0), (g1, p1), (g2, p2)):
            t = jax.nn.sigmoid(g_ref[...]) * p_ref[...]
            acc = t if acc is None else acc + t
        o_ref[...] = acc.astype(BF16)

    row = pl.BlockSpec((tm, D_MODEL), lambda i: (i, 0))
    return pl.pallas_call(
        body, name=name, grid=(L // tm,),
        in_specs=[_col_spec(tm, D_MODEL, GATES + n * D_MODEL) for n in range(N_BRANCH)] + [row] * N_BRANCH,
        out_specs=row, out_shape=jax.ShapeDtypeStruct((L, D_MODEL), BF16),
        compiler_params=_cparams(("parallel",)),
    )(proj, proj, proj, *pbs)


def _merge_bwd(dmerged, proj, pbs, dproj, *, name):
    L = proj.shape[0]
    tm = MERGE_T

    def body(dm_ref, g_ref, p0, p1, p2, _, dg_ref, dp0, dp1, dp2):
        dm = dm_ref[...]
        for n, (p_ref, dp_ref) in enumerate(((p0, dp0), (p1, dp1), (p2, dp2))):
            cols = slice(n * D_MODEL, (n + 1) * D_MODEL)
            sg = jax.nn.sigmoid(g_ref[:, cols])
            dp_ref[...] = (dm * sg).astype(BF16)
            dg_ref[:, cols] = (dm * p_ref[...] * (sg * (1.0 - sg))).astype(BF16)

    row = pl.BlockSpec((tm, D_MODEL), lambda i: (i, 0))
    gates = _col_spec(tm, N_BRANCH * D_MODEL, GATES)
    outs = pl.pallas_call(
        body, name=name, grid=(L // tm,),
        in_specs=[row, gates] + [row] * N_BRANCH + [pl.BlockSpec(memory_space=pl.ANY)],
        out_specs=[gates] + [row] * N_BRANCH,
        out_shape=[jax.ShapeDtypeStruct(dproj.shape, dproj.dtype)]
        + [jax.ShapeDtypeStruct((L, D_MODEL), BF16)] * N_BRANCH,
        input_output_aliases={5: 0},
        compiler_params=_cparams(("parallel",)),
    )(dmerged, proj, *pbs, dproj)
    return outs[0], outs[1:]


def _forget_bias(af, b_f):
    return jnp.cumsum(jax.nn.log_sigmoid(af + b_f), axis=0).T


def _key_bias(L, T, ct=None):
    padb = jnp.where(jnp.arange(L) < PAD, BIG, 0.0).astype(F32)[None]
    kb = padb if ct is None else ct + padb
    return kb.reshape(kb.shape[0], L // T, 1, T)


def _layer_fwd(h, w, tabs, l, side=None):
    tag = f"l{l}"
    L = h.shape[0]
    hn = _rms_fwd(h, w["norm_g"], name=f"{tag}_rms_in")
    proj = _mm(hn, w["w_in"], name=f"{tag}_mm_in")
    ct, vjp_f = jax.vjp(_forget_bias, proj[:, FA:FA + HEADS], w["b_f"])
    ops_a = (*_prep_a(proj, name=f"{tag}_prep_a"), _key_bias(L, ATT_T, ct))
    ya, lsea, *side_out = _attn_fwd(*ops_a, T=ATT_T, side=side, name=f"{tag}_attn_a")
    cqn = _rms_fwd(proj, w["g_cq"], col=CQ, name=f"{tag}_rms_cq")
    ckvn = _rms_fwd(proj, w["g_ckv"], col=CKV, name=f"{tag}_rms_ckv")
    qbm = _mm(cqn, w["w_uq"], name=f"{tag}_mm_uq")
    kvbm = _mm(ckvn, w["w_ukv"], name=f"{tag}_mm_ukv")
    ops_b = (*_prep_b(qbm, kvbm, proj, tabs["bq"], tabs["bk"], name=f"{tag}_prep_b"), _key_bias(L, ATT_T))
    yb, lseb = _attn_fwd(*ops_b, T=ATT_T, name=f"{tag}_attn_b")
    ops_c = (*_prep_c(proj, tabs["c"], name=f"{tag}_prep_c"), _key_bias(L, WINDOW))
    sink = jnp.broadcast_to(w["sinks"][:, None, None], (HEADS, 1, LANES))
    yc, lsec = _attn_fwd(*ops_c, T=WINDOW, window=True, sink=sink, name=f"{tag}_attn_c")
    us = [_gate(y, proj, zcol, name=f"{tag}_gate{n}") for n, (y, zcol) in enumerate(((ya, ZA), (yb, ZB), (yc, ZC)))]
    pbr = [_mm(us[n], w["w_branch"][n], name=f"{tag}_mm_br{n}") for n in range(N_BRANCH)]
    merged = _merge(proj, pbr, name=f"{tag}_merge")
    out = _mm(merged, w["w_out"], add=h, name=f"{tag}_mm_out")
    saved = dict(h=h, hn=hn, proj=proj, vjp_f=vjp_f, ops_a=ops_a, ya=ya, lsea=lsea, cqn=cqn, ckvn=ckvn,
                 ops_b=ops_b, yb=yb, lseb=lseb, ops_c=ops_c, yc=yc, lsec=lsec, us=us, pbr=pbr, merged=merged)
    return out, saved, side_out


def _layer_bwd(dout, s, w, tabs, l, side=None):
    tag = f"l{l}"
    L = dout.shape[0]
    proj = s["proj"]
    g = {}
    dproj = jnp.zeros((L, NP_IN), BF16)
    dmerged = _mm(dout, w["w_out"], tb=True, name=f"{tag}_mm_out_dx")
    g["w_out"] = _mm(s["merged"], dout, ta=True, out_dtype=BF16, name=f"{tag}_mm_out_dw")
    dproj, dpbr = _merge_bwd(dmerged, proj, s["pbr"], dproj, name=f"{tag}_merge_bwd")
    dus = [_mm(dpbr[n], w["w_branch"][n], tb=True, name=f"{tag}_mm_br{n}_dx") for n in range(N_BRANCH)]
    g["w_branch"] = jnp.stack([_mm(s["us"][n], dpbr[n], ta=True, out_dtype=BF16, name=f"{tag}_mm_br{n}_dw")
                               for n in range(N_BRANCH)])
    dys = []
    for n, (y, zcol) in enumerate(((s["ya"], ZA), (s["yb"], ZB), (s["yc"], ZC))):
        dproj, dy = _gate_bwd(dus[n], y, proj, zcol, dproj, name=f"{tag}_gate{n}_bwd")
        dys.append(dy)
    dya, dyb, dyc = dys
    dqa, dka, dva, dcq, dck, *side_out = _attn_bwd(*s["ops_a"], s["ya"], s["lsea"], dya, T=ATT_T, fox=True,
                                                   side=side, name=f"{tag}_attn_a_bwd")
    dproj = _unprep_a(dqa, dka, dva, dproj, name=f"{tag}_unprep_a")
    daf, g["b_f"] = s["vjp_f"](dcq[:, :, 0] + dck.reshape(HEADS, L))
    dproj = lax.dynamic_update_slice(dproj, daf.astype(BF16), (0, FA))
    dqb, dkb, dvb, _, _ = _attn_bwd(*s["ops_b"], s["yb"], s["lseb"], dyb, T=ATT_T, name=f"{tag}_attn_b_bwd")
    dproj, dqbm, dkvbm = _unprep_b(dqb, dkb, dvb, tabs["bq"], tabs["bk"], dproj, name=f"{tag}_unprep_b")
    dcqn = _mm(dqbm, w["w_uq"], tb=True, name=f"{tag}_mm_uq_dx")
    g["w_uq"] = _mm(s["cqn"], dqbm, ta=True, out_dtype=BF16, name=f"{tag}_mm_uq_dw")
    dckvn = _mm(dkvbm, w["w_ukv"], tb=True, name=f"{tag}_mm_ukv_dx")
    g["w_ukv"] = _mm(s["ckvn"], dkvbm, ta=True, out_dtype=BF16, name=f"{tag}_mm_ukv_dw")
    dproj, g["g_cq"] = _rms_bwd(dcqn, proj, w["g_cq"], col=CQ, into=dproj, name=f"{tag}_rms_cq_bwd")
    dproj, g["g_ckv"] = _rms_bwd(dckvn, proj, w["g_ckv"], col=CKV, into=dproj, name=f"{tag}_rms_ckv_bwd")
    dqc, dkc, dvc = _attn_bwd_window(*s["ops_c"], s["yc"], s["lsec"], dyc, name=f"{tag}_attn_c_bwd")
    dproj = _unprep_c(dqc, dkc, dvc, tabs["c"], dproj, name=f"{tag}_unprep_c")
    delta_c = jnp.sum(dyc * s["yc"], axis=-1)
    g["sinks"] = -jnp.sum(jnp.exp(w["sinks"][:, None] - s["lsec"][:, :, 0]) * delta_c, axis=1)
    dhn = _mm(dproj, w["w_in"], tb=True, name=f"{tag}_mm_in_dx")
    g["w_in"] = _mm(s["hn"], dproj, ta=True, out_dtype=BF16, name=f"{tag}_mm_in_dw")
    dh, g["norm_g"] = _rms_bwd(dhn, s["h"], w["norm_g"], add=dout, name=f"{tag}_rms_in_bwd")
    return dh, g, side_out


def _cols_from_shards(g):
    return jnp.moveaxis(g, 0, 1).reshape(g.shape[1], N_DEV * g.shape[2])


def _cols_to_shards(w):
    R = w.shape[0]
    return jnp.moveaxis(w.reshape(R, N_DEV, w.shape[1] // N_DEV), 1, 0)


def _pad_in(w):
    parts, at = [], 0
    for lo, hi, dst in sorted(_RUNS, key=lambda r: r[2]):
        if dst > at:
            parts.append(jnp.zeros((w.shape[0], dst - at), w.dtype))
        parts.append(w[:, lo:hi])
        at = dst + hi - lo
    parts.append(jnp.zeros((w.shape[0], NP_IN - at), w.dtype))
    return jnp.concatenate(parts, axis=1)


def _unpad_in(w):
    return jnp.concatenate([w[:, dst:dst + hi - lo] for lo, hi, dst in _RUNS], axis=1)


_SMALL = (("norm_g", DEPTH * D_MODEL), ("b_f", DEPTH * HEADS), ("g_cq", DEPTH * MLA_QLORA),
          ("g_ckv", DEPTH * MLA_KVLORA), ("sinks", DEPTH * HEADS), ("final_g", D_MODEL), ("loss", 1),
          ("meta", N_META * D_MODEL))
SMALL_ROWS = 168


def _pack_small(d):
    parts = []
    for name, size in _SMALL:
        padded = -(-size // 128) * 128
        v = d[name].reshape(-1).astype(F32) if name in d else jnp.zeros((size,), F32)
        parts.append(jnp.pad(v, (0, padded - size)))
    flat = jnp.concatenate(parts)
    return jnp.pad(flat, (0, SMALL_ROWS * 128 - flat.shape[0])).reshape(SMALL_ROWS, 128)


def _unpack_small(p, shapes):
    flat = p.reshape(-1)
    out, at = {}, 0
    for name, size in _SMALL:
        if name in shapes:
            out[name] = flat[at:at + size].reshape(shapes[name])
        at += -(-size // 128) * 128
    return out


def kernel(x, meta_tokens, norm_g, w_in, b_f, g_cq, g_ckv, w_uq, w_ukv, sinks, w_branch, w_out, final_g, loss_target, m_meta_tokens, m_norm_g, m_w_in, m_b_f, m_g_cq, m_g_ckv, m_w_uq, m_w_ukv, m_sinks, m_w_branch, m_w_out, m_final_g, v_meta_tokens, v_norm_g, v_w_in, v_b_f, v_g_cq, v_g_ckv, v_w_uq, v_w_ukv, v_sinks, v_w_branch, v_w_out, v_final_g):
    S = x.shape[1]
    L = BLK + S
    cx, cy, cc = _me()
    my_idx = 4 * cx + 2 * cy + cc

    def shards(l):
        return [t[l].astype(BF16) for t in (w_in, w_uq, w_ukv, w_branch, w_out)]

    def layer_weights(l, gw_in, gw_uq, gw_ukv, gw_br, gw_out):
        return dict(
            norm_g=norm_g[l], b_f=b_f[l], g_cq=g_cq[l], g_ckv=g_ckv[l], sinks=sinks[l],
            w_in=_pad_in(_cols_from_shards(gw_in)), w_uq=_cols_from_shards(gw_uq), w_ukv=_cols_from_shards(gw_ukv),
            w_branch=jnp.stack([_cols_from_shards(gw_br[:, n]) for n in range(N_BRANCH)]),
            w_out=gw_out.reshape(D_MODEL, D_MODEL))

    *gw0, g_meta = _comm(_Gather(shards(0) + [meta_tokens]), name="gather_l0")
    layers = [layer_weights(0, *gw0), None]
    meta_full = _cols_from_shards(g_meta)

    h = jnp.concatenate([jnp.zeros((PAD, D_MODEL), F32), meta_full, x[0]], axis=0)
    q_lanes = jnp.arange(HEADS * DQK_B)
    tabs = dict(c=_rope_tables(L, DH // 2, LANES), bk=_rope_tables(L, MLA_ROPE // 2, LANES),
                bq=_rope_tables(L, MLA_ROPE // 2, HEADS * DQK_B, rope_lane=(q_lanes % DQK_B) >= MLA_NOPE))
    saved = [None] * DEPTH
    h, saved[0], gw1 = _layer_fwd(h, layers[0], tabs, 0, side=_Gather(shards(1)))
    layers[1] = layer_weights(1, *gw1)
    h, saved[1], _ = _layer_fwd(h, layers[1], tabs, 1)
    loss_vec, dh, g_final = _loss_head(h, final_g, loss_target[0], name="loss_head")

    def chunks(g):
        return [_cols_to_shards(_unpad_in(g["w_in"])), _cols_to_shards(g["w_uq"]), _cols_to_shards(g["w_ukv"]),
                jnp.stack([_cols_to_shards(g["w_branch"][n]) for n in range(N_BRANCH)], axis=1),
                g["w_out"].reshape(N_DEV, D_MODEL // N_DEV, D_MODEL)]

    grads = [None] * DEPTH
    dh, grads[1], _ = _layer_bwd(dh, saved[1], layers[1], tabs, 1)
    dh, grads[0], recv1 = _layer_bwd(dh, saved[0], layers[0], tabs, 0, side=_Exchange(chunks(grads[1])))
    recv0 = _comm(_Exchange(chunks(grads[0])), name="exchange_l0")
    r_in, r_uq, r_ukv, r_br, r_out = (jnp.stack([a, b], axis=1) for a, b in zip(recv0, recv1))

    def stack(name):
        return jnp.stack([grads[l][name] for l in range(DEPTH)])

    small = _pack_small(dict(norm_g=stack("norm_g"), b_f=stack("b_f"), g_cq=stack("g_cq"), g_ckv=stack("g_ckv"),
                             sinks=stack("sinks"), final_g=g_final, loss=loss_vec[0, 0:1],
                             meta=dh[PAD:BLK]))
    (g_small,) = _comm(_Gather([small]), name="gather_small")

    def adam_big(w_, parts, m_, v_, name):
        shape = w_.shape
        C = shape[-1]
        R = math.prod(shape[:-1])
        outs = _adamw(w_.reshape(R, C), parts.reshape(parts.shape[0], R, C), m_.reshape(R, C), v_.reshape(R, C),
                      name=name)
        return [o.reshape(shape) for o in outs]

    res = {}
    res["w_in"] = adam_big(w_in, r_in, m_w_in, v_w_in, "adam_w_in")
    res["w_uq"] = adam_big(w_uq, r_uq, m_w_uq, v_w_uq, "adam_w_uq")
    res["w_ukv"] = adam_big(w_ukv, r_ukv, m_w_ukv, v_w_ukv, "adam_w_ukv")
    res["w_branch"] = adam_big(w_branch, r_br, m_w_branch, v_w_branch, "adam_w_branch")
    res["w_out"] = adam_big(w_out, r_out, m_w_out, v_w_out, "adam_w_out")

    small_w = dict(norm_g=norm_g, b_f=b_f, g_cq=g_cq, g_ckv=g_ckv, sinks=sinks, final_g=final_g)
    small_m = dict(norm_g=m_norm_g, b_f=m_b_f, g_cq=m_g_cq, g_ckv=m_g_ckv, sinks=m_sinks, final_g=m_final_g)
    small_v = dict(norm_g=v_norm_g, b_f=v_b_f, g_cq=v_g_cq, g_ckv=v_g_ckv, sinks=v_sinks, final_g=v_final_g)
    sm = _adamw(_pack_small(small_w), g_small, _pack_small(small_m), _pack_small(small_v), name="adam_small")
    shapes = {k: a.shape for k, a in small_w.items()}
    shapes_all = dict(shapes, loss=(), meta=(N_META, D_MODEL))
    sm_g = _unpack_small(sm[0], shapes_all)
    sm_d, sm_m, sm_v = (_unpack_small(t, shapes) for t in sm[1:])
    for k in shapes:
        res[k] = [sm_g[k], sm_d[k], sm_m[k], sm_v[k]]
    g_meta_mine = lax.dynamic_slice(sm_g["meta"], (0, my_idx * 128), (N_META, 128))
    res["meta_tokens"] = _adamw(meta_tokens, g_meta_mine[None], m_meta_tokens, v_meta_tokens, name="adam_meta")

    order = ["meta_tokens", "norm_g", "w_in", "b_f", "g_cq", "g_ckv", "w_uq", "w_ukv", "sinks", "w_branch", "w_out",
             "final_g"]
    grad_x = dh[BLK:][None]
    return (sm_g["loss"], grad_x, *[res[k][0] for k in order], *[res[k][1] for k in order],
            *[res[k][2] for k in order], *[res[k][3] for k in order])
```

```python
import functools
import math

import jax
import jax.numpy as jnp
from jax import lax
from jax.experimental import pallas as pl
from jax.experimental.pallas import tpu as pltpu

F32 = jnp.float32
BF16 = jnp.bfloat16

D_MODEL = 1024
DEPTH = 2
N_META = 16
BLK = 128
PAD = BLK - N_META
ROPE_THETA = 10000.0
EPS = 1e-6
NEG = -1e30
BIG = 1e30
HEADS = 8
DH = 64
MLA_NOPE = 64
MLA_ROPE = 32
MLA_QLORA = 384
MLA_KVLORA = 256
SWA_KV_HEADS = 2
WINDOW = 128
BRANCH_W = 512
N_BRANCH = 3
N_IN = 7592

ADAM_LR = 0.001
ADAM_B1 = 0.9
ADAM_B2 = 0.999
ADAM_EPS = 1e-08
ADAM_WD = 0.01
ADAM_STEP = 10

N_DEV = 8
MESH = pl.DeviceIdType.MESH

NP_IN = 8192
QKVA, ZA, FA = 0, 1536, 2048
CKV, KR, CQ = 2304, 2560, 2688
GATES = 3072
ZB = 6144
QKVC, ZC = 6912, 7680
_RUNS = ((0, 1536, QKVA), (1536, 1544, FA), (1544, 2056, ZA), (2056, 2440, CQ), (2440, 2696, CKV), (2696, 2728, KR),
         (2728, 3240, ZB), (3240, 4008, QKVC), (4008, 4520, ZC), (4520, 7592, GATES))

VMEM_LIMIT = 48 * 1024 * 1024
ATT_T = 384
ATT_HP = 1
ROW_T = 384
LANES = 128


def _pick(dim, prefs):
    for p in prefs:
        if dim % p == 0:
            return p
    return dim


def _cparams(sem):
    return pltpu.CompilerParams(dimension_semantics=sem, vmem_limit_bytes=VMEM_LIMIT)


def _side_cparams(side, sem):
    if side is None:
        return _cparams(sem)
    return pltpu.CompilerParams(dimension_semantics=("arbitrary",) * len(sem), vmem_limit_bytes=VMEM_LIMIT,
                                has_side_effects=True)


def _mm(a, b, *, ta=False, tb=False, add=None, out_dtype=F32, side=None, name):
    M = a.shape[1] if ta else a.shape[0]
    K = a.shape[0] if ta else a.shape[1]
    N = b.shape[0] if tb else b.shape[1]
    assert K == (b.shape[1] if tb else b.shape[0])
    tm = _pick(M, (704, 1024, 512, 384, 256, 128))
    tn = _pick(N, (1024, 768, 512, 384, 256, 128))
    tk = _pick(K, (2048, 1408, 1024, 768, 512, 384, 256, 128))
    nk = K // tk
    dims = (((0 if ta else 1,), (1 if tb else 0,)), ((), ()))

    sn = 0 if side is None else side.n
    n_in = 2 + (add is not None)
    grid = (M // tm, N // tn, nk)

    def body(*refs):
        a_ref, b_ref = refs[:2]
        c_ref = refs[2] if add is not None else None
        o_ref = refs[n_in + sn]
        scratch = refs[n_in + 2 * sn + 1:]
        if side is not None:
            step = (pl.program_id(0) * grid[1] + pl.program_id(1)) * nk + pl.program_id(2)
            _ride(side, [*refs[n_in:n_in + sn], *refs[n_in + sn + 1:n_in + 2 * sn + 1], *scratch[nk > 1:]],
                  step, grid[0] * grid[1] * nk)
        r = lax.dot_general(a_ref[...].astype(BF16), b_ref[...].astype(BF16), dims, preferred_element_type=F32)

        def finish(total):
            if c_ref is not None:
                total = total + c_ref[...]
            o_ref[...] = total.astype(out_dtype)

        if nk == 1:
            finish(r)
        else:
            acc = scratch[0]
            k = pl.program_id(2)

            @pl.when(k == 0)
            def _():
                acc[...] = r

            @pl.when(k > 0)
            def _():
                acc[...] += r

            @pl.when(k == nk - 1)
            def _():
                finish(acc[...])

    a_spec = pl.BlockSpec((tk, tm), lambda i, j, k: (k, i)) if ta else pl.BlockSpec((tm, tk), lambda i, j, k: (i, k))
    b_spec = pl.BlockSpec((tn, tk), lambda i, j, k: (j, k)) if tb else pl.BlockSpec((tk, tn), lambda i, j, k: (k, j))
    o_spec = pl.BlockSpec((tm, tn), lambda i, j, k: (i, j))
    any_spec = pl.BlockSpec(memory_space=pl.ANY)
    outs = pl.pallas_call(
        body, name=name,
        grid=grid,
        in_specs=[a_spec, b_spec] + ([o_spec] if add is not None else []) + [any_spec] * sn,
        out_specs=[o_spec] + [any_spec] * sn,
        out_shape=[jax.ShapeDtypeStruct((M, N), out_dtype)] + ([] if side is None else side.out_shape),
        scratch_shapes=([pltpu.VMEM((tm, tn), F32)] if nk > 1 else []) + ([] if side is None else side.scratch),
        compiler_params=_side_cparams(side, ("parallel", "parallel", "arbitrary")),
    )(*((a, b) if add is None else (a, b, add)), *([] if side is None else side.arrs))
    return outs[0] if side is None else outs


def _col_spec(tm, width, col):
    assert col % width == 0
    return pl.BlockSpec((tm, width), lambda i, _c=col // width: (i, _c))


def _rms_fwd(x, g, *, col=0, name):
    L = x.shape[0]
    D = g.shape[0]
    tm = ROW_T

    def body(x_ref, g_ref, y_ref):
        xv = x_ref[...]
        rstd = lax.rsqrt(jnp.mean(xv * xv, axis=-1, keepdims=True) + EPS)
        y_ref[...] = (xv * rstd * g_ref[...]).astype(BF16)

    return pl.pallas_call(
        body, name=name, grid=(L // tm,),
        in_specs=[_col_spec(tm, D, col), pl.BlockSpec((1, D), lambda i: (0, 0))],
        out_specs=pl.BlockSpec((tm, D), lambda i: (i, 0)),
        out_shape=jax.ShapeDtypeStruct((L, D), BF16),
        compiler_params=_cparams(("parallel",)),
    )(x, g.reshape(1, D))


def _rms_bwd(dy, x, g, *, col=0, add=None, into=None, name):
    L = x.shape[0]
    D = g.shape[0]
    tm = ROW_T

    def body(*refs):
        dy_ref, x_ref, g_ref = refs[:3]
        add_ref = refs[3] if add is not None else None
        dx_ref, dg_ref = refs[-2:]
        i = pl.program_id(0)
        xv = x_ref[...]
        dyv = dy_ref[...]
        rstd = lax.rsqrt(jnp.mean(xv * xv, axis=-1, keepdims=True) + EPS)
        xhat = xv * rstd
        part = jnp.sum(dyv * xhat, axis=0, keepdims=True)

        @pl.when(i == 0)
        def _():
            dg_ref[...] = part

        @pl.when(i > 0)
        def _():
            dg_ref[...] += part

        dxh = dyv * g_ref[...]
        dx = rstd * (dxh - xhat * jnp.mean(dxh * xhat, axis=-1, keepdims=True))
        if add_ref is not None:
            dx = dx + add_ref[...]
        dx_ref[...] = dx.astype(dx_ref.dtype)

    row = pl.BlockSpec((tm, D), lambda i: (i, 0))
    in_specs = [row, _col_spec(tm, D, col), pl.BlockSpec((1, D), lambda i: (0, 0))]
    args = [dy, x, g.reshape(1, D)]
    aliases = {}
    if add is not None:
        in_specs.append(row)
        args.append(add)
    if into is not None:
        in_specs.append(pl.BlockSpec(memory_space=pl.ANY))
        args.append(into)
        aliases = {len(args) - 1: 0}
        dx_spec, dx_shape = _col_spec(tm, D, col), jax.ShapeDtypeStruct(into.shape, into.dtype)
    else:
        dx_spec, dx_shape = row, jax.ShapeDtypeStruct((L, D), F32)
    dx, dg = pl.pallas_call(
        body, name=name, grid=(L // tm,),
        in_specs=in_specs,
        out_specs=[dx_spec, pl.BlockSpec((1, D), lambda i: (0, 0))],
        out_shape=[dx_shape, jax.ShapeDtypeStruct((1, D), F32)],
        input_output_aliases=aliases,
        compiler_params=_cparams(("arbitrary",)),
    )(*args)
    return dx, dg.reshape(D)


def _loss_head(h, g, target, *, name):
    L, D = h.shape
    nb = L // BLK

    def body(h_ref, g_ref, t_ref, loss_ref, dh_ref, dg_ref):
        i = pl.program_id(0)

        @pl.when(i == 0)
        def _():
            loss_ref[...] = jnp.zeros_like(loss_ref)
            dg_ref[...] = jnp.zeros_like(dg_ref)
            dh_ref[...] = jnp.zeros_like(dh_ref)

        @pl.when(i > 0)
        def _():
            xv = h_ref[...]
            gv = g_ref[...]
            rstd = lax.rsqrt(jnp.mean(xv * xv, axis=-1, keepdims=True) + EPS)
            xhat = xv * rstd
            err = xhat * gv - t_ref[...]
            row = jnp.mean(err * err, axis=-1, keepdims=True)
            loss_ref[...] += 0.5 * jnp.sum(row, axis=0, keepdims=True)
            dy = err * (1.0 / D)
            dg_ref[...] += jnp.sum(dy * xhat, axis=0, keepdims=True)
            dxh = dy * gv
            dh_ref[...] = rstd * (dxh - xhat * jnp.mean(dxh * xhat, axis=-1, keepdims=True))

    loss, dh, dg = pl.pallas_call(
        body, name=name, grid=(nb,),
        in_specs=[pl.BlockSpec((BLK, D), lambda i: (i, 0)), pl.BlockSpec((1, D), lambda i: (0, 0)),
                  pl.BlockSpec((BLK, D), lambda i: (jnp.maximum(i - 1, 0), 0))],
        out_specs=[pl.BlockSpec((1, 128), lambda i: (0, 0)), pl.BlockSpec((BLK, D), lambda i: (i, 0)),
                   pl.BlockSpec((1, D), lambda i: (0, 0))],
        out_shape=[jax.ShapeDtypeStruct((1, 128), F32), jax.ShapeDtypeStruct((L, D), F32),
                   jax.ShapeDtypeStruct((1, D), F32)],
        compiler_params=_cparams(("arbitrary",)),
    )(h, g.reshape(1, D), target)
    return loss, dh, dg.reshape(D)


_NT = (((1,), (1,)), ((), ()))
_TN = (((0,), (0,)), ((), ()))


def _attn_fwd(q, k, v, kbias, *, T, window=False, sink=None, side=None, name):
    sn = 0 if side is None else side.n
    H, L, dk = q.shape
    Hkv = k.shape[0]
    dv = v.shape[2]
    G = H // Hkv
    nt = L // T
    Hb = kbias.shape[0]
    reps = T // LANES
    assert not window or T == WINDOW
    HP = G if G > 1 else ATT_HP
    NS = 1 if G > 1 else HP
    R = HP * T // NS
    HKV = HP // G
    HB = HP if Hb > 1 else 1
    assert G == 1 or Hb == 1

    def body(*refs):
        q_ref, k_ref, v_ref, kb_ref = refs[:4]
        n = 4
        if sink is not None:
            sk_ref = refs[n]
            n += 1
        side_in = refs[n:n + sn]
        n += sn
        o_ref, lse_ref = refs[n:n + 2]
        side_out = refs[n + 2:n + 2 + sn]
        n += 2 + sn
        m_scs, l_scs, acc_scs, buf_a, buf_b = (refs[n + t * NS:n + (t + 1) * NS] for t in range(5))
        i = pl.program_id(1)
        if side is not None:
            _ride(side, [*side_in, *side_out, *refs[n + 5 * NS:]], pl.program_id(0) * nt + i, (H // HP) * nt)
        for a in range(NS):
            if sink is not None:
                sk = [jnp.broadcast_to(sk_ref[b, :, 0:1], (T, LANES)) for b in range(HP)]
                m_scs[a][...] = jnp.concatenate(sk, axis=0) if G > 1 else sk[a]
                l_scs[a][...] = jnp.ones((R, LANES), F32)
            else:
                m_scs[a][...] = jnp.full((R, LANES), NEG, F32)
                l_scs[a][...] = jnp.zeros((R, LANES), F32)
            acc_scs[a][...] = jnp.zeros((R, dv), F32)
        row = lax.broadcasted_iota(jnp.int32, (R, T), 0) & (T - 1) if G > 1 else \
            lax.broadcasted_iota(jnp.int32, (R, T), 0)
        col = lax.broadcasted_iota(jnp.int32, (R, T), 1)

        def logits(a, j):
            rows = pl.ds(pl.multiple_of(j * T, T), T)
            qv = q_ref[...].reshape(R, dk) if G > 1 else q_ref[a]
            s = lax.dot_general(qv, k_ref[a, rows, :], _NT, preferred_element_type=F32)
            return s - kb_ref[a if HB > 1 else 0, j]

        def update(a, s, j, kind):
            rows = pl.ds(pl.multiple_of(j * T, T), T)
            m_sc, l_sc, acc_sc = m_scs[a], l_scs[a], acc_scs[a]
            if kind == "diag":
                s = jnp.where(row >= col, s, NEG)
            elif kind == "prev":
                s = jnp.where(col > row, s, NEG)
            m_prev = m_sc[...]
            m_new = jnp.maximum(m_prev, jnp.max(s, axis=1, keepdims=True))
            alpha = jnp.exp(m_prev - m_new)
            p = jnp.exp(s - jnp.tile(m_new, (1, reps)))
            l_sc[...] = alpha * l_sc[...] + jnp.sum(p, axis=1, keepdims=True)
            acc_sc[...] = alpha[:, :dv] * acc_sc[...] + jnp.dot(p.astype(BF16), v_ref[a, rows, :],
                                                                preferred_element_type=F32)
            m_sc[...] = m_new

        if window:
            @pl.when(i > 0)
            def _():
                for a in range(NS):
                    update(a, logits(a, i - 1), i - 1, "prev")

            for a in range(NS):
                update(a, logits(a, i), i, "diag")
        else:
            def fill(buf, j):
                for a in range(NS):
                    buf[a][...] = logits(a, j)

            def drain(buf, j, kind):
                for a in range(NS):
                    update(a, buf[a][...], j, kind)

            fill(buf_a, 0)

            def pair(t, c):
                fill(buf_b, 2 * t + 1)
                drain(buf_a, 2 * t, "full")
                fill(buf_a, 2 * t + 2)
                drain(buf_b, 2 * t + 1, "full")
                return c

            lax.fori_loop(0, i // 2, pair, 0)

            @pl.when(i % 2 == 1)
            def _():
                fill(buf_b, i)
                drain(buf_a, i - 1, "full")
                drain(buf_b, i, "diag")

            @pl.when(i % 2 == 0)
            def _():
                drain(buf_a, i, "diag")

        for a in range(NS):
            lv = l_scs[a][...]
            ov = acc_scs[a][...] / lv[:, :dv]
            lsev = (m_scs[a][...] + jnp.log(lv))[:, 0:1]
            if G > 1:
                o_ref[...] = ov.reshape(HP, T, dv)
                lse_ref[...] = lsev.reshape(HP, T, 1)
            else:
                o_ref[a] = ov
                lse_ref[a] = lsev

    in_specs = [pl.BlockSpec((HP, T, dk), lambda h, i: (h, i, 0)),
                pl.BlockSpec((HKV, L, dk), lambda h, i: (h, 0, 0)),
                pl.BlockSpec((HKV, L, dv), lambda h, i: (h, 0, 0)),
                pl.BlockSpec((HB, nt, 1, T), lambda h, i: (h if Hb > 1 else 0, 0, 0, 0))]
    args = [q, k, v, kbias]
    if sink is not None:
        in_specs += [pl.BlockSpec((HP, 1, LANES), lambda h, i: (h, 0, 0))]
        args += [sink]
    any_spec = pl.BlockSpec(memory_space=pl.ANY)
    return pl.pallas_call(
        body, name=name, grid=(H // HP, nt),
        in_specs=in_specs + [any_spec] * sn,
        out_specs=[pl.BlockSpec((HP, T, dv), lambda h, i: (h, i, 0)),
                   pl.BlockSpec((HP, T, 1), lambda h, i: (h, i, 0))] + [any_spec] * sn,
        out_shape=[jax.ShapeDtypeStruct((H, L, dv), F32), jax.ShapeDtypeStruct((H, L, 1), F32)]
        + ([] if side is None else side.out_shape),
        scratch_shapes=[pltpu.VMEM((R, LANES), F32)] * (2 * NS) + [pltpu.VMEM((R, dv), F32)] * NS
        + [pltpu.VMEM((R, T), F32)] * (2 * NS) + ([] if side is None else side.scratch),
        compiler_params=_side_cparams(side, ("parallel", "arbitrary")),
    )(*args, *([] if side is None else side.arrs))


def _attn_bwd(q, k, v, kbias, o, lse, do, *, T, fox=False, side=None, name):
    sn = 0 if side is None else side.n
    H, L, dk = q.shape
    Hkv = k.shape[0]
    dv = v.shape[2]
    G = H // Hkv
    nt = L // T
    Hb = kbias.shape[0]

    def body(*refs):
        q_ref, k_ref, v_ref, kb_ref, o_ref, lse_ref, do_ref = refs[:7]
        side_in = refs[7:7 + sn]
        n = 7 + sn
        dq_ref, dk_ref, dv_ref = refs[n:n + 3]
        n += 3
        if fox:
            dcq_ref, dck_ref = refs[n:n + 2]
            n += 2
        side_out = refs[n:n + sn]
        n += sn
        delta_sc = refs[n]
        buf_a, buf_b = refs[n + 1:n + 3], refs[n + 3:n + 5]
        j = pl.program_id(1)
        if side is not None:
            _ride(side, [*side_in, *side_out, *refs[n + 5:]], pl.program_id(0) * nt + j, H * nt)

        @pl.when(j == 0)
        def _():
            dq_ref[...] = jnp.zeros_like(dq_ref)
            if fox:
                dcq_ref[...] = jnp.zeros_like(dcq_ref)

            def dl(i, c):
                rows = pl.ds(pl.multiple_of(i * T, T), T)
                delta_sc[rows, :] = jnp.sum(do_ref[0, rows, :] * o_ref[0, rows, :], axis=1, keepdims=True)
                return c

            lax.fori_loop(0, nt, dl, 0)

        dk_ref[...] = jnp.zeros_like(dk_ref)
        dv_ref[...] = jnp.zeros_like(dv_ref)
        if fox:
            dck_ref[...] = jnp.zeros_like(dck_ref)
        kb = k_ref[0]
        vb = v_ref[0]
        kbias_j = kb_ref[0, j]
        row = lax.broadcasted_iota(jnp.int32, (T, T), 0)
        col = lax.broadcasted_iota(jnp.int32, (T, T), 1)

        def fill(buf, i):
            rows = pl.ds(pl.multiple_of(i * T, T), T)
            buf[0][...] = lax.dot_general(q_ref[0, rows, :], kb, _NT, preferred_element_type=F32) - kbias_j
            buf[1][...] = lax.dot_general(do_ref[0, rows, :].astype(BF16), vb, _NT, preferred_element_type=F32)

        def drain(buf, i, kind):
            rows = pl.ds(pl.multiple_of(i * T, T), T)
            qb = q_ref[0, rows, :]
            dob = do_ref[0, rows, :].astype(BF16)
            s = buf[0][...]
            if kind == "diag":
                s = jnp.where(row >= col, s, NEG)
            p = jnp.exp(s - lse_ref[0, rows, :])
            ds = p * (buf[1][...] - delta_sc[rows, :])
            dsb = ds.astype(BF16)
            dv_ref[0] += lax.dot_general(p.astype(BF16), dob, _TN, preferred_element_type=F32)
            dk_ref[0] += lax.dot_general(dsb, qb, _TN, preferred_element_type=F32)
            dq_ref[0, rows, :] += jnp.dot(dsb, kb, preferred_element_type=F32)
            if fox:
                dcq_ref[0, rows, :] += jnp.sum(ds, axis=1, keepdims=True)
                dck_ref[0, 0] += -jnp.sum(ds, axis=0, keepdims=True)

        fill(buf_a, j)
        drain(buf_a, j, "diag")
        first = j + 1
        rest = nt - first

        @pl.when(rest > 0)
        def _():
            fill(buf_a, first)

        def pair(t, c):
            i0 = first + 2 * t
            fill(buf_b, i0 + 1)
            drain(buf_a, i0, "full")
            fill(buf_a, jnp.minimum(i0 + 2, nt - 1))
            drain(buf_b, i0 + 1, "full")
            return c

        lax.fori_loop(0, rest // 2, pair, 0)

        @pl.when(rest % 2 == 1)
        def _():
            drain(buf_a, nt - 1, "full")

    in_specs = [pl.BlockSpec((1, L, dk), lambda h, j: (h, 0, 0)),
                pl.BlockSpec((1, T, dk), lambda h, j: (h // G, j, 0)),
                pl.BlockSpec((1, T, dv), lambda h, j: (h // G, j, 0)),
                pl.BlockSpec((1, nt, 1, T), lambda h, j: (h if Hb > 1 else 0, 0, 0, 0)),
                pl.BlockSpec((1, L, dv), lambda h, j: (h, 0, 0)),
                pl.BlockSpec((1, L, 1), lambda h, j: (h, 0, 0)),
                pl.BlockSpec((1, L, dv), lambda h, j: (h, 0, 0))]
    out_specs = [pl.BlockSpec((1, L, dk), lambda h, j: (h, 0, 0)),
                 pl.BlockSpec((1, T, dk), lambda h, j: (h, j, 0)),
                 pl.BlockSpec((1, T, dv), lambda h, j: (h, j, 0))]
    out_shape = [jax.ShapeDtypeStruct((H, L, dk), F32), jax.ShapeDtypeStruct((H, L, dk), F32),
                 jax.ShapeDtypeStruct((H, L, dv), F32)]
    if fox:
        out_specs += [pl.BlockSpec((1, L, 1), lambda h, j: (h, 0, 0)),
                      pl.BlockSpec((1, 1, 1, T), lambda h, j: (h, j, 0, 0))]
        out_shape += [jax.ShapeDtypeStruct((H, L, 1), F32), jax.ShapeDtypeStruct((H, nt, 1, T), F32)]
    any_spec = pl.BlockSpec(memory_space=pl.ANY)
    outs = pl.pallas_call(
        body, name=name, grid=(H, nt),
        in_specs=in_specs + [any_spec] * sn, out_specs=out_specs + [any_spec] * sn,
        out_shape=out_shape + ([] if side is None else side.out_shape),
        scratch_shapes=[pltpu.VMEM((L, 1), F32)] + [pltpu.VMEM((T, T), F32)] * 4
        + ([] if side is None else side.scratch),
        compiler_params=_side_cparams(side, ("parallel", "arbitrary")),
    )(q, k, v, kbias, o, lse, do, *([] if side is None else side.arrs))
    main, rest = outs[:len(outs) - sn], outs[len(outs) - sn:]
    return (*(main if fox else (*main, None, None)), *rest)


def _attn_bwd_window(q, k, v, kbias, o, lse, do, *, name):
    H, L, dk = q.shape
    Hkv = k.shape[0]
    dv = v.shape[2]
    G = H // Hkv
    T = WINDOW
    nt = L // T
    R = G * T

    def body(q_ref, kc_ref, kp_ref, vc_ref, vp_ref, kb_ref, o_ref, lse_ref, do_ref, dq_ref, dk_ref, dv_ref,
             dk_sc, dv_sc):
        i = pl.program_id(1)
        row = lax.broadcasted_iota(jnp.int32, (R, T), 0) & (T - 1)
        col = lax.broadcasted_iota(jnp.int32, (R, T), 1)

        @pl.when(i == 0)
        def _():
            dk_sc[...] = jnp.zeros_like(dk_sc)
            dv_sc[...] = jnp.zeros_like(dv_sc)

        @pl.when(i == nt)
        def _():
            dk_ref[0] = dk_sc[...]
            dv_ref[0] = dv_sc[...]

        @pl.when(i < nt)
        def _():
            qb = q_ref[...].reshape(R, dk)
            dof = do_ref[...].reshape(R, dv)
            dob = dof.astype(BF16)
            lse_c = lse_ref[...].reshape(R, 1)
            delta = jnp.sum(dof * o_ref[...].reshape(R, dv), axis=1, keepdims=True)

            def grads(kt, vt, kbias_j, mask):
                s = lax.dot_general(qb, kt, _NT, preferred_element_type=F32) - kbias_j
                p = jnp.exp(jnp.where(mask, s, NEG) - lse_c)
                dp = lax.dot_general(dob, vt, _NT, preferred_element_type=F32)
                ds = (p * (dp - delta)).astype(BF16)
                return (jnp.dot(ds, kt, preferred_element_type=F32),
                        lax.dot_general(ds, qb, _TN, preferred_element_type=F32),
                        lax.dot_general(p.astype(BF16), dob, _TN, preferred_element_type=F32))

            ip = jnp.maximum(i - 1, 0)
            dq_p, dk_p, dv_p = grads(kp_ref[0], vp_ref[0], kb_ref[0, ip], (col > row) & (i > 0))
            dq_c, dk_c, dv_c = grads(kc_ref[0], vc_ref[0], kb_ref[0, i], row >= col)
            dq_ref[...] = (dq_p + dq_c).reshape(G, T, dk)
            dk_ref[0] = dk_sc[...] + dk_p
            dv_ref[0] = dv_sc[...] + dv_p
            dk_sc[...] = dk_c
            dv_sc[...] = dv_c

    def cur(i):
        return jnp.minimum(i, nt - 1)

    def prev(i):
        return jnp.maximum(jnp.minimum(i, nt - 1) - 1, 0)

    def written(i):
        return jnp.maximum(i - 1, 0)

    qs = lambda d: pl.BlockSpec((G, T, d), lambda h, i: (h, cur(i), 0))
    return pl.pallas_call(
        body, name=name, grid=(Hkv, nt + 1),
        in_specs=[qs(dk),
                  pl.BlockSpec((1, T, dk), lambda h, i: (h, cur(i), 0)),
                  pl.BlockSpec((1, T, dk), lambda h, i: (h, prev(i), 0)),
                  pl.BlockSpec((1, T, dv), lambda h, i: (h, cur(i), 0)),
                  pl.BlockSpec((1, T, dv), lambda h, i: (h, prev(i), 0)),
                  pl.BlockSpec((1, nt, 1, T), lambda h, i: (0, 0, 0, 0)),
                  qs(dv), qs(1), qs(dv)],
        out_specs=[qs(dk),
                   pl.BlockSpec((1, T, dk), lambda h, i: (h, written(i), 0)),
                   pl.BlockSpec((1, T, dv), lambda h, i: (h, written(i), 0))],
        out_shape=[jax.ShapeDtypeStruct((H, L, dk), F32), jax.ShapeDtypeStruct((Hkv, L, dk), F32),
                   jax.ShapeDtypeStruct((Hkv, L, dv), F32)],
        scratch_shapes=[pltpu.VMEM((T, dk), F32), pltpu.VMEM((T, dv), F32)],
        compiler_params=_cparams(("parallel", "arbitrary")),
    )(q, k, k, v, v, kbias, o, lse, do)


def _adamw(w, gparts, m, v, *, name):
    n, R, C = gparts.shape
    tr = _pick(R, (128, 64, 32, 16, 8))
    c1 = 1.0 - ADAM_B1 ** ADAM_STEP
    c2 = 1.0 - ADAM_B2 ** ADAM_STEP

    def body(w_ref, g_ref, m_ref, v_ref, go_ref, d_ref, mo_ref, vo_ref):
        g = g_ref[0].astype(F32)
        for t in range(1, n):
            g = g + g_ref[t].astype(F32)
        mn = ADAM_B1 * m_ref[...] + (1.0 - ADAM_B1) * g
        vn = ADAM_B2 * v_ref[...] + (1.0 - ADAM_B2) * (g * g)
        go_ref[...] = g
        mo_ref[...] = mn
        vo_ref[...] = vn
        d_ref[...] = -ADAM_LR * ((mn / c1) / (jnp.sqrt(vn / c2) + ADAM_EPS) + ADAM_WD * w_ref[...])

    spec = pl.BlockSpec((tr, C), lambda i: (i, 0))
    return pl.pallas_call(
        body, name=name, grid=(R // tr,),
        in_specs=[spec, pl.BlockSpec((n, tr, C), lambda i: (0, i, 0)), spec, spec],
        out_specs=[spec] * 4,
        out_shape=[jax.ShapeDtypeStruct((R, C), F32)] * 4,
        compiler_params=_cparams(("parallel",)),
    )(w, gparts, m, v)


def _me():
    return lax.axis_index("x"), lax.axis_index("y"), lax.axis_index("c")


class _CommJob:
    def __init__(self, arrs):
        self.arrs = list(arrs)
        self.n = len(arrs)
        self.scratch = [pltpu.SemaphoreType.DMA((self.n, 7)), pltpu.SemaphoreType.DMA((self.n, 7)),
                        pltpu.SemaphoreType.DMA((self.n,))]

    def bind(self, refs):
        n = self.n
        self.ins, self.outs = refs[:n], refs[n:2 * n]
        self.send_sems, self.recv_sems, self.local_sems = refs[2 * n:2 * n + 3]

    def middle(self):
        pass


class _Gather(_CommJob):
    def __init__(self, arrs):
        super().__init__(arrs)
        self.out_shape = [jax.ShapeDtypeStruct((N_DEV, *a.shape), a.dtype) for a in arrs]

    def _where(self):
        x, y, c = _me()
        return (x, y, c), (x, y, 1 - c), [(1 - x, y), (x, 1 - y), (1 - x, 1 - y)], c

    def _copy(self, t, k, block, to, src=None):
        dst = self.outs[t].at[4 * block[0] + 2 * block[1] + block[2]]
        return pltpu.make_async_remote_copy(
            src_ref=dst if src is None else src, dst_ref=dst,
            send_sem=self.send_sems.at[t, k], recv_sem=self.recv_sems.at[t, k],
            device_id=to, device_id_type=MESH)

    def _mine(self, t, me):
        return pltpu.make_async_copy(self.ins[t], self.outs[t].at[4 * me[0] + 2 * me[1] + me[2]],
                                     self.local_sems.at[t])

    def _first(self, t, me, sibling, chips, c):
        return [self._copy(t, 0, me, sibling, src=self.ins[t])] + \
               [self._copy(t, 1 + j, me, (*chip, c), src=self.ins[t]) for j, chip in enumerate(chips)]

    def start(self):
        me, sibling, chips, c = self._where()
        for t in range(self.n):
            self._mine(t, me).start()
            for cp in self._first(t, me, sibling, chips, c):
                cp.start()

    def middle(self):
        me, sibling, chips, c = self._where()
        for j, chip in enumerate(chips):
            for t in range(self.n):
                self._copy(t, 1 + j, (*chip, c), me).wait_recv()
                self._copy(t, 4 + j, (*chip, c), sibling).start()

    def finish(self):
        me, sibling, chips, c = self._where()
        for t in range(self.n):
            self._copy(t, 0, sibling, me).wait_recv()
            for j, chip in enumerate(chips):
                self._copy(t, 4 + j, (*chip, 1 - c), me).wait_recv()
        for t in range(self.n):
            for cp in self._first(t, me, sibling, chips, c):
                cp.wait_send()
            for j, chip in enumerate(chips):
                self._copy(t, 4 + j, (*chip, c), sibling).wait_send()
            self._mine(t, me).wait()


class _Exchange(_CommJob):
    def __init__(self, arrs):
        super().__init__(arrs)
        self.out_shape = [jax.ShapeDtypeStruct(a.shape, a.dtype) for a in arrs]

    def _copies(self, t):
        x, y, c = _me()
        my_idx = 4 * x + 2 * y + c
        pairs = []
        for k in range(1, N_DEV):
            peer = (x ^ ((k >> 2) & 1), y ^ ((k >> 1) & 1), c ^ (k & 1))
            peer_idx = 4 * peer[0] + 2 * peer[1] + peer[2]
            sems = dict(send_sem=self.send_sems.at[t, k - 1], recv_sem=self.recv_sems.at[t, k - 1],
                        device_id=peer, device_id_type=MESH)
            pairs.append((pltpu.make_async_remote_copy(src_ref=self.ins[t].at[peer_idx],
                                                       dst_ref=self.outs[t].at[my_idx], **sems),
                          pltpu.make_async_remote_copy(src_ref=self.ins[t].at[peer_idx],
                                                       dst_ref=self.outs[t].at[peer_idx], **sems)))
        return pairs

    def _mine(self, t):
        x, y, c = _me()
        my_idx = 4 * x + 2 * y + c
        return pltpu.make_async_copy(self.ins[t].at[my_idx], self.outs[t].at[my_idx], self.local_sems.at[t])

    def start(self):
        for t in range(self.n):
            self._mine(t).start()
            for snd, _ in self._copies(t):
                snd.start()

    def finish(self):
        for t in range(self.n):
            pairs = self._copies(t)
            for _, rcv in pairs:
                rcv.wait_recv()
            for snd, _ in pairs:
                snd.wait_send()
            self._mine(t).wait()


def _comm(job, *, name):
    def body(*refs):
        job.bind(refs)
        job.start()
        job.middle()
        job.finish()

    any_spec = pl.BlockSpec(memory_space=pl.ANY)
    return pl.pallas_call(
        body, name=name,
        in_specs=[any_spec] * job.n, out_specs=[any_spec] * job.n,
        out_shape=job.out_shape, scratch_shapes=job.scratch,
        compiler_params=pltpu.CompilerParams(has_side_effects=True),
    )(*job.arrs)


def _ride(job, refs, step, total):
    job.bind(refs)

    @pl.when(step == 0)
    def _():
        job.start()

    @pl.when(step == (total * 3) // 5)
    def _():
        job.middle()

    @pl.when(step == total - 1)
    def _():
        job.finish()


SCALE_A = DH ** -0.5
SCALE_B = (MLA_NOPE + MLA_ROPE) ** -0.5
SCALE_C = DH ** -0.5
DQK_B = MLA_NOPE + MLA_ROPE


def _rope_tables(L, half, width, rope_lane=None):
    pos = (jnp.arange(L) - PAD).astype(F32)
    inv = ROPE_THETA ** (-jnp.arange(half, dtype=F32) / half)
    lane = jnp.arange(width)
    ang = pos[:, None] * inv[lane % half][None, :]
    sign = jnp.where(lane % (2 * half) < half, -1.0, 1.0).astype(F32)
    cos, sin = jnp.cos(ang), jnp.sin(ang) * sign[None, :]
    if rope_lane is not None:
        cos = jnp.where(rope_lane[None, :], cos, 1.0)
        sin = jnp.where(rope_lane[None, :], sin, 0.0)
    return cos, sin


def _rope_lanes(x, cos, sin, half):
    W = x.shape[1]
    lane = lax.broadcasted_iota(jnp.int32, x.shape, 1)
    first = (lane & (2 * half - 1)) < half
    partner = jnp.where(first, pltpu.roll(x, W - half, 1), pltpu.roll(x, half, 1))
    return x * cos + partner * sin


def _tile_lanes(t, width):
    return t if t.shape[1] == width else jnp.tile(t, (1, width // t.shape[1]))


def _split(x, H, d, dst):
    for h in range(H):
        dst[h] = x[:, d * h:d * (h + 1)].astype(dst.dtype)


def _join(src, H):
    return jnp.concatenate([src[h] for h in range(H)], axis=1)


def _head_spec(H, tm, d):
    return pl.BlockSpec((H, tm, d), lambda i: (0, i, 0))


def _prep_a(proj, *, name):
    L = proj.shape[0]
    tm = ROW_T

    def body(x_ref, qo, ko, vo):
        _split(x_ref[:, 0:512] * SCALE_A, HEADS, DH, qo)
        _split(x_ref[:, 512:1024], HEADS, DH, ko)
        _split(x_ref[:, 1024:1536], HEADS, DH, vo)

    return pl.pallas_call(
        body, name=name, grid=(L // tm,),
        in_specs=[_col_spec(tm, 1536, QKVA)],
        out_specs=[_head_spec(HEADS, tm, DH)] * 3,
        out_shape=[jax.ShapeDtypeStruct((HEADS, L, DH), BF16)] * 3,
        compiler_params=_cparams(("parallel",)),
    )(proj)


def _unprep_a(dq, dk, dv, dproj, *, name):
    L = dq.shape[1]
    tm = ROW_T

    def body(dq_ref, dk_ref, dv_ref, _, dp_ref):
        dp_ref[:, 0:512] = (_join(dq_ref, HEADS) * SCALE_A).astype(BF16)
        dp_ref[:, 512:1024] = _join(dk_ref, HEADS).astype(BF16)
        dp_ref[:, 1024:1536] = _join(dv_ref, HEADS).astype(BF16)

    hs = _head_spec(HEADS, tm, DH)
    return pl.pallas_call(
        body, name=name, grid=(L // tm,),
        in_specs=[hs, hs, hs, pl.BlockSpec(memory_space=pl.ANY)],
        out_specs=_col_spec(tm, 1536, QKVA),
        out_shape=jax.ShapeDtypeStruct(dproj.shape, dproj.dtype),
        input_output_aliases={3: 0},
        compiler_params=_cparams(("parallel",)),
    )(dq, dk, dv, dproj)


def _prep_c(proj, tab, *, name):
    L = proj.shape[0]
    tm = ROW_T

    def body(x_ref, cos_ref, sin_ref, qo, ko, vo):
        cos, sin = cos_ref[...], sin_ref[...]
        q = _rope_lanes(x_ref[:, 0:512], _tile_lanes(cos, 512), _tile_lanes(sin, 512), DH // 2)
        _split(q * SCALE_C, HEADS, DH, qo)
        _split(_rope_lanes(x_ref[:, 512:640], cos, sin, DH // 2), SWA_KV_HEADS, DH, ko)
        _split(x_ref[:, 640:768], SWA_KV_HEADS, DH, vo)

    t128 = pl.BlockSpec((tm, LANES), lambda i: (i, 0))
    return pl.pallas_call(
        body, name=name, grid=(L // tm,),
        in_specs=[_col_spec(tm, 768, QKVC), t128, t128],
        out_specs=[_head_spec(HEADS, tm, DH), _head_spec(SWA_KV_HEADS, tm, DH), _head_spec(SWA_KV_HEADS, tm, DH)],
        out_shape=[jax.ShapeDtypeStruct((HEADS, L, DH), BF16), jax.ShapeDtypeStruct((SWA_KV_HEADS, L, DH), BF16),
                   jax.ShapeDtypeStruct((SWA_KV_HEADS, L, DH), BF16)],
        compiler_params=_cparams(("parallel",)),
    )(proj, *tab)


def _unprep_c(dq, dk, dv, tab, dproj, *, name):
    L = dq.shape[1]
    tm = ROW_T

    def body(dq_ref, dk_ref, dv_ref, cos_ref, sin_ref, _, dp_ref):
        cos, nsin = cos_ref[...], -sin_ref[...]
        dqv = _join(dq_ref, HEADS) * SCALE_C
        dp_ref[:, 0:512] = _rope_lanes(dqv, _tile_lanes(cos, 512), _tile_lanes(nsin, 512), DH // 2).astype(BF16)
        dp_ref[:, 512:640] = _rope_lanes(_join(dk_ref, SWA_KV_HEADS), cos, nsin, DH // 2).astype(BF16)
        dp_ref[:, 640:768] = _join(dv_ref, SWA_KV_HEADS).astype(BF16)

    hs = _head_spec(HEADS, tm, DH)
    hkv = _head_spec(SWA_KV_HEADS, tm, DH)
    t128 = pl.BlockSpec((tm, LANES), lambda i: (i, 0))
    return pl.pallas_call(
        body, name=name, grid=(L // tm,),
        in_specs=[hs, hkv, hkv, t128, t128, pl.BlockSpec(memory_space=pl.ANY)],
        out_specs=_col_spec(tm, 768, QKVC),
        out_shape=jax.ShapeDtypeStruct(dproj.shape, dproj.dtype),
        input_output_aliases={5: 0},
        compiler_params=_cparams(("parallel",)),
    )(dq, dk, dv, *tab, dproj)


def _prep_b(qbm, kvbm, proj, tab_q, tab_k, *, name):
    L = proj.shape[0]
    tm = ROW_T

    def body(q_ref, kv_ref, kr_ref, cq_ref, sq_ref, ck_ref, sk_ref, qo, ko, vo):
        q = _rope_lanes(q_ref[...], cq_ref[...], sq_ref[...], MLA_ROPE // 2) * SCALE_B
        _split(q, HEADS, DQK_B, qo)
        kr = _rope_lanes(kr_ref[...], ck_ref[...], sk_ref[...], MLA_ROPE // 2)[:, :MLA_ROPE].astype(BF16)
        kv = kv_ref[...]
        for h in range(HEADS):
            ko[h] = jnp.concatenate([kv[:, 128 * h:128 * h + MLA_NOPE].astype(BF16), kr], axis=1)
            vo[h] = kv[:, 128 * h + MLA_NOPE:128 * (h + 1)].astype(BF16)

    t128 = pl.BlockSpec((tm, LANES), lambda i: (i, 0))
    t768 = pl.BlockSpec((tm, 768), lambda i: (i, 0))
    return pl.pallas_call(
        body, name=name, grid=(L // tm,),
        in_specs=[t768, pl.BlockSpec((tm, 1024), lambda i: (i, 0)), _col_spec(tm, 128, KR), t768, t768, t128, t128],
        out_specs=[_head_spec(HEADS, tm, DQK_B), _head_spec(HEADS, tm, DQK_B), _head_spec(HEADS, tm, DH)],
        out_shape=[jax.ShapeDtypeStruct((HEADS, L, DQK_B), BF16), jax.ShapeDtypeStruct((HEADS, L, DQK_B), BF16),
                   jax.ShapeDtypeStruct((HEADS, L, DH), BF16)],
        compiler_params=_cparams(("parallel",)),
    )(qbm, kvbm, proj, *tab_q, *tab_k)


def _unprep_b(dq, dk, dv, tab_q, tab_k, dproj, *, name):
    L = dq.shape[1]
    tm = ROW_T

    def body(dq_ref, dk_ref, dv_ref, cq_ref, sq_ref, ck_ref, sk_ref, _, dp_kr, dqo, dkvo):
        dqv = _join(dq_ref, HEADS) * SCALE_B
        dqo[...] = _rope_lanes(dqv, cq_ref[...], -sq_ref[...], MLA_ROPE // 2).astype(BF16)
        parts = []
        dkr = None
        for h in range(HEADS):
            dkh = dk_ref[h]
            parts += [dkh[:, :MLA_NOPE], dv_ref[h]]
            r = dkh[:, MLA_NOPE:]
            dkr = r if dkr is None else dkr + r
        dkvo[...] = jnp.concatenate(parts, axis=1).astype(BF16)
        dkr = jnp.concatenate([dkr, jnp.zeros((tm, LANES - MLA_ROPE), F32)], axis=1)
        dp_kr[...] = _rope_lanes(dkr, ck_ref[...], -sk_ref[...], MLA_ROPE // 2).astype(BF16)

    t128 = pl.BlockSpec((tm, LANES), lambda i: (i, 0))
    t768 = pl.BlockSpec((tm, 768), lambda i: (i, 0))
    hq = _head_spec(HEADS, tm, DQK_B)
    return pl.pallas_call(
        body, name=name, grid=(L // tm,),
        in_specs=[hq, hq, _head_spec(HEADS, tm, DH), t768, t768, t128, t128, pl.BlockSpec(memory_space=pl.ANY)],
        out_specs=[_col_spec(tm, 128, KR), t768, pl.BlockSpec((tm, 1024), lambda i: (i, 0))],
        out_shape=[jax.ShapeDtypeStruct(dproj.shape, dproj.dtype), jax.ShapeDtypeStruct((L, 768), BF16),
                   jax.ShapeDtypeStruct((L, 1024), BF16)],
        input_output_aliases={7: 0},
        compiler_params=_cparams(("parallel",)),
    )(dq, dk, dv, *tab_q, *tab_k, dproj)


def _gate(y, proj, zcol, *, name):
    L = proj.shape[0]
    tm = ROW_T

    def body(y_ref, z_ref, u_ref):
        z = z_ref[...]
        u_ref[...] = (_join(y_ref, HEADS) * (z * jax.nn.sigmoid(z))).astype(BF16)

    return pl.pallas_call(
        body, name=name, grid=(L // tm,),
        in_specs=[_head_spec(HEADS, tm, DH), _col_spec(tm, 512, zcol)],
        out_specs=pl.BlockSpec((tm, 512), lambda i: (i, 0)),
        out_shape=jax.ShapeDtypeStruct((L, 512), BF16),
        compiler_params=_cparams(("parallel",)),
    )(y, proj)


def _gate_bwd(du, y, proj, zcol, dproj, *, name):
    L = proj.shape[0]
    tm = ROW_T

    def body(du_ref, y_ref, z_ref, _, dz_ref, dy_ref):
        z = z_ref[...]
        duv = du_ref[...]
        sg = jax.nn.sigmoid(z)
        _split(duv * (z * sg), HEADS, DH, dy_ref)
        dz_ref[...] = (duv * _join(y_ref, HEADS) * (sg * (1.0 + z * (1.0 - sg)))).astype(BF16)

    hs = _head_spec(HEADS, tm, DH)
    return pl.pallas_call(
        body, name=name, grid=(L // tm,),
        in_specs=[pl.BlockSpec((tm, 512), lambda i: (i, 0)), hs, _col_spec(tm, 512, zcol),
                  pl.BlockSpec(memory_space=pl.ANY)],
        out_specs=[_col_spec(tm, 512, zcol), hs],
        out_shape=[jax.ShapeDtypeStruct(dproj.shape, dproj.dtype), jax.ShapeDtypeStruct((HEADS, L, DH), F32)],
        input_output_aliases={3: 0},
        compiler_params=_cparams(("parallel",)),
    )(du, y, proj, dproj)


MERGE_T = 192


def _merge(proj, pbs, *, name):
    L = proj.shape[0]
    tm = MERGE_T

    def body(g0, g1, g2, p0, p1, p2, o_ref):
        acc = None
        for g_ref, p_ref in ((g0, p0), (g1, p1), (g2, p2)):
            t = jax.nn.sigmoid(g_ref[...]) * p_ref[...]
            acc = t if acc is None else acc + t
        o_ref[...] = acc.astype(BF16)

    row = pl.BlockSpec((tm, D_MODEL), lambda i: (i, 0))
    return pl.pallas_call(
        body, name=name, grid=(L // tm,),
        in_specs=[_col_spec(tm, D_MODEL, GATES + n * D_MODEL) for n in range(N_BRANCH)] + [row] * N_BRANCH,
        out_specs=row, out_shape=jax.ShapeDtypeStruct((L, D_MODEL), BF16),
        compiler_params=_cparams(("parallel",)),
    )(proj, proj, proj, *pbs)


def _merge_bwd(dmerged, proj, pbs, dproj, *, name):
    L = proj.shape[0]
    tm = MERGE_T

    def body(dm_ref, g_ref, p0, p1, p2, _, dg_ref, dp0, dp1, dp2):
        dm = dm_ref[...]
        for n, (p_ref, dp_ref) in enumerate(((p0, dp0), (p1, dp1), (p2, dp2))):
            cols = slice(n * D_MODEL, (n + 1) * D_MODEL)
            sg = jax.nn.sigmoid(g_ref[:, cols])
            dp_ref[...] = (dm * sg).astype(BF16)
            dg_ref[:, cols] = (dm * p_ref[...] * (sg * (1.0 - sg))).astype(BF16)

    row = pl.BlockSpec((tm, D_MODEL), lambda i: (i, 0))
    gates = _col_spec(tm, N_BRANCH * D_MODEL, GATES)
    outs = pl.pallas_call(
        body, name=name, grid=(L // tm,),
        in_specs=[row, gates] + [row] * N_BRANCH + [pl.BlockSpec(memory_space=pl.ANY)],
        out_specs=[gates] + [row] * N_BRANCH,
        out_shape=[jax.ShapeDtypeStruct(dproj.shape, dproj.dtype)]
        + [jax.ShapeDtypeStruct((L, D_MODEL), BF16)] * N_BRANCH,
        input_output_aliases={5: 0},
        compiler_params=_cparams(("parallel",)),
    )(dmerged, proj, *pbs, dproj)
    return outs[0], outs[1:]


def _forget_bias(af, b_f):
    return jnp.cumsum(jax.nn.log_sigmoid(af + b_f), axis=0).T


def _key_bias(L, T, ct=None):
    padb = jnp.where(jnp.arange(L) < PAD, BIG, 0.0).astype(F32)[None]
    kb = padb if ct is None else ct + padb
    return kb.reshape(kb.shape[0], L // T, 1, T)


def _layer_fwd(h, w, tabs, l, side=None):
    tag = f"l{l}"
    L = h.shape[0]
    hn = _rms_fwd(h, w["norm_g"], name=f"{tag}_rms_in")
    proj = _mm(hn, w["w_in"], name=f"{tag}_mm_in")
    ct, vjp_f = jax.vjp(_forget_bias, proj[:, FA:FA + HEADS], w["b_f"])
    ops_a = (*_prep_a(proj, name=f"{tag}_prep_a"), _key_bias(L, ATT_T, ct))
    ya, lsea, *side_out = _attn_fwd(*ops_a, T=ATT_T, side=side, name=f"{tag}_attn_a")
    cqn = _rms_fwd(proj, w["g_cq"], col=CQ, name=f"{tag}_rms_cq")
    ckvn = _rms_fwd(proj, w["g_ckv"], col=CKV, name=f"{tag}_rms_ckv")
    qbm = _mm(cqn, w["w_uq"], name=f"{tag}_mm_uq")
    kvbm = _mm(ckvn, w["w_ukv"], name=f"{tag}_mm_ukv")
    ops_b = (*_prep_b(qbm, kvbm, proj, tabs["bq"], tabs["bk"], name=f"{tag}_prep_b"), _key_bias(L, ATT_T))
    yb, lseb = _attn_fwd(*ops_b, T=ATT_T, name=f"{tag}_attn_b")
    ops_c = (*_prep_c(proj, tabs["c"], name=f"{tag}_prep_c"), _key_bias(L, WINDOW))
    sink = jnp.broadcast_to(w["sinks"][:, None, None], (HEADS, 1, LANES))
    yc, lsec = _attn_fwd(*ops_c, T=WINDOW, window=True, sink=sink, name=f"{tag}_attn_c")
    us = [_gate(y, proj, zcol, name=f"{tag}_gate{n}") for n, (y, zcol) in enumerate(((ya, ZA), (yb, ZB), (yc, ZC)))]
    pbr = [_mm(us[n], w["w_branch"][n], name=f"{tag}_mm_br{n}") for n in range(N_BRANCH)]
    merged = _merge(proj, pbr, name=f"{tag}_merge")
    out = _mm(merged, w["w_out"], add=h, name=f"{tag}_mm_out")
    saved = dict(h=h, hn=hn, proj=proj, vjp_f=vjp_f, ops_a=ops_a, ya=ya, lsea=lsea, cqn=cqn, ckvn=ckvn,
                 ops_b=ops_b, yb=yb, lseb=lseb, ops_c=ops_c, yc=yc, lsec=lsec, us=us, pbr=pbr, merged=merged)
    return out, saved, side_out


def _w_in_chunks(g):
    return _cols_to_shards(_unpad_in(g["w_in"]))


def _grad_chunks(g, with_w_in=True):
    rest = [_cols_to_shards(g["w_uq"]), _cols_to_shards(g["w_ukv"]),
            jnp.stack([_cols_to_shards(g["w_branch"][n]) for n in range(N_BRANCH)], axis=1),
            g["w_out"].reshape(N_DEV, D_MODEL // N_DEV, D_MODEL)]
    return ([_w_in_chunks(g)] if with_w_in else []) + rest


def _layer_bwd(dout, s, w, tabs, l, side=None, own_exchange=False):
    tag = f"l{l}"
    L = dout.shape[0]
    proj = s["proj"]
    g = {}
    dproj = jnp.zeros((L, NP_IN), BF16)
    dmerged = _mm(dout, w["w_out"], tb=True, name=f"{tag}_mm_out_dx")
    g["w_out"] = _mm(s["merged"], dout, ta=True, out_dtype=BF16, name=f"{tag}_mm_out_dw")
    dproj, dpbr = _merge_bwd(dmerged, proj, s["pbr"], dproj, name=f"{tag}_merge_bwd")
    dus = [_mm(dpbr[n], w["w_branch"][n], tb=True, name=f"{tag}_mm_br{n}_dx") for n in range(N_BRANCH)]
    g["w_branch"] = jnp.stack([_mm(s["us"][n], dpbr[n], ta=True, out_dtype=BF16, name=f"{tag}_mm_br{n}_dw")
                               for n in range(N_BRANCH)])
    dys = []
    for n, (y, zcol) in enumerate(((s["ya"], ZA), (s["yb"], ZB), (s["yc"], ZC))):
        dproj, dy = _gate_bwd(dus[n], y, proj, zcol, dproj, name=f"{tag}_gate{n}_bwd")
        dys.append(dy)
    dya, dyb, dyc = dys
    dqa, dka, dva, dcq, dck, *side_out = _attn_bwd(*s["ops_a"], s["ya"], s["lsea"], dya, T=ATT_T, fox=True,
                                                   side=side, name=f"{tag}_attn_a_bwd")
    dproj = _unprep_a(dqa, dka, dva, dproj, name=f"{tag}_unprep_a")
    daf, g["b_f"] = s["vjp_f"](dcq[:, :, 0] + dck.reshape(HEADS, L))
    dproj = lax.dynamic_update_slice(dproj, daf.astype(BF16), (0, FA))
    dqb, dkb, dvb, _, _ = _attn_bwd(*s["ops_b"], s["yb"], s["lseb"], dyb, T=ATT_T, name=f"{tag}_attn_b_bwd")
    dproj, dqbm, dkvbm = _unprep_b(dqb, dkb, dvb, tabs["bq"], tabs["bk"], dproj, name=f"{tag}_unprep_b")
    dcqn = _mm(dqbm, w["w_uq"], tb=True, name=f"{tag}_mm_uq_dx")
    g["w_uq"] = _mm(s["cqn"], dqbm, ta=True, out_dtype=BF16, name=f"{tag}_mm_uq_dw")
    dckvn = _mm(dkvbm, w["w_ukv"], tb=True, name=f"{tag}_mm_ukv_dx")
    g["w_ukv"] = _mm(s["ckvn"], dkvbm, ta=True, out_dtype=BF16, name=f"{tag}_mm_ukv_dw")
    dproj, g["g_cq"] = _rms_bwd(dcqn, proj, w["g_cq"], col=CQ, into=dproj, name=f"{tag}_rms_cq_bwd")
    dproj, g["g_ckv"] = _rms_bwd(dckvn, proj, w["g_ckv"], col=CKV, into=dproj, name=f"{tag}_rms_ckv_bwd")
    dqc, dkc, dvc = _attn_bwd_window(*s["ops_c"], s["yc"], s["lsec"], dyc, name=f"{tag}_attn_c_bwd")
    dproj = _unprep_c(dqc, dkc, dvc, tabs["c"], dproj, name=f"{tag}_unprep_c")
    delta_c = jnp.sum(dyc * s["yc"], axis=-1)
    g["sinks"] = -jnp.sum(jnp.exp(w["sinks"][:, None] - s["lsec"][:, :, 0]) * delta_c, axis=1)
    own_out = []
    if own_exchange:
        g["w_in"], *r_rest = _mm(s["hn"], dproj, ta=True, out_dtype=BF16, side=_Exchange(_grad_chunks(g, False)),
                                 name=f"{tag}_mm_in_dw")
        dhn, r_in = _mm(dproj, w["w_in"], tb=True, side=_Exchange([_w_in_chunks(g)]), name=f"{tag}_mm_in_dx")
        own_out = [r_in, *r_rest]
    else:
        g["w_in"] = _mm(s["hn"], dproj, ta=True, out_dtype=BF16, name=f"{tag}_mm_in_dw")
        dhn = _mm(dproj, w["w_in"], tb=True, name=f"{tag}_mm_in_dx")
    dh, g["norm_g"] = _rms_bwd(dhn, s["h"], w["norm_g"], add=dout, name=f"{tag}_rms_in_bwd")
    return dh, g, side_out, own_out


def _cols_from_shards(g):
    return jnp.moveaxis(g, 0, 1).reshape(g.shape[1], N_DEV * g.shape[2])


def _cols_to_shards(w):
    R = w.shape[0]
    return jnp.moveaxis(w.reshape(R, N_DEV, w.shape[1] // N_DEV), 1, 0)


def _pad_in(w):
    parts, at = [], 0
    for lo, hi, dst in sorted(_RUNS, key=lambda r: r[2]):
        if dst > at:
            parts.append(jnp.zeros((w.shape[0], dst - at), w.dtype))
        parts.append(w[:, lo:hi])
        at = dst + hi - lo
    parts.append(jnp.zeros((w.shape[0], NP_IN - at), w.dtype))
    return jnp.concatenate(parts, axis=1)


def _unpad_in(w):
    return jnp.concatenate([w[:, dst:dst + hi - lo] for lo, hi, dst in _RUNS], axis=1)


_SMALL = (("norm_g", DEPTH * D_MODEL), ("b_f", DEPTH * HEADS), ("g_cq", DEPTH * MLA_QLORA),
          ("g_ckv", DEPTH * MLA_KVLORA), ("sinks", DEPTH * HEADS), ("final_g", D_MODEL), ("loss", 1),
          ("meta", N_META * D_MODEL))
SMALL_ROWS = 168


def _pack_small(d):
    parts = []
    for name, size in _SMALL:
        padded = -(-size // 128) * 128
        v = d[name].reshape(-1).astype(F32) if name in d else jnp.zeros((size,), F32)
        parts.append(jnp.pad(v, (0, padded - size)))
    flat = jnp.concatenate(parts)
    return jnp.pad(flat, (0, SMALL_ROWS * 128 - flat.shape[0])).reshape(SMALL_ROWS, 128)


def _unpack_small(p, shapes):
    flat = p.reshape(-1)
    out, at = {}, 0
    for name, size in _SMALL:
        if name in shapes:
            out[name] = flat[at:at + size].reshape(shapes[name])
        at += -(-size // 128) * 128
    return out


def kernel(x, meta_tokens, norm_g, w_in, b_f, g_cq, g_ckv, w_uq, w_ukv, sinks, w_branch, w_out, final_g, loss_target, m_meta_tokens, m_norm_g, m_w_in, m_b_f, m_g_cq, m_g_ckv, m_w_uq, m_w_ukv, m_sinks, m_w_branch, m_w_out, m_final_g, v_meta_tokens, v_norm_g, v_w_in, v_b_f, v_g_cq, v_g_ckv, v_w_uq, v_w_ukv, v_sinks, v_w_branch, v_w_out, v_final_g):
    S = x.shape[1]
    L = BLK + S
    cx, cy, cc = _me()
    my_idx = 4 * cx + 2 * cy + cc

    def shards(l):
        return [t[l].astype(BF16) for t in (w_in, w_uq, w_ukv, w_branch, w_out)]

    def layer_weights(l, gw_in, gw_uq, gw_ukv, gw_br, gw_out):
        return dict(
            norm_g=norm_g[l], b_f=b_f[l], g_cq=g_cq[l], g_ckv=g_ckv[l], sinks=sinks[l],
            w_in=_pad_in(_cols_from_shards(gw_in)), w_uq=_cols_from_shards(gw_uq), w_ukv=_cols_from_shards(gw_ukv),
            w_branch=jnp.stack([_cols_from_shards(gw_br[:, n]) for n in range(N_BRANCH)]),
            w_out=gw_out.reshape(D_MODEL, D_MODEL))

    *gw0, g_meta = _comm(_Gather(shards(0) + [meta_tokens]), name="gather_l0")
    layers = [layer_weights(0, *gw0), None]
    meta_full = _cols_from_shards(g_meta)

    h = jnp.concatenate([jnp.zeros((PAD, D_MODEL), F32), meta_full, x[0]], axis=0)
    q_lanes = jnp.arange(HEADS * DQK_B)
    tabs = dict(c=_rope_tables(L, DH // 2, LANES), bk=_rope_tables(L, MLA_ROPE // 2, LANES),
                bq=_rope_tables(L, MLA_ROPE // 2, HEADS * DQK_B, rope_lane=(q_lanes % DQK_B) >= MLA_NOPE))
    saved = [None] * DEPTH
    h, saved[0], gw1 = _layer_fwd(h, layers[0], tabs, 0, side=_Gather(shards(1)))
    layers[1] = layer_weights(1, *gw1)
    h, saved[1], _ = _layer_fwd(h, layers[1], tabs, 1)
    loss_vec, dh, g_final = _loss_head(h, final_g, loss_target[0], name="loss_head")

    grads = [None] * DEPTH
    dh, grads[1], _, _ = _layer_bwd(dh, saved[1], layers[1], tabs, 1)
    dh, grads[0], recv1, recv0 = _layer_bwd(dh, saved[0], layers[0], tabs, 0, side=_Exchange(_grad_chunks(grads[1])),
                                            own_exchange=True)
    r_in, r_uq, r_ukv, r_br, r_out = (jnp.stack([a, b], axis=1) for a, b in zip(recv0, recv1))

    def stack(name):
        return jnp.stack([grads[l][name] for l in range(DEPTH)])

    small = _pack_small(dict(norm_g=stack("norm_g"), b_f=stack("b_f"), g_cq=stack("g_cq"), g_ckv=stack("g_ckv"),
                             sinks=stack("sinks"), final_g=g_final, loss=loss_vec[0, 0:1],
                             meta=dh[PAD:BLK]))
    (g_small,) = _comm(_Gather([small]), name="gather_small")

    def adam_big(w_, parts, m_, v_, name):
        shape = w_.shape
        C = shape[-1]
        R = math.prod(shape[:-1])
        outs = _adamw(w_.reshape(R, C), parts.reshape(parts.shape[0], R, C), m_.reshape(R, C), v_.reshape(R, C),
                      name=name)
        return [o.reshape(shape) for o in outs]

    res = {}
    res["w_in"] = adam_big(w_in, r_in, m_w_in, v_w_in, "adam_w_in")
    res["w_uq"] = adam_big(w_uq, r_uq, m_w_uq, v_w_uq, "adam_w_uq")
    res["w_ukv"] = adam_big(w_ukv, r_ukv, m_w_ukv, v_w_ukv, "adam_w_ukv")
    res["w_branch"] = adam_big(w_branch, r_br, m_w_branch, v_w_branch, "adam_w_branch")
    res["w_out"] = adam_big(w_out, r_out, m_w_out, v_w_out, "adam_w_out")

    small_w = dict(norm_g=norm_g, b_f=b_f, g_cq=g_cq, g_ckv=g_ckv, sinks=sinks, final_g=final_g)
    small_m = dict(norm_g=m_norm_g, b_f=m_b_f, g_cq=m_g_cq, g_ckv=m_g_ckv, sinks=m_sinks, final_g=m_final_g)
    small_v = dict(norm_g=v_norm_g, b_f=v_b_f, g_cq=v_g_cq, g_ckv=v_g_ckv, sinks=v_sinks, final_g=v_final_g)
    sm = _adamw(_pack_small(small_w), g_small, _pack_small(small_m), _pack_small(small_v), name="adam_small")
    shapes = {k: a.shape for k, a in small_w.items()}
    shapes_all = dict(shapes, loss=(), meta=(N_META, D_MODEL))
    sm_g = _unpack_small(sm[0], shapes_all)
    sm_d, sm_m, sm_v = (_unpack_small(t, shapes) for t in sm[1:])
    for k in shapes:
        res[k] = [sm_g[k], sm_d[k], sm_m[k], sm_v[k]]
    g_meta_mine = lax.dynamic_slice(sm_g["meta"], (0, my_idx * 128), (N_META, 128))
    res["meta_tokens"] = _adamw(meta_tokens, g_meta_mine[None], m_meta_tokens, v_meta_tokens, name="adam_meta")

    order = ["meta_tokens", "norm_g", "w_in", "b_f", "g_cq", "g_ckv", "w_uq", "w_ukv", "sinks", "w_branch", "w_out",
             "final_g"]
    grad_x = dh[BLK:][None]
    return (sm_g["loss"], grad_x, *[res[k][0] for k in order], *[res[k][1] for k in order],
            *[res[k][2] for k in order], *[res[k][3] for k in order])
```

```python
import functools
import math

import jax
import jax.numpy as jnp
from jax import lax
from jax.experimental import pallas as pl
from jax.experimental.pallas import tpu as pltpu

F32 = jnp.float32
BF16 = jnp.bfloat16

D_MODEL = 1024
DEPTH = 2
N_META = 16
BLK = 128
PAD = BLK - N_META
ROPE_THETA = 10000.0
EPS = 1e-6
NEG = -1e30
BIG = 1e30
HEADS = 8
DH = 64
MLA_NOPE = 64
MLA_ROPE = 32
MLA_QLORA = 384
MLA_KVLORA = 256
SWA_KV_HEADS = 2
WINDOW = 128
BRANCH_W = 512
N_BRANCH = 3
N_IN = 7592

ADAM_LR = 0.001
ADAM_B1 = 0.9
ADAM_B2 = 0.999
ADAM_EPS = 1e-08
ADAM_WD = 0.01
ADAM_STEP = 10

N_DEV = 8
MESH = pl.DeviceIdType.MESH

NP_IN = 8192
QKVA, ZA, FA = 0, 1536, 2048
CKV, KR, CQ = 2304, 2560, 2688
GATES = 3072
ZB = 6144
QKVC, ZC = 6912, 7680
_RUNS = ((0, 1536, QKVA), (1536, 1544, FA), (1544, 2056, ZA), (2056, 2440, CQ), (2440, 2696, CKV), (2696, 2728, KR),
         (2728, 3240, ZB), (3240, 4008, QKVC), (4008, 4520, ZC), (4520, 7592, GATES))

VMEM_LIMIT = 48 * 1024 * 1024
ATT_T = 384
ATT_HP = 1
ROW_T = 384
LANES = 128


def _pick(dim, prefs):
    for p in prefs:
        if dim % p == 0:
            return p
    return dim


def _cparams(sem):
    return pltpu.CompilerParams(dimension_semantics=sem, vmem_limit_bytes=VMEM_LIMIT)


def _side_cparams(side, sem):
    if side is None:
        return _cparams(sem)
    return pltpu.CompilerParams(dimension_semantics=("arbitrary",) * len(sem), vmem_limit_bytes=VMEM_LIMIT,
                                has_side_effects=True)


def _mm(a, b, *, ta=False, tb=False, add=None, out_dtype=F32, side=None, name):
    M = a.shape[1] if ta else a.shape[0]
    K = a.shape[0] if ta else a.shape[1]
    N = b.shape[0] if tb else b.shape[1]
    assert K == (b.shape[1] if tb else b.shape[0])
    tm = _pick(M, (704, 1024, 512, 384, 256, 128))
    tn = _pick(N, (1024, 768, 512, 384, 256, 128))
    tk = _pick(K, (2048, 1408, 1024, 768, 512, 384, 256, 128))
    nk = K // tk
    dims = (((0 if ta else 1,), (1 if tb else 0,)), ((), ()))

    sn = 0 if side is None else side.n
    n_in = 2 + (add is not None)
    grid = (M // tm, N // tn, nk)

    def body(*refs):
        a_ref, b_ref = refs[:2]
        c_ref = refs[2] if add is not None else None
        o_ref = refs[n_in + sn]
        scratch = refs[n_in + 2 * sn + 1:]
        if side is not None:
            step = (pl.program_id(0) * grid[1] + pl.program_id(1)) * nk + pl.program_id(2)
            _ride(side, [*refs[n_in:n_in + sn], *refs[n_in + sn + 1:n_in + 2 * sn + 1], *scratch[nk > 1:]],
                  step, grid[0] * grid[1] * nk)
        r = lax.dot_general(a_ref[...].astype(BF16), b_ref[...].astype(BF16), dims, preferred_element_type=F32)

        def finish(total):
            if c_ref is not None:
                total = total + c_ref[...]
            o_ref[...] = total.astype(out_dtype)

        if nk == 1:
            finish(r)
        else:
            acc = scratch[0]
            k = pl.program_id(2)

            @pl.when(k == 0)
            def _():
                acc[...] = r

            @pl.when(k > 0)
            def _():
                acc[...] += r

            @pl.when(k == nk - 1)
            def _():
                finish(acc[...])

    a_spec = pl.BlockSpec((tk, tm), lambda i, j, k: (k, i)) if ta else pl.BlockSpec((tm, tk), lambda i, j, k: (i, k))
    b_spec = pl.BlockSpec((tn, tk), lambda i, j, k: (j, k)) if tb else pl.BlockSpec((tk, tn), lambda i, j, k: (k, j))
    o_spec = pl.BlockSpec((tm, tn), lambda i, j, k: (i, j))
    any_spec = pl.BlockSpec(memory_space=pl.ANY)
    outs = pl.pallas_call(
        body, name=name,
        grid=grid,
        in_specs=[a_spec, b_spec] + ([o_spec] if add is not None else []) + [any_spec] * sn,
        out_specs=[o_spec] + [any_spec] * sn,
        out_shape=[jax.ShapeDtypeStruct((M, N), out_dtype)] + ([] if side is None else side.out_shape),
        scratch_shapes=([pltpu.VMEM((tm, tn), F32)] if nk > 1 else []) + ([] if side is None else side.scratch),
        compiler_params=_side_cparams(side, ("parallel", "parallel", "arbitrary")),
    )(*((a, b) if add is None else (a, b, add)), *([] if side is None else side.arrs))
    return outs[0] if side is None else outs


def _col_spec(tm, width, col):
    assert col % width == 0
    return pl.BlockSpec((tm, width), lambda i, _c=col // width: (i, _c))


def _rms_fwd(x, g, *, col=0, name):
    L = x.shape[0]
    D = g.shape[0]
    tm = ROW_T

    def body(x_ref, g_ref, y_ref):
        xv = x_ref[...]
        rstd = lax.rsqrt(jnp.mean(xv * xv, axis=-1, keepdims=True) + EPS)
        y_ref[...] = (xv * rstd * g_ref[...]).astype(BF16)

    return pl.pallas_call(
        body, name=name, grid=(L // tm,),
        in_specs=[_col_spec(tm, D, col), pl.BlockSpec((1, D), lambda i: (0, 0))],
        out_specs=pl.BlockSpec((tm, D), lambda i: (i, 0)),
        out_shape=jax.ShapeDtypeStruct((L, D), BF16),
        compiler_params=_cparams(("parallel",)),
    )(x, g.reshape(1, D))


def _rms_bwd(dy, x, g, *, col=0, add=None, into=None, name):
    L = x.shape[0]
    D = g.shape[0]
    tm = ROW_T

    def body(*refs):
        dy_ref, x_ref, g_ref = refs[:3]
        add_ref = refs[3] if add is not None else None
        dx_ref, dg_ref = refs[-2:]
        i = pl.program_id(0)
        xv = x_ref[...]
        dyv = dy_ref[...]
        rstd = lax.rsqrt(jnp.mean(xv * xv, axis=-1, keepdims=True) + EPS)
        xhat = xv * rstd
        part = jnp.sum(dyv * xhat, axis=0, keepdims=True)

        @pl.when(i == 0)
        def _():
            dg_ref[...] = part

        @pl.when(i > 0)
        def _():
            dg_ref[...] += part

        dxh = dyv * g_ref[...]
        dx = rstd * (dxh - xhat * jnp.mean(dxh * xhat, axis=-1, keepdims=True))
        if add_ref is not None:
            dx = dx + add_ref[...]
        dx_ref[...] = dx.astype(dx_ref.dtype)

    row = pl.BlockSpec((tm, D), lambda i: (i, 0))
    in_specs = [row, _col_spec(tm, D, col), pl.BlockSpec((1, D), lambda i: (0, 0))]
    args = [dy, x, g.reshape(1, D)]
    aliases = {}
    if add is not None:
        in_specs.append(row)
        args.append(add)
    if into is not None:
        in_specs.append(pl.BlockSpec(memory_space=pl.ANY))
        args.append(into)
        aliases = {len(args) - 1: 0}
        dx_spec, dx_shape = _col_spec(tm, D, col), jax.ShapeDtypeStruct(into.shape, into.dtype)
    else:
        dx_spec, dx_shape = row, jax.ShapeDtypeStruct((L, D), F32)
    dx, dg = pl.pallas_call(
        body, name=name, grid=(L // tm,),
        in_specs=in_specs,
        out_specs=[dx_spec, pl.BlockSpec((1, D), lambda i: (0, 0))],
        out_shape=[dx_shape, jax.ShapeDtypeStruct((1, D), F32)],
        input_output_aliases=aliases,
        compiler_params=_cparams(("arbitrary",)),
    )(*args)
    return dx, dg.reshape(D)


def _loss_head(h, g, target, *, name):
    L, D = h.shape
    nb = L // BLK

    def body(h_ref, g_ref, t_ref, loss_ref, dh_ref, dg_ref):
        i = pl.program_id(0)

        @pl.when(i == 0)
        def _():
            loss_ref[...] = jnp.zeros_like(loss_ref)
            dg_ref[...] = jnp.zeros_like(dg_ref)
            dh_ref[...] = jnp.zeros_like(dh_ref)

        @pl.when(i > 0)
        def _():
            xv = h_ref[...]
            gv = g_ref[...]
            rstd = lax.rsqrt(jnp.mean(xv * xv, axis=-1, keepdims=True) + EPS)
            xhat = xv * rstd
            err = xhat * gv - t_ref[...]
            row = jnp.mean(err * err, axis=-1, keepdims=True)
            loss_ref[...] += 0.5 * jnp.sum(row, axis=0, keepdims=True)
            dy = err * (1.0 / D)
            dg_ref[...] += jnp.sum(dy * xhat, axis=0, keepdims=True)
            dxh = dy * gv
            dh_ref[...] = rstd * (dxh - xhat * jnp.mean(dxh * xhat, axis=-1, keepdims=True))

    loss, dh, dg = pl.pallas_call(
        body, name=name, grid=(nb,),
        in_specs=[pl.BlockSpec((BLK, D), lambda i: (i, 0)), pl.BlockSpec((1, D), lambda i: (0, 0)),
                  pl.BlockSpec((BLK, D), lambda i: (jnp.maximum(i - 1, 0), 0))],
        out_specs=[pl.BlockSpec((1, 128), lambda i: (0, 0)), pl.BlockSpec((BLK, D), lambda i: (i, 0)),
                   pl.BlockSpec((1, D), lambda i: (0, 0))],
        out_shape=[jax.ShapeDtypeStruct((1, 128), F32), jax.ShapeDtypeStruct((L, D), F32),
                   jax.ShapeDtypeStruct((1, D), F32)],
        compiler_params=_cparams(("arbitrary",)),
    )(h, g.reshape(1, D), target)
    return loss, dh, dg.reshape(D)


_NT = (((1,), (1,)), ((), ()))
_TN = (((0,), (0,)), ((), ()))


def _attn_fwd(q, k, v, kbias, *, T, window=False, sink=None, side=None, name):
    sn = 0 if side is None else side.n
    H, L, dk = q.shape
    Hkv = k.shape[0]
    dv = v.shape[2]
    G = H // Hkv
    nt = L // T
    Hb = kbias.shape[0]
    reps = T // LANES
    assert not window or T == WINDOW
    HP = G if G > 1 else ATT_HP
    NS = 1 if G > 1 else HP
    R = HP * T // NS
    HKV = HP // G
    HB = HP if Hb > 1 else 1
    assert G == 1 or Hb == 1

    def body(*refs):
        q_ref, k_ref, v_ref, kb_ref = refs[:4]
        n = 4
        if sink is not None:
            sk_ref = refs[n]
            n += 1
        side_in = refs[n:n + sn]
        n += sn
        o_ref, lse_ref = refs[n:n + 2]
        side_out = refs[n + 2:n + 2 + sn]
        n += 2 + sn
        m_scs, l_scs, acc_scs, buf_a, buf_b = (refs[n + t * NS:n + (t + 1) * NS] for t in range(5))
        i = pl.program_id(1)
        if side is not None:
            _ride(side, [*side_in, *side_out, *refs[n + 5 * NS:]], pl.program_id(0) * nt + i, (H // HP) * nt)
        for a in range(NS):
            if sink is not None:
                sk = [jnp.broadcast_to(sk_ref[b, :, 0:1], (T, LANES)) for b in range(HP)]
                m_scs[a][...] = jnp.concatenate(sk, axis=0) if G > 1 else sk[a]
                l_scs[a][...] = jnp.ones((R, LANES), F32)
            else:
                m_scs[a][...] = jnp.full((R, LANES), NEG, F32)
                l_scs[a][...] = jnp.zeros((R, LANES), F32)
            acc_scs[a][...] = jnp.zeros((R, dv), F32)
        row = lax.broadcasted_iota(jnp.int32, (R, T), 0) & (T - 1) if G > 1 else \
            lax.broadcasted_iota(jnp.int32, (R, T), 0)
        col = lax.broadcasted_iota(jnp.int32, (R, T), 1)

        def logits(a, j):
            rows = pl.ds(pl.multiple_of(j * T, T), T)
            qv = q_ref[...].reshape(R, dk) if G > 1 else q_ref[a]
            s = lax.dot_general(qv, k_ref[a, rows, :], _NT, preferred_element_type=F32)
            return s - kb_ref[a if HB > 1 else 0, j]

        def update(a, s, j, kind):
            rows = pl.ds(pl.multiple_of(j * T, T), T)
            m_sc, l_sc, acc_sc = m_scs[a], l_scs[a], acc_scs[a]
            if kind == "diag":
                s = jnp.where(row >= col, s, NEG)
            elif kind == "prev":
                s = jnp.where(col > row, s, NEG)
            m_prev = m_sc[...]
            m_new = jnp.maximum(m_prev, jnp.max(s, axis=1, keepdims=True))
            alpha = jnp.exp(m_prev - m_new)
            p = jnp.exp(s - jnp.tile(m_new, (1, reps)))
            l_sc[...] = alpha * l_sc[...] + jnp.sum(p, axis=1, keepdims=True)
            acc_sc[...] = alpha[:, :dv] * acc_sc[...] + jnp.dot(p.astype(BF16), v_ref[a, rows, :],
                                                                preferred_element_type=F32)
            m_sc[...] = m_new

        if window:
            @pl.when(i > 0)
            def _():
                for a in range(NS):
                    update(a, logits(a, i - 1), i - 1, "prev")

            for a in range(NS):
                update(a, logits(a, i), i, "diag")
        else:
            def fill(buf, j):
                for a in range(NS):
                    buf[a][...] = logits(a, j)

            def drain(buf, j, kind):
                for a in range(NS):
                    update(a, buf[a][...], j, kind)

            fill(buf_a, 0)

            def pair(t, c):
                fill(buf_b, 2 * t + 1)
                drain(buf_a, 2 * t, "full")
                fill(buf_a, 2 * t + 2)
                drain(buf_b, 2 * t + 1, "full")
                return c

            lax.fori_loop(0, i // 2, pair, 0)

            @pl.when(i % 2 == 1)
            def _():
                fill(buf_b, i)
                drain(buf_a, i - 1, "full")
                drain(buf_b, i, "diag")

            @pl.when(i % 2 == 0)
            def _():
                drain(buf_a, i, "diag")

        for a in range(NS):
            lv = l_scs[a][...]
            ov = acc_scs[a][...] / lv[:, :dv]
            lsev = (m_scs[a][...] + jnp.log(lv))[:, 0:1]
            if G > 1:
                o_ref[...] = ov.reshape(HP, T, dv)
                lse_ref[...] = lsev.reshape(HP, T, 1)
            else:
                o_ref[a] = ov
                lse_ref[a] = lsev

    in_specs = [pl.BlockSpec((HP, T, dk), lambda h, i: (h, i, 0)),
                pl.BlockSpec((HKV, L, dk), lambda h, i: (h, 0, 0)),
                pl.BlockSpec((HKV, L, dv), lambda h, i: (h, 0, 0)),
                pl.BlockSpec((HB, nt, 1, T), lambda h, i: (h if Hb > 1 else 0, 0, 0, 0))]
    args = [q, k, v, kbias]
    if sink is not None:
        in_specs += [pl.BlockSpec((HP, 1, LANES), lambda h, i: (h, 0, 0))]
        args += [sink]
    any_spec = pl.BlockSpec(memory_space=pl.ANY)
    return pl.pallas_call(
        body, name=name, grid=(H // HP, nt),
        in_specs=in_specs + [any_spec] * sn,
        out_specs=[pl.BlockSpec((HP, T, dv), lambda h, i: (h, i, 0)),
                   pl.BlockSpec((HP, T, 1), lambda h, i: (h, i, 0))] + [any_spec] * sn,
        out_shape=[jax.ShapeDtypeStruct((H, L, dv), F32), jax.ShapeDtypeStruct((H, L, 1), F32)]
        + ([] if side is None else side.out_shape),
        scratch_shapes=[pltpu.VMEM((R, LANES), F32)] * (2 * NS) + [pltpu.VMEM((R, dv), F32)] * NS
        + [pltpu.VMEM((R, T), F32)] * (2 * NS) + ([] if side is None else side.scratch),
        compiler_params=_side_cparams(side, ("parallel", "arbitrary")),
    )(*args, *([] if side is None else side.arrs))


def _attn_bwd(q, k, v, kbias, o, lse, do, *, T, fox=False, side=None, name):
    sn = 0 if side is None else side.n
    H, L, dk = q.shape
    Hkv = k.shape[0]
    dv = v.shape[2]
    G = H // Hkv
    nt = L // T
    Hb = kbias.shape[0]

    def body(*refs):
        q_ref, k_ref, v_ref, kb_ref, o_ref, lse_ref, do_ref = refs[:7]
        side_in = refs[7:7 + sn]
        n = 7 + sn
        dq_ref, dk_ref, dv_ref = refs[n:n + 3]
        n += 3
        if fox:
            dcq_ref, dck_ref = refs[n:n + 2]
            n += 2
        side_out = refs[n:n + sn]
        n += sn
        delta_sc = refs[n]
        buf_a, buf_b = refs[n + 1:n + 3], refs[n + 3:n + 5]
        j = pl.program_id(1)
        if side is not None:
            _ride(side, [*side_in, *side_out, *refs[n + 5:]], pl.program_id(0) * nt + j, H * nt)

        @pl.when(j == 0)
        def _():
            dq_ref[...] = jnp.zeros_like(dq_ref)
            if fox:
                dcq_ref[...] = jnp.zeros_like(dcq_ref)

            def dl(i, c):
                rows = pl.ds(pl.multiple_of(i * T, T), T)
                delta_sc[rows, :] = jnp.sum(do_ref[0, rows, :] * o_ref[0, rows, :], axis=1, keepdims=True)
                return c

            lax.fori_loop(0, nt, dl, 0)

        dk_ref[...] = jnp.zeros_like(dk_ref)
        dv_ref[...] = jnp.zeros_like(dv_ref)
        if fox:
            dck_ref[...] = jnp.zeros_like(dck_ref)
        kb = k_ref[0]
        vb = v_ref[0]
        kbias_j = kb_ref[0, j]
        row = lax.broadcasted_iota(jnp.int32, (T, T), 0)
        col = lax.broadcasted_iota(jnp.int32, (T, T), 1)

        def fill(buf, i):
            rows = pl.ds(pl.multiple_of(i * T, T), T)
            buf[0][...] = lax.dot_general(q_ref[0, rows, :], kb, _NT, preferred_element_type=F32) - kbias_j
            buf[1][...] = lax.dot_general(do_ref[0, rows, :].astype(BF16), vb, _NT, preferred_element_type=F32)

        def drain(buf, i, kind):
            rows = pl.ds(pl.multiple_of(i * T, T), T)
            qb = q_ref[0, rows, :]
            dob = do_ref[0, rows, :].astype(BF16)
            s = buf[0][...]
            if kind == "diag":
                s = jnp.where(row >= col, s, NEG)
            p = jnp.exp(s - lse_ref[0, rows, :])
            ds = p * (buf[1][...] - delta_sc[rows, :])
            dsb = ds.astype(BF16)
            dv_ref[0] += lax.dot_general(p.astype(BF16), dob, _TN, preferred_element_type=F32)
            dk_ref[0] += lax.dot_general(dsb, qb, _TN, preferred_element_type=F32)
            dq_ref[0, rows, :] += jnp.dot(dsb, kb, preferred_element_type=F32)
            if fox:
                dcq_ref[0, rows, :] += jnp.sum(ds, axis=1, keepdims=True)
                dck_ref[0, 0] += -jnp.sum(ds, axis=0, keepdims=True)

        fill(buf_a, j)
        drain(buf_a, j, "diag")
        first = j + 1
        rest = nt - first

        @pl.when(rest > 0)
        def _():
            fill(buf_a, first)

        def pair(t, c):
            i0 = first + 2 * t
            fill(buf_b, i0 + 1)
            drain(buf_a, i0, "full")
            fill(buf_a, jnp.minimum(i0 + 2, nt - 1))
            drain(buf_b, i0 + 1, "full")
            return c

        lax.fori_loop(0, rest // 2, pair, 0)

        @pl.when(rest % 2 == 1)
        def _():
            drain(buf_a, nt - 1, "full")

    in_specs = [pl.BlockSpec((1, L, dk), lambda h, j: (h, 0, 0)),
                pl.BlockSpec((1, T, dk), lambda h, j: (h // G, j, 0)),
                pl.BlockSpec((1, T, dv), lambda h, j: (h // G, j, 0)),
                pl.BlockSpec((1, nt, 1, T), lambda h, j: (h if Hb > 1 else 0, 0, 0, 0)),
                pl.BlockSpec((1, L, dv), lambda h, j: (h, 0, 0)),
                pl.BlockSpec((1, L, 1), lambda h, j: (h, 0, 0)),
                pl.BlockSpec((1, L, dv), lambda h, j: (h, 0, 0))]
    out_specs = [pl.BlockSpec((1, L, dk), lambda h, j: (h, 0, 0)),
                 pl.BlockSpec((1, T, dk), lambda h, j: (h, j, 0)),
                 pl.BlockSpec((1, T, dv), lambda h, j: (h, j, 0))]
    out_shape = [jax.ShapeDtypeStruct((H, L, dk), F32), jax.ShapeDtypeStruct((H, L, dk), F32),
                 jax.ShapeDtypeStruct((H, L, dv), F32)]
    if fox:
        out_specs += [pl.BlockSpec((1, L, 1), lambda h, j: (h, 0, 0)),
                      pl.BlockSpec((1, 1, 1, T), lambda h, j: (h, j, 0, 0))]
        out_shape += [jax.ShapeDtypeStruct((H, L, 1), F32), jax.ShapeDtypeStruct((H, nt, 1, T), F32)]
    any_spec = pl.BlockSpec(memory_space=pl.ANY)
    outs = pl.pallas_call(
        body, name=name, grid=(H, nt),
        in_specs=in_specs + [any_spec] * sn, out_specs=out_specs + [any_spec] * sn,
        out_shape=out_shape + ([] if side is None else side.out_shape),
        scratch_shapes=[pltpu.VMEM((L, 1), F32)] + [pltpu.VMEM((T, T), F32)] * 4
        + ([] if side is None else side.scratch),
        compiler_params=_side_cparams(side, ("parallel", "arbitrary")),
    )(q, k, v, kbias, o, lse, do, *([] if side is None else side.arrs))
    main, rest = outs[:len(outs) - sn], outs[len(outs) - sn:]
    return (*(main if fox else (*main, None, None)), *rest)


def _attn_bwd_t(q, k, v, kbias_col, lse_row, delta_row, do, *, T, fox=False, side=None, name):
    sn = 0 if side is None else side.n
    H, L, dk = q.shape
    dv = v.shape[2]
    nt = L // T
    Hb = kbias_col.shape[0]

    def body(*refs):
        q_ref, k_ref, v_ref, kb_ref, lse_ref, dl_ref, do_ref = refs[:7]
        side_in = refs[7:7 + sn]
        n = 7 + sn
        dq_ref, dk_ref, dv_ref = refs[n:n + 3]
        n += 3
        if fox:
            dcq_ref, dck_ref = refs[n:n + 2]
            n += 2
        side_out = refs[n:n + sn]
        n += sn
        buf_a, buf_b = refs[n:n + 2], refs[n + 2:n + 4]
        j = pl.program_id(1)
        if side is not None:
            _ride(side, [*side_in, *side_out, *refs[n + 4:]], pl.program_id(0) * nt + j, H * nt)

        @pl.when(j == 0)
        def _():
            dq_ref[...] = jnp.zeros_like(dq_ref)
            if fox:
                dcq_ref[...] = jnp.zeros_like(dcq_ref)

        dk_ref[...] = jnp.zeros_like(dk_ref)
        dv_ref[...] = jnp.zeros_like(dv_ref)
        if fox:
            dck_ref[...] = jnp.zeros_like(dck_ref)
        kb = k_ref[0]
        vb = v_ref[0]
        kbias_j = kb_ref[0]
        key = lax.broadcasted_iota(jnp.int32, (T, T), 0)
        qry = lax.broadcasted_iota(jnp.int32, (T, T), 1)

        def fill(buf, i):
            rows = pl.ds(pl.multiple_of(i * T, T), T)
            buf[0][...] = lax.dot_general(kb, q_ref[0, rows, :], _NT, preferred_element_type=F32) - kbias_j
            buf[1][...] = lax.dot_general(vb, do_ref[0, rows, :], _NT, preferred_element_type=F32)

        def drain(buf, i, kind):
            rows = pl.ds(pl.multiple_of(i * T, T), T)
            st = buf[0][...]
            if kind == "diag":
                st = jnp.where(key <= qry, st, NEG)
            pt = jnp.exp(st - lse_ref[0, i])
            dst = pt * (buf[1][...] - dl_ref[0, i])
            dsb = dst.astype(BF16)
            dv_ref[0] += jnp.dot(pt.astype(BF16), do_ref[0, rows, :], preferred_element_type=F32)
            dk_ref[0] += jnp.dot(dsb, q_ref[0, rows, :], preferred_element_type=F32)
            dq_ref[0, rows, :] += lax.dot_general(dsb, kb, _TN, preferred_element_type=F32)
            if fox:
                dcq_ref[0, i] += jnp.sum(dst, axis=0, keepdims=True)
                dck_ref[0] += -jnp.sum(dst, axis=1, keepdims=True)

        fill(buf_a, j)
        drain(buf_a, j, "diag")
        first = j + 1
        rest = nt - first

        @pl.when(rest > 0)
        def _():
            fill(buf_a, first)

        def pair(t, c):
            i0 = first + 2 * t
            fill(buf_b, i0 + 1)
            drain(buf_a, i0, "full")
            fill(buf_a, jnp.minimum(i0 + 2, nt - 1))
            drain(buf_b, i0 + 1, "full")
            return c

        lax.fori_loop(0, rest // 2, pair, 0)

        @pl.when(rest % 2 == 1)
        def _():
            drain(buf_a, nt - 1, "full")

    rows_spec = pl.BlockSpec((1, nt, 1, T), lambda h, j: (h, 0, 0, 0))
    in_specs = [pl.BlockSpec((1, L, dk), lambda h, j: (h, 0, 0)),
                pl.BlockSpec((1, T, dk), lambda h, j: (h, j, 0)),
                pl.BlockSpec((1, T, dv), lambda h, j: (h, j, 0)),
                pl.BlockSpec((1, T, 1), lambda h, j: (h if Hb > 1 else 0, j, 0)),
                rows_spec, rows_spec,
                pl.BlockSpec((1, L, dv), lambda h, j: (h, 0, 0))]
    out_specs = [pl.BlockSpec((1, L, dk), lambda h, j: (h, 0, 0)),
                 pl.BlockSpec((1, T, dk), lambda h, j: (h, j, 0)),
                 pl.BlockSpec((1, T, dv), lambda h, j: (h, j, 0))]
    out_shape = [jax.ShapeDtypeStruct((H, L, dk), F32), jax.ShapeDtypeStruct((H, L, dk), F32),
                 jax.ShapeDtypeStruct((H, L, dv), F32)]
    if fox:
        out_specs += [rows_spec, pl.BlockSpec((1, T, 1), lambda h, j: (h, j, 0))]
        out_shape += [jax.ShapeDtypeStruct((H, nt, 1, T), F32), jax.ShapeDtypeStruct((H, L, 1), F32)]
    any_spec = pl.BlockSpec(memory_space=pl.ANY)
    outs = pl.pallas_call(
        body, name=name, grid=(H, nt),
        in_specs=in_specs + [any_spec] * sn, out_specs=out_specs + [any_spec] * sn,
        out_shape=out_shape + ([] if side is None else side.out_shape),
        scratch_shapes=[pltpu.VMEM((T, T), F32)] * 4 + ([] if side is None else side.scratch),
        compiler_params=_side_cparams(side, ("parallel", "arbitrary")),
    )(q, k, v, kbias_col, lse_row, delta_row, do, *([] if side is None else side.arrs))
    main, rest = outs[:len(outs) - sn], outs[len(outs) - sn:]
    return (*(main if fox else (*main, None, None)), *rest)


def _attn_bwd_window(q, k, v, kbias, o, lse, do, *, name):
    H, L, dk = q.shape
    Hkv = k.shape[0]
    dv = v.shape[2]
    G = H // Hkv
    T = WINDOW
    nt = L // T
    R = G * T

    def body(q_ref, kc_ref, kp_ref, vc_ref, vp_ref, kb_ref, o_ref, lse_ref, do_ref, dq_ref, dk_ref, dv_ref,
             dk_sc, dv_sc):
        i = pl.program_id(1)
        row = lax.broadcasted_iota(jnp.int32, (R, T), 0) & (T - 1)
        col = lax.broadcasted_iota(jnp.int32, (R, T), 1)

        @pl.when(i == 0)
        def _():
            dk_sc[...] = jnp.zeros_like(dk_sc)
            dv_sc[...] = jnp.zeros_like(dv_sc)

        @pl.when(i == nt)
        def _():
            dk_ref[0] = dk_sc[...]
            dv_ref[0] = dv_sc[...]

        @pl.when(i < nt)
        def _():
            qb = q_ref[...].reshape(R, dk)
            dof = do_ref[...].reshape(R, dv)
            dob = dof.astype(BF16)
            lse_c = lse_ref[...].reshape(R, 1)
            delta = jnp.sum(dof * o_ref[...].reshape(R, dv), axis=1, keepdims=True)

            def grads(kt, vt, kbias_j, mask):
                s = lax.dot_general(qb, kt, _NT, preferred_element_type=F32) - kbias_j
                p = jnp.exp(jnp.where(mask, s, NEG) - lse_c)
                dp = lax.dot_general(dob, vt, _NT, preferred_element_type=F32)
                ds = (p * (dp - delta)).astype(BF16)
                return (jnp.dot(ds, kt, preferred_element_type=F32),
                        lax.dot_general(ds, qb, _TN, preferred_element_type=F32),
                        lax.dot_general(p.astype(BF16), dob, _TN, preferred_element_type=F32))

            ip = jnp.maximum(i - 1, 0)
            dq_p, dk_p, dv_p = grads(kp_ref[0], vp_ref[0], kb_ref[0, ip], (col > row) & (i > 0))
            dq_c, dk_c, dv_c = grads(kc_ref[0], vc_ref[0], kb_ref[0, i], row >= col)
            dq_ref[...] = (dq_p + dq_c).reshape(G, T, dk)
            dk_ref[0] = dk_sc[...] + dk_p
            dv_ref[0] = dv_sc[...] + dv_p
            dk_sc[...] = dk_c
            dv_sc[...] = dv_c

    def cur(i):
        return jnp.minimum(i, nt - 1)

    def prev(i):
        return jnp.maximum(jnp.minimum(i, nt - 1) - 1, 0)

    def written(i):
        return jnp.maximum(i - 1, 0)

    qs = lambda d: pl.BlockSpec((G, T, d), lambda h, i: (h, cur(i), 0))
    return pl.pallas_call(
        body, name=name, grid=(Hkv, nt + 1),
        in_specs=[qs(dk),
                  pl.BlockSpec((1, T, dk), lambda h, i: (h, cur(i), 0)),
                  pl.BlockSpec((1, T, dk), lambda h, i: (h, prev(i), 0)),
                  pl.BlockSpec((1, T, dv), lambda h, i: (h, cur(i), 0)),
                  pl.BlockSpec((1, T, dv), lambda h, i: (h, prev(i), 0)),
                  pl.BlockSpec((1, nt, 1, T), lambda h, i: (0, 0, 0, 0)),
                  qs(dv), qs(1), qs(dv)],
        out_specs=[qs(dk),
                   pl.BlockSpec((1, T, dk), lambda h, i: (h, written(i), 0)),
                   pl.BlockSpec((1, T, dv), lambda h, i: (h, written(i), 0))],
        out_shape=[jax.ShapeDtypeStruct((H, L, dk), F32), jax.ShapeDtypeStruct((Hkv, L, dk), F32),
                   jax.ShapeDtypeStruct((Hkv, L, dv), F32)],
        scratch_shapes=[pltpu.VMEM((T, dk), F32), pltpu.VMEM((T, dv), F32)],
        compiler_params=_cparams(("parallel", "arbitrary")),
    )(q, k, k, v, v, kbias, o, lse, do)


def _adamw(w, gparts, m, v, *, name):
    n, R, C = gparts.shape
    tr = _pick(R, (128, 64, 32, 16, 8))
    c1 = 1.0 - ADAM_B1 ** ADAM_STEP
    c2 = 1.0 - ADAM_B2 ** ADAM_STEP

    def body(w_ref, g_ref, m_ref, v_ref, go_ref, d_ref, mo_ref, vo_ref):
        g = g_ref[0].astype(F32)
        for t in range(1, n):
            g = g + g_ref[t].astype(F32)
        mn = ADAM_B1 * m_ref[...] + (1.0 - ADAM_B1) * g
        vn = ADAM_B2 * v_ref[...] + (1.0 - ADAM_B2) * (g * g)
        go_ref[...] = g
        mo_ref[...] = mn
        vo_ref[...] = vn
        d_ref[...] = -ADAM_LR * ((mn / c1) / (jnp.sqrt(vn / c2) + ADAM_EPS) + ADAM_WD * w_ref[...])

    spec = pl.BlockSpec((tr, C), lambda i: (i, 0))
    return pl.pallas_call(
        body, name=name, grid=(R // tr,),
        in_specs=[spec, pl.BlockSpec((n, tr, C), lambda i: (0, i, 0)), spec, spec],
        out_specs=[spec] * 4,
        out_shape=[jax.ShapeDtypeStruct((R, C), F32)] * 4,
        compiler_params=_cparams(("parallel",)),
    )(w, gparts, m, v)


def _me():
    return lax.axis_index("x"), lax.axis_index("y"), lax.axis_index("c")


class _CommJob:
    def __init__(self, arrs):
        self.arrs = list(arrs)
        self.n = len(arrs)
        self.scratch = [pltpu.SemaphoreType.DMA((self.n, 7)), pltpu.SemaphoreType.DMA((self.n, 7)),
                        pltpu.SemaphoreType.DMA((self.n,))]

    def bind(self, refs):
        n = self.n
        self.ins, self.outs = refs[:n], refs[n:2 * n]
        self.send_sems, self.recv_sems, self.local_sems = refs[2 * n:2 * n + 3]

    def middle(self):
        pass


class _Gather(_CommJob):
    def __init__(self, arrs):
        super().__init__(arrs)
        self.out_shape = [jax.ShapeDtypeStruct((N_DEV, *a.shape), a.dtype) for a in arrs]

    def _where(self):
        x, y, c = _me()
        return (x, y, c), (x, y, 1 - c), [(1 - x, y), (x, 1 - y), (1 - x, 1 - y)], c

    def _copy(self, t, k, block, to, src=None):
        dst = self.outs[t].at[4 * block[0] + 2 * block[1] + block[2]]
        return pltpu.make_async_remote_copy(
            src_ref=dst if src is None else src, dst_ref=dst,
            send_sem=self.send_sems.at[t, k], recv_sem=self.recv_sems.at[t, k],
            device_id=to, device_id_type=MESH)

    def _mine(self, t, me):
        return pltpu.make_async_copy(self.ins[t], self.outs[t].at[4 * me[0] + 2 * me[1] + me[2]],
                                     self.local_sems.at[t])

    def _first(self, t, me, sibling, chips, c):
        return [self._copy(t, 0, me, sibling, src=self.ins[t])] + \
               [self._copy(t, 1 + j, me, (*chip, c), src=self.ins[t]) for j, chip in enumerate(chips)]

    def start(self):
        me, sibling, chips, c = self._where()
        for t in range(self.n):
            self._mine(t, me).start()
            for cp in self._first(t, me, sibling, chips, c):
                cp.start()

    def middle(self):
        me, sibling, chips, c = self._where()
        for j, chip in enumerate(chips):
            for t in range(self.n):
                self._copy(t, 1 + j, (*chip, c), me).wait_recv()
                self._copy(t, 4 + j, (*chip, c), sibling).start()

    def finish(self):
        me, sibling, chips, c = self._where()
        for t in range(self.n):
            self._copy(t, 0, sibling, me).wait_recv()
            for j, chip in enumerate(chips):
                self._copy(t, 4 + j, (*chip, 1 - c), me).wait_recv()
        for t in range(self.n):
            for cp in self._first(t, me, sibling, chips, c):
                cp.wait_send()
            for j, chip in enumerate(chips):
                self._copy(t, 4 + j, (*chip, c), sibling).wait_send()
            self._mine(t, me).wait()


class _Exchange(_CommJob):
    def __init__(self, arrs):
        super().__init__(arrs)
        self.out_shape = [jax.ShapeDtypeStruct(a.shape, a.dtype) for a in arrs]

    def _copies(self, t):
        x, y, c = _me()
        my_idx = 4 * x + 2 * y + c
        pairs = []
        for k in range(1, N_DEV):
            peer = (x ^ ((k >> 2) & 1), y ^ ((k >> 1) & 1), c ^ (k & 1))
            peer_idx = 4 * peer[0] + 2 * peer[1] + peer[2]
            sems = dict(send_sem=self.send_sems.at[t, k - 1], recv_sem=self.recv_sems.at[t, k - 1],
                        device_id=peer, device_id_type=MESH)
            pairs.append((pltpu.make_async_remote_copy(src_ref=self.ins[t].at[peer_idx],
                                                       dst_ref=self.outs[t].at[my_idx], **sems),
                          pltpu.make_async_remote_copy(src_ref=self.ins[t].at[peer_idx],
                                                       dst_ref=self.outs[t].at[peer_idx], **sems)))
        return pairs

    def _mine(self, t):
        x, y, c = _me()
        my_idx = 4 * x + 2 * y + c
        return pltpu.make_async_copy(self.ins[t].at[my_idx], self.outs[t].at[my_idx], self.local_sems.at[t])

    def start(self):
        for t in range(self.n):
            self._mine(t).start()
            for snd, _ in self._copies(t):
                snd.start()

    def finish(self):
        for t in range(self.n):
            pairs = self._copies(t)
            for _, rcv in pairs:
                rcv.wait_recv()
            for snd, _ in pairs:
                snd.wait_send()
            self._mine(t).wait()


def _comm(job, *, name):
    def body(*refs):
        job.bind(refs)
        job.start()
        job.middle()
        job.finish()

    any_spec = pl.BlockSpec(memory_space=pl.ANY)
    return pl.pallas_call(
        body, name=name,
        in_specs=[any_spec] * job.n, out_specs=[any_spec] * job.n,
        out_shape=job.out_shape, scratch_shapes=job.scratch,
        compiler_params=pltpu.CompilerParams(has_side_effects=True),
    )(*job.arrs)


def _ride(job, refs, step, total):
    job.bind(refs)

    @pl.when(step == 0)
    def _():
        job.start()

    @pl.when(step == (total * 3) // 5)
    def _():
        job.middle()

    @pl.when(step == total - 1)
    def _():
        job.finish()


SCALE_A = DH ** -0.5
SCALE_B = (MLA_NOPE + MLA_ROPE) ** -0.5
SCALE_C = DH ** -0.5
DQK_B = MLA_NOPE + MLA_ROPE


def _rope_tables(L, half, width, rope_lane=None):
    pos = (jnp.arange(L) - PAD).astype(F32)
    inv = ROPE_THETA ** (-jnp.arange(half, dtype=F32) / half)
    lane = jnp.arange(width)
    ang = pos[:, None] * inv[lane % half][None, :]
    sign = jnp.where(lane % (2 * half) < half, -1.0, 1.0).astype(F32)
    cos, sin = jnp.cos(ang), jnp.sin(ang) * sign[None, :]
    if rope_lane is not None:
        cos = jnp.where(rope_lane[None, :], cos, 1.0)
        sin = jnp.where(rope_lane[None, :], sin, 0.0)
    return cos, sin


def _rope_lanes(x, cos, sin, half):
    W = x.shape[1]
    lane = lax.broadcasted_iota(jnp.int32, x.shape, 1)
    first = (lane & (2 * half - 1)) < half
    partner = jnp.where(first, pltpu.roll(x, W - half, 1), pltpu.roll(x, half, 1))
    return x * cos + partner * sin


def _tile_lanes(t, width):
    return t if t.shape[1] == width else jnp.tile(t, (1, width // t.shape[1]))


def _split(x, H, d, dst):
    for h in range(H):
        dst[h] = x[:, d * h:d * (h + 1)].astype(dst.dtype)


def _join(src, H):
    return jnp.concatenate([src[h] for h in range(H)], axis=1)


def _head_spec(H, tm, d):
    return pl.BlockSpec((H, tm, d), lambda i: (0, i, 0))


def _prep_a(proj, *, name):
    L = proj.shape[0]
    tm = ROW_T

    def body(x_ref, qo, ko, vo):
        _split(x_ref[:, 0:512] * SCALE_A, HEADS, DH, qo)
        _split(x_ref[:, 512:1024], HEADS, DH, ko)
        _split(x_ref[:, 1024:1536], HEADS, DH, vo)

    return pl.pallas_call(
        body, name=name, grid=(L // tm,),
        in_specs=[_col_spec(tm, 1536, QKVA)],
        out_specs=[_head_spec(HEADS, tm, DH)] * 3,
        out_shape=[jax.ShapeDtypeStruct((HEADS, L, DH), BF16)] * 3,
        compiler_params=_cparams(("parallel",)),
    )(proj)


def _unprep_a(dq, dk, dv, dproj, *, name):
    L = dq.shape[1]
    tm = ROW_T

    def body(dq_ref, dk_ref, dv_ref, _, dp_ref):
        dp_ref[:, 0:512] = (_join(dq_ref, HEADS) * SCALE_A).astype(BF16)
        dp_ref[:, 512:1024] = _join(dk_ref, HEADS).astype(BF16)
        dp_ref[:, 1024:1536] = _join(dv_ref, HEADS).astype(BF16)

    hs = _head_spec(HEADS, tm, DH)
    return pl.pallas_call(
        body, name=name, grid=(L // tm,),
        in_specs=[hs, hs, hs, pl.BlockSpec(memory_space=pl.ANY)],
        out_specs=_col_spec(tm, 1536, QKVA),
        out_shape=jax.ShapeDtypeStruct(dproj.shape, dproj.dtype),
        input_output_aliases={3: 0},
        compiler_params=_cparams(("parallel",)),
    )(dq, dk, dv, dproj)


def _prep_c(proj, tab, *, name):
    L = proj.shape[0]
    tm = ROW_T

    def body(x_ref, cos_ref, sin_ref, qo, ko, vo):
        cos, sin = cos_ref[...], sin_ref[...]
        q = _rope_lanes(x_ref[:, 0:512], _tile_lanes(cos, 512), _tile_lanes(sin, 512), DH // 2)
        _split(q * SCALE_C, HEADS, DH, qo)
        _split(_rope_lanes(x_ref[:, 512:640], cos, sin, DH // 2), SWA_KV_HEADS, DH, ko)
        _split(x_ref[:, 640:768], SWA_KV_HEADS, DH, vo)

    t128 = pl.BlockSpec((tm, LANES), lambda i: (i, 0))
    return pl.pallas_call(
        body, name=name, grid=(L // tm,),
        in_specs=[_col_spec(tm, 768, QKVC), t128, t128],
        out_specs=[_head_spec(HEADS, tm, DH), _head_spec(SWA_KV_HEADS, tm, DH), _head_spec(SWA_KV_HEADS, tm, DH)],
        out_shape=[jax.ShapeDtypeStruct((HEADS, L, DH), BF16), jax.ShapeDtypeStruct((SWA_KV_HEADS, L, DH), BF16),
                   jax.ShapeDtypeStruct((SWA_KV_HEADS, L, DH), BF16)],
        compiler_params=_cparams(("parallel",)),
    )(proj, *tab)


def _unprep_c(dq, dk, dv, tab, dproj, *, name):
    L = dq.shape[1]
    tm = ROW_T

    def body(dq_ref, dk_ref, dv_ref, cos_ref, sin_ref, _, dp_ref):
        cos, nsin = cos_ref[...], -sin_ref[...]
        dqv = _join(dq_ref, HEADS) * SCALE_C
        dp_ref[:, 0:512] = _rope_lanes(dqv, _tile_lanes(cos, 512), _tile_lanes(nsin, 512), DH // 2).astype(BF16)
        dp_ref[:, 512:640] = _rope_lanes(_join(dk_ref, SWA_KV_HEADS), cos, nsin, DH // 2).astype(BF16)
        dp_ref[:, 640:768] = _join(dv_ref, SWA_KV_HEADS).astype(BF16)

    hs = _head_spec(HEADS, tm, DH)
    hkv = _head_spec(SWA_KV_HEADS, tm, DH)
    t128 = pl.BlockSpec((tm, LANES), lambda i: (i, 0))
    return pl.pallas_call(
        body, name=name, grid=(L // tm,),
        in_specs=[hs, hkv, hkv, t128, t128, pl.BlockSpec(memory_space=pl.ANY)],
        out_specs=_col_spec(tm, 768, QKVC),
        out_shape=jax.ShapeDtypeStruct(dproj.shape, dproj.dtype),
        input_output_aliases={5: 0},
        compiler_params=_cparams(("parallel",)),
    )(dq, dk, dv, *tab, dproj)


def _prep_b(qbm, kvbm, proj, tab_q, tab_k, *, name):
    L = proj.shape[0]
    tm = ROW_T

    def body(q_ref, kv_ref, kr_ref, cq_ref, sq_ref, ck_ref, sk_ref, qo, ko, vo):
        q = _rope_lanes(q_ref[...], cq_ref[...], sq_ref[...], MLA_ROPE // 2) * SCALE_B
        _split(q, HEADS, DQK_B, qo)
        kr = _rope_lanes(kr_ref[...], ck_ref[...], sk_ref[...], MLA_ROPE // 2)[:, :MLA_ROPE].astype(BF16)
        kv = kv_ref[...]
        for h in range(HEADS):
            ko[h] = jnp.concatenate([kv[:, 128 * h:128 * h + MLA_NOPE].astype(BF16), kr], axis=1)
            vo[h] = kv[:, 128 * h + MLA_NOPE:128 * (h + 1)].astype(BF16)

    t128 = pl.BlockSpec((tm, LANES), lambda i: (i, 0))
    t768 = pl.BlockSpec((tm, 768), lambda i: (i, 0))
    return pl.pallas_call(
        body, name=name, grid=(L // tm,),
        in_specs=[t768, pl.BlockSpec((tm, 1024), lambda i: (i, 0)), _col_spec(tm, 128, KR), t768, t768, t128, t128],
        out_specs=[_head_spec(HEADS, tm, DQK_B), _head_spec(HEADS, tm, DQK_B), _head_spec(HEADS, tm, DH)],
        out_shape=[jax.ShapeDtypeStruct((HEADS, L, DQK_B), BF16), jax.ShapeDtypeStruct((HEADS, L, DQK_B), BF16),
                   jax.ShapeDtypeStruct((HEADS, L, DH), BF16)],
        compiler_params=_cparams(("parallel",)),
    )(qbm, kvbm, proj, *tab_q, *tab_k)


def _unprep_b(dq, dk, dv, tab_q, tab_k, dproj, *, name):
    L = dq.shape[1]
    tm = ROW_T

    def body(dq_ref, dk_ref, dv_ref, cq_ref, sq_ref, ck_ref, sk_ref, _, dp_kr, dqo, dkvo):
        dqv = _join(dq_ref, HEADS) * SCALE_B
        dqo[...] = _rope_lanes(dqv, cq_ref[...], -sq_ref[...], MLA_ROPE // 2).astype(BF16)
        parts = []
        dkr = None
        for h in range(HEADS):
            dkh = dk_ref[h]
            parts += [dkh[:, :MLA_NOPE], dv_ref[h]]
            r = dkh[:, MLA_NOPE:]
            dkr = r if dkr is None else dkr + r
        dkvo[...] = jnp.concatenate(parts, axis=1).astype(BF16)
        dkr = jnp.concatenate([dkr, jnp.zeros((tm, LANES - MLA_ROPE), F32)], axis=1)
        dp_kr[...] = _rope_lanes(dkr, ck_ref[...], -sk_ref[...], MLA_ROPE // 2).astype(BF16)

    t128 = pl.BlockSpec((tm, LANES), lambda i: (i, 0))
    t768 = pl.BlockSpec((tm, 768), lambda i: (i, 0))
    hq = _head_spec(HEADS, tm, DQK_B)
    return pl.pallas_call(
        body, name=name, grid=(L // tm,),
        in_specs=[hq, hq, _head_spec(HEADS, tm, DH), t768, t768, t128, t128, pl.BlockSpec(memory_space=pl.ANY)],
        out_specs=[_col_spec(tm, 128, KR), t768, pl.BlockSpec((tm, 1024), lambda i: (i, 0))],
        out_shape=[jax.ShapeDtypeStruct(dproj.shape, dproj.dtype), jax.ShapeDtypeStruct((L, 768), BF16),
                   jax.ShapeDtypeStruct((L, 1024), BF16)],
        input_output_aliases={7: 0},
        compiler_params=_cparams(("parallel",)),
    )(dq, dk, dv, *tab_q, *tab_k, dproj)


def _gate(y, proj, zcol, *, name):
    L = proj.shape[0]
    tm = ROW_T

    def body(y_ref, z_ref, u_ref):
        z = z_ref[...]
        u_ref[...] = (_join(y_ref, HEADS) * (z * jax.nn.sigmoid(z))).astype(BF16)

    return pl.pallas_call(
        body, name=name, grid=(L // tm,),
        in_specs=[_head_spec(HEADS, tm, DH), _col_spec(tm, 512, zcol)],
        out_specs=pl.BlockSpec((tm, 512), lambda i: (i, 0)),
        out_shape=jax.ShapeDtypeStruct((L, 512), BF16),
        compiler_params=_cparams(("parallel",)),
    )(y, proj)


def _gate_bwd(du, y, proj, zcol, dproj, *, name):
    L = proj.shape[0]
    tm = ROW_T

    def body(du_ref, y_ref, z_ref, _, dz_ref, dy_ref):
        z = z_ref[...]
        duv = du_ref[...]
        sg = jax.nn.sigmoid(z)
        _split(duv * (z * sg), HEADS, DH, dy_ref)
        dz_ref[...] = (duv * _join(y_ref, HEADS) * (sg * (1.0 + z * (1.0 - sg)))).astype(BF16)

    hs = _head_spec(HEADS, tm, DH)
    return pl.pallas_call(
        body, name=name, grid=(L // tm,),
        in_specs=[pl.BlockSpec((tm, 512), lambda i: (i, 0)), hs, _col_spec(tm, 512, zcol),
                  pl.BlockSpec(memory_space=pl.ANY)],
        out_specs=[_col_spec(tm, 512, zcol), hs],
        out_shape=[jax.ShapeDtypeStruct(dproj.shape, dproj.dtype), jax.ShapeDtypeStruct((HEADS, L, DH), F32)],
        input_output_aliases={3: 0},
        compiler_params=_cparams(("parallel",)),
    )(du, y, proj, dproj)


MERGE_T = 192


def _merge(proj, pbs, *, name):
    L = proj.shape[0]
    tm = MERGE_T

    def body(g0, g1, g2, p0, p1, p2, o_ref):
        acc = None
        for g_ref, p_ref in ((g0, p0), (g1, p1), (g2, p2)):
            t = jax.nn.sigmoid(g_ref[...]) * p_ref[...]
            acc = t if acc is None else acc + t
        o_ref[...] = acc.astype(BF16)

    row = pl.BlockSpec((tm, D_MODEL), lambda i: (i, 0))
    return pl.pallas_call(
        body, name=name, grid=(L // tm,),
        in_specs=[_col_spec(tm, D_MODEL, GATES + n * D_MODEL) for n in range(N_BRANCH)] + [row] * N_BRANCH,
        out_specs=row, out_shape=jax.ShapeDtypeStruct((L, D_MODEL), BF16),
        compiler_params=_cparams(("parallel",)),
    )(proj, proj, proj, *pbs)


def _merge_bwd(dmerged, proj, pbs, dproj, *, name):
    L = proj.shape[0]
    tm = MERGE_T

    def body(dm_ref, g_ref, p0, p1, p2, _, dg_ref, dp0, dp1, dp2):
        dm = dm_ref[...]
        for n, (p_ref, dp_ref) in enumerate(((p0, dp0), (p1, dp1), (p2, dp2))):
            cols = slice(n * D_MODEL, (n + 1) * D_MODEL)
            sg = jax.nn.sigmoid(g_ref[:, cols])
            dp_ref[...] = (dm * sg).astype(BF16)
            dg_ref[:, cols] = (dm * p_ref[...] * (sg * (1.0 - sg))).astype(BF16)

    row = pl.BlockSpec((tm, D_MODEL), lambda i: (i, 0))
    gates = _col_spec(tm, N_BRANCH * D_MODEL, GATES)
    outs = pl.pallas_call(
        body, name=name, grid=(L // tm,),
        in_specs=[row, gates] + [row] * N_BRANCH + [pl.BlockSpec(memory_space=pl.ANY)],
        out_specs=[gates] + [row] * N_BRANCH,
        out_shape=[jax.ShapeDtypeStruct(dproj.shape, dproj.dtype)]
        + [jax.ShapeDtypeStruct((L, D_MODEL), BF16)] * N_BRANCH,
        input_output_aliases={5: 0},
        compiler_params=_cparams(("parallel",)),
    )(dmerged, proj, *pbs, dproj)
    return outs[0], outs[1:]


def _forget_bias(af, b_f):
    return jnp.cumsum(jax.nn.log_sigmoid(af + b_f), axis=0).T


def _key_bias(L, T, ct=None):
    padb = jnp.where(jnp.arange(L) < PAD, BIG, 0.0).astype(F32)[None]
    kb = padb if ct is None else ct + padb
    return kb.reshape(kb.shape[0], L // T, 1, T)


def _layer_fwd(h, w, tabs, l, side=None):
    tag = f"l{l}"
    L = h.shape[0]
    hn = _rms_fwd(h, w["norm_g"], name=f"{tag}_rms_in")
    if "late" in w:
        late_job, assemble = w["late"]
        proj, *late = _mm(hn, w["w_in"], side=late_job, name=f"{tag}_mm_in")
        w = {**{key: val for key, val in w.items() if key != "late"}, **assemble(*late)}
    else:
        proj = _mm(hn, w["w_in"], name=f"{tag}_mm_in")
    ct, vjp_f = jax.vjp(_forget_bias, proj[:, FA:FA + HEADS], w["b_f"])
    ops_a = (*_prep_a(proj, name=f"{tag}_prep_a"), _key_bias(L, ATT_T, ct))
    ya, lsea, *side_out = _attn_fwd(*ops_a, T=ATT_T, side=side, name=f"{tag}_attn_a")
    cqn = _rms_fwd(proj, w["g_cq"], col=CQ, name=f"{tag}_rms_cq")
    ckvn = _rms_fwd(proj, w["g_ckv"], col=CKV, name=f"{tag}_rms_ckv")
    qbm = _mm(cqn, w["w_uq"], name=f"{tag}_mm_uq")
    kvbm = _mm(ckvn, w["w_ukv"], name=f"{tag}_mm_ukv")
    ops_b = (*_prep_b(qbm, kvbm, proj, tabs["bq"], tabs["bk"], name=f"{tag}_prep_b"), _key_bias(L, ATT_T))
    yb, lseb = _attn_fwd(*ops_b, T=ATT_T, name=f"{tag}_attn_b")
    ops_c = (*_prep_c(proj, tabs["c"], name=f"{tag}_prep_c"), _key_bias(L, WINDOW))
    sink = jnp.broadcast_to(w["sinks"][:, None, None], (HEADS, 1, LANES))
    yc, lsec = _attn_fwd(*ops_c, T=WINDOW, window=True, sink=sink, name=f"{tag}_attn_c")
    us = [_gate(y, proj, zcol, name=f"{tag}_gate{n}") for n, (y, zcol) in enumerate(((ya, ZA), (yb, ZB), (yc, ZC)))]
    pbr = [_mm(us[n], w["w_branch"][n], name=f"{tag}_mm_br{n}") for n in range(N_BRANCH)]
    merged = _merge(proj, pbr, name=f"{tag}_merge")
    out = _mm(merged, w["w_out"], add=h, name=f"{tag}_mm_out")
    saved = dict(h=h, hn=hn, proj=proj, vjp_f=vjp_f, ops_a=ops_a, ya=ya, lsea=lsea, cqn=cqn, ckvn=ckvn,
                 ops_b=ops_b, yb=yb, lseb=lseb, ops_c=ops_c, yc=yc, lsec=lsec, us=us, pbr=pbr, merged=merged)
    return out, saved, side_out, w


def _w_in_chunks(g):
    return _cols_to_shards(_unpad_in(g["w_in"]))


def _grad_chunks(g, with_w_in=True):
    rest = [_cols_to_shards(g["w_uq"]), _cols_to_shards(g["w_ukv"]),
            jnp.stack([_cols_to_shards(g["w_branch"][n]) for n in range(N_BRANCH)], axis=1),
            g["w_out"].reshape(N_DEV, D_MODEL // N_DEV, D_MODEL)]
    return ([_w_in_chunks(g)] if with_w_in else []) + rest


def _layer_bwd(dout, s, w, tabs, l, side=None, own_exchange=False):
    tag = f"l{l}"
    L = dout.shape[0]
    proj = s["proj"]
    g = {}
    dproj = jnp.zeros((L, NP_IN), BF16)
    dmerged = _mm(dout, w["w_out"], tb=True, name=f"{tag}_mm_out_dx")
    g["w_out"] = _mm(s["merged"], dout, ta=True, out_dtype=BF16, name=f"{tag}_mm_out_dw")
    dproj, dpbr = _merge_bwd(dmerged, proj, s["pbr"], dproj, name=f"{tag}_merge_bwd")
    dus = [_mm(dpbr[n], w["w_branch"][n], tb=True, name=f"{tag}_mm_br{n}_dx") for n in range(N_BRANCH)]
    g["w_branch"] = jnp.stack([_mm(s["us"][n], dpbr[n], ta=True, out_dtype=BF16, name=f"{tag}_mm_br{n}_dw")
                               for n in range(N_BRANCH)])
    dys = []
    for n, (y, zcol) in enumerate(((s["ya"], ZA), (s["yb"], ZB), (s["yc"], ZC))):
        dproj, dy = _gate_bwd(dus[n], y, proj, zcol, dproj, name=f"{tag}_gate{n}_bwd")
        dys.append(dy)
    dya, dyb, dyc = dys
    def bwd_operands(ops, y, lse, dy):
        q16, k16, v16, kbias = ops
        as_rows = lambda t: t.reshape(HEADS, L // ATT_T, 1, ATT_T)
        return (q16, k16, v16, kbias.reshape(kbias.shape[0], L, 1), as_rows(lse), as_rows(jnp.sum(dy * y, axis=-1)),
                dy.astype(BF16))

    dqa, dka, dva, dcq, dck, *side_out = _attn_bwd_t(*bwd_operands(s["ops_a"], s["ya"], s["lsea"], dya), T=ATT_T,
                                                     fox=True, side=side, name=f"{tag}_attn_a_bwd")
    dproj = _unprep_a(dqa, dka, dva, dproj, name=f"{tag}_unprep_a")
    daf, g["b_f"] = s["vjp_f"](dcq.reshape(HEADS, L) + dck[:, :, 0])
    dproj = lax.dynamic_update_slice(dproj, daf.astype(BF16), (0, FA))
    dqb, dkb, dvb, _, _ = _attn_bwd_t(*bwd_operands(s["ops_b"], s["yb"], s["lseb"], dyb), T=ATT_T,
                                      name=f"{tag}_attn_b_bwd")
    dproj, dqbm, dkvbm = _unprep_b(dqb, dkb, dvb, tabs["bq"], tabs["bk"], dproj, name=f"{tag}_unprep_b")
    dcqn = _mm(dqbm, w["w_uq"], tb=True, name=f"{tag}_mm_uq_dx")
    g["w_uq"] = _mm(s["cqn"], dqbm, ta=True, out_dtype=BF16, name=f"{tag}_mm_uq_dw")
    dckvn = _mm(dkvbm, w["w_ukv"], tb=True, name=f"{tag}_mm_ukv_dx")
    g["w_ukv"] = _mm(s["ckvn"], dkvbm, ta=True, out_dtype=BF16, name=f"{tag}_mm_ukv_dw")
    dproj, g["g_cq"] = _rms_bwd(dcqn, proj, w["g_cq"], col=CQ, into=dproj, name=f"{tag}_rms_cq_bwd")
    dproj, g["g_ckv"] = _rms_bwd(dckvn, proj, w["g_ckv"], col=CKV, into=dproj, name=f"{tag}_rms_ckv_bwd")
    dqc, dkc, dvc = _attn_bwd_window(*s["ops_c"], s["yc"], s["lsec"], dyc, name=f"{tag}_attn_c_bwd")
    dproj = _unprep_c(dqc, dkc, dvc, tabs["c"], dproj, name=f"{tag}_unprep_c")
    delta_c = jnp.sum(dyc * s["yc"], axis=-1)
    g["sinks"] = -jnp.sum(jnp.exp(w["sinks"][:, None] - s["lsec"][:, :, 0]) * delta_c, axis=1)
    own_out = []
    if own_exchange:
        g["w_in"], *r_rest = _mm(s["hn"], dproj, ta=True, out_dtype=BF16, side=_Exchange(_grad_chunks(g, False)),
                                 name=f"{tag}_mm_in_dw")
        dhn, r_in = _mm(dproj, w["w_in"], tb=True, side=_Exchange([_w_in_chunks(g)]), name=f"{tag}_mm_in_dx")
        own_out = [r_in, *r_rest]
    else:
        g["w_in"] = _mm(s["hn"], dproj, ta=True, out_dtype=BF16, name=f"{tag}_mm_in_dw")
        dhn = _mm(dproj, w["w_in"], tb=True, name=f"{tag}_mm_in_dx")
    dh, g["norm_g"] = _rms_bwd(dhn, s["h"], w["norm_g"], add=dout, name=f"{tag}_rms_in_bwd")
    return dh, g, side_out, own_out


def _cols_from_shards(g):
    return jnp.moveaxis(g, 0, 1).reshape(g.shape[1], N_DEV * g.shape[2])


def _cols_to_shards(w):
    R = w.shape[0]
    return jnp.moveaxis(w.reshape(R, N_DEV, w.shape[1] // N_DEV), 1, 0)


def _pad_in(w):
    parts, at = [], 0
    for lo, hi, dst in sorted(_RUNS, key=lambda r: r[2]):
        if dst > at:
            parts.append(jnp.zeros((w.shape[0], dst - at), w.dtype))
        parts.append(w[:, lo:hi])
        at = dst + hi - lo
    parts.append(jnp.zeros((w.shape[0], NP_IN - at), w.dtype))
    return jnp.concatenate(parts, axis=1)


def _unpad_in(w):
    return jnp.concatenate([w[:, dst:dst + hi - lo] for lo, hi, dst in _RUNS], axis=1)


_SMALL = (("norm_g", DEPTH * D_MODEL), ("b_f", DEPTH * HEADS), ("g_cq", DEPTH * MLA_QLORA),
          ("g_ckv", DEPTH * MLA_KVLORA), ("sinks", DEPTH * HEADS), ("final_g", D_MODEL), ("loss", 1),
          ("meta", N_META * D_MODEL))
SMALL_ROWS = 168


def _pack_small(d):
    parts = []
    for name, size in _SMALL:
        padded = -(-size // 128) * 128
        v = d[name].reshape(-1).astype(F32) if name in d else jnp.zeros((size,), F32)
        parts.append(jnp.pad(v, (0, padded - size)))
    flat = jnp.concatenate(parts)
    return jnp.pad(flat, (0, SMALL_ROWS * 128 - flat.shape[0])).reshape(SMALL_ROWS, 128)


def _unpack_small(p, shapes):
    flat = p.reshape(-1)
    out, at = {}, 0
    for name, size in _SMALL:
        if name in shapes:
            out[name] = flat[at:at + size].reshape(shapes[name])
        at += -(-size // 128) * 128
    return out


def kernel(x, meta_tokens, norm_g, w_in, b_f, g_cq, g_ckv, w_uq, w_ukv, sinks, w_branch, w_out, final_g, loss_target, m_meta_tokens, m_norm_g, m_w_in, m_b_f, m_g_cq, m_g_ckv, m_w_uq, m_w_ukv, m_sinks, m_w_branch, m_w_out, m_final_g, v_meta_tokens, v_norm_g, v_w_in, v_b_f, v_g_cq, v_g_ckv, v_w_uq, v_w_ukv, v_sinks, v_w_branch, v_w_out, v_final_g):
    S = x.shape[1]
    L = BLK + S
    cx, cy, cc = _me()
    my_idx = 4 * cx + 2 * cy + cc

    def shards(l):
        return [t[l].astype(BF16) for t in (w_in, w_uq, w_ukv, w_branch, w_out)]

    def small_weights(l):
        return dict(norm_g=norm_g[l], b_f=b_f[l], g_cq=g_cq[l], g_ckv=g_ckv[l], sinks=sinks[l])

    def rest_weights(gw_uq, gw_ukv, gw_br, gw_out):
        return dict(w_uq=_cols_from_shards(gw_uq), w_ukv=_cols_from_shards(gw_ukv),
                    w_branch=jnp.stack([_cols_from_shards(gw_br[:, n]) for n in range(N_BRANCH)]),
                    w_out=gw_out.reshape(D_MODEL, D_MODEL))

    def layer_weights(l, gw_in, *gw_rest):
        return dict(small_weights(l), w_in=_pad_in(_cols_from_shards(gw_in)), **rest_weights(*gw_rest))

    gw_in0, g_meta = _comm(_Gather([shards(0)[0], meta_tokens]), name="gather_l0")
    layers = [dict(small_weights(0), w_in=_pad_in(_cols_from_shards(gw_in0)),
                   late=(_Gather(shards(0)[1:]), rest_weights)), None]
    meta_full = _cols_from_shards(g_meta)

    h = jnp.concatenate([jnp.zeros((PAD, D_MODEL), F32), meta_full, x[0]], axis=0)
    q_lanes = jnp.arange(HEADS * DQK_B)
    tabs = dict(c=_rope_tables(L, DH // 2, LANES), bk=_rope_tables(L, MLA_ROPE // 2, LANES),
                bq=_rope_tables(L, MLA_ROPE // 2, HEADS * DQK_B, rope_lane=(q_lanes % DQK_B) >= MLA_NOPE))
    saved = [None] * DEPTH
    h, saved[0], gw1, layers[0] = _layer_fwd(h, layers[0], tabs, 0, side=_Gather(shards(1)))
    layers[1] = layer_weights(1, *gw1)
    h, saved[1], _, _ = _layer_fwd(h, layers[1], tabs, 1)
    loss_vec, dh, g_final = _loss_head(h, final_g, loss_target[0], name="loss_head")

    grads = [None] * DEPTH
    dh, grads[1], _, _ = _layer_bwd(dh, saved[1], layers[1], tabs, 1)
    dh, grads[0], recv1, recv0 = _layer_bwd(dh, saved[0], layers[0], tabs, 0, side=_Exchange(_grad_chunks(grads[1])),
                                            own_exchange=True)
    r_in, r_uq, r_ukv, r_br, r_out = (jnp.stack([a, b], axis=1) for a, b in zip(recv0, recv1))

    def stack(name):
        return jnp.stack([grads[l][name] for l in range(DEPTH)])

    small = _pack_small(dict(norm_g=stack("norm_g"), b_f=stack("b_f"), g_cq=stack("g_cq"), g_ckv=stack("g_ckv"),
                             sinks=stack("sinks"), final_g=g_final, loss=loss_vec[0, 0:1],
                             meta=dh[PAD:BLK]))
    (g_small,) = _comm(_Gather([small]), name="gather_small")

    def adam_big(w_, parts, m_, v_, name):
        shape = w_.shape
        C = shape[-1]
        R = math.prod(shape[:-1])
        outs = _adamw(w_.reshape(R, C), parts.reshape(parts.shape[0], R, C), m_.reshape(R, C), v_.reshape(R, C),
                      name=name)
        return [o.reshape(shape) for o in outs]

    res = {}
    res["w_in"] = adam_big(w_in, r_in, m_w_in, v_w_in, "adam_w_in")
    res["w_uq"] = adam_big(w_uq, r_uq, m_w_uq, v_w_uq, "adam_w_uq")
    res["w_ukv"] = adam_big(w_ukv, r_ukv, m_w_ukv, v_w_ukv, "adam_w_ukv")
    res["w_branch"] = adam_big(w_branch, r_br, m_w_branch, v_w_branch, "adam_w_branch")
    res["w_out"] = adam_big(w_out, r_out, m_w_out, v_w_out, "adam_w_out")

    small_w = dict(norm_g=norm_g, b_f=b_f, g_cq=g_cq, g_ckv=g_ckv, sinks=sinks, final_g=final_g)
    small_m = dict(norm_g=m_norm_g, b_f=m_b_f, g_cq=m_g_cq, g_ckv=m_g_ckv, sinks=m_sinks, final_g=m_final_g)
    small_v = dict(norm_g=v_norm_g, b_f=v_b_f, g_cq=v_g_cq, g_ckv=v_g_ckv, sinks=v_sinks, final_g=v_final_g)
    sm = _adamw(_pack_small(small_w), g_small, _pack_small(small_m), _pack_small(small_v), name="adam_small")
    shapes = {k: a.shape for k, a in small_w.items()}
    shapes_all = dict(shapes, loss=(), meta=(N_META, D_MODEL))
    sm_g = _unpack_small(sm[0], shapes_all)
    sm_d, sm_m, sm_v = (_unpack_small(t, shapes) for t in sm[1:])
    for k in shapes:
        res[k] = [sm_g[k], sm_d[k], sm_m[k], sm_v[k]]
    g_meta_mine = lax.dynamic_slice(sm_g["meta"], (0, my_idx * 128), (N_META, 128))
    res["meta_tokens"] = _adamw(meta_tokens, g_meta_mine[None], m_meta_tokens, v_meta_tokens, name="adam_meta")

    order = ["meta_tokens", "norm_g", "w_in", "b_f", "g_cq", "g_ckv", "w_uq", "w_ukv", "sinks", "w_branch", "w_out",
             "final_g"]
    grad_x = dh[BLK:][None]
    return (sm_g["loss"], grad_x, *[res[k][0] for k in order], *[res[k][1] for k in order],
            *[res[k][2] for k in order], *[res[k][3] for k in order])
```

```python
import functools
import math

import jax
import jax.numpy as jnp
from jax import lax
from jax.experimental import pallas as pl
from jax.experimental.pallas import tpu as pltpu

F32 = jnp.float32
BF16 = jnp.bfloat16

D_MODEL = 1024
DEPTH = 2
N_META = 16
BLK = 128
PAD = BLK - N_META
ROPE_THETA = 10000.0
EPS = 1e-6
NEG = -1e30
BIG = 1e30
HEADS = 8
DH = 64
MLA_NOPE = 64
MLA_ROPE = 32
MLA_QLORA = 384
MLA_KVLORA = 256
SWA_KV_HEADS = 2
WINDOW = 128
BRANCH_W = 512
N_BRANCH = 3
N_IN = 7592

ADAM_LR = 0.001
ADAM_B1 = 0.9
ADAM_B2 = 0.999
ADAM_EPS = 1e-08
ADAM_WD = 0.01
ADAM_STEP = 10

N_DEV = 8
MESH = pl.DeviceIdType.MESH

NP_IN = 8192
QKVA, ZA, FA = 0, 1536, 2048
CKV, KR, CQ = 2304, 2560, 2688
GATES = 3072
ZB = 6144
QKVC, ZC = 6912, 7680
_RUNS = ((0, 1536, QKVA), (1536, 1544, FA), (1544, 2056, ZA), (2056, 2440, CQ), (2440, 2696, CKV), (2696, 2728, KR),
         (2728, 3240, ZB), (3240, 4008, QKVC), (4008, 4520, ZC), (4520, 7592, GATES))

VMEM_LIMIT = 48 * 1024 * 1024
ATT_T = 384
ATT_HP = 1
ROW_T = 384
LANES = 128


def _pick(dim, prefs):
    for p in prefs:
        if dim % p == 0:
            return p
    return dim


def _cparams(sem):
    return pltpu.CompilerParams(dimension_semantics=sem, vmem_limit_bytes=VMEM_LIMIT)


def _side_cparams(side, sem):
    if side is None:
        return _cparams(sem)
    return pltpu.CompilerParams(dimension_semantics=("arbitrary",) * len(sem), vmem_limit_bytes=VMEM_LIMIT,
                                has_side_effects=True)


def _mm(a, b, *, ta=False, tb=False, add=None, out_dtype=F32, side=None, name):
    M = a.shape[1] if ta else a.shape[0]
    K = a.shape[0] if ta else a.shape[1]
    N = b.shape[0] if tb else b.shape[1]
    assert K == (b.shape[1] if tb else b.shape[0])
    tm = _pick(M, (704, 1024, 512, 384, 256, 128))
    tn = _pick(N, (1024, 768, 512, 384, 256, 128))
    tk = _pick(K, (2048, 1408, 1024, 768, 512, 384, 256, 128))
    nk = K // tk
    dims = (((0 if ta else 1,), (1 if tb else 0,)), ((), ()))

    sn = 0 if side is None else side.n
    n_in = 2 + (add is not None)
    grid = (M // tm, N // tn, nk)

    def body(*refs):
        a_ref, b_ref = refs[:2]
        c_ref = refs[2] if add is not None else None
        o_ref = refs[n_in + sn]
        scratch = refs[n_in + 2 * sn + 1:]
        if side is not None:
            step = (pl.program_id(0) * grid[1] + pl.program_id(1)) * nk + pl.program_id(2)
            _ride(side, [*refs[n_in:n_in + sn], *refs[n_in + sn + 1:n_in + 2 * sn + 1], *scratch[nk > 1:]],
                  step, grid[0] * grid[1] * nk)
        r = lax.dot_general(a_ref[...].astype(BF16), b_ref[...].astype(BF16), dims, preferred_element_type=F32)

        def finish(total):
            if c_ref is not None:
                total = total + c_ref[...]
            o_ref[...] = total.astype(out_dtype)

        if nk == 1:
            finish(r)
        else:
            acc = scratch[0]
            k = pl.program_id(2)

            @pl.when(k == 0)
            def _():
                acc[...] = r

            @pl.when(k > 0)
            def _():
                acc[...] += r

            @pl.when(k == nk - 1)
            def _():
                finish(acc[...])

    a_spec = pl.BlockSpec((tk, tm), lambda i, j, k: (k, i)) if ta else pl.BlockSpec((tm, tk), lambda i, j, k: (i, k))
    b_spec = pl.BlockSpec((tn, tk), lambda i, j, k: (j, k)) if tb else pl.BlockSpec((tk, tn), lambda i, j, k: (k, j))
    o_spec = pl.BlockSpec((tm, tn), lambda i, j, k: (i, j))
    any_spec = pl.BlockSpec(memory_space=pl.ANY)
    outs = pl.pallas_call(
        body, name=name,
        grid=grid,
        in_specs=[a_spec, b_spec] + ([o_spec] if add is not None else []) + [any_spec] * sn,
        out_specs=[o_spec] + [any_spec] * sn,
        out_shape=[jax.ShapeDtypeStruct((M, N), out_dtype)] + ([] if side is None else side.out_shape),
        scratch_shapes=([pltpu.VMEM((tm, tn), F32)] if nk > 1 else []) + ([] if side is None else side.scratch),
        compiler_params=_side_cparams(side, ("parallel", "parallel", "arbitrary")),
    )(*((a, b) if add is None else (a, b, add)), *([] if side is None else side.arrs))
    return outs[0] if side is None else outs


def _col_spec(tm, width, col):
    assert col % width == 0
    return pl.BlockSpec((tm, width), lambda i, _c=col // width: (i, _c))


def _rms_fwd(x, g, *, col=0, name):
    L = x.shape[0]
    D = g.shape[0]
    tm = ROW_T

    def body(x_ref, g_ref, y_ref):
        xv = x_ref[...]
        rstd = lax.rsqrt(jnp.mean(xv * xv, axis=-1, keepdims=True) + EPS)
        y_ref[...] = (xv * rstd * g_ref[...]).astype(BF16)

    return pl.pallas_call(
        body, name=name, grid=(L // tm,),
        in_specs=[_col_spec(tm, D, col), pl.BlockSpec((1, D), lambda i: (0, 0))],
        out_specs=pl.BlockSpec((tm, D), lambda i: (i, 0)),
        out_shape=jax.ShapeDtypeStruct((L, D), BF16),
        compiler_params=_cparams(("parallel",)),
    )(x, g.reshape(1, D))


def _rms_bwd(dy, x, g, *, col=0, add=None, into=None, name):
    L = x.shape[0]
    D = g.shape[0]
    tm = ROW_T

    def body(*refs):
        dy_ref, x_ref, g_ref = refs[:3]
        add_ref = refs[3] if add is not None else None
        dx_ref, dg_ref = refs[-2:]
        i = pl.program_id(0)
        xv = x_ref[...]
        dyv = dy_ref[...]
        rstd = lax.rsqrt(jnp.mean(xv * xv, axis=-1, keepdims=True) + EPS)
        xhat = xv * rstd
        part = jnp.sum(dyv * xhat, axis=0, keepdims=True)

        @pl.when(i == 0)
        def _():
            dg_ref[...] = part

        @pl.when(i > 0)
        def _():
            dg_ref[...] += part

        dxh = dyv * g_ref[...]
        dx = rstd * (dxh - xhat * jnp.mean(dxh * xhat, axis=-1, keepdims=True))
        if add_ref is not None:
            dx = dx + add_ref[...]
        dx_ref[...] = dx.astype(dx_ref.dtype)

    row = pl.BlockSpec((tm, D), lambda i: (i, 0))
    in_specs = [row, _col_spec(tm, D, col), pl.BlockSpec((1, D), lambda i: (0, 0))]
    args = [dy, x, g.reshape(1, D)]
    aliases = {}
    if add is not None:
        in_specs.append(row)
        args.append(add)
    if into is not None:
        in_specs.append(pl.BlockSpec(memory_space=pl.ANY))
        args.append(into)
        aliases = {len(args) - 1: 0}
        dx_spec, dx_shape = _col_spec(tm, D, col), jax.ShapeDtypeStruct(into.shape, into.dtype)
    else:
        dx_spec, dx_shape = row, jax.ShapeDtypeStruct((L, D), F32)
    dx, dg = pl.pallas_call(
        body, name=name, grid=(L // tm,),
        in_specs=in_specs,
        out_specs=[dx_spec, pl.BlockSpec((1, D), lambda i: (0, 0))],
        out_shape=[dx_shape, jax.ShapeDtypeStruct((1, D), F32)],
        input_output_aliases=aliases,
        compiler_params=_cparams(("arbitrary",)),
    )(*args)
    return dx, dg.reshape(D)


def _loss_head(h, g, target, *, name):
    L, D = h.shape
    nb = L // BLK

    def body(h_ref, g_ref, t_ref, loss_ref, dh_ref, dg_ref):
        i = pl.program_id(0)

        @pl.when(i == 0)
        def _():
            loss_ref[...] = jnp.zeros_like(loss_ref)
            dg_ref[...] = jnp.zeros_like(dg_ref)
            dh_ref[...] = jnp.zeros_like(dh_ref)

        @pl.when(i > 0)
        def _():
            xv = h_ref[...]
            gv = g_ref[...]
            rstd = lax.rsqrt(jnp.mean(xv * xv, axis=-1, keepdims=True) + EPS)
            xhat = xv * rstd
            err = xhat * gv - t_ref[...]
            row = jnp.mean(err * err, axis=-1, keepdims=True)
            loss_ref[...] += 0.5 * jnp.sum(row, axis=0, keepdims=True)
            dy = err * (1.0 / D)
            dg_ref[...] += jnp.sum(dy * xhat, axis=0, keepdims=True)
            dxh = dy * gv
            dh_ref[...] = rstd * (dxh - xhat * jnp.mean(dxh * xhat, axis=-1, keepdims=True))

    loss, dh, dg = pl.pallas_call(
        body, name=name, grid=(nb,),
        in_specs=[pl.BlockSpec((BLK, D), lambda i: (i, 0)), pl.BlockSpec((1, D), lambda i: (0, 0)),
                  pl.BlockSpec((BLK, D), lambda i: (jnp.maximum(i - 1, 0), 0))],
        out_specs=[pl.BlockSpec((1, 128), lambda i: (0, 0)), pl.BlockSpec((BLK, D), lambda i: (i, 0)),
                   pl.BlockSpec((1, D), lambda i: (0, 0))],
        out_shape=[jax.ShapeDtypeStruct((1, 128), F32), jax.ShapeDtypeStruct((L, D), F32),
                   jax.ShapeDtypeStruct((1, D), F32)],
        compiler_params=_cparams(("arbitrary",)),
    )(h, g.reshape(1, D), target)
    return loss, dh, dg.reshape(D)


_NT = (((1,), (1,)), ((), ()))
_TN = (((0,), (0,)), ((), ()))


def _attn_fwd(q, k, v, kbias, *, T, window=False, sink=None, side=None, name):
    sn = 0 if side is None else side.n
    H, L, dk = q.shape
    Hkv = k.shape[0]
    dv = v.shape[2]
    G = H // Hkv
    nt = L // T
    Hb = kbias.shape[0]
    reps = T // LANES
    assert not window or T == WINDOW
    HP = G if G > 1 else ATT_HP
    NS = 1 if G > 1 else HP
    R = HP * T // NS
    HKV = HP // G
    HB = HP if Hb > 1 else 1
    assert G == 1 or Hb == 1

    def body(*refs):
        q_ref, k_ref, v_ref, kb_ref = refs[:4]
        n = 4
        if sink is not None:
            sk_ref = refs[n]
            n += 1
        side_in = refs[n:n + sn]
        n += sn
        o_ref, lse_ref = refs[n:n + 2]
        side_out = refs[n + 2:n + 2 + sn]
        n += 2 + sn
        m_scs, l_scs, acc_scs, buf_a, buf_b = (refs[n + t * NS:n + (t + 1) * NS] for t in range(5))
        i = pl.program_id(1)
        if side is not None:
            _ride(side, [*side_in, *side_out, *refs[n + 5 * NS:]], pl.program_id(0) * nt + i, (H // HP) * nt)
        for a in range(NS):
            if sink is not None:
                sk = [jnp.broadcast_to(sk_ref[b, :, 0:1], (T, LANES)) for b in range(HP)]
                m_scs[a][...] = jnp.concatenate(sk, axis=0) if G > 1 else sk[a]
                l_scs[a][...] = jnp.ones((R, LANES), F32)
            else:
                m_scs[a][...] = jnp.full((R, LANES), NEG, F32)
                l_scs[a][...] = jnp.zeros((R, LANES), F32)
            acc_scs[a][...] = jnp.zeros((R, dv), F32)
        row = lax.broadcasted_iota(jnp.int32, (R, T), 0) & (T - 1) if G > 1 else \
            lax.broadcasted_iota(jnp.int32, (R, T), 0)
        col = lax.broadcasted_iota(jnp.int32, (R, T), 1)

        def logits(a, j):
            rows = pl.ds(pl.multiple_of(j * T, T), T)
            qv = q_ref[...].reshape(R, dk) if G > 1 else q_ref[a]
            s = lax.dot_general(qv, k_ref[a, rows, :], _NT, preferred_element_type=F32)
            return s - kb_ref[a if HB > 1 else 0, j]

        def update(a, s, j, kind):
            rows = pl.ds(pl.multiple_of(j * T, T), T)
            m_sc, l_sc, acc_sc = m_scs[a], l_scs[a], acc_scs[a]
            if kind == "diag":
                s = jnp.where(row >= col, s, NEG)
            elif kind == "prev":
                s = jnp.where(col > row, s, NEG)
            m_prev = m_sc[...]
            m_new = jnp.maximum(m_prev, jnp.max(s, axis=1, keepdims=True))
            alpha = jnp.exp(m_prev - m_new)
            p = jnp.exp(s - jnp.tile(m_new, (1, reps)))
            l_sc[...] = alpha * l_sc[...] + jnp.sum(p, axis=1, keepdims=True)
            acc_sc[...] = alpha[:, :dv] * acc_sc[...] + jnp.dot(p.astype(BF16), v_ref[a, rows, :],
                                                                preferred_element_type=F32)
            m_sc[...] = m_new

        if window:
            @pl.when(i > 0)
            def _():
                for a in range(NS):
                    update(a, logits(a, i - 1), i - 1, "prev")

            for a in range(NS):
                update(a, logits(a, i), i, "diag")
        else:
            def fill(buf, j):
                for a in range(NS):
                    buf[a][...] = logits(a, j)

            def drain(buf, j, kind):
                for a in range(NS):
                    update(a, buf[a][...], j, kind)

            fill(buf_a, 0)

            def pair(t, c):
                fill(buf_b, 2 * t + 1)
                drain(buf_a, 2 * t, "full")
                fill(buf_a, 2 * t + 2)
                drain(buf_b, 2 * t + 1, "full")
                return c

            lax.fori_loop(0, i // 2, pair, 0)

            @pl.when(i % 2 == 1)
            def _():
                fill(buf_b, i)
                drain(buf_a, i - 1, "full")
                drain(buf_b, i, "diag")

            @pl.when(i % 2 == 0)
            def _():
                drain(buf_a, i, "diag")

        for a in range(NS):
            lv = l_scs[a][...]
            ov = acc_scs[a][...] / lv[:, :dv]
            lsev = (m_scs[a][...] + jnp.log(lv))[:, 0:1]
            if G > 1:
                o_ref[...] = ov.reshape(HP, T, dv)
                lse_ref[...] = lsev.reshape(HP, T, 1)
            else:
                o_ref[a] = ov
                lse_ref[a] = lsev

    in_specs = [pl.BlockSpec((HP, T, dk), lambda h, i: (h, i, 0)),
                pl.BlockSpec((HKV, L, dk), lambda h, i: (h, 0, 0)),
                pl.BlockSpec((HKV, L, dv), lambda h, i: (h, 0, 0)),
                pl.BlockSpec((HB, nt, 1, T), lambda h, i: (h if Hb > 1 else 0, 0, 0, 0))]
    args = [q, k, v, kbias]
    if sink is not None:
        in_specs += [pl.BlockSpec((HP, 1, LANES), lambda h, i: (h, 0, 0))]
        args += [sink]
    any_spec = pl.BlockSpec(memory_space=pl.ANY)
    return pl.pallas_call(
        body, name=name, grid=(H // HP, nt),
        in_specs=in_specs + [any_spec] * sn,
        out_specs=[pl.BlockSpec((HP, T, dv), lambda h, i: (h, i, 0)),
                   pl.BlockSpec((HP, T, 1), lambda h, i: (h, i, 0))] + [any_spec] * sn,
        out_shape=[jax.ShapeDtypeStruct((H, L, dv), F32), jax.ShapeDtypeStruct((H, L, 1), F32)]
        + ([] if side is None else side.out_shape),
        scratch_shapes=[pltpu.VMEM((R, LANES), F32)] * (2 * NS) + [pltpu.VMEM((R, dv), F32)] * NS
        + [pltpu.VMEM((R, T), F32)] * (2 * NS) + ([] if side is None else side.scratch),
        compiler_params=_side_cparams(side, ("parallel", "arbitrary")),
    )(*args, *([] if side is None else side.arrs))


def _attn_bwd(q, k, v, kbias, o, lse, do, *, T, fox=False, side=None, name):
    sn = 0 if side is None else side.n
    H, L, dk = q.shape
    Hkv = k.shape[0]
    dv = v.shape[2]
    G = H // Hkv
    nt = L // T
    Hb = kbias.shape[0]

    def body(*refs):
        q_ref, k_ref, v_ref, kb_ref, o_ref, lse_ref, do_ref = refs[:7]
        side_in = refs[7:7 + sn]
        n = 7 + sn
        dq_ref, dk_ref, dv_ref = refs[n:n + 3]
        n += 3
        if fox:
            dcq_ref, dck_ref = refs[n:n + 2]
            n += 2
        side_out = refs[n:n + sn]
        n += sn
        delta_sc = refs[n]
        buf_a, buf_b = refs[n + 1:n + 3], refs[n + 3:n + 5]
        j = pl.program_id(1)
        if side is not None:
            _ride(side, [*side_in, *side_out, *refs[n + 5:]], pl.program_id(0) * nt + j, H * nt)

        @pl.when(j == 0)
        def _():
            dq_ref[...] = jnp.zeros_like(dq_ref)
            if fox:
                dcq_ref[...] = jnp.zeros_like(dcq_ref)

            def dl(i, c):
                rows = pl.ds(pl.multiple_of(i * T, T), T)
                delta_sc[rows, :] = jnp.sum(do_ref[0, rows, :] * o_ref[0, rows, :], axis=1, keepdims=True)
                return c

            lax.fori_loop(0, nt, dl, 0)

        dk_ref[...] = jnp.zeros_like(dk_ref)
        dv_ref[...] = jnp.zeros_like(dv_ref)
        if fox:
            dck_ref[...] = jnp.zeros_like(dck_ref)
        kb = k_ref[0]
        vb = v_ref[0]
        kbias_j = kb_ref[0, j]
        row = lax.broadcasted_iota(jnp.int32, (T, T), 0)
        col = lax.broadcasted_iota(jnp.int32, (T, T), 1)

        def fill(buf, i):
            rows = pl.ds(pl.multiple_of(i * T, T), T)
            buf[0][...] = lax.dot_general(q_ref[0, rows, :], kb, _NT, preferred_element_type=F32) - kbias_j
            buf[1][...] = lax.dot_general(do_ref[0, rows, :].astype(BF16), vb, _NT, preferred_element_type=F32)

        def drain(buf, i, kind):
            rows = pl.ds(pl.multiple_of(i * T, T), T)
            qb = q_ref[0, rows, :]
            dob = do_ref[0, rows, :].astype(BF16)
            s = buf[0][...]
            if kind == "diag":
                s = jnp.where(row >= col, s, NEG)
            p = jnp.exp(s - lse_ref[0, rows, :])
            ds = p * (buf[1][...] - delta_sc[rows, :])
            dsb = ds.astype(BF16)
            dv_ref[0] += lax.dot_general(p.astype(BF16), dob, _TN, preferred_element_type=F32)
            dk_ref[0] += lax.dot_general(dsb, qb, _TN, preferred_element_type=F32)
            dq_ref[0, rows, :] += jnp.dot(dsb, kb, preferred_element_type=F32)
            if fox:
                dcq_ref[0, rows, :] += jnp.sum(ds, axis=1, keepdims=True)
                dck_ref[0, 0] += -jnp.sum(ds, axis=0, keepdims=True)

        fill(buf_a, j)
        drain(buf_a, j, "diag")
        first = j + 1
        rest = nt - first

        @pl.when(rest > 0)
        def _():
            fill(buf_a, first)

        def pair(t, c):
            i0 = first + 2 * t
            fill(buf_b, i0 + 1)
            drain(buf_a, i0, "full")
            fill(buf_a, jnp.minimum(i0 + 2, nt - 1))
            drain(buf_b, i0 + 1, "full")
            return c

        lax.fori_loop(0, rest // 2, pair, 0)

        @pl.when(rest % 2 == 1)
        def _():
            drain(buf_a, nt - 1, "full")

    in_specs = [pl.BlockSpec((1, L, dk), lambda h, j: (h, 0, 0)),
                pl.BlockSpec((1, T, dk), lambda h, j: (h // G, j, 0)),
                pl.BlockSpec((1, T, dv), lambda h, j: (h // G, j, 0)),
                pl.BlockSpec((1, nt, 1, T), lambda h, j: (h if Hb > 1 else 0, 0, 0, 0)),
                pl.BlockSpec((1, L, dv), lambda h, j: (h, 0, 0)),
                pl.BlockSpec((1, L, 1), lambda h, j: (h, 0, 0)),
                pl.BlockSpec((1, L, dv), lambda h, j: (h, 0, 0))]
    out_specs = [pl.BlockSpec((1, L, dk), lambda h, j: (h, 0, 0)),
                 pl.BlockSpec((1, T, dk), lambda h, j: (h, j, 0)),
                 pl.BlockSpec((1, T, dv), lambda h, j: (h, j, 0))]
    out_shape = [jax.ShapeDtypeStruct((H, L, dk), F32), jax.ShapeDtypeStruct((H, L, dk), F32),
                 jax.ShapeDtypeStruct((H, L, dv), F32)]
    if fox:
        out_specs += [pl.BlockSpec((1, L, 1), lambda h, j: (h, 0, 0)),
                      pl.BlockSpec((1, 1, 1, T), lambda h, j: (h, j, 0, 0))]
        out_shape += [jax.ShapeDtypeStruct((H, L, 1), F32), jax.ShapeDtypeStruct((H, nt, 1, T), F32)]
    any_spec = pl.BlockSpec(memory_space=pl.ANY)
    outs = pl.pallas_call(
        body, name=name, grid=(H, nt),
        in_specs=in_specs + [any_spec] * sn, out_specs=out_specs + [any_spec] * sn,
        out_shape=out_shape + ([] if side is None else side.out_shape),
        scratch_shapes=[pltpu.VMEM((L, 1), F32)] + [pltpu.VMEM((T, T), F32)] * 4
        + ([] if side is None else side.scratch),
        compiler_params=_side_cparams(side, ("parallel", "arbitrary")),
    )(q, k, v, kbias, o, lse, do, *([] if side is None else side.arrs))
    main, rest = outs[:len(outs) - sn], outs[len(outs) - sn:]
    return (*(main if fox else (*main, None, None)), *rest)


def _attn_bwd_t(q, k, v, kbias_col, lse_row, delta_row, do, *, T, fox=False, side=None, name):
    sn = 0 if side is None else side.n
    H, L, dk = q.shape
    dv = v.shape[2]
    nt = L // T
    Hb = kbias_col.shape[0]

    def body(*refs):
        q_ref, k_ref, v_ref, kb_ref, lse_ref, dl_ref, do_ref = refs[:7]
        side_in = refs[7:7 + sn]
        n = 7 + sn
        dq_ref, dk_ref, dv_ref = refs[n:n + 3]
        n += 3
        if fox:
            dcq_ref, dck_ref = refs[n:n + 2]
            n += 2
        side_out = refs[n:n + sn]
        n += sn
        buf_a, buf_b = refs[n:n + 2], refs[n + 2:n + 4]
        j = pl.program_id(1)
        if side is not None:
            _ride(side, [*side_in, *side_out, *refs[n + 4:]], pl.program_id(0) * nt + j, H * nt)

        @pl.when(j == 0)
        def _():
            dq_ref[...] = jnp.zeros_like(dq_ref)
            if fox:
                dcq_ref[...] = jnp.zeros_like(dcq_ref)

        dk_ref[...] = jnp.zeros_like(dk_ref)
        dv_ref[...] = jnp.zeros_like(dv_ref)
        if fox:
            dck_ref[...] = jnp.zeros_like(dck_ref)
        kb = k_ref[0]
        vb = v_ref[0]
        kbias_j = kb_ref[0]
        key = lax.broadcasted_iota(jnp.int32, (T, T), 0)
        qry = lax.broadcasted_iota(jnp.int32, (T, T), 1)

        def fill(buf, i):
            rows = pl.ds(pl.multiple_of(i * T, T), T)
            buf[0][...] = lax.dot_general(kb, q_ref[0, rows, :], _NT, preferred_element_type=F32) - kbias_j
            buf[1][...] = lax.dot_general(vb, do_ref[0, rows, :], _NT, preferred_element_type=F32)

        def drain(buf, i, kind):
            rows = pl.ds(pl.multiple_of(i * T, T), T)
            st = buf[0][...]
            if kind == "diag":
                st = jnp.where(key <= qry, st, NEG)
            pt = jnp.exp(st - lse_ref[0, i])
            dst = pt * (buf[1][...] - dl_ref[0, i])
            dsb = dst.astype(BF16)
            dv_ref[0] += jnp.dot(pt.astype(BF16), do_ref[0, rows, :], preferred_element_type=F32)
            dk_ref[0] += jnp.dot(dsb, q_ref[0, rows, :], preferred_element_type=F32)
            dq_ref[0, rows, :] += lax.dot_general(dsb, kb, _TN, preferred_element_type=F32)
            if fox:
                dcq_ref[0, i] += jnp.sum(dst, axis=0, keepdims=True)
                dck_ref[0] += -jnp.sum(dst, axis=1, keepdims=True)

        fill(buf_a, j)
        drain(buf_a, j, "diag")
        first = j + 1
        rest = nt - first

        @pl.when(rest > 0)
        def _():
            fill(buf_a, first)

        def pair(t, c):
            i0 = first + 2 * t
            fill(buf_b, i0 + 1)
            drain(buf_a, i0, "full")
            fill(buf_a, jnp.minimum(i0 + 2, nt - 1))
            drain(buf_b, i0 + 1, "full")
            return c

        lax.fori_loop(0, rest // 2, pair, 0)

        @pl.when(rest % 2 == 1)
        def _():
            drain(buf_a, nt - 1, "full")

    rows_spec = pl.BlockSpec((1, nt, 1, T), lambda h, j: (h, 0, 0, 0))
    in_specs = [pl.BlockSpec((1, L, dk), lambda h, j: (h, 0, 0)),
                pl.BlockSpec((1, T, dk), lambda h, j: (h, j, 0)),
                pl.BlockSpec((1, T, dv), lambda h, j: (h, j, 0)),
                pl.BlockSpec((1, T, 1), lambda h, j: (h if Hb > 1 else 0, j, 0)),
                rows_spec, rows_spec,
                pl.BlockSpec((1, L, dv), lambda h, j: (h, 0, 0))]
    out_specs = [pl.BlockSpec((1, L, dk), lambda h, j: (h, 0, 0)),
                 pl.BlockSpec((1, T, dk), lambda h, j: (h, j, 0)),
                 pl.BlockSpec((1, T, dv), lambda h, j: (h, j, 0))]
    out_shape = [jax.ShapeDtypeStruct((H, L, dk), F32), jax.ShapeDtypeStruct((H, L, dk), F32),
                 jax.ShapeDtypeStruct((H, L, dv), F32)]
    if fox:
        out_specs += [rows_spec, pl.BlockSpec((1, T, 1), lambda h, j: (h, j, 0))]
        out_shape += [jax.ShapeDtypeStruct((H, nt, 1, T), F32), jax.ShapeDtypeStruct((H, L, 1), F32)]
    any_spec = pl.BlockSpec(memory_space=pl.ANY)
    outs = pl.pallas_call(
        body, name=name, grid=(H, nt),
        in_specs=in_specs + [any_spec] * sn, out_specs=out_specs + [any_spec] * sn,
        out_shape=out_shape + ([] if side is None else side.out_shape),
        scratch_shapes=[pltpu.VMEM((T, T), F32)] * 4 + ([] if side is None else side.scratch),
        compiler_params=_side_cparams(side, ("parallel", "arbitrary")),
    )(q, k, v, kbias_col, lse_row, delta_row, do, *([] if side is None else side.arrs))
    main, rest = outs[:len(outs) - sn], outs[len(outs) - sn:]
    return (*(main if fox else (*main, None, None)), *rest)


def _attn_bwd_window(q, k, v, kbias, o, lse, do, *, name):
    H, L, dk = q.shape
    Hkv = k.shape[0]
    dv = v.shape[2]
    G = H // Hkv
    T = WINDOW
    nt = L // T
    R = G * T

    def body(q_ref, kc_ref, kp_ref, vc_ref, vp_ref, kb_ref, o_ref, lse_ref, do_ref, dq_ref, dk_ref, dv_ref,
             dk_sc, dv_sc):
        i = pl.program_id(1)
        row = lax.broadcasted_iota(jnp.int32, (R, T), 0) & (T - 1)
        col = lax.broadcasted_iota(jnp.int32, (R, T), 1)

        @pl.when(i == 0)
        def _():
            dk_sc[...] = jnp.zeros_like(dk_sc)
            dv_sc[...] = jnp.zeros_like(dv_sc)

        @pl.when(i == nt)
        def _():
            dk_ref[0] = dk_sc[...]
            dv_ref[0] = dv_sc[...]

        @pl.when(i < nt)
        def _():
            qb = q_ref[...].reshape(R, dk)
            dof = do_ref[...].reshape(R, dv)
            dob = dof.astype(BF16)
            lse_c = lse_ref[...].reshape(R, 1)
            delta = jnp.sum(dof * o_ref[...].reshape(R, dv), axis=1, keepdims=True)

            def grads(kt, vt, kbias_j, mask):
                s = lax.dot_general(qb, kt, _NT, preferred_element_type=F32) - kbias_j
                p = jnp.exp(jnp.where(mask, s, NEG) - lse_c)
                dp = lax.dot_general(dob, vt, _NT, preferred_element_type=F32)
                ds = (p * (dp - delta)).astype(BF16)
                return (jnp.dot(ds, kt, preferred_element_type=F32),
                        lax.dot_general(ds, qb, _TN, preferred_element_type=F32),
                        lax.dot_general(p.astype(BF16), dob, _TN, preferred_element_type=F32))

            ip = jnp.maximum(i - 1, 0)
            dq_p, dk_p, dv_p = grads(kp_ref[0], vp_ref[0], kb_ref[0, ip], (col > row) & (i > 0))
            dq_c, dk_c, dv_c = grads(kc_ref[0], vc_ref[0], kb_ref[0, i], row >= col)
            dq_ref[...] = (dq_p + dq_c).reshape(G, T, dk)
            dk_ref[0] = dk_sc[...] + dk_p
            dv_ref[0] = dv_sc[...] + dv_p
            dk_sc[...] = dk_c
            dv_sc[...] = dv_c

    def cur(i):
        return jnp.minimum(i, nt - 1)

    def prev(i):
        return jnp.maximum(jnp.minimum(i, nt - 1) - 1, 0)

    def written(i):
        return jnp.maximum(i - 1, 0)

    qs = lambda d: pl.BlockSpec((G, T, d), lambda h, i: (h, cur(i), 0))
    return pl.pallas_call(
        body, name=name, grid=(Hkv, nt + 1),
        in_specs=[qs(dk),
                  pl.BlockSpec((1, T, dk), lambda h, i: (h, cur(i), 0)),
                  pl.BlockSpec((1, T, dk), lambda h, i: (h, prev(i), 0)),
                  pl.BlockSpec((1, T, dv), lambda h, i: (h, cur(i), 0)),
                  pl.BlockSpec((1, T, dv), lambda h, i: (h, prev(i), 0)),
                  pl.BlockSpec((1, nt, 1, T), lambda h, i: (0, 0, 0, 0)),
                  qs(dv), qs(1), qs(dv)],
        out_specs=[qs(dk),
                   pl.BlockSpec((1, T, dk), lambda h, i: (h, written(i), 0)),
                   pl.BlockSpec((1, T, dv), lambda h, i: (h, written(i), 0))],
        out_shape=[jax.ShapeDtypeStruct((H, L, dk), F32), jax.ShapeDtypeStruct((Hkv, L, dk), F32),
                   jax.ShapeDtypeStruct((Hkv, L, dv), F32)],
        scratch_shapes=[pltpu.VMEM((T, dk), F32), pltpu.VMEM((T, dv), F32)],
        compiler_params=_cparams(("parallel", "arbitrary")),
    )(q, k, k, v, v, kbias, o, lse, do)


def _adamw(w, gparts, m, v, *, name):
    n, R, C = gparts.shape
    tr = _pick(R, (128, 64, 32, 16, 8))
    c1 = 1.0 - ADAM_B1 ** ADAM_STEP
    c2 = 1.0 - ADAM_B2 ** ADAM_STEP

    def body(w_ref, g_ref, m_ref, v_ref, go_ref, d_ref, mo_ref, vo_ref):
        g = g_ref[0].astype(F32)
        for t in range(1, n):
            g = g + g_ref[t].astype(F32)
        mn = ADAM_B1 * m_ref[...] + (1.0 - ADAM_B1) * g
        vn = ADAM_B2 * v_ref[...] + (1.0 - ADAM_B2) * (g * g)
        go_ref[...] = g
        mo_ref[...] = mn
        vo_ref[...] = vn
        d_ref[...] = -ADAM_LR * ((mn / c1) / (jnp.sqrt(vn / c2) + ADAM_EPS) + ADAM_WD * w_ref[...])

    spec = pl.BlockSpec((tr, C), lambda i: (i, 0))
    return pl.pallas_call(
        body, name=name, grid=(R // tr,),
        in_specs=[spec, pl.BlockSpec((n, tr, C), lambda i: (0, i, 0)), spec, spec],
        out_specs=[spec] * 4,
        out_shape=[jax.ShapeDtypeStruct((R, C), F32)] * 4,
        compiler_params=_cparams(("parallel",)),
    )(w, gparts, m, v)


def _me():
    return lax.axis_index("x"), lax.axis_index("y"), lax.axis_index("c")


class _CommJob:
    def __init__(self, arrs):
        self.arrs = list(arrs)
        self.n = len(arrs)
        self.scratch = [pltpu.SemaphoreType.DMA((self.n, 7)), pltpu.SemaphoreType.DMA((self.n, 7)),
                        pltpu.SemaphoreType.DMA((self.n,))]

    def bind(self, refs):
        n = self.n
        self.ins, self.outs = refs[:n], refs[n:2 * n]
        self.send_sems, self.recv_sems, self.local_sems = refs[2 * n:2 * n + 3]

    def middle(self):
        pass


class _Gather(_CommJob):
    def __init__(self, arrs):
        super().__init__(arrs)
        self.out_shape = [jax.ShapeDtypeStruct((N_DEV, *a.shape), a.dtype) for a in arrs]

    def _where(self):
        x, y, c = _me()
        return (x, y, c), (x, y, 1 - c), [(1 - x, y), (x, 1 - y), (1 - x, 1 - y)], c

    def _copy(self, t, k, block, to, src=None):
        dst = self.outs[t].at[4 * block[0] + 2 * block[1] + block[2]]
        return pltpu.make_async_remote_copy(
            src_ref=dst if src is None else src, dst_ref=dst,
            send_sem=self.send_sems.at[t, k], recv_sem=self.recv_sems.at[t, k],
            device_id=to, device_id_type=MESH)

    def _mine(self, t, me):
        return pltpu.make_async_copy(self.ins[t], self.outs[t].at[4 * me[0] + 2 * me[1] + me[2]],
                                     self.local_sems.at[t])

    def _first(self, t, me, sibling, chips, c):
        return [self._copy(t, 0, me, sibling, src=self.ins[t])] + \
               [self._copy(t, 1 + j, me, (*chip, c), src=self.ins[t]) for j, chip in enumerate(chips)]

    def start(self):
        me, sibling, chips, c = self._where()
        for t in range(self.n):
            self._mine(t, me).start()
            for cp in self._first(t, me, sibling, chips, c):
                cp.start()

    def middle(self):
        me, sibling, chips, c = self._where()
        for j, chip in enumerate(chips):
            for t in range(self.n):
                self._copy(t, 1 + j, (*chip, c), me).wait_recv()
                self._copy(t, 4 + j, (*chip, c), sibling).start()

    def finish(self):
        me, sibling, chips, c = self._where()
        for t in range(self.n):
            self._copy(t, 0, sibling, me).wait_recv()
            for j, chip in enumerate(chips):
                self._copy(t, 4 + j, (*chip, 1 - c), me).wait_recv()
        for t in range(self.n):
            for cp in self._first(t, me, sibling, chips, c):
                cp.wait_send()
            for j, chip in enumerate(chips):
                self._copy(t, 4 + j, (*chip, c), sibling).wait_send()
            self._mine(t, me).wait()


class _Exchange(_CommJob):
    def __init__(self, arrs):
        super().__init__(arrs)
        self.out_shape = [jax.ShapeDtypeStruct(a.shape, a.dtype) for a in arrs]

    def _copies(self, t):
        x, y, c = _me()
        my_idx = 4 * x + 2 * y + c
        pairs = []
        for k in range(1, N_DEV):
            peer = (x ^ ((k >> 2) & 1), y ^ ((k >> 1) & 1), c ^ (k & 1))
            peer_idx = 4 * peer[0] + 2 * peer[1] + peer[2]
            sems = dict(send_sem=self.send_sems.at[t, k - 1], recv_sem=self.recv_sems.at[t, k - 1],
                        device_id=peer, device_id_type=MESH)
            pairs.append((pltpu.make_async_remote_copy(src_ref=self.ins[t].at[peer_idx],
                                                       dst_ref=self.outs[t].at[my_idx], **sems),
                          pltpu.make_async_remote_copy(src_ref=self.ins[t].at[peer_idx],
                                                       dst_ref=self.outs[t].at[peer_idx], **sems)))
        return pairs

    def _mine(self, t):
        x, y, c = _me()
        my_idx = 4 * x + 2 * y + c
        return pltpu.make_async_copy(self.ins[t].at[my_idx], self.outs[t].at[my_idx], self.local_sems.at[t])

    def start(self):
        for t in range(self.n):
            self._mine(t).start()
            for snd, _ in self._copies(t):
                snd.start()

    def finish(self):
        for t in range(self.n):
            pairs = self._copies(t)
            for _, rcv in pairs:
                rcv.wait_recv()
            for snd, _ in pairs:
                snd.wait_send()
            self._mine(t).wait()


def _comm(job, *, name):
    def body(*refs):
        job.bind(refs)
        job.start()
        job.middle()
        job.finish()

    any_spec = pl.BlockSpec(memory_space=pl.ANY)
    return pl.pallas_call(
        body, name=name,
        in_specs=[any_spec] * job.n, out_specs=[any_spec] * job.n,
        out_shape=job.out_shape, scratch_shapes=job.scratch,
        compiler_params=pltpu.CompilerParams(has_side_effects=True),
    )(*job.arrs)


def _ride(job, refs, step, total):
    job.bind(refs)

    @pl.when(step == 0)
    def _():
        job.start()

    @pl.when(step == (total * 3) // 5)
    def _():
        job.middle()

    @pl.when(step == total - 1)
    def _():
        job.finish()


SCALE_A = DH ** -0.5
SCALE_B = (MLA_NOPE + MLA_ROPE) ** -0.5
SCALE_C = DH ** -0.5
DQK_B = MLA_NOPE + MLA_ROPE


def _rope_tables(L, half, width):
    pos = (jnp.arange(L) - PAD).astype(F32)
    inv = ROPE_THETA ** (-jnp.arange(half, dtype=F32) / half)
    lane = jnp.arange(width)
    ang = pos[:, None] * inv[lane % half][None, :]
    sign = jnp.where(lane % (2 * half) < half, -1.0, 1.0).astype(F32)
    return jnp.cos(ang), jnp.sin(ang) * sign[None, :]


def _rope_lanes(x, cos, sin, half):
    W = x.shape[1]
    lane = lax.broadcasted_iota(jnp.int32, x.shape, 1)
    first = (lane & (2 * half - 1)) < half
    partner = jnp.where(first, pltpu.roll(x, W - half, 1), pltpu.roll(x, half, 1))
    return x * cos + partner * sin


def _tile_lanes(t, width):
    return t if t.shape[1] == width else jnp.tile(t, (1, width // t.shape[1]))


def _split(x, H, d, dst):
    for h in range(H):
        dst[h] = x[:, d * h:d * (h + 1)].astype(dst.dtype)


def _join(src, H):
    return jnp.concatenate([src[h] for h in range(H)], axis=1)


def _head_spec(H, tm, d):
    return pl.BlockSpec((H, tm, d), lambda i: (0, i, 0))


def _prep_a(proj, *, name):
    L = proj.shape[0]
    tm = ROW_T

    def body(x_ref, qo, ko, vo):
        _split(x_ref[:, 0:512] * SCALE_A, HEADS, DH, qo)
        _split(x_ref[:, 512:1024], HEADS, DH, ko)
        _split(x_ref[:, 1024:1536], HEADS, DH, vo)

    return pl.pallas_call(
        body, name=name, grid=(L // tm,),
        in_specs=[_col_spec(tm, 1536, QKVA)],
        out_specs=[_head_spec(HEADS, tm, DH)] * 3,
        out_shape=[jax.ShapeDtypeStruct((HEADS, L, DH), BF16)] * 3,
        compiler_params=_cparams(("parallel",)),
    )(proj)


def _unprep_a(dq, dk, dv, dproj, *, name):
    L = dq.shape[1]
    tm = ROW_T

    def body(dq_ref, dk_ref, dv_ref, _, dp_ref):
        dp_ref[:, 0:512] = (_join(dq_ref, HEADS) * SCALE_A).astype(BF16)
        dp_ref[:, 512:1024] = _join(dk_ref, HEADS).astype(BF16)
        dp_ref[:, 1024:1536] = _join(dv_ref, HEADS).astype(BF16)

    hs = _head_spec(HEADS, tm, DH)
    return pl.pallas_call(
        body, name=name, grid=(L // tm,),
        in_specs=[hs, hs, hs, pl.BlockSpec(memory_space=pl.ANY)],
        out_specs=_col_spec(tm, 1536, QKVA),
        out_shape=jax.ShapeDtypeStruct(dproj.shape, dproj.dtype),
        input_output_aliases={3: 0},
        compiler_params=_cparams(("parallel",)),
    )(dq, dk, dv, dproj)


def _prep_c(proj, tab, *, name):
    L = proj.shape[0]
    tm = ROW_T

    def body(x_ref, cos_ref, sin_ref, qo, ko, vo):
        cos, sin = cos_ref[...], sin_ref[...]
        q = _rope_lanes(x_ref[:, 0:512], _tile_lanes(cos, 512), _tile_lanes(sin, 512), DH // 2)
        _split(q * SCALE_C, HEADS, DH, qo)
        _split(_rope_lanes(x_ref[:, 512:640], cos, sin, DH // 2), SWA_KV_HEADS, DH, ko)
        _split(x_ref[:, 640:768], SWA_KV_HEADS, DH, vo)

    t128 = pl.BlockSpec((tm, LANES), lambda i: (i, 0))
    return pl.pallas_call(
        body, name=name, grid=(L // tm,),
        in_specs=[_col_spec(tm, 768, QKVC), t128, t128],
        out_specs=[_head_spec(HEADS, tm, DH), _head_spec(SWA_KV_HEADS, tm, DH), _head_spec(SWA_KV_HEADS, tm, DH)],
        out_shape=[jax.ShapeDtypeStruct((HEADS, L, DH), BF16), jax.ShapeDtypeStruct((SWA_KV_HEADS, L, DH), BF16),
                   jax.ShapeDtypeStruct((SWA_KV_HEADS, L, DH), BF16)],
        compiler_params=_cparams(("parallel",)),
    )(proj, *tab)


def _unprep_c(dq, dk, dv, tab, dproj, *, name):
    L = dq.shape[1]
    tm = ROW_T

    def body(dq_ref, dk_ref, dv_ref, cos_ref, sin_ref, _, dp_ref):
        cos, nsin = cos_ref[...], -sin_ref[...]
        dqv = _join(dq_ref, HEADS) * SCALE_C
        dp_ref[:, 0:512] = _rope_lanes(dqv, _tile_lanes(cos, 512), _tile_lanes(nsin, 512), DH // 2).astype(BF16)
        dp_ref[:, 512:640] = _rope_lanes(_join(dk_ref, SWA_KV_HEADS), cos, nsin, DH // 2).astype(BF16)
        dp_ref[:, 640:768] = _join(dv_ref, SWA_KV_HEADS).astype(BF16)

    hs = _head_spec(HEADS, tm, DH)
    hkv = _head_spec(SWA_KV_HEADS, tm, DH)
    t128 = pl.BlockSpec((tm, LANES), lambda i: (i, 0))
    return pl.pallas_call(
        body, name=name, grid=(L // tm,),
        in_specs=[hs, hkv, hkv, t128, t128, pl.BlockSpec(memory_space=pl.ANY)],
        out_specs=_col_spec(tm, 768, QKVC),
        out_shape=jax.ShapeDtypeStruct(dproj.shape, dproj.dtype),
        input_output_aliases={5: 0},
        compiler_params=_cparams(("parallel",)),
    )(dq, dk, dv, *tab, dproj)


def _q_tables(cos, sin, on_ref):
    on = on_ref[...] > 0.5
    width = on_ref.shape[1]
    return jnp.where(on, _tile_lanes(cos, width), 1.0), jnp.where(on, _tile_lanes(sin, width), 0.0)


def _prep_b(qbm, kvbm, proj, tab_k, q_rotary, *, name):
    L = proj.shape[0]
    tm = ROW_T

    def body(q_ref, kv_ref, kr_ref, ck_ref, sk_ref, on_ref, qo, ko, vo):
        cq, sq = _q_tables(ck_ref[...], sk_ref[...], on_ref)
        q = _rope_lanes(q_ref[...], cq, sq, MLA_ROPE // 2) * SCALE_B
        _split(q, HEADS, DQK_B, qo)
        kr = _rope_lanes(kr_ref[...], ck_ref[...], sk_ref[...], MLA_ROPE // 2)[:, :MLA_ROPE].astype(BF16)
        kv = kv_ref[...]
        for h in range(HEADS):
            ko[h] = jnp.concatenate([kv[:, 128 * h:128 * h + MLA_NOPE].astype(BF16), kr], axis=1)
            vo[h] = kv[:, 128 * h + MLA_NOPE:128 * (h + 1)].astype(BF16)

    t128 = pl.BlockSpec((tm, LANES), lambda i: (i, 0))
    t768 = pl.BlockSpec((tm, 768), lambda i: (i, 0))
    return pl.pallas_call(
        body, name=name, grid=(L // tm,),
        in_specs=[t768, pl.BlockSpec((tm, 1024), lambda i: (i, 0)), _col_spec(tm, 128, KR), t128, t128,
                  pl.BlockSpec((1, 768), lambda i: (0, 0))],
        out_specs=[_head_spec(HEADS, tm, DQK_B), _head_spec(HEADS, tm, DQK_B), _head_spec(HEADS, tm, DH)],
        out_shape=[jax.ShapeDtypeStruct((HEADS, L, DQK_B), BF16), jax.ShapeDtypeStruct((HEADS, L, DQK_B), BF16),
                   jax.ShapeDtypeStruct((HEADS, L, DH), BF16)],
        compiler_params=_cparams(("parallel",)),
    )(qbm, kvbm, proj, *tab_k, q_rotary)


def _unprep_b(dq, dk, dv, tab_k, q_rotary, dproj, *, name):
    L = dq.shape[1]
    tm = ROW_T

    def body(dq_ref, dk_ref, dv_ref, ck_ref, sk_ref, on_ref, _, dp_kr, dqo, dkvo):
        cq, sq = _q_tables(ck_ref[...], sk_ref[...], on_ref)
        dqv = _join(dq_ref, HEADS) * SCALE_B
        dqo[...] = _rope_lanes(dqv, cq, -sq, MLA_ROPE // 2).astype(BF16)
        parts = []
        dkr = None
        for h in range(HEADS):
            dkh = dk_ref[h]
            parts += [dkh[:, :MLA_NOPE], dv_ref[h]]
            r = dkh[:, MLA_NOPE:]
            dkr = r if dkr is None else dkr + r
        dkvo[...] = jnp.concatenate(parts, axis=1).astype(BF16)
        dkr = jnp.concatenate([dkr, jnp.zeros((tm, LANES - MLA_ROPE), F32)], axis=1)
        dp_kr[...] = _rope_lanes(dkr, ck_ref[...], -sk_ref[...], MLA_ROPE // 2).astype(BF16)

    t128 = pl.BlockSpec((tm, LANES), lambda i: (i, 0))
    t768 = pl.BlockSpec((tm, 768), lambda i: (i, 0))
    hq = _head_spec(HEADS, tm, DQK_B)
    return pl.pallas_call(
        body, name=name, grid=(L // tm,),
        in_specs=[hq, hq, _head_spec(HEADS, tm, DH), t128, t128, pl.BlockSpec((1, 768), lambda i: (0, 0)),
                  pl.BlockSpec(memory_space=pl.ANY)],
        out_specs=[_col_spec(tm, 128, KR), t768, pl.BlockSpec((tm, 1024), lambda i: (i, 0))],
        out_shape=[jax.ShapeDtypeStruct(dproj.shape, dproj.dtype), jax.ShapeDtypeStruct((L, 768), BF16),
                   jax.ShapeDtypeStruct((L, 1024), BF16)],
        input_output_aliases={6: 0},
        compiler_params=_cparams(("parallel",)),
    )(dq, dk, dv, *tab_k, q_rotary, dproj)


def _gate(y, proj, zcol, *, name):
    L = proj.shape[0]
    tm = ROW_T

    def body(y_ref, z_ref, u_ref):
        z = z_ref[...]
        u_ref[...] = (_join(y_ref, HEADS) * (z * jax.nn.sigmoid(z))).astype(BF16)

    return pl.pallas_call(
        body, name=name, grid=(L // tm,),
        in_specs=[_head_spec(HEADS, tm, DH), _col_spec(tm, 512, zcol)],
        out_specs=pl.BlockSpec((tm, 512), lambda i: (i, 0)),
        out_shape=jax.ShapeDtypeStruct((L, 512), BF16),
        compiler_params=_cparams(("parallel",)),
    )(y, proj)


def _gate_bwd(du, y, proj, zcol, dproj, *, delta_rows=False, name):
    L = proj.shape[0]
    tm = ROW_T
    assert not delta_rows or tm == ATT_T

    def body(du_ref, y_ref, z_ref, _, dz_ref, dy_ref, *dl_ref):
        z = z_ref[...]
        duv = du_ref[...]
        sg = jax.nn.sigmoid(z)
        yv = _join(y_ref, HEADS)
        dyv = duv * (z * sg)
        _split(dyv, HEADS, DH, dy_ref)
        dz_ref[...] = (duv * yv * (sg * (1.0 + z * (1.0 - sg)))).astype(BF16)
        if delta_rows:
            prod = dyv * yv
            ones = jnp.ones((8, DH), F32)
            for h in range(HEADS):
                sums = lax.dot_general(ones, prod[:, DH * h:DH * (h + 1)], _NT, preferred_element_type=F32)
                dl_ref[0][h, 0] = sums[0:1, :]

    hs = _head_spec(HEADS, tm, DH)
    out_specs = [_col_spec(tm, 512, zcol), hs]
    out_shape = [jax.ShapeDtypeStruct(dproj.shape, dproj.dtype),
                 jax.ShapeDtypeStruct((HEADS, L, DH), BF16 if delta_rows else F32)]
    if delta_rows:
        out_specs.append(pl.BlockSpec((HEADS, 1, 1, tm), lambda i: (0, i, 0, 0)))
        out_shape.append(jax.ShapeDtypeStruct((HEADS, L // tm, 1, tm), F32))
    return pl.pallas_call(
        body, name=name, grid=(L // tm,),
        in_specs=[pl.BlockSpec((tm, 512), lambda i: (i, 0)), hs, _col_spec(tm, 512, zcol),
                  pl.BlockSpec(memory_space=pl.ANY)],
        out_specs=out_specs, out_shape=out_shape,
        input_output_aliases={3: 0},
        compiler_params=_cparams(("parallel",)),
    )(du, y, proj, dproj)


MERGE_T = 192


def _merge(proj, pbs, *, name):
    L = proj.shape[0]
    tm = MERGE_T

    def body(g0, g1, g2, p0, p1, p2, o_ref):
        acc = None
        for g_ref, p_ref in ((g0, p0), (g1, p1), (g2, p2)):
            t = jax.nn.sigmoid(g_ref[...]) * p_ref[...]
            acc = t if acc is None else acc + t
        o_ref[...] = acc.astype(BF16)

    row = pl.BlockSpec((tm, D_MODEL), lambda i: (i, 0))
    return pl.pallas_call(
        body, name=name, grid=(L // tm,),
        in_specs=[_col_spec(tm, D_MODEL, GATES + n * D_MODEL) for n in range(N_BRANCH)] + [row] * N_BRANCH,
        out_specs=row, out_shape=jax.ShapeDtypeStruct((L, D_MODEL), BF16),
        compiler_params=_cparams(("parallel",)),
    )(proj, proj, proj, *pbs)


def _merge_bwd(dmerged, proj, pbs, dproj, *, name):
    L = proj.shape[0]
    tm = MERGE_T

    def body(dm_ref, g_ref, p0, p1, p2, _, dg_ref, dp0, dp1, dp2):
        dm = dm_ref[...]
        for n, (p_ref, dp_ref) in enumerate(((p0, dp0), (p1, dp1), (p2, dp2))):
            cols = slice(n * D_MODEL, (n + 1) * D_MODEL)
            sg = jax.nn.sigmoid(g_ref[:, cols])
            dp_ref[...] = (dm * sg).astype(BF16)
            dg_ref[:, cols] = (dm * p_ref[...] * (sg * (1.0 - sg))).astype(BF16)

    row = pl.BlockSpec((tm, D_MODEL), lambda i: (i, 0))
    gates = _col_spec(tm, N_BRANCH * D_MODEL, GATES)
    outs = pl.pallas_call(
        body, name=name, grid=(L // tm,),
        in_specs=[row, gates] + [row] * N_BRANCH + [pl.BlockSpec(memory_space=pl.ANY)],
        out_specs=[gates] + [row] * N_BRANCH,
        out_shape=[jax.ShapeDtypeStruct(dproj.shape, dproj.dtype)]
        + [jax.ShapeDtypeStruct((L, D_MODEL), BF16)] * N_BRANCH,
        input_output_aliases={5: 0},
        compiler_params=_cparams(("parallel",)),
    )(dmerged, proj, *pbs, dproj)
    return outs[0], outs[1:]


def _forget_bias(af, b_f):
    return jnp.cumsum(jax.nn.log_sigmoid(af + b_f), axis=0).T


def _key_bias(L, T, ct=None):
    padb = jnp.where(jnp.arange(L) < PAD, BIG, 0.0).astype(F32)[None]
    kb = padb if ct is None else ct + padb
    return kb.reshape(kb.shape[0], L // T, 1, T)


def _layer_fwd(h, w, tabs, l, side=None):
    tag = f"l{l}"
    L = h.shape[0]
    hn = _rms_fwd(h, w["norm_g"], name=f"{tag}_rms_in")
    if "late" in w:
        late_job, assemble = w["late"]
        proj, *late = _mm(hn, w["w_in"], side=late_job, name=f"{tag}_mm_in")
        w = {**{key: val for key, val in w.items() if key != "late"}, **assemble(*late)}
    else:
        proj = _mm(hn, w["w_in"], name=f"{tag}_mm_in")
    ct, vjp_f = jax.vjp(_forget_bias, proj[:, FA:FA + HEADS], w["b_f"])
    ops_a = (*_prep_a(proj, name=f"{tag}_prep_a"), _key_bias(L, ATT_T, ct))
    ya, lsea, *side_out = _attn_fwd(*ops_a, T=ATT_T, side=side, name=f"{tag}_attn_a")
    cqn = _rms_fwd(proj, w["g_cq"], col=CQ, name=f"{tag}_rms_cq")
    ckvn = _rms_fwd(proj, w["g_ckv"], col=CKV, name=f"{tag}_rms_ckv")
    qbm = _mm(cqn, w["w_uq"], name=f"{tag}_mm_uq")
    kvbm = _mm(ckvn, w["w_ukv"], name=f"{tag}_mm_ukv")
    ops_b = (*_prep_b(qbm, kvbm, proj, tabs["bk"], tabs["bq"], name=f"{tag}_prep_b"), _key_bias(L, ATT_T))
    yb, lseb = _attn_fwd(*ops_b, T=ATT_T, name=f"{tag}_attn_b")
    ops_c = (*_prep_c(proj, tabs["c"], name=f"{tag}_prep_c"), _key_bias(L, WINDOW))
    sink = jnp.broadcast_to(w["sinks"][:, None, None], (HEADS, 1, LANES))
    yc, lsec = _attn_fwd(*ops_c, T=WINDOW, window=True, sink=sink, name=f"{tag}_attn_c")
    us = [_gate(y, proj, zcol, name=f"{tag}_gate{n}") for n, (y, zcol) in enumerate(((ya, ZA), (yb, ZB), (yc, ZC)))]
    pbr = [_mm(us[n], w["w_branch"][n], name=f"{tag}_mm_br{n}") for n in range(N_BRANCH)]
    merged = _merge(proj, pbr, name=f"{tag}_merge")
    out = _mm(merged, w["w_out"], add=h, name=f"{tag}_mm_out")
    saved = dict(h=h, hn=hn, proj=proj, vjp_f=vjp_f, ops_a=ops_a, ya=ya, lsea=lsea, cqn=cqn, ckvn=ckvn,
                 ops_b=ops_b, yb=yb, lseb=lseb, ops_c=ops_c, yc=yc, lsec=lsec, us=us, pbr=pbr, merged=merged)
    return out, saved, side_out, w


def _w_in_chunks(g):
    return _cols_to_shards(_unpad_in(g["w_in"]))


def _grad_chunks(g, with_w_in=True):
    rest = [_cols_to_shards(g["w_uq"]), _cols_to_shards(g["w_ukv"]),
            jnp.stack([_cols_to_shards(g["w_branch"][n]) for n in range(N_BRANCH)], axis=1),
            g["w_out"].reshape(N_DEV, D_MODEL // N_DEV, D_MODEL)]
    return ([_w_in_chunks(g)] if with_w_in else []) + rest


def _layer_bwd(dout, s, w, tabs, l, side=None, own_exchange=False):
    tag = f"l{l}"
    L = dout.shape[0]
    proj = s["proj"]
    g = {}
    dproj = jnp.zeros((L, NP_IN), BF16)
    dmerged = _mm(dout, w["w_out"], tb=True, name=f"{tag}_mm_out_dx")
    g["w_out"] = _mm(s["merged"], dout, ta=True, out_dtype=BF16, name=f"{tag}_mm_out_dw")
    dproj, dpbr = _merge_bwd(dmerged, proj, s["pbr"], dproj, name=f"{tag}_merge_bwd")
    dus = [_mm(dpbr[n], w["w_branch"][n], tb=True, name=f"{tag}_mm_br{n}_dx") for n in range(N_BRANCH)]
    g["w_branch"] = jnp.stack([_mm(s["us"][n], dpbr[n], ta=True, out_dtype=BF16, name=f"{tag}_mm_br{n}_dw")
                               for n in range(N_BRANCH)])
    dproj, dya, dla = _gate_bwd(dus[0], s["ya"], proj, ZA, dproj, delta_rows=True, name=f"{tag}_gate0_bwd")
    dproj, dyb, dlb = _gate_bwd(dus[1], s["yb"], proj, ZB, dproj, delta_rows=True, name=f"{tag}_gate1_bwd")
    dproj, dyc = _gate_bwd(dus[2], s["yc"], proj, ZC, dproj, name=f"{tag}_gate2_bwd")

    def bwd_operands(ops, lse, delta, dy16):
        q16, k16, v16, kbias = ops
        return (q16, k16, v16, kbias.reshape(kbias.shape[0], L, 1), lse.reshape(HEADS, L // ATT_T, 1, ATT_T), delta,
                dy16)

    dqa, dka, dva, dcq, dck, *side_out = _attn_bwd_t(*bwd_operands(s["ops_a"], s["lsea"], dla, dya), T=ATT_T,
                                                     fox=True, side=side, name=f"{tag}_attn_a_bwd")
    dproj = _unprep_a(dqa, dka, dva, dproj, name=f"{tag}_unprep_a")
    daf, g["b_f"] = s["vjp_f"](dcq.reshape(HEADS, L) + dck[:, :, 0])
    dproj = lax.dynamic_update_slice(dproj, daf.astype(BF16), (0, FA))
    dqb, dkb, dvb, _, _ = _attn_bwd_t(*bwd_operands(s["ops_b"], s["lseb"], dlb, dyb), T=ATT_T,
                                      name=f"{tag}_attn_b_bwd")
    dproj, dqbm, dkvbm = _unprep_b(dqb, dkb, dvb, tabs["bk"], tabs["bq"], dproj, name=f"{tag}_unprep_b")
    dcqn = _mm(dqbm, w["w_uq"], tb=True, name=f"{tag}_mm_uq_dx")
    g["w_uq"] = _mm(s["cqn"], dqbm, ta=True, out_dtype=BF16, name=f"{tag}_mm_uq_dw")
    dckvn = _mm(dkvbm, w["w_ukv"], tb=True, name=f"{tag}_mm_ukv_dx")
    g["w_ukv"] = _mm(s["ckvn"], dkvbm, ta=True, out_dtype=BF16, name=f"{tag}_mm_ukv_dw")
    dproj, g["g_cq"] = _rms_bwd(dcqn, proj, w["g_cq"], col=CQ, into=dproj, name=f"{tag}_rms_cq_bwd")
    dproj, g["g_ckv"] = _rms_bwd(dckvn, proj, w["g_ckv"], col=CKV, into=dproj, name=f"{tag}_rms_ckv_bwd")
    dqc, dkc, dvc = _attn_bwd_window(*s["ops_c"], s["yc"], s["lsec"], dyc, name=f"{tag}_attn_c_bwd")
    dproj = _unprep_c(dqc, dkc, dvc, tabs["c"], dproj, name=f"{tag}_unprep_c")
    delta_c = jnp.sum(dyc * s["yc"], axis=-1)
    g["sinks"] = -jnp.sum(jnp.exp(w["sinks"][:, None] - s["lsec"][:, :, 0]) * delta_c, axis=1)
    own_out = []
    if own_exchange:
        g["w_in"], *r_rest = _mm(s["hn"], dproj, ta=True, out_dtype=BF16, side=_Exchange(_grad_chunks(g, False)),
                                 name=f"{tag}_mm_in_dw")
        dhn, r_in = _mm(dproj, w["w_in"], tb=True, side=_Exchange([_w_in_chunks(g)]), name=f"{tag}_mm_in_dx")
        own_out = [r_in, *r_rest]
    else:
        g["w_in"] = _mm(s["hn"], dproj, ta=True, out_dtype=BF16, name=f"{tag}_mm_in_dw")
        dhn = _mm(dproj, w["w_in"], tb=True, name=f"{tag}_mm_in_dx")
    dh, g["norm_g"] = _rms_bwd(dhn, s["h"], w["norm_g"], add=dout, name=f"{tag}_rms_in_bwd")
    return dh, g, side_out, own_out


def _cols_from_shards(g):
    return jnp.moveaxis(g, 0, 1).reshape(g.shape[1], N_DEV * g.shape[2])


def _cols_to_shards(w):
    R = w.shape[0]
    return jnp.moveaxis(w.reshape(R, N_DEV, w.shape[1] // N_DEV), 1, 0)


def _pad_in(w):
    parts, at = [], 0
    for lo, hi, dst in sorted(_RUNS, key=lambda r: r[2]):
        if dst > at:
            parts.append(jnp.zeros((w.shape[0], dst - at), w.dtype))
        parts.append(w[:, lo:hi])
        at = dst + hi - lo
    parts.append(jnp.zeros((w.shape[0], NP_IN - at), w.dtype))
    return jnp.concatenate(parts, axis=1)


def _unpad_in(w):
    return jnp.concatenate([w[:, dst:dst + hi - lo] for lo, hi, dst in _RUNS], axis=1)


_SMALL = (("norm_g", DEPTH * D_MODEL), ("b_f", DEPTH * HEADS), ("g_cq", DEPTH * MLA_QLORA),
          ("g_ckv", DEPTH * MLA_KVLORA), ("sinks", DEPTH * HEADS), ("final_g", D_MODEL), ("loss", 1),
          ("meta", N_META * D_MODEL))
SMALL_ROWS = 168


def _pack_small(d):
    parts = []
    for name, size in _SMALL:
        padded = -(-size // 128) * 128
        v = d[name].reshape(-1).astype(F32) if name in d else jnp.zeros((size,), F32)
        parts.append(jnp.pad(v, (0, padded - size)))
    flat = jnp.concatenate(parts)
    return jnp.pad(flat, (0, SMALL_ROWS * 128 - flat.shape[0])).reshape(SMALL_ROWS, 128)


def _unpack_small(p, shapes):
    flat = p.reshape(-1)
    out, at = {}, 0
    for name, size in _SMALL:
        if name in shapes:
            out[name] = flat[at:at + size].reshape(shapes[name])
        at += -(-size // 128) * 128
    return out


def kernel(x, meta_tokens, norm_g, w_in, b_f, g_cq, g_ckv, w_uq, w_ukv, sinks, w_branch, w_out, final_g, loss_target, m_meta_tokens, m_norm_g, m_w_in, m_b_f, m_g_cq, m_g_ckv, m_w_uq, m_w_ukv, m_sinks, m_w_branch, m_w_out, m_final_g, v_meta_tokens, v_norm_g, v_w_in, v_b_f, v_g_cq, v_g_ckv, v_w_uq, v_w_ukv, v_sinks, v_w_branch, v_w_out, v_final_g):
    S = x.shape[1]
    L = BLK + S
    cx, cy, cc = _me()
    my_idx = 4 * cx + 2 * cy + cc

    def shards(l):
        return [t[l].astype(BF16) for t in (w_in, w_uq, w_ukv, w_branch, w_out)]

    def small_weights(l):
        return dict(norm_g=norm_g[l], b_f=b_f[l], g_cq=g_cq[l], g_ckv=g_ckv[l], sinks=sinks[l])

    def rest_weights(gw_uq, gw_ukv, gw_br, gw_out):
        return dict(w_uq=_cols_from_shards(gw_uq), w_ukv=_cols_from_shards(gw_ukv),
                    w_branch=jnp.stack([_cols_from_shards(gw_br[:, n]) for n in range(N_BRANCH)]),
                    w_out=gw_out.reshape(D_MODEL, D_MODEL))

    def layer_weights(l, gw_in, *gw_rest):
        return dict(small_weights(l), w_in=_pad_in(_cols_from_shards(gw_in)), **rest_weights(*gw_rest))

    gw_in0, g_meta = _comm(_Gather([shards(0)[0], meta_tokens]), name="gather_l0")
    layers = [dict(small_weights(0), w_in=_pad_in(_cols_from_shards(gw_in0)),
                   late=(_Gather(shards(0)[1:]), rest_weights)), None]
    meta_full = _cols_from_shards(g_meta)

    h = jnp.concatenate([jnp.zeros((PAD, D_MODEL), F32), meta_full, x[0]], axis=0)
    q_rotary = ((jnp.arange(HEADS * DQK_B) % DQK_B) >= MLA_NOPE).astype(F32)[None, :]
    tabs = dict(c=_rope_tables(L, DH // 2, LANES), bk=_rope_tables(L, MLA_ROPE // 2, LANES), bq=q_rotary)
    saved = [None] * DEPTH
    h, saved[0], gw1, layers[0] = _layer_fwd(h, layers[0], tabs, 0, side=_Gather(shards(1)))
    layers[1] = layer_weights(1, *gw1)
    h, saved[1], _, _ = _layer_fwd(h, layers[1], tabs, 1)
    loss_vec, dh, g_final = _loss_head(h, final_g, loss_target[0], name="loss_head")

    grads = [None] * DEPTH
    dh, grads[1], _, _ = _layer_bwd(dh, saved[1], layers[1], tabs, 1)
    dh, grads[0], recv1, recv0 = _layer_bwd(dh, saved[0], layers[0], tabs, 0, side=_Exchange(_grad_chunks(grads[1])),
                                            own_exchange=True)
    r_in, r_uq, r_ukv, r_br, r_out = (jnp.stack([a, b], axis=1) for a, b in zip(recv0, recv1))

    def stack(name):
        return jnp.stack([grads[l][name] for l in range(DEPTH)])

    small = _pack_small(dict(norm_g=stack("norm_g"), b_f=stack("b_f"), g_cq=stack("g_cq"), g_ckv=stack("g_ckv"),
                             sinks=stack("sinks"), final_g=g_final, loss=loss_vec[0, 0:1],
                             meta=dh[PAD:BLK]))
    (g_small,) = _comm(_Gather([small]), name="gather_small")

    def adam_big(w_, parts, m_, v_, name):
        shape = w_.shape
        C = shape[-1]
        R = math.prod(shape[:-1])
        outs = _adamw(w_.reshape(R, C), parts.reshape(parts.shape[0], R, C), m_.reshape(R, C), v_.reshape(R, C),
                      name=name)
        return [o.reshape(shape) for o in outs]

    res = {}
    res["w_in"] = adam_big(w_in, r_in, m_w_in, v_w_in, "adam_w_in")
    res["w_uq"] = adam_big(w_uq, r_uq, m_w_uq, v_w_uq, "adam_w_uq")
    res["w_ukv"] = adam_big(w_ukv, r_ukv, m_w_ukv, v_w_ukv, "adam_w_ukv")
    res["w_branch"] = adam_big(w_branch, r_br, m_w_branch, v_w_branch, "adam_w_branch")
    res["w_out"] = adam_big(w_out, r_out, m_w_out, v_w_out, "adam_w_out")

    small_w = dict(norm_g=norm_g, b_f=b_f, g_cq=g_cq, g_ckv=g_ckv, sinks=sinks, final_g=final_g)
    small_m = dict(norm_g=m_norm_g, b_f=m_b_f, g_cq=m_g_cq, g_ckv=m_g_ckv, sinks=m_sinks, final_g=m_final_g)
    small_v = dict(norm_g=v_norm_g, b_f=v_b_f, g_cq=v_g_cq, g_ckv=v_g_ckv, sinks=v_sinks, final_g=v_final_g)
    sm = _adamw(_pack_small(small_w), g_small, _pack_small(small_m), _pack_small(small_v), name="adam_small")
    shapes = {k: a.shape for k, a in small_w.items()}
    shapes_all = dict(shapes, loss=(), meta=(N_META, D_MODEL))
    sm_g = _unpack_small(sm[0], shapes_all)
    sm_d, sm_m, sm_v = (_unpack_small(t, shapes) for t in sm[1:])
    for k in shapes:
        res[k] = [sm_g[k], sm_d[k], sm_m[k], sm_v[k]]
    g_meta_mine = lax.dynamic_slice(sm_g["meta"], (0, my_idx * 128), (N_META, 128))
    res["meta_tokens"] = _adamw(meta_tokens, g_meta_mine[None], m_meta_tokens, v_meta_tokens, name="adam_meta")

    order = ["meta_tokens", "norm_g", "w_in", "b_f", "g_cq", "g_ckv", "w_uq", "w_ukv", "sinks", "w_branch", "w_out",
             "final_g"]
    grad_x = dh[BLK:][None]
    return (sm_g["loss"], grad_x, *[res[k][0] for k in order], *[res[k][1] for k in order],
            *[res[k][2] for k in order], *[res[k][3] for k in order])
```

```python
import functools
import math

import jax
import jax.numpy as jnp
from jax import lax
from jax.experimental import pallas as pl
from jax.experimental.pallas import tpu as pltpu

F32 = jnp.float32
BF16 = jnp.bfloat16

D_MODEL = 1024
DEPTH = 2
N_META = 16
BLK = 128
PAD = BLK - N_META
ROPE_THETA = 10000.0
EPS = 1e-6
NEG = -1e30
BIG = 1e30
HEADS = 8
DH = 64
MLA_NOPE = 64
MLA_ROPE = 32
MLA_QLORA = 384
MLA_KVLORA = 256
SWA_KV_HEADS = 2
WINDOW = 128
BRANCH_W = 512
N_BRANCH = 3
N_IN = 7592

ADAM_LR = 0.001
ADAM_B1 = 0.9
ADAM_B2 = 0.999
ADAM_EPS = 1e-08
ADAM_WD = 0.01
ADAM_STEP = 10

N_DEV = 8
MESH = pl.DeviceIdType.MESH

NP_IN = 8192
QKVA, ZA, FA = 0, 1536, 2048
CKV, KR, CQ = 2304, 2560, 2688
GATES = 3072
ZB = 6144
QKVC, ZC = 6912, 7680
_RUNS = ((0, 1536, QKVA), (1536, 1544, FA), (1544, 2056, ZA), (2056, 2440, CQ), (2440, 2696, CKV), (2696, 2728, KR),
         (2728, 3240, ZB), (3240, 4008, QKVC), (4008, 4520, ZC), (4520, 7592, GATES))

VMEM_LIMIT = 48 * 1024 * 1024
ATT_T = 384
ATT_HP = 1
ROW_T = 384
LANES = 128


def _pick(dim, prefs):
    for p in prefs:
        if dim % p == 0:
            return p
    return dim


def _cparams(sem):
    return pltpu.CompilerParams(dimension_semantics=sem, vmem_limit_bytes=VMEM_LIMIT)


def _side_cparams(side, sem):
    if side is None:
        return _cparams(sem)
    return pltpu.CompilerParams(dimension_semantics=("arbitrary",) * len(sem), vmem_limit_bytes=VMEM_LIMIT,
                                has_side_effects=True)


def _mm(a, b, *, ta=False, tb=False, add=None, out_dtype=F32, side=None, name):
    M = a.shape[1] if ta else a.shape[0]
    K = a.shape[0] if ta else a.shape[1]
    N = b.shape[0] if tb else b.shape[1]
    assert K == (b.shape[1] if tb else b.shape[0])
    tm = _pick(M, (704, 1024, 512, 384, 256, 128))
    tn = _pick(N, (1024, 768, 512, 384, 256, 128))
    tk = _pick(K, (2048, 1408, 1024, 768, 512, 384, 256, 128))
    nk = K // tk
    dims = (((0 if ta else 1,), (1 if tb else 0,)), ((), ()))

    sn = 0 if side is None else side.n
    n_in = 2 + (add is not None)
    grid = (M // tm, N // tn, nk)

    def body(*refs):
        a_ref, b_ref = refs[:2]
        c_ref = refs[2] if add is not None else None
        o_ref = refs[n_in + sn]
        scratch = refs[n_in + 2 * sn + 1:]
        if side is not None:
            step = (pl.program_id(0) * grid[1] + pl.program_id(1)) * nk + pl.program_id(2)
            _ride(side, [*refs[n_in:n_in + sn], *refs[n_in + sn + 1:n_in + 2 * sn + 1], *scratch[nk > 1:]],
                  step, grid[0] * grid[1] * nk)
        r = lax.dot_general(a_ref[...].astype(BF16), b_ref[...].astype(BF16), dims, preferred_element_type=F32)

        def finish(total):
            if c_ref is not None:
                total = total + c_ref[...]
            o_ref[...] = total.astype(out_dtype)

        if nk == 1:
            finish(r)
        else:
            acc = scratch[0]
            k = pl.program_id(2)

            @pl.when(k == 0)
            def _():
                acc[...] = r

            @pl.when(k > 0)
            def _():
                acc[...] += r

            @pl.when(k == nk - 1)
            def _():
                finish(acc[...])

    a_spec = pl.BlockSpec((tk, tm), lambda i, j, k: (k, i)) if ta else pl.BlockSpec((tm, tk), lambda i, j, k: (i, k))
    b_spec = pl.BlockSpec((tn, tk), lambda i, j, k: (j, k)) if tb else pl.BlockSpec((tk, tn), lambda i, j, k: (k, j))
    o_spec = pl.BlockSpec((tm, tn), lambda i, j, k: (i, j))
    any_spec = pl.BlockSpec(memory_space=pl.ANY)
    outs = pl.pallas_call(
        body, name=name,
        grid=grid,
        in_specs=[a_spec, b_spec] + ([o_spec] if add is not None else []) + [any_spec] * sn,
        out_specs=[o_spec] + [any_spec] * sn,
        out_shape=[jax.ShapeDtypeStruct((M, N), out_dtype)] + ([] if side is None else side.out_shape),
        scratch_shapes=([pltpu.VMEM((tm, tn), F32)] if nk > 1 else []) + ([] if side is None else side.scratch),
        compiler_params=_side_cparams(side, ("parallel", "parallel", "arbitrary")),
    )(*((a, b) if add is None else (a, b, add)), *([] if side is None else side.arrs))
    return outs[0] if side is None else outs


def _col_spec(tm, width, col):
    assert col % width == 0
    return pl.BlockSpec((tm, width), lambda i, _c=col // width: (i, _c))


def _rms_fwd(x, g, *, col=0, name):
    L = x.shape[0]
    D = g.shape[0]
    tm = ROW_T

    def body(x_ref, g_ref, y_ref):
        xv = x_ref[...]
        rstd = lax.rsqrt(jnp.mean(xv * xv, axis=-1, keepdims=True) + EPS)
        y_ref[...] = (xv * rstd * g_ref[...]).astype(BF16)

    return pl.pallas_call(
        body, name=name, grid=(L // tm,),
        in_specs=[_col_spec(tm, D, col), pl.BlockSpec((1, D), lambda i: (0, 0))],
        out_specs=pl.BlockSpec((tm, D), lambda i: (i, 0)),
        out_shape=jax.ShapeDtypeStruct((L, D), BF16),
        compiler_params=_cparams(("parallel",)),
    )(x, g.reshape(1, D))


def _rms_bwd(dy, x, g, *, col=0, add=None, into=None, name):
    L = x.shape[0]
    D = g.shape[0]
    tm = ROW_T

    def body(*refs):
        dy_ref, x_ref, g_ref = refs[:3]
        add_ref = refs[3] if add is not None else None
        dx_ref, dg_ref = refs[-2:]
        i = pl.program_id(0)
        xv = x_ref[...]
        dyv = dy_ref[...]
        rstd = lax.rsqrt(jnp.mean(xv * xv, axis=-1, keepdims=True) + EPS)
        xhat = xv * rstd
        part = jnp.sum(dyv * xhat, axis=0, keepdims=True)

        @pl.when(i == 0)
        def _():
            dg_ref[...] = part

        @pl.when(i > 0)
        def _():
            dg_ref[...] += part

        dxh = dyv * g_ref[...]
        dx = rstd * (dxh - xhat * jnp.mean(dxh * xhat, axis=-1, keepdims=True))
        if add_ref is not None:
            dx = dx + add_ref[...]
        dx_ref[...] = dx.astype(dx_ref.dtype)

    row = pl.BlockSpec((tm, D), lambda i: (i, 0))
    in_specs = [row, _col_spec(tm, D, col), pl.BlockSpec((1, D), lambda i: (0, 0))]
    args = [dy, x, g.reshape(1, D)]
    aliases = {}
    if add is not None:
        in_specs.append(row)
        args.append(add)
    if into is not None:
        in_specs.append(pl.BlockSpec(memory_space=pl.ANY))
        args.append(into)
        aliases = {len(args) - 1: 0}
        dx_spec, dx_shape = _col_spec(tm, D, col), jax.ShapeDtypeStruct(into.shape, into.dtype)
    else:
        dx_spec, dx_shape = row, jax.ShapeDtypeStruct((L, D), F32)
    dx, dg = pl.pallas_call(
        body, name=name, grid=(L // tm,),
        in_specs=in_specs,
        out_specs=[dx_spec, pl.BlockSpec((1, D), lambda i: (0, 0))],
        out_shape=[dx_shape, jax.ShapeDtypeStruct((1, D), F32)],
        input_output_aliases=aliases,
        compiler_params=_cparams(("arbitrary",)),
    )(*args)
    return dx, dg.reshape(D)


def _loss_head(h, g, target, *, name):
    L, D = h.shape
    nb = L // BLK

    def body(h_ref, g_ref, t_ref, loss_ref, dh_ref, dg_ref):
        i = pl.program_id(0)

        @pl.when(i == 0)
        def _():
            loss_ref[...] = jnp.zeros_like(loss_ref)
            dg_ref[...] = jnp.zeros_like(dg_ref)
            dh_ref[...] = jnp.zeros_like(dh_ref)

        @pl.when(i > 0)
        def _():
            xv = h_ref[...]
            gv = g_ref[...]
            rstd = lax.rsqrt(jnp.mean(xv * xv, axis=-1, keepdims=True) + EPS)
            xhat = xv * rstd
            err = xhat * gv - t_ref[...]
            row = jnp.mean(err * err, axis=-1, keepdims=True)
            loss_ref[...] += 0.5 * jnp.sum(row, axis=0, keepdims=True)
            dy = err * (1.0 / D)
            dg_ref[...] += jnp.sum(dy * xhat, axis=0, keepdims=True)
            dxh = dy * gv
            dh_ref[...] = rstd * (dxh - xhat * jnp.mean(dxh * xhat, axis=-1, keepdims=True))

    loss, dh, dg = pl.pallas_call(
        body, name=name, grid=(nb,),
        in_specs=[pl.BlockSpec((BLK, D), lambda i: (i, 0)), pl.BlockSpec((1, D), lambda i: (0, 0)),
                  pl.BlockSpec((BLK, D), lambda i: (jnp.maximum(i - 1, 0), 0))],
        out_specs=[pl.BlockSpec((1, 128), lambda i: (0, 0)), pl.BlockSpec((BLK, D), lambda i: (i, 0)),
                   pl.BlockSpec((1, D), lambda i: (0, 0))],
        out_shape=[jax.ShapeDtypeStruct((1, 128), F32), jax.ShapeDtypeStruct((L, D), F32),
                   jax.ShapeDtypeStruct((1, D), F32)],
        compiler_params=_cparams(("arbitrary",)),
    )(h, g.reshape(1, D), target)
    return loss, dh, dg.reshape(D)


_NT = (((1,), (1,)), ((), ()))
_TN = (((0,), (0,)), ((), ()))


def _attn_fwd(q, k, v, kbias, *, T, window=False, sink=None, side=None, name):
    sn = 0 if side is None else side.n
    H, L, dk = q.shape
    Hkv = k.shape[0]
    dv = v.shape[2]
    G = H // Hkv
    nt = L // T
    Hb = kbias.shape[0]
    reps = T // LANES
    assert not window or T == WINDOW
    HP = G if G > 1 else ATT_HP
    NS = 1 if G > 1 else HP
    R = HP * T // NS
    HKV = HP // G
    HB = HP if Hb > 1 else 1
    assert G == 1 or Hb == 1

    def body(*refs):
        q_ref, k_ref, v_ref, kb_ref = refs[:4]
        n = 4
        if sink is not None:
            sk_ref = refs[n]
            n += 1
        side_in = refs[n:n + sn]
        n += sn
        o_ref, lse_ref = refs[n:n + 2]
        side_out = refs[n + 2:n + 2 + sn]
        n += 2 + sn
        m_scs, l_scs, acc_scs, buf_a, buf_b = (refs[n + t * NS:n + (t + 1) * NS] for t in range(5))
        i = pl.program_id(1)
        if side is not None:
            _ride(side, [*side_in, *side_out, *refs[n + 5 * NS:]], pl.program_id(0) * nt + i, (H // HP) * nt)
        for a in range(NS):
            if sink is not None:
                sk = [jnp.broadcast_to(sk_ref[b, :, 0:1], (T, LANES)) for b in range(HP)]
                m_scs[a][...] = jnp.concatenate(sk, axis=0) if G > 1 else sk[a]
                l_scs[a][...] = jnp.ones((R, LANES), F32)
            else:
                m_scs[a][...] = jnp.full((R, LANES), NEG, F32)
                l_scs[a][...] = jnp.zeros((R, LANES), F32)
            acc_scs[a][...] = jnp.zeros((R, dv), F32)
        row = lax.broadcasted_iota(jnp.int32, (R, T), 0) & (T - 1) if G > 1 else \
            lax.broadcasted_iota(jnp.int32, (R, T), 0)
        col = lax.broadcasted_iota(jnp.int32, (R, T), 1)

        def logits(a, j):
            rows = pl.ds(pl.multiple_of(j * T, T), T)
            qv = q_ref[...].reshape(R, dk) if G > 1 else q_ref[a]
            s = lax.dot_general(qv, k_ref[a, rows, :], _NT, preferred_element_type=F32)
            return s - kb_ref[a if HB > 1 else 0, j]

        def update(a, s, j, kind):
            rows = pl.ds(pl.multiple_of(j * T, T), T)
            m_sc, l_sc, acc_sc = m_scs[a], l_scs[a], acc_scs[a]
            if kind == "diag":
                s = jnp.where(row >= col, s, NEG)
            elif kind == "prev":
                s = jnp.where(col > row, s, NEG)
            m_prev = m_sc[...]
            m_new = jnp.maximum(m_prev, jnp.max(s, axis=1, keepdims=True))
            alpha = jnp.exp(m_prev - m_new)
            p = jnp.exp(s - jnp.tile(m_new, (1, reps)))
            l_sc[...] = alpha * l_sc[...] + jnp.sum(p, axis=1, keepdims=True)
            acc_sc[...] = alpha[:, :dv] * acc_sc[...] + jnp.dot(p.astype(BF16), v_ref[a, rows, :],
                                                                preferred_element_type=F32)
            m_sc[...] = m_new

        if window:
            @pl.when(i > 0)
            def _():
                for a in range(NS):
                    update(a, logits(a, i - 1), i - 1, "prev")

            for a in range(NS):
                update(a, logits(a, i), i, "diag")
        else:
            def fill(buf, j):
                for a in range(NS):
                    buf[a][...] = logits(a, j)

            def drain(buf, j, kind):
                for a in range(NS):
                    update(a, buf[a][...], j, kind)

            fill(buf_a, 0)

            def pair(t, c):
                fill(buf_b, 2 * t + 1)
                drain(buf_a, 2 * t, "full")
                fill(buf_a, 2 * t + 2)
                drain(buf_b, 2 * t + 1, "full")
                return c

            lax.fori_loop(0, i // 2, pair, 0)

            @pl.when(i % 2 == 1)
            def _():
                fill(buf_b, i)
                drain(buf_a, i - 1, "full")
                drain(buf_b, i, "diag")

            @pl.when(i % 2 == 0)
            def _():
                drain(buf_a, i, "diag")

        for a in range(NS):
            lv = l_scs[a][...]
            ov = acc_scs[a][...] / lv[:, :dv]
            lsev = (m_scs[a][...] + jnp.log(lv))[:, 0:1]
            if G > 1:
                o_ref[...] = ov.reshape(HP, T, dv)
                lse_ref[...] = lsev.reshape(HP, T, 1)
            else:
                o_ref[a] = ov
                lse_ref[a] = lsev

    in_specs = [pl.BlockSpec((HP, T, dk), lambda h, i: (h, i, 0)),
                pl.BlockSpec((HKV, L, dk), lambda h, i: (h, 0, 0)),
                pl.BlockSpec((HKV, L, dv), lambda h, i: (h, 0, 0)),
                pl.BlockSpec((HB, nt, 1, T), lambda h, i: (h if Hb > 1 else 0, 0, 0, 0))]
    args = [q, k, v, kbias]
    if sink is not None:
        in_specs += [pl.BlockSpec((HP, 1, LANES), lambda h, i: (h, 0, 0))]
        args += [sink]
    any_spec = pl.BlockSpec(memory_space=pl.ANY)
    return pl.pallas_call(
        body, name=name, grid=(H // HP, nt),
        in_specs=in_specs + [any_spec] * sn,
        out_specs=[pl.BlockSpec((HP, T, dv), lambda h, i: (h, i, 0)),
                   pl.BlockSpec((HP, T, 1), lambda h, i: (h, i, 0))] + [any_spec] * sn,
        out_shape=[jax.ShapeDtypeStruct((H, L, dv), F32), jax.ShapeDtypeStruct((H, L, 1), F32)]
        + ([] if side is None else side.out_shape),
        scratch_shapes=[pltpu.VMEM((R, LANES), F32)] * (2 * NS) + [pltpu.VMEM((R, dv), F32)] * NS
        + [pltpu.VMEM((R, T), F32)] * (2 * NS) + ([] if side is None else side.scratch),
        compiler_params=_side_cparams(side, ("parallel", "arbitrary")),
    )(*args, *([] if side is None else side.arrs))


def _attn_bwd(q, k, v, kbias, o, lse, do, *, T, fox=False, side=None, name):
    sn = 0 if side is None else side.n
    H, L, dk = q.shape
    Hkv = k.shape[0]
    dv = v.shape[2]
    G = H // Hkv
    nt = L // T
    Hb = kbias.shape[0]

    def body(*refs):
        q_ref, k_ref, v_ref, kb_ref, o_ref, lse_ref, do_ref = refs[:7]
        side_in = refs[7:7 + sn]
        n = 7 + sn
        dq_ref, dk_ref, dv_ref = refs[n:n + 3]
        n += 3
        if fox:
            dcq_ref, dck_ref = refs[n:n + 2]
            n += 2
        side_out = refs[n:n + sn]
        n += sn
        delta_sc = refs[n]
        buf_a, buf_b = refs[n + 1:n + 3], refs[n + 3:n + 5]
        j = pl.program_id(1)
        if side is not None:
            _ride(side, [*side_in, *side_out, *refs[n + 5:]], pl.program_id(0) * nt + j, H * nt)

        @pl.when(j == 0)
        def _():
            dq_ref[...] = jnp.zeros_like(dq_ref)
            if fox:
                dcq_ref[...] = jnp.zeros_like(dcq_ref)

            def dl(i, c):
                rows = pl.ds(pl.multiple_of(i * T, T), T)
                delta_sc[rows, :] = jnp.sum(do_ref[0, rows, :] * o_ref[0, rows, :], axis=1, keepdims=True)
                return c

            lax.fori_loop(0, nt, dl, 0)

        dk_ref[...] = jnp.zeros_like(dk_ref)
        dv_ref[...] = jnp.zeros_like(dv_ref)
        if fox:
            dck_ref[...] = jnp.zeros_like(dck_ref)
        kb = k_ref[0]
        vb = v_ref[0]
        kbias_j = kb_ref[0, j]
        row = lax.broadcasted_iota(jnp.int32, (T, T), 0)
        col = lax.broadcasted_iota(jnp.int32, (T, T), 1)

        def fill(buf, i):
            rows = pl.ds(pl.multiple_of(i * T, T), T)
            buf[0][...] = lax.dot_general(q_ref[0, rows, :], kb, _NT, preferred_element_type=F32) - kbias_j
            buf[1][...] = lax.dot_general(do_ref[0, rows, :].astype(BF16), vb, _NT, preferred_element_type=F32)

        def drain(buf, i, kind):
            rows = pl.ds(pl.multiple_of(i * T, T), T)
            qb = q_ref[0, rows, :]
            dob = do_ref[0, rows, :].astype(BF16)
            s = buf[0][...]
            if kind == "diag":
                s = jnp.where(row >= col, s, NEG)
            p = jnp.exp(s - lse_ref[0, rows, :])
            ds = p * (buf[1][...] - delta_sc[rows, :])
            dsb = ds.astype(BF16)
            dv_ref[0] += lax.dot_general(p.astype(BF16), dob, _TN, preferred_element_type=F32)
            dk_ref[0] += lax.dot_general(dsb, qb, _TN, preferred_element_type=F32)
            dq_ref[0, rows, :] += jnp.dot(dsb, kb, preferred_element_type=F32)
            if fox:
                dcq_ref[0, rows, :] += jnp.sum(ds, axis=1, keepdims=True)
                dck_ref[0, 0] += -jnp.sum(ds, axis=0, keepdims=True)

        fill(buf_a, j)
        drain(buf_a, j, "diag")
        first = j + 1
        rest = nt - first

        @pl.when(rest > 0)
        def _():
            fill(buf_a, first)

        def pair(t, c):
            i0 = first + 2 * t
            fill(buf_b, i0 + 1)
            drain(buf_a, i0, "full")
            fill(buf_a, jnp.minimum(i0 + 2, nt - 1))
            drain(buf_b, i0 + 1, "full")
            return c

        lax.fori_loop(0, rest // 2, pair, 0)

        @pl.when(rest % 2 == 1)
        def _():
            drain(buf_a, nt - 1, "full")

    in_specs = [pl.BlockSpec((1, L, dk), lambda h, j: (h, 0, 0)),
                pl.BlockSpec((1, T, dk), lambda h, j: (h // G, j, 0)),
                pl.BlockSpec((1, T, dv), lambda h, j: (h // G, j, 0)),
                pl.BlockSpec((1, nt, 1, T), lambda h, j: (h if Hb > 1 else 0, 0, 0, 0)),
                pl.BlockSpec((1, L, dv), lambda h, j: (h, 0, 0)),
                pl.BlockSpec((1, L, 1), lambda h, j: (h, 0, 0)),
                pl.BlockSpec((1, L, dv), lambda h, j: (h, 0, 0))]
    out_specs = [pl.BlockSpec((1, L, dk), lambda h, j: (h, 0, 0)),
                 pl.BlockSpec((1, T, dk), lambda h, j: (h, j, 0)),
                 pl.BlockSpec((1, T, dv), lambda h, j: (h, j, 0))]
    out_shape = [jax.ShapeDtypeStruct((H, L, dk), F32), jax.ShapeDtypeStruct((H, L, dk), F32),
                 jax.ShapeDtypeStruct((H, L, dv), F32)]
    if fox:
        out_specs += [pl.BlockSpec((1, L, 1), lambda h, j: (h, 0, 0)),
                      pl.BlockSpec((1, 1, 1, T), lambda h, j: (h, j, 0, 0))]
        out_shape += [jax.ShapeDtypeStruct((H, L, 1), F32), jax.ShapeDtypeStruct((H, nt, 1, T), F32)]
    any_spec = pl.BlockSpec(memory_space=pl.ANY)
    outs = pl.pallas_call(
        body, name=name, grid=(H, nt),
        in_specs=in_specs + [any_spec] * sn, out_specs=out_specs + [any_spec] * sn,
        out_shape=out_shape + ([] if side is None else side.out_shape),
        scratch_shapes=[pltpu.VMEM((L, 1), F32)] + [pltpu.VMEM((T, T), F32)] * 4
        + ([] if side is None else side.scratch),
        compiler_params=_side_cparams(side, ("parallel", "arbitrary")),
    )(q, k, v, kbias, o, lse, do, *([] if side is None else side.arrs))
    main, rest = outs[:len(outs) - sn], outs[len(outs) - sn:]
    return (*(main if fox else (*main, None, None)), *rest)


def _attn_bwd_t(q, k, v, kbias_col, lse_row, delta_row, do, *, T, fox=False, side=None, name):
    sn = 0 if side is None else side.n
    H, L, dk = q.shape
    dv = v.shape[2]
    nt = L // T
    Hb = kbias_col.shape[0]

    def body(*refs):
        q_ref, k_ref, v_ref, kb_ref, lse_ref, dl_ref, do_ref = refs[:7]
        side_in = refs[7:7 + sn]
        n = 7 + sn
        dq_ref, dk_ref, dv_ref = refs[n:n + 3]
        n += 3
        if fox:
            dcq_ref, dck_ref = refs[n:n + 2]
            n += 2
        side_out = refs[n:n + sn]
        n += sn
        buf_a, buf_b = refs[n:n + 2], refs[n + 2:n + 4]
        j = pl.program_id(1)
        if side is not None:
            _ride(side, [*side_in, *side_out, *refs[n + 4:]], pl.program_id(0) * nt + j, H * nt)

        @pl.when(j == 0)
        def _():
            dq_ref[...] = jnp.zeros_like(dq_ref)
            if fox:
                dcq_ref[...] = jnp.zeros_like(dcq_ref)

        dk_ref[...] = jnp.zeros_like(dk_ref)
        dv_ref[...] = jnp.zeros_like(dv_ref)
        if fox:
            dck_ref[...] = jnp.zeros_like(dck_ref)
        kb = k_ref[0]
        vb = v_ref[0]
        kbias_j = kb_ref[0]
        key = lax.broadcasted_iota(jnp.int32, (T, T), 0)
        qry = lax.broadcasted_iota(jnp.int32, (T, T), 1)

        def fill(buf, i):
            rows = pl.ds(pl.multiple_of(i * T, T), T)
            buf[0][...] = lax.dot_general(kb, q_ref[0, rows, :], _NT, preferred_element_type=F32) - kbias_j
            buf[1][...] = lax.dot_general(vb, do_ref[0, rows, :], _NT, preferred_element_type=F32)

        def drain(buf, i, kind):
            rows = pl.ds(pl.multiple_of(i * T, T), T)
            st = buf[0][...]
            if kind == "diag":
                st = jnp.where(key <= qry, st, NEG)
            pt = jnp.exp(st - lse_ref[0, i])
            dst = pt * (buf[1][...] - dl_ref[0, i])
            dsb = dst.astype(BF16)
            dv_ref[0] += jnp.dot(pt.astype(BF16), do_ref[0, rows, :], preferred_element_type=F32)
            dk_ref[0] += jnp.dot(dsb, q_ref[0, rows, :], preferred_element_type=F32)
            dq_ref[0, rows, :] += lax.dot_general(dsb, kb, _TN, preferred_element_type=F32)
            if fox:
                dcq_ref[0, i] += jnp.sum(dst, axis=0, keepdims=True)
                dck_ref[0] += -jnp.sum(dst, axis=1, keepdims=True)

        fill(buf_a, j)
        drain(buf_a, j, "diag")
        first = j + 1
        rest = nt - first

        @pl.when(rest > 0)
        def _():
            fill(buf_a, first)

        def pair(t, c):
            i0 = first + 2 * t
            fill(buf_b, i0 + 1)
            drain(buf_a, i0, "full")
            fill(buf_a, jnp.minimum(i0 + 2, nt - 1))
            drain(buf_b, i0 + 1, "full")
            return c

        lax.fori_loop(0, rest // 2, pair, 0)

        @pl.when(rest % 2 == 1)
        def _():
            drain(buf_a, nt - 1, "full")

    rows_spec = pl.BlockSpec((1, nt, 1, T), lambda h, j: (h, 0, 0, 0))
    in_specs = [pl.BlockSpec((1, L, dk), lambda h, j: (h, 0, 0)),
                pl.BlockSpec((1, T, dk), lambda h, j: (h, j, 0)),
                pl.BlockSpec((1, T, dv), lambda h, j: (h, j, 0)),
                pl.BlockSpec((1, T, 1), lambda h, j: (h if Hb > 1 else 0, j, 0)),
                rows_spec, rows_spec,
                pl.BlockSpec((1, L, dv), lambda h, j: (h, 0, 0))]
    out_specs = [pl.BlockSpec((1, L, dk), lambda h, j: (h, 0, 0)),
                 pl.BlockSpec((1, T, dk), lambda h, j: (h, j, 0)),
                 pl.BlockSpec((1, T, dv), lambda h, j: (h, j, 0))]
    out_shape = [jax.ShapeDtypeStruct((H, L, dk), F32), jax.ShapeDtypeStruct((H, L, dk), F32),
                 jax.ShapeDtypeStruct((H, L, dv), F32)]
    if fox:
        out_specs += [rows_spec, pl.BlockSpec((1, T, 1), lambda h, j: (h, j, 0))]
        out_shape += [jax.ShapeDtypeStruct((H, nt, 1, T), F32), jax.ShapeDtypeStruct((H, L, 1), F32)]
    any_spec = pl.BlockSpec(memory_space=pl.ANY)
    outs = pl.pallas_call(
        body, name=name, grid=(H, nt),
        in_specs=in_specs + [any_spec] * sn, out_specs=out_specs + [any_spec] * sn,
        out_shape=out_shape + ([] if side is None else side.out_shape),
        scratch_shapes=[pltpu.VMEM((T, T), F32)] * 4 + ([] if side is None else side.scratch),
        compiler_params=_side_cparams(side, ("parallel", "arbitrary")),
    )(q, k, v, kbias_col, lse_row, delta_row, do, *([] if side is None else side.arrs))
    main, rest = outs[:len(outs) - sn], outs[len(outs) - sn:]
    return (*(main if fox else (*main, None, None)), *rest)


def _attn_bwd_window(q, k, v, kbias, o, lse, do, *, name):
    H, L, dk = q.shape
    Hkv = k.shape[0]
    dv = v.shape[2]
    G = H // Hkv
    T = WINDOW
    nt = L // T
    R = G * T

    def body(q_ref, kc_ref, kp_ref, vc_ref, vp_ref, kb_ref, o_ref, lse_ref, do_ref, dq_ref, dk_ref, dv_ref,
             dk_sc, dv_sc):
        i = pl.program_id(1)
        row = lax.broadcasted_iota(jnp.int32, (R, T), 0) & (T - 1)
        col = lax.broadcasted_iota(jnp.int32, (R, T), 1)

        @pl.when(i == 0)
        def _():
            dk_sc[...] = jnp.zeros_like(dk_sc)
            dv_sc[...] = jnp.zeros_like(dv_sc)

        @pl.when(i == nt)
        def _():
            dk_ref[0] = dk_sc[...]
            dv_ref[0] = dv_sc[...]

        @pl.when(i < nt)
        def _():
            qb = q_ref[...].reshape(R, dk)
            dof = do_ref[...].reshape(R, dv)
            dob = dof.astype(BF16)
            lse_c = lse_ref[...].reshape(R, 1)
            delta = jnp.sum(dof * o_ref[...].reshape(R, dv), axis=1, keepdims=True)

            def grads(kt, vt, kbias_j, mask):
                s = lax.dot_general(qb, kt, _NT, preferred_element_type=F32) - kbias_j
                p = jnp.exp(jnp.where(mask, s, NEG) - lse_c)
                dp = lax.dot_general(dob, vt, _NT, preferred_element_type=F32)
                ds = (p * (dp - delta)).astype(BF16)
                return (jnp.dot(ds, kt, preferred_element_type=F32),
                        lax.dot_general(ds, qb, _TN, preferred_element_type=F32),
                        lax.dot_general(p.astype(BF16), dob, _TN, preferred_element_type=F32))

            ip = jnp.maximum(i - 1, 0)
            dq_p, dk_p, dv_p = grads(kp_ref[0], vp_ref[0], kb_ref[0, ip], (col > row) & (i > 0))
            dq_c, dk_c, dv_c = grads(kc_ref[0], vc_ref[0], kb_ref[0, i], row >= col)
            dq_ref[...] = (dq_p + dq_c).reshape(G, T, dk)
            dk_ref[0] = dk_sc[...] + dk_p
            dv_ref[0] = dv_sc[...] + dv_p
            dk_sc[...] = dk_c
            dv_sc[...] = dv_c

    def cur(i):
        return jnp.minimum(i, nt - 1)

    def prev(i):
        return jnp.maximum(jnp.minimum(i, nt - 1) - 1, 0)

    def written(i):
        return jnp.maximum(i - 1, 0)

    qs = lambda d: pl.BlockSpec((G, T, d), lambda h, i: (h, cur(i), 0))
    return pl.pallas_call(
        body, name=name, grid=(Hkv, nt + 1),
        in_specs=[qs(dk),
                  pl.BlockSpec((1, T, dk), lambda h, i: (h, cur(i), 0)),
                  pl.BlockSpec((1, T, dk), lambda h, i: (h, prev(i), 0)),
                  pl.BlockSpec((1, T, dv), lambda h, i: (h, cur(i), 0)),
                  pl.BlockSpec((1, T, dv), lambda h, i: (h, prev(i), 0)),
                  pl.BlockSpec((1, nt, 1, T), lambda h, i: (0, 0, 0, 0)),
                  qs(dv), qs(1), qs(dv)],
        out_specs=[qs(dk),
                   pl.BlockSpec((1, T, dk), lambda h, i: (h, written(i), 0)),
                   pl.BlockSpec((1, T, dv), lambda h, i: (h, written(i), 0))],
        out_shape=[jax.ShapeDtypeStruct((H, L, dk), F32), jax.ShapeDtypeStruct((Hkv, L, dk), F32),
                   jax.ShapeDtypeStruct((Hkv, L, dv), F32)],
        scratch_shapes=[pltpu.VMEM((T, dk), F32), pltpu.VMEM((T, dv), F32)],
        compiler_params=_cparams(("parallel", "arbitrary")),
    )(q, k, k, v, v, kbias, o, lse, do)


def _adamw(w, gparts, m, v, *, name):
    n, R, C = gparts.shape
    tr = _pick(R, (128, 64, 32, 16, 8))
    c1 = 1.0 - ADAM_B1 ** ADAM_STEP
    c2 = 1.0 - ADAM_B2 ** ADAM_STEP

    def body(w_ref, g_ref, m_ref, v_ref, go_ref, d_ref, mo_ref, vo_ref):
        g = g_ref[0].astype(F32)
        for t in range(1, n):
            g = g + g_ref[t].astype(F32)
        mn = ADAM_B1 * m_ref[...] + (1.0 - ADAM_B1) * g
        vn = ADAM_B2 * v_ref[...] + (1.0 - ADAM_B2) * (g * g)
        go_ref[...] = g
        mo_ref[...] = mn
        vo_ref[...] = vn
        d_ref[...] = -ADAM_LR * ((mn / c1) / (jnp.sqrt(vn / c2) + ADAM_EPS) + ADAM_WD * w_ref[...])

    spec = pl.BlockSpec((tr, C), lambda i: (i, 0))
    return pl.pallas_call(
        body, name=name, grid=(R // tr,),
        in_specs=[spec, pl.BlockSpec((n, tr, C), lambda i: (0, i, 0)), spec, spec],
        out_specs=[spec] * 4,
        out_shape=[jax.ShapeDtypeStruct((R, C), F32)] * 4,
        compiler_params=_cparams(("parallel",)),
    )(w, gparts, m, v)


def _me():
    return lax.axis_index("x"), lax.axis_index("y"), lax.axis_index("c")


class _CommJob:
    def __init__(self, arrs):
        self.arrs = list(arrs)
        self.n = len(arrs)
        self.scratch = [pltpu.SemaphoreType.DMA((self.n, 7)), pltpu.SemaphoreType.DMA((self.n, 7)),
                        pltpu.SemaphoreType.DMA((self.n,))]

    def bind(self, refs):
        n = self.n
        self.ins, self.outs = refs[:n], refs[n:2 * n]
        self.send_sems, self.recv_sems, self.local_sems = refs[2 * n:2 * n + 3]

    def middle(self):
        pass


class _Gather(_CommJob):
    def __init__(self, arrs):
        super().__init__(arrs)
        self.out_shape = [jax.ShapeDtypeStruct((N_DEV, *a.shape), a.dtype) for a in arrs]

    def _where(self):
        x, y, c = _me()
        return (x, y, c), (x, y, 1 - c), [(1 - x, y), (x, 1 - y), (1 - x, 1 - y)], c

    def _copy(self, t, k, block, to, src=None):
        dst = self.outs[t].at[4 * block[0] + 2 * block[1] + block[2]]
        return pltpu.make_async_remote_copy(
            src_ref=dst if src is None else src, dst_ref=dst,
            send_sem=self.send_sems.at[t, k], recv_sem=self.recv_sems.at[t, k],
            device_id=to, device_id_type=MESH)

    def _mine(self, t, me):
        return pltpu.make_async_copy(self.ins[t], self.outs[t].at[4 * me[0] + 2 * me[1] + me[2]],
                                     self.local_sems.at[t])

    def _first(self, t, me, sibling, chips, c):
        return [self._copy(t, 0, me, sibling, src=self.ins[t])] + \
               [self._copy(t, 1 + j, me, (*chip, c), src=self.ins[t]) for j, chip in enumerate(chips)]

    def start(self):
        me, sibling, chips, c = self._where()
        for t in range(self.n):
            self._mine(t, me).start()
            for cp in self._first(t, me, sibling, chips, c):
                cp.start()

    def middle(self):
        me, sibling, chips, c = self._where()
        for j, chip in enumerate(chips):
            for t in range(self.n):
                self._copy(t, 1 + j, (*chip, c), me).wait_recv()
                self._copy(t, 4 + j, (*chip, c), sibling).start()

    def finish(self):
        me, sibling, chips, c = self._where()
        for t in range(self.n):
            self._copy(t, 0, sibling, me).wait_recv()
            for j, chip in enumerate(chips):
                self._copy(t, 4 + j, (*chip, 1 - c), me).wait_recv()
        for t in range(self.n):
            for cp in self._first(t, me, sibling, chips, c):
                cp.wait_send()
            for j, chip in enumerate(chips):
                self._copy(t, 4 + j, (*chip, c), sibling).wait_send()
            self._mine(t, me).wait()


class _Exchange(_CommJob):
    def __init__(self, arrs):
        super().__init__(arrs)
        self.out_shape = [jax.ShapeDtypeStruct(a.shape, a.dtype) for a in arrs]

    def _copies(self, t):
        x, y, c = _me()
        my_idx = 4 * x + 2 * y + c
        pairs = []
        for k in range(1, N_DEV):
            peer = (x ^ ((k >> 2) & 1), y ^ ((k >> 1) & 1), c ^ (k & 1))
            peer_idx = 4 * peer[0] + 2 * peer[1] + peer[2]
            sems = dict(send_sem=self.send_sems.at[t, k - 1], recv_sem=self.recv_sems.at[t, k - 1],
                        device_id=peer, device_id_type=MESH)
            pairs.append((pltpu.make_async_remote_copy(src_ref=self.ins[t].at[peer_idx],
                                                       dst_ref=self.outs[t].at[my_idx], **sems),
                          pltpu.make_async_remote_copy(src_ref=self.ins[t].at[peer_idx],
                                                       dst_ref=self.outs[t].at[peer_idx], **sems)))
        return pairs

    def _mine(self, t):
        x, y, c = _me()
        my_idx = 4 * x + 2 * y + c
        return pltpu.make_async_copy(self.ins[t].at[my_idx], self.outs[t].at[my_idx], self.local_sems.at[t])

    def start(self):
        for t in range(self.n):
            self._mine(t).start()
            for snd, _ in self._copies(t):
                snd.start()

    def finish(self):
        for t in range(self.n):
            pairs = self._copies(t)
            for _, rcv in pairs:
                rcv.wait_recv()
            for snd, _ in pairs:
                snd.wait_send()
            self._mine(t).wait()


def _comm(job, *, name):
    def body(*refs):
        job.bind(refs)
        job.start()
        job.middle()
        job.finish()

    any_spec = pl.BlockSpec(memory_space=pl.ANY)
    return pl.pallas_call(
        body, name=name,
        in_specs=[any_spec] * job.n, out_specs=[any_spec] * job.n,
        out_shape=job.out_shape, scratch_shapes=job.scratch,
        compiler_params=pltpu.CompilerParams(has_side_effects=True),
    )(*job.arrs)


def _ride(job, refs, step, total):
    job.bind(refs)

    @pl.when(step == 0)
    def _():
        job.start()

    @pl.when(step == (total * 3) // 5)
    def _():
        job.middle()

    @pl.when(step == total - 1)
    def _():
        job.finish()


SCALE_A = DH ** -0.5
SCALE_B = (MLA_NOPE + MLA_ROPE) ** -0.5
SCALE_C = DH ** -0.5
DQK_B = MLA_NOPE + MLA_ROPE


def _rope_tables(L, half, width):
    pos = (jnp.arange(L) - PAD).astype(F32)
    inv = ROPE_THETA ** (-jnp.arange(half, dtype=F32) / half)
    lane = jnp.arange(width)
    ang = pos[:, None] * inv[lane % half][None, :]
    sign = jnp.where(lane % (2 * half) < half, -1.0, 1.0).astype(F32)
    return jnp.cos(ang), jnp.sin(ang) * sign[None, :]


def _rope_lanes(x, cos, sin, half):
    W = x.shape[1]
    lane = lax.broadcasted_iota(jnp.int32, x.shape, 1)
    first = (lane & (2 * half - 1)) < half
    partner = jnp.where(first, pltpu.roll(x, W - half, 1), pltpu.roll(x, half, 1))
    return x * cos + partner * sin


def _tile_lanes(t, width):
    return t if t.shape[1] == width else jnp.tile(t, (1, width // t.shape[1]))


def _split(x, H, d, dst):
    for h in range(H):
        dst[h] = x[:, d * h:d * (h + 1)].astype(dst.dtype)


def _join(src, H):
    return jnp.concatenate([src[h] for h in range(H)], axis=1)


def _head_spec(H, tm, d):
    return pl.BlockSpec((H, tm, d), lambda i: (0, i, 0))


def _prep_a(proj, *, name):
    L = proj.shape[0]
    tm = ROW_T

    def body(x_ref, qo, ko, vo):
        _split(x_ref[:, 0:512] * SCALE_A, HEADS, DH, qo)
        _split(x_ref[:, 512:1024], HEADS, DH, ko)
        _split(x_ref[:, 1024:1536], HEADS, DH, vo)

    return pl.pallas_call(
        body, name=name, grid=(L // tm,),
        in_specs=[_col_spec(tm, 1536, QKVA)],
        out_specs=[_head_spec(HEADS, tm, DH)] * 3,
        out_shape=[jax.ShapeDtypeStruct((HEADS, L, DH), BF16)] * 3,
        compiler_params=_cparams(("parallel",)),
    )(proj)


def _unprep_a(dq, dk, dv, dproj, *, name):
    L = dq.shape[1]
    tm = ROW_T

    def body(dq_ref, dk_ref, dv_ref, _, dp_ref):
        dp_ref[:, 0:512] = (_join(dq_ref, HEADS) * SCALE_A).astype(BF16)
        dp_ref[:, 512:1024] = _join(dk_ref, HEADS).astype(BF16)
        dp_ref[:, 1024:1536] = _join(dv_ref, HEADS).astype(BF16)

    hs = _head_spec(HEADS, tm, DH)
    return pl.pallas_call(
        body, name=name, grid=(L // tm,),
        in_specs=[hs, hs, hs, pl.BlockSpec(memory_space=pl.ANY)],
        out_specs=_col_spec(tm, 1536, QKVA),
        out_shape=jax.ShapeDtypeStruct(dproj.shape, dproj.dtype),
        input_output_aliases={3: 0},
        compiler_params=_cparams(("parallel",)),
    )(dq, dk, dv, dproj)


def _prep_c(proj, tab, *, name):
    L = proj.shape[0]
    tm = ROW_T

    def body(x_ref, cos_ref, sin_ref, qo, ko, vo):
        cos, sin = cos_ref[...], sin_ref[...]
        q = _rope_lanes(x_ref[:, 0:512], _tile_lanes(cos, 512), _tile_lanes(sin, 512), DH // 2)
        _split(q * SCALE_C, HEADS, DH, qo)
        _split(_rope_lanes(x_ref[:, 512:640], cos, sin, DH // 2), SWA_KV_HEADS, DH, ko)
        _split(x_ref[:, 640:768], SWA_KV_HEADS, DH, vo)

    t128 = pl.BlockSpec((tm, LANES), lambda i: (i, 0))
    return pl.pallas_call(
        body, name=name, grid=(L // tm,),
        in_specs=[_col_spec(tm, 768, QKVC), t128, t128],
        out_specs=[_head_spec(HEADS, tm, DH), _head_spec(SWA_KV_HEADS, tm, DH), _head_spec(SWA_KV_HEADS, tm, DH)],
        out_shape=[jax.ShapeDtypeStruct((HEADS, L, DH), BF16), jax.ShapeDtypeStruct((SWA_KV_HEADS, L, DH), BF16),
                   jax.ShapeDtypeStruct((SWA_KV_HEADS, L, DH), BF16)],
        compiler_params=_cparams(("parallel",)),
    )(proj, *tab)


def _unprep_c(dq, dk, dv, tab, dproj, *, name):
    L = dq.shape[1]
    tm = ROW_T

    def body(dq_ref, dk_ref, dv_ref, cos_ref, sin_ref, _, dp_ref):
        cos, nsin = cos_ref[...], -sin_ref[...]
        dqv = _join(dq_ref, HEADS) * SCALE_C
        dp_ref[:, 0:512] = _rope_lanes(dqv, _tile_lanes(cos, 512), _tile_lanes(nsin, 512), DH // 2).astype(BF16)
        dp_ref[:, 512:640] = _rope_lanes(_join(dk_ref, SWA_KV_HEADS), cos, nsin, DH // 2).astype(BF16)
        dp_ref[:, 640:768] = _join(dv_ref, SWA_KV_HEADS).astype(BF16)

    hs = _head_spec(HEADS, tm, DH)
    hkv = _head_spec(SWA_KV_HEADS, tm, DH)
    t128 = pl.BlockSpec((tm, LANES), lambda i: (i, 0))
    return pl.pallas_call(
        body, name=name, grid=(L // tm,),
        in_specs=[hs, hkv, hkv, t128, t128, pl.BlockSpec(memory_space=pl.ANY)],
        out_specs=_col_spec(tm, 768, QKVC),
        out_shape=jax.ShapeDtypeStruct(dproj.shape, dproj.dtype),
        input_output_aliases={5: 0},
        compiler_params=_cparams(("parallel",)),
    )(dq, dk, dv, *tab, dproj)


def _q_tables(cos, sin, on_ref):
    on = on_ref[...] > 0.5
    width = on_ref.shape[1]
    return jnp.where(on, _tile_lanes(cos, width), 1.0), jnp.where(on, _tile_lanes(sin, width), 0.0)


def _prep_b(qbm, kvbm, proj, tab_k, q_rotary, *, name):
    L = proj.shape[0]
    tm = ROW_T

    def body(q_ref, kv_ref, kr_ref, ck_ref, sk_ref, on_ref, qo, ko, vo):
        cq, sq = _q_tables(ck_ref[...], sk_ref[...], on_ref)
        q = _rope_lanes(q_ref[...], cq, sq, MLA_ROPE // 2) * SCALE_B
        _split(q, HEADS, DQK_B, qo)
        kr = _rope_lanes(kr_ref[...], ck_ref[...], sk_ref[...], MLA_ROPE // 2)[:, :MLA_ROPE].astype(BF16)
        kv = kv_ref[...]
        for h in range(HEADS):
            ko[h] = jnp.concatenate([kv[:, 128 * h:128 * h + MLA_NOPE].astype(BF16), kr], axis=1)
            vo[h] = kv[:, 128 * h + MLA_NOPE:128 * (h + 1)].astype(BF16)

    t128 = pl.BlockSpec((tm, LANES), lambda i: (i, 0))
    t768 = pl.BlockSpec((tm, 768), lambda i: (i, 0))
    return pl.pallas_call(
        body, name=name, grid=(L // tm,),
        in_specs=[t768, pl.BlockSpec((tm, 1024), lambda i: (i, 0)), _col_spec(tm, 128, KR), t128, t128,
                  pl.BlockSpec((1, 768), lambda i: (0, 0))],
        out_specs=[_head_spec(HEADS, tm, DQK_B), _head_spec(HEADS, tm, DQK_B), _head_spec(HEADS, tm, DH)],
        out_shape=[jax.ShapeDtypeStruct((HEADS, L, DQK_B), BF16), jax.ShapeDtypeStruct((HEADS, L, DQK_B), BF16),
                   jax.ShapeDtypeStruct((HEADS, L, DH), BF16)],
        compiler_params=_cparams(("parallel",)),
    )(qbm, kvbm, proj, *tab_k, q_rotary)


def _unprep_b(dq, dk, dv, tab_k, q_rotary, dproj, *, name):
    L = dq.shape[1]
    tm = ROW_T

    def body(dq_ref, dk_ref, dv_ref, ck_ref, sk_ref, on_ref, _, dp_kr, dqo, dkvo):
        cq, sq = _q_tables(ck_ref[...], sk_ref[...], on_ref)
        dqv = _join(dq_ref, HEADS) * SCALE_B
        dqo[...] = _rope_lanes(dqv, cq, -sq, MLA_ROPE // 2).astype(BF16)
        parts = []
        dkr = None
        for h in range(HEADS):
            dkh = dk_ref[h]
            parts += [dkh[:, :MLA_NOPE], dv_ref[h]]
            r = dkh[:, MLA_NOPE:]
            dkr = r if dkr is None else dkr + r
        dkvo[...] = jnp.concatenate(parts, axis=1).astype(BF16)
        dkr = jnp.concatenate([dkr, jnp.zeros((tm, LANES - MLA_ROPE), F32)], axis=1)
        dp_kr[...] = _rope_lanes(dkr, ck_ref[...], -sk_ref[...], MLA_ROPE // 2).astype(BF16)

    t128 = pl.BlockSpec((tm, LANES), lambda i: (i, 0))
    t768 = pl.BlockSpec((tm, 768), lambda i: (i, 0))
    hq = _head_spec(HEADS, tm, DQK_B)
    return pl.pallas_call(
        body, name=name, grid=(L // tm,),
        in_specs=[hq, hq, _head_spec(HEADS, tm, DH), t128, t128, pl.BlockSpec((1, 768), lambda i: (0, 0)),
                  pl.BlockSpec(memory_space=pl.ANY)],
        out_specs=[_col_spec(tm, 128, KR), t768, pl.BlockSpec((tm, 1024), lambda i: (i, 0))],
        out_shape=[jax.ShapeDtypeStruct(dproj.shape, dproj.dtype), jax.ShapeDtypeStruct((L, 768), BF16),
                   jax.ShapeDtypeStruct((L, 1024), BF16)],
        input_output_aliases={6: 0},
        compiler_params=_cparams(("parallel",)),
    )(dq, dk, dv, *tab_k, q_rotary, dproj)


def _gate(y, proj, zcol, *, name):
    L = proj.shape[0]
    tm = ROW_T

    def body(y_ref, z_ref, u_ref):
        z = z_ref[...]
        u_ref[...] = (_join(y_ref, HEADS) * (z * jax.nn.sigmoid(z))).astype(BF16)

    return pl.pallas_call(
        body, name=name, grid=(L // tm,),
        in_specs=[_head_spec(HEADS, tm, DH), _col_spec(tm, 512, zcol)],
        out_specs=pl.BlockSpec((tm, 512), lambda i: (i, 0)),
        out_shape=jax.ShapeDtypeStruct((L, 512), BF16),
        compiler_params=_cparams(("parallel",)),
    )(y, proj)


def _gate_bwd(du, y, proj, zcol, dproj, *, delta_rows=False, name):
    L = proj.shape[0]
    tm = ROW_T
    assert not delta_rows or tm == ATT_T

    def body(du_ref, y_ref, z_ref, _, dz_ref, dy_ref, *dl_ref):
        z = z_ref[...]
        duv = du_ref[...]
        sg = jax.nn.sigmoid(z)
        yv = _join(y_ref, HEADS)
        dyv = duv * (z * sg)
        _split(dyv, HEADS, DH, dy_ref)
        dz_ref[...] = (duv * yv * (sg * (1.0 + z * (1.0 - sg)))).astype(BF16)
        if delta_rows:
            prod = dyv * yv
            ones = jnp.ones((8, DH), F32)
            for h in range(HEADS):
                sums = lax.dot_general(ones, prod[:, DH * h:DH * (h + 1)], _NT, preferred_element_type=F32)
                dl_ref[0][h, 0] = sums[0:1, :]

    hs = _head_spec(HEADS, tm, DH)
    out_specs = [_col_spec(tm, 512, zcol), hs]
    out_shape = [jax.ShapeDtypeStruct(dproj.shape, dproj.dtype),
                 jax.ShapeDtypeStruct((HEADS, L, DH), BF16 if delta_rows else F32)]
    if delta_rows:
        out_specs.append(pl.BlockSpec((HEADS, 1, 1, tm), lambda i: (0, i, 0, 0)))
        out_shape.append(jax.ShapeDtypeStruct((HEADS, L // tm, 1, tm), F32))
    return pl.pallas_call(
        body, name=name, grid=(L // tm,),
        in_specs=[pl.BlockSpec((tm, 512), lambda i: (i, 0)), hs, _col_spec(tm, 512, zcol),
                  pl.BlockSpec(memory_space=pl.ANY)],
        out_specs=out_specs, out_shape=out_shape,
        input_output_aliases={3: 0},
        compiler_params=_cparams(("parallel",)),
    )(du, y, proj, dproj)


MERGE_T = 192


def _merge(proj, pbs, *, name):
    L = proj.shape[0]
    tm = MERGE_T

    def body(g0, g1, g2, p0, p1, p2, o_ref):
        acc = None
        for g_ref, p_ref in ((g0, p0), (g1, p1), (g2, p2)):
            t = jax.nn.sigmoid(g_ref[...]) * p_ref[...]
            acc = t if acc is None else acc + t
        o_ref[...] = acc.astype(BF16)

    row = pl.BlockSpec((tm, D_MODEL), lambda i: (i, 0))
    return pl.pallas_call(
        body, name=name, grid=(L // tm,),
        in_specs=[_col_spec(tm, D_MODEL, GATES + n * D_MODEL) for n in range(N_BRANCH)] + [row] * N_BRANCH,
        out_specs=row, out_shape=jax.ShapeDtypeStruct((L, D_MODEL), BF16),
        compiler_params=_cparams(("parallel",)),
    )(proj, proj, proj, *pbs)


def _merge_bwd(dmerged, proj, pbs, dproj, *, name):
    L = proj.shape[0]
    tm = MERGE_T

    def body(dm_ref, g_ref, p0, p1, p2, _, dg_ref, dp0, dp1, dp2):
        dm = dm_ref[...]
        for n, (p_ref, dp_ref) in enumerate(((p0, dp0), (p1, dp1), (p2, dp2))):
            cols = slice(n * D_MODEL, (n + 1) * D_MODEL)
            sg = jax.nn.sigmoid(g_ref[:, cols])
            dp_ref[...] = (dm * sg).astype(BF16)
            dg_ref[:, cols] = (dm * p_ref[...] * (sg * (1.0 - sg))).astype(BF16)

    row = pl.BlockSpec((tm, D_MODEL), lambda i: (i, 0))
    gates = _col_spec(tm, N_BRANCH * D_MODEL, GATES)
    outs = pl.pallas_call(
        body, name=name, grid=(L // tm,),
        in_specs=[row, gates] + [row] * N_BRANCH + [pl.BlockSpec(memory_space=pl.ANY)],
        out_specs=[gates] + [row] * N_BRANCH,
        out_shape=[jax.ShapeDtypeStruct(dproj.shape, dproj.dtype)]
        + [jax.ShapeDtypeStruct((L, D_MODEL), BF16)] * N_BRANCH,
        input_output_aliases={5: 0},
        compiler_params=_cparams(("parallel",)),
    )(dmerged, proj, *pbs, dproj)
    return outs[0], outs[1:]


def _forget_bias(af, b_f):
    return jnp.cumsum(jax.nn.log_sigmoid(af + b_f), axis=0).T


def _key_bias(L, T, ct=None):
    padb = jnp.where(jnp.arange(L) < PAD, BIG, 0.0).astype(F32)[None]
    kb = padb if ct is None else ct + padb
    return kb.reshape(kb.shape[0], L // T, 1, T)


def _layer_fwd(h, w, tabs, l, side=None):
    tag = f"l{l}"
    L = h.shape[0]
    hn = _rms_fwd(h, w["norm_g"], name=f"{tag}_rms_in")
    if "late" in w:
        late_job, assemble = w["late"]
        proj, *late = _mm(hn, w["w_in"], side=late_job, name=f"{tag}_mm_in")
        w = {**{key: val for key, val in w.items() if key != "late"}, **assemble(*late)}
    else:
        proj = _mm(hn, w["w_in"], name=f"{tag}_mm_in")
    ct, vjp_f = jax.vjp(_forget_bias, proj[:, FA:FA + HEADS], w["b_f"])
    ops_a = (*_prep_a(proj, name=f"{tag}_prep_a"), _key_bias(L, ATT_T, ct))
    ya, lsea, *side_out = _attn_fwd(*ops_a, T=ATT_T, side=side, name=f"{tag}_attn_a")
    cqn = _rms_fwd(proj, w["g_cq"], col=CQ, name=f"{tag}_rms_cq")
    ckvn = _rms_fwd(proj, w["g_ckv"], col=CKV, name=f"{tag}_rms_ckv")
    qbm = _mm(cqn, w["w_uq"], name=f"{tag}_mm_uq")
    kvbm = _mm(ckvn, w["w_ukv"], name=f"{tag}_mm_ukv")
    ops_b = (*_prep_b(qbm, kvbm, proj, tabs["bk"], tabs["bq"], name=f"{tag}_prep_b"), _key_bias(L, ATT_T))
    yb, lseb = _attn_fwd(*ops_b, T=ATT_T, name=f"{tag}_attn_b")
    ops_c = (*_prep_c(proj, tabs["c"], name=f"{tag}_prep_c"), _key_bias(L, WINDOW))
    sink = jnp.broadcast_to(w["sinks"][:, None, None], (HEADS, 1, LANES))
    yc, lsec = _attn_fwd(*ops_c, T=WINDOW, window=True, sink=sink, name=f"{tag}_attn_c")
    us = [_gate(y, proj, zcol, name=f"{tag}_gate{n}") for n, (y, zcol) in enumerate(((ya, ZA), (yb, ZB), (yc, ZC)))]
    pbr = [_mm(us[n], w["w_branch"][n], name=f"{tag}_mm_br{n}") for n in range(N_BRANCH)]
    merged = _merge(proj, pbr, name=f"{tag}_merge")
    out = _mm(merged, w["w_out"], add=h, name=f"{tag}_mm_out")
    saved = dict(h=h, hn=hn, proj=proj, vjp_f=vjp_f, ops_a=ops_a, ya=ya, lsea=lsea, cqn=cqn, ckvn=ckvn,
                 ops_b=ops_b, yb=yb, lseb=lseb, ops_c=ops_c, yc=yc, lsec=lsec, us=us, pbr=pbr, merged=merged)
    return out, saved, side_out, w


def _w_in_chunks(g, tag):
    return _w_in_to_shards(g["w_in"], name=f"{tag}_w_in_chunks")


def _grad_chunks(g, tag, with_w_in=True):
    rest = [_cols_to_shards(g["w_uq"]), _cols_to_shards(g["w_ukv"]),
            jnp.stack([_cols_to_shards(g["w_branch"][n]) for n in range(N_BRANCH)], axis=1),
            g["w_out"].reshape(N_DEV, D_MODEL // N_DEV, D_MODEL)]
    return ([_w_in_chunks(g, tag)] if with_w_in else []) + rest


def _layer_bwd(dout, s, w, tabs, l, side=None, own_exchange=False):
    tag = f"l{l}"
    L = dout.shape[0]
    proj = s["proj"]
    g = {}
    dproj = jnp.zeros((L, NP_IN), BF16)
    dmerged = _mm(dout, w["w_out"], tb=True, name=f"{tag}_mm_out_dx")
    g["w_out"] = _mm(s["merged"], dout, ta=True, out_dtype=BF16, name=f"{tag}_mm_out_dw")
    dproj, dpbr = _merge_bwd(dmerged, proj, s["pbr"], dproj, name=f"{tag}_merge_bwd")
    dus = [_mm(dpbr[n], w["w_branch"][n], tb=True, name=f"{tag}_mm_br{n}_dx") for n in range(N_BRANCH)]
    g["w_branch"] = jnp.stack([_mm(s["us"][n], dpbr[n], ta=True, out_dtype=BF16, name=f"{tag}_mm_br{n}_dw")
                               for n in range(N_BRANCH)])
    dproj, dya, dla = _gate_bwd(dus[0], s["ya"], proj, ZA, dproj, delta_rows=True, name=f"{tag}_gate0_bwd")
    dproj, dyb, dlb = _gate_bwd(dus[1], s["yb"], proj, ZB, dproj, delta_rows=True, name=f"{tag}_gate1_bwd")
    dproj, dyc = _gate_bwd(dus[2], s["yc"], proj, ZC, dproj, name=f"{tag}_gate2_bwd")

    def bwd_operands(ops, lse, delta, dy16):
        q16, k16, v16, kbias = ops
        return (q16, k16, v16, kbias.reshape(kbias.shape[0], L, 1), lse.reshape(HEADS, L // ATT_T, 1, ATT_T), delta,
                dy16)

    dqa, dka, dva, dcq, dck, *side_out = _attn_bwd_t(*bwd_operands(s["ops_a"], s["lsea"], dla, dya), T=ATT_T,
                                                     fox=True, side=side, name=f"{tag}_attn_a_bwd")
    dproj = _unprep_a(dqa, dka, dva, dproj, name=f"{tag}_unprep_a")
    daf, g["b_f"] = s["vjp_f"](dcq.reshape(HEADS, L) + dck[:, :, 0])
    dproj = lax.dynamic_update_slice(dproj, daf.astype(BF16), (0, FA))
    dqb, dkb, dvb, _, _ = _attn_bwd_t(*bwd_operands(s["ops_b"], s["lseb"], dlb, dyb), T=ATT_T,
                                      name=f"{tag}_attn_b_bwd")
    dproj, dqbm, dkvbm = _unprep_b(dqb, dkb, dvb, tabs["bk"], tabs["bq"], dproj, name=f"{tag}_unprep_b")
    dcqn = _mm(dqbm, w["w_uq"], tb=True, name=f"{tag}_mm_uq_dx")
    g["w_uq"] = _mm(s["cqn"], dqbm, ta=True, out_dtype=BF16, name=f"{tag}_mm_uq_dw")
    dckvn = _mm(dkvbm, w["w_ukv"], tb=True, name=f"{tag}_mm_ukv_dx")
    g["w_ukv"] = _mm(s["ckvn"], dkvbm, ta=True, out_dtype=BF16, name=f"{tag}_mm_ukv_dw")
    dproj, g["g_cq"] = _rms_bwd(dcqn, proj, w["g_cq"], col=CQ, into=dproj, name=f"{tag}_rms_cq_bwd")
    dproj, g["g_ckv"] = _rms_bwd(dckvn, proj, w["g_ckv"], col=CKV, into=dproj, name=f"{tag}_rms_ckv_bwd")
    dqc, dkc, dvc = _attn_bwd_window(*s["ops_c"], s["yc"], s["lsec"], dyc, name=f"{tag}_attn_c_bwd")
    dproj = _unprep_c(dqc, dkc, dvc, tabs["c"], dproj, name=f"{tag}_unprep_c")
    delta_c = jnp.sum(dyc * s["yc"], axis=-1)
    g["sinks"] = -jnp.sum(jnp.exp(w["sinks"][:, None] - s["lsec"][:, :, 0]) * delta_c, axis=1)
    own_out = []
    if own_exchange:
        g["w_in"], *r_rest = _mm(s["hn"], dproj, ta=True, out_dtype=BF16, side=_Exchange(_grad_chunks(g, tag, False)),
                                 name=f"{tag}_mm_in_dw")
        dhn, r_in = _mm(dproj, w["w_in"], tb=True, side=_Exchange([_w_in_chunks(g, tag)]), name=f"{tag}_mm_in_dx")
        own_out = [r_in, *r_rest]
    else:
        g["w_in"] = _mm(s["hn"], dproj, ta=True, out_dtype=BF16, name=f"{tag}_mm_in_dw")
        dhn = _mm(dproj, w["w_in"], tb=True, name=f"{tag}_mm_in_dx")
    dh, g["norm_g"] = _rms_bwd(dhn, s["h"], w["norm_g"], add=dout, name=f"{tag}_rms_in_bwd")
    return dh, g, side_out, own_out


def _cols_from_shards(g):
    return jnp.moveaxis(g, 0, 1).reshape(g.shape[1], N_DEV * g.shape[2])


def _cols_to_shards(w):
    R = w.shape[0]
    return jnp.moveaxis(w.reshape(R, N_DEV, w.shape[1] // N_DEV), 1, 0)


def _pad_in(w):
    parts, at = [], 0
    for lo, hi, dst in sorted(_RUNS, key=lambda r: r[2]):
        if dst > at:
            parts.append(jnp.zeros((w.shape[0], dst - at), w.dtype))
        parts.append(w[:, lo:hi])
        at = dst + hi - lo
    parts.append(jnp.zeros((w.shape[0], NP_IN - at), w.dtype))
    return jnp.concatenate(parts, axis=1)


def _unpad_in(w):
    return jnp.concatenate([w[:, dst:dst + hi - lo] for lo, hi, dst in _RUNS], axis=1)


W_IN_SHARD = N_IN // N_DEV


def _w_in_pieces():
    pieces = []
    for lo, hi, dst in _RUNS:
        for d in range(N_DEV):
            a, b = max(lo, d * W_IN_SHARD), min(hi, (d + 1) * W_IN_SHARD)
            if a < b:
                pieces.append((d, a - d * W_IN_SHARD, b - d * W_IN_SHARD, dst + a - lo))
    return pieces


def _w_in_from_shards(g, *, name):
    R = g.shape[1]
    tr = 256
    covered = sorted((dst, dst + b - a) for _, a, b, dst in _w_in_pieces())

    def body(g_ref, o_ref):
        at = 0
        for lo, hi in covered + [(NP_IN, NP_IN)]:
            if lo > at:
                o_ref[:, at:lo] = jnp.zeros((tr, lo - at), BF16)
            at = max(at, hi)
        for d, a, b, dst in _w_in_pieces():
            o_ref[:, dst:dst + b - a] = g_ref[d, :, a:b]

    return pl.pallas_call(
        body, name=name, grid=(R // tr,),
        in_specs=[pl.BlockSpec((N_DEV, tr, W_IN_SHARD), lambda i: (0, i, 0))],
        out_specs=pl.BlockSpec((tr, NP_IN), lambda i: (i, 0)),
        out_shape=jax.ShapeDtypeStruct((R, NP_IN), BF16),
        compiler_params=_cparams(("parallel",)),
    )(g)


def _w_in_to_shards(dw, *, name):
    R = dw.shape[0]
    tr = 256

    def body(w_ref, o_ref):
        for d, a, b, dst in _w_in_pieces():
            o_ref[d, :, a:b] = w_ref[:, dst:dst + b - a]

    return pl.pallas_call(
        body, name=name, grid=(R // tr,),
        in_specs=[pl.BlockSpec((tr, NP_IN), lambda i: (i, 0))],
        out_specs=pl.BlockSpec((N_DEV, tr, W_IN_SHARD), lambda i: (0, i, 0)),
        out_shape=jax.ShapeDtypeStruct((N_DEV, R, W_IN_SHARD), BF16),
        compiler_params=_cparams(("parallel",)),
    )(dw)


_SMALL = (("norm_g", DEPTH * D_MODEL), ("b_f", DEPTH * HEADS), ("g_cq", DEPTH * MLA_QLORA),
          ("g_ckv", DEPTH * MLA_KVLORA), ("sinks", DEPTH * HEADS), ("final_g", D_MODEL), ("loss", 1),
          ("meta", N_META * D_MODEL))
SMALL_ROWS = 168


def _pack_small(d):
    parts = []
    for name, size in _SMALL:
        padded = -(-size // 128) * 128
        v = d[name].reshape(-1).astype(F32) if name in d else jnp.zeros((size,), F32)
        parts.append(jnp.pad(v, (0, padded - size)))
    flat = jnp.concatenate(parts)
    return jnp.pad(flat, (0, SMALL_ROWS * 128 - flat.shape[0])).reshape(SMALL_ROWS, 128)


def _unpack_small(p, shapes):
    flat = p.reshape(-1)
    out, at = {}, 0
    for name, size in _SMALL:
        if name in shapes:
            out[name] = flat[at:at + size].reshape(shapes[name])
        at += -(-size // 128) * 128
    return out


def kernel(x, meta_tokens, norm_g, w_in, b_f, g_cq, g_ckv, w_uq, w_ukv, sinks, w_branch, w_out, final_g, loss_target, m_meta_tokens, m_norm_g, m_w_in, m_b_f, m_g_cq, m_g_ckv, m_w_uq, m_w_ukv, m_sinks, m_w_branch, m_w_out, m_final_g, v_meta_tokens, v_norm_g, v_w_in, v_b_f, v_g_cq, v_g_ckv, v_w_uq, v_w_ukv, v_sinks, v_w_branch, v_w_out, v_final_g):
    S = x.shape[1]
    L = BLK + S
    cx, cy, cc = _me()
    my_idx = 4 * cx + 2 * cy + cc

    def shards(l):
        return [t[l].astype(BF16) for t in (w_in, w_uq, w_ukv, w_branch, w_out)]

    def small_weights(l):
        return dict(norm_g=norm_g[l], b_f=b_f[l], g_cq=g_cq[l], g_ckv=g_ckv[l], sinks=sinks[l])

    def rest_weights(gw_uq, gw_ukv, gw_br, gw_out):
        return dict(w_uq=_cols_from_shards(gw_uq), w_ukv=_cols_from_shards(gw_ukv),
                    w_branch=jnp.stack([_cols_from_shards(gw_br[:, n]) for n in range(N_BRANCH)]),
                    w_out=gw_out.reshape(D_MODEL, D_MODEL))

    def layer_weights(l, gw_in, *gw_rest):
        return dict(small_weights(l), w_in=_w_in_from_shards(gw_in, name=f"l{l}_w_in_full"), **rest_weights(*gw_rest))

    gw_in0, g_meta = _comm(_Gather([shards(0)[0], meta_tokens]), name="gather_l0")
    layers = [dict(small_weights(0), w_in=_w_in_from_shards(gw_in0, name="l0_w_in_full"),
                   late=(_Gather(shards(0)[1:]), rest_weights)), None]
    meta_full = _cols_from_shards(g_meta)

    h = jnp.concatenate([jnp.zeros((PAD, D_MODEL), F32), meta_full, x[0]], axis=0)
    q_rotary = ((jnp.arange(HEADS * DQK_B) % DQK_B) >= MLA_NOPE).astype(F32)[None, :]
    tabs = dict(c=_rope_tables(L, DH // 2, LANES), bk=_rope_tables(L, MLA_ROPE // 2, LANES), bq=q_rotary)
    saved = [None] * DEPTH
    h, saved[0], gw1, layers[0] = _layer_fwd(h, layers[0], tabs, 0, side=_Gather(shards(1)))
    layers[1] = layer_weights(1, *gw1)
    h, saved[1], _, _ = _layer_fwd(h, layers[1], tabs, 1)
    loss_vec, dh, g_final = _loss_head(h, final_g, loss_target[0], name="loss_head")

    grads = [None] * DEPTH
    dh, grads[1], _, _ = _layer_bwd(dh, saved[1], layers[1], tabs, 1)
    dh, grads[0], recv1, recv0 = _layer_bwd(dh, saved[0], layers[0], tabs, 0, side=_Exchange(_grad_chunks(grads[1], "l1")),
                                            own_exchange=True)
    r_in, r_uq, r_ukv, r_br, r_out = (jnp.stack([a, b], axis=1) for a, b in zip(recv0, recv1))

    def stack(name):
        return jnp.stack([grads[l][name] for l in range(DEPTH)])

    small = _pack_small(dict(norm_g=stack("norm_g"), b_f=stack("b_f"), g_cq=stack("g_cq"), g_ckv=stack("g_ckv"),
                             sinks=stack("sinks"), final_g=g_final, loss=loss_vec[0, 0:1],
                             meta=dh[PAD:BLK]))
    (g_small,) = _comm(_Gather([small]), name="gather_small")

    def adam_big(w_, parts, m_, v_, name):
        shape = w_.shape
        C = shape[-1]
        R = math.prod(shape[:-1])
        outs = _adamw(w_.reshape(R, C), parts.reshape(parts.shape[0], R, C), m_.reshape(R, C), v_.reshape(R, C),
                      name=name)
        return [o.reshape(shape) for o in outs]

    res = {}
    res["w_in"] = adam_big(w_in, r_in, m_w_in, v_w_in, "adam_w_in")
    res["w_uq"] = adam_big(w_uq, r_uq, m_w_uq, v_w_uq, "adam_w_uq")
    res["w_ukv"] = adam_big(w_ukv, r_ukv, m_w_ukv, v_w_ukv, "adam_w_ukv")
    res["w_branch"] = adam_big(w_branch, r_br, m_w_branch, v_w_branch, "adam_w_branch")
    res["w_out"] = adam_big(w_out, r_out, m_w_out, v_w_out, "adam_w_out")

    small_w = dict(norm_g=norm_g, b_f=b_f, g_cq=g_cq, g_ckv=g_ckv, sinks=sinks, final_g=final_g)
    small_m = dict(norm_g=m_norm_g, b_f=m_b_f, g_cq=m_g_cq, g_ckv=m_g_ckv, sinks=m_sinks, final_g=m_final_g)
    small_v = dict(norm_g=v_norm_g, b_f=v_b_f, g_cq=v_g_cq, g_ckv=v_g_ckv, sinks=v_sinks, final_g=v_final_g)
    sm = _adamw(_pack_small(small_w), g_small, _pack_small(small_m), _pack_small(small_v), name="adam_small")
    shapes = {k: a.shape for k, a in small_w.items()}
    shapes_all = dict(shapes, loss=(), meta=(N_META, D_MODEL))
    sm_g = _unpack_small(sm[0], shapes_all)
    sm_d, sm_m, sm_v = (_unpack_small(t, shapes) for t in sm[1:])
    for k in shapes:
        res[k] = [sm_g[k], sm_d[k], sm_m[k], sm_v[k]]
    g_meta_mine = lax.dynamic_slice(sm_g["meta"], (0, my_idx * 128), (N_META, 128))
    res["meta_tokens"] = _adamw(meta_tokens, g_meta_mine[None], m_meta_tokens, v_meta_tokens, name="adam_meta")

    order = ["meta_tokens", "norm_g", "w_in", "b_f", "g_cq", "g_ckv", "w_uq", "w_ukv", "sinks", "w_branch", "w_out",
             "final_g"]
    grad_x = dh[BLK:][None]
    return (sm_g["loss"], grad_x, *[res[k][0] for k in order], *[res[k][1] for k in order],
            *[res[k][2] for k in order], *[res[k][3] for k in order])
```

```python
import functools
import math

import jax
import jax.numpy as jnp
from jax import lax
from jax.experimental import pallas as pl
from jax.experimental.pallas import tpu as pltpu

F32 = jnp.float32
BF16 = jnp.bfloat16

D_MODEL = 1024
DEPTH = 2
N_META = 16
BLK = 128
PAD = BLK - N_META
ROPE_THETA = 10000.0
EPS = 1e-6
NEG = -1e30
BIG = 1e30
HEADS = 8
DH = 64
MLA_NOPE = 64
MLA_ROPE = 32
MLA_QLORA = 384
MLA_KVLORA = 256
SWA_KV_HEADS = 2
WINDOW = 128
BRANCH_W = 512
N_BRANCH = 3
N_IN = 7592

ADAM_LR = 0.001
ADAM_B1 = 0.9
ADAM_B2 = 0.999
ADAM_EPS = 1e-08
ADAM_WD = 0.01
ADAM_STEP = 10

N_DEV = 8
MESH = pl.DeviceIdType.MESH

NP_IN = 8192
QKVA, ZA, FA = 0, 1536, 2048
CKV, KR, CQ = 2304, 2560, 2688
GATES = 3072
ZB = 6144
QKVC, ZC = 6912, 7680
_RUNS = ((0, 1536, QKVA), (1536, 1544, FA), (1544, 2056, ZA), (2056, 2440, CQ), (2440, 2696, CKV), (2696, 2728, KR),
         (2728, 3240, ZB), (3240, 4008, QKVC), (4008, 4520, ZC), (4520, 7592, GATES))

VMEM_LIMIT = 48 * 1024 * 1024
ATT_T = 384
ATT_HP = 1
ROW_T = 384
LANES = 128


def _pick(dim, prefs):
    for p in prefs:
        if dim % p == 0:
            return p
    return dim


def _cparams(sem):
    return pltpu.CompilerParams(dimension_semantics=sem, vmem_limit_bytes=VMEM_LIMIT)


def _side_cparams(side, sem):
    if side is None:
        return _cparams(sem)
    return pltpu.CompilerParams(dimension_semantics=("arbitrary",) * len(sem), vmem_limit_bytes=VMEM_LIMIT,
                                has_side_effects=True)


def _mm(a, b, *, ta=False, tb=False, add=None, out_dtype=F32, side=None, name):
    M = a.shape[1] if ta else a.shape[0]
    K = a.shape[0] if ta else a.shape[1]
    N = b.shape[0] if tb else b.shape[1]
    assert K == (b.shape[1] if tb else b.shape[0])
    tm = _pick(M, (704, 1024, 512, 384, 256, 128))
    tn = _pick(N, (1024, 768, 512, 384, 256, 128))
    tk = _pick(K, (2048, 1408, 1024, 768, 512, 384, 256, 128))
    nk = K // tk
    dims = (((0 if ta else 1,), (1 if tb else 0,)), ((), ()))

    sn = 0 if side is None else side.n
    n_in = 2 + (add is not None)
    grid = (M // tm, N // tn, nk)

    def body(*refs):
        a_ref, b_ref = refs[:2]
        c_ref = refs[2] if add is not None else None
        o_ref = refs[n_in + sn]
        scratch = refs[n_in + 2 * sn + 1:]
        if side is not None:
            step = (pl.program_id(0) * grid[1] + pl.program_id(1)) * nk + pl.program_id(2)
            _ride(side, [*refs[n_in:n_in + sn], *refs[n_in + sn + 1:n_in + 2 * sn + 1], *scratch[nk > 1:]],
                  step, grid[0] * grid[1] * nk)
        r = lax.dot_general(a_ref[...].astype(BF16), b_ref[...].astype(BF16), dims, preferred_element_type=F32)

        def finish(total):
            if c_ref is not None:
                total = total + c_ref[...]
            o_ref[...] = total.astype(out_dtype)

        if nk == 1:
            finish(r)
        else:
            acc = scratch[0]
            k = pl.program_id(2)

            @pl.when(k == 0)
            def _():
                acc[...] = r

            @pl.when(k > 0)
            def _():
                acc[...] += r

            @pl.when(k == nk - 1)
            def _():
                finish(acc[...])

    a_spec = pl.BlockSpec((tk, tm), lambda i, j, k: (k, i)) if ta else pl.BlockSpec((tm, tk), lambda i, j, k: (i, k))
    b_spec = pl.BlockSpec((tn, tk), lambda i, j, k: (j, k)) if tb else pl.BlockSpec((tk, tn), lambda i, j, k: (k, j))
    o_spec = pl.BlockSpec((tm, tn), lambda i, j, k: (i, j))
    any_spec = pl.BlockSpec(memory_space=pl.ANY)
    outs = pl.pallas_call(
        body, name=name,
        grid=grid,
        in_specs=[a_spec, b_spec] + ([o_spec] if add is not None else []) + [any_spec] * sn,
        out_specs=[o_spec] + [any_spec] * sn,
        out_shape=[jax.ShapeDtypeStruct((M, N), out_dtype)] + ([] if side is None else side.out_shape),
        scratch_shapes=([pltpu.VMEM((tm, tn), F32)] if nk > 1 else []) + ([] if side is None else side.scratch),
        compiler_params=_side_cparams(side, ("parallel", "parallel", "arbitrary")),
    )(*((a, b) if add is None else (a, b, add)), *([] if side is None else side.arrs))
    return outs[0] if side is None else outs


def _col_spec(tm, width, col):
    assert col % width == 0
    return pl.BlockSpec((tm, width), lambda i, _c=col // width: (i, _c))


def _rms_fwd(x, g, *, col=0, name):
    L = x.shape[0]
    D = g.shape[0]
    tm = ROW_T

    def body(x_ref, g_ref, y_ref):
        xv = x_ref[...]
        rstd = lax.rsqrt(jnp.mean(xv * xv, axis=-1, keepdims=True) + EPS)
        y_ref[...] = (xv * rstd * g_ref[...]).astype(BF16)

    return pl.pallas_call(
        body, name=name, grid=(L // tm,),
        in_specs=[_col_spec(tm, D, col), pl.BlockSpec((1, D), lambda i: (0, 0))],
        out_specs=pl.BlockSpec((tm, D), lambda i: (i, 0)),
        out_shape=jax.ShapeDtypeStruct((L, D), BF16),
        compiler_params=_cparams(("parallel",)),
    )(x, g.reshape(1, D))


def _rms_bwd(dy, x, g, *, col=0, add=None, into=None, name):
    L = x.shape[0]
    D = g.shape[0]
    tm = ROW_T

    def body(*refs):
        dy_ref, x_ref, g_ref = refs[:3]
        add_ref = refs[3] if add is not None else None
        dx_ref, dg_ref = refs[-2:]
        i = pl.program_id(0)
        xv = x_ref[...]
        dyv = dy_ref[...]
        rstd = lax.rsqrt(jnp.mean(xv * xv, axis=-1, keepdims=True) + EPS)
        xhat = xv * rstd
        part = jnp.sum(dyv * xhat, axis=0, keepdims=True)

        @pl.when(i == 0)
        def _():
            dg_ref[...] = part

        @pl.when(i > 0)
        def _():
            dg_ref[...] += part

        dxh = dyv * g_ref[...]
        dx = rstd * (dxh - xhat * jnp.mean(dxh * xhat, axis=-1, keepdims=True))
        if add_ref is not None:
            dx = dx + add_ref[...]
        dx_ref[...] = dx.astype(dx_ref.dtype)

    row = pl.BlockSpec((tm, D), lambda i: (i, 0))
    in_specs = [row, _col_spec(tm, D, col), pl.BlockSpec((1, D), lambda i: (0, 0))]
    args = [dy, x, g.reshape(1, D)]
    aliases = {}
    if add is not None:
        in_specs.append(row)
        args.append(add)
    if into is not None:
        in_specs.append(pl.BlockSpec(memory_space=pl.ANY))
        args.append(into)
        aliases = {len(args) - 1: 0}
        dx_spec, dx_shape = _col_spec(tm, D, col), jax.ShapeDtypeStruct(into.shape, into.dtype)
    else:
        dx_spec, dx_shape = row, jax.ShapeDtypeStruct((L, D), F32)
    dx, dg = pl.pallas_call(
        body, name=name, grid=(L // tm,),
        in_specs=in_specs,
        out_specs=[dx_spec, pl.BlockSpec((1, D), lambda i: (0, 0))],
        out_shape=[dx_shape, jax.ShapeDtypeStruct((1, D), F32)],
        input_output_aliases=aliases,
        compiler_params=_cparams(("arbitrary",)),
    )(*args)
    return dx, dg.reshape(D)


def _loss_head(h, g, target, *, name):
    L, D = h.shape
    nb = L // BLK

    def body(h_ref, g_ref, t_ref, loss_ref, dh_ref, dg_ref):
        i = pl.program_id(0)

        @pl.when(i == 0)
        def _():
            loss_ref[...] = jnp.zeros_like(loss_ref)
            dg_ref[...] = jnp.zeros_like(dg_ref)
            dh_ref[...] = jnp.zeros_like(dh_ref)

        @pl.when(i > 0)
        def _():
            xv = h_ref[...]
            gv = g_ref[...]
            rstd = lax.rsqrt(jnp.mean(xv * xv, axis=-1, keepdims=True) + EPS)
            xhat = xv * rstd
            err = xhat * gv - t_ref[...]
            row = jnp.mean(err * err, axis=-1, keepdims=True)
            loss_ref[...] += 0.5 * jnp.sum(row, axis=0, keepdims=True)
            dy = err * (1.0 / D)
            dg_ref[...] += jnp.sum(dy * xhat, axis=0, keepdims=True)
            dxh = dy * gv
            dh_ref[...] = rstd * (dxh - xhat * jnp.mean(dxh * xhat, axis=-1, keepdims=True))

    loss, dh, dg = pl.pallas_call(
        body, name=name, grid=(nb,),
        in_specs=[pl.BlockSpec((BLK, D), lambda i: (i, 0)), pl.BlockSpec((1, D), lambda i: (0, 0)),
                  pl.BlockSpec((BLK, D), lambda i: (jnp.maximum(i - 1, 0), 0))],
        out_specs=[pl.BlockSpec((1, 128), lambda i: (0, 0)), pl.BlockSpec((BLK, D), lambda i: (i, 0)),
                   pl.BlockSpec((1, D), lambda i: (0, 0))],
        out_shape=[jax.ShapeDtypeStruct((1, 128), F32), jax.ShapeDtypeStruct((L, D), F32),
                   jax.ShapeDtypeStruct((1, D), F32)],
        compiler_params=_cparams(("arbitrary",)),
    )(h, g.reshape(1, D), target)
    return loss, dh, dg.reshape(D)


_NT = (((1,), (1,)), ((), ()))
_TN = (((0,), (0,)), ((), ()))


def _attn_fwd(q, k, v, kbias, *, T, window=False, sink=None, side=None, name):
    sn = 0 if side is None else side.n
    H, L, dk = q.shape
    Hkv = k.shape[0]
    dv = v.shape[2]
    G = H // Hkv
    nt = L // T
    Hb = kbias.shape[0]
    reps = T // LANES
    assert not window or T == WINDOW
    HP = G if G > 1 else ATT_HP
    NS = 1 if G > 1 else HP
    R = HP * T // NS
    HKV = HP // G
    HB = HP if Hb > 1 else 1
    assert G == 1 or Hb == 1

    def body(*refs):
        q_ref, k_ref, v_ref, kb_ref = refs[:4]
        n = 4
        if sink is not None:
            sk_ref = refs[n]
            n += 1
        side_in = refs[n:n + sn]
        n += sn
        o_ref, lse_ref = refs[n:n + 2]
        side_out = refs[n + 2:n + 2 + sn]
        n += 2 + sn
        m_scs, l_scs, acc_scs, buf_a, buf_b = (refs[n + t * NS:n + (t + 1) * NS] for t in range(5))
        i = pl.program_id(1)
        if side is not None:
            _ride(side, [*side_in, *side_out, *refs[n + 5 * NS:]], pl.program_id(0) * nt + i, (H // HP) * nt)
        for a in range(NS):
            if sink is not None:
                sk = [jnp.broadcast_to(sk_ref[b, :, 0:1], (T, LANES)) for b in range(HP)]
                m_scs[a][...] = jnp.concatenate(sk, axis=0) if G > 1 else sk[a]
                l_scs[a][...] = jnp.ones((R, LANES), F32)
            else:
                m_scs[a][...] = jnp.full((R, LANES), NEG, F32)
                l_scs[a][...] = jnp.zeros((R, LANES), F32)
            acc_scs[a][...] = jnp.zeros((R, dv), F32)
        row = lax.broadcasted_iota(jnp.int32, (R, T), 0) & (T - 1) if G > 1 else \
            lax.broadcasted_iota(jnp.int32, (R, T), 0)
        col = lax.broadcasted_iota(jnp.int32, (R, T), 1)

        def logits(a, j):
            rows = pl.ds(pl.multiple_of(j * T, T), T)
            qv = q_ref[...].reshape(R, dk) if G > 1 else q_ref[a]
            s = lax.dot_general(qv, k_ref[a, rows, :], _NT, preferred_element_type=F32)
            return s - kb_ref[a if HB > 1 else 0, j]

        def update(a, s, j, kind):
            rows = pl.ds(pl.multiple_of(j * T, T), T)
            m_sc, l_sc, acc_sc = m_scs[a], l_scs[a], acc_scs[a]
            if kind == "diag":
                s = jnp.where(row >= col, s, NEG)
            elif kind == "prev":
                s = jnp.where(col > row, s, NEG)
            m_prev = m_sc[...]
            m_new = jnp.maximum(m_prev, jnp.max(s, axis=1, keepdims=True))
            alpha = jnp.exp(m_prev - m_new)
            p = jnp.exp(s - jnp.tile(m_new, (1, reps)))
            l_sc[...] = alpha * l_sc[...] + jnp.sum(p, axis=1, keepdims=True)
            acc_sc[...] = alpha[:, :dv] * acc_sc[...] + jnp.dot(p.astype(BF16), v_ref[a, rows, :],
                                                                preferred_element_type=F32)
            m_sc[...] = m_new

        if window:
            @pl.when(i > 0)
            def _():
                for a in range(NS):
                    update(a, logits(a, i - 1), i - 1, "prev")

            for a in range(NS):
                update(a, logits(a, i), i, "diag")
        else:
            def fill(buf, j):
                for a in range(NS):
                    buf[a][...] = logits(a, j)

            def drain(buf, j, kind):
                for a in range(NS):
                    update(a, buf[a][...], j, kind)

            fill(buf_a, 0)

            def pair(t, c):
                fill(buf_b, 2 * t + 1)
                drain(buf_a, 2 * t, "full")
                fill(buf_a, 2 * t + 2)
                drain(buf_b, 2 * t + 1, "full")
                return c

            lax.fori_loop(0, i // 2, pair, 0)

            @pl.when(i % 2 == 1)
            def _():
                fill(buf_b, i)
                drain(buf_a, i - 1, "full")
                drain(buf_b, i, "diag")

            @pl.when(i % 2 == 0)
            def _():
                drain(buf_a, i, "diag")

        for a in range(NS):
            lv = l_scs[a][...]
            ov = acc_scs[a][...] / lv[:, :dv]
            lsev = (m_scs[a][...] + jnp.log(lv))[:, 0:1]
            if G > 1:
                o_ref[...] = ov.reshape(HP, T, dv)
                lse_ref[...] = lsev.reshape(HP, T, 1)
            else:
                o_ref[a] = ov
                lse_ref[a] = lsev

    in_specs = [pl.BlockSpec((HP, T, dk), lambda h, i: (h, i, 0)),
                pl.BlockSpec((HKV, L, dk), lambda h, i: (h, 0, 0)),
                pl.BlockSpec((HKV, L, dv), lambda h, i: (h, 0, 0)),
                pl.BlockSpec((HB, nt, 1, T), lambda h, i: (h if Hb > 1 else 0, 0, 0, 0))]
    args = [q, k, v, kbias]
    if sink is not None:
        in_specs += [pl.BlockSpec((HP, 1, LANES), lambda h, i: (h, 0, 0))]
        args += [sink]
    any_spec = pl.BlockSpec(memory_space=pl.ANY)
    return pl.pallas_call(
        body, name=name, grid=(H // HP, nt),
        in_specs=in_specs + [any_spec] * sn,
        out_specs=[pl.BlockSpec((HP, T, dv), lambda h, i: (h, i, 0)),
                   pl.BlockSpec((HP, T, 1), lambda h, i: (h, i, 0))] + [any_spec] * sn,
        out_shape=[jax.ShapeDtypeStruct((H, L, dv), F32), jax.ShapeDtypeStruct((H, L, 1), F32)]
        + ([] if side is None else side.out_shape),
        scratch_shapes=[pltpu.VMEM((R, LANES), F32)] * (2 * NS) + [pltpu.VMEM((R, dv), F32)] * NS
        + [pltpu.VMEM((R, T), F32)] * (2 * NS) + ([] if side is None else side.scratch),
        compiler_params=_side_cparams(side, ("parallel", "arbitrary")),
    )(*args, *([] if side is None else side.arrs))


def _attn_bwd(q, k, v, kbias, o, lse, do, *, T, fox=False, side=None, name):
    sn = 0 if side is None else side.n
    H, L, dk = q.shape
    Hkv = k.shape[0]
    dv = v.shape[2]
    G = H // Hkv
    nt = L // T
    Hb = kbias.shape[0]

    def body(*refs):
        q_ref, k_ref, v_ref, kb_ref, o_ref, lse_ref, do_ref = refs[:7]
        side_in = refs[7:7 + sn]
        n = 7 + sn
        dq_ref, dk_ref, dv_ref = refs[n:n + 3]
        n += 3
        if fox:
            dcq_ref, dck_ref = refs[n:n + 2]
            n += 2
        side_out = refs[n:n + sn]
        n += sn
        delta_sc = refs[n]
        buf_a, buf_b = refs[n + 1:n + 3], refs[n + 3:n + 5]
        j = pl.program_id(1)
        if side is not None:
            _ride(side, [*side_in, *side_out, *refs[n + 5:]], pl.program_id(0) * nt + j, H * nt)

        @pl.when(j == 0)
        def _():
            dq_ref[...] = jnp.zeros_like(dq_ref)
            if fox:
                dcq_ref[...] = jnp.zeros_like(dcq_ref)

            def dl(i, c):
                rows = pl.ds(pl.multiple_of(i * T, T), T)
                delta_sc[rows, :] = jnp.sum(do_ref[0, rows, :] * o_ref[0, rows, :], axis=1, keepdims=True)
                return c

            lax.fori_loop(0, nt, dl, 0)

        dk_ref[...] = jnp.zeros_like(dk_ref)
        dv_ref[...] = jnp.zeros_like(dv_ref)
        if fox:
            dck_ref[...] = jnp.zeros_like(dck_ref)
        kb = k_ref[0]
        vb = v_ref[0]
        kbias_j = kb_ref[0, j]
        row = lax.broadcasted_iota(jnp.int32, (T, T), 0)
        col = lax.broadcasted_iota(jnp.int32, (T, T), 1)

        def fill(buf, i):
            rows = pl.ds(pl.multiple_of(i * T, T), T)
            buf[0][...] = lax.dot_general(q_ref[0, rows, :], kb, _NT, preferred_element_type=F32) - kbias_j
            buf[1][...] = lax.dot_general(do_ref[0, rows, :].astype(BF16), vb, _NT, preferred_element_type=F32)

        def drain(buf, i, kind):
            rows = pl.ds(pl.multiple_of(i * T, T), T)
            qb = q_ref[0, rows, :]
            dob = do_ref[0, rows, :].astype(BF16)
            s = buf[0][...]
            if kind == "diag":
                s = jnp.where(row >= col, s, NEG)
            p = jnp.exp(s - lse_ref[0, rows, :])
            ds = p * (buf[1][...] - delta_sc[rows, :])
            dsb = ds.astype(BF16)
            dv_ref[0] += lax.dot_general(p.astype(BF16), dob, _TN, preferred_element_type=F32)
            dk_ref[0] += lax.dot_general(dsb, qb, _TN, preferred_element_type=F32)
            dq_ref[0, rows, :] += jnp.dot(dsb, kb, preferred_element_type=F32)
            if fox:
                dcq_ref[0, rows, :] += jnp.sum(ds, axis=1, keepdims=True)
                dck_ref[0, 0] += -jnp.sum(ds, axis=0, keepdims=True)

        fill(buf_a, j)
        drain(buf_a, j, "diag")
        first = j + 1
        rest = nt - first

        @pl.when(rest > 0)
        def _():
            fill(buf_a, first)

        def pair(t, c):
            i0 = first + 2 * t
            fill(buf_b, i0 + 1)
            drain(buf_a, i0, "full")
            fill(buf_a, jnp.minimum(i0 + 2, nt - 1))
            drain(buf_b, i0 + 1, "full")
            return c

        lax.fori_loop(0, rest // 2, pair, 0)

        @pl.when(rest % 2 == 1)
        def _():
            drain(buf_a, nt - 1, "full")

    in_specs = [pl.BlockSpec((1, L, dk), lambda h, j: (h, 0, 0)),
                pl.BlockSpec((1, T, dk), lambda h, j: (h // G, j, 0)),
                pl.BlockSpec((1, T, dv), lambda h, j: (h // G, j, 0)),
                pl.BlockSpec((1, nt, 1, T), lambda h, j: (h if Hb > 1 else 0, 0, 0, 0)),
                pl.BlockSpec((1, L, dv), lambda h, j: (h, 0, 0)),
                pl.BlockSpec((1, L, 1), lambda h, j: (h, 0, 0)),
                pl.BlockSpec((1, L, dv), lambda h, j: (h, 0, 0))]
    out_specs = [pl.BlockSpec((1, L, dk), lambda h, j: (h, 0, 0)),
                 pl.BlockSpec((1, T, dk), lambda h, j: (h, j, 0)),
                 pl.BlockSpec((1, T, dv), lambda h, j: (h, j, 0))]
    out_shape = [jax.ShapeDtypeStruct((H, L, dk), F32), jax.ShapeDtypeStruct((H, L, dk), F32),
                 jax.ShapeDtypeStruct((H, L, dv), F32)]
    if fox:
        out_specs += [pl.BlockSpec((1, L, 1), lambda h, j: (h, 0, 0)),
                      pl.BlockSpec((1, 1, 1, T), lambda h, j: (h, j, 0, 0))]
        out_shape += [jax.ShapeDtypeStruct((H, L, 1), F32), jax.ShapeDtypeStruct((H, nt, 1, T), F32)]
    any_spec = pl.BlockSpec(memory_space=pl.ANY)
    outs = pl.pallas_call(
        body, name=name, grid=(H, nt),
        in_specs=in_specs + [any_spec] * sn, out_specs=out_specs + [any_spec] * sn,
        out_shape=out_shape + ([] if side is None else side.out_shape),
        scratch_shapes=[pltpu.VMEM((L, 1), F32)] + [pltpu.VMEM((T, T), F32)] * 4
        + ([] if side is None else side.scratch),
        compiler_params=_side_cparams(side, ("parallel", "arbitrary")),
    )(q, k, v, kbias, o, lse, do, *([] if side is None else side.arrs))
    main, rest = outs[:len(outs) - sn], outs[len(outs) - sn:]
    return (*(main if fox else (*main, None, None)), *rest)


def _attn_bwd_t(q, k, v, kbias_col, lse_row, delta_row, do, *, T, fox=False, side=None, name):
    sn = 0 if side is None else side.n
    H, L, dk = q.shape
    dv = v.shape[2]
    nt = L // T
    Hb = kbias_col.shape[0]

    def body(*refs):
        q_ref, k_ref, v_ref, kb_ref, lse_ref, dl_ref, do_ref = refs[:7]
        side_in = refs[7:7 + sn]
        n = 7 + sn
        dq_ref, dk_ref, dv_ref = refs[n:n + 3]
        n += 3
        if fox:
            dcq_ref, dck_ref = refs[n:n + 2]
            n += 2
        side_out = refs[n:n + sn]
        n += sn
        buf_a, buf_b = refs[n:n + 2], refs[n + 2:n + 4]
        j = pl.program_id(1)
        if side is not None:
            _ride(side, [*side_in, *side_out, *refs[n + 4:]], pl.program_id(0) * nt + j, H * nt)

        @pl.when(j == 0)
        def _():
            dq_ref[...] = jnp.zeros_like(dq_ref)
            if fox:
                dcq_ref[...] = jnp.zeros_like(dcq_ref)

        dk_ref[...] = jnp.zeros_like(dk_ref)
        dv_ref[...] = jnp.zeros_like(dv_ref)
        if fox:
            dck_ref[...] = jnp.zeros_like(dck_ref)
        kb = k_ref[0]
        vb = v_ref[0]
        kbias_j = kb_ref[0]
        key = lax.broadcasted_iota(jnp.int32, (T, T), 0)
        qry = lax.broadcasted_iota(jnp.int32, (T, T), 1)

        def fill(buf, i):
            rows = pl.ds(pl.multiple_of(i * T, T), T)
            buf[0][...] = lax.dot_general(kb, q_ref[0, rows, :], _NT, preferred_element_type=F32) - kbias_j
            buf[1][...] = lax.dot_general(vb, do_ref[0, rows, :], _NT, preferred_element_type=F32)

        def drain(buf, i, kind):
            rows = pl.ds(pl.multiple_of(i * T, T), T)
            st = buf[0][...]
            if kind == "diag":
                st = jnp.where(key <= qry, st, NEG)
            pt = jnp.exp(st - lse_ref[0, i])
            dst = pt * (buf[1][...] - dl_ref[0, i])
            dsb = dst.astype(BF16)
            dv_ref[0] += jnp.dot(pt.astype(BF16), do_ref[0, rows, :], preferred_element_type=F32)
            dk_ref[0] += jnp.dot(dsb, q_ref[0, rows, :], preferred_element_type=F32)
            dq_ref[0, rows, :] += lax.dot_general(dsb, kb, _TN, preferred_element_type=F32)
            if fox:
                dcq_ref[0, i] += jnp.sum(dst, axis=0, keepdims=True)
                dck_ref[0] += -jnp.sum(dst, axis=1, keepdims=True)

        fill(buf_a, j)
        drain(buf_a, j, "diag")
        first = j + 1
        rest = nt - first

        @pl.when(rest > 0)
        def _():
            fill(buf_a, first)

        def pair(t, c):
            i0 = first + 2 * t
            fill(buf_b, i0 + 1)
            drain(buf_a, i0, "full")
            fill(buf_a, jnp.minimum(i0 + 2, nt - 1))
            drain(buf_b, i0 + 1, "full")
            return c

        lax.fori_loop(0, rest // 2, pair, 0)

        @pl.when(rest % 2 == 1)
        def _():
            drain(buf_a, nt - 1, "full")

    rows_spec = pl.BlockSpec((1, nt, 1, T), lambda h, j: (h, 0, 0, 0))
    in_specs = [pl.BlockSpec((1, L, dk), lambda h, j: (h, 0, 0)),
                pl.BlockSpec((1, T, dk), lambda h, j: (h, j, 0)),
                pl.BlockSpec((1, T, dv), lambda h, j: (h, j, 0)),
                pl.BlockSpec((1, T, 1), lambda h, j: (h if Hb > 1 else 0, j, 0)),
                rows_spec, rows_spec,
                pl.BlockSpec((1, L, dv), lambda h, j: (h, 0, 0))]
    out_specs = [pl.BlockSpec((1, L, dk), lambda h, j: (h, 0, 0)),
                 pl.BlockSpec((1, T, dk), lambda h, j: (h, j, 0)),
                 pl.BlockSpec((1, T, dv), lambda h, j: (h, j, 0))]
    out_shape = [jax.ShapeDtypeStruct((H, L, dk), F32), jax.ShapeDtypeStruct((H, L, dk), F32),
                 jax.ShapeDtypeStruct((H, L, dv), F32)]
    if fox:
        out_specs += [rows_spec, pl.BlockSpec((1, T, 1), lambda h, j: (h, j, 0))]
        out_shape += [jax.ShapeDtypeStruct((H, nt, 1, T), F32), jax.ShapeDtypeStruct((H, L, 1), F32)]
    any_spec = pl.BlockSpec(memory_space=pl.ANY)
    outs = pl.pallas_call(
        body, name=name, grid=(H, nt),
        in_specs=in_specs + [any_spec] * sn, out_specs=out_specs + [any_spec] * sn,
        out_shape=out_shape + ([] if side is None else side.out_shape),
        scratch_shapes=[pltpu.VMEM((T, T), F32)] * 4 + ([] if side is None else side.scratch),
        compiler_params=_side_cparams(side, ("parallel", "arbitrary")),
    )(q, k, v, kbias_col, lse_row, delta_row, do, *([] if side is None else side.arrs))
    main, rest = outs[:len(outs) - sn], outs[len(outs) - sn:]
    return (*(main if fox else (*main, None, None)), *rest)


def _attn_bwd_window(q, k, v, kbias, o, lse, do, *, name):
    H, L, dk = q.shape
    Hkv = k.shape[0]
    dv = v.shape[2]
    G = H // Hkv
    T = WINDOW
    nt = L // T
    R = G * T

    def body(q_ref, kc_ref, kp_ref, vc_ref, vp_ref, kb_ref, o_ref, lse_ref, do_ref, dq_ref, dk_ref, dv_ref,
             dk_sc, dv_sc):
        i = pl.program_id(1)
        row = lax.broadcasted_iota(jnp.int32, (R, T), 0) & (T - 1)
        col = lax.broadcasted_iota(jnp.int32, (R, T), 1)

        @pl.when(i == 0)
        def _():
            dk_sc[...] = jnp.zeros_like(dk_sc)
            dv_sc[...] = jnp.zeros_like(dv_sc)

        @pl.when(i == nt)
        def _():
            dk_ref[0] = dk_sc[...]
            dv_ref[0] = dv_sc[...]

        @pl.when(i < nt)
        def _():
            qb = q_ref[...].reshape(R, dk)
            dof = do_ref[...].reshape(R, dv)
            dob = dof.astype(BF16)
            lse_c = lse_ref[...].reshape(R, 1)
            delta = jnp.sum(dof * o_ref[...].reshape(R, dv), axis=1, keepdims=True)

            def grads(kt, vt, kbias_j, mask):
                s = lax.dot_general(qb, kt, _NT, preferred_element_type=F32) - kbias_j
                p = jnp.exp(jnp.where(mask, s, NEG) - lse_c)
                dp = lax.dot_general(dob, vt, _NT, preferred_element_type=F32)
                ds = (p * (dp - delta)).astype(BF16)
                return (jnp.dot(ds, kt, preferred_element_type=F32),
                        lax.dot_general(ds, qb, _TN, preferred_element_type=F32),
                        lax.dot_general(p.astype(BF16), dob, _TN, preferred_element_type=F32))

            ip = jnp.maximum(i - 1, 0)
            dq_p, dk_p, dv_p = grads(kp_ref[0], vp_ref[0], kb_ref[0, ip], (col > row) & (i > 0))
            dq_c, dk_c, dv_c = grads(kc_ref[0], vc_ref[0], kb_ref[0, i], row >= col)
            dq_ref[...] = (dq_p + dq_c).reshape(G, T, dk)
            dk_ref[0] = dk_sc[...] + dk_p
            dv_ref[0] = dv_sc[...] + dv_p
            dk_sc[...] = dk_c
            dv_sc[...] = dv_c

    def cur(i):
        return jnp.minimum(i, nt - 1)

    def prev(i):
        return jnp.maximum(jnp.minimum(i, nt - 1) - 1, 0)

    def written(i):
        return jnp.maximum(i - 1, 0)

    qs = lambda d: pl.BlockSpec((G, T, d), lambda h, i: (h, cur(i), 0))
    return pl.pallas_call(
        body, name=name, grid=(Hkv, nt + 1),
        in_specs=[qs(dk),
                  pl.BlockSpec((1, T, dk), lambda h, i: (h, cur(i), 0)),
                  pl.BlockSpec((1, T, dk), lambda h, i: (h, prev(i), 0)),
                  pl.BlockSpec((1, T, dv), lambda h, i: (h, cur(i), 0)),
                  pl.BlockSpec((1, T, dv), lambda h, i: (h, prev(i), 0)),
                  pl.BlockSpec((1, nt, 1, T), lambda h, i: (0, 0, 0, 0)),
                  qs(dv), qs(1), qs(dv)],
        out_specs=[qs(dk),
                   pl.BlockSpec((1, T, dk), lambda h, i: (h, written(i), 0)),
                   pl.BlockSpec((1, T, dv), lambda h, i: (h, written(i), 0))],
        out_shape=[jax.ShapeDtypeStruct((H, L, dk), F32), jax.ShapeDtypeStruct((Hkv, L, dk), F32),
                   jax.ShapeDtypeStruct((Hkv, L, dv), F32)],
        scratch_shapes=[pltpu.VMEM((T, dk), F32), pltpu.VMEM((T, dv), F32)],
        compiler_params=_cparams(("parallel", "arbitrary")),
    )(q, k, k, v, v, kbias, o, lse, do)


def _adamw(w, gparts, m, v, *, name):
    n, R, C = gparts.shape
    tr = _pick(R, (128, 64, 32, 16, 8))
    c1 = 1.0 - ADAM_B1 ** ADAM_STEP
    c2 = 1.0 - ADAM_B2 ** ADAM_STEP

    def body(w_ref, g_ref, m_ref, v_ref, go_ref, d_ref, mo_ref, vo_ref):
        g = g_ref[0].astype(F32)
        for t in range(1, n):
            g = g + g_ref[t].astype(F32)
        mn = ADAM_B1 * m_ref[...] + (1.0 - ADAM_B1) * g
        vn = ADAM_B2 * v_ref[...] + (1.0 - ADAM_B2) * (g * g)
        go_ref[...] = g
        mo_ref[...] = mn
        vo_ref[...] = vn
        d_ref[...] = -ADAM_LR * ((mn / c1) / (jnp.sqrt(vn / c2) + ADAM_EPS) + ADAM_WD * w_ref[...])

    spec = pl.BlockSpec((tr, C), lambda i: (i, 0))
    return pl.pallas_call(
        body, name=name, grid=(R // tr,),
        in_specs=[spec, pl.BlockSpec((n, tr, C), lambda i: (0, i, 0)), spec, spec],
        out_specs=[spec] * 4,
        out_shape=[jax.ShapeDtypeStruct((R, C), F32)] * 4,
        compiler_params=_cparams(("parallel",)),
    )(w, gparts, m, v)


def _me():
    return lax.axis_index("x"), lax.axis_index("y"), lax.axis_index("c")


class _CommJob:
    def __init__(self, arrs):
        self.arrs = list(arrs)
        self.n = len(arrs)
        self.scratch = [pltpu.SemaphoreType.DMA((self.n, 7)), pltpu.SemaphoreType.DMA((self.n, 7)),
                        pltpu.SemaphoreType.DMA((self.n,))]

    def bind(self, refs):
        n = self.n
        self.ins, self.outs = refs[:n], refs[n:2 * n]
        self.send_sems, self.recv_sems, self.local_sems = refs[2 * n:2 * n + 3]

    def middle(self):
        pass


class _Gather(_CommJob):
    def __init__(self, arrs):
        super().__init__(arrs)
        self.out_shape = [jax.ShapeDtypeStruct((N_DEV, *a.shape), a.dtype) for a in arrs]

    def _where(self):
        x, y, c = _me()
        return (x, y, c), (x, y, 1 - c), [(1 - x, y), (x, 1 - y), (1 - x, 1 - y)], c

    def _copy(self, t, k, block, to, src=None):
        dst = self.outs[t].at[4 * block[0] + 2 * block[1] + block[2]]
        return pltpu.make_async_remote_copy(
            src_ref=dst if src is None else src, dst_ref=dst,
            send_sem=self.send_sems.at[t, k], recv_sem=self.recv_sems.at[t, k],
            device_id=to, device_id_type=MESH)

    def _mine(self, t, me):
        return pltpu.make_async_copy(self.ins[t], self.outs[t].at[4 * me[0] + 2 * me[1] + me[2]],
                                     self.local_sems.at[t])

    def _first(self, t, me, sibling, chips, c):
        return [self._copy(t, 0, me, sibling, src=self.ins[t])] + \
               [self._copy(t, 1 + j, me, (*chip, c), src=self.ins[t]) for j, chip in enumerate(chips)]

    def start(self):
        me, sibling, chips, c = self._where()
        for t in range(self.n):
            self._mine(t, me).start()
            for cp in self._first(t, me, sibling, chips, c):
                cp.start()

    def middle(self):
        me, sibling, chips, c = self._where()
        for j, chip in enumerate(chips):
            for t in range(self.n):
                self._copy(t, 1 + j, (*chip, c), me).wait_recv()
                self._copy(t, 4 + j, (*chip, c), sibling).start()

    def finish(self):
        me, sibling, chips, c = self._where()
        for t in range(self.n):
            self._copy(t, 0, sibling, me).wait_recv()
            for j, chip in enumerate(chips):
                self._copy(t, 4 + j, (*chip, 1 - c), me).wait_recv()
        for t in range(self.n):
            for cp in self._first(t, me, sibling, chips, c):
                cp.wait_send()
            for j, chip in enumerate(chips):
                self._copy(t, 4 + j, (*chip, c), sibling).wait_send()
            self._mine(t, me).wait()


class _Exchange(_CommJob):
    def __init__(self, arrs):
        super().__init__(arrs)
        self.out_shape = [jax.ShapeDtypeStruct(a.shape, a.dtype) for a in arrs]

    def _copies(self, t):
        x, y, c = _me()
        my_idx = 4 * x + 2 * y + c
        pairs = []
        for k in range(1, N_DEV):
            peer = (x ^ ((k >> 2) & 1), y ^ ((k >> 1) & 1), c ^ (k & 1))
            peer_idx = 4 * peer[0] + 2 * peer[1] + peer[2]
            sems = dict(send_sem=self.send_sems.at[t, k - 1], recv_sem=self.recv_sems.at[t, k - 1],
                        device_id=peer, device_id_type=MESH)
            pairs.append((pltpu.make_async_remote_copy(src_ref=self.ins[t].at[peer_idx],
                                                       dst_ref=self.outs[t].at[my_idx], **sems),
                          pltpu.make_async_remote_copy(src_ref=self.ins[t].at[peer_idx],
                                                       dst_ref=self.outs[t].at[peer_idx], **sems)))
        return pairs

    def _mine(self, t):
        x, y, c = _me()
        my_idx = 4 * x + 2 * y + c
        return pltpu.make_async_copy(self.ins[t].at[my_idx], self.outs[t].at[my_idx], self.local_sems.at[t])

    def start(self):
        for t in range(self.n):
            self._mine(t).start()
            for snd, _ in self._copies(t):
                snd.start()

    def finish(self):
        for t in range(self.n):
            pairs = self._copies(t)
            for _, rcv in pairs:
                rcv.wait_recv()
            for snd, _ in pairs:
                snd.wait_send()
            self._mine(t).wait()


def _comm(job, *, name):
    def body(*refs):
        job.bind(refs)
        job.start()
        job.middle()
        job.finish()

    any_spec = pl.BlockSpec(memory_space=pl.ANY)
    return pl.pallas_call(
        body, name=name,
        in_specs=[any_spec] * job.n, out_specs=[any_spec] * job.n,
        out_shape=job.out_shape, scratch_shapes=job.scratch,
        compiler_params=pltpu.CompilerParams(has_side_effects=True),
    )(*job.arrs)


def _ride(job, refs, step, total):
    job.bind(refs)

    @pl.when(step == 0)
    def _():
        job.start()

    @pl.when(step == (total * 3) // 5)
    def _():
        job.middle()

    @pl.when(step == total - 1)
    def _():
        job.finish()


SCALE_A = DH ** -0.5
SCALE_B = (MLA_NOPE + MLA_ROPE) ** -0.5
SCALE_C = DH ** -0.5
DQK_B = MLA_NOPE + MLA_ROPE


def _rope_tables(L, half, width):
    pos = (jnp.arange(L) - PAD).astype(F32)
    lane = jnp.arange(width)
    inv = ROPE_THETA ** (-(lane % half).astype(F32) / half)
    ang = pos[:, None] * inv[None, :]
    sign = jnp.where(lane % (2 * half) < half, -1.0, 1.0).astype(F32)
    return jnp.cos(ang), jnp.sin(ang) * sign[None, :]


def _rope_lanes(x, cos, sin, half):
    W = x.shape[1]
    lane = lax.broadcasted_iota(jnp.int32, x.shape, 1)
    first = (lane & (2 * half - 1)) < half
    partner = jnp.where(first, pltpu.roll(x, W - half, 1), pltpu.roll(x, half, 1))
    return x * cos + partner * sin


def _tile_lanes(t, width):
    return t if t.shape[1] == width else jnp.tile(t, (1, width // t.shape[1]))


def _split(x, H, d, dst):
    for h in range(H):
        dst[h] = x[:, d * h:d * (h + 1)].astype(dst.dtype)


def _join(src, H):
    return jnp.concatenate([src[h] for h in range(H)], axis=1)


def _head_spec(H, tm, d):
    return pl.BlockSpec((H, tm, d), lambda i: (0, i, 0))


def _prep_a(proj, *, name):
    L = proj.shape[0]
    tm = ROW_T

    def body(x_ref, qo, ko, vo):
        _split(x_ref[:, 0:512] * SCALE_A, HEADS, DH, qo)
        _split(x_ref[:, 512:1024], HEADS, DH, ko)
        _split(x_ref[:, 1024:1536], HEADS, DH, vo)

    return pl.pallas_call(
        body, name=name, grid=(L // tm,),
        in_specs=[_col_spec(tm, 1536, QKVA)],
        out_specs=[_head_spec(HEADS, tm, DH)] * 3,
        out_shape=[jax.ShapeDtypeStruct((HEADS, L, DH), BF16)] * 3,
        compiler_params=_cparams(("parallel",)),
    )(proj)


def _unprep_a(dq, dk, dv, dproj, *, name):
    L = dq.shape[1]
    tm = ROW_T

    def body(dq_ref, dk_ref, dv_ref, _, dp_ref):
        dp_ref[:, 0:512] = (_join(dq_ref, HEADS) * SCALE_A).astype(BF16)
        dp_ref[:, 512:1024] = _join(dk_ref, HEADS).astype(BF16)
        dp_ref[:, 1024:1536] = _join(dv_ref, HEADS).astype(BF16)

    hs = _head_spec(HEADS, tm, DH)
    return pl.pallas_call(
        body, name=name, grid=(L // tm,),
        in_specs=[hs, hs, hs, pl.BlockSpec(memory_space=pl.ANY)],
        out_specs=_col_spec(tm, 1536, QKVA),
        out_shape=jax.ShapeDtypeStruct(dproj.shape, dproj.dtype),
        input_output_aliases={3: 0},
        compiler_params=_cparams(("parallel",)),
    )(dq, dk, dv, dproj)


def _prep_c(proj, tab, *, name):
    L = proj.shape[0]
    tm = ROW_T

    def body(x_ref, cos_ref, sin_ref, qo, ko, vo):
        cos, sin = cos_ref[...], sin_ref[...]
        q = _rope_lanes(x_ref[:, 0:512], _tile_lanes(cos, 512), _tile_lanes(sin, 512), DH // 2)
        _split(q * SCALE_C, HEADS, DH, qo)
        _split(_rope_lanes(x_ref[:, 512:640], cos, sin, DH // 2), SWA_KV_HEADS, DH, ko)
        _split(x_ref[:, 640:768], SWA_KV_HEADS, DH, vo)

    t128 = pl.BlockSpec((tm, LANES), lambda i: (i, 0))
    return pl.pallas_call(
        body, name=name, grid=(L // tm,),
        in_specs=[_col_spec(tm, 768, QKVC), t128, t128],
        out_specs=[_head_spec(HEADS, tm, DH), _head_spec(SWA_KV_HEADS, tm, DH), _head_spec(SWA_KV_HEADS, tm, DH)],
        out_shape=[jax.ShapeDtypeStruct((HEADS, L, DH), BF16), jax.ShapeDtypeStruct((SWA_KV_HEADS, L, DH), BF16),
                   jax.ShapeDtypeStruct((SWA_KV_HEADS, L, DH), BF16)],
        compiler_params=_cparams(("parallel",)),
    )(proj, *tab)


def _unprep_c(dq, dk, dv, tab, dproj, *, name):
    L = dq.shape[1]
    tm = ROW_T

    def body(dq_ref, dk_ref, dv_ref, cos_ref, sin_ref, _, dp_ref):
        cos, nsin = cos_ref[...], -sin_ref[...]
        dqv = _join(dq_ref, HEADS) * SCALE_C
        dp_ref[:, 0:512] = _rope_lanes(dqv, _tile_lanes(cos, 512), _tile_lanes(nsin, 512), DH // 2).astype(BF16)
        dp_ref[:, 512:640] = _rope_lanes(_join(dk_ref, SWA_KV_HEADS), cos, nsin, DH // 2).astype(BF16)
        dp_ref[:, 640:768] = _join(dv_ref, SWA_KV_HEADS).astype(BF16)

    hs = _head_spec(HEADS, tm, DH)
    hkv = _head_spec(SWA_KV_HEADS, tm, DH)
    t128 = pl.BlockSpec((tm, LANES), lambda i: (i, 0))
    return pl.pallas_call(
        body, name=name, grid=(L // tm,),
        in_specs=[hs, hkv, hkv, t128, t128, pl.BlockSpec(memory_space=pl.ANY)],
        out_specs=_col_spec(tm, 768, QKVC),
        out_shape=jax.ShapeDtypeStruct(dproj.shape, dproj.dtype),
        input_output_aliases={5: 0},
        compiler_params=_cparams(("parallel",)),
    )(dq, dk, dv, *tab, dproj)


def _q_tables(cos, sin, on_ref):
    on = on_ref[...] > 0.5
    width = on_ref.shape[1]
    return jnp.where(on, _tile_lanes(cos, width), 1.0), jnp.where(on, _tile_lanes(sin, width), 0.0)


def _prep_b(qbm, kvbm, proj, tab_k, q_rotary, *, name):
    L = proj.shape[0]
    tm = ROW_T

    def body(q_ref, kv_ref, kr_ref, ck_ref, sk_ref, on_ref, qo, ko, vo):
        cq, sq = _q_tables(ck_ref[...], sk_ref[...], on_ref)
        q = _rope_lanes(q_ref[...], cq, sq, MLA_ROPE // 2) * SCALE_B
        _split(q, HEADS, DQK_B, qo)
        kr = _rope_lanes(kr_ref[...], ck_ref[...], sk_ref[...], MLA_ROPE // 2)[:, :MLA_ROPE].astype(BF16)
        kv = kv_ref[...]
        for h in range(HEADS):
            ko[h] = jnp.concatenate([kv[:, 128 * h:128 * h + MLA_NOPE].astype(BF16), kr], axis=1)
            vo[h] = kv[:, 128 * h + MLA_NOPE:128 * (h + 1)].astype(BF16)

    t128 = pl.BlockSpec((tm, LANES), lambda i: (i, 0))
    t768 = pl.BlockSpec((tm, 768), lambda i: (i, 0))
    return pl.pallas_call(
        body, name=name, grid=(L // tm,),
        in_specs=[t768, pl.BlockSpec((tm, 1024), lambda i: (i, 0)), _col_spec(tm, 128, KR), t128, t128,
                  pl.BlockSpec((1, 768), lambda i: (0, 0))],
        out_specs=[_head_spec(HEADS, tm, DQK_B), _head_spec(HEADS, tm, DQK_B), _head_spec(HEADS, tm, DH)],
        out_shape=[jax.ShapeDtypeStruct((HEADS, L, DQK_B), BF16), jax.ShapeDtypeStruct((HEADS, L, DQK_B), BF16),
                   jax.ShapeDtypeStruct((HEADS, L, DH), BF16)],
        compiler_params=_cparams(("parallel",)),
    )(qbm, kvbm, proj, *tab_k, q_rotary)


def _unprep_b(dq, dk, dv, tab_k, q_rotary, dproj, *, name):
    L = dq.shape[1]
    tm = ROW_T

    def body(dq_ref, dk_ref, dv_ref, ck_ref, sk_ref, on_ref, _, dp_kr, dqo, dkvo):
        cq, sq = _q_tables(ck_ref[...], sk_ref[...], on_ref)
        dqv = _join(dq_ref, HEADS) * SCALE_B
        dqo[...] = _rope_lanes(dqv, cq, -sq, MLA_ROPE // 2).astype(BF16)
        parts = []
        dkr = None
        for h in range(HEADS):
            dkh = dk_ref[h]
            parts += [dkh[:, :MLA_NOPE], dv_ref[h]]
            r = dkh[:, MLA_NOPE:]
            dkr = r if dkr is None else dkr + r
        dkvo[...] = jnp.concatenate(parts, axis=1).astype(BF16)
        dkr = jnp.concatenate([dkr, jnp.zeros((tm, LANES - MLA_ROPE), F32)], axis=1)
        dp_kr[...] = _rope_lanes(dkr, ck_ref[...], -sk_ref[...], MLA_ROPE // 2).astype(BF16)

    t128 = pl.BlockSpec((tm, LANES), lambda i: (i, 0))
    t768 = pl.BlockSpec((tm, 768), lambda i: (i, 0))
    hq = _head_spec(HEADS, tm, DQK_B)
    return pl.pallas_call(
        body, name=name, grid=(L // tm,),
        in_specs=[hq, hq, _head_spec(HEADS, tm, DH), t128, t128, pl.BlockSpec((1, 768), lambda i: (0, 0)),
                  pl.BlockSpec(memory_space=pl.ANY)],
        out_specs=[_col_spec(tm, 128, KR), t768, pl.BlockSpec((tm, 1024), lambda i: (i, 0))],
        out_shape=[jax.ShapeDtypeStruct(dproj.shape, dproj.dtype), jax.ShapeDtypeStruct((L, 768), BF16),
                   jax.ShapeDtypeStruct((L, 1024), BF16)],
        input_output_aliases={6: 0},
        compiler_params=_cparams(("parallel",)),
    )(dq, dk, dv, *tab_k, q_rotary, dproj)


def _gate(y, proj, zcol, *, name):
    L = proj.shape[0]
    tm = ROW_T

    def body(y_ref, z_ref, u_ref):
        z = z_ref[...]
        u_ref[...] = (_join(y_ref, HEADS) * (z * jax.nn.sigmoid(z))).astype(BF16)

    return pl.pallas_call(
        body, name=name, grid=(L // tm,),
        in_specs=[_head_spec(HEADS, tm, DH), _col_spec(tm, 512, zcol)],
        out_specs=pl.BlockSpec((tm, 512), lambda i: (i, 0)),
        out_shape=jax.ShapeDtypeStruct((L, 512), BF16),
        compiler_params=_cparams(("parallel",)),
    )(y, proj)


def _gate_bwd(du, y, proj, zcol, dproj, *, delta_rows=False, name):
    L = proj.shape[0]
    tm = ROW_T
    assert not delta_rows or tm == ATT_T

    def body(du_ref, y_ref, z_ref, _, dz_ref, dy_ref, *dl_ref):
        z = z_ref[...]
        duv = du_ref[...]
        sg = jax.nn.sigmoid(z)
        yv = _join(y_ref, HEADS)
        dyv = duv * (z * sg)
        _split(dyv, HEADS, DH, dy_ref)
        dz_ref[...] = (duv * yv * (sg * (1.0 + z * (1.0 - sg)))).astype(BF16)
        if delta_rows:
            prod = dyv * yv
            ones = jnp.ones((8, DH), F32)
            for h in range(HEADS):
                sums = lax.dot_general(ones, prod[:, DH * h:DH * (h + 1)], _NT, preferred_element_type=F32)
                dl_ref[0][h, 0] = sums[0:1, :]

    hs = _head_spec(HEADS, tm, DH)
    out_specs = [_col_spec(tm, 512, zcol), hs]
    out_shape = [jax.ShapeDtypeStruct(dproj.shape, dproj.dtype),
                 jax.ShapeDtypeStruct((HEADS, L, DH), BF16 if delta_rows else F32)]
    if delta_rows:
        out_specs.append(pl.BlockSpec((HEADS, 1, 1, tm), lambda i: (0, i, 0, 0)))
        out_shape.append(jax.ShapeDtypeStruct((HEADS, L // tm, 1, tm), F32))
    return pl.pallas_call(
        body, name=name, grid=(L // tm,),
        in_specs=[pl.BlockSpec((tm, 512), lambda i: (i, 0)), hs, _col_spec(tm, 512, zcol),
                  pl.BlockSpec(memory_space=pl.ANY)],
        out_specs=out_specs, out_shape=out_shape,
        input_output_aliases={3: 0},
        compiler_params=_cparams(("parallel",)),
    )(du, y, proj, dproj)


MERGE_T = 192


def _merge(proj, pbs, *, name):
    L = proj.shape[0]
    tm = MERGE_T

    def body(g0, g1, g2, p0, p1, p2, o_ref):
        acc = None
        for g_ref, p_ref in ((g0, p0), (g1, p1), (g2, p2)):
            t = jax.nn.sigmoid(g_ref[...]) * p_ref[...]
            acc = t if acc is None else acc + t
        o_ref[...] = acc.astype(BF16)

    row = pl.BlockSpec((tm, D_MODEL), lambda i: (i, 0))
    return pl.pallas_call(
        body, name=name, grid=(L // tm,),
        in_specs=[_col_spec(tm, D_MODEL, GATES + n * D_MODEL) for n in range(N_BRANCH)] + [row] * N_BRANCH,
        out_specs=row, out_shape=jax.ShapeDtypeStruct((L, D_MODEL), BF16),
        compiler_params=_cparams(("parallel",)),
    )(proj, proj, proj, *pbs)


def _merge_bwd(dmerged, proj, pbs, dproj, *, name):
    L = proj.shape[0]
    tm = MERGE_T

    def body(dm_ref, g_ref, p0, p1, p2, _, dg_ref, dp0, dp1, dp2):
        dm = dm_ref[...]
        for n, (p_ref, dp_ref) in enumerate(((p0, dp0), (p1, dp1), (p2, dp2))):
            cols = slice(n * D_MODEL, (n + 1) * D_MODEL)
            sg = jax.nn.sigmoid(g_ref[:, cols])
            dp_ref[...] = (dm * sg).astype(BF16)
            dg_ref[:, cols] = (dm * p_ref[...] * (sg * (1.0 - sg))).astype(BF16)

    row = pl.BlockSpec((tm, D_MODEL), lambda i: (i, 0))
    gates = _col_spec(tm, N_BRANCH * D_MODEL, GATES)
    outs = pl.pallas_call(
        body, name=name, grid=(L // tm,),
        in_specs=[row, gates] + [row] * N_BRANCH + [pl.BlockSpec(memory_space=pl.ANY)],
        out_specs=[gates] + [row] * N_BRANCH,
        out_shape=[jax.ShapeDtypeStruct(dproj.shape, dproj.dtype)]
        + [jax.ShapeDtypeStruct((L, D_MODEL), BF16)] * N_BRANCH,
        input_output_aliases={5: 0},
        compiler_params=_cparams(("parallel",)),
    )(dmerged, proj, *pbs, dproj)
    return outs[0], outs[1:]


def _scan_rows(x, reverse):
    rows = lax.broadcasted_iota(jnp.int32, x.shape, 0)
    step = 1
    while step < BLK:
        if reverse:
            x = x + jnp.where(rows < BLK - step, pltpu.roll(x, BLK - step, 0), 0.0)
        else:
            x = x + jnp.where(rows >= step, pltpu.roll(x, step, 0), 0.0)
        step *= 2
    return x


def _forget_fwd(proj, b_f, *, name):
    L = proj.shape[0]
    nb = L // BLK

    def body(x_ref, b_ref, o_ref, carry):
        i = pl.program_id(0)

        @pl.when(i == 0)
        def _():
            carry[...] = jnp.zeros_like(carry)

        c = _scan_rows(jax.nn.log_sigmoid(x_ref[...] + b_ref[...]), False) + carry[...]
        carry[...] = c[BLK - 1:BLK, :]
        pos = i * BLK + lax.broadcasted_iota(jnp.int32, (HEADS, BLK), 1)
        o_ref[...] = c.T[:HEADS, :] + jnp.where(pos < PAD, BIG, 0.0)

    return pl.pallas_call(
        body, name=name, grid=(nb,),
        in_specs=[_col_spec(BLK, LANES, FA), pl.BlockSpec((1, LANES), lambda i: (0, 0))],
        out_specs=pl.BlockSpec((HEADS, BLK), lambda i: (0, i)),
        out_shape=jax.ShapeDtypeStruct((HEADS, L), F32),
        scratch_shapes=[pltpu.VMEM((1, LANES), F32)],
        compiler_params=_cparams(("arbitrary",)),
    )(proj, jnp.pad(b_f, (0, LANES - HEADS)).reshape(1, LANES))


def _forget_bwd(dct, proj, b_f, dproj, *, name):
    L = proj.shape[0]
    nb = L // BLK

    def body(d_ref, x_ref, b_ref, _, dp_ref, db_ref, carry):
        i = pl.program_id(0)

        @pl.when(i == 0)
        def _():
            carry[...] = jnp.zeros_like(carry)
            db_ref[...] = jnp.zeros_like(db_ref)

        d = jnp.concatenate([d_ref[...], jnp.zeros((BLK - HEADS, BLK), F32)], axis=0).T
        dlog = _scan_rows(d, True) + carry[...]
        carry[...] = dlog[0:1, :]
        lane = lax.broadcasted_iota(jnp.int32, (BLK, LANES), 1)
        daf = jnp.where(lane < HEADS, dlog * jax.nn.sigmoid(-(x_ref[...] + b_ref[...])), 0.0)
        db_ref[...] += jnp.sum(daf, axis=0, keepdims=True)
        dp_ref[...] = jnp.concatenate([daf, jnp.zeros((BLK, LANES), F32)], axis=1).astype(BF16)

    back = lambda i: nb - 1 - i
    dp, db = pl.pallas_call(
        body, name=name, grid=(nb,),
        in_specs=[pl.BlockSpec((HEADS, BLK), lambda i: (0, back(i))),
                  pl.BlockSpec((BLK, LANES), lambda i: (back(i), FA // LANES)),
                  pl.BlockSpec((1, LANES), lambda i: (0, 0)), pl.BlockSpec(memory_space=pl.ANY)],
        out_specs=[pl.BlockSpec((BLK, 2 * LANES), lambda i: (back(i), FA // (2 * LANES))),
                   pl.BlockSpec((1, LANES), lambda i: (0, 0))],
        out_shape=[jax.ShapeDtypeStruct(dproj.shape, dproj.dtype), jax.ShapeDtypeStruct((1, LANES), F32)],
        scratch_shapes=[pltpu.VMEM((1, LANES), F32)],
        input_output_aliases={3: 0},
        compiler_params=_cparams(("arbitrary",)),
    )(dct, proj, jnp.pad(b_f, (0, LANES - HEADS)).reshape(1, LANES), dproj)
    return dp, db[0, :HEADS]


def _key_bias(L, T, ct=None):
    padb = jnp.where(jnp.arange(L) < PAD, BIG, 0.0).astype(F32)[None]
    kb = padb if ct is None else ct + padb
    return kb.reshape(kb.shape[0], L // T, 1, T)


def _layer_fwd(h, w, tabs, l, side=None):
    tag = f"l{l}"
    L = h.shape[0]
    hn = _rms_fwd(h, w["norm_g"], name=f"{tag}_rms_in")
    if "late" in w:
        late_job, assemble = w["late"]
        proj, *late = _mm(hn, w["w_in"], side=late_job, name=f"{tag}_mm_in")
        w = {**{key: val for key, val in w.items() if key != "late"}, **assemble(*late)}
    else:
        proj = _mm(hn, w["w_in"], name=f"{tag}_mm_in")
    kb_a = _forget_fwd(proj, w["b_f"], name=f"{tag}_forget").reshape(HEADS, L // ATT_T, 1, ATT_T)
    ops_a = (*_prep_a(proj, name=f"{tag}_prep_a"), kb_a)
    ya, lsea, *side_out = _attn_fwd(*ops_a, T=ATT_T, side=side, name=f"{tag}_attn_a")
    cqn = _rms_fwd(proj, w["g_cq"], col=CQ, name=f"{tag}_rms_cq")
    ckvn = _rms_fwd(proj, w["g_ckv"], col=CKV, name=f"{tag}_rms_ckv")
    qbm = _mm(cqn, w["w_uq"], name=f"{tag}_mm_uq")
    kvbm = _mm(ckvn, w["w_ukv"], name=f"{tag}_mm_ukv")
    ops_b = (*_prep_b(qbm, kvbm, proj, tabs["bk"], tabs["bq"], name=f"{tag}_prep_b"), _key_bias(L, ATT_T))
    yb, lseb = _attn_fwd(*ops_b, T=ATT_T, name=f"{tag}_attn_b")
    ops_c = (*_prep_c(proj, tabs["c"], name=f"{tag}_prep_c"), _key_bias(L, WINDOW))
    sink = jnp.broadcast_to(w["sinks"][:, None, None], (HEADS, 1, LANES))
    yc, lsec = _attn_fwd(*ops_c, T=WINDOW, window=True, sink=sink, name=f"{tag}_attn_c")
    us = [_gate(y, proj, zcol, name=f"{tag}_gate{n}") for n, (y, zcol) in enumerate(((ya, ZA), (yb, ZB), (yc, ZC)))]
    pbr = [_mm(us[n], w["w_branch"][n], name=f"{tag}_mm_br{n}") for n in range(N_BRANCH)]
    merged = _merge(proj, pbr, name=f"{tag}_merge")
    out = _mm(merged, w["w_out"], add=h, name=f"{tag}_mm_out")
    saved = dict(h=h, hn=hn, proj=proj, ops_a=ops_a, ya=ya, lsea=lsea, cqn=cqn, ckvn=ckvn,
                 ops_b=ops_b, yb=yb, lseb=lseb, ops_c=ops_c, yc=yc, lsec=lsec, us=us, pbr=pbr, merged=merged)
    return out, saved, side_out, w


def _w_in_chunks(g, tag):
    return _w_in_to_shards(g["w_in"], name=f"{tag}_w_in_chunks")


def _grad_chunks(g, tag, with_w_in=True):
    br = _cut_shards(g["w_branch"].reshape(N_BRANCH * BRANCH_W, D_MODEL), name=f"{tag}_w_branch_chunks")
    rest = [_cut_shards(g["w_uq"], name=f"{tag}_w_uq_chunks"), _cut_shards(g["w_ukv"], name=f"{tag}_w_ukv_chunks"),
            br.reshape(N_DEV, N_BRANCH, BRANCH_W, D_MODEL // N_DEV),
            g["w_out"].reshape(N_DEV, D_MODEL // N_DEV, D_MODEL)]
    return ([_w_in_chunks(g, tag)] if with_w_in else []) + rest


def _layer_bwd(dout, s, w, tabs, l, side=None, own_exchange=False):
    tag = f"l{l}"
    L = dout.shape[0]
    proj = s["proj"]
    g = {}
    dproj = jnp.zeros((L, NP_IN), BF16)
    dmerged = _mm(dout, w["w_out"], tb=True, name=f"{tag}_mm_out_dx")
    g["w_out"] = _mm(s["merged"], dout, ta=True, out_dtype=BF16, name=f"{tag}_mm_out_dw")
    dproj, dpbr = _merge_bwd(dmerged, proj, s["pbr"], dproj, name=f"{tag}_merge_bwd")
    dus = [_mm(dpbr[n], w["w_branch"][n], tb=True, name=f"{tag}_mm_br{n}_dx") for n in range(N_BRANCH)]
    g["w_branch"] = jnp.stack([_mm(s["us"][n], dpbr[n], ta=True, out_dtype=BF16, name=f"{tag}_mm_br{n}_dw")
                               for n in range(N_BRANCH)])
    dproj, dya, dla = _gate_bwd(dus[0], s["ya"], proj, ZA, dproj, delta_rows=True, name=f"{tag}_gate0_bwd")
    dproj, dyb, dlb = _gate_bwd(dus[1], s["yb"], proj, ZB, dproj, delta_rows=True, name=f"{tag}_gate1_bwd")
    dproj, dyc = _gate_bwd(dus[2], s["yc"], proj, ZC, dproj, name=f"{tag}_gate2_bwd")

    def bwd_operands(ops, lse, delta, dy16):
        q16, k16, v16, kbias = ops
        return (q16, k16, v16, kbias.reshape(kbias.shape[0], L, 1), lse.reshape(HEADS, L // ATT_T, 1, ATT_T), delta,
                dy16)

    dqa, dka, dva, dcq, dck, *side_out = _attn_bwd_t(*bwd_operands(s["ops_a"], s["lsea"], dla, dya), T=ATT_T,
                                                     fox=True, side=side, name=f"{tag}_attn_a_bwd")
    dproj = _unprep_a(dqa, dka, dva, dproj, name=f"{tag}_unprep_a")
    dproj, g["b_f"] = _forget_bwd(dcq.reshape(HEADS, L) + dck[:, :, 0], proj, w["b_f"], dproj,
                                  name=f"{tag}_forget_bwd")
    dqb, dkb, dvb, _, _ = _attn_bwd_t(*bwd_operands(s["ops_b"], s["lseb"], dlb, dyb), T=ATT_T,
                                      name=f"{tag}_attn_b_bwd")
    dproj, dqbm, dkvbm = _unprep_b(dqb, dkb, dvb, tabs["bk"], tabs["bq"], dproj, name=f"{tag}_unprep_b")
    dcqn = _mm(dqbm, w["w_uq"], tb=True, name=f"{tag}_mm_uq_dx")
    g["w_uq"] = _mm(s["cqn"], dqbm, ta=True, out_dtype=BF16, name=f"{tag}_mm_uq_dw")
    dckvn = _mm(dkvbm, w["w_ukv"], tb=True, name=f"{tag}_mm_ukv_dx")
    g["w_ukv"] = _mm(s["ckvn"], dkvbm, ta=True, out_dtype=BF16, name=f"{tag}_mm_ukv_dw")
    dproj, g["g_cq"] = _rms_bwd(dcqn, proj, w["g_cq"], col=CQ, into=dproj, name=f"{tag}_rms_cq_bwd")
    dproj, g["g_ckv"] = _rms_bwd(dckvn, proj, w["g_ckv"], col=CKV, into=dproj, name=f"{tag}_rms_ckv_bwd")
    dqc, dkc, dvc = _attn_bwd_window(*s["ops_c"], s["yc"], s["lsec"], dyc, name=f"{tag}_attn_c_bwd")
    dproj = _unprep_c(dqc, dkc, dvc, tabs["c"], dproj, name=f"{tag}_unprep_c")
    delta_c = jnp.sum(dyc * s["yc"], axis=-1)
    g["sinks"] = -jnp.sum(jnp.exp(w["sinks"][:, None] - s["lsec"][:, :, 0]) * delta_c, axis=1)
    own_out = []
    if own_exchange:
        g["w_in"], *r_rest = _mm(s["hn"], dproj, ta=True, out_dtype=BF16, side=_Exchange(_grad_chunks(g, tag, False)),
                                 name=f"{tag}_mm_in_dw")
        dhn, r_in = _mm(dproj, w["w_in"], tb=True, side=_Exchange([_w_in_chunks(g, tag)]), name=f"{tag}_mm_in_dx")
        own_out = [r_in, *r_rest]
    else:
        g["w_in"] = _mm(s["hn"], dproj, ta=True, out_dtype=BF16, name=f"{tag}_mm_in_dw")
        dhn = _mm(dproj, w["w_in"], tb=True, name=f"{tag}_mm_in_dx")
    dh, g["norm_g"] = _rms_bwd(dhn, s["h"], w["norm_g"], add=dout, name=f"{tag}_rms_in_bwd")
    return dh, g, side_out, own_out


def _cols_from_shards(g):
    return jnp.moveaxis(g, 0, 1).reshape(g.shape[1], N_DEV * g.shape[2])


def _cols_to_shards(w):
    R = w.shape[0]
    return jnp.moveaxis(w.reshape(R, N_DEV, w.shape[1] // N_DEV), 1, 0)


def _join_shards(g, *, name):
    _, R, C = g.shape
    tr = _pick(R, (512, 384, 256))

    def body(g_ref, o_ref):
        for d in range(N_DEV):
            o_ref[:, C * d:C * (d + 1)] = g_ref[d]

    return pl.pallas_call(
        body, name=name, grid=(R // tr,),
        in_specs=[pl.BlockSpec((N_DEV, tr, C), lambda i: (0, i, 0))],
        out_specs=pl.BlockSpec((tr, N_DEV * C), lambda i: (i, 0)),
        out_shape=jax.ShapeDtypeStruct((R, N_DEV * C), g.dtype),
        compiler_params=_cparams(("parallel",)),
    )(g)


def _cut_shards(w, *, name):
    R = w.shape[0]
    C = w.shape[1] // N_DEV
    tr = _pick(R, (512, 384, 256))

    def body(w_ref, o_ref):
        for d in range(N_DEV):
            o_ref[d] = w_ref[:, C * d:C * (d + 1)]

    return pl.pallas_call(
        body, name=name, grid=(R // tr,),
        in_specs=[pl.BlockSpec((tr, N_DEV * C), lambda i: (i, 0))],
        out_specs=pl.BlockSpec((N_DEV, tr, C), lambda i: (0, i, 0)),
        out_shape=jax.ShapeDtypeStruct((N_DEV, R, C), w.dtype),
        compiler_params=_cparams(("parallel",)),
    )(w)


def _pad_in(w):
    parts, at = [], 0
    for lo, hi, dst in sorted(_RUNS, key=lambda r: r[2]):
        if dst > at:
            parts.append(jnp.zeros((w.shape[0], dst - at), w.dtype))
        parts.append(w[:, lo:hi])
        at = dst + hi - lo
    parts.append(jnp.zeros((w.shape[0], NP_IN - at), w.dtype))
    return jnp.concatenate(parts, axis=1)


def _unpad_in(w):
    return jnp.concatenate([w[:, dst:dst + hi - lo] for lo, hi, dst in _RUNS], axis=1)


W_IN_SHARD = N_IN // N_DEV


def _w_in_pieces():
    pieces = []
    for lo, hi, dst in _RUNS:
        for d in range(N_DEV):
            a, b = max(lo, d * W_IN_SHARD), min(hi, (d + 1) * W_IN_SHARD)
            if a < b:
                pieces.append((d, a - d * W_IN_SHARD, b - d * W_IN_SHARD, dst + a - lo))
    return pieces


def _w_in_from_shards(g, *, name):
    R = g.shape[1]
    tr = 256
    covered = sorted((dst, dst + b - a) for _, a, b, dst in _w_in_pieces())

    def body(g_ref, o_ref):
        at = 0
        for lo, hi in covered + [(NP_IN, NP_IN)]:
            if lo > at:
                o_ref[:, at:lo] = jnp.zeros((tr, lo - at), BF16)
            at = max(at, hi)
        for d, a, b, dst in _w_in_pieces():
            o_ref[:, dst:dst + b - a] = g_ref[d, :, a:b]

    return pl.pallas_call(
        body, name=name, grid=(R // tr,),
        in_specs=[pl.BlockSpec((N_DEV, tr, W_IN_SHARD), lambda i: (0, i, 0))],
        out_specs=pl.BlockSpec((tr, NP_IN), lambda i: (i, 0)),
        out_shape=jax.ShapeDtypeStruct((R, NP_IN), BF16),
        compiler_params=_cparams(("parallel",)),
    )(g)


def _w_in_to_shards(dw, *, name):
    R = dw.shape[0]
    tr = 256

    def body(w_ref, o_ref):
        for d, a, b, dst in _w_in_pieces():
            o_ref[d, :, a:b] = w_ref[:, dst:dst + b - a]

    return pl.pallas_call(
        body, name=name, grid=(R // tr,),
        in_specs=[pl.BlockSpec((tr, NP_IN), lambda i: (i, 0))],
        out_specs=pl.BlockSpec((N_DEV, tr, W_IN_SHARD), lambda i: (0, i, 0)),
        out_shape=jax.ShapeDtypeStruct((N_DEV, R, W_IN_SHARD), BF16),
        compiler_params=_cparams(("parallel",)),
    )(dw)


_SMALL = (("norm_g", DEPTH * D_MODEL), ("b_f", DEPTH * HEADS), ("g_cq", DEPTH * MLA_QLORA),
          ("g_ckv", DEPTH * MLA_KVLORA), ("sinks", DEPTH * HEADS), ("final_g", D_MODEL), ("loss", 1),
          ("meta", N_META * D_MODEL))
SMALL_ROWS = 168


def _pack_small(d):
    parts = []
    for name, size in _SMALL:
        padded = -(-size // 128) * 128
        v = d[name].reshape(-1).astype(F32) if name in d else jnp.zeros((size,), F32)
        parts.append(jnp.pad(v, (0, padded - size)))
    flat = jnp.concatenate(parts)
    return jnp.pad(flat, (0, SMALL_ROWS * 128 - flat.shape[0])).reshape(SMALL_ROWS, 128)


def _unpack_small(p, shapes):
    flat = p.reshape(-1)
    out, at = {}, 0
    for name, size in _SMALL:
        if name in shapes:
            out[name] = flat[at:at + size].reshape(shapes[name])
        at += -(-size // 128) * 128
    return out


def kernel(x, meta_tokens, norm_g, w_in, b_f, g_cq, g_ckv, w_uq, w_ukv, sinks, w_branch, w_out, final_g, loss_target, m_meta_tokens, m_norm_g, m_w_in, m_b_f, m_g_cq, m_g_ckv, m_w_uq, m_w_ukv, m_sinks, m_w_branch, m_w_out, m_final_g, v_meta_tokens, v_norm_g, v_w_in, v_b_f, v_g_cq, v_g_ckv, v_w_uq, v_w_ukv, v_sinks, v_w_branch, v_w_out, v_final_g):
    S = x.shape[1]
    L = BLK + S
    cx, cy, cc = _me()
    my_idx = 4 * cx + 2 * cy + cc

    def shards(l):
        return [t[l].astype(BF16) for t in (w_in, w_uq, w_ukv, w_branch, w_out)]

    def small_weights(l):
        return dict(norm_g=norm_g[l], b_f=b_f[l], g_cq=g_cq[l], g_ckv=g_ckv[l], sinks=sinks[l])

    def rest_weights(l):
        def assemble(gw_uq, gw_ukv, gw_br, gw_out):
            br = _join_shards(gw_br.reshape(N_DEV, N_BRANCH * BRANCH_W, D_MODEL // N_DEV), name=f"l{l}_w_branch_full")
            return dict(w_uq=_join_shards(gw_uq, name=f"l{l}_w_uq_full"),
                        w_ukv=_join_shards(gw_ukv, name=f"l{l}_w_ukv_full"),
                        w_branch=br.reshape(N_BRANCH, BRANCH_W, D_MODEL), w_out=gw_out.reshape(D_MODEL, D_MODEL))
        return assemble

    def layer_weights(l, gw_in, *gw_rest):
        return dict(small_weights(l), w_in=_w_in_from_shards(gw_in, name=f"l{l}_w_in_full"),
                    **rest_weights(l)(*gw_rest))

    gw_in0, g_meta = _comm(_Gather([shards(0)[0], meta_tokens]), name="gather_l0")
    layers = [dict(small_weights(0), w_in=_w_in_from_shards(gw_in0, name="l0_w_in_full"),
                   late=(_Gather(shards(0)[1:]), rest_weights(0))), None]
    meta_full = _cols_from_shards(g_meta)

    h = jnp.concatenate([jnp.zeros((PAD, D_MODEL), F32), meta_full, x[0]], axis=0)
    q_rotary = ((jnp.arange(HEADS * DQK_B) % DQK_B) >= MLA_NOPE).astype(F32)[None, :]
    tabs = dict(c=_rope_tables(L, DH // 2, LANES), bk=_rope_tables(L, MLA_ROPE // 2, LANES), bq=q_rotary)
    saved = [None] * DEPTH
    h, saved[0], gw1, layers[0] = _layer_fwd(h, layers[0], tabs, 0, side=_Gather(shards(1)))
    layers[1] = layer_weights(1, *gw1)
    h, saved[1], _, _ = _layer_fwd(h, layers[1], tabs, 1)
    loss_vec, dh, g_final = _loss_head(h, final_g, loss_target[0], name="loss_head")

    grads = [None] * DEPTH
    dh, grads[1], _, _ = _layer_bwd(dh, saved[1], layers[1], tabs, 1)
    dh, grads[0], recv1, recv0 = _layer_bwd(dh, saved[0], layers[0], tabs, 0, side=_Exchange(_grad_chunks(grads[1], "l1")),
                                            own_exchange=True)
    r_in, r_uq, r_ukv, r_br, r_out = (jnp.stack([a, b], axis=1) for a, b in zip(recv0, recv1))

    def stack(name):
        return jnp.stack([grads[l][name] for l in range(DEPTH)])

    small = _pack_small(dict(norm_g=stack("norm_g"), b_f=stack("b_f"), g_cq=stack("g_cq"), g_ckv=stack("g_ckv"),
                             sinks=stack("sinks"), final_g=g_final, loss=loss_vec[0, 0:1],
                             meta=dh[PAD:BLK]))
    (g_small,) = _comm(_Gather([small]), name="gather_small")

    def adam_big(w_, parts, m_, v_, name):
        shape = w_.shape
        C = shape[-1]
        R = math.prod(shape[:-1])
        outs = _adamw(w_.reshape(R, C), parts.reshape(parts.shape[0], R, C), m_.reshape(R, C), v_.reshape(R, C),
                      name=name)
        return [o.reshape(shape) for o in outs]

    res = {}
    res["w_in"] = adam_big(w_in, r_in, m_w_in, v_w_in, "adam_w_in")
    res["w_uq"] = adam_big(w_uq, r_uq, m_w_uq, v_w_uq, "adam_w_uq")
    res["w_ukv"] = adam_big(w_ukv, r_ukv, m_w_ukv, v_w_ukv, "adam_w_ukv")
    res["w_branch"] = adam_big(w_branch, r_br, m_w_branch, v_w_branch, "adam_w_branch")
    res["w_out"] = adam_big(w_out, r_out, m_w_out, v_w_out, "adam_w_out")

    small_w = dict(norm_g=norm_g, b_f=b_f, g_cq=g_cq, g_ckv=g_ckv, sinks=sinks, final_g=final_g)
    small_m = dict(norm_g=m_norm_g, b_f=m_b_f, g_cq=m_g_cq, g_ckv=m_g_ckv, sinks=m_sinks, final_g=m_final_g)
    small_v = dict(norm_g=v_norm_g, b_f=v_b_f, g_cq=v_g_cq, g_ckv=v_g_ckv, sinks=v_sinks, final_g=v_final_g)
    sm = _adamw(_pack_small(small_w), g_small, _pack_small(small_m), _pack_small(small_v), name="adam_small")
    shapes = {k: a.shape for k, a in small_w.items()}
    shapes_all = dict(shapes, loss=(), meta=(N_META, D_MODEL))
    sm_g = _unpack_small(sm[0], shapes_all)
    sm_d, sm_m, sm_v = (_unpack_small(t, shapes) for t in sm[1:])
    for k in shapes:
        res[k] = [sm_g[k], sm_d[k], sm_m[k], sm_v[k]]
    g_meta_mine = lax.dynamic_slice(sm_g["meta"], (0, my_idx * 128), (N_META, 128))
    res["meta_tokens"] = _adamw(meta_tokens, g_meta_mine[None], m_meta_tokens, v_meta_tokens, name="adam_meta")

    order = ["meta_tokens", "norm_g", "w_in", "b_f", "g_cq", "g_ckv", "w_uq", "w_ukv", "sinks", "w_branch", "w_out",
             "final_g"]
    grad_x = dh[BLK:][None]
    return (sm_g["loss"], grad_x, *[res[k][0] for k in order], *[res[k][1] for k in order],
            *[res[k][2] for k in order], *[res[k][3] for k in order])
```

```python
import functools
import math

import jax
import jax.numpy as jnp
from jax import lax
from jax.experimental import pallas as pl
from jax.experimental.pallas import tpu as pltpu

F32 = jnp.float32
BF16 = jnp.bfloat16

D_MODEL = 1024
DEPTH = 2
N_META = 16
BLK = 128
PAD = BLK - N_META
ROPE_THETA = 10000.0
EPS = 1e-6
NEG = -1e30
BIG = 1e30
HEADS = 8
DH = 64
MLA_NOPE = 64
MLA_ROPE = 32
MLA_QLORA = 384
MLA_KVLORA = 256
SWA_KV_HEADS = 2
WINDOW = 128
BRANCH_W = 512
N_BRANCH = 3
N_IN = 7592

ADAM_LR = 0.001
ADAM_B1 = 0.9
ADAM_B2 = 0.999
ADAM_EPS = 1e-08
ADAM_WD = 0.01
ADAM_STEP = 10

N_DEV = 8
MESH = pl.DeviceIdType.MESH

NP_IN = 8192
QKVA, ZA, FA = 0, 1536, 2048
CKV, KR, CQ = 2304, 2560, 2688
GATES = 3072
ZB = 6144
QKVC, ZC = 6912, 7680
_RUNS = ((0, 1536, QKVA), (1536, 1544, FA), (1544, 2056, ZA), (2056, 2440, CQ), (2440, 2696, CKV), (2696, 2728, KR),
         (2728, 3240, ZB), (3240, 4008, QKVC), (4008, 4520, ZC), (4520, 7592, GATES))

VMEM_LIMIT = 48 * 1024 * 1024
ATT_T = 384
ATT_HP = 1
ROW_T = 384
LANES = 128


def _pick(dim, prefs):
    for p in prefs:
        if dim % p == 0:
            return p
    return dim


def _cparams(sem):
    return pltpu.CompilerParams(dimension_semantics=sem, vmem_limit_bytes=VMEM_LIMIT)


def _side_cparams(side, sem):
    if side is None:
        return _cparams(sem)
    return pltpu.CompilerParams(dimension_semantics=("arbitrary",) * len(sem), vmem_limit_bytes=VMEM_LIMIT,
                                has_side_effects=True)


def _mm(a, b, *, ta=False, tb=False, add=None, out_dtype=F32, side=None, name):
    M = a.shape[1] if ta else a.shape[0]
    K = a.shape[0] if ta else a.shape[1]
    N = b.shape[0] if tb else b.shape[1]
    assert K == (b.shape[1] if tb else b.shape[0])
    tm = _pick(M, (704, 1024, 512, 384, 256, 128))
    tn = _pick(N, (1024, 768, 512, 384, 256, 128))
    tk = _pick(K, (2048, 1408, 1024, 768, 512, 384, 256, 128))
    nk = K // tk
    dims = (((0 if ta else 1,), (1 if tb else 0,)), ((), ()))

    sn = 0 if side is None else side.n
    n_in = 2 + (add is not None)
    grid = (M // tm, N // tn, nk)

    def body(*refs):
        a_ref, b_ref = refs[:2]
        c_ref = refs[2] if add is not None else None
        o_ref = refs[n_in + sn]
        scratch = refs[n_in + 2 * sn + 1:]
        if side is not None:
            step = (pl.program_id(0) * grid[1] + pl.program_id(1)) * nk + pl.program_id(2)
            _ride(side, [*refs[n_in:n_in + sn], *refs[n_in + sn + 1:n_in + 2 * sn + 1], *scratch[nk > 1:]],
                  step, grid[0] * grid[1] * nk)
        r = lax.dot_general(a_ref[...].astype(BF16), b_ref[...].astype(BF16), dims, preferred_element_type=F32)

        def finish(total):
            if c_ref is not None:
                total = total + c_ref[...]
            o_ref[...] = total.astype(out_dtype)

        if nk == 1:
            finish(r)
        else:
            acc = scratch[0]
            k = pl.program_id(2)

            @pl.when(k == 0)
            def _():
                acc[...] = r

            @pl.when(k > 0)
            def _():
                acc[...] += r

            @pl.when(k == nk - 1)
            def _():
                finish(acc[...])

    a_spec = pl.BlockSpec((tk, tm), lambda i, j, k: (k, i)) if ta else pl.BlockSpec((tm, tk), lambda i, j, k: (i, k))
    b_spec = pl.BlockSpec((tn, tk), lambda i, j, k: (j, k)) if tb else pl.BlockSpec((tk, tn), lambda i, j, k: (k, j))
    o_spec = pl.BlockSpec((tm, tn), lambda i, j, k: (i, j))
    any_spec = pl.BlockSpec(memory_space=pl.ANY)
    outs = pl.pallas_call(
        body, name=name,
        grid=grid,
        in_specs=[a_spec, b_spec] + ([o_spec] if add is not None else []) + [any_spec] * sn,
        out_specs=[o_spec] + [any_spec] * sn,
        out_shape=[jax.ShapeDtypeStruct((M, N), out_dtype)] + ([] if side is None else side.out_shape),
        scratch_shapes=([pltpu.VMEM((tm, tn), F32)] if nk > 1 else []) + ([] if side is None else side.scratch),
        compiler_params=_side_cparams(side, ("parallel", "parallel", "arbitrary")),
    )(*((a, b) if add is None else (a, b, add)), *([] if side is None else side.arrs))
    return outs[0] if side is None else outs


def _col_spec(tm, width, col):
    assert col % width == 0
    return pl.BlockSpec((tm, width), lambda i, _c=col // width: (i, _c))


def _rms_fwd(x, g, *, col=0, name):
    L = x.shape[0]
    D = g.shape[0]
    tm = ROW_T

    def body(x_ref, g_ref, y_ref):
        xv = x_ref[...]
        rstd = lax.rsqrt(jnp.mean(xv * xv, axis=-1, keepdims=True) + EPS)
        y_ref[...] = (xv * rstd * g_ref[...]).astype(BF16)

    return pl.pallas_call(
        body, name=name, grid=(L // tm,),
        in_specs=[_col_spec(tm, D, col), pl.BlockSpec((1, D), lambda i: (0, 0))],
        out_specs=pl.BlockSpec((tm, D), lambda i: (i, 0)),
        out_shape=jax.ShapeDtypeStruct((L, D), BF16),
        compiler_params=_cparams(("parallel",)),
    )(x, g.reshape(1, D))


def _rms_bwd(dy, x, g, *, col=0, add=None, into=None, name):
    L = x.shape[0]
    D = g.shape[0]
    tm = ROW_T

    def body(*refs):
        dy_ref, x_ref, g_ref = refs[:3]
        add_ref = refs[3] if add is not None else None
        dx_ref, dg_ref = refs[-2:]
        i = pl.program_id(0)
        xv = x_ref[...]
        dyv = dy_ref[...]
        rstd = lax.rsqrt(jnp.mean(xv * xv, axis=-1, keepdims=True) + EPS)
        xhat = xv * rstd
        part = jnp.sum(dyv * xhat, axis=0, keepdims=True)

        @pl.when(i == 0)
        def _():
            dg_ref[...] = part

        @pl.when(i > 0)
        def _():
            dg_ref[...] += part

        dxh = dyv * g_ref[...]
        dx = rstd * (dxh - xhat * jnp.mean(dxh * xhat, axis=-1, keepdims=True))
        if add_ref is not None:
            dx = dx + add_ref[...]
        dx_ref[...] = dx.astype(dx_ref.dtype)

    row = pl.BlockSpec((tm, D), lambda i: (i, 0))
    in_specs = [row, _col_spec(tm, D, col), pl.BlockSpec((1, D), lambda i: (0, 0))]
    args = [dy, x, g.reshape(1, D)]
    aliases = {}
    if add is not None:
        in_specs.append(row)
        args.append(add)
    if into is not None:
        in_specs.append(pl.BlockSpec(memory_space=pl.ANY))
        args.append(into)
        aliases = {len(args) - 1: 0}
        dx_spec, dx_shape = _col_spec(tm, D, col), jax.ShapeDtypeStruct(into.shape, into.dtype)
    else:
        dx_spec, dx_shape = row, jax.ShapeDtypeStruct((L, D), F32)
    dx, dg = pl.pallas_call(
        body, name=name, grid=(L // tm,),
        in_specs=in_specs,
        out_specs=[dx_spec, pl.BlockSpec((1, D), lambda i: (0, 0))],
        out_shape=[dx_shape, jax.ShapeDtypeStruct((1, D), F32)],
        input_output_aliases=aliases,
        compiler_params=_cparams(("arbitrary",)),
    )(*args)
    return dx, dg.reshape(D)


def _loss_head(h, g, target, *, name):
    L, D = h.shape
    nb = L // BLK

    def body(h_ref, g_ref, t_ref, loss_ref, dh_ref, dg_ref):
        i = pl.program_id(0)

        @pl.when(i == 0)
        def _():
            loss_ref[...] = jnp.zeros_like(loss_ref)
            dg_ref[...] = jnp.zeros_like(dg_ref)
            dh_ref[...] = jnp.zeros_like(dh_ref)

        @pl.when(i > 0)
        def _():
            xv = h_ref[...]
            gv = g_ref[...]
            rstd = lax.rsqrt(jnp.mean(xv * xv, axis=-1, keepdims=True) + EPS)
            xhat = xv * rstd
            err = xhat * gv - t_ref[...]
            row = jnp.mean(err * err, axis=-1, keepdims=True)
            loss_ref[...] += 0.5 * jnp.sum(row, axis=0, keepdims=True)
            dy = err * (1.0 / D)
            dg_ref[...] += jnp.sum(dy * xhat, axis=0, keepdims=True)
            dxh = dy * gv
            dh_ref[...] = rstd * (dxh - xhat * jnp.mean(dxh * xhat, axis=-1, keepdims=True))

    loss, dh, dg = pl.pallas_call(
        body, name=name, grid=(nb,),
        in_specs=[pl.BlockSpec((BLK, D), lambda i: (i, 0)), pl.BlockSpec((1, D), lambda i: (0, 0)),
                  pl.BlockSpec((BLK, D), lambda i: (jnp.maximum(i - 1, 0), 0))],
        out_specs=[pl.BlockSpec((1, 128), lambda i: (0, 0)), pl.BlockSpec((BLK, D), lambda i: (i, 0)),
                   pl.BlockSpec((1, D), lambda i: (0, 0))],
        out_shape=[jax.ShapeDtypeStruct((1, 128), F32), jax.ShapeDtypeStruct((L, D), F32),
                   jax.ShapeDtypeStruct((1, D), F32)],
        compiler_params=_cparams(("arbitrary",)),
    )(h, g.reshape(1, D), target)
    return loss, dh, dg.reshape(D)


_NT = (((1,), (1,)), ((), ()))
_TN = (((0,), (0,)), ((), ()))


def _attn_fwd(q, k, v, kbias, *, T, window=False, sink=None, side=None, name):
    sn = 0 if side is None else side.n
    H, L, dk = q.shape
    Hkv = k.shape[0]
    dv = v.shape[2]
    G = H // Hkv
    nt = L // T
    Hb = kbias.shape[0]
    reps = T // LANES
    assert not window or T == WINDOW
    HP = G if G > 1 else ATT_HP
    NS = 1 if G > 1 else HP
    R = HP * T // NS
    HKV = HP // G
    HB = HP if Hb > 1 else 1
    assert G == 1 or Hb == 1

    def body(*refs):
        q_ref, k_ref, v_ref, kb_ref = refs[:4]
        n = 4
        if sink is not None:
            sk_ref = refs[n]
            n += 1
        side_in = refs[n:n + sn]
        n += sn
        o_ref, lse_ref = refs[n:n + 2]
        side_out = refs[n + 2:n + 2 + sn]
        n += 2 + sn
        m_scs, l_scs, acc_scs, buf_a, buf_b = (refs[n + t * NS:n + (t + 1) * NS] for t in range(5))
        i = pl.program_id(1)
        if side is not None:
            _ride(side, [*side_in, *side_out, *refs[n + 5 * NS:]], pl.program_id(0) * nt + i, (H // HP) * nt)
        for a in range(NS):
            if sink is not None:
                sk = [jnp.broadcast_to(sk_ref[b, :, 0:1], (T, LANES)) for b in range(HP)]
                m_scs[a][...] = jnp.concatenate(sk, axis=0) if G > 1 else sk[a]
                l_scs[a][...] = jnp.ones((R, LANES), F32)
            else:
                m_scs[a][...] = jnp.full((R, LANES), NEG, F32)
                l_scs[a][...] = jnp.zeros((R, LANES), F32)
            acc_scs[a][...] = jnp.zeros((R, dv), F32)
        row = lax.broadcasted_iota(jnp.int32, (R, T), 0) & (T - 1) if G > 1 else \
            lax.broadcasted_iota(jnp.int32, (R, T), 0)
        col = lax.broadcasted_iota(jnp.int32, (R, T), 1)

        def logits(a, j):
            rows = pl.ds(pl.multiple_of(j * T, T), T)
            qv = q_ref[...].reshape(R, dk) if G > 1 else q_ref[a]
            s = lax.dot_general(qv, k_ref[a, rows, :], _NT, preferred_element_type=F32)
            return s - kb_ref[a if HB > 1 else 0, j]

        def update(a, s, j, kind):
            rows = pl.ds(pl.multiple_of(j * T, T), T)
            m_sc, l_sc, acc_sc = m_scs[a], l_scs[a], acc_scs[a]
            if kind == "diag":
                s = jnp.where(row >= col, s, NEG)
            elif kind == "prev":
                s = jnp.where(col > row, s, NEG)
            m_prev = m_sc[...]
            m_new = jnp.maximum(m_prev, jnp.max(s, axis=1, keepdims=True))
            alpha = jnp.exp(m_prev - m_new)
            p = jnp.exp(s - jnp.tile(m_new, (1, reps)))
            l_sc[...] = alpha * l_sc[...] + jnp.sum(p, axis=1, keepdims=True)
            acc_sc[...] = alpha[:, :dv] * acc_sc[...] + jnp.dot(p.astype(BF16), v_ref[a, rows, :],
                                                                preferred_element_type=F32)
            m_sc[...] = m_new

        if window:
            @pl.when(i > 0)
            def _():
                for a in range(NS):
                    update(a, logits(a, i - 1), i - 1, "prev")

            for a in range(NS):
                update(a, logits(a, i), i, "diag")
        else:
            def fill(buf, j):
                for a in range(NS):
                    buf[a][...] = logits(a, j)

            def drain(buf, j, kind):
                for a in range(NS):
                    update(a, buf[a][...], j, kind)

            fill(buf_a, 0)

            def pair(t, c):
                fill(buf_b, 2 * t + 1)
                drain(buf_a, 2 * t, "full")
                fill(buf_a, 2 * t + 2)
                drain(buf_b, 2 * t + 1, "full")
                return c

            lax.fori_loop(0, i // 2, pair, 0)

            @pl.when(i % 2 == 1)
            def _():
                fill(buf_b, i)
                drain(buf_a, i - 1, "full")
                drain(buf_b, i, "diag")

            @pl.when(i % 2 == 0)
            def _():
                drain(buf_a, i, "diag")

        for a in range(NS):
            lv = l_scs[a][...]
            ov = acc_scs[a][...] / lv[:, :dv]
            lsev = (m_scs[a][...] + jnp.log(lv))[:, 0:1]
            if G > 1:
                o_ref[...] = ov.reshape(HP, T, dv)
                lse_ref[...] = lsev.reshape(HP, T, 1)
            else:
                o_ref[a] = ov
                lse_ref[a] = lsev

    in_specs = [pl.BlockSpec((HP, T, dk), lambda h, i: (h, i, 0)),
                pl.BlockSpec((HKV, L, dk), lambda h, i: (h, 0, 0)),
                pl.BlockSpec((HKV, L, dv), lambda h, i: (h, 0, 0)),
                pl.BlockSpec((HB, nt, 1, T), lambda h, i: (h if Hb > 1 else 0, 0, 0, 0))]
    args = [q, k, v, kbias]
    if sink is not None:
        in_specs += [pl.BlockSpec((HP, 1, LANES), lambda h, i: (h, 0, 0))]
        args += [sink]
    any_spec = pl.BlockSpec(memory_space=pl.ANY)
    return pl.pallas_call(
        body, name=name, grid=(H // HP, nt),
        in_specs=in_specs + [any_spec] * sn,
        out_specs=[pl.BlockSpec((HP, T, dv), lambda h, i: (h, i, 0)),
                   pl.BlockSpec((HP, T, 1), lambda h, i: (h, i, 0))] + [any_spec] * sn,
        out_shape=[jax.ShapeDtypeStruct((H, L, dv), F32), jax.ShapeDtypeStruct((H, L, 1), F32)]
        + ([] if side is None else side.out_shape),
        scratch_shapes=[pltpu.VMEM((R, LANES), F32)] * (2 * NS) + [pltpu.VMEM((R, dv), F32)] * NS
        + [pltpu.VMEM((R, T), F32)] * (2 * NS) + ([] if side is None else side.scratch),
        compiler_params=_side_cparams(side, ("parallel", "arbitrary")),
    )(*args, *([] if side is None else side.arrs))


def _attn_bwd_t(q, k, v, kbias_col, lse_row, delta_row, do, *, T, fox=False, side=None, name):
    sn = 0 if side is None else side.n
    H, L, dk = q.shape
    dv = v.shape[2]
    nt = L // T
    Hb = kbias_col.shape[0]

    def body(*refs):
        q_ref, k_ref, v_ref, kb_ref, lse_ref, dl_ref, do_ref = refs[:7]
        side_in = refs[7:7 + sn]
        n = 7 + sn
        dq_ref, dk_ref, dv_ref = refs[n:n + 3]
        n += 3
        if fox:
            dcq_ref, dck_ref = refs[n:n + 2]
            n += 2
        side_out = refs[n:n + sn]
        n += sn
        buf_a, buf_b = refs[n:n + 2], refs[n + 2:n + 4]
        j = pl.program_id(1)
        if side is not None:
            _ride(side, [*side_in, *side_out, *refs[n + 4:]], pl.program_id(0) * nt + j, H * nt)

        @pl.when(j == 0)
        def _():
            dq_ref[...] = jnp.zeros_like(dq_ref)
            if fox:
                dcq_ref[...] = jnp.zeros_like(dcq_ref)

        dk_ref[...] = jnp.zeros_like(dk_ref)
        dv_ref[...] = jnp.zeros_like(dv_ref)
        if fox:
            dck_ref[...] = jnp.zeros_like(dck_ref)
        kb = k_ref[0]
        vb = v_ref[0]
        kbias_j = kb_ref[0]
        key = lax.broadcasted_iota(jnp.int32, (T, T), 0)
        qry = lax.broadcasted_iota(jnp.int32, (T, T), 1)

        def fill(buf, i):
            rows = pl.ds(pl.multiple_of(i * T, T), T)
            buf[0][...] = lax.dot_general(kb, q_ref[0, rows, :], _NT, preferred_element_type=F32) - kbias_j
            buf[1][...] = lax.dot_general(vb, do_ref[0, rows, :], _NT, preferred_element_type=F32)

        def drain(buf, i, kind):
            rows = pl.ds(pl.multiple_of(i * T, T), T)
            st = buf[0][...]
            if kind == "diag":
                st = jnp.where(key <= qry, st, NEG)
            pt = jnp.exp(st - lse_ref[0, i])
            dst = pt * (buf[1][...] - dl_ref[0, i])
            dsb = dst.astype(BF16)
            dv_ref[0] += jnp.dot(pt.astype(BF16), do_ref[0, rows, :], preferred_element_type=F32)
            dk_ref[0] += jnp.dot(dsb, q_ref[0, rows, :], preferred_element_type=F32)
            dq_ref[0, rows, :] += lax.dot_general(dsb, kb, _TN, preferred_element_type=F32)
            if fox:
                dcq_ref[0, i] += jnp.sum(dst, axis=0, keepdims=True)
                dck_ref[0] += -jnp.sum(dst, axis=1, keepdims=True)

        first = j + 1
        rest = nt - first
        fill(buf_a, j)
        fill(buf_b, jnp.minimum(first, nt - 1))
        drain(buf_a, j, "diag")

        def pair(t, c):
            i0 = first + 2 * t
            fill(buf_a, i0 + 1)
            drain(buf_b, i0, "full")
            fill(buf_b, jnp.minimum(i0 + 2, nt - 1))
            drain(buf_a, i0 + 1, "full")
            return c

        lax.fori_loop(0, rest // 2, pair, 0)

        @pl.when(rest % 2 == 1)
        def _():
            drain(buf_b, nt - 1, "full")

    rows_spec = pl.BlockSpec((1, nt, 1, T), lambda h, j: (h, 0, 0, 0))
    in_specs = [pl.BlockSpec((1, L, dk), lambda h, j: (h, 0, 0)),
                pl.BlockSpec((1, T, dk), lambda h, j: (h, j, 0)),
                pl.BlockSpec((1, T, dv), lambda h, j: (h, j, 0)),
                pl.BlockSpec((1, T, 1), lambda h, j: (h if Hb > 1 else 0, j, 0)),
                rows_spec, rows_spec,
                pl.BlockSpec((1, L, dv), lambda h, j: (h, 0, 0))]
    out_specs = [pl.BlockSpec((1, L, dk), lambda h, j: (h, 0, 0)),
                 pl.BlockSpec((1, T, dk), lambda h, j: (h, j, 0)),
                 pl.BlockSpec((1, T, dv), lambda h, j: (h, j, 0))]
    out_shape = [jax.ShapeDtypeStruct((H, L, dk), F32), jax.ShapeDtypeStruct((H, L, dk), F32),
                 jax.ShapeDtypeStruct((H, L, dv), F32)]
    if fox:
        out_specs += [rows_spec, pl.BlockSpec((1, T, 1), lambda h, j: (h, j, 0))]
        out_shape += [jax.ShapeDtypeStruct((H, nt, 1, T), F32), jax.ShapeDtypeStruct((H, L, 1), F32)]
    any_spec = pl.BlockSpec(memory_space=pl.ANY)
    outs = pl.pallas_call(
        body, name=name, grid=(H, nt),
        in_specs=in_specs + [any_spec] * sn, out_specs=out_specs + [any_spec] * sn,
        out_shape=out_shape + ([] if side is None else side.out_shape),
        scratch_shapes=[pltpu.VMEM((T, T), F32)] * 4 + ([] if side is None else side.scratch),
        compiler_params=_side_cparams(side, ("parallel", "arbitrary")),
    )(q, k, v, kbias_col, lse_row, delta_row, do, *([] if side is None else side.arrs))
    main, rest = outs[:len(outs) - sn], outs[len(outs) - sn:]
    return (*(main if fox else (*main, None, None)), *rest)


def _attn_bwd_window(q, k, v, kbias, o, lse, do, *, name):
    H, L, dk = q.shape
    Hkv = k.shape[0]
    dv = v.shape[2]
    G = H // Hkv
    T = WINDOW
    nt = L // T
    R = G * T

    def body(q_ref, kc_ref, kp_ref, vc_ref, vp_ref, kb_ref, o_ref, lse_ref, do_ref, dq_ref, dk_ref, dv_ref,
             dk_sc, dv_sc):
        i = pl.program_id(1)
        row = lax.broadcasted_iota(jnp.int32, (R, T), 0) & (T - 1)
        col = lax.broadcasted_iota(jnp.int32, (R, T), 1)

        @pl.when(i == 0)
        def _():
            dk_sc[...] = jnp.zeros_like(dk_sc)
            dv_sc[...] = jnp.zeros_like(dv_sc)

        @pl.when(i == nt)
        def _():
            dk_ref[0] = dk_sc[...]
            dv_ref[0] = dv_sc[...]

        @pl.when(i < nt)
        def _():
            qb = q_ref[...].reshape(R, dk)
            dof = do_ref[...].reshape(R, dv)
            dob = dof.astype(BF16)
            lse_c = lse_ref[...].reshape(R, 1)
            delta = jnp.sum(dof * o_ref[...].reshape(R, dv), axis=1, keepdims=True)

            def grads(kt, vt, kbias_j, mask):
                s = lax.dot_general(qb, kt, _NT, preferred_element_type=F32) - kbias_j
                p = jnp.exp(jnp.where(mask, s, NEG) - lse_c)
                dp = lax.dot_general(dob, vt, _NT, preferred_element_type=F32)
                ds = (p * (dp - delta)).astype(BF16)
                return (jnp.dot(ds, kt, preferred_element_type=F32),
                        lax.dot_general(ds, qb, _TN, preferred_element_type=F32),
                        lax.dot_general(p.astype(BF16), dob, _TN, preferred_element_type=F32))

            ip = jnp.maximum(i - 1, 0)
            dq_p, dk_p, dv_p = grads(kp_ref[0], vp_ref[0], kb_ref[0, ip], (col > row) & (i > 0))
            dq_c, dk_c, dv_c = grads(kc_ref[0], vc_ref[0], kb_ref[0, i], row >= col)
            dq_ref[...] = (dq_p + dq_c).reshape(G, T, dk)
            dk_ref[0] = dk_sc[...] + dk_p
            dv_ref[0] = dv_sc[...] + dv_p
            dk_sc[...] = dk_c
            dv_sc[...] = dv_c

    def cur(i):
        return jnp.minimum(i, nt - 1)

    def prev(i):
        return jnp.maximum(jnp.minimum(i, nt - 1) - 1, 0)

    def written(i):
        return jnp.maximum(i - 1, 0)

    qs = lambda d: pl.BlockSpec((G, T, d), lambda h, i: (h, cur(i), 0))
    return pl.pallas_call(
        body, name=name, grid=(Hkv, nt + 1),
        in_specs=[qs(dk),
                  pl.BlockSpec((1, T, dk), lambda h, i: (h, cur(i), 0)),
                  pl.BlockSpec((1, T, dk), lambda h, i: (h, prev(i), 0)),
                  pl.BlockSpec((1, T, dv), lambda h, i: (h, cur(i), 0)),
                  pl.BlockSpec((1, T, dv), lambda h, i: (h, prev(i), 0)),
                  pl.BlockSpec((1, nt, 1, T), lambda h, i: (0, 0, 0, 0)),
                  qs(dv), qs(1), qs(dv)],
        out_specs=[qs(dk),
                   pl.BlockSpec((1, T, dk), lambda h, i: (h, written(i), 0)),
                   pl.BlockSpec((1, T, dv), lambda h, i: (h, written(i), 0))],
        out_shape=[jax.ShapeDtypeStruct((H, L, dk), F32), jax.ShapeDtypeStruct((Hkv, L, dk), F32),
                   jax.ShapeDtypeStruct((Hkv, L, dv), F32)],
        scratch_shapes=[pltpu.VMEM((T, dk), F32), pltpu.VMEM((T, dv), F32)],
        compiler_params=_cparams(("parallel", "arbitrary")),
    )(q, k, k, v, v, kbias, o, lse, do)


def _adamw(w, gparts, m, v, *, name):
    n, R, C = gparts.shape
    tr = _pick(R, (128, 64, 32, 16, 8))
    c1 = 1.0 - ADAM_B1 ** ADAM_STEP
    c2 = 1.0 - ADAM_B2 ** ADAM_STEP

    def body(w_ref, g_ref, m_ref, v_ref, go_ref, d_ref, mo_ref, vo_ref):
        g = g_ref[0].astype(F32)
        for t in range(1, n):
            g = g + g_ref[t].astype(F32)
        mn = ADAM_B1 * m_ref[...] + (1.0 - ADAM_B1) * g
        vn = ADAM_B2 * v_ref[...] + (1.0 - ADAM_B2) * (g * g)
        go_ref[...] = g
        mo_ref[...] = mn
        vo_ref[...] = vn
        d_ref[...] = -ADAM_LR * ((mn / c1) / (jnp.sqrt(vn / c2) + ADAM_EPS) + ADAM_WD * w_ref[...])

    spec = pl.BlockSpec((tr, C), lambda i: (i, 0))
    return pl.pallas_call(
        body, name=name, grid=(R // tr,),
        in_specs=[spec, pl.BlockSpec((n, tr, C), lambda i: (0, i, 0)), spec, spec],
        out_specs=[spec] * 4,
        out_shape=[jax.ShapeDtypeStruct((R, C), F32)] * 4,
        compiler_params=_cparams(("parallel",)),
    )(w, gparts, m, v)


def _me():
    return lax.axis_index("x"), lax.axis_index("y"), lax.axis_index("c")


class _CommJob:
    def __init__(self, arrs):
        self.arrs = list(arrs)
        self.n = len(arrs)
        self.scratch = [pltpu.SemaphoreType.DMA((self.n, 7)), pltpu.SemaphoreType.DMA((self.n, 7)),
                        pltpu.SemaphoreType.DMA((self.n,))]

    def bind(self, refs):
        n = self.n
        self.ins, self.outs = refs[:n], refs[n:2 * n]
        self.send_sems, self.recv_sems, self.local_sems = refs[2 * n:2 * n + 3]

    def middle(self):
        pass


class _Gather(_CommJob):
    def __init__(self, arrs):
        super().__init__(arrs)
        self.out_shape = [jax.ShapeDtypeStruct((N_DEV, *a.shape), a.dtype) for a in arrs]

    def _where(self):
        x, y, c = _me()
        return (x, y, c), (x, y, 1 - c), [(1 - x, y), (x, 1 - y), (1 - x, 1 - y)], c

    def _copy(self, t, k, block, to, src=None):
        dst = self.outs[t].at[4 * block[0] + 2 * block[1] + block[2]]
        return pltpu.make_async_remote_copy(
            src_ref=dst if src is None else src, dst_ref=dst,
            send_sem=self.send_sems.at[t, k], recv_sem=self.recv_sems.at[t, k],
            device_id=to, device_id_type=MESH)

    def _mine(self, t, me):
        return pltpu.make_async_copy(self.ins[t], self.outs[t].at[4 * me[0] + 2 * me[1] + me[2]],
                                     self.local_sems.at[t])

    def _first(self, t, me, sibling, chips, c):
        return [self._copy(t, 0, me, sibling, src=self.ins[t])] + \
               [self._copy(t, 1 + j, me, (*chip, c), src=self.ins[t]) for j, chip in enumerate(chips)]

    def start(self):
        me, sibling, chips, c = self._where()
        for t in range(self.n):
            self._mine(t, me).start()
            for cp in self._first(t, me, sibling, chips, c):
                cp.start()

    def middle(self):
        me, sibling, chips, c = self._where()
        for j, chip in enumerate(chips):
            for t in range(self.n):
                self._copy(t, 1 + j, (*chip, c), me).wait_recv()
                self._copy(t, 4 + j, (*chip, c), sibling).start()

    def finish(self):
        me, sibling, chips, c = self._where()
        for t in range(self.n):
            self._copy(t, 0, sibling, me).wait_recv()
            for j, chip in enumerate(chips):
                self._copy(t, 4 + j, (*chip, 1 - c), me).wait_recv()
        for t in range(self.n):
            for cp in self._first(t, me, sibling, chips, c):
                cp.wait_send()
            for j, chip in enumerate(chips):
                self._copy(t, 4 + j, (*chip, c), sibling).wait_send()
            self._mine(t, me).wait()


class _Exchange(_CommJob):
    def __init__(self, arrs):
        super().__init__(arrs)
        self.out_shape = [jax.ShapeDtypeStruct(a.shape, a.dtype) for a in arrs]

    def _copies(self, t):
        x, y, c = _me()
        my_idx = 4 * x + 2 * y + c
        pairs = []
        for k in range(1, N_DEV):
            peer = (x ^ ((k >> 2) & 1), y ^ ((k >> 1) & 1), c ^ (k & 1))
            peer_idx = 4 * peer[0] + 2 * peer[1] + peer[2]
            sems = dict(send_sem=self.send_sems.at[t, k - 1], recv_sem=self.recv_sems.at[t, k - 1],
                        device_id=peer, device_id_type=MESH)
            pairs.append((pltpu.make_async_remote_copy(src_ref=self.ins[t].at[peer_idx],
                                                       dst_ref=self.outs[t].at[my_idx], **sems),
                          pltpu.make_async_remote_copy(src_ref=self.ins[t].at[peer_idx],
                                                       dst_ref=self.outs[t].at[peer_idx], **sems)))
        return pairs

    def _mine(self, t):
        x, y, c = _me()
        my_idx = 4 * x + 2 * y + c
        return pltpu.make_async_copy(self.ins[t].at[my_idx], self.outs[t].at[my_idx], self.local_sems.at[t])

    def start(self):
        for t in range(self.n):
            self._mine(t).start()
            for snd, _ in self._copies(t):
                snd.start()

    def finish(self):
        for t in range(self.n):
            pairs = self._copies(t)
            for _, rcv in pairs:
                rcv.wait_recv()
            for snd, _ in pairs:
                snd.wait_send()
            self._mine(t).wait()


def _comm(job, *, name):
    def body(*refs):
        job.bind(refs)
        job.start()
        job.middle()
        job.finish()

    any_spec = pl.BlockSpec(memory_space=pl.ANY)
    return pl.pallas_call(
        body, name=name,
        in_specs=[any_spec] * job.n, out_specs=[any_spec] * job.n,
        out_shape=job.out_shape, scratch_shapes=job.scratch,
        compiler_params=pltpu.CompilerParams(has_side_effects=True),
    )(*job.arrs)


def _ride(job, refs, step, total):
    job.bind(refs)

    @pl.when(step == 0)
    def _():
        job.start()

    @pl.when(step == (total * 3) // 5)
    def _():
        job.middle()

    @pl.when(step == total - 1)
    def _():
        job.finish()


SCALE_A = DH ** -0.5
SCALE_B = (MLA_NOPE + MLA_ROPE) ** -0.5
SCALE_C = DH ** -0.5
DQK_B = MLA_NOPE + MLA_ROPE


def _rope_tables(L, half, width):
    pos = (jnp.arange(L) - PAD).astype(F32)
    lane = jnp.arange(width)
    inv = ROPE_THETA ** (-(lane % half).astype(F32) / half)
    ang = pos[:, None] * inv[None, :]
    sign = jnp.where(lane % (2 * half) < half, -1.0, 1.0).astype(F32)
    return jnp.cos(ang), jnp.sin(ang) * sign[None, :]


def _rope_lanes(x, cos, sin, half):
    W = x.shape[1]
    lane = lax.broadcasted_iota(jnp.int32, x.shape, 1)
    first = (lane & (2 * half - 1)) < half
    partner = jnp.where(first, pltpu.roll(x, W - half, 1), pltpu.roll(x, half, 1))
    return x * cos + partner * sin


def _tile_lanes(t, width):
    return t if t.shape[1] == width else jnp.tile(t, (1, width // t.shape[1]))


def _split(x, H, d, dst):
    for h in range(H):
        dst[h] = x[:, d * h:d * (h + 1)].astype(dst.dtype)


def _join(src, H):
    return jnp.concatenate([src[h] for h in range(H)], axis=1)


def _head_spec(H, tm, d):
    return pl.BlockSpec((H, tm, d), lambda i: (0, i, 0))


def _prep_a(proj, *, name):
    L = proj.shape[0]
    tm = ROW_T

    def body(x_ref, qo, ko, vo):
        _split(x_ref[:, 0:512] * SCALE_A, HEADS, DH, qo)
        _split(x_ref[:, 512:1024], HEADS, DH, ko)
        _split(x_ref[:, 1024:1536], HEADS, DH, vo)

    return pl.pallas_call(
        body, name=name, grid=(L // tm,),
        in_specs=[_col_spec(tm, 1536, QKVA)],
        out_specs=[_head_spec(HEADS, tm, DH)] * 3,
        out_shape=[jax.ShapeDtypeStruct((HEADS, L, DH), BF16)] * 3,
        compiler_params=_cparams(("parallel",)),
    )(proj)


def _unprep_a(dq, dk, dv, dproj, *, name):
    L = dq.shape[1]
    tm = ROW_T

    def body(dq_ref, dk_ref, dv_ref, _, dp_ref):
        dp_ref[:, 0:512] = (_join(dq_ref, HEADS) * SCALE_A).astype(BF16)
        dp_ref[:, 512:1024] = _join(dk_ref, HEADS).astype(BF16)
        dp_ref[:, 1024:1536] = _join(dv_ref, HEADS).astype(BF16)

    hs = _head_spec(HEADS, tm, DH)
    return pl.pallas_call(
        body, name=name, grid=(L // tm,),
        in_specs=[hs, hs, hs, pl.BlockSpec(memory_space=pl.ANY)],
        out_specs=_col_spec(tm, 1536, QKVA),
        out_shape=jax.ShapeDtypeStruct(dproj.shape, dproj.dtype),
        input_output_aliases={3: 0},
        compiler_params=_cparams(("parallel",)),
    )(dq, dk, dv, dproj)


def _prep_c(proj, tab, *, name):
    L = proj.shape[0]
    tm = ROW_T

    def body(x_ref, cos_ref, sin_ref, qo, ko, vo):
        cos, sin = cos_ref[...], sin_ref[...]
        q = _rope_lanes(x_ref[:, 0:512], _tile_lanes(cos, 512), _tile_lanes(sin, 512), DH // 2)
        _split(q * SCALE_C, HEADS, DH, qo)
        _split(_rope_lanes(x_ref[:, 512:640], cos, sin, DH // 2), SWA_KV_HEADS, DH, ko)
        _split(x_ref[:, 640:768], SWA_KV_HEADS, DH, vo)

    t128 = pl.BlockSpec((tm, LANES), lambda i: (i, 0))
    return pl.pallas_call(
        body, name=name, grid=(L // tm,),
        in_specs=[_col_spec(tm, 768, QKVC), t128, t128],
        out_specs=[_head_spec(HEADS, tm, DH), _head_spec(SWA_KV_HEADS, tm, DH), _head_spec(SWA_KV_HEADS, tm, DH)],
        out_shape=[jax.ShapeDtypeStruct((HEADS, L, DH), BF16), jax.ShapeDtypeStruct((SWA_KV_HEADS, L, DH), BF16),
                   jax.ShapeDtypeStruct((SWA_KV_HEADS, L, DH), BF16)],
        compiler_params=_cparams(("parallel",)),
    )(proj, *tab)


def _unprep_c(dq, dk, dv, tab, dproj, *, name):
    L = dq.shape[1]
    tm = ROW_T

    def body(dq_ref, dk_ref, dv_ref, cos_ref, sin_ref, _, dp_ref):
        cos, nsin = cos_ref[...], -sin_ref[...]
        dqv = _join(dq_ref, HEADS) * SCALE_C
        dp_ref[:, 0:512] = _rope_lanes(dqv, _tile_lanes(cos, 512), _tile_lanes(nsin, 512), DH // 2).astype(BF16)
        dp_ref[:, 512:640] = _rope_lanes(_join(dk_ref, SWA_KV_HEADS), cos, nsin, DH // 2).astype(BF16)
        dp_ref[:, 640:768] = _join(dv_ref, SWA_KV_HEADS).astype(BF16)

    hs = _head_spec(HEADS, tm, DH)
    hkv = _head_spec(SWA_KV_HEADS, tm, DH)
    t128 = pl.BlockSpec((tm, LANES), lambda i: (i, 0))
    return pl.pallas_call(
        body, name=name, grid=(L // tm,),
        in_specs=[hs, hkv, hkv, t128, t128, pl.BlockSpec(memory_space=pl.ANY)],
        out_specs=_col_spec(tm, 768, QKVC),
        out_shape=jax.ShapeDtypeStruct(dproj.shape, dproj.dtype),
        input_output_aliases={5: 0},
        compiler_params=_cparams(("parallel",)),
    )(dq, dk, dv, *tab, dproj)


def _q_tables(cos, sin, on_ref):
    on = on_ref[...] > 0.5
    width = on_ref.shape[1]
    return jnp.where(on, _tile_lanes(cos, width), 1.0), jnp.where(on, _tile_lanes(sin, width), 0.0)


def _prep_b(qbm, kvbm, proj, tab_k, q_rotary, *, name):
    L = proj.shape[0]
    tm = ROW_T

    def body(q_ref, kv_ref, kr_ref, ck_ref, sk_ref, on_ref, qo, ko, vo):
        cq, sq = _q_tables(ck_ref[...], sk_ref[...], on_ref)
        q = _rope_lanes(q_ref[...], cq, sq, MLA_ROPE // 2) * SCALE_B
        _split(q, HEADS, DQK_B, qo)
        kr = _rope_lanes(kr_ref[...], ck_ref[...], sk_ref[...], MLA_ROPE // 2)[:, :MLA_ROPE].astype(BF16)
        kv = kv_ref[...]
        for h in range(HEADS):
            ko[h] = jnp.concatenate([kv[:, 128 * h:128 * h + MLA_NOPE].astype(BF16), kr], axis=1)
            vo[h] = kv[:, 128 * h + MLA_NOPE:128 * (h + 1)].astype(BF16)

    t128 = pl.BlockSpec((tm, LANES), lambda i: (i, 0))
    t768 = pl.BlockSpec((tm, 768), lambda i: (i, 0))
    return pl.pallas_call(
        body, name=name, grid=(L // tm,),
        in_specs=[t768, pl.BlockSpec((tm, 1024), lambda i: (i, 0)), _col_spec(tm, 128, KR), t128, t128,
                  pl.BlockSpec((1, 768), lambda i: (0, 0))],
        out_specs=[_head_spec(HEADS, tm, DQK_B), _head_spec(HEADS, tm, DQK_B), _head_spec(HEADS, tm, DH)],
        out_shape=[jax.ShapeDtypeStruct((HEADS, L, DQK_B), BF16), jax.ShapeDtypeStruct((HEADS, L, DQK_B), BF16),
                   jax.ShapeDtypeStruct((HEADS, L, DH), BF16)],
        compiler_params=_cparams(("parallel",)),
    )(qbm, kvbm, proj, *tab_k, q_rotary)


def _unprep_b(dq, dk, dv, tab_k, q_rotary, dproj, *, name):
    L = dq.shape[1]
    tm = ROW_T

    def body(dq_ref, dk_ref, dv_ref, ck_ref, sk_ref, on_ref, _, dp_kr, dqo, dkvo):
        cq, sq = _q_tables(ck_ref[...], sk_ref[...], on_ref)
        dqv = _join(dq_ref, HEADS) * SCALE_B
        dqo[...] = _rope_lanes(dqv, cq, -sq, MLA_ROPE // 2).astype(BF16)
        parts = []
        dkr = None
        for h in range(HEADS):
            dkh = dk_ref[h]
            parts += [dkh[:, :MLA_NOPE], dv_ref[h]]
            r = dkh[:, MLA_NOPE:]
            dkr = r if dkr is None else dkr + r
        dkvo[...] = jnp.concatenate(parts, axis=1).astype(BF16)
        dkr = jnp.concatenate([dkr, jnp.zeros((tm, LANES - MLA_ROPE), F32)], axis=1)
        dp_kr[...] = _rope_lanes(dkr, ck_ref[...], -sk_ref[...], MLA_ROPE // 2).astype(BF16)

    t128 = pl.BlockSpec((tm, LANES), lambda i: (i, 0))
    t768 = pl.BlockSpec((tm, 768), lambda i: (i, 0))
    hq = _head_spec(HEADS, tm, DQK_B)
    return pl.pallas_call(
        body, name=name, grid=(L // tm,),
        in_specs=[hq, hq, _head_spec(HEADS, tm, DH), t128, t128, pl.BlockSpec((1, 768), lambda i: (0, 0)),
                  pl.BlockSpec(memory_space=pl.ANY)],
        out_specs=[_col_spec(tm, 128, KR), t768, pl.BlockSpec((tm, 1024), lambda i: (i, 0))],
        out_shape=[jax.ShapeDtypeStruct(dproj.shape, dproj.dtype), jax.ShapeDtypeStruct((L, 768), BF16),
                   jax.ShapeDtypeStruct((L, 1024), BF16)],
        input_output_aliases={6: 0},
        compiler_params=_cparams(("parallel",)),
    )(dq, dk, dv, *tab_k, q_rotary, dproj)


def _gate(y, proj, zcol, *, name):
    L = proj.shape[0]
    tm = ROW_T

    def body(y_ref, z_ref, u_ref):
        z = z_ref[...]
        u_ref[...] = (_join(y_ref, HEADS) * (z * jax.nn.sigmoid(z))).astype(BF16)

    return pl.pallas_call(
        body, name=name, grid=(L // tm,),
        in_specs=[_head_spec(HEADS, tm, DH), _col_spec(tm, 512, zcol)],
        out_specs=pl.BlockSpec((tm, 512), lambda i: (i, 0)),
        out_shape=jax.ShapeDtypeStruct((L, 512), BF16),
        compiler_params=_cparams(("parallel",)),
    )(y, proj)


def _gate_bwd(du, y, proj, zcol, dproj, *, delta_rows=False, name):
    L = proj.shape[0]
    tm = ROW_T
    assert not delta_rows or tm == ATT_T

    def body(du_ref, y_ref, z_ref, _, dz_ref, dy_ref, *dl_ref):
        z = z_ref[...]
        duv = du_ref[...]
        sg = jax.nn.sigmoid(z)
        yv = _join(y_ref, HEADS)
        dyv = duv * (z * sg)
        _split(dyv, HEADS, DH, dy_ref)
        dz_ref[...] = (duv * yv * (sg * (1.0 + z * (1.0 - sg)))).astype(BF16)
        if delta_rows:
            prod = dyv * yv
            ones = jnp.ones((8, DH), F32)
            for h in range(HEADS):
                sums = lax.dot_general(ones, prod[:, DH * h:DH * (h + 1)], _NT, preferred_element_type=F32)
                dl_ref[0][h, 0] = sums[0:1, :]

    hs = _head_spec(HEADS, tm, DH)
    out_specs = [_col_spec(tm, 512, zcol), hs]
    out_shape = [jax.ShapeDtypeStruct(dproj.shape, dproj.dtype),
                 jax.ShapeDtypeStruct((HEADS, L, DH), BF16 if delta_rows else F32)]
    if delta_rows:
        out_specs.append(pl.BlockSpec((HEADS, 1, 1, tm), lambda i: (0, i, 0, 0)))
        out_shape.append(jax.ShapeDtypeStruct((HEADS, L // tm, 1, tm), F32))
    return pl.pallas_call(
        body, name=name, grid=(L // tm,),
        in_specs=[pl.BlockSpec((tm, 512), lambda i: (i, 0)), hs, _col_spec(tm, 512, zcol),
                  pl.BlockSpec(memory_space=pl.ANY)],
        out_specs=out_specs, out_shape=out_shape,
        input_output_aliases={3: 0},
        compiler_params=_cparams(("parallel",)),
    )(du, y, proj, dproj)


MERGE_T = 192


def _merge(proj, pbs, *, name):
    L = proj.shape[0]
    tm = MERGE_T

    def body(g0, g1, g2, p0, p1, p2, o_ref):
        acc = None
        for g_ref, p_ref in ((g0, p0), (g1, p1), (g2, p2)):
            t = jax.nn.sigmoid(g_ref[...]) * p_ref[...]
            acc = t if acc is None else acc + t
        o_ref[...] = acc.astype(BF16)

    row = pl.BlockSpec((tm, D_MODEL), lambda i: (i, 0))
    return pl.pallas_call(
        body, name=name, grid=(L // tm,),
        in_specs=[_col_spec(tm, D_MODEL, GATES + n * D_MODEL) for n in range(N_BRANCH)] + [row] * N_BRANCH,
        out_specs=row, out_shape=jax.ShapeDtypeStruct((L, D_MODEL), BF16),
        compiler_params=_cparams(("parallel",)),
    )(proj, proj, proj, *pbs)


def _merge_bwd(dmerged, proj, pbs, dproj, *, name):
    L = proj.shape[0]
    tm = MERGE_T

    def body(dm_ref, g_ref, p0, p1, p2, _, dg_ref, dp0, dp1, dp2):
        dm = dm_ref[...]
        for n, (p_ref, dp_ref) in enumerate(((p0, dp0), (p1, dp1), (p2, dp2))):
            cols = slice(n * D_MODEL, (n + 1) * D_MODEL)
            sg = jax.nn.sigmoid(g_ref[:, cols])
            dp_ref[...] = (dm * sg).astype(BF16)
            dg_ref[:, cols] = (dm * p_ref[...] * (sg * (1.0 - sg))).astype(BF16)

    row = pl.BlockSpec((tm, D_MODEL), lambda i: (i, 0))
    gates = _col_spec(tm, N_BRANCH * D_MODEL, GATES)
    outs = pl.pallas_call(
        body, name=name, grid=(L // tm,),
        in_specs=[row, gates] + [row] * N_BRANCH + [pl.BlockSpec(memory_space=pl.ANY)],
        out_specs=[gates] + [row] * N_BRANCH,
        out_shape=[jax.ShapeDtypeStruct(dproj.shape, dproj.dtype)]
        + [jax.ShapeDtypeStruct((L, D_MODEL), BF16)] * N_BRANCH,
        input_output_aliases={5: 0},
        compiler_params=_cparams(("parallel",)),
    )(dmerged, proj, *pbs, dproj)
    return outs[0], outs[1:]


def _scan_rows(x, reverse):
    rows = lax.broadcasted_iota(jnp.int32, x.shape, 0)
    step = 1
    while step < BLK:
        if reverse:
            x = x + jnp.where(rows < BLK - step, pltpu.roll(x, BLK - step, 0), 0.0)
        else:
            x = x + jnp.where(rows >= step, pltpu.roll(x, step, 0), 0.0)
        step *= 2
    return x


def _forget_fwd(proj, b_f, *, name):
    L = proj.shape[0]
    nb = L // BLK

    def body(x_ref, b_ref, o_ref, carry):
        i = pl.program_id(0)

        @pl.when(i == 0)
        def _():
            carry[...] = jnp.zeros_like(carry)

        c = _scan_rows(jax.nn.log_sigmoid(x_ref[...] + b_ref[...]), False) + carry[...]
        carry[...] = c[BLK - 1:BLK, :]
        pos = i * BLK + lax.broadcasted_iota(jnp.int32, (HEADS, BLK), 1)
        o_ref[...] = c.T[:HEADS, :] + jnp.where(pos < PAD, BIG, 0.0)

    return pl.pallas_call(
        body, name=name, grid=(nb,),
        in_specs=[_col_spec(BLK, LANES, FA), pl.BlockSpec((1, LANES), lambda i: (0, 0))],
        out_specs=pl.BlockSpec((HEADS, BLK), lambda i: (0, i)),
        out_shape=jax.ShapeDtypeStruct((HEADS, L), F32),
        scratch_shapes=[pltpu.VMEM((1, LANES), F32)],
        compiler_params=_cparams(("arbitrary",)),
    )(proj, jnp.pad(b_f, (0, LANES - HEADS)).reshape(1, LANES))


def _forget_bwd(dct, proj, b_f, dproj, *, name):
    L = proj.shape[0]
    nb = L // BLK

    def body(d_ref, x_ref, b_ref, _, dp_ref, db_ref, carry):
        i = pl.program_id(0)

        @pl.when(i == 0)
        def _():
            carry[...] = jnp.zeros_like(carry)
            db_ref[...] = jnp.zeros_like(db_ref)

        d = jnp.concatenate([d_ref[...], jnp.zeros((BLK - HEADS, BLK), F32)], axis=0).T
        dlog = _scan_rows(d, True) + carry[...]
        carry[...] = dlog[0:1, :]
        lane = lax.broadcasted_iota(jnp.int32, (BLK, LANES), 1)
        daf = jnp.where(lane < HEADS, dlog * jax.nn.sigmoid(-(x_ref[...] + b_ref[...])), 0.0)
        db_ref[...] += jnp.sum(daf, axis=0, keepdims=True)
        dp_ref[...] = jnp.concatenate([daf, jnp.zeros((BLK, LANES), F32)], axis=1).astype(BF16)

    back = lambda i: nb - 1 - i
    dp, db = pl.pallas_call(
        body, name=name, grid=(nb,),
        in_specs=[pl.BlockSpec((HEADS, BLK), lambda i: (0, back(i))),
                  pl.BlockSpec((BLK, LANES), lambda i: (back(i), FA // LANES)),
                  pl.BlockSpec((1, LANES), lambda i: (0, 0)), pl.BlockSpec(memory_space=pl.ANY)],
        out_specs=[pl.BlockSpec((BLK, 2 * LANES), lambda i: (back(i), FA // (2 * LANES))),
                   pl.BlockSpec((1, LANES), lambda i: (0, 0))],
        out_shape=[jax.ShapeDtypeStruct(dproj.shape, dproj.dtype), jax.ShapeDtypeStruct((1, LANES), F32)],
        scratch_shapes=[pltpu.VMEM((1, LANES), F32)],
        input_output_aliases={3: 0},
        compiler_params=_cparams(("arbitrary",)),
    )(dct, proj, jnp.pad(b_f, (0, LANES - HEADS)).reshape(1, LANES), dproj)
    return dp, db[0, :HEADS]


def _key_bias(L, T, ct=None):
    padb = jnp.where(jnp.arange(L) < PAD, BIG, 0.0).astype(F32)[None]
    kb = padb if ct is None else ct + padb
    return kb.reshape(kb.shape[0], L // T, 1, T)


def _layer_fwd(h, w, tabs, l, side=None):
    tag = f"l{l}"
    L = h.shape[0]
    hn = _rms_fwd(h, w["norm_g"], name=f"{tag}_rms_in")
    if "late" in w:
        late_job, assemble = w["late"]
        proj, *late = _mm(hn, w["w_in"], side=late_job, name=f"{tag}_mm_in")
        w = {**{key: val for key, val in w.items() if key != "late"}, **assemble(*late)}
    else:
        proj = _mm(hn, w["w_in"], name=f"{tag}_mm_in")
    kb_a = _forget_fwd(proj, w["b_f"], name=f"{tag}_forget").reshape(HEADS, L // ATT_T, 1, ATT_T)
    ops_a = (*_prep_a(proj, name=f"{tag}_prep_a"), kb_a)
    ya, lsea, *side_out = _attn_fwd(*ops_a, T=ATT_T, side=side, name=f"{tag}_attn_a")
    cqn = _rms_fwd(proj, w["g_cq"], col=CQ, name=f"{tag}_rms_cq")
    ckvn = _rms_fwd(proj, w["g_ckv"], col=CKV, name=f"{tag}_rms_ckv")
    qbm = _mm(cqn, w["w_uq"], name=f"{tag}_mm_uq")
    kvbm = _mm(ckvn, w["w_ukv"], name=f"{tag}_mm_ukv")
    ops_b = (*_prep_b(qbm, kvbm, proj, tabs["bk"], tabs["bq"], name=f"{tag}_prep_b"), _key_bias(L, ATT_T))
    yb, lseb = _attn_fwd(*ops_b, T=ATT_T, name=f"{tag}_attn_b")
    ops_c = (*_prep_c(proj, tabs["c"], name=f"{tag}_prep_c"), _key_bias(L, WINDOW))
    sink = jnp.broadcast_to(w["sinks"][:, None, None], (HEADS, 1, LANES))
    yc, lsec = _attn_fwd(*ops_c, T=WINDOW, window=True, sink=sink, name=f"{tag}_attn_c")
    us = [_gate(y, proj, zcol, name=f"{tag}_gate{n}") for n, (y, zcol) in enumerate(((ya, ZA), (yb, ZB), (yc, ZC)))]
    pbr = [_mm(us[n], w["w_branch"][n], name=f"{tag}_mm_br{n}") for n in range(N_BRANCH)]
    merged = _merge(proj, pbr, name=f"{tag}_merge")
    out = _mm(merged, w["w_out"], add=h, name=f"{tag}_mm_out")
    saved = dict(h=h, hn=hn, proj=proj, ops_a=ops_a, ya=ya, lsea=lsea, cqn=cqn, ckvn=ckvn,
                 ops_b=ops_b, yb=yb, lseb=lseb, ops_c=ops_c, yc=yc, lsec=lsec, us=us, pbr=pbr, merged=merged)
    return out, saved, side_out, w


def _w_in_chunks(g, tag):
    return _w_in_to_shards(g["w_in"], name=f"{tag}_w_in_chunks")


def _grad_chunks(g, tag, with_w_in=True):
    br = _cut_shards(g["w_branch"].reshape(N_BRANCH * BRANCH_W, D_MODEL), name=f"{tag}_w_branch_chunks")
    rest = [_cut_shards(g["w_uq"], name=f"{tag}_w_uq_chunks"), _cut_shards(g["w_ukv"], name=f"{tag}_w_ukv_chunks"),
            br.reshape(N_DEV, N_BRANCH, BRANCH_W, D_MODEL // N_DEV),
            g["w_out"].reshape(N_DEV, D_MODEL // N_DEV, D_MODEL)]
    return ([_w_in_chunks(g, tag)] if with_w_in else []) + rest


def _layer_bwd(dout, s, w, tabs, l, side=None, own_exchange=False):
    tag = f"l{l}"
    L = dout.shape[0]
    proj = s["proj"]
    g = {}
    dproj = jnp.zeros((L, NP_IN), BF16)
    dmerged = _mm(dout, w["w_out"], tb=True, name=f"{tag}_mm_out_dx")
    g["w_out"] = _mm(s["merged"], dout, ta=True, out_dtype=BF16, name=f"{tag}_mm_out_dw")
    dproj, dpbr = _merge_bwd(dmerged, proj, s["pbr"], dproj, name=f"{tag}_merge_bwd")
    dus = [_mm(dpbr[n], w["w_branch"][n], tb=True, name=f"{tag}_mm_br{n}_dx") for n in range(N_BRANCH)]
    g["w_branch"] = jnp.stack([_mm(s["us"][n], dpbr[n], ta=True, out_dtype=BF16, name=f"{tag}_mm_br{n}_dw")
                               for n in range(N_BRANCH)])
    dproj, dya, dla = _gate_bwd(dus[0], s["ya"], proj, ZA, dproj, delta_rows=True, name=f"{tag}_gate0_bwd")
    dproj, dyb, dlb = _gate_bwd(dus[1], s["yb"], proj, ZB, dproj, delta_rows=True, name=f"{tag}_gate1_bwd")
    dproj, dyc = _gate_bwd(dus[2], s["yc"], proj, ZC, dproj, name=f"{tag}_gate2_bwd")

    def bwd_operands(ops, lse, delta, dy16):
        q16, k16, v16, kbias = ops
        return (q16, k16, v16, kbias.reshape(kbias.shape[0], L, 1), lse.reshape(HEADS, L // ATT_T, 1, ATT_T), delta,
                dy16)

    dqa, dka, dva, dcq, dck, *side_out = _attn_bwd_t(*bwd_operands(s["ops_a"], s["lsea"], dla, dya), T=ATT_T,
                                                     fox=True, side=side, name=f"{tag}_attn_a_bwd")
    dproj = _unprep_a(dqa, dka, dva, dproj, name=f"{tag}_unprep_a")
    dproj, g["b_f"] = _forget_bwd(dcq.reshape(HEADS, L) + dck[:, :, 0], proj, w["b_f"], dproj,
                                  name=f"{tag}_forget_bwd")
    dqb, dkb, dvb, _, _ = _attn_bwd_t(*bwd_operands(s["ops_b"], s["lseb"], dlb, dyb), T=ATT_T,
                                      name=f"{tag}_attn_b_bwd")
    dproj, dqbm, dkvbm = _unprep_b(dqb, dkb, dvb, tabs["bk"], tabs["bq"], dproj, name=f"{tag}_unprep_b")
    dcqn = _mm(dqbm, w["w_uq"], tb=True, name=f"{tag}_mm_uq_dx")
    g["w_uq"] = _mm(s["cqn"], dqbm, ta=True, out_dtype=BF16, name=f"{tag}_mm_uq_dw")
    dckvn = _mm(dkvbm, w["w_ukv"], tb=True, name=f"{tag}_mm_ukv_dx")
    g["w_ukv"] = _mm(s["ckvn"], dkvbm, ta=True, out_dtype=BF16, name=f"{tag}_mm_ukv_dw")
    dproj, g["g_cq"] = _rms_bwd(dcqn, proj, w["g_cq"], col=CQ, into=dproj, name=f"{tag}_rms_cq_bwd")
    dproj, g["g_ckv"] = _rms_bwd(dckvn, proj, w["g_ckv"], col=CKV, into=dproj, name=f"{tag}_rms_ckv_bwd")
    dqc, dkc, dvc = _attn_bwd_window(*s["ops_c"], s["yc"], s["lsec"], dyc, name=f"{tag}_attn_c_bwd")
    dproj = _unprep_c(dqc, dkc, dvc, tabs["c"], dproj, name=f"{tag}_unprep_c")
    delta_c = jnp.sum(dyc * s["yc"], axis=-1)
    g["sinks"] = -jnp.sum(jnp.exp(w["sinks"][:, None] - s["lsec"][:, :, 0]) * delta_c, axis=1)
    own_out = []
    if own_exchange:
        g["w_in"], *r_rest = _mm(s["hn"], dproj, ta=True, out_dtype=BF16, side=_Exchange(_grad_chunks(g, tag, False)),
                                 name=f"{tag}_mm_in_dw")
        dhn, r_in = _mm(dproj, w["w_in"], tb=True, side=_Exchange([_w_in_chunks(g, tag)]), name=f"{tag}_mm_in_dx")
        own_out = [r_in, *r_rest]
    else:
        g["w_in"] = _mm(s["hn"], dproj, ta=True, out_dtype=BF16, name=f"{tag}_mm_in_dw")
        dhn = _mm(dproj, w["w_in"], tb=True, name=f"{tag}_mm_in_dx")
    dh, g["norm_g"] = _rms_bwd(dhn, s["h"], w["norm_g"], add=dout, name=f"{tag}_rms_in_bwd")
    return dh, g, side_out, own_out


def _cols_from_shards(g):
    return jnp.moveaxis(g, 0, 1).reshape(g.shape[1], N_DEV * g.shape[2])


def _cols_to_shards(w):
    R = w.shape[0]
    return jnp.moveaxis(w.reshape(R, N_DEV, w.shape[1] // N_DEV), 1, 0)


def _join_shards(g, *, name):
    _, R, C = g.shape
    tr = _pick(R, (512, 384, 256))

    def body(g_ref, o_ref):
        for d in range(N_DEV):
            o_ref[:, C * d:C * (d + 1)] = g_ref[d]

    return pl.pallas_call(
        body, name=name, grid=(R // tr,),
        in_specs=[pl.BlockSpec((N_DEV, tr, C), lambda i: (0, i, 0))],
        out_specs=pl.BlockSpec((tr, N_DEV * C), lambda i: (i, 0)),
        out_shape=jax.ShapeDtypeStruct((R, N_DEV * C), g.dtype),
        compiler_params=_cparams(("parallel",)),
    )(g)


def _cut_shards(w, *, name):
    R = w.shape[0]
    C = w.shape[1] // N_DEV
    tr = _pick(R, (512, 384, 256))

    def body(w_ref, o_ref):
        for d in range(N_DEV):
            o_ref[d] = w_ref[:, C * d:C * (d + 1)]

    return pl.pallas_call(
        body, name=name, grid=(R // tr,),
        in_specs=[pl.BlockSpec((tr, N_DEV * C), lambda i: (i, 0))],
        out_specs=pl.BlockSpec((N_DEV, tr, C), lambda i: (0, i, 0)),
        out_shape=jax.ShapeDtypeStruct((N_DEV, R, C), w.dtype),
        compiler_params=_cparams(("parallel",)),
    )(w)


def _pad_in(w):
    parts, at = [], 0
    for lo, hi, dst in sorted(_RUNS, key=lambda r: r[2]):
        if dst > at:
            parts.append(jnp.zeros((w.shape[0], dst - at), w.dtype))
        parts.append(w[:, lo:hi])
        at = dst + hi - lo
    parts.append(jnp.zeros((w.shape[0], NP_IN - at), w.dtype))
    return jnp.concatenate(parts, axis=1)


def _unpad_in(w):
    return jnp.concatenate([w[:, dst:dst + hi - lo] for lo, hi, dst in _RUNS], axis=1)


W_IN_SHARD = N_IN // N_DEV


def _w_in_pieces():
    pieces = []
    for lo, hi, dst in _RUNS:
        for d in range(N_DEV):
            a, b = max(lo, d * W_IN_SHARD), min(hi, (d + 1) * W_IN_SHARD)
            if a < b:
                pieces.append((d, a - d * W_IN_SHARD, b - d * W_IN_SHARD, dst + a - lo))
    return pieces


def _w_in_from_shards(g, *, name):
    R = g.shape[1]
    tr = 256
    covered = sorted((dst, dst + b - a) for _, a, b, dst in _w_in_pieces())

    def body(g_ref, o_ref):
        at = 0
        for lo, hi in covered + [(NP_IN, NP_IN)]:
            if lo > at:
                o_ref[:, at:lo] = jnp.zeros((tr, lo - at), BF16)
            at = max(at, hi)
        for d, a, b, dst in _w_in_pieces():
            o_ref[:, dst:dst + b - a] = g_ref[d, :, a:b]

    return pl.pallas_call(
        body, name=name, grid=(R // tr,),
        in_specs=[pl.BlockSpec((N_DEV, tr, W_IN_SHARD), lambda i: (0, i, 0))],
        out_specs=pl.BlockSpec((tr, NP_IN), lambda i: (i, 0)),
        out_shape=jax.ShapeDtypeStruct((R, NP_IN), BF16),
        compiler_params=_cparams(("parallel",)),
    )(g)


def _w_in_to_shards(dw, *, name):
    R = dw.shape[0]
    tr = 256

    def body(w_ref, o_ref):
        for d, a, b, dst in _w_in_pieces():
            o_ref[d, :, a:b] = w_ref[:, dst:dst + b - a]

    return pl.pallas_call(
        body, name=name, grid=(R // tr,),
        in_specs=[pl.BlockSpec((tr, NP_IN), lambda i: (i, 0))],
        out_specs=pl.BlockSpec((N_DEV, tr, W_IN_SHARD), lambda i: (0, i, 0)),
        out_shape=jax.ShapeDtypeStruct((N_DEV, R, W_IN_SHARD), BF16),
        compiler_params=_cparams(("parallel",)),
    )(dw)


_SMALL = (("norm_g", DEPTH * D_MODEL), ("b_f", DEPTH * HEADS), ("g_cq", DEPTH * MLA_QLORA),
          ("g_ckv", DEPTH * MLA_KVLORA), ("sinks", DEPTH * HEADS), ("final_g", D_MODEL), ("loss", 1),
          ("meta", N_META * D_MODEL))
SMALL_ROWS = 168


def _pack_small(d):
    parts = []
    for name, size in _SMALL:
        padded = -(-size // 128) * 128
        v = d[name].reshape(-1).astype(F32) if name in d else jnp.zeros((size,), F32)
        parts.append(jnp.pad(v, (0, padded - size)))
    flat = jnp.concatenate(parts)
    return jnp.pad(flat, (0, SMALL_ROWS * 128 - flat.shape[0])).reshape(SMALL_ROWS, 128)


def _unpack_small(p, shapes):
    flat = p.reshape(-1)
    out, at = {}, 0
    for name, size in _SMALL:
        if name in shapes:
            out[name] = flat[at:at + size].reshape(shapes[name])
        at += -(-size // 128) * 128
    return out


def kernel(x, meta_tokens, norm_g, w_in, b_f, g_cq, g_ckv, w_uq, w_ukv, sinks, w_branch, w_out, final_g, loss_target, m_meta_tokens, m_norm_g, m_w_in, m_b_f, m_g_cq, m_g_ckv, m_w_uq, m_w_ukv, m_sinks, m_w_branch, m_w_out, m_final_g, v_meta_tokens, v_norm_g, v_w_in, v_b_f, v_g_cq, v_g_ckv, v_w_uq, v_w_ukv, v_sinks, v_w_branch, v_w_out, v_final_g):
    S = x.shape[1]
    L = BLK + S
    cx, cy, cc = _me()
    my_idx = 4 * cx + 2 * cy + cc

    def shards(l):
        return [t[l].astype(BF16) for t in (w_in, w_uq, w_ukv, w_branch, w_out)]

    def small_weights(l):
        return dict(norm_g=norm_g[l], b_f=b_f[l], g_cq=g_cq[l], g_ckv=g_ckv[l], sinks=sinks[l])

    def rest_weights(l):
        def assemble(gw_uq, gw_ukv, gw_br, gw_out):
            br = _join_shards(gw_br.reshape(N_DEV, N_BRANCH * BRANCH_W, D_MODEL // N_DEV), name=f"l{l}_w_branch_full")
            return dict(w_uq=_join_shards(gw_uq, name=f"l{l}_w_uq_full"),
                        w_ukv=_join_shards(gw_ukv, name=f"l{l}_w_ukv_full"),
                        w_branch=br.reshape(N_BRANCH, BRANCH_W, D_MODEL), w_out=gw_out.reshape(D_MODEL, D_MODEL))
        return assemble

    def layer_weights(l, gw_in, *gw_rest):
        return dict(small_weights(l), w_in=_w_in_from_shards(gw_in, name=f"l{l}_w_in_full"),
                    **rest_weights(l)(*gw_rest))

    gw_in0, g_meta = _comm(_Gather([shards(0)[0], meta_tokens]), name="gather_l0")
    layers = [dict(small_weights(0), w_in=_w_in_from_shards(gw_in0, name="l0_w_in_full"),
                   late=(_Gather(shards(0)[1:]), rest_weights(0))), None]
    meta_full = _cols_from_shards(g_meta)

    h = jnp.concatenate([jnp.zeros((PAD, D_MODEL), F32), meta_full, x[0]], axis=0)
    q_rotary = ((jnp.arange(HEADS * DQK_B) % DQK_B) >= MLA_NOPE).astype(F32)[None, :]
    tabs = dict(c=_rope_tables(L, DH // 2, LANES), bk=_rope_tables(L, MLA_ROPE // 2, LANES), bq=q_rotary)
    saved = [None] * DEPTH
    h, saved[0], gw1, layers[0] = _layer_fwd(h, layers[0], tabs, 0, side=_Gather(shards(1)))
    layers[1] = layer_weights(1, *gw1)
    h, saved[1], _, _ = _layer_fwd(h, layers[1], tabs, 1)
    loss_vec, dh, g_final = _loss_head(h, final_g, loss_target[0], name="loss_head")

    grads = [None] * DEPTH
    dh, grads[1], _, _ = _layer_bwd(dh, saved[1], layers[1], tabs, 1)
    dh, grads[0], recv1, recv0 = _layer_bwd(dh, saved[0], layers[0], tabs, 0, side=_Exchange(_grad_chunks(grads[1], "l1")),
                                            own_exchange=True)
    r_in, r_uq, r_ukv, r_br, r_out = (jnp.stack([a, b], axis=1) for a, b in zip(recv0, recv1))

    def stack(name):
        return jnp.stack([grads[l][name] for l in range(DEPTH)])

    small = _pack_small(dict(norm_g=stack("norm_g"), b_f=stack("b_f"), g_cq=stack("g_cq"), g_ckv=stack("g_ckv"),
                             sinks=stack("sinks"), final_g=g_final, loss=loss_vec[0, 0:1],
                             meta=dh[PAD:BLK]))
    (g_small,) = _comm(_Gather([small]), name="gather_small")

    def adam_big(w_, parts, m_, v_, name):
        shape = w_.shape
        C = shape[-1]
        R = math.prod(shape[:-1])
        outs = _adamw(w_.reshape(R, C), parts.reshape(parts.shape[0], R, C), m_.reshape(R, C), v_.reshape(R, C),
                      name=name)
        return [o.reshape(shape) for o in outs]

    res = {}
    res["w_in"] = adam_big(w_in, r_in, m_w_in, v_w_in, "adam_w_in")
    res["w_uq"] = adam_big(w_uq, r_uq, m_w_uq, v_w_uq, "adam_w_uq")
    res["w_ukv"] = adam_big(w_ukv, r_ukv, m_w_ukv, v_w_ukv, "adam_w_ukv")
    res["w_branch"] = adam_big(w_branch, r_br, m_w_branch, v_w_branch, "adam_w_branch")
    res["w_out"] = adam_big(w_out, r_out, m_w_out, v_w_out, "adam_w_out")

    small_w = dict(norm_g=norm_g, b_f=b_f, g_cq=g_cq, g_ckv=g_ckv, sinks=sinks, final_g=final_g)
    small_m = dict(norm_g=m_norm_g, b_f=m_b_f, g_cq=m_g_cq, g_ckv=m_g_ckv, sinks=m_sinks, final_g=m_final_g)
    small_v = dict(norm_g=v_norm_g, b_f=v_b_f, g_cq=v_g_cq, g_ckv=v_g_ckv, sinks=v_sinks, final_g=v_final_g)
    sm = _adamw(_pack_small(small_w), g_small, _pack_small(small_m), _pack_small(small_v), name="adam_small")
    shapes = {k: a.shape for k, a in small_w.items()}
    shapes_all = dict(shapes, loss=(), meta=(N_META, D_MODEL))
    sm_g = _unpack_small(sm[0], shapes_all)
    sm_d, sm_m, sm_v = (_unpack_small(t, shapes) for t in sm[1:])
    for k in shapes:
        res[k] = [sm_g[k], sm_d[k], sm_m[k], sm_v[k]]
    g_meta_mine = lax.dynamic_slice(sm_g["meta"], (0, my_idx * 128), (N_META, 128))
    res["meta_tokens"] = _adamw(meta_tokens, g_meta_mine[None], m_meta_tokens, v_meta_tokens, name="adam_meta")

    order = ["meta_tokens", "norm_g", "w_in", "b_f", "g_cq", "g_ckv", "w_uq", "w_ukv", "sinks", "w_branch", "w_out",
             "final_g"]
    grad_x = dh[BLK:][None]
    return (sm_g["loss"], grad_x, *[res[k][0] for k in order], *[res[k][1] for k in order],
            *[res[k][2] for k in order], *[res[k][3] for k in order])
```

```python
import functools
import math

import jax
import jax.numpy as jnp
from jax import lax
from jax.experimental import pallas as pl
from jax.experimental.pallas import tpu as pltpu

F32 = jnp.float32
BF16 = jnp.bfloat16

D_MODEL = 1024
DEPTH = 2
N_META = 16
BLK = 128
PAD = BLK - N_META
ROPE_THETA = 10000.0
EPS = 1e-6
NEG = -1e30
BIG = 1e30
HEADS = 8
DH = 64
MLA_NOPE = 64
MLA_ROPE = 32
MLA_QLORA = 384
MLA_KVLORA = 256
SWA_KV_HEADS = 2
WINDOW = 128
BRANCH_W = 512
N_BRANCH = 3
N_IN = 7592

ADAM_LR = 0.001
ADAM_B1 = 0.9
ADAM_B2 = 0.999
ADAM_EPS = 1e-08
ADAM_WD = 0.01
ADAM_STEP = 10

N_DEV = 8
MESH = pl.DeviceIdType.MESH

NP_IN = 8192
QKVA, ZA, FA = 0, 1536, 2048
CKV, KR, CQ = 2304, 2560, 2688
GATES = 3072
ZB = 6144
QKVC, ZC = 6912, 7680
_RUNS = ((0, 1536, QKVA), (1536, 1544, FA), (1544, 2056, ZA), (2056, 2440, CQ), (2440, 2696, CKV), (2696, 2728, KR),
         (2728, 3240, ZB), (3240, 4008, QKVC), (4008, 4520, ZC), (4520, 7592, GATES))

VMEM_LIMIT = 48 * 1024 * 1024
ATT_T = 384
ATT_HP = 1
ROW_T = 384
LANES = 128


def _pick(dim, prefs):
    for p in prefs:
        if dim % p == 0:
            return p
    return dim


def _cparams(sem):
    return pltpu.CompilerParams(dimension_semantics=sem, vmem_limit_bytes=VMEM_LIMIT)


def _side_cparams(side, sem):
    if side is None:
        return _cparams(sem)
    return pltpu.CompilerParams(dimension_semantics=("arbitrary",) * len(sem), vmem_limit_bytes=VMEM_LIMIT,
                                has_side_effects=True)


def _mm(a, b, *, ta=False, tb=False, add=None, out_dtype=F32, side=None, name):
    M = a.shape[1] if ta else a.shape[0]
    K = a.shape[0] if ta else a.shape[1]
    N = b.shape[0] if tb else b.shape[1]
    assert K == (b.shape[1] if tb else b.shape[0])
    tm = _pick(M, (704, 1024, 512, 384, 256, 128))
    tn = _pick(N, (1024, 768, 512, 384, 256, 128))
    tk = _pick(K, (4096, 1408, 1024, 768, 512, 384, 256, 128))
    nk = K // tk
    dims = (((0 if ta else 1,), (1 if tb else 0,)), ((), ()))

    sn = 0 if side is None else side.n
    n_in = 2 + (add is not None)
    grid = (M // tm, N // tn, nk)

    def body(*refs):
        a_ref, b_ref = refs[:2]
        c_ref = refs[2] if add is not None else None
        o_ref = refs[n_in + sn]
        scratch = refs[n_in + 2 * sn + 1:]
        if side is not None:
            step = (pl.program_id(0) * grid[1] + pl.program_id(1)) * nk + pl.program_id(2)
            _ride(side, [*refs[n_in:n_in + sn], *refs[n_in + sn + 1:n_in + 2 * sn + 1], *scratch[nk > 1:]],
                  step, grid[0] * grid[1] * nk)
        r = lax.dot_general(a_ref[...].astype(BF16), b_ref[...].astype(BF16), dims, preferred_element_type=F32)

        def finish(total):
            if c_ref is not None:
                total = total + c_ref[...]
            o_ref[...] = total.astype(out_dtype)

        if nk == 1:
            finish(r)
        else:
            acc = scratch[0]
            k = pl.program_id(2)

            @pl.when(k == 0)
            def _():
                acc[...] = r

            @pl.when(k > 0)
            def _():
                acc[...] += r

            @pl.when(k == nk - 1)
            def _():
                finish(acc[...])

    a_spec = pl.BlockSpec((tk, tm), lambda i, j, k: (k, i)) if ta else pl.BlockSpec((tm, tk), lambda i, j, k: (i, k))
    b_spec = pl.BlockSpec((tn, tk), lambda i, j, k: (j, k)) if tb else pl.BlockSpec((tk, tn), lambda i, j, k: (k, j))
    o_spec = pl.BlockSpec((tm, tn), lambda i, j, k: (i, j))
    any_spec = pl.BlockSpec(memory_space=pl.ANY)
    outs = pl.pallas_call(
        body, name=name,
        grid=grid,
        in_specs=[a_spec, b_spec] + ([o_spec] if add is not None else []) + [any_spec] * sn,
        out_specs=[o_spec] + [any_spec] * sn,
        out_shape=[jax.ShapeDtypeStruct((M, N), out_dtype)] + ([] if side is None else side.out_shape),
        scratch_shapes=([pltpu.VMEM((tm, tn), F32)] if nk > 1 else []) + ([] if side is None else side.scratch),
        compiler_params=_side_cparams(side, ("parallel", "parallel", "arbitrary")),
    )(*((a, b) if add is None else (a, b, add)), *([] if side is None else side.arrs))
    return outs[0] if side is None else outs


def _col_spec(tm, width, col):
    assert col % width == 0
    return pl.BlockSpec((tm, width), lambda i, _c=col // width: (i, _c))


def _rms_fwd(x, g, *, col=0, name):
    L = x.shape[0]
    D = g.shape[0]
    tm = ROW_T

    def body(x_ref, g_ref, y_ref):
        xv = x_ref[...]
        rstd = lax.rsqrt(jnp.mean(xv * xv, axis=-1, keepdims=True) + EPS)
        y_ref[...] = (xv * rstd * g_ref[...]).astype(BF16)

    return pl.pallas_call(
        body, name=name, grid=(L // tm,),
        in_specs=[_col_spec(tm, D, col), pl.BlockSpec((1, D), lambda i: (0, 0))],
        out_specs=pl.BlockSpec((tm, D), lambda i: (i, 0)),
        out_shape=jax.ShapeDtypeStruct((L, D), BF16),
        compiler_params=_cparams(("parallel",)),
    )(x, g.reshape(1, D))


def _rms_bwd(dy, x, g, *, col=0, add=None, into=None, name):
    L = x.shape[0]
    D = g.shape[0]
    tm = ROW_T

    def body(*refs):
        dy_ref, x_ref, g_ref = refs[:3]
        add_ref = refs[3] if add is not None else None
        dx_ref, dg_ref = refs[-2:]
        i = pl.program_id(0)
        xv = x_ref[...]
        dyv = dy_ref[...]
        rstd = lax.rsqrt(jnp.mean(xv * xv, axis=-1, keepdims=True) + EPS)
        xhat = xv * rstd
        part = jnp.sum(dyv * xhat, axis=0, keepdims=True)

        @pl.when(i == 0)
        def _():
            dg_ref[...] = part

        @pl.when(i > 0)
        def _():
            dg_ref[...] += part

        dxh = dyv * g_ref[...]
        dx = rstd * (dxh - xhat * jnp.mean(dxh * xhat, axis=-1, keepdims=True))
        if add_ref is not None:
            dx = dx + add_ref[...]
        dx_ref[...] = dx.astype(dx_ref.dtype)

    row = pl.BlockSpec((tm, D), lambda i: (i, 0))
    in_specs = [row, _col_spec(tm, D, col), pl.BlockSpec((1, D), lambda i: (0, 0))]
    args = [dy, x, g.reshape(1, D)]
    aliases = {}
    if add is not None:
        in_specs.append(row)
        args.append(add)
    if into is not None:
        in_specs.append(pl.BlockSpec(memory_space=pl.ANY))
        args.append(into)
        aliases = {len(args) - 1: 0}
        dx_spec, dx_shape = _col_spec(tm, D, col), jax.ShapeDtypeStruct(into.shape, into.dtype)
    else:
        dx_spec, dx_shape = row, jax.ShapeDtypeStruct((L, D), F32)
    dx, dg = pl.pallas_call(
        body, name=name, grid=(L // tm,),
        in_specs=in_specs,
        out_specs=[dx_spec, pl.BlockSpec((1, D), lambda i: (0, 0))],
        out_shape=[dx_shape, jax.ShapeDtypeStruct((1, D), F32)],
        input_output_aliases=aliases,
        compiler_params=_cparams(("arbitrary",)),
    )(*args)
    return dx, dg.reshape(D)


def _loss_head(h, g, target, *, name):
    L, D = h.shape
    nb = L // BLK

    def body(h_ref, g_ref, t_ref, loss_ref, dh_ref, dg_ref):
        i = pl.program_id(0)

        @pl.when(i == 0)
        def _():
            loss_ref[...] = jnp.zeros_like(loss_ref)
            dg_ref[...] = jnp.zeros_like(dg_ref)
            dh_ref[...] = jnp.zeros_like(dh_ref)

        @pl.when(i > 0)
        def _():
            xv = h_ref[...]
            gv = g_ref[...]
            rstd = lax.rsqrt(jnp.mean(xv * xv, axis=-1, keepdims=True) + EPS)
            xhat = xv * rstd
            err = xhat * gv - t_ref[...]
            row = jnp.mean(err * err, axis=-1, keepdims=True)
            loss_ref[...] += 0.5 * jnp.sum(row, axis=0, keepdims=True)
            dy = err * (1.0 / D)
            dg_ref[...] += jnp.sum(dy * xhat, axis=0, keepdims=True)
            dxh = dy * gv
            dh_ref[...] = rstd * (dxh - xhat * jnp.mean(dxh * xhat, axis=-1, keepdims=True))

    loss, dh, dg = pl.pallas_call(
        body, name=name, grid=(nb,),
        in_specs=[pl.BlockSpec((BLK, D), lambda i: (i, 0)), pl.BlockSpec((1, D), lambda i: (0, 0)),
                  pl.BlockSpec((BLK, D), lambda i: (jnp.maximum(i - 1, 0), 0))],
        out_specs=[pl.BlockSpec((1, 128), lambda i: (0, 0)), pl.BlockSpec((BLK, D), lambda i: (i, 0)),
                   pl.BlockSpec((1, D), lambda i: (0, 0))],
        out_shape=[jax.ShapeDtypeStruct((1, 128), F32), jax.ShapeDtypeStruct((L, D), F32),
                   jax.ShapeDtypeStruct((1, D), F32)],
        compiler_params=_cparams(("arbitrary",)),
    )(h, g.reshape(1, D), target)
    return loss, dh, dg.reshape(D)


_NT = (((1,), (1,)), ((), ()))
_TN = (((0,), (0,)), ((), ()))


def _attn_fwd(q, k, v, kbias, *, T, window=False, sink=None, side=None, name):
    sn = 0 if side is None else side.n
    H, L, dk = q.shape
    Hkv = k.shape[0]
    dv = v.shape[2]
    G = H // Hkv
    nt = L // T
    Hb = kbias.shape[0]
    reps = T // LANES
    assert not window or T == WINDOW
    HP = G if G > 1 else ATT_HP
    NS = 1 if G > 1 else HP
    R = HP * T // NS
    HKV = HP // G
    HB = HP if Hb > 1 else 1
    assert G == 1 or Hb == 1

    def body(*refs):
        q_ref, k_ref, v_ref, kb_ref = refs[:4]
        n = 4
        if sink is not None:
            sk_ref = refs[n]
            n += 1
        side_in = refs[n:n + sn]
        n += sn
        o_ref, lse_ref = refs[n:n + 2]
        side_out = refs[n + 2:n + 2 + sn]
        n += 2 + sn
        m_scs, l_scs, acc_scs, buf_a, buf_b = (refs[n + t * NS:n + (t + 1) * NS] for t in range(5))
        i = pl.program_id(1)
        if side is not None:
            _ride(side, [*side_in, *side_out, *refs[n + 5 * NS:]], pl.program_id(0) * nt + i, (H // HP) * nt)
        for a in range(NS):
            if sink is not None:
                sk = [jnp.broadcast_to(sk_ref[b, :, 0:1], (T, LANES)) for b in range(HP)]
                m_scs[a][...] = jnp.concatenate(sk, axis=0) if G > 1 else sk[a]
                l_scs[a][...] = jnp.ones((R, LANES), F32)
            else:
                m_scs[a][...] = jnp.full((R, LANES), NEG, F32)
                l_scs[a][...] = jnp.zeros((R, LANES), F32)
            acc_scs[a][...] = jnp.zeros((R, dv), F32)
        row = lax.broadcasted_iota(jnp.int32, (R, T), 0) & (T - 1) if G > 1 else \
            lax.broadcasted_iota(jnp.int32, (R, T), 0)
        col = lax.broadcasted_iota(jnp.int32, (R, T), 1)

        def logits(a, j):
            rows = pl.ds(pl.multiple_of(j * T, T), T)
            qv = q_ref[...].reshape(R, dk) if G > 1 else q_ref[a]
            s = lax.dot_general(qv, k_ref[a, rows, :], _NT, preferred_element_type=F32)
            return s - kb_ref[a if HB > 1 else 0, j]

        def update(a, s, j, kind):
            rows = pl.ds(pl.multiple_of(j * T, T), T)
            m_sc, l_sc, acc_sc = m_scs[a], l_scs[a], acc_scs[a]
            if kind == "diag":
                s = jnp.where(row >= col, s, NEG)
            elif kind == "prev":
                s = jnp.where((col > row) & (i > 0), s, NEG)
            m_prev = m_sc[...]
            m_new = jnp.maximum(m_prev, jnp.max(s, axis=1, keepdims=True))
            alpha = jnp.exp(m_prev - m_new)
            p = jnp.exp(s - jnp.tile(m_new, (1, reps)))
            l_sc[...] = alpha * l_sc[...] + jnp.sum(p, axis=1, keepdims=True)
            acc_sc[...] = alpha[:, :dv] * acc_sc[...] + jnp.dot(p.astype(BF16), v_ref[a, rows, :],
                                                                preferred_element_type=F32)
            m_sc[...] = m_new

        if window:
            before = jnp.maximum(i - 1, 0)
            s_prev = [logits(a, before) for a in range(NS)]
            s_diag = [logits(a, i) for a in range(NS)]
            for a in range(NS):
                update(a, s_prev[a], before, "prev")
                update(a, s_diag[a], i, "diag")
        else:
            def fill(buf, j):
                for a in range(NS):
                    buf[a][...] = logits(a, j)

            def drain(buf, j, kind):
                for a in range(NS):
                    update(a, buf[a][...], j, kind)

            fill(buf_a, 0)

            def pair(t, c):
                fill(buf_b, 2 * t + 1)
                drain(buf_a, 2 * t, "full")
                fill(buf_a, 2 * t + 2)
                drain(buf_b, 2 * t + 1, "full")
                return c

            lax.fori_loop(0, i // 2, pair, 0)

            @pl.when(i % 2 == 1)
            def _():
                fill(buf_b, i)
                drain(buf_a, i - 1, "full")
                drain(buf_b, i, "diag")

            @pl.when(i % 2 == 0)
            def _():
                drain(buf_a, i, "diag")

        for a in range(NS):
            lv = l_scs[a][...]
            ov = acc_scs[a][...] / lv[:, :dv]
            lsev = (m_scs[a][...] + jnp.log(lv))[:, 0:1]
            if G > 1:
                o_ref[...] = ov.reshape(HP, T, dv)
                lse_ref[...] = lsev.reshape(HP, T, 1)
            else:
                o_ref[a] = ov
                lse_ref[a] = lsev

    in_specs = [pl.BlockSpec((HP, T, dk), lambda h, i: (h, i, 0)),
                pl.BlockSpec((HKV, L, dk), lambda h, i: (h, 0, 0)),
                pl.BlockSpec((HKV, L, dv), lambda h, i: (h, 0, 0)),
                pl.BlockSpec((HB, nt, 1, T), lambda h, i: (h if Hb > 1 else 0, 0, 0, 0))]
    args = [q, k, v, kbias]
    if sink is not None:
        in_specs += [pl.BlockSpec((HP, 1, LANES), lambda h, i: (h, 0, 0))]
        args += [sink]
    any_spec = pl.BlockSpec(memory_space=pl.ANY)
    return pl.pallas_call(
        body, name=name, grid=(H // HP, nt),
        in_specs=in_specs + [any_spec] * sn,
        out_specs=[pl.BlockSpec((HP, T, dv), lambda h, i: (h, i, 0)),
                   pl.BlockSpec((HP, T, 1), lambda h, i: (h, i, 0))] + [any_spec] * sn,
        out_shape=[jax.ShapeDtypeStruct((H, L, dv), F32), jax.ShapeDtypeStruct((H, L, 1), F32)]
        + ([] if side is None else side.out_shape),
        scratch_shapes=[pltpu.VMEM((R, LANES), F32)] * (2 * NS) + [pltpu.VMEM((R, dv), F32)] * NS
        + [pltpu.VMEM((R, T), F32)] * (2 * NS) + ([] if side is None else side.scratch),
        compiler_params=_side_cparams(side, ("parallel", "arbitrary")),
    )(*args, *([] if side is None else side.arrs))


def _attn_bwd_t(q, k, v, kbias_col, lse_row, delta_row, do, *, T, fox=False, side=None, name):
    sn = 0 if side is None else side.n
    H, L, dk = q.shape
    dv = v.shape[2]
    nt = L // T
    Hb = kbias_col.shape[0]

    def body(*refs):
        q_ref, k_ref, v_ref, kb_ref, lse_ref, dl_ref, do_ref = refs[:7]
        side_in = refs[7:7 + sn]
        n = 7 + sn
        dq_ref, dk_ref, dv_ref = refs[n:n + 3]
        n += 3
        if fox:
            dcq_ref, dck_ref = refs[n:n + 2]
            n += 2
        side_out = refs[n:n + sn]
        n += sn
        buf_a, buf_b = refs[n:n + 2], refs[n + 2:n + 4]
        j = pl.program_id(1)
        if side is not None:
            _ride(side, [*side_in, *side_out, *refs[n + 4:]], pl.program_id(0) * nt + j, H * nt)

        @pl.when(j == 0)
        def _():
            dq_ref[...] = jnp.zeros_like(dq_ref)
            if fox:
                dcq_ref[...] = jnp.zeros_like(dcq_ref)

        dk_ref[...] = jnp.zeros_like(dk_ref)
        dv_ref[...] = jnp.zeros_like(dv_ref)
        if fox:
            dck_ref[...] = jnp.zeros_like(dck_ref)
        kb = k_ref[0]
        vb = v_ref[0]
        kbias_j = kb_ref[0]
        key = lax.broadcasted_iota(jnp.int32, (T, T), 0)
        qry = lax.broadcasted_iota(jnp.int32, (T, T), 1)

        def fill(buf, i):
            rows = pl.ds(pl.multiple_of(i * T, T), T)
            buf[0][...] = lax.dot_general(kb, q_ref[0, rows, :], _NT, preferred_element_type=F32) - kbias_j
            buf[1][...] = lax.dot_general(vb, do_ref[0, rows, :], _NT, preferred_element_type=F32)

        def drain(buf, i, kind):
            rows = pl.ds(pl.multiple_of(i * T, T), T)
            st = buf[0][...]
            if kind == "diag":
                st = jnp.where(key <= qry, st, NEG)
            pt = jnp.exp(st - lse_ref[0, i])
            dst = pt * (buf[1][...] - dl_ref[0, i])
            dsb = dst.astype(BF16)
            dv_ref[0] += jnp.dot(pt.astype(BF16), do_ref[0, rows, :], preferred_element_type=F32)
            dk_ref[0] += jnp.dot(dsb, q_ref[0, rows, :], preferred_element_type=F32)
            dq_ref[0, rows, :] += lax.dot_general(dsb, kb, _TN, preferred_element_type=F32)
            if fox:
                dcq_ref[0, i] += jnp.sum(dst, axis=0, keepdims=True)
                dck_ref[0] += -jnp.sum(dst, axis=1, keepdims=True)

        first = j + 1
        rest = nt - first
        fill(buf_a, j)
        fill(buf_b, jnp.minimum(first, nt - 1))
        drain(buf_a, j, "diag")

        def pair(t, c):
            i0 = first + 2 * t
            fill(buf_a, i0 + 1)
            drain(buf_b, i0, "full")
            fill(buf_b, jnp.minimum(i0 + 2, nt - 1))
            drain(buf_a, i0 + 1, "full")
            return c

        lax.fori_loop(0, rest // 2, pair, 0)

        @pl.when(rest % 2 == 1)
        def _():
            drain(buf_b, nt - 1, "full")

    rows_spec = pl.BlockSpec((1, nt, 1, T), lambda h, j: (h, 0, 0, 0))
    in_specs = [pl.BlockSpec((1, L, dk), lambda h, j: (h, 0, 0)),
                pl.BlockSpec((1, T, dk), lambda h, j: (h, j, 0)),
                pl.BlockSpec((1, T, dv), lambda h, j: (h, j, 0)),
                pl.BlockSpec((1, T, 1), lambda h, j: (h if Hb > 1 else 0, j, 0)),
                rows_spec, rows_spec,
                pl.BlockSpec((1, L, dv), lambda h, j: (h, 0, 0))]
    out_specs = [pl.BlockSpec((1, L, dk), lambda h, j: (h, 0, 0)),
                 pl.BlockSpec((1, T, dk), lambda h, j: (h, j, 0)),
                 pl.BlockSpec((1, T, dv), lambda h, j: (h, j, 0))]
    out_shape = [jax.ShapeDtypeStruct((H, L, dk), F32), jax.ShapeDtypeStruct((H, L, dk), F32),
                 jax.ShapeDtypeStruct((H, L, dv), F32)]
    if fox:
        out_specs += [rows_spec, pl.BlockSpec((1, T, 1), lambda h, j: (h, j, 0))]
        out_shape += [jax.ShapeDtypeStruct((H, nt, 1, T), F32), jax.ShapeDtypeStruct((H, L, 1), F32)]
    any_spec = pl.BlockSpec(memory_space=pl.ANY)
    outs = pl.pallas_call(
        body, name=name, grid=(H, nt),
        in_specs=in_specs + [any_spec] * sn, out_specs=out_specs + [any_spec] * sn,
        out_shape=out_shape + ([] if side is None else side.out_shape),
        scratch_shapes=[pltpu.VMEM((T, T), F32)] * 4 + ([] if side is None else side.scratch),
        compiler_params=_side_cparams(side, ("parallel", "arbitrary")),
    )(q, k, v, kbias_col, lse_row, delta_row, do, *([] if side is None else side.arrs))
    main, rest = outs[:len(outs) - sn], outs[len(outs) - sn:]
    return (*(main if fox else (*main, None, None)), *rest)


def _attn_bwd_window(q, k, v, kbias, o, lse, do, *, name):
    H, L, dk = q.shape
    Hkv = k.shape[0]
    dv = v.shape[2]
    G = H // Hkv
    T = WINDOW
    nt = L // T
    R = G * T

    def body(q_ref, kc_ref, kp_ref, vc_ref, vp_ref, kb_ref, o_ref, lse_ref, do_ref, dq_ref, dk_ref, dv_ref,
             dk_sc, dv_sc):
        i = pl.program_id(1)
        row = lax.broadcasted_iota(jnp.int32, (R, T), 0) & (T - 1)
        col = lax.broadcasted_iota(jnp.int32, (R, T), 1)

        @pl.when(i == 0)
        def _():
            dk_sc[...] = jnp.zeros_like(dk_sc)
            dv_sc[...] = jnp.zeros_like(dv_sc)

        @pl.when(i == nt)
        def _():
            dk_ref[0] = dk_sc[...]
            dv_ref[0] = dv_sc[...]

        @pl.when(i < nt)
        def _():
            qb = q_ref[...].reshape(R, dk)
            dof = do_ref[...].reshape(R, dv)
            dob = dof.astype(BF16)
            lse_c = lse_ref[...].reshape(R, 1)
            delta = jnp.sum(dof * o_ref[...].reshape(R, dv), axis=1, keepdims=True)

            def grads(kt, vt, kbias_j, mask):
                s = lax.dot_general(qb, kt, _NT, preferred_element_type=F32) - kbias_j
                p = jnp.exp(jnp.where(mask, s, NEG) - lse_c)
                dp = lax.dot_general(dob, vt, _NT, preferred_element_type=F32)
                ds = (p * (dp - delta)).astype(BF16)
                return (jnp.dot(ds, kt, preferred_element_type=F32),
                        lax.dot_general(ds, qb, _TN, preferred_element_type=F32),
                        lax.dot_general(p.astype(BF16), dob, _TN, preferred_element_type=F32))

            ip = jnp.maximum(i - 1, 0)
            dq_p, dk_p, dv_p = grads(kp_ref[0], vp_ref[0], kb_ref[0, ip], (col > row) & (i > 0))
            dq_c, dk_c, dv_c = grads(kc_ref[0], vc_ref[0], kb_ref[0, i], row >= col)
            dq_ref[...] = (dq_p + dq_c).reshape(G, T, dk)
            dk_ref[0] = dk_sc[...] + dk_p
            dv_ref[0] = dv_sc[...] + dv_p
            dk_sc[...] = dk_c
            dv_sc[...] = dv_c

    def cur(i):
        return jnp.minimum(i, nt - 1)

    def prev(i):
        return jnp.maximum(jnp.minimum(i, nt - 1) - 1, 0)

    def written(i):
        return jnp.maximum(i - 1, 0)

    qs = lambda d: pl.BlockSpec((G, T, d), lambda h, i: (h, cur(i), 0))
    return pl.pallas_call(
        body, name=name, grid=(Hkv, nt + 1),
        in_specs=[qs(dk),
                  pl.BlockSpec((1, T, dk), lambda h, i: (h, cur(i), 0)),
                  pl.BlockSpec((1, T, dk), lambda h, i: (h, prev(i), 0)),
                  pl.BlockSpec((1, T, dv), lambda h, i: (h, cur(i), 0)),
                  pl.BlockSpec((1, T, dv), lambda h, i: (h, prev(i), 0)),
                  pl.BlockSpec((1, nt, 1, T), lambda h, i: (0, 0, 0, 0)),
                  qs(dv), qs(1), qs(dv)],
        out_specs=[qs(dk),
                   pl.BlockSpec((1, T, dk), lambda h, i: (h, written(i), 0)),
                   pl.BlockSpec((1, T, dv), lambda h, i: (h, written(i), 0))],
        out_shape=[jax.ShapeDtypeStruct((H, L, dk), F32), jax.ShapeDtypeStruct((Hkv, L, dk), F32),
                   jax.ShapeDtypeStruct((Hkv, L, dv), F32)],
        scratch_shapes=[pltpu.VMEM((T, dk), F32), pltpu.VMEM((T, dv), F32)],
        compiler_params=_cparams(("parallel", "arbitrary")),
    )(q, k, k, v, v, kbias, o, lse, do)


def _adamw(w, gparts, m, v, *, name):
    n, R, C = gparts.shape
    tr = _pick(R, (128, 64, 32, 16, 8))
    c1 = 1.0 - ADAM_B1 ** ADAM_STEP
    c2 = 1.0 - ADAM_B2 ** ADAM_STEP

    def body(w_ref, g_ref, m_ref, v_ref, go_ref, d_ref, mo_ref, vo_ref):
        g = g_ref[0].astype(F32)
        for t in range(1, n):
            g = g + g_ref[t].astype(F32)
        mn = ADAM_B1 * m_ref[...] + (1.0 - ADAM_B1) * g
        vn = ADAM_B2 * v_ref[...] + (1.0 - ADAM_B2) * (g * g)
        go_ref[...] = g
        mo_ref[...] = mn
        vo_ref[...] = vn
        d_ref[...] = -ADAM_LR * ((mn / c1) / (jnp.sqrt(vn / c2) + ADAM_EPS) + ADAM_WD * w_ref[...])

    spec = pl.BlockSpec((tr, C), lambda i: (i, 0))
    return pl.pallas_call(
        body, name=name, grid=(R // tr,),
        in_specs=[spec, pl.BlockSpec((n, tr, C), lambda i: (0, i, 0)), spec, spec],
        out_specs=[spec] * 4,
        out_shape=[jax.ShapeDtypeStruct((R, C), F32)] * 4,
        compiler_params=_cparams(("parallel",)),
    )(w, gparts, m, v)


def _me():
    return lax.axis_index("x"), lax.axis_index("y"), lax.axis_index("c")


class _CommJob:
    def __init__(self, arrs):
        self.arrs = list(arrs)
        self.n = len(arrs)
        self.scratch = [pltpu.SemaphoreType.DMA((self.n, 7)), pltpu.SemaphoreType.DMA((self.n, 7)),
                        pltpu.SemaphoreType.DMA((self.n,))]

    def bind(self, refs):
        n = self.n
        self.ins, self.outs = refs[:n], refs[n:2 * n]
        self.send_sems, self.recv_sems, self.local_sems = refs[2 * n:2 * n + 3]

    def middle(self):
        pass


class _Gather(_CommJob):
    def __init__(self, arrs):
        super().__init__(arrs)
        self.out_shape = [jax.ShapeDtypeStruct((N_DEV, *a.shape), a.dtype) for a in arrs]

    def _where(self):
        x, y, c = _me()
        return (x, y, c), (x, y, 1 - c), [(1 - x, y), (x, 1 - y), (1 - x, 1 - y)], c

    def _copy(self, t, k, block, to, src=None):
        dst = self.outs[t].at[4 * block[0] + 2 * block[1] + block[2]]
        return pltpu.make_async_remote_copy(
            src_ref=dst if src is None else src, dst_ref=dst,
            send_sem=self.send_sems.at[t, k], recv_sem=self.recv_sems.at[t, k],
            device_id=to, device_id_type=MESH)

    def _mine(self, t, me):
        return pltpu.make_async_copy(self.ins[t], self.outs[t].at[4 * me[0] + 2 * me[1] + me[2]],
                                     self.local_sems.at[t])

    def _first(self, t, me, sibling, chips, c):
        return [self._copy(t, 0, me, sibling, src=self.ins[t])] + \
               [self._copy(t, 1 + j, me, (*chip, c), src=self.ins[t]) for j, chip in enumerate(chips)]

    def start(self):
        me, sibling, chips, c = self._where()
        for t in range(self.n):
            self._mine(t, me).start()
            for cp in self._first(t, me, sibling, chips, c):
                cp.start()

    def middle(self):
        me, sibling, chips, c = self._where()
        for j, chip in enumerate(chips):
            for t in range(self.n):
                self._copy(t, 1 + j, (*chip, c), me).wait_recv()
                self._copy(t, 4 + j, (*chip, c), sibling).start()

    def finish(self):
        me, sibling, chips, c = self._where()
        for t in range(self.n):
            self._copy(t, 0, sibling, me).wait_recv()
            for j, chip in enumerate(chips):
                self._copy(t, 4 + j, (*chip, 1 - c), me).wait_recv()
        for t in range(self.n):
            for cp in self._first(t, me, sibling, chips, c):
                cp.wait_send()
            for j, chip in enumerate(chips):
                self._copy(t, 4 + j, (*chip, c), sibling).wait_send()
            self._mine(t, me).wait()


class _Exchange(_CommJob):
    def __init__(self, arrs):
        super().__init__(arrs)
        self.out_shape = [jax.ShapeDtypeStruct(a.shape, a.dtype) for a in arrs]

    def _copies(self, t):
        x, y, c = _me()
        my_idx = 4 * x + 2 * y + c
        pairs = []
        for k in range(1, N_DEV):
            peer = (x ^ ((k >> 2) & 1), y ^ ((k >> 1) & 1), c ^ (k & 1))
            peer_idx = 4 * peer[0] + 2 * peer[1] + peer[2]
            sems = dict(send_sem=self.send_sems.at[t, k - 1], recv_sem=self.recv_sems.at[t, k - 1],
                        device_id=peer, device_id_type=MESH)
            pairs.append((pltpu.make_async_remote_copy(src_ref=self.ins[t].at[peer_idx],
                                                       dst_ref=self.outs[t].at[my_idx], **sems),
                          pltpu.make_async_remote_copy(src_ref=self.ins[t].at[peer_idx],
                                                       dst_ref=self.outs[t].at[peer_idx], **sems)))
        return pairs

    def _mine(self, t):
        x, y, c = _me()
        my_idx = 4 * x + 2 * y + c
        return pltpu.make_async_copy(self.ins[t].at[my_idx], self.outs[t].at[my_idx], self.local_sems.at[t])

    def start(self):
        for t in range(self.n):
            self._mine(t).start()
            for snd, _ in self._copies(t):
                snd.start()

    def finish(self):
        for t in range(self.n):
            pairs = self._copies(t)
            for _, rcv in pairs:
                rcv.wait_recv()
            for snd, _ in pairs:
                snd.wait_send()
            self._mine(t).wait()


def _comm(job, *, name):
    def body(*refs):
        job.bind(refs)
        job.start()
        job.middle()
        job.finish()

    any_spec = pl.BlockSpec(memory_space=pl.ANY)
    return pl.pallas_call(
        body, name=name,
        in_specs=[any_spec] * job.n, out_specs=[any_spec] * job.n,
        out_shape=job.out_shape, scratch_shapes=job.scratch,
        compiler_params=pltpu.CompilerParams(has_side_effects=True),
    )(*job.arrs)


def _ride(job, refs, step, total):
    job.bind(refs)

    @pl.when(step == 0)
    def _():
        job.start()

    @pl.when(step == (total * 3) // 5)
    def _():
        job.middle()

    @pl.when(step == total - 1)
    def _():
        job.finish()


SCALE_A = DH ** -0.5
SCALE_B = (MLA_NOPE + MLA_ROPE) ** -0.5
SCALE_C = DH ** -0.5
DQK_B = MLA_NOPE + MLA_ROPE


def _rope_tables(L, half, width):
    pos = (jnp.arange(L) - PAD).astype(F32)
    lane = jnp.arange(width)
    inv = ROPE_THETA ** (-(lane % half).astype(F32) / half)
    ang = pos[:, None] * inv[None, :]
    sign = jnp.where(lane % (2 * half) < half, -1.0, 1.0).astype(F32)
    return jnp.cos(ang), jnp.sin(ang) * sign[None, :]


def _rope_lanes(x, cos, sin, half):
    W = x.shape[1]
    lane = lax.broadcasted_iota(jnp.int32, x.shape, 1)
    first = (lane & (2 * half - 1)) < half
    partner = jnp.where(first, pltpu.roll(x, W - half, 1), pltpu.roll(x, half, 1))
    return x * cos + partner * sin


def _tile_lanes(t, width):
    return t if t.shape[1] == width else jnp.tile(t, (1, width // t.shape[1]))


def _split(x, H, d, dst):
    for h in range(H):
        dst[h] = x[:, d * h:d * (h + 1)].astype(dst.dtype)


def _join(src, H):
    return jnp.concatenate([src[h] for h in range(H)], axis=1)


def _head_spec(H, tm, d):
    return pl.BlockSpec((H, tm, d), lambda i: (0, i, 0))


def _prep_a(proj, *, name):
    L = proj.shape[0]
    tm = ROW_T

    def body(x_ref, qo, ko, vo):
        _split(x_ref[:, 0:512] * SCALE_A, HEADS, DH, qo)
        _split(x_ref[:, 512:1024], HEADS, DH, ko)
        _split(x_ref[:, 1024:1536], HEADS, DH, vo)

    return pl.pallas_call(
        body, name=name, grid=(L // tm,),
        in_specs=[_col_spec(tm, 1536, QKVA)],
        out_specs=[_head_spec(HEADS, tm, DH)] * 3,
        out_shape=[jax.ShapeDtypeStruct((HEADS, L, DH), BF16)] * 3,
        compiler_params=_cparams(("parallel",)),
    )(proj)


def _unprep_a(dq, dk, dv, dproj, *, name):
    L = dq.shape[1]
    tm = ROW_T

    def body(dq_ref, dk_ref, dv_ref, _, dp_ref):
        dp_ref[:, 0:512] = (_join(dq_ref, HEADS) * SCALE_A).astype(BF16)
        dp_ref[:, 512:1024] = _join(dk_ref, HEADS).astype(BF16)
        dp_ref[:, 1024:1536] = _join(dv_ref, HEADS).astype(BF16)

    hs = _head_spec(HEADS, tm, DH)
    return pl.pallas_call(
        body, name=name, grid=(L // tm,),
        in_specs=[hs, hs, hs, pl.BlockSpec(memory_space=pl.ANY)],
        out_specs=_col_spec(tm, 1536, QKVA),
        out_shape=jax.ShapeDtypeStruct(dproj.shape, dproj.dtype),
        input_output_aliases={3: 0},
        compiler_params=_cparams(("parallel",)),
    )(dq, dk, dv, dproj)


def _prep_c(proj, tab, *, name):
    L = proj.shape[0]
    tm = ROW_T

    def body(x_ref, cos_ref, sin_ref, qo, ko, vo):
        cos, sin = cos_ref[...], sin_ref[...]
        q = _rope_lanes(x_ref[:, 0:512], _tile_lanes(cos, 512), _tile_lanes(sin, 512), DH // 2)
        _split(q * SCALE_C, HEADS, DH, qo)
        _split(_rope_lanes(x_ref[:, 512:640], cos, sin, DH // 2), SWA_KV_HEADS, DH, ko)
        _split(x_ref[:, 640:768], SWA_KV_HEADS, DH, vo)

    t128 = pl.BlockSpec((tm, LANES), lambda i: (i, 0))
    return pl.pallas_call(
        body, name=name, grid=(L // tm,),
        in_specs=[_col_spec(tm, 768, QKVC), t128, t128],
        out_specs=[_head_spec(HEADS, tm, DH), _head_spec(SWA_KV_HEADS, tm, DH), _head_spec(SWA_KV_HEADS, tm, DH)],
        out_shape=[jax.ShapeDtypeStruct((HEADS, L, DH), BF16), jax.ShapeDtypeStruct((SWA_KV_HEADS, L, DH), BF16),
                   jax.ShapeDtypeStruct((SWA_KV_HEADS, L, DH), BF16)],
        compiler_params=_cparams(("parallel",)),
    )(proj, *tab)


def _unprep_c(dq, dk, dv, tab, dproj, *, name):
    L = dq.shape[1]
    tm = ROW_T

    def body(dq_ref, dk_ref, dv_ref, cos_ref, sin_ref, _, dp_ref):
        cos, nsin = cos_ref[...], -sin_ref[...]
        dqv = _join(dq_ref, HEADS) * SCALE_C
        dp_ref[:, 0:512] = _rope_lanes(dqv, _tile_lanes(cos, 512), _tile_lanes(nsin, 512), DH // 2).astype(BF16)
        dp_ref[:, 512:640] = _rope_lanes(_join(dk_ref, SWA_KV_HEADS), cos, nsin, DH // 2).astype(BF16)
        dp_ref[:, 640:768] = _join(dv_ref, SWA_KV_HEADS).astype(BF16)

    hs = _head_spec(HEADS, tm, DH)
    hkv = _head_spec(SWA_KV_HEADS, tm, DH)
    t128 = pl.BlockSpec((tm, LANES), lambda i: (i, 0))
    return pl.pallas_call(
        body, name=name, grid=(L // tm,),
        in_specs=[hs, hkv, hkv, t128, t128, pl.BlockSpec(memory_space=pl.ANY)],
        out_specs=_col_spec(tm, 768, QKVC),
        out_shape=jax.ShapeDtypeStruct(dproj.shape, dproj.dtype),
        input_output_aliases={5: 0},
        compiler_params=_cparams(("parallel",)),
    )(dq, dk, dv, *tab, dproj)


def _q_tables(cos, sin, on_ref):
    on = on_ref[...] > 0.5
    width = on_ref.shape[1]
    return jnp.where(on, _tile_lanes(cos, width), 1.0), jnp.where(on, _tile_lanes(sin, width), 0.0)


def _prep_b(qbm, kvbm, proj, tab_k, q_rotary, *, name):
    L = proj.shape[0]
    tm = ROW_T

    def body(q_ref, kv_ref, kr_ref, ck_ref, sk_ref, on_ref, qo, ko, vo):
        cq, sq = _q_tables(ck_ref[...], sk_ref[...], on_ref)
        q = _rope_lanes(q_ref[...], cq, sq, MLA_ROPE // 2) * SCALE_B
        _split(q, HEADS, DQK_B, qo)
        kr = _rope_lanes(kr_ref[...], ck_ref[...], sk_ref[...], MLA_ROPE // 2)[:, :MLA_ROPE].astype(BF16)
        kv = kv_ref[...]
        for h in range(HEADS):
            ko[h] = jnp.concatenate([kv[:, 128 * h:128 * h + MLA_NOPE].astype(BF16), kr], axis=1)
            vo[h] = kv[:, 128 * h + MLA_NOPE:128 * (h + 1)].astype(BF16)

    t128 = pl.BlockSpec((tm, LANES), lambda i: (i, 0))
    t768 = pl.BlockSpec((tm, 768), lambda i: (i, 0))
    return pl.pallas_call(
        body, name=name, grid=(L // tm,),
        in_specs=[t768, pl.BlockSpec((tm, 1024), lambda i: (i, 0)), _col_spec(tm, 128, KR), t128, t128,
                  pl.BlockSpec((1, 768), lambda i: (0, 0))],
        out_specs=[_head_spec(HEADS, tm, DQK_B), _head_spec(HEADS, tm, DQK_B), _head_spec(HEADS, tm, DH)],
        out_shape=[jax.ShapeDtypeStruct((HEADS, L, DQK_B), BF16), jax.ShapeDtypeStruct((HEADS, L, DQK_B), BF16),
                   jax.ShapeDtypeStruct((HEADS, L, DH), BF16)],
        compiler_params=_cparams(("parallel",)),
    )(qbm, kvbm, proj, *tab_k, q_rotary)


def _unprep_b(dq, dk, dv, tab_k, q_rotary, dproj, *, name):
    L = dq.shape[1]
    tm = ROW_T

    def body(dq_ref, dk_ref, dv_ref, ck_ref, sk_ref, on_ref, _, dp_kr, dqo, dkvo):
        cq, sq = _q_tables(ck_ref[...], sk_ref[...], on_ref)
        dqv = _join(dq_ref, HEADS) * SCALE_B
        dqo[...] = _rope_lanes(dqv, cq, -sq, MLA_ROPE // 2).astype(BF16)
        parts = []
        dkr = None
        for h in range(HEADS):
            dkh = dk_ref[h]
            parts += [dkh[:, :MLA_NOPE], dv_ref[h]]
            r = dkh[:, MLA_NOPE:]
            dkr = r if dkr is None else dkr + r
        dkvo[...] = jnp.concatenate(parts, axis=1).astype(BF16)
        dkr = jnp.concatenate([dkr, jnp.zeros((tm, LANES - MLA_ROPE), F32)], axis=1)
        dp_kr[...] = _rope_lanes(dkr, ck_ref[...], -sk_ref[...], MLA_ROPE // 2).astype(BF16)

    t128 = pl.BlockSpec((tm, LANES), lambda i: (i, 0))
    t768 = pl.BlockSpec((tm, 768), lambda i: (i, 0))
    hq = _head_spec(HEADS, tm, DQK_B)
    return pl.pallas_call(
        body, name=name, grid=(L // tm,),
        in_specs=[hq, hq, _head_spec(HEADS, tm, DH), t128, t128, pl.BlockSpec((1, 768), lambda i: (0, 0)),
                  pl.BlockSpec(memory_space=pl.ANY)],
        out_specs=[_col_spec(tm, 128, KR), t768, pl.BlockSpec((tm, 1024), lambda i: (i, 0))],
        out_shape=[jax.ShapeDtypeStruct(dproj.shape, dproj.dtype), jax.ShapeDtypeStruct((L, 768), BF16),
                   jax.ShapeDtypeStruct((L, 1024), BF16)],
        input_output_aliases={6: 0},
        compiler_params=_cparams(("parallel",)),
    )(dq, dk, dv, *tab_k, q_rotary, dproj)


def _gate(y, proj, zcol, *, name):
    L = proj.shape[0]
    tm = ROW_T

    def body(y_ref, z_ref, u_ref):
        z = z_ref[...]
        u_ref[...] = (_join(y_ref, HEADS) * (z * jax.nn.sigmoid(z))).astype(BF16)

    return pl.pallas_call(
        body, name=name, grid=(L // tm,),
        in_specs=[_head_spec(HEADS, tm, DH), _col_spec(tm, 512, zcol)],
        out_specs=pl.BlockSpec((tm, 512), lambda i: (i, 0)),
        out_shape=jax.ShapeDtypeStruct((L, 512), BF16),
        compiler_params=_cparams(("parallel",)),
    )(y, proj)


def _gate_bwd(du, y, proj, zcol, dproj, *, delta_rows=False, name):
    L = proj.shape[0]
    tm = ROW_T
    assert not delta_rows or tm == ATT_T

    def body(du_ref, y_ref, z_ref, _, dz_ref, dy_ref, *dl_ref):
        z = z_ref[...]
        duv = du_ref[...]
        sg = jax.nn.sigmoid(z)
        yv = _join(y_ref, HEADS)
        dyv = duv * (z * sg)
        _split(dyv, HEADS, DH, dy_ref)
        dz_ref[...] = (duv * yv * (sg * (1.0 + z * (1.0 - sg)))).astype(BF16)
        if delta_rows:
            prod = dyv * yv
            ones = jnp.ones((8, DH), F32)
            for h in range(HEADS):
                sums = lax.dot_general(ones, prod[:, DH * h:DH * (h + 1)], _NT, preferred_element_type=F32)
                dl_ref[0][h, 0] = sums[0:1, :]

    hs = _head_spec(HEADS, tm, DH)
    out_specs = [_col_spec(tm, 512, zcol), hs]
    out_shape = [jax.ShapeDtypeStruct(dproj.shape, dproj.dtype),
                 jax.ShapeDtypeStruct((HEADS, L, DH), BF16 if delta_rows else F32)]
    if delta_rows:
        out_specs.append(pl.BlockSpec((HEADS, 1, 1, tm), lambda i: (0, i, 0, 0)))
        out_shape.append(jax.ShapeDtypeStruct((HEADS, L // tm, 1, tm), F32))
    return pl.pallas_call(
        body, name=name, grid=(L // tm,),
        in_specs=[pl.BlockSpec((tm, 512), lambda i: (i, 0)), hs, _col_spec(tm, 512, zcol),
                  pl.BlockSpec(memory_space=pl.ANY)],
        out_specs=out_specs, out_shape=out_shape,
        input_output_aliases={3: 0},
        compiler_params=_cparams(("parallel",)),
    )(du, y, proj, dproj)


MERGE_T = 192


def _merge(proj, pbs, *, name):
    L = proj.shape[0]
    tm = MERGE_T

    def body(g0, g1, g2, p0, p1, p2, o_ref):
        acc = None
        for g_ref, p_ref in ((g0, p0), (g1, p1), (g2, p2)):
            t = jax.nn.sigmoid(g_ref[...]) * p_ref[...]
            acc = t if acc is None else acc + t
        o_ref[...] = acc.astype(BF16)

    row = pl.BlockSpec((tm, D_MODEL), lambda i: (i, 0))
    return pl.pallas_call(
        body, name=name, grid=(L // tm,),
        in_specs=[_col_spec(tm, D_MODEL, GATES + n * D_MODEL) for n in range(N_BRANCH)] + [row] * N_BRANCH,
        out_specs=row, out_shape=jax.ShapeDtypeStruct((L, D_MODEL), BF16),
        compiler_params=_cparams(("parallel",)),
    )(proj, proj, proj, *pbs)


def _merge_bwd(dmerged, proj, pbs, dproj, *, name):
    L = proj.shape[0]
    tm = MERGE_T

    def body(dm_ref, g_ref, p0, p1, p2, _, dg_ref, dp0, dp1, dp2):
        dm = dm_ref[...]
        for n, (p_ref, dp_ref) in enumerate(((p0, dp0), (p1, dp1), (p2, dp2))):
            cols = slice(n * D_MODEL, (n + 1) * D_MODEL)
            sg = jax.nn.sigmoid(g_ref[:, cols])
            dp_ref[...] = (dm * sg).astype(BF16)
            dg_ref[:, cols] = (dm * p_ref[...] * (sg * (1.0 - sg))).astype(BF16)

    row = pl.BlockSpec((tm, D_MODEL), lambda i: (i, 0))
    gates = _col_spec(tm, N_BRANCH * D_MODEL, GATES)
    outs = pl.pallas_call(
        body, name=name, grid=(L // tm,),
        in_specs=[row, gates] + [row] * N_BRANCH + [pl.BlockSpec(memory_space=pl.ANY)],
        out_specs=[gates] + [row] * N_BRANCH,
        out_shape=[jax.ShapeDtypeStruct(dproj.shape, dproj.dtype)]
        + [jax.ShapeDtypeStruct((L, D_MODEL), BF16)] * N_BRANCH,
        input_output_aliases={5: 0},
        compiler_params=_cparams(("parallel",)),
    )(dmerged, proj, *pbs, dproj)
    return outs[0], outs[1:]


def _scan_rows(x, reverse):
    rows = lax.broadcasted_iota(jnp.int32, x.shape, 0)
    step = 1
    while step < BLK:
        if reverse:
            x = x + jnp.where(rows < BLK - step, pltpu.roll(x, BLK - step, 0), 0.0)
        else:
            x = x + jnp.where(rows >= step, pltpu.roll(x, step, 0), 0.0)
        step *= 2
    return x


def _forget_fwd(proj, b_f, *, name):
    L = proj.shape[0]
    nb = L // BLK

    def body(x_ref, b_ref, o_ref, carry):
        i = pl.program_id(0)

        @pl.when(i == 0)
        def _():
            carry[...] = jnp.zeros_like(carry)

        c = _scan_rows(jax.nn.log_sigmoid(x_ref[...] + b_ref[...]), False) + carry[...]
        carry[...] = c[BLK - 1:BLK, :]
        pos = i * BLK + lax.broadcasted_iota(jnp.int32, (HEADS, BLK), 1)
        o_ref[...] = c.T[:HEADS, :] + jnp.where(pos < PAD, BIG, 0.0)

    return pl.pallas_call(
        body, name=name, grid=(nb,),
        in_specs=[_col_spec(BLK, LANES, FA), pl.BlockSpec((1, LANES), lambda i: (0, 0))],
        out_specs=pl.BlockSpec((HEADS, BLK), lambda i: (0, i)),
        out_shape=jax.ShapeDtypeStruct((HEADS, L), F32),
        scratch_shapes=[pltpu.VMEM((1, LANES), F32)],
        compiler_params=_cparams(("arbitrary",)),
    )(proj, jnp.pad(b_f, (0, LANES - HEADS)).reshape(1, LANES))


def _forget_bwd(dct, proj, b_f, dproj, *, name):
    L = proj.shape[0]
    nb = L // BLK

    def body(d_ref, x_ref, b_ref, _, dp_ref, db_ref, carry):
        i = pl.program_id(0)

        @pl.when(i == 0)
        def _():
            carry[...] = jnp.zeros_like(carry)
            db_ref[...] = jnp.zeros_like(db_ref)

        d = jnp.concatenate([d_ref[...], jnp.zeros((BLK - HEADS, BLK), F32)], axis=0).T
        dlog = _scan_rows(d, True) + carry[...]
        carry[...] = dlog[0:1, :]
        lane = lax.broadcasted_iota(jnp.int32, (BLK, LANES), 1)
        daf = jnp.where(lane < HEADS, dlog * jax.nn.sigmoid(-(x_ref[...] + b_ref[...])), 0.0)
        db_ref[...] += jnp.sum(daf, axis=0, keepdims=True)
        dp_ref[...] = jnp.concatenate([daf, jnp.zeros((BLK, LANES), F32)], axis=1).astype(BF16)

    back = lambda i: nb - 1 - i
    dp, db = pl.pallas_call(
        body, name=name, grid=(nb,),
        in_specs=[pl.BlockSpec((HEADS, BLK), lambda i: (0, back(i))),
                  pl.BlockSpec((BLK, LANES), lambda i: (back(i), FA // LANES)),
                  pl.BlockSpec((1, LANES), lambda i: (0, 0)), pl.BlockSpec(memory_space=pl.ANY)],
        out_specs=[pl.BlockSpec((BLK, 2 * LANES), lambda i: (back(i), FA // (2 * LANES))),
                   pl.BlockSpec((1, LANES), lambda i: (0, 0))],
        out_shape=[jax.ShapeDtypeStruct(dproj.shape, dproj.dtype), jax.ShapeDtypeStruct((1, LANES), F32)],
        scratch_shapes=[pltpu.VMEM((1, LANES), F32)],
        input_output_aliases={3: 0},
        compiler_params=_cparams(("arbitrary",)),
    )(dct, proj, jnp.pad(b_f, (0, LANES - HEADS)).reshape(1, LANES), dproj)
    return dp, db[0, :HEADS]


def _key_bias(L, T, ct=None):
    padb = jnp.where(jnp.arange(L) < PAD, BIG, 0.0).astype(F32)[None]
    kb = padb if ct is None else ct + padb
    return kb.reshape(kb.shape[0], L // T, 1, T)


def _layer_fwd(h, w, tabs, l, side=None):
    tag = f"l{l}"
    L = h.shape[0]
    hn = _rms_fwd(h, w["norm_g"], name=f"{tag}_rms_in")
    if "late" in w:
        late_job, assemble = w["late"]
        proj, *late = _mm(hn, w["w_in"], side=late_job, name=f"{tag}_mm_in")
        w = {**{key: val for key, val in w.items() if key != "late"}, **assemble(*late)}
    else:
        proj = _mm(hn, w["w_in"], name=f"{tag}_mm_in")
    kb_a = _forget_fwd(proj, w["b_f"], name=f"{tag}_forget").reshape(HEADS, L // ATT_T, 1, ATT_T)
    ops_a = (*_prep_a(proj, name=f"{tag}_prep_a"), kb_a)
    ya, lsea, *side_out = _attn_fwd(*ops_a, T=ATT_T, side=side, name=f"{tag}_attn_a")
    cqn = _rms_fwd(proj, w["g_cq"], col=CQ, name=f"{tag}_rms_cq")
    ckvn = _rms_fwd(proj, w["g_ckv"], col=CKV, name=f"{tag}_rms_ckv")
    qbm = _mm(cqn, w["w_uq"], name=f"{tag}_mm_uq")
    kvbm = _mm(ckvn, w["w_ukv"], name=f"{tag}_mm_ukv")
    ops_b = (*_prep_b(qbm, kvbm, proj, tabs["bk"], tabs["bq"], name=f"{tag}_prep_b"), _key_bias(L, ATT_T))
    yb, lseb = _attn_fwd(*ops_b, T=ATT_T, name=f"{tag}_attn_b")
    ops_c = (*_prep_c(proj, tabs["c"], name=f"{tag}_prep_c"), _key_bias(L, WINDOW))
    sink = jnp.broadcast_to(w["sinks"][:, None, None], (HEADS, 1, LANES))
    yc, lsec = _attn_fwd(*ops_c, T=WINDOW, window=True, sink=sink, name=f"{tag}_attn_c")
    us = [_gate(y, proj, zcol, name=f"{tag}_gate{n}") for n, (y, zcol) in enumerate(((ya, ZA), (yb, ZB), (yc, ZC)))]
    pbr = [_mm(us[n], w["w_branch"][n], name=f"{tag}_mm_br{n}") for n in range(N_BRANCH)]
    merged = _merge(proj, pbr, name=f"{tag}_merge")
    out = _mm(merged, w["w_out"], add=h, name=f"{tag}_mm_out")
    saved = dict(h=h, hn=hn, proj=proj, ops_a=ops_a, ya=ya, lsea=lsea, cqn=cqn, ckvn=ckvn,
                 ops_b=ops_b, yb=yb, lseb=lseb, ops_c=ops_c, yc=yc, lsec=lsec, us=us, pbr=pbr, merged=merged)
    return out, saved, side_out, w


def _w_in_chunks(g, tag):
    return _w_in_to_shards(g["w_in"], name=f"{tag}_w_in_chunks")


def _grad_chunks(g, tag, with_w_in=True):
    br = _cut_shards(g["w_branch"].reshape(N_BRANCH * BRANCH_W, D_MODEL), name=f"{tag}_w_branch_chunks")
    rest = [_cut_shards(g["w_uq"], name=f"{tag}_w_uq_chunks"), _cut_shards(g["w_ukv"], name=f"{tag}_w_ukv_chunks"),
            br.reshape(N_DEV, N_BRANCH, BRANCH_W, D_MODEL // N_DEV),
            g["w_out"].reshape(N_DEV, D_MODEL // N_DEV, D_MODEL)]
    return ([_w_in_chunks(g, tag)] if with_w_in else []) + rest


def _layer_bwd(dout, s, w, tabs, l, side=None, own_exchange=False):
    tag = f"l{l}"
    L = dout.shape[0]
    proj = s["proj"]
    g = {}
    dproj = jnp.zeros((L, NP_IN), BF16)
    dmerged = _mm(dout, w["w_out"], tb=True, name=f"{tag}_mm_out_dx")
    g["w_out"] = _mm(s["merged"], dout, ta=True, out_dtype=BF16, name=f"{tag}_mm_out_dw")
    dproj, dpbr = _merge_bwd(dmerged, proj, s["pbr"], dproj, name=f"{tag}_merge_bwd")
    dus = [_mm(dpbr[n], w["w_branch"][n], tb=True, name=f"{tag}_mm_br{n}_dx") for n in range(N_BRANCH)]
    g["w_branch"] = jnp.stack([_mm(s["us"][n], dpbr[n], ta=True, out_dtype=BF16, name=f"{tag}_mm_br{n}_dw")
                               for n in range(N_BRANCH)])
    dproj, dya, dla = _gate_bwd(dus[0], s["ya"], proj, ZA, dproj, delta_rows=True, name=f"{tag}_gate0_bwd")
    dproj, dyb, dlb = _gate_bwd(dus[1], s["yb"], proj, ZB, dproj, delta_rows=True, name=f"{tag}_gate1_bwd")
    dproj, dyc = _gate_bwd(dus[2], s["yc"], proj, ZC, dproj, name=f"{tag}_gate2_bwd")

    def bwd_operands(ops, lse, delta, dy16):
        q16, k16, v16, kbias = ops
        return (q16, k16, v16, kbias.reshape(kbias.shape[0], L, 1), lse.reshape(HEADS, L // ATT_T, 1, ATT_T), delta,
                dy16)

    dqa, dka, dva, dcq, dck, *side_out = _attn_bwd_t(*bwd_operands(s["ops_a"], s["lsea"], dla, dya), T=ATT_T,
                                                     fox=True, side=side, name=f"{tag}_attn_a_bwd")
    dproj = _unprep_a(dqa, dka, dva, dproj, name=f"{tag}_unprep_a")
    dproj, g["b_f"] = _forget_bwd(dcq.reshape(HEADS, L) + dck[:, :, 0], proj, w["b_f"], dproj,
                                  name=f"{tag}_forget_bwd")
    dqb, dkb, dvb, _, _ = _attn_bwd_t(*bwd_operands(s["ops_b"], s["lseb"], dlb, dyb), T=ATT_T,
                                      name=f"{tag}_attn_b_bwd")
    dproj, dqbm, dkvbm = _unprep_b(dqb, dkb, dvb, tabs["bk"], tabs["bq"], dproj, name=f"{tag}_unprep_b")
    dcqn = _mm(dqbm, w["w_uq"], tb=True, name=f"{tag}_mm_uq_dx")
    g["w_uq"] = _mm(s["cqn"], dqbm, ta=True, out_dtype=BF16, name=f"{tag}_mm_uq_dw")
    dckvn = _mm(dkvbm, w["w_ukv"], tb=True, name=f"{tag}_mm_ukv_dx")
    g["w_ukv"] = _mm(s["ckvn"], dkvbm, ta=True, out_dtype=BF16, name=f"{tag}_mm_ukv_dw")
    dproj, g["g_cq"] = _rms_bwd(dcqn, proj, w["g_cq"], col=CQ, into=dproj, name=f"{tag}_rms_cq_bwd")
    dproj, g["g_ckv"] = _rms_bwd(dckvn, proj, w["g_ckv"], col=CKV, into=dproj, name=f"{tag}_rms_ckv_bwd")
    dqc, dkc, dvc = _attn_bwd_window(*s["ops_c"], s["yc"], s["lsec"], dyc, name=f"{tag}_attn_c_bwd")
    dproj = _unprep_c(dqc, dkc, dvc, tabs["c"], dproj, name=f"{tag}_unprep_c")
    delta_c = jnp.sum(dyc * s["yc"], axis=-1)
    g["sinks"] = -jnp.sum(jnp.exp(w["sinks"][:, None] - s["lsec"][:, :, 0]) * delta_c, axis=1)
    own_out = []
    if own_exchange:
        g["w_in"], *r_rest = _mm(s["hn"].T, dproj, out_dtype=BF16, side=_Exchange(_grad_chunks(g, tag, False)),
                                 name=f"{tag}_mm_in_dw")
        dhn, r_in = _mm(dproj, w["w_in"], tb=True, side=_Exchange([_w_in_chunks(g, tag)]), name=f"{tag}_mm_in_dx")
        own_out = [r_in, *r_rest]
    else:
        g["w_in"] = _mm(s["hn"].T, dproj, out_dtype=BF16, name=f"{tag}_mm_in_dw")
        dhn = _mm(dproj, w["w_in"], tb=True, name=f"{tag}_mm_in_dx")
    dh, g["norm_g"] = _rms_bwd(dhn, s["h"], w["norm_g"], add=dout, name=f"{tag}_rms_in_bwd")
    return dh, g, side_out, own_out


def _cols_from_shards(g):
    return jnp.moveaxis(g, 0, 1).reshape(g.shape[1], N_DEV * g.shape[2])


def _cols_to_shards(w):
    R = w.shape[0]
    return jnp.moveaxis(w.reshape(R, N_DEV, w.shape[1] // N_DEV), 1, 0)


def _join_shards(g, *, name):
    _, R, C = g.shape
    tr = _pick(R, (512, 384, 256))

    def body(g_ref, o_ref):
        for d in range(N_DEV):
            o_ref[:, C * d:C * (d + 1)] = g_ref[d]

    return pl.pallas_call(
        body, name=name, grid=(R // tr,),
        in_specs=[pl.BlockSpec((N_DEV, tr, C), lambda i: (0, i, 0))],
        out_specs=pl.BlockSpec((tr, N_DEV * C), lambda i: (i, 0)),
        out_shape=jax.ShapeDtypeStruct((R, N_DEV * C), g.dtype),
        compiler_params=_cparams(("parallel",)),
    )(g)


def _cut_shards(w, *, name):
    R = w.shape[0]
    C = w.shape[1] // N_DEV
    tr = _pick(R, (512, 384, 256))

    def body(w_ref, o_ref):
        for d in range(N_DEV):
            o_ref[d] = w_ref[:, C * d:C * (d + 1)]

    return pl.pallas_call(
        body, name=name, grid=(R // tr,),
        in_specs=[pl.BlockSpec((tr, N_DEV * C), lambda i: (i, 0))],
        out_specs=pl.BlockSpec((N_DEV, tr, C), lambda i: (0, i, 0)),
        out_shape=jax.ShapeDtypeStruct((N_DEV, R, C), w.dtype),
        compiler_params=_cparams(("parallel",)),
    )(w)


def _pad_in(w):
    parts, at = [], 0
    for lo, hi, dst in sorted(_RUNS, key=lambda r: r[2]):
        if dst > at:
            parts.append(jnp.zeros((w.shape[0], dst - at), w.dtype))
        parts.append(w[:, lo:hi])
        at = dst + hi - lo
    parts.append(jnp.zeros((w.shape[0], NP_IN - at), w.dtype))
    return jnp.concatenate(parts, axis=1)


def _unpad_in(w):
    return jnp.concatenate([w[:, dst:dst + hi - lo] for lo, hi, dst in _RUNS], axis=1)


W_IN_SHARD = N_IN // N_DEV


def _w_in_pieces():
    pieces = []
    for lo, hi, dst in _RUNS:
        for d in range(N_DEV):
            a, b = max(lo, d * W_IN_SHARD), min(hi, (d + 1) * W_IN_SHARD)
            if a < b:
                pieces.append((d, a - d * W_IN_SHARD, b - d * W_IN_SHARD, dst + a - lo))
    return pieces


def _w_in_from_shards(g, *, name):
    R = g.shape[1]
    tr = 256
    covered = sorted((dst, dst + b - a) for _, a, b, dst in _w_in_pieces())

    def body(g_ref, o_ref):
        at = 0
        for lo, hi in covered + [(NP_IN, NP_IN)]:
            if lo > at:
                o_ref[:, at:lo] = jnp.zeros((tr, lo - at), BF16)
            at = max(at, hi)
        for d, a, b, dst in _w_in_pieces():
            o_ref[:, dst:dst + b - a] = g_ref[d, :, a:b]

    return pl.pallas_call(
        body, name=name, grid=(R // tr,),
        in_specs=[pl.BlockSpec((N_DEV, tr, W_IN_SHARD), lambda i: (0, i, 0))],
        out_specs=pl.BlockSpec((tr, NP_IN), lambda i: (i, 0)),
        out_shape=jax.ShapeDtypeStruct((R, NP_IN), BF16),
        compiler_params=_cparams(("parallel",)),
    )(g)


def _w_in_to_shards(dw, *, name):
    R = dw.shape[0]
    tr = 256

    def body(w_ref, o_ref):
        for d, a, b, dst in _w_in_pieces():
            o_ref[d, :, a:b] = w_ref[:, dst:dst + b - a]

    return pl.pallas_call(
        body, name=name, grid=(R // tr,),
        in_specs=[pl.BlockSpec((tr, NP_IN), lambda i: (i, 0))],
        out_specs=pl.BlockSpec((N_DEV, tr, W_IN_SHARD), lambda i: (0, i, 0)),
        out_shape=jax.ShapeDtypeStruct((N_DEV, R, W_IN_SHARD), BF16),
        compiler_params=_cparams(("parallel",)),
    )(dw)


_SMALL = (("norm_g", DEPTH * D_MODEL), ("b_f", DEPTH * HEADS), ("g_cq", DEPTH * MLA_QLORA),
          ("g_ckv", DEPTH * MLA_KVLORA), ("sinks", DEPTH * HEADS), ("final_g", D_MODEL), ("loss", 1),
          ("meta", N_META * D_MODEL))
SMALL_ROWS = 168


def _pack_small(d):
    parts = []
    for name, size in _SMALL:
        padded = -(-size // 128) * 128
        v = d[name].reshape(-1).astype(F32) if name in d else jnp.zeros((size,), F32)
        parts.append(jnp.pad(v, (0, padded - size)))
    flat = jnp.concatenate(parts)
    return jnp.pad(flat, (0, SMALL_ROWS * 128 - flat.shape[0])).reshape(SMALL_ROWS, 128)


def _unpack_small(p, shapes):
    flat = p.reshape(-1)
    out, at = {}, 0
    for name, size in _SMALL:
        if name in shapes:
            out[name] = flat[at:at + size].reshape(shapes[name])
        at += -(-size // 128) * 128
    return out


def kernel(x, meta_tokens, norm_g, w_in, b_f, g_cq, g_ckv, w_uq, w_ukv, sinks, w_branch, w_out, final_g, loss_target, m_meta_tokens, m_norm_g, m_w_in, m_b_f, m_g_cq, m_g_ckv, m_w_uq, m_w_ukv, m_sinks, m_w_branch, m_w_out, m_final_g, v_meta_tokens, v_norm_g, v_w_in, v_b_f, v_g_cq, v_g_ckv, v_w_uq, v_w_ukv, v_sinks, v_w_branch, v_w_out, v_final_g):
    S = x.shape[1]
    L = BLK + S
    cx, cy, cc = _me()
    my_idx = 4 * cx + 2 * cy + cc

    def shards(l):
        return [t[l].astype(BF16) for t in (w_in, w_uq, w_ukv, w_branch, w_out)]

    def small_weights(l):
        return dict(norm_g=norm_g[l], b_f=b_f[l], g_cq=g_cq[l], g_ckv=g_ckv[l], sinks=sinks[l])

    def rest_weights(l):
        def assemble(gw_uq, gw_ukv, gw_br, gw_out):
            br = _join_shards(gw_br.reshape(N_DEV, N_BRANCH * BRANCH_W, D_MODEL // N_DEV), name=f"l{l}_w_branch_full")
            return dict(w_uq=_join_shards(gw_uq, name=f"l{l}_w_uq_full"),
                        w_ukv=_join_shards(gw_ukv, name=f"l{l}_w_ukv_full"),
                        w_branch=br.reshape(N_BRANCH, BRANCH_W, D_MODEL), w_out=gw_out.reshape(D_MODEL, D_MODEL))
        return assemble

    def layer_weights(l, gw_in, *gw_rest):
        return dict(small_weights(l), w_in=_w_in_from_shards(gw_in, name=f"l{l}_w_in_full"),
                    **rest_weights(l)(*gw_rest))

    gw_in0, g_meta = _comm(_Gather([shards(0)[0], meta_tokens]), name="gather_l0")
    layers = [dict(small_weights(0), w_in=_w_in_from_shards(gw_in0, name="l0_w_in_full"),
                   late=(_Gather(shards(0)[1:]), rest_weights(0))), None]
    meta_full = _cols_from_shards(g_meta)

    h = jnp.concatenate([jnp.zeros((PAD, D_MODEL), F32), meta_full, x[0]], axis=0)
    q_rotary = ((jnp.arange(HEADS * DQK_B) % DQK_B) >= MLA_NOPE).astype(F32)[None, :]
    tabs = dict(c=_rope_tables(L, DH // 2, LANES), bk=_rope_tables(L, MLA_ROPE // 2, LANES), bq=q_rotary)
    saved = [None] * DEPTH
    h, saved[0], gw1, layers[0] = _layer_fwd(h, layers[0], tabs, 0, side=_Gather(shards(1)))
    layers[1] = layer_weights(1, *gw1)
    h, saved[1], _, _ = _layer_fwd(h, layers[1], tabs, 1)
    loss_vec, dh, g_final = _loss_head(h, final_g, loss_target[0], name="loss_head")

    grads = [None] * DEPTH
    dh, grads[1], _, _ = _layer_bwd(dh, saved[1], layers[1], tabs, 1)
    dh, grads[0], recv1, recv0 = _layer_bwd(dh, saved[0], layers[0], tabs, 0, side=_Exchange(_grad_chunks(grads[1], "l1")),
                                            own_exchange=True)
    r_in, r_uq, r_ukv, r_br, r_out = (jnp.stack([a, b], axis=1) for a, b in zip(recv0, recv1))

    def stack(name):
        return jnp.stack([grads[l][name] for l in range(DEPTH)])

    small = _pack_small(dict(norm_g=stack("norm_g"), b_f=stack("b_f"), g_cq=stack("g_cq"), g_ckv=stack("g_ckv"),
                             sinks=stack("sinks"), final_g=g_final, loss=loss_vec[0, 0:1],
                             meta=dh[PAD:BLK]))
    (g_small,) = _comm(_Gather([small]), name="gather_small")

    def adam_big(w_, parts, m_, v_, name):
        shape = w_.shape
        C = shape[-1]
        R = math.prod(shape[:-1])
        outs = _adamw(w_.reshape(R, C), parts.reshape(parts.shape[0], R, C), m_.reshape(R, C), v_.reshape(R, C),
                      name=name)
        return [o.reshape(shape) for o in outs]

    res = {}
    res["w_in"] = adam_big(w_in, r_in, m_w_in, v_w_in, "adam_w_in")
    res["w_uq"] = adam_big(w_uq, r_uq, m_w_uq, v_w_uq, "adam_w_uq")
    res["w_ukv"] = adam_big(w_ukv, r_ukv, m_w_ukv, v_w_ukv, "adam_w_ukv")
    res["w_branch"] = adam_big(w_branch, r_br, m_w_branch, v_w_branch, "adam_w_branch")
    res["w_out"] = adam_big(w_out, r_out, m_w_out, v_w_out, "adam_w_out")

    small_w = dict(norm_g=norm_g, b_f=b_f, g_cq=g_cq, g_ckv=g_ckv, sinks=sinks, final_g=final_g)
    small_m = dict(norm_g=m_norm_g, b_f=m_b_f, g_cq=m_g_cq, g_ckv=m_g_ckv, sinks=m_sinks, final_g=m_final_g)
    small_v = dict(norm_g=v_norm_g, b_f=v_b_f, g_cq=v_g_cq, g_ckv=v_g_ckv, sinks=v_sinks, final_g=v_final_g)
    sm = _adamw(_pack_small(small_w), g_small, _pack_small(small_m), _pack_small(small_v), name="adam_small")
    shapes = {k: a.shape for k, a in small_w.items()}
    shapes_all = dict(shapes, loss=(), meta=(N_META, D_MODEL))
    sm_g = _unpack_small(sm[0], shapes_all)
    sm_d, sm_m, sm_v = (_unpack_small(t, shapes) for t in sm[1:])
    for k in shapes:
        res[k] = [sm_g[k], sm_d[k], sm_m[k], sm_v[k]]
    g_meta_mine = lax.dynamic_slice(sm_g["meta"], (0, my_idx * 128), (N_META, 128))
    res["meta_tokens"] = _adamw(meta_tokens, g_meta_mine[None], m_meta_tokens, v_meta_tokens, name="adam_meta")

    order = ["meta_tokens", "norm_g", "w_in", "b_f", "g_cq", "g_ckv", "w_uq", "w_ukv", "sinks", "w_branch", "w_out",
             "final_g"]
    grad_x = dh[BLK:][None]
    return (sm_g["loss"], grad_x, *[res[k][0] for k in order], *[res[k][1] for k in order],
            *[res[k][2] for k in order], *[res[k][3] for k in order])
```

```python
import functools
import math

import jax
import jax.numpy as jnp
from jax import lax
from jax.experimental import pallas as pl
from jax.experimental.pallas import tpu as pltpu

F32 = jnp.float32
BF16 = jnp.bfloat16

D_MODEL = 1024
DEPTH = 2
N_META = 16
BLK = 128
PAD = BLK - N_META
ROPE_THETA = 10000.0
EPS = 1e-6
NEG = -1e30
BIG = 1e30
HEADS = 8
DH = 64
MLA_NOPE = 64
MLA_ROPE = 32
MLA_QLORA = 384
MLA_KVLORA = 256
SWA_KV_HEADS = 2
WINDOW = 128
BRANCH_W = 512
N_BRANCH = 3
N_IN = 7592

ADAM_LR = 0.001
ADAM_B1 = 0.9
ADAM_B2 = 0.999
ADAM_EPS = 1e-08
ADAM_WD = 0.01
ADAM_STEP = 10

N_DEV = 8
MESH = pl.DeviceIdType.MESH

NP_IN = 8192
QKVA, ZA, FA = 0, 1536, 2048
CKV, KR, CQ = 2304, 2560, 2688
GATES = 3072
ZB = 6144
QKVC, ZC = 6912, 7680
_RUNS = ((0, 1536, QKVA), (1536, 1544, FA), (1544, 2056, ZA), (2056, 2440, CQ), (2440, 2696, CKV), (2696, 2728, KR),
         (2728, 3240, ZB), (3240, 4008, QKVC), (4008, 4520, ZC), (4520, 7592, GATES))

VMEM_LIMIT = 48 * 1024 * 1024
ATT_T = 384
ATT_HP = 1
ROW_T = 384
LANES = 128


def _pick(dim, prefs):
    for p in prefs:
        if dim % p == 0:
            return p
    return dim


def _cparams(sem):
    return pltpu.CompilerParams(dimension_semantics=sem, vmem_limit_bytes=VMEM_LIMIT)


def _side_cparams(side, sem):
    if side is None:
        return _cparams(sem)
    return pltpu.CompilerParams(dimension_semantics=("arbitrary",) * len(sem), vmem_limit_bytes=VMEM_LIMIT,
                                has_side_effects=True)


def _mm(a, b, *, ta=False, tb=False, add=None, out_dtype=F32, side=None, name):
    M = a.shape[1] if ta else a.shape[0]
    K = a.shape[0] if ta else a.shape[1]
    N = b.shape[0] if tb else b.shape[1]
    assert K == (b.shape[1] if tb else b.shape[0])
    tm = _pick(M, (704, 1024, 512, 384, 256, 128))
    tn = _pick(N, (1024, 768, 512, 384, 256, 128))
    tk = _pick(K, (4096, 1408, 1024, 768, 512, 384, 256, 128))
    nk = K // tk
    dims = (((0 if ta else 1,), (1 if tb else 0,)), ((), ()))

    sn = 0 if side is None else side.n
    n_in = 2 + (add is not None)
    grid = (M // tm, N // tn, nk)

    def body(*refs):
        a_ref, b_ref = refs[:2]
        c_ref = refs[2] if add is not None else None
        o_ref = refs[n_in + sn]
        scratch = refs[n_in + 2 * sn + 1:]
        if side is not None:
            step = (pl.program_id(0) * grid[1] + pl.program_id(1)) * nk + pl.program_id(2)
            _ride(side, [*refs[n_in:n_in + sn], *refs[n_in + sn + 1:n_in + 2 * sn + 1], *scratch[nk > 1:]],
                  step, grid[0] * grid[1] * nk)
        r = lax.dot_general(a_ref[...].astype(BF16), b_ref[...].astype(BF16), dims, preferred_element_type=F32)

        def finish(total):
            if c_ref is not None:
                total = total + c_ref[...]
            o_ref[...] = total.astype(out_dtype)

        if nk == 1:
            finish(r)
        else:
            acc = scratch[0]
            k = pl.program_id(2)

            @pl.when(k == 0)
            def _():
                acc[...] = r

            @pl.when(k > 0)
            def _():
                acc[...] += r

            @pl.when(k == nk - 1)
            def _():
                finish(acc[...])

    a_spec = pl.BlockSpec((tk, tm), lambda i, j, k: (k, i)) if ta else pl.BlockSpec((tm, tk), lambda i, j, k: (i, k))
    b_spec = pl.BlockSpec((tn, tk), lambda i, j, k: (j, k)) if tb else pl.BlockSpec((tk, tn), lambda i, j, k: (k, j))
    o_spec = pl.BlockSpec((tm, tn), lambda i, j, k: (i, j))
    any_spec = pl.BlockSpec(memory_space=pl.ANY)
    outs = pl.pallas_call(
        body, name=name,
        grid=grid,
        in_specs=[a_spec, b_spec] + ([o_spec] if add is not None else []) + [any_spec] * sn,
        out_specs=[o_spec] + [any_spec] * sn,
        out_shape=[jax.ShapeDtypeStruct((M, N), out_dtype)] + ([] if side is None else side.out_shape),
        scratch_shapes=([pltpu.VMEM((tm, tn), F32)] if nk > 1 else []) + ([] if side is None else side.scratch),
        compiler_params=_side_cparams(side, ("parallel", "parallel", "arbitrary")),
    )(*((a, b) if add is None else (a, b, add)), *([] if side is None else side.arrs))
    return outs[0] if side is None else outs


def _col_spec(tm, width, col):
    assert col % width == 0
    return pl.BlockSpec((tm, width), lambda i, _c=col // width: (i, _c))


def _rms_fwd(x, g, *, col=0, name):
    L = x.shape[0]
    D = g.shape[0]
    tm = ROW_T

    def body(x_ref, g_ref, y_ref):
        xv = x_ref[...]
        rstd = lax.rsqrt(jnp.mean(xv * xv, axis=-1, keepdims=True) + EPS)
        y_ref[...] = (xv * rstd * g_ref[...]).astype(BF16)

    return pl.pallas_call(
        body, name=name, grid=(L // tm,),
        in_specs=[_col_spec(tm, D, col), pl.BlockSpec((1, D), lambda i: (0, 0))],
        out_specs=pl.BlockSpec((tm, D), lambda i: (i, 0)),
        out_shape=jax.ShapeDtypeStruct((L, D), BF16),
        compiler_params=_cparams(("parallel",)),
    )(x, g.reshape(1, D))


def _rms_bwd(dy, x, g, *, col=0, add=None, into=None, name):
    L = x.shape[0]
    D = g.shape[0]
    tm = ROW_T

    def body(*refs):
        dy_ref, x_ref, g_ref = refs[:3]
        add_ref = refs[3] if add is not None else None
        dx_ref, dg_ref = refs[-2:]
        i = pl.program_id(0)
        xv = x_ref[...]
        dyv = dy_ref[...]
        rstd = lax.rsqrt(jnp.mean(xv * xv, axis=-1, keepdims=True) + EPS)
        xhat = xv * rstd
        part = jnp.sum(dyv * xhat, axis=0, keepdims=True)

        @pl.when(i == 0)
        def _():
            dg_ref[...] = part

        @pl.when(i > 0)
        def _():
            dg_ref[...] += part

        dxh = dyv * g_ref[...]
        dx = rstd * (dxh - xhat * jnp.mean(dxh * xhat, axis=-1, keepdims=True))
        if add_ref is not None:
            dx = dx + add_ref[...]
        dx_ref[...] = dx.astype(dx_ref.dtype)

    row = pl.BlockSpec((tm, D), lambda i: (i, 0))
    in_specs = [row, _col_spec(tm, D, col), pl.BlockSpec((1, D), lambda i: (0, 0))]
    args = [dy, x, g.reshape(1, D)]
    aliases = {}
    if add is not None:
        in_specs.append(row)
        args.append(add)
    if into is not None:
        in_specs.append(pl.BlockSpec(memory_space=pl.ANY))
        args.append(into)
        aliases = {len(args) - 1: 0}
        dx_spec, dx_shape = _col_spec(tm, D, col), jax.ShapeDtypeStruct(into.shape, into.dtype)
    else:
        dx_spec, dx_shape = row, jax.ShapeDtypeStruct((L, D), F32)
    dx, dg = pl.pallas_call(
        body, name=name, grid=(L // tm,),
        in_specs=in_specs,
        out_specs=[dx_spec, pl.BlockSpec((1, D), lambda i: (0, 0))],
        out_shape=[dx_shape, jax.ShapeDtypeStruct((1, D), F32)],
        input_output_aliases=aliases,
        compiler_params=_cparams(("arbitrary",)),
    )(*args)
    return dx, dg.reshape(D)


def _loss_head(h, g, target, *, name):
    L, D = h.shape
    nb = L // BLK

    def body(h_ref, g_ref, t_ref, loss_ref, dh_ref, dg_ref):
        i = pl.program_id(0)

        @pl.when(i == 0)
        def _():
            loss_ref[...] = jnp.zeros_like(loss_ref)
            dg_ref[...] = jnp.zeros_like(dg_ref)
            dh_ref[...] = jnp.zeros_like(dh_ref)

        @pl.when(i > 0)
        def _():
            xv = h_ref[...]
            gv = g_ref[...]
            rstd = lax.rsqrt(jnp.mean(xv * xv, axis=-1, keepdims=True) + EPS)
            xhat = xv * rstd
            err = xhat * gv - t_ref[...]
            row = jnp.mean(err * err, axis=-1, keepdims=True)
            loss_ref[...] += 0.5 * jnp.sum(row, axis=0, keepdims=True)
            dy = err * (1.0 / D)
            dg_ref[...] += jnp.sum(dy * xhat, axis=0, keepdims=True)
            dxh = dy * gv
            dh_ref[...] = rstd * (dxh - xhat * jnp.mean(dxh * xhat, axis=-1, keepdims=True))

    loss, dh, dg = pl.pallas_call(
        body, name=name, grid=(nb,),
        in_specs=[pl.BlockSpec((BLK, D), lambda i: (i, 0)), pl.BlockSpec((1, D), lambda i: (0, 0)),
                  pl.BlockSpec((BLK, D), lambda i: (jnp.maximum(i - 1, 0), 0))],
        out_specs=[pl.BlockSpec((1, 128), lambda i: (0, 0)), pl.BlockSpec((BLK, D), lambda i: (i, 0)),
                   pl.BlockSpec((1, D), lambda i: (0, 0))],
        out_shape=[jax.ShapeDtypeStruct((1, 128), F32), jax.ShapeDtypeStruct((L, D), F32),
                   jax.ShapeDtypeStruct((1, D), F32)],
        compiler_params=_cparams(("arbitrary",)),
    )(h, g.reshape(1, D), target)
    return loss, dh, dg.reshape(D)


_NT = (((1,), (1,)), ((), ()))
_TN = (((0,), (0,)), ((), ()))


def _attn_fwd(q, k, v, kbias, *, T, window=False, sink=None, side=None, name):
    sn = 0 if side is None else side.n
    H, L, dk = q.shape
    Hkv = k.shape[0]
    dv = v.shape[2]
    G = H // Hkv
    nt = L // T
    Hb = kbias.shape[0]
    reps = T // LANES
    assert not window or T == WINDOW
    HP = G if G > 1 else ATT_HP
    NS = 1 if G > 1 else HP
    R = HP * T // NS
    HKV = HP // G
    HB = HP if Hb > 1 else 1
    assert G == 1 or Hb == 1

    def body(*refs):
        q_ref, k_ref, v_ref, kb_ref = refs[:4]
        n = 4
        if sink is not None:
            sk_ref = refs[n]
            n += 1
        side_in = refs[n:n + sn]
        n += sn
        o_ref, lse_ref = refs[n:n + 2]
        side_out = refs[n + 2:n + 2 + sn]
        n += 2 + sn
        m_scs, l_scs, acc_scs, buf_a, buf_b = (refs[n + t * NS:n + (t + 1) * NS] for t in range(5))
        i = pl.program_id(1)
        if side is not None:
            _ride(side, [*side_in, *side_out, *refs[n + 5 * NS:]], pl.program_id(0) * nt + i, (H // HP) * nt)
        for a in range(NS):
            if sink is not None:
                sk = [jnp.broadcast_to(sk_ref[b, :, 0:1], (T, LANES)) for b in range(HP)]
                m_scs[a][...] = jnp.concatenate(sk, axis=0) if G > 1 else sk[a]
                l_scs[a][...] = jnp.ones((R, LANES), F32)
            else:
                m_scs[a][...] = jnp.full((R, LANES), NEG, F32)
                l_scs[a][...] = jnp.zeros((R, LANES), F32)
            acc_scs[a][...] = jnp.zeros((R, dv), F32)
        row = lax.broadcasted_iota(jnp.int32, (R, T), 0) & (T - 1) if G > 1 else \
            lax.broadcasted_iota(jnp.int32, (R, T), 0)
        col = lax.broadcasted_iota(jnp.int32, (R, T), 1)

        def logits(a, j):
            rows = pl.ds(pl.multiple_of(j * T, T), T)
            qv = q_ref[...].reshape(R, dk) if G > 1 else q_ref[a]
            s = lax.dot_general(qv, k_ref[a, rows, :], _NT, preferred_element_type=F32)
            return s - kb_ref[a if HB > 1 else 0, j]

        def update(a, s, j, kind):
            rows = pl.ds(pl.multiple_of(j * T, T), T)
            m_sc, l_sc, acc_sc = m_scs[a], l_scs[a], acc_scs[a]
            if kind == "diag":
                s = jnp.where(row >= col, s, NEG)
            elif kind == "prev":
                s = jnp.where((col > row) & (i > 0), s, NEG)
            m_prev = m_sc[...]
            m_new = jnp.maximum(m_prev, jnp.max(s, axis=1, keepdims=True))
            alpha = jnp.exp(m_prev - m_new)
            p = jnp.exp(s - jnp.tile(m_new, (1, reps)))
            l_sc[...] = alpha * l_sc[...] + jnp.sum(p, axis=1, keepdims=True)
            acc_sc[...] = alpha[:, :dv] * acc_sc[...] + jnp.dot(p.astype(BF16), v_ref[a, rows, :],
                                                                preferred_element_type=F32)
            m_sc[...] = m_new

        if window:
            before = jnp.maximum(i - 1, 0)
            s_prev = [logits(a, before) for a in range(NS)]
            s_diag = [logits(a, i) for a in range(NS)]
            for a in range(NS):
                update(a, s_prev[a], before, "prev")
                update(a, s_diag[a], i, "diag")
        else:
            def fill(buf, j):
                for a in range(NS):
                    buf[a][...] = logits(a, j)

            def drain(buf, j, kind):
                for a in range(NS):
                    update(a, buf[a][...], j, kind)

            fill(buf_a, 0)

            def pair(t, c):
                fill(buf_b, 2 * t + 1)
                drain(buf_a, 2 * t, "full")
                fill(buf_a, 2 * t + 2)
                drain(buf_b, 2 * t + 1, "full")
                return c

            lax.fori_loop(0, i // 2, pair, 0)

            @pl.when(i % 2 == 1)
            def _():
                fill(buf_b, i)
                drain(buf_a, i - 1, "full")
                drain(buf_b, i, "diag")

            @pl.when(i % 2 == 0)
            def _():
                drain(buf_a, i, "diag")

        for a in range(NS):
            lv = l_scs[a][...]
            ov = acc_scs[a][...] / lv[:, :dv]
            lsev = (m_scs[a][...] + jnp.log(lv))[:, 0:1]
            if G > 1:
                o_ref[...] = ov.reshape(HP, T, dv)
                lse_ref[...] = lsev.reshape(HP, T, 1)
            else:
                o_ref[a] = ov
                lse_ref[a] = lsev

    in_specs = [pl.BlockSpec((HP, T, dk), lambda h, i: (h, i, 0)),
                pl.BlockSpec((HKV, L, dk), lambda h, i: (h, 0, 0)),
                pl.BlockSpec((HKV, L, dv), lambda h, i: (h, 0, 0)),
                pl.BlockSpec((HB, nt, 1, T), lambda h, i: (h if Hb > 1 else 0, 0, 0, 0))]
    args = [q, k, v, kbias]
    if sink is not None:
        in_specs += [pl.BlockSpec((HP, 1, LANES), lambda h, i: (h, 0, 0))]
        args += [sink]
    any_spec = pl.BlockSpec(memory_space=pl.ANY)
    return pl.pallas_call(
        body, name=name, grid=(H // HP, nt),
        in_specs=in_specs + [any_spec] * sn,
        out_specs=[pl.BlockSpec((HP, T, dv), lambda h, i: (h, i, 0)),
                   pl.BlockSpec((HP, T, 1), lambda h, i: (h, i, 0))] + [any_spec] * sn,
        out_shape=[jax.ShapeDtypeStruct((H, L, dv), F32), jax.ShapeDtypeStruct((H, L, 1), F32)]
        + ([] if side is None else side.out_shape),
        scratch_shapes=[pltpu.VMEM((R, LANES), F32)] * (2 * NS) + [pltpu.VMEM((R, dv), F32)] * NS
        + [pltpu.VMEM((R, T), F32)] * (2 * NS) + ([] if side is None else side.scratch),
        compiler_params=_side_cparams(side, ("parallel", "arbitrary")),
    )(*args, *([] if side is None else side.arrs))


def _attn_bwd_t(q, k, v, kbias_col, lse_row, delta_row, do, *, T, fox=False, side=None, name):
    sn = 0 if side is None else side.n
    H, L, dk = q.shape
    dv = v.shape[2]
    nt = L // T
    Hb = kbias_col.shape[0]

    def body(*refs):
        q_ref, k_ref, v_ref, kb_ref, lse_ref, dl_ref, do_ref = refs[:7]
        side_in = refs[7:7 + sn]
        n = 7 + sn
        dq_ref, dk_ref, dv_ref = refs[n:n + 3]
        n += 3
        if fox:
            dcq_ref, dck_ref = refs[n:n + 2]
            n += 2
        side_out = refs[n:n + sn]
        n += sn
        buf_a, buf_b = refs[n:n + 2], refs[n + 2:n + 4]
        j = pl.program_id(1)
        if side is not None:
            _ride(side, [*side_in, *side_out, *refs[n + 4:]], pl.program_id(0) * nt + j, H * nt)

        @pl.when(j == 0)
        def _():
            dq_ref[...] = jnp.zeros_like(dq_ref)
            if fox:
                dcq_ref[...] = jnp.zeros_like(dcq_ref)

        dk_ref[...] = jnp.zeros_like(dk_ref)
        dv_ref[...] = jnp.zeros_like(dv_ref)
        if fox:
            dck_ref[...] = jnp.zeros_like(dck_ref)
        kb = k_ref[0]
        vb = v_ref[0]
        kbias_j = kb_ref[0]
        key = lax.broadcasted_iota(jnp.int32, (T, T), 0)
        qry = lax.broadcasted_iota(jnp.int32, (T, T), 1)

        def fill(buf, i):
            rows = pl.ds(pl.multiple_of(i * T, T), T)
            buf[0][...] = lax.dot_general(kb, q_ref[0, rows, :], _NT, preferred_element_type=F32) - kbias_j
            buf[1][...] = lax.dot_general(vb, do_ref[0, rows, :], _NT, preferred_element_type=F32)

        def drain(buf, i, kind):
            rows = pl.ds(pl.multiple_of(i * T, T), T)
            st = buf[0][...]
            if kind == "diag":
                st = jnp.where(key <= qry, st, NEG)
            pt = jnp.exp(st - lse_ref[0, i])
            dst = pt * (buf[1][...] - dl_ref[0, i])
            dsb = dst.astype(BF16)
            dv_ref[0] += jnp.dot(pt.astype(BF16), do_ref[0, rows, :], preferred_element_type=F32)
            dk_ref[0] += jnp.dot(dsb, q_ref[0, rows, :], preferred_element_type=F32)
            dq_ref[0, rows, :] += lax.dot_general(dsb, kb, _TN, preferred_element_type=F32)
            if fox:
                dcq_ref[0, i] += jnp.sum(dst, axis=0, keepdims=True)
                dck_ref[0] += -jnp.sum(dst, axis=1, keepdims=True)

        first = j + 1
        rest = nt - first
        fill(buf_a, j)
        fill(buf_b, jnp.minimum(first, nt - 1))
        drain(buf_a, j, "diag")

        def pair(t, c):
            i0 = first + 2 * t
            fill(buf_a, i0 + 1)
            drain(buf_b, i0, "full")
            fill(buf_b, jnp.minimum(i0 + 2, nt - 1))
            drain(buf_a, i0 + 1, "full")
            return c

        lax.fori_loop(0, rest // 2, pair, 0)

        @pl.when(rest % 2 == 1)
        def _():
            drain(buf_b, nt - 1, "full")

    rows_spec = pl.BlockSpec((1, nt, 1, T), lambda h, j: (h, 0, 0, 0))
    in_specs = [pl.BlockSpec((1, L, dk), lambda h, j: (h, 0, 0)),
                pl.BlockSpec((1, T, dk), lambda h, j: (h, j, 0)),
                pl.BlockSpec((1, T, dv), lambda h, j: (h, j, 0)),
                pl.BlockSpec((1, T, 1), lambda h, j: (h if Hb > 1 else 0, j, 0)),
                rows_spec, rows_spec,
                pl.BlockSpec((1, L, dv), lambda h, j: (h, 0, 0))]
    out_specs = [pl.BlockSpec((1, L, dk), lambda h, j: (h, 0, 0)),
                 pl.BlockSpec((1, T, dk), lambda h, j: (h, j, 0)),
                 pl.BlockSpec((1, T, dv), lambda h, j: (h, j, 0))]
    out_shape = [jax.ShapeDtypeStruct((H, L, dk), F32), jax.ShapeDtypeStruct((H, L, dk), F32),
                 jax.ShapeDtypeStruct((H, L, dv), F32)]
    if fox:
        out_specs += [rows_spec, pl.BlockSpec((1, T, 1), lambda h, j: (h, j, 0))]
        out_shape += [jax.ShapeDtypeStruct((H, nt, 1, T), F32), jax.ShapeDtypeStruct((H, L, 1), F32)]
    any_spec = pl.BlockSpec(memory_space=pl.ANY)
    outs = pl.pallas_call(
        body, name=name, grid=(H, nt),
        in_specs=in_specs + [any_spec] * sn, out_specs=out_specs + [any_spec] * sn,
        out_shape=out_shape + ([] if side is None else side.out_shape),
        scratch_shapes=[pltpu.VMEM((T, T), F32)] * 4 + ([] if side is None else side.scratch),
        compiler_params=_side_cparams(side, ("parallel", "arbitrary")),
    )(q, k, v, kbias_col, lse_row, delta_row, do, *([] if side is None else side.arrs))
    main, rest = outs[:len(outs) - sn], outs[len(outs) - sn:]
    return (*(main if fox else (*main, None, None)), *rest)


def _attn_bwd_window(q, k, v, kbias, o, lse, do, *, name):
    H, L, dk = q.shape
    Hkv = k.shape[0]
    dv = v.shape[2]
    G = H // Hkv
    T = WINDOW
    nt = L // T
    R = G * T

    def body(q_ref, kc_ref, kp_ref, vc_ref, vp_ref, kb_ref, o_ref, lse_ref, do_ref, dq_ref, dk_ref, dv_ref,
             dk_sc, dv_sc):
        i = pl.program_id(1)
        row = lax.broadcasted_iota(jnp.int32, (R, T), 0) & (T - 1)
        col = lax.broadcasted_iota(jnp.int32, (R, T), 1)

        @pl.when(i == 0)
        def _():
            dk_sc[...] = jnp.zeros_like(dk_sc)
            dv_sc[...] = jnp.zeros_like(dv_sc)

        @pl.when(i == nt)
        def _():
            dk_ref[0] = dk_sc[...]
            dv_ref[0] = dv_sc[...]

        @pl.when(i < nt)
        def _():
            qb = q_ref[...].reshape(R, dk)
            dof = do_ref[...].reshape(R, dv)
            dob = dof.astype(BF16)
            lse_c = lse_ref[...].reshape(R, 1)
            delta = jnp.sum(dof * o_ref[...].reshape(R, dv), axis=1, keepdims=True)

            def grads(kt, vt, kbias_j, mask):
                s = lax.dot_general(qb, kt, _NT, preferred_element_type=F32) - kbias_j
                p = jnp.exp(jnp.where(mask, s, NEG) - lse_c)
                dp = lax.dot_general(dob, vt, _NT, preferred_element_type=F32)
                ds = (p * (dp - delta)).astype(BF16)
                return (jnp.dot(ds, kt, preferred_element_type=F32),
                        lax.dot_general(ds, qb, _TN, preferred_element_type=F32),
                        lax.dot_general(p.astype(BF16), dob, _TN, preferred_element_type=F32))

            ip = jnp.maximum(i - 1, 0)
            dq_p, dk_p, dv_p = grads(kp_ref[0], vp_ref[0], kb_ref[0, ip], (col > row) & (i > 0))
            dq_c, dk_c, dv_c = grads(kc_ref[0], vc_ref[0], kb_ref[0, i], row >= col)
            dq_ref[...] = (dq_p + dq_c).reshape(G, T, dk)
            dk_ref[0] = dk_sc[...] + dk_p
            dv_ref[0] = dv_sc[...] + dv_p
            dk_sc[...] = dk_c
            dv_sc[...] = dv_c

    def cur(i):
        return jnp.minimum(i, nt - 1)

    def prev(i):
        return jnp.maximum(jnp.minimum(i, nt - 1) - 1, 0)

    def written(i):
        return jnp.maximum(i - 1, 0)

    qs = lambda d: pl.BlockSpec((G, T, d), lambda h, i: (h, cur(i), 0))
    return pl.pallas_call(
        body, name=name, grid=(Hkv, nt + 1),
        in_specs=[qs(dk),
                  pl.BlockSpec((1, T, dk), lambda h, i: (h, cur(i), 0)),
                  pl.BlockSpec((1, T, dk), lambda h, i: (h, prev(i), 0)),
                  pl.BlockSpec((1, T, dv), lambda h, i: (h, cur(i), 0)),
                  pl.BlockSpec((1, T, dv), lambda h, i: (h, prev(i), 0)),
                  pl.BlockSpec((1, nt, 1, T), lambda h, i: (0, 0, 0, 0)),
                  qs(dv), qs(1), qs(dv)],
        out_specs=[qs(dk),
                   pl.BlockSpec((1, T, dk), lambda h, i: (h, written(i), 0)),
                   pl.BlockSpec((1, T, dv), lambda h, i: (h, written(i), 0))],
        out_shape=[jax.ShapeDtypeStruct((H, L, dk), F32), jax.ShapeDtypeStruct((Hkv, L, dk), F32),
                   jax.ShapeDtypeStruct((Hkv, L, dv), F32)],
        scratch_shapes=[pltpu.VMEM((T, dk), F32), pltpu.VMEM((T, dv), F32)],
        compiler_params=_cparams(("parallel", "arbitrary")),
    )(q, k, k, v, v, kbias, o, lse, do)


def _adamw(w, gparts, m, v, *, name):
    n, R, C = gparts.shape
    tr = _pick(R, (128, 64, 32, 16, 8))
    c1 = 1.0 - ADAM_B1 ** ADAM_STEP
    c2 = 1.0 - ADAM_B2 ** ADAM_STEP

    def body(w_ref, g_ref, m_ref, v_ref, go_ref, d_ref, mo_ref, vo_ref):
        g = g_ref[0].astype(F32)
        for t in range(1, n):
            g = g + g_ref[t].astype(F32)
        mn = ADAM_B1 * m_ref[...] + (1.0 - ADAM_B1) * g
        vn = ADAM_B2 * v_ref[...] + (1.0 - ADAM_B2) * (g * g)
        go_ref[...] = g
        mo_ref[...] = mn
        vo_ref[...] = vn
        d_ref[...] = -ADAM_LR * ((mn / c1) / (jnp.sqrt(vn / c2) + ADAM_EPS) + ADAM_WD * w_ref[...])

    spec = pl.BlockSpec((tr, C), lambda i: (i, 0))
    return pl.pallas_call(
        body, name=name, grid=(R // tr,),
        in_specs=[spec, pl.BlockSpec((n, tr, C), lambda i: (0, i, 0)), spec, spec],
        out_specs=[spec] * 4,
        out_shape=[jax.ShapeDtypeStruct((R, C), F32)] * 4,
        compiler_params=_cparams(("parallel",)),
    )(w, gparts, m, v)


def _me():
    return lax.axis_index("x"), lax.axis_index("y"), lax.axis_index("c")


class _CommJob:
    def __init__(self, arrs):
        self.arrs = list(arrs)
        self.n = len(arrs)
        self.scratch = [pltpu.SemaphoreType.DMA((self.n, 7)), pltpu.SemaphoreType.DMA((self.n, 7)),
                        pltpu.SemaphoreType.DMA((self.n,))]

    def bind(self, refs):
        n = self.n
        self.ins, self.outs = refs[:n], refs[n:2 * n]
        self.send_sems, self.recv_sems, self.local_sems = refs[2 * n:2 * n + 3]

    def middle(self):
        pass


class _Gather(_CommJob):
    def __init__(self, arrs):
        super().__init__(arrs)
        self.out_shape = [jax.ShapeDtypeStruct((N_DEV, *a.shape), a.dtype) for a in arrs]

    def _where(self):
        x, y, c = _me()
        return (x, y, c), (x, y, 1 - c), [(1 - x, y), (x, 1 - y), (1 - x, 1 - y)], c

    def _copy(self, t, k, block, to, src=None):
        dst = self.outs[t].at[4 * block[0] + 2 * block[1] + block[2]]
        return pltpu.make_async_remote_copy(
            src_ref=dst if src is None else src, dst_ref=dst,
            send_sem=self.send_sems.at[t, k], recv_sem=self.recv_sems.at[t, k],
            device_id=to, device_id_type=MESH)

    def _mine(self, t, me):
        return pltpu.make_async_copy(self.ins[t], self.outs[t].at[4 * me[0] + 2 * me[1] + me[2]],
                                     self.local_sems.at[t])

    def _first(self, t, me, sibling, chips, c):
        return [self._copy(t, 0, me, sibling, src=self.ins[t])] + \
               [self._copy(t, 1 + j, me, (*chip, c), src=self.ins[t]) for j, chip in enumerate(chips)]

    def start(self):
        me, sibling, chips, c = self._where()
        for t in range(self.n):
            self._mine(t, me).start()
            for cp in self._first(t, me, sibling, chips, c):
                cp.start()

    def middle(self):
        me, sibling, chips, c = self._where()
        for j, chip in enumerate(chips):
            for t in range(self.n):
                self._copy(t, 1 + j, (*chip, c), me).wait_recv()
                self._copy(t, 4 + j, (*chip, c), sibling).start()

    def finish(self):
        me, sibling, chips, c = self._where()
        for t in range(self.n):
            self._copy(t, 0, sibling, me).wait_recv()
            for j, chip in enumerate(chips):
                self._copy(t, 4 + j, (*chip, 1 - c), me).wait_recv()
        for t in range(self.n):
            for cp in self._first(t, me, sibling, chips, c):
                cp.wait_send()
            for j, chip in enumerate(chips):
                self._copy(t, 4 + j, (*chip, c), sibling).wait_send()
            self._mine(t, me).wait()


class _Exchange(_CommJob):
    def __init__(self, arrs):
        super().__init__(arrs)
        self.out_shape = [jax.ShapeDtypeStruct(a.shape, a.dtype) for a in arrs]

    def _copies(self, t):
        x, y, c = _me()
        my_idx = 4 * x + 2 * y + c
        pairs = []
        for k in range(1, N_DEV):
            peer = (x ^ ((k >> 2) & 1), y ^ ((k >> 1) & 1), c ^ (k & 1))
            peer_idx = 4 * peer[0] + 2 * peer[1] + peer[2]
            sems = dict(send_sem=self.send_sems.at[t, k - 1], recv_sem=self.recv_sems.at[t, k - 1],
                        device_id=peer, device_id_type=MESH)
            pairs.append((pltpu.make_async_remote_copy(src_ref=self.ins[t].at[peer_idx],
                                                       dst_ref=self.outs[t].at[my_idx], **sems),
                          pltpu.make_async_remote_copy(src_ref=self.ins[t].at[peer_idx],
                                                       dst_ref=self.outs[t].at[peer_idx], **sems)))
        return pairs

    def _mine(self, t):
        x, y, c = _me()
        my_idx = 4 * x + 2 * y + c
        return pltpu.make_async_copy(self.ins[t].at[my_idx], self.outs[t].at[my_idx], self.local_sems.at[t])

    def start(self):
        for t in range(self.n):
            self._mine(t).start()
            for snd, _ in self._copies(t):
                snd.start()

    def finish(self):
        for t in range(self.n):
            pairs = self._copies(t)
            for _, rcv in pairs:
                rcv.wait_recv()
            for snd, _ in pairs:
                snd.wait_send()
            self._mine(t).wait()


def _comm(job, *, name):
    def body(*refs):
        job.bind(refs)
        job.start()
        job.middle()
        job.finish()

    any_spec = pl.BlockSpec(memory_space=pl.ANY)
    return pl.pallas_call(
        body, name=name,
        in_specs=[any_spec] * job.n, out_specs=[any_spec] * job.n,
        out_shape=job.out_shape, scratch_shapes=job.scratch,
        compiler_params=pltpu.CompilerParams(has_side_effects=True),
    )(*job.arrs)


def _ride(job, refs, step, total):
    job.bind(refs)

    @pl.when(step == 0)
    def _():
        job.start()

    @pl.when(step == (total * 3) // 5)
    def _():
        job.middle()

    @pl.when(step == total - 1)
    def _():
        job.finish()


SCALE_A = DH ** -0.5
SCALE_B = (MLA_NOPE + MLA_ROPE) ** -0.5
SCALE_C = DH ** -0.5
DQK_B = MLA_NOPE + MLA_ROPE


def _rope_tables(L, half, width):
    pos = (jnp.arange(L) - PAD).astype(F32)
    lane = jnp.arange(width)
    inv = ROPE_THETA ** (-(lane % half).astype(F32) / half)
    ang = pos[:, None] * inv[None, :]
    sign = jnp.where(lane % (2 * half) < half, -1.0, 1.0).astype(F32)
    return jnp.cos(ang), jnp.sin(ang) * sign[None, :]


def _rope_lanes(x, cos, sin, half):
    W = x.shape[1]
    lane = lax.broadcasted_iota(jnp.int32, x.shape, 1)
    first = (lane & (2 * half - 1)) < half
    partner = jnp.where(first, pltpu.roll(x, W - half, 1), pltpu.roll(x, half, 1))
    return x * cos + partner * sin


def _tile_lanes(t, width):
    return t if t.shape[1] == width else jnp.tile(t, (1, width // t.shape[1]))


def _split(x, H, d, dst):
    for h in range(H):
        dst[h] = x[:, d * h:d * (h + 1)].astype(dst.dtype)


def _join(src, H):
    return jnp.concatenate([src[h] for h in range(H)], axis=1)


def _head_spec(H, tm, d):
    return pl.BlockSpec((H, tm, d), lambda i: (0, i, 0))


def _prep_a(proj, *, name):
    L = proj.shape[0]
    tm = ROW_T

    def body(x_ref, qo, ko, vo):
        _split(x_ref[:, 0:512] * SCALE_A, HEADS, DH, qo)
        _split(x_ref[:, 512:1024], HEADS, DH, ko)
        _split(x_ref[:, 1024:1536], HEADS, DH, vo)

    return pl.pallas_call(
        body, name=name, grid=(L // tm,),
        in_specs=[_col_spec(tm, 1536, QKVA)],
        out_specs=[_head_spec(HEADS, tm, DH)] * 3,
        out_shape=[jax.ShapeDtypeStruct((HEADS, L, DH), BF16)] * 3,
        compiler_params=_cparams(("parallel",)),
    )(proj)


def _unprep_a(dq, dk, dv, dproj, *, name):
    L = dq.shape[1]
    tm = ROW_T

    def body(dq_ref, dk_ref, dv_ref, _, dp_ref):
        dp_ref[:, 0:512] = (_join(dq_ref, HEADS) * SCALE_A).astype(BF16)
        dp_ref[:, 512:1024] = _join(dk_ref, HEADS).astype(BF16)
        dp_ref[:, 1024:1536] = _join(dv_ref, HEADS).astype(BF16)

    hs = _head_spec(HEADS, tm, DH)
    return pl.pallas_call(
        body, name=name, grid=(L // tm,),
        in_specs=[hs, hs, hs, pl.BlockSpec(memory_space=pl.ANY)],
        out_specs=_col_spec(tm, 1536, QKVA),
        out_shape=jax.ShapeDtypeStruct(dproj.shape, dproj.dtype),
        input_output_aliases={3: 0},
        compiler_params=_cparams(("parallel",)),
    )(dq, dk, dv, dproj)


def _prep_c(proj, tab, *, name):
    L = proj.shape[0]
    tm = ROW_T

    def body(x_ref, cos_ref, sin_ref, qo, ko, vo):
        cos, sin = cos_ref[...], sin_ref[...]
        q = _rope_lanes(x_ref[:, 0:512], _tile_lanes(cos, 512), _tile_lanes(sin, 512), DH // 2)
        _split(q * SCALE_C, HEADS, DH, qo)
        _split(_rope_lanes(x_ref[:, 512:640], cos, sin, DH // 2), SWA_KV_HEADS, DH, ko)
        _split(x_ref[:, 640:768], SWA_KV_HEADS, DH, vo)

    t128 = pl.BlockSpec((tm, LANES), lambda i: (i, 0))
    return pl.pallas_call(
        body, name=name, grid=(L // tm,),
        in_specs=[_col_spec(tm, 768, QKVC), t128, t128],
        out_specs=[_head_spec(HEADS, tm, DH), _head_spec(SWA_KV_HEADS, tm, DH), _head_spec(SWA_KV_HEADS, tm, DH)],
        out_shape=[jax.ShapeDtypeStruct((HEADS, L, DH), BF16), jax.ShapeDtypeStruct((SWA_KV_HEADS, L, DH), BF16),
                   jax.ShapeDtypeStruct((SWA_KV_HEADS, L, DH), BF16)],
        compiler_params=_cparams(("parallel",)),
    )(proj, *tab)


def _unprep_c(dq, dk, dv, tab, dproj, *, name):
    L = dq.shape[1]
    tm = ROW_T

    def body(dq_ref, dk_ref, dv_ref, cos_ref, sin_ref, _, dp_ref):
        cos, nsin = cos_ref[...], -sin_ref[...]
        dqv = _join(dq_ref, HEADS) * SCALE_C
        dp_ref[:, 0:512] = _rope_lanes(dqv, _tile_lanes(cos, 512), _tile_lanes(nsin, 512), DH // 2).astype(BF16)
        dp_ref[:, 512:640] = _rope_lanes(_join(dk_ref, SWA_KV_HEADS), cos, nsin, DH // 2).astype(BF16)
        dp_ref[:, 640:768] = _join(dv_ref, SWA_KV_HEADS).astype(BF16)

    hs = _head_spec(HEADS, tm, DH)
    hkv = _head_spec(SWA_KV_HEADS, tm, DH)
    t128 = pl.BlockSpec((tm, LANES), lambda i: (i, 0))
    return pl.pallas_call(
        body, name=name, grid=(L // tm,),
        in_specs=[hs, hkv, hkv, t128, t128, pl.BlockSpec(memory_space=pl.ANY)],
        out_specs=_col_spec(tm, 768, QKVC),
        out_shape=jax.ShapeDtypeStruct(dproj.shape, dproj.dtype),
        input_output_aliases={5: 0},
        compiler_params=_cparams(("parallel",)),
    )(dq, dk, dv, *tab, dproj)


def _q_tables(cos, sin, on_ref):
    on = on_ref[...] > 0.5
    width = on_ref.shape[1]
    return jnp.where(on, _tile_lanes(cos, width), 1.0), jnp.where(on, _tile_lanes(sin, width), 0.0)


def _prep_b(qbm, kvbm, proj, tab_k, q_rotary, *, name):
    L = proj.shape[0]
    tm = ROW_T

    def body(q_ref, kv_ref, kr_ref, ck_ref, sk_ref, on_ref, qo, ko, vo):
        cq, sq = _q_tables(ck_ref[...], sk_ref[...], on_ref)
        q = _rope_lanes(q_ref[...], cq, sq, MLA_ROPE // 2) * SCALE_B
        _split(q, HEADS, DQK_B, qo)
        kr = _rope_lanes(kr_ref[...], ck_ref[...], sk_ref[...], MLA_ROPE // 2)[:, :MLA_ROPE].astype(BF16)
        kv = kv_ref[...]
        for h in range(HEADS):
            ko[h] = jnp.concatenate([kv[:, 128 * h:128 * h + MLA_NOPE].astype(BF16), kr], axis=1)
            vo[h] = kv[:, 128 * h + MLA_NOPE:128 * (h + 1)].astype(BF16)

    t128 = pl.BlockSpec((tm, LANES), lambda i: (i, 0))
    t768 = pl.BlockSpec((tm, 768), lambda i: (i, 0))
    return pl.pallas_call(
        body, name=name, grid=(L // tm,),
        in_specs=[t768, pl.BlockSpec((tm, 1024), lambda i: (i, 0)), _col_spec(tm, 128, KR), t128, t128,
                  pl.BlockSpec((1, 768), lambda i: (0, 0))],
        out_specs=[_head_spec(HEADS, tm, DQK_B), _head_spec(HEADS, tm, DQK_B), _head_spec(HEADS, tm, DH)],
        out_shape=[jax.ShapeDtypeStruct((HEADS, L, DQK_B), BF16), jax.ShapeDtypeStruct((HEADS, L, DQK_B), BF16),
                   jax.ShapeDtypeStruct((HEADS, L, DH), BF16)],
        compiler_params=_cparams(("parallel",)),
    )(qbm, kvbm, proj, *tab_k, q_rotary)


def _unprep_b(dq, dk, dv, tab_k, q_rotary, dproj, *, name):
    L = dq.shape[1]
    tm = ROW_T

    def body(dq_ref, dk_ref, dv_ref, ck_ref, sk_ref, on_ref, _, dp_kr, dqo, dkvo):
        cq, sq = _q_tables(ck_ref[...], sk_ref[...], on_ref)
        dqv = _join(dq_ref, HEADS) * SCALE_B
        dqo[...] = _rope_lanes(dqv, cq, -sq, MLA_ROPE // 2).astype(BF16)
        parts = []
        dkr = None
        for h in range(HEADS):
            dkh = dk_ref[h]
            parts += [dkh[:, :MLA_NOPE], dv_ref[h]]
            r = dkh[:, MLA_NOPE:]
            dkr = r if dkr is None else dkr + r
        dkvo[...] = jnp.concatenate(parts, axis=1).astype(BF16)
        dkr = jnp.concatenate([dkr, jnp.zeros((tm, LANES - MLA_ROPE), F32)], axis=1)
        dp_kr[...] = _rope_lanes(dkr, ck_ref[...], -sk_ref[...], MLA_ROPE // 2).astype(BF16)

    t128 = pl.BlockSpec((tm, LANES), lambda i: (i, 0))
    t768 = pl.BlockSpec((tm, 768), lambda i: (i, 0))
    hq = _head_spec(HEADS, tm, DQK_B)
    return pl.pallas_call(
        body, name=name, grid=(L // tm,),
        in_specs=[hq, hq, _head_spec(HEADS, tm, DH), t128, t128, pl.BlockSpec((1, 768), lambda i: (0, 0)),
                  pl.BlockSpec(memory_space=pl.ANY)],
        out_specs=[_col_spec(tm, 128, KR), t768, pl.BlockSpec((tm, 1024), lambda i: (i, 0))],
        out_shape=[jax.ShapeDtypeStruct(dproj.shape, dproj.dtype), jax.ShapeDtypeStruct((L, 768), BF16),
                   jax.ShapeDtypeStruct((L, 1024), BF16)],
        input_output_aliases={6: 0},
        compiler_params=_cparams(("parallel",)),
    )(dq, dk, dv, *tab_k, q_rotary, dproj)


def _gate(y, proj, zcol, *, name):
    L = proj.shape[0]
    tm = ROW_T

    def body(y_ref, z_ref, u_ref):
        z = z_ref[...]
        u_ref[...] = (_join(y_ref, HEADS) * (z * jax.nn.sigmoid(z))).astype(BF16)

    return pl.pallas_call(
        body, name=name, grid=(L // tm,),
        in_specs=[_head_spec(HEADS, tm, DH), _col_spec(tm, 512, zcol)],
        out_specs=pl.BlockSpec((tm, 512), lambda i: (i, 0)),
        out_shape=jax.ShapeDtypeStruct((L, 512), BF16),
        compiler_params=_cparams(("parallel",)),
    )(y, proj)


def _gate_bwd(du, y, proj, zcol, dproj, *, delta_rows=False, name):
    L = proj.shape[0]
    tm = ROW_T
    assert not delta_rows or tm == ATT_T

    def body(du_ref, y_ref, z_ref, _, dz_ref, dy_ref, *dl_ref):
        z = z_ref[...]
        duv = du_ref[...]
        sg = jax.nn.sigmoid(z)
        yv = _join(y_ref, HEADS)
        dyv = duv * (z * sg)
        _split(dyv, HEADS, DH, dy_ref)
        dz_ref[...] = (duv * yv * (sg * (1.0 + z * (1.0 - sg)))).astype(BF16)
        if delta_rows:
            prod = dyv * yv
            ones = jnp.ones((8, DH), F32)
            for h in range(HEADS):
                sums = lax.dot_general(ones, prod[:, DH * h:DH * (h + 1)], _NT, preferred_element_type=F32)
                dl_ref[0][h, 0] = sums[0:1, :]

    hs = _head_spec(HEADS, tm, DH)
    out_specs = [_col_spec(tm, 512, zcol), hs]
    out_shape = [jax.ShapeDtypeStruct(dproj.shape, dproj.dtype),
                 jax.ShapeDtypeStruct((HEADS, L, DH), BF16 if delta_rows else F32)]
    if delta_rows:
        out_specs.append(pl.BlockSpec((HEADS, 1, 1, tm), lambda i: (0, i, 0, 0)))
        out_shape.append(jax.ShapeDtypeStruct((HEADS, L // tm, 1, tm), F32))
    return pl.pallas_call(
        body, name=name, grid=(L // tm,),
        in_specs=[pl.BlockSpec((tm, 512), lambda i: (i, 0)), hs, _col_spec(tm, 512, zcol),
                  pl.BlockSpec(memory_space=pl.ANY)],
        out_specs=out_specs, out_shape=out_shape,
        input_output_aliases={3: 0},
        compiler_params=_cparams(("parallel",)),
    )(du, y, proj, dproj)


MERGE_T = 192


def _branch_merge(us, w_branch, proj, *, name):
    L = proj.shape[0]
    tm = ROW_T

    def body(u0, u1, u2, w_ref, g_ref, m_ref, pb_ref):
        acc = None
        for n, u_ref in enumerate((u0, u1, u2)):
            pb = jnp.dot(u_ref[...], w_ref[n], preferred_element_type=F32)
            pb_ref[n] = pb.astype(BF16)
            t = jax.nn.sigmoid(g_ref[:, n * D_MODEL:(n + 1) * D_MODEL]) * pb
            acc = t if acc is None else acc + t
        m_ref[...] = acc.astype(BF16)

    urow = pl.BlockSpec((tm, BRANCH_W), lambda i: (i, 0))
    return pl.pallas_call(
        body, name=name, grid=(L // tm,),
        in_specs=[urow] * N_BRANCH + [pl.BlockSpec((N_BRANCH, BRANCH_W, D_MODEL), lambda i: (0, 0, 0)),
                                      _col_spec(tm, N_BRANCH * D_MODEL, GATES)],
        out_specs=[pl.BlockSpec((tm, D_MODEL), lambda i: (i, 0)),
                   pl.BlockSpec((N_BRANCH, tm, D_MODEL), lambda i: (0, i, 0))],
        out_shape=[jax.ShapeDtypeStruct((L, D_MODEL), BF16), jax.ShapeDtypeStruct((N_BRANCH, L, D_MODEL), BF16)],
        compiler_params=_cparams(("parallel",)),
    )(*us, w_branch, proj)


def _merge_bwd(dmerged, proj, pb, dproj, *, name):
    L = proj.shape[0]
    tm = MERGE_T

    def body(dm_ref, g_ref, p_ref, _, dg_ref, dp0, dp1, dp2):
        dm = dm_ref[...]
        for n, dp_ref in enumerate((dp0, dp1, dp2)):
            cols = slice(n * D_MODEL, (n + 1) * D_MODEL)
            sg = jax.nn.sigmoid(g_ref[:, cols])
            dp_ref[...] = (dm * sg).astype(BF16)
            dg_ref[:, cols] = (dm * p_ref[n].astype(F32) * (sg * (1.0 - sg))).astype(BF16)

    row = pl.BlockSpec((tm, D_MODEL), lambda i: (i, 0))
    gates = _col_spec(tm, N_BRANCH * D_MODEL, GATES)
    outs = pl.pallas_call(
        body, name=name, grid=(L // tm,),
        in_specs=[row, gates, pl.BlockSpec((N_BRANCH, tm, D_MODEL), lambda i: (0, i, 0)),
                  pl.BlockSpec(memory_space=pl.ANY)],
        out_specs=[gates] + [row] * N_BRANCH,
        out_shape=[jax.ShapeDtypeStruct(dproj.shape, dproj.dtype)]
        + [jax.ShapeDtypeStruct((L, D_MODEL), BF16)] * N_BRANCH,
        input_output_aliases={3: 0},
        compiler_params=_cparams(("parallel",)),
    )(dmerged, proj, pb, dproj)
    return outs[0], outs[1:]


def _scan_rows(x, reverse):
    rows = lax.broadcasted_iota(jnp.int32, x.shape, 0)
    step = 1
    while step < BLK:
        if reverse:
            x = x + jnp.where(rows < BLK - step, pltpu.roll(x, BLK - step, 0), 0.0)
        else:
            x = x + jnp.where(rows >= step, pltpu.roll(x, step, 0), 0.0)
        step *= 2
    return x


def _forget_fwd(proj, b_f, *, name):
    L = proj.shape[0]
    nb = L // BLK

    def body(x_ref, b_ref, o_ref, carry):
        i = pl.program_id(0)

        @pl.when(i == 0)
        def _():
            carry[...] = jnp.zeros_like(carry)

        c = _scan_rows(jax.nn.log_sigmoid(x_ref[...] + b_ref[...]), False) + carry[...]
        carry[...] = c[BLK - 1:BLK, :]
        pos = i * BLK + lax.broadcasted_iota(jnp.int32, (HEADS, BLK), 1)
        o_ref[...] = c.T[:HEADS, :] + jnp.where(pos < PAD, BIG, 0.0)

    return pl.pallas_call(
        body, name=name, grid=(nb,),
        in_specs=[_col_spec(BLK, LANES, FA), pl.BlockSpec((1, LANES), lambda i: (0, 0))],
        out_specs=pl.BlockSpec((HEADS, BLK), lambda i: (0, i)),
        out_shape=jax.ShapeDtypeStruct((HEADS, L), F32),
        scratch_shapes=[pltpu.VMEM((1, LANES), F32)],
        compiler_params=_cparams(("arbitrary",)),
    )(proj, jnp.pad(b_f, (0, LANES - HEADS)).reshape(1, LANES))


def _forget_bwd(dct, proj, b_f, dproj, *, name):
    L = proj.shape[0]
    nb = L // BLK

    def body(d_ref, x_ref, b_ref, _, dp_ref, db_ref, carry):
        i = pl.program_id(0)

        @pl.when(i == 0)
        def _():
            carry[...] = jnp.zeros_like(carry)
            db_ref[...] = jnp.zeros_like(db_ref)

        d = jnp.concatenate([d_ref[...], jnp.zeros((BLK - HEADS, BLK), F32)], axis=0).T
        dlog = _scan_rows(d, True) + carry[...]
        carry[...] = dlog[0:1, :]
        lane = lax.broadcasted_iota(jnp.int32, (BLK, LANES), 1)
        daf = jnp.where(lane < HEADS, dlog * jax.nn.sigmoid(-(x_ref[...] + b_ref[...])), 0.0)
        db_ref[...] += jnp.sum(daf, axis=0, keepdims=True)
        dp_ref[...] = jnp.concatenate([daf, jnp.zeros((BLK, LANES), F32)], axis=1).astype(BF16)

    back = lambda i: nb - 1 - i
    dp, db = pl.pallas_call(
        body, name=name, grid=(nb,),
        in_specs=[pl.BlockSpec((HEADS, BLK), lambda i: (0, back(i))),
                  pl.BlockSpec((BLK, LANES), lambda i: (back(i), FA // LANES)),
                  pl.BlockSpec((1, LANES), lambda i: (0, 0)), pl.BlockSpec(memory_space=pl.ANY)],
        out_specs=[pl.BlockSpec((BLK, 2 * LANES), lambda i: (back(i), FA // (2 * LANES))),
                   pl.BlockSpec((1, LANES), lambda i: (0, 0))],
        out_shape=[jax.ShapeDtypeStruct(dproj.shape, dproj.dtype), jax.ShapeDtypeStruct((1, LANES), F32)],
        scratch_shapes=[pltpu.VMEM((1, LANES), F32)],
        input_output_aliases={3: 0},
        compiler_params=_cparams(("arbitrary",)),
    )(dct, proj, jnp.pad(b_f, (0, LANES - HEADS)).reshape(1, LANES), dproj)
    return dp, db[0, :HEADS]


def _key_bias(L, T, ct=None):
    padb = jnp.where(jnp.arange(L) < PAD, BIG, 0.0).astype(F32)[None]
    kb = padb if ct is None else ct + padb
    return kb.reshape(kb.shape[0], L // T, 1, T)


def _layer_fwd(h, w, tabs, l, side=None):
    tag = f"l{l}"
    L = h.shape[0]
    hn = _rms_fwd(h, w["norm_g"], name=f"{tag}_rms_in")
    if "late" in w:
        late_job, assemble = w["late"]
        proj, *late = _mm(hn, w["w_in"], side=late_job, name=f"{tag}_mm_in")
        w = {**{key: val for key, val in w.items() if key != "late"}, **assemble(*late)}
    else:
        proj = _mm(hn, w["w_in"], name=f"{tag}_mm_in")
    kb_a = _forget_fwd(proj, w["b_f"], name=f"{tag}_forget").reshape(HEADS, L // ATT_T, 1, ATT_T)
    ops_a = (*_prep_a(proj, name=f"{tag}_prep_a"), kb_a)
    ya, lsea, *side_out = _attn_fwd(*ops_a, T=ATT_T, side=side, name=f"{tag}_attn_a")
    cqn = _rms_fwd(proj, w["g_cq"], col=CQ, name=f"{tag}_rms_cq")
    ckvn = _rms_fwd(proj, w["g_ckv"], col=CKV, name=f"{tag}_rms_ckv")
    qbm = _mm(cqn, w["w_uq"], name=f"{tag}_mm_uq")
    kvbm = _mm(ckvn, w["w_ukv"], name=f"{tag}_mm_ukv")
    ops_b = (*_prep_b(qbm, kvbm, proj, tabs["bk"], tabs["bq"], name=f"{tag}_prep_b"), _key_bias(L, ATT_T))
    yb, lseb = _attn_fwd(*ops_b, T=ATT_T, name=f"{tag}_attn_b")
    ops_c = (*_prep_c(proj, tabs["c"], name=f"{tag}_prep_c"), _key_bias(L, WINDOW))
    sink = jnp.broadcast_to(w["sinks"][:, None, None], (HEADS, 1, LANES))
    yc, lsec = _attn_fwd(*ops_c, T=WINDOW, window=True, sink=sink, name=f"{tag}_attn_c")
    us = [_gate(y, proj, zcol, name=f"{tag}_gate{n}") for n, (y, zcol) in enumerate(((ya, ZA), (yb, ZB), (yc, ZC)))]
    merged, pbr = _branch_merge(us, w["w_branch"], proj, name=f"{tag}_branch_merge")
    out = _mm(merged, w["w_out"], add=h, name=f"{tag}_mm_out")
    saved = dict(h=h, hn=hn, proj=proj, ops_a=ops_a, ya=ya, lsea=lsea, cqn=cqn, ckvn=ckvn,
                 ops_b=ops_b, yb=yb, lseb=lseb, ops_c=ops_c, yc=yc, lsec=lsec, us=us, pbr=pbr, merged=merged)
    return out, saved, side_out, w


def _w_in_chunks(g, tag):
    return _w_in_to_shards(g["w_in"], name=f"{tag}_w_in_chunks")


def _grad_chunks(g, tag, with_w_in=True):
    br = _cut_shards(g["w_branch"].reshape(N_BRANCH * BRANCH_W, D_MODEL), name=f"{tag}_w_branch_chunks")
    rest = [_cut_shards(g["w_uq"], name=f"{tag}_w_uq_chunks"), _cut_shards(g["w_ukv"], name=f"{tag}_w_ukv_chunks"),
            br.reshape(N_DEV, N_BRANCH, BRANCH_W, D_MODEL // N_DEV),
            g["w_out"].reshape(N_DEV, D_MODEL // N_DEV, D_MODEL)]
    return ([_w_in_chunks(g, tag)] if with_w_in else []) + rest


def _layer_bwd(dout, s, w, tabs, l, side=None, own_exchange=False):
    tag = f"l{l}"
    L = dout.shape[0]
    proj = s["proj"]
    g = {}
    dproj = jnp.zeros((L, NP_IN), BF16)
    dmerged = _mm(dout, w["w_out"], tb=True, name=f"{tag}_mm_out_dx")
    g["w_out"] = _mm(s["merged"], dout, ta=True, out_dtype=BF16, name=f"{tag}_mm_out_dw")
    dproj, dpbr = _merge_bwd(dmerged, proj, s["pbr"], dproj, name=f"{tag}_merge_bwd")
    dus = [_mm(dpbr[n], w["w_branch"][n], tb=True, name=f"{tag}_mm_br{n}_dx") for n in range(N_BRANCH)]
    g["w_branch"] = jnp.stack([_mm(s["us"][n], dpbr[n], ta=True, out_dtype=BF16, name=f"{tag}_mm_br{n}_dw")
                               for n in range(N_BRANCH)])
    dproj, dya, dla = _gate_bwd(dus[0], s["ya"], proj, ZA, dproj, delta_rows=True, name=f"{tag}_gate0_bwd")
    dproj, dyb, dlb = _gate_bwd(dus[1], s["yb"], proj, ZB, dproj, delta_rows=True, name=f"{tag}_gate1_bwd")
    dproj, dyc = _gate_bwd(dus[2], s["yc"], proj, ZC, dproj, name=f"{tag}_gate2_bwd")

    def bwd_operands(ops, lse, delta, dy16):
        q16, k16, v16, kbias = ops
        return (q16, k16, v16, kbias.reshape(kbias.shape[0], L, 1), lse.reshape(HEADS, L // ATT_T, 1, ATT_T), delta,
                dy16)

    dqa, dka, dva, dcq, dck, *side_out = _attn_bwd_t(*bwd_operands(s["ops_a"], s["lsea"], dla, dya), T=ATT_T,
                                                     fox=True, side=side, name=f"{tag}_attn_a_bwd")
    dproj = _unprep_a(dqa, dka, dva, dproj, name=f"{tag}_unprep_a")
    dproj, g["b_f"] = _forget_bwd(dcq.reshape(HEADS, L) + dck[:, :, 0], proj, w["b_f"], dproj,
                                  name=f"{tag}_forget_bwd")
    dqb, dkb, dvb, _, _ = _attn_bwd_t(*bwd_operands(s["ops_b"], s["lseb"], dlb, dyb), T=ATT_T,
                                      name=f"{tag}_attn_b_bwd")
    dproj, dqbm, dkvbm = _unprep_b(dqb, dkb, dvb, tabs["bk"], tabs["bq"], dproj, name=f"{tag}_unprep_b")
    dcqn = _mm(dqbm, w["w_uq"], tb=True, name=f"{tag}_mm_uq_dx")
    g["w_uq"] = _mm(s["cqn"], dqbm, ta=True, out_dtype=BF16, name=f"{tag}_mm_uq_dw")
    dckvn = _mm(dkvbm, w["w_ukv"], tb=True, name=f"{tag}_mm_ukv_dx")
    g["w_ukv"] = _mm(s["ckvn"], dkvbm, ta=True, out_dtype=BF16, name=f"{tag}_mm_ukv_dw")
    dproj, g["g_cq"] = _rms_bwd(dcqn, proj, w["g_cq"], col=CQ, into=dproj, name=f"{tag}_rms_cq_bwd")
    dproj, g["g_ckv"] = _rms_bwd(dckvn, proj, w["g_ckv"], col=CKV, into=dproj, name=f"{tag}_rms_ckv_bwd")
    dqc, dkc, dvc = _attn_bwd_window(*s["ops_c"], s["yc"], s["lsec"], dyc, name=f"{tag}_attn_c_bwd")
    dproj = _unprep_c(dqc, dkc, dvc, tabs["c"], dproj, name=f"{tag}_unprep_c")
    delta_c = jnp.sum(dyc * s["yc"], axis=-1)
    g["sinks"] = -jnp.sum(jnp.exp(w["sinks"][:, None] - s["lsec"][:, :, 0]) * delta_c, axis=1)
    own_out = []
    if own_exchange:
        g["w_in"], *r_rest = _mm(s["hn"].T, dproj, out_dtype=BF16, side=_Exchange(_grad_chunks(g, tag, False)),
                                 name=f"{tag}_mm_in_dw")
        dhn, r_in = _mm(dproj, w["w_in"], tb=True, side=_Exchange([_w_in_chunks(g, tag)]), name=f"{tag}_mm_in_dx")
        own_out = [r_in, *r_rest]
    else:
        g["w_in"] = _mm(s["hn"].T, dproj, out_dtype=BF16, name=f"{tag}_mm_in_dw")
        dhn = _mm(dproj, w["w_in"], tb=True, name=f"{tag}_mm_in_dx")
    dh, g["norm_g"] = _rms_bwd(dhn, s["h"], w["norm_g"], add=dout, name=f"{tag}_rms_in_bwd")
    return dh, g, side_out, own_out


def _cols_from_shards(g):
    return jnp.moveaxis(g, 0, 1).reshape(g.shape[1], N_DEV * g.shape[2])


def _cols_to_shards(w):
    R = w.shape[0]
    return jnp.moveaxis(w.reshape(R, N_DEV, w.shape[1] // N_DEV), 1, 0)


def _join_shards(g, *, name):
    _, R, C = g.shape
    tr = _pick(R, (512, 384, 256))

    def body(g_ref, o_ref):
        for d in range(N_DEV):
            o_ref[:, C * d:C * (d + 1)] = g_ref[d]

    return pl.pallas_call(
        body, name=name, grid=(R // tr,),
        in_specs=[pl.BlockSpec((N_DEV, tr, C), lambda i: (0, i, 0))],
        out_specs=pl.BlockSpec((tr, N_DEV * C), lambda i: (i, 0)),
        out_shape=jax.ShapeDtypeStruct((R, N_DEV * C), g.dtype),
        compiler_params=_cparams(("parallel",)),
    )(g)


def _cut_shards(w, *, name):
    R = w.shape[0]
    C = w.shape[1] // N_DEV
    tr = _pick(R, (512, 384, 256))

    def body(w_ref, o_ref):
        for d in range(N_DEV):
            o_ref[d] = w_ref[:, C * d:C * (d + 1)]

    return pl.pallas_call(
        body, name=name, grid=(R // tr,),
        in_specs=[pl.BlockSpec((tr, N_DEV * C), lambda i: (i, 0))],
        out_specs=pl.BlockSpec((N_DEV, tr, C), lambda i: (0, i, 0)),
        out_shape=jax.ShapeDtypeStruct((N_DEV, R, C), w.dtype),
        compiler_params=_cparams(("parallel",)),
    )(w)


def _pad_in(w):
    parts, at = [], 0
    for lo, hi, dst in sorted(_RUNS, key=lambda r: r[2]):
        if dst > at:
            parts.append(jnp.zeros((w.shape[0], dst - at), w.dtype))
        parts.append(w[:, lo:hi])
        at = dst + hi - lo
    parts.append(jnp.zeros((w.shape[0], NP_IN - at), w.dtype))
    return jnp.concatenate(parts, axis=1)


def _unpad_in(w):
    return jnp.concatenate([w[:, dst:dst + hi - lo] for lo, hi, dst in _RUNS], axis=1)


W_IN_SHARD = N_IN // N_DEV


def _w_in_pieces():
    pieces = []
    for lo, hi, dst in _RUNS:
        for d in range(N_DEV):
            a, b = max(lo, d * W_IN_SHARD), min(hi, (d + 1) * W_IN_SHARD)
            if a < b:
                pieces.append((d, a - d * W_IN_SHARD, b - d * W_IN_SHARD, dst + a - lo))
    return pieces


def _w_in_from_shards(g, *, name):
    R = g.shape[1]
    tr = 256
    covered = sorted((dst, dst + b - a) for _, a, b, dst in _w_in_pieces())

    def body(g_ref, o_ref):
        at = 0
        for lo, hi in covered + [(NP_IN, NP_IN)]:
            if lo > at:
                o_ref[:, at:lo] = jnp.zeros((tr, lo - at), BF16)
            at = max(at, hi)
        for d, a, b, dst in _w_in_pieces():
            o_ref[:, dst:dst + b - a] = g_ref[d, :, a:b]

    return pl.pallas_call(
        body, name=name, grid=(R // tr,),
        in_specs=[pl.BlockSpec((N_DEV, tr, W_IN_SHARD), lambda i: (0, i, 0))],
        out_specs=pl.BlockSpec((tr, NP_IN), lambda i: (i, 0)),
        out_shape=jax.ShapeDtypeStruct((R, NP_IN), BF16),
        compiler_params=_cparams(("parallel",)),
    )(g)


def _w_in_to_shards(dw, *, name):
    R = dw.shape[0]
    tr = 256

    def body(w_ref, o_ref):
        for d, a, b, dst in _w_in_pieces():
            o_ref[d, :, a:b] = w_ref[:, dst:dst + b - a]

    return pl.pallas_call(
        body, name=name, grid=(R // tr,),
        in_specs=[pl.BlockSpec((tr, NP_IN), lambda i: (i, 0))],
        out_specs=pl.BlockSpec((N_DEV, tr, W_IN_SHARD), lambda i: (0, i, 0)),
        out_shape=jax.ShapeDtypeStruct((N_DEV, R, W_IN_SHARD), BF16),
        compiler_params=_cparams(("parallel",)),
    )(dw)


_SMALL = (("norm_g", DEPTH * D_MODEL), ("b_f", DEPTH * HEADS), ("g_cq", DEPTH * MLA_QLORA),
          ("g_ckv", DEPTH * MLA_KVLORA), ("sinks", DEPTH * HEADS), ("final_g", D_MODEL), ("loss", 1),
          ("meta", N_META * D_MODEL))
SMALL_ROWS = 168


def _pack_small(d):
    parts = []
    for name, size in _SMALL:
        padded = -(-size // 128) * 128
        v = d[name].reshape(-1).astype(F32) if name in d else jnp.zeros((size,), F32)
        parts.append(jnp.pad(v, (0, padded - size)))
    flat = jnp.concatenate(parts)
    return jnp.pad(flat, (0, SMALL_ROWS * 128 - flat.shape[0])).reshape(SMALL_ROWS, 128)


def _unpack_small(p, shapes):
    flat = p.reshape(-1)
    out, at = {}, 0
    for name, size in _SMALL:
        if name in shapes:
            out[name] = flat[at:at + size].reshape(shapes[name])
        at += -(-size // 128) * 128
    return out


def kernel(x, meta_tokens, norm_g, w_in, b_f, g_cq, g_ckv, w_uq, w_ukv, sinks, w_branch, w_out, final_g, loss_target, m_meta_tokens, m_norm_g, m_w_in, m_b_f, m_g_cq, m_g_ckv, m_w_uq, m_w_ukv, m_sinks, m_w_branch, m_w_out, m_final_g, v_meta_tokens, v_norm_g, v_w_in, v_b_f, v_g_cq, v_g_ckv, v_w_uq, v_w_ukv, v_sinks, v_w_branch, v_w_out, v_final_g):
    S = x.shape[1]
    L = BLK + S
    cx, cy, cc = _me()
    my_idx = 4 * cx + 2 * cy + cc

    def shards(l):
        return [t[l].astype(BF16) for t in (w_in, w_uq, w_ukv, w_branch, w_out)]

    def small_weights(l):
        return dict(norm_g=norm_g[l], b_f=b_f[l], g_cq=g_cq[l], g_ckv=g_ckv[l], sinks=sinks[l])

    def rest_weights(l):
        def assemble(gw_uq, gw_ukv, gw_br, gw_out):
            br = _join_shards(gw_br.reshape(N_DEV, N_BRANCH * BRANCH_W, D_MODEL // N_DEV), name=f"l{l}_w_branch_full")
            return dict(w_uq=_join_shards(gw_uq, name=f"l{l}_w_uq_full"),
                        w_ukv=_join_shards(gw_ukv, name=f"l{l}_w_ukv_full"),
                        w_branch=br.reshape(N_BRANCH, BRANCH_W, D_MODEL), w_out=gw_out.reshape(D_MODEL, D_MODEL))
        return assemble

    def layer_weights(l, gw_in, *gw_rest):
        return dict(small_weights(l), w_in=_w_in_from_shards(gw_in, name=f"l{l}_w_in_full"),
                    **rest_weights(l)(*gw_rest))

    gw_in0, g_meta = _comm(_Gather([shards(0)[0], meta_tokens]), name="gather_l0")
    layers = [dict(small_weights(0), w_in=_w_in_from_shards(gw_in0, name="l0_w_in_full"),
                   late=(_Gather(shards(0)[1:]), rest_weights(0))), None]
    meta_full = _cols_from_shards(g_meta)

    h = jnp.concatenate([jnp.zeros((PAD, D_MODEL), F32), meta_full, x[0]], axis=0)
    q_rotary = ((jnp.arange(HEADS * DQK_B) % DQK_B) >= MLA_NOPE).astype(F32)[None, :]
    tabs = dict(c=_rope_tables(L, DH // 2, LANES), bk=_rope_tables(L, MLA_ROPE // 2, LANES), bq=q_rotary)
    saved = [None] * DEPTH
    h, saved[0], gw1, layers[0] = _layer_fwd(h, layers[0], tabs, 0, side=_Gather(shards(1)))
    layers[1] = layer_weights(1, *gw1)
    h, saved[1], _, _ = _layer_fwd(h, layers[1], tabs, 1)
    loss_vec, dh, g_final = _loss_head(h, final_g, loss_target[0], name="loss_head")

    grads = [None] * DEPTH
    dh, grads[1], _, _ = _layer_bwd(dh, saved[1], layers[1], tabs, 1)
    dh, grads[0], recv1, recv0 = _layer_bwd(dh, saved[0], layers[0], tabs, 0, side=_Exchange(_grad_chunks(grads[1], "l1")),
                                            own_exchange=True)
    r_in, r_uq, r_ukv, r_br, r_out = (jnp.stack([a, b], axis=1) for a, b in zip(recv0, recv1))

    def stack(name):
        return jnp.stack([grads[l][name] for l in range(DEPTH)])

    small = _pack_small(dict(norm_g=stack("norm_g"), b_f=stack("b_f"), g_cq=stack("g_cq"), g_ckv=stack("g_ckv"),
                             sinks=stack("sinks"), final_g=g_final, loss=loss_vec[0, 0:1],
                             meta=dh[PAD:BLK]))
    (g_small,) = _comm(_Gather([small]), name="gather_small")

    def adam_big(w_, parts, m_, v_, name):
        shape = w_.shape
        C = shape[-1]
        R = math.prod(shape[:-1])
        outs = _adamw(w_.reshape(R, C), parts.reshape(parts.shape[0], R, C), m_.reshape(R, C), v_.reshape(R, C),
                      name=name)
        return [o.reshape(shape) for o in outs]

    res = {}
    res["w_in"] = adam_big(w_in, r_in, m_w_in, v_w_in, "adam_w_in")
    res["w_uq"] = adam_big(w_uq, r_uq, m_w_uq, v_w_uq, "adam_w_uq")
    res["w_ukv"] = adam_big(w_ukv, r_ukv, m_w_ukv, v_w_ukv, "adam_w_ukv")
    res["w_branch"] = adam_big(w_branch, r_br, m_w_branch, v_w_branch, "adam_w_branch")
    res["w_out"] = adam_big(w_out, r_out, m_w_out, v_w_out, "adam_w_out")

    small_w = dict(norm_g=norm_g, b_f=b_f, g_cq=g_cq, g_ckv=g_ckv, sinks=sinks, final_g=final_g)
    small_m = dict(norm_g=m_norm_g, b_f=m_b_f, g_cq=m_g_cq, g_ckv=m_g_ckv, sinks=m_sinks, final_g=m_final_g)
    small_v = dict(norm_g=v_norm_g, b_f=v_b_f, g_cq=v_g_cq, g_ckv=v_g_ckv, sinks=v_sinks, final_g=v_final_g)
    sm = _adamw(_pack_small(small_w), g_small, _pack_small(small_m), _pack_small(small_v), name="adam_small")
    shapes = {k: a.shape for k, a in small_w.items()}
    shapes_all = dict(shapes, loss=(), meta=(N_META, D_MODEL))
    sm_g = _unpack_small(sm[0], shapes_all)
    sm_d, sm_m, sm_v = (_unpack_small(t, shapes) for t in sm[1:])
    for k in shapes:
        res[k] = [sm_g[k], sm_d[k], sm_m[k], sm_v[k]]
    g_meta_mine = lax.dynamic_slice(sm_g["meta"], (0, my_idx * 128), (N_META, 128))
    res["meta_tokens"] = _adamw(meta_tokens, g_meta_mine[None], m_meta_tokens, v_meta_tokens, name="adam_meta")

    order = ["meta_tokens", "norm_g", "w_in", "b_f", "g_cq", "g_ckv", "w_uq", "w_ukv", "sinks", "w_branch", "w_out",
             "final_g"]
    grad_x = dh[BLK:][None]
    return (sm_g["loss"], grad_x, *[res[k][0] for k in order], *[res[k][1] for k in order],
            *[res[k][2] for k in order], *[res[k][3] for k in order])
```

```python
import functools
import math

import jax
import jax.numpy as jnp
from jax import lax
from jax.experimental import pallas as pl
from jax.experimental.pallas import tpu as pltpu

F32 = jnp.float32
BF16 = jnp.bfloat16

D_MODEL = 1024
DEPTH = 2
N_META = 16
BLK = 128
PAD = BLK - N_META
ROPE_THETA = 10000.0
EPS = 1e-6
NEG = -1e30
BIG = 1e30
HEADS = 8
DH = 64
MLA_NOPE = 64
MLA_ROPE = 32
MLA_QLORA = 384
MLA_KVLORA = 256
SWA_KV_HEADS = 2
WINDOW = 128
BRANCH_W = 512
N_BRANCH = 3
N_IN = 7592

ADAM_LR = 0.001
ADAM_B1 = 0.9
ADAM_B2 = 0.999
ADAM_EPS = 1e-08
ADAM_WD = 0.01
ADAM_STEP = 10

N_DEV = 8
MESH = pl.DeviceIdType.MESH

NP_IN = 8192
QKVA, ZA, FA = 0, 1536, 2048
CKV, KR, CQ = 2304, 2560, 2688
GATES = 3072
ZB = 6144
QKVC, ZC = 6912, 7680
_RUNS = ((0, 1536, QKVA), (1536, 1544, FA), (1544, 2056, ZA), (2056, 2440, CQ), (2440, 2696, CKV), (2696, 2728, KR),
         (2728, 3240, ZB), (3240, 4008, QKVC), (4008, 4520, ZC), (4520, 7592, GATES))

VMEM_LIMIT = 48 * 1024 * 1024
ATT_T = 384
ATT_HP = 1
ROW_T = 384
LANES = 128


def _pick(dim, prefs):
    for p in prefs:
        if dim % p == 0:
            return p
    return dim


def _cparams(sem):
    return pltpu.CompilerParams(dimension_semantics=sem, vmem_limit_bytes=VMEM_LIMIT)


def _side_cparams(side, sem):
    if side is None:
        return _cparams(sem)
    return pltpu.CompilerParams(dimension_semantics=("arbitrary",) * len(sem), vmem_limit_bytes=VMEM_LIMIT,
                                has_side_effects=True)


def _mm(a, b, *, ta=False, tb=False, add=None, out_dtype=F32, side=None, name):
    M = a.shape[1] if ta else a.shape[0]
    K = a.shape[0] if ta else a.shape[1]
    N = b.shape[0] if tb else b.shape[1]
    assert K == (b.shape[1] if tb else b.shape[0])
    tm = _pick(M, (704, 1024, 512, 384, 256, 128))
    tn = _pick(N, (1024, 768, 512, 384, 256, 128))
    tk = _pick(K, (4096, 1408, 1024, 768, 512, 384, 256, 128))
    nk = K // tk
    dims = (((0 if ta else 1,), (1 if tb else 0,)), ((), ()))

    sn = 0 if side is None else side.n
    n_in = 2 + (add is not None)
    grid = (M // tm, N // tn, nk)

    def body(*refs):
        a_ref, b_ref = refs[:2]
        c_ref = refs[2] if add is not None else None
        o_ref = refs[n_in + sn]
        scratch = refs[n_in + 2 * sn + 1:]
        if side is not None:
            step = (pl.program_id(0) * grid[1] + pl.program_id(1)) * nk + pl.program_id(2)
            _ride(side, [*refs[n_in:n_in + sn], *refs[n_in + sn + 1:n_in + 2 * sn + 1], *scratch[nk > 1:]],
                  step, grid[0] * grid[1] * nk)
        r = lax.dot_general(a_ref[...].astype(BF16), b_ref[...].astype(BF16), dims, preferred_element_type=F32)

        def finish(total):
            if c_ref is not None:
                total = total + c_ref[...]
            o_ref[...] = total.astype(out_dtype)

        if nk == 1:
            finish(r)
        else:
            acc = scratch[0]
            k = pl.program_id(2)

            @pl.when(k == 0)
            def _():
                acc[...] = r

            @pl.when(k > 0)
            def _():
                acc[...] += r

            @pl.when(k == nk - 1)
            def _():
                finish(acc[...])

    a_spec = pl.BlockSpec((tk, tm), lambda i, j, k: (k, i)) if ta else pl.BlockSpec((tm, tk), lambda i, j, k: (i, k))
    b_spec = pl.BlockSpec((tn, tk), lambda i, j, k: (j, k)) if tb else pl.BlockSpec((tk, tn), lambda i, j, k: (k, j))
    o_spec = pl.BlockSpec((tm, tn), lambda i, j, k: (i, j))
    any_spec = pl.BlockSpec(memory_space=pl.ANY)
    outs = pl.pallas_call(
        body, name=name,
        grid=grid,
        in_specs=[a_spec, b_spec] + ([o_spec] if add is not None else []) + [any_spec] * sn,
        out_specs=[o_spec] + [any_spec] * sn,
        out_shape=[jax.ShapeDtypeStruct((M, N), out_dtype)] + ([] if side is None else side.out_shape),
        scratch_shapes=([pltpu.VMEM((tm, tn), F32)] if nk > 1 else []) + ([] if side is None else side.scratch),
        compiler_params=_side_cparams(side, ("parallel", "parallel", "arbitrary")),
    )(*((a, b) if add is None else (a, b, add)), *([] if side is None else side.arrs))
    return outs[0] if side is None else outs


def _col_spec(tm, width, col):
    assert col % width == 0
    return pl.BlockSpec((tm, width), lambda i, _c=col // width: (i, _c))


def _rms_fwd(x, g, *, col=0, name):
    L = x.shape[0]
    D = g.shape[0]
    tm = ROW_T

    def body(x_ref, g_ref, y_ref):
        xv = x_ref[...]
        rstd = lax.rsqrt(jnp.mean(xv * xv, axis=-1, keepdims=True) + EPS)
        y_ref[...] = (xv * rstd * g_ref[...]).astype(BF16)

    return pl.pallas_call(
        body, name=name, grid=(L // tm,),
        in_specs=[_col_spec(tm, D, col), pl.BlockSpec((1, D), lambda i: (0, 0))],
        out_specs=pl.BlockSpec((tm, D), lambda i: (i, 0)),
        out_shape=jax.ShapeDtypeStruct((L, D), BF16),
        compiler_params=_cparams(("parallel",)),
    )(x, g.reshape(1, D))


def _rms_bwd(dy, x, g, *, col=0, add=None, into=None, name):
    L = x.shape[0]
    D = g.shape[0]
    tm = ROW_T

    def body(*refs):
        dy_ref, x_ref, g_ref = refs[:3]
        add_ref = refs[3] if add is not None else None
        dx_ref, dg_ref = refs[-2:]
        i = pl.program_id(0)
        xv = x_ref[...]
        dyv = dy_ref[...]
        rstd = lax.rsqrt(jnp.mean(xv * xv, axis=-1, keepdims=True) + EPS)
        xhat = xv * rstd
        part = jnp.sum(dyv * xhat, axis=0, keepdims=True)

        @pl.when(i == 0)
        def _():
            dg_ref[...] = part

        @pl.when(i > 0)
        def _():
            dg_ref[...] += part

        dxh = dyv * g_ref[...]
        dx = rstd * (dxh - xhat * jnp.mean(dxh * xhat, axis=-1, keepdims=True))
        if add_ref is not None:
            dx = dx + add_ref[...]
        dx_ref[...] = dx.astype(dx_ref.dtype)

    row = pl.BlockSpec((tm, D), lambda i: (i, 0))
    in_specs = [row, _col_spec(tm, D, col), pl.BlockSpec((1, D), lambda i: (0, 0))]
    args = [dy, x, g.reshape(1, D)]
    aliases = {}
    if add is not None:
        in_specs.append(row)
        args.append(add)
    if into is not None:
        in_specs.append(pl.BlockSpec(memory_space=pl.ANY))
        args.append(into)
        aliases = {len(args) - 1: 0}
        dx_spec, dx_shape = _col_spec(tm, D, col), jax.ShapeDtypeStruct(into.shape, into.dtype)
    else:
        dx_spec, dx_shape = row, jax.ShapeDtypeStruct((L, D), F32)
    dx, dg = pl.pallas_call(
        body, name=name, grid=(L // tm,),
        in_specs=in_specs,
        out_specs=[dx_spec, pl.BlockSpec((1, D), lambda i: (0, 0))],
        out_shape=[dx_shape, jax.ShapeDtypeStruct((1, D), F32)],
        input_output_aliases=aliases,
        compiler_params=_cparams(("arbitrary",)),
    )(*args)
    return dx, dg.reshape(D)


def _loss_head(h, g, target, *, name):
    L, D = h.shape
    nb = L // BLK

    def body(h_ref, g_ref, t_ref, loss_ref, dh_ref, dg_ref):
        i = pl.program_id(0)

        @pl.when(i == 0)
        def _():
            loss_ref[...] = jnp.zeros_like(loss_ref)
            dg_ref[...] = jnp.zeros_like(dg_ref)
            dh_ref[...] = jnp.zeros_like(dh_ref)

        @pl.when(i > 0)
        def _():
            xv = h_ref[...]
            gv = g_ref[...]
            rstd = lax.rsqrt(jnp.mean(xv * xv, axis=-1, keepdims=True) + EPS)
            xhat = xv * rstd
            err = xhat * gv - t_ref[...]
            row = jnp.mean(err * err, axis=-1, keepdims=True)
            loss_ref[...] += 0.5 * jnp.sum(row, axis=0, keepdims=True)
            dy = err * (1.0 / D)
            dg_ref[...] += jnp.sum(dy * xhat, axis=0, keepdims=True)
            dxh = dy * gv
            dh_ref[...] = rstd * (dxh - xhat * jnp.mean(dxh * xhat, axis=-1, keepdims=True))

    loss, dh, dg = pl.pallas_call(
        body, name=name, grid=(nb,),
        in_specs=[pl.BlockSpec((BLK, D), lambda i: (i, 0)), pl.BlockSpec((1, D), lambda i: (0, 0)),
                  pl.BlockSpec((BLK, D), lambda i: (jnp.maximum(i - 1, 0), 0))],
        out_specs=[pl.BlockSpec((1, 128), lambda i: (0, 0)), pl.BlockSpec((BLK, D), lambda i: (i, 0)),
                   pl.BlockSpec((1, D), lambda i: (0, 0))],
        out_shape=[jax.ShapeDtypeStruct((1, 128), F32), jax.ShapeDtypeStruct((L, D), F32),
                   jax.ShapeDtypeStruct((1, D), F32)],
        compiler_params=_cparams(("arbitrary",)),
    )(h, g.reshape(1, D), target)
    return loss, dh, dg.reshape(D)


_NT = (((1,), (1,)), ((), ()))
_TN = (((0,), (0,)), ((), ()))


def _attn_fwd(q, k, v, kbias, *, T, window=False, sink=None, side=None, name):
    sn = 0 if side is None else side.n
    H, L, dk = q.shape
    Hkv = k.shape[0]
    dvx = v.shape[2]
    dv = dvx // 2
    G = H // Hkv
    nt = L // T
    Hb = kbias.shape[0]
    reps = T // LANES
    assert not window or T == WINDOW
    HP = G if G > 1 else ATT_HP
    NS = 1 if G > 1 else HP
    R = HP * T // NS
    HKV = HP // G
    HB = HP if Hb > 1 else 1
    assert G == 1 or Hb == 1

    def body(*refs):
        q_ref, k_ref, v_ref, kb_ref = refs[:4]
        n = 4
        if sink is not None:
            sk_ref = refs[n]
            n += 1
        side_in = refs[n:n + sn]
        n += sn
        o_ref, lse_ref = refs[n:n + 2]
        side_out = refs[n + 2:n + 2 + sn]
        n += 2 + sn
        m_scs, acc_scs, buf_a, buf_b = (refs[n + t * NS:n + (t + 1) * NS] for t in range(4))
        i = pl.program_id(1)
        if side is not None:
            _ride(side, [*side_in, *side_out, *refs[n + 4 * NS:]], pl.program_id(0) * nt + i, (H // HP) * nt)
        for a in range(NS):
            if sink is not None:
                sk = [jnp.broadcast_to(sk_ref[b, :, 0:1], (T, LANES)) for b in range(HP)]
                m_scs[a][...] = jnp.concatenate(sk, axis=0) if G > 1 else sk[a]
                lane = lax.broadcasted_iota(jnp.int32, (R, dvx), 1)
                acc_scs[a][...] = jnp.where(lane >= dv, 1.0, 0.0)
            else:
                m_scs[a][...] = jnp.full((R, LANES), NEG, F32)
                acc_scs[a][...] = jnp.zeros((R, dvx), F32)
        row = lax.broadcasted_iota(jnp.int32, (R, T), 0) & (T - 1) if G > 1 else \
            lax.broadcasted_iota(jnp.int32, (R, T), 0)
        col = lax.broadcasted_iota(jnp.int32, (R, T), 1)

        def logits(a, j):
            rows = pl.ds(pl.multiple_of(j * T, T), T)
            qv = q_ref[...].reshape(R, dk) if G > 1 else q_ref[a]
            s = lax.dot_general(qv, k_ref[a, rows, :], _NT, preferred_element_type=F32)
            return s - kb_ref[a if HB > 1 else 0, j]

        def update(a, s, j, kind):
            rows = pl.ds(pl.multiple_of(j * T, T), T)
            m_sc, acc_sc = m_scs[a], acc_scs[a]
            if kind == "diag":
                s = jnp.where(row >= col, s, NEG)
            elif kind == "prev":
                s = jnp.where((col > row) & (i > 0), s, NEG)
            m_prev = m_sc[...]
            m_new = jnp.maximum(m_prev, jnp.max(s, axis=1, keepdims=True))
            alpha = jnp.exp(m_prev - m_new)
            p = jnp.exp(s - jnp.tile(m_new, (1, reps)))
            acc_sc[...] = alpha[:, :dvx] * acc_sc[...] + jnp.dot(p.astype(BF16), v_ref[a, rows, :],
                                                                 preferred_element_type=F32)
            m_sc[...] = m_new

        if window:
            before = jnp.maximum(i - 1, 0)
            s_prev = [logits(a, before) for a in range(NS)]
            s_diag = [logits(a, i) for a in range(NS)]
            for a in range(NS):
                update(a, s_prev[a], before, "prev")
                update(a, s_diag[a], i, "diag")
        else:
            def fill(buf, j):
                for a in range(NS):
                    buf[a][...] = logits(a, j)

            def drain(buf, j, kind):
                for a in range(NS):
                    update(a, buf[a][...], j, kind)

            fill(buf_a, 0)

            def pair(t, c):
                fill(buf_b, 2 * t + 1)
                drain(buf_a, 2 * t, "full")
                fill(buf_a, 2 * t + 2)
                drain(buf_b, 2 * t + 1, "full")
                return c

            lax.fori_loop(0, i // 2, pair, 0)

            @pl.when(i % 2 == 1)
            def _():
                fill(buf_b, i)
                drain(buf_a, i - 1, "full")
                drain(buf_b, i, "diag")

            @pl.when(i % 2 == 0)
            def _():
                drain(buf_a, i, "diag")

        for a in range(NS):
            acc = acc_scs[a][...]
            ov = acc[:, :dv] / acc[:, dv:]
            lsev = m_scs[a][:, 0:1] + jnp.log(acc[:, dv:dv + 1])
            if G > 1:
                o_ref[...] = ov.reshape(HP, T, dv)
                lse_ref[...] = lsev.reshape(HP, T, 1)
            else:
                o_ref[a] = ov
                lse_ref[a] = lsev

    in_specs = [pl.BlockSpec((HP, T, dk), lambda h, i: (h, i, 0)),
                pl.BlockSpec((HKV, L, dk), lambda h, i: (h, 0, 0)),
                pl.BlockSpec((HKV, L, dvx), lambda h, i: (h, 0, 0)),
                pl.BlockSpec((HB, nt, 1, T), lambda h, i: (h if Hb > 1 else 0, 0, 0, 0))]
    args = [q, k, v, kbias]
    if sink is not None:
        in_specs += [pl.BlockSpec((HP, 1, LANES), lambda h, i: (h, 0, 0))]
        args += [sink]
    any_spec = pl.BlockSpec(memory_space=pl.ANY)
    return pl.pallas_call(
        body, name=name, grid=(H // HP, nt),
        in_specs=in_specs + [any_spec] * sn,
        out_specs=[pl.BlockSpec((HP, T, dv), lambda h, i: (h, i, 0)),
                   pl.BlockSpec((HP, T, 1), lambda h, i: (h, i, 0))] + [any_spec] * sn,
        out_shape=[jax.ShapeDtypeStruct((H, L, dv), F32), jax.ShapeDtypeStruct((H, L, 1), F32)]
        + ([] if side is None else side.out_shape),
        scratch_shapes=[pltpu.VMEM((R, LANES), F32)] * NS + [pltpu.VMEM((R, dvx), F32)] * NS
        + [pltpu.VMEM((R, T), F32)] * (2 * NS) + ([] if side is None else side.scratch),
        compiler_params=_side_cparams(side, ("parallel", "arbitrary")),
    )(*args, *([] if side is None else side.arrs))


def _attn_bwd_t(q, k, v, kbias_col, lse_row, delta_row, do, *, T, fox=False, side=None, name):
    sn = 0 if side is None else side.n
    H, L, dk = q.shape
    dv = do.shape[2]
    nt = L // T
    Hb = kbias_col.shape[0]

    def body(*refs):
        q_ref, k_ref, v_ref, kb_ref, lse_ref, dl_ref, do_ref = refs[:7]
        side_in = refs[7:7 + sn]
        n = 7 + sn
        dq_ref, dk_ref, dv_ref = refs[n:n + 3]
        n += 3
        if fox:
            dcq_ref, dck_ref = refs[n:n + 2]
            n += 2
        side_out = refs[n:n + sn]
        n += sn
        buf_a, buf_b = refs[n:n + 2], refs[n + 2:n + 4]
        j = pl.program_id(1)
        if side is not None:
            _ride(side, [*side_in, *side_out, *refs[n + 4:]], pl.program_id(0) * nt + j, H * nt)

        @pl.when(j == 0)
        def _():
            dq_ref[...] = jnp.zeros_like(dq_ref)
            if fox:
                dcq_ref[...] = jnp.zeros_like(dcq_ref)

        dk_ref[...] = jnp.zeros_like(dk_ref)
        dv_ref[...] = jnp.zeros_like(dv_ref)
        if fox:
            dck_ref[...] = jnp.zeros_like(dck_ref)
        kb = k_ref[0]
        vb = v_ref[0][:, :dv]
        kbias_j = kb_ref[0]
        key = lax.broadcasted_iota(jnp.int32, (T, T), 0)
        qry = lax.broadcasted_iota(jnp.int32, (T, T), 1)

        def fill(buf, i):
            rows = pl.ds(pl.multiple_of(i * T, T), T)
            buf[0][...] = lax.dot_general(kb, q_ref[0, rows, :], _NT, preferred_element_type=F32) - kbias_j
            buf[1][...] = lax.dot_general(vb, do_ref[0, rows, :], _NT, preferred_element_type=F32)

        def drain(buf, i, kind):
            rows = pl.ds(pl.multiple_of(i * T, T), T)
            st = buf[0][...]
            if kind == "diag":
                st = jnp.where(key <= qry, st, NEG)
            pt = jnp.exp(st - lse_ref[0, i])
            dst = pt * (buf[1][...] - dl_ref[0, i])
            dsb = dst.astype(BF16)
            dv_ref[0] += jnp.dot(pt.astype(BF16), do_ref[0, rows, :], preferred_element_type=F32)
            dk_ref[0] += jnp.dot(dsb, q_ref[0, rows, :], preferred_element_type=F32)
            dq_ref[0, rows, :] += lax.dot_general(dsb, kb, _TN, preferred_element_type=F32)
            if fox:
                dcq_ref[0, i] += jnp.sum(dst, axis=0, keepdims=True)
                dck_ref[0] += -jnp.sum(dst, axis=1, keepdims=True)

        first = j + 1
        rest = nt - first
        fill(buf_a, j)
        fill(buf_b, jnp.minimum(first, nt - 1))
        drain(buf_a, j, "diag")

        def pair(t, c):
            i0 = first + 2 * t
            fill(buf_a, i0 + 1)
            drain(buf_b, i0, "full")
            fill(buf_b, jnp.minimum(i0 + 2, nt - 1))
            drain(buf_a, i0 + 1, "full")
            return c

        lax.fori_loop(0, rest // 2, pair, 0)

        @pl.when(rest % 2 == 1)
        def _():
            drain(buf_b, nt - 1, "full")

    rows_spec = pl.BlockSpec((1, nt, 1, T), lambda h, j: (h, 0, 0, 0))
    in_specs = [pl.BlockSpec((1, L, dk), lambda h, j: (h, 0, 0)),
                pl.BlockSpec((1, T, dk), lambda h, j: (h, j, 0)),
                pl.BlockSpec((1, T, v.shape[2]), lambda h, j: (h, j, 0)),
                pl.BlockSpec((1, T, 1), lambda h, j: (h if Hb > 1 else 0, j, 0)),
                rows_spec, rows_spec,
                pl.BlockSpec((1, L, dv), lambda h, j: (h, 0, 0))]
    out_specs = [pl.BlockSpec((1, L, dk), lambda h, j: (h, 0, 0)),
                 pl.BlockSpec((1, T, dk), lambda h, j: (h, j, 0)),
                 pl.BlockSpec((1, T, dv), lambda h, j: (h, j, 0))]
    out_shape = [jax.ShapeDtypeStruct((H, L, dk), F32), jax.ShapeDtypeStruct((H, L, dk), F32),
                 jax.ShapeDtypeStruct((H, L, dv), F32)]
    if fox:
        out_specs += [rows_spec, pl.BlockSpec((1, T, 1), lambda h, j: (h, j, 0))]
        out_shape += [jax.ShapeDtypeStruct((H, nt, 1, T), F32), jax.ShapeDtypeStruct((H, L, 1), F32)]
    any_spec = pl.BlockSpec(memory_space=pl.ANY)
    outs = pl.pallas_call(
        body, name=name, grid=(H, nt),
        in_specs=in_specs + [any_spec] * sn, out_specs=out_specs + [any_spec] * sn,
        out_shape=out_shape + ([] if side is None else side.out_shape),
        scratch_shapes=[pltpu.VMEM((T, T), F32)] * 4 + ([] if side is None else side.scratch),
        compiler_params=_side_cparams(side, ("parallel", "arbitrary")),
    )(q, k, v, kbias_col, lse_row, delta_row, do, *([] if side is None else side.arrs))
    main, rest = outs[:len(outs) - sn], outs[len(outs) - sn:]
    return (*(main if fox else (*main, None, None)), *rest)


def _attn_bwd_window(q, k, v, kbias, o, lse, do, *, name):
    H, L, dk = q.shape
    Hkv = k.shape[0]
    dv = do.shape[2]
    G = H // Hkv
    T = WINDOW
    nt = L // T
    R = G * T

    def body(q_ref, kc_ref, kp_ref, vc_ref, vp_ref, kb_ref, o_ref, lse_ref, do_ref, dq_ref, dk_ref, dv_ref,
             dk_sc, dv_sc):
        i = pl.program_id(1)
        row = lax.broadcasted_iota(jnp.int32, (R, T), 0) & (T - 1)
        col = lax.broadcasted_iota(jnp.int32, (R, T), 1)

        @pl.when(i == 0)
        def _():
            dk_sc[...] = jnp.zeros_like(dk_sc)
            dv_sc[...] = jnp.zeros_like(dv_sc)

        @pl.when(i == nt)
        def _():
            dk_ref[0] = dk_sc[...]
            dv_ref[0] = dv_sc[...]

        @pl.when(i < nt)
        def _():
            qb = q_ref[...].reshape(R, dk)
            dof = do_ref[...].reshape(R, dv)
            dob = dof.astype(BF16)
            lse_c = lse_ref[...].reshape(R, 1)
            delta = jnp.sum(dof * o_ref[...].reshape(R, dv), axis=1, keepdims=True)

            def grads(kt, vt, kbias_j, mask):
                s = lax.dot_general(qb, kt, _NT, preferred_element_type=F32) - kbias_j
                p = jnp.exp(jnp.where(mask, s, NEG) - lse_c)
                dp = lax.dot_general(dob, vt, _NT, preferred_element_type=F32)
                ds = (p * (dp - delta)).astype(BF16)
                return (jnp.dot(ds, kt, preferred_element_type=F32),
                        lax.dot_general(ds, qb, _TN, preferred_element_type=F32),
                        lax.dot_general(p.astype(BF16), dob, _TN, preferred_element_type=F32))

            ip = jnp.maximum(i - 1, 0)
            dq_p, dk_p, dv_p = grads(kp_ref[0], vp_ref[0][:, :dv], kb_ref[0, ip], (col > row) & (i > 0))
            dq_c, dk_c, dv_c = grads(kc_ref[0], vc_ref[0][:, :dv], kb_ref[0, i], row >= col)
            dq_ref[...] = (dq_p + dq_c).reshape(G, T, dk)
            dk_ref[0] = dk_sc[...] + dk_p
            dv_ref[0] = dv_sc[...] + dv_p
            dk_sc[...] = dk_c
            dv_sc[...] = dv_c

    def cur(i):
        return jnp.minimum(i, nt - 1)

    def prev(i):
        return jnp.maximum(jnp.minimum(i, nt - 1) - 1, 0)

    def written(i):
        return jnp.maximum(i - 1, 0)

    qs = lambda d: pl.BlockSpec((G, T, d), lambda h, i: (h, cur(i), 0))
    return pl.pallas_call(
        body, name=name, grid=(Hkv, nt + 1),
        in_specs=[qs(dk),
                  pl.BlockSpec((1, T, dk), lambda h, i: (h, cur(i), 0)),
                  pl.BlockSpec((1, T, dk), lambda h, i: (h, prev(i), 0)),
                  pl.BlockSpec((1, T, v.shape[2]), lambda h, i: (h, cur(i), 0)),
                  pl.BlockSpec((1, T, v.shape[2]), lambda h, i: (h, prev(i), 0)),
                  pl.BlockSpec((1, nt, 1, T), lambda h, i: (0, 0, 0, 0)),
                  qs(dv), qs(1), qs(dv)],
        out_specs=[qs(dk),
                   pl.BlockSpec((1, T, dk), lambda h, i: (h, written(i), 0)),
                   pl.BlockSpec((1, T, dv), lambda h, i: (h, written(i), 0))],
        out_shape=[jax.ShapeDtypeStruct((H, L, dk), F32), jax.ShapeDtypeStruct((Hkv, L, dk), F32),
                   jax.ShapeDtypeStruct((Hkv, L, dv), F32)],
        scratch_shapes=[pltpu.VMEM((T, dk), F32), pltpu.VMEM((T, dv), F32)],
        compiler_params=_cparams(("parallel", "arbitrary")),
    )(q, k, k, v, v, kbias, o, lse, do)


def _adamw(w, gparts, m, v, *, name):
    n, R, C = gparts.shape
    tr = _pick(R, (128, 64, 32, 16, 8))
    c1 = 1.0 - ADAM_B1 ** ADAM_STEP
    c2 = 1.0 - ADAM_B2 ** ADAM_STEP

    def body(w_ref, g_ref, m_ref, v_ref, go_ref, d_ref, mo_ref, vo_ref):
        g = g_ref[0].astype(F32)
        for t in range(1, n):
            g = g + g_ref[t].astype(F32)
        mn = ADAM_B1 * m_ref[...] + (1.0 - ADAM_B1) * g
        vn = ADAM_B2 * v_ref[...] + (1.0 - ADAM_B2) * (g * g)
        go_ref[...] = g
        mo_ref[...] = mn
        vo_ref[...] = vn
        d_ref[...] = -ADAM_LR * ((mn / c1) / (jnp.sqrt(vn / c2) + ADAM_EPS) + ADAM_WD * w_ref[...])

    spec = pl.BlockSpec((tr, C), lambda i: (i, 0))
    return pl.pallas_call(
        body, name=name, grid=(R // tr,),
        in_specs=[spec, pl.BlockSpec((n, tr, C), lambda i: (0, i, 0)), spec, spec],
        out_specs=[spec] * 4,
        out_shape=[jax.ShapeDtypeStruct((R, C), F32)] * 4,
        compiler_params=_cparams(("parallel",)),
    )(w, gparts, m, v)


def _me():
    return lax.axis_index("x"), lax.axis_index("y"), lax.axis_index("c")


class _CommJob:
    def __init__(self, arrs):
        self.arrs = list(arrs)
        self.n = len(arrs)
        self.scratch = [pltpu.SemaphoreType.DMA((self.n, 7)), pltpu.SemaphoreType.DMA((self.n, 7)),
                        pltpu.SemaphoreType.DMA((self.n,))]

    def bind(self, refs):
        n = self.n
        self.ins, self.outs = refs[:n], refs[n:2 * n]
        self.send_sems, self.recv_sems, self.local_sems = refs[2 * n:2 * n + 3]

    def middle(self):
        pass


class _Gather(_CommJob):
    def __init__(self, arrs):
        super().__init__(arrs)
        self.out_shape = [jax.ShapeDtypeStruct((N_DEV, *a.shape), a.dtype) for a in arrs]

    def _where(self):
        x, y, c = _me()
        return (x, y, c), (x, y, 1 - c), [(1 - x, y), (x, 1 - y), (1 - x, 1 - y)], c

    def _copy(self, t, k, block, to, src=None):
        dst = self.outs[t].at[4 * block[0] + 2 * block[1] + block[2]]
        return pltpu.make_async_remote_copy(
            src_ref=dst if src is None else src, dst_ref=dst,
            send_sem=self.send_sems.at[t, k], recv_sem=self.recv_sems.at[t, k],
            device_id=to, device_id_type=MESH)

    def _mine(self, t, me):
        return pltpu.make_async_copy(self.ins[t], self.outs[t].at[4 * me[0] + 2 * me[1] + me[2]],
                                     self.local_sems.at[t])

    def _first(self, t, me, sibling, chips, c):
        return [self._copy(t, 0, me, sibling, src=self.ins[t])] + \
               [self._copy(t, 1 + j, me, (*chip, c), src=self.ins[t]) for j, chip in enumerate(chips)]

    def start(self):
        me, sibling, chips, c = self._where()
        for t in range(self.n):
            self._mine(t, me).start()
            for cp in self._first(t, me, sibling, chips, c):
                cp.start()

    def middle(self):
        me, sibling, chips, c = self._where()
        for j, chip in enumerate(chips):
            for t in range(self.n):
                self._copy(t, 1 + j, (*chip, c), me).wait_recv()
                self._copy(t, 4 + j, (*chip, c), sibling).start()

    def finish(self):
        me, sibling, chips, c = self._where()
        for t in range(self.n):
            self._copy(t, 0, sibling, me).wait_recv()
            for j, chip in enumerate(chips):
                self._copy(t, 4 + j, (*chip, 1 - c), me).wait_recv()
        for t in range(self.n):
            for cp in self._first(t, me, sibling, chips, c):
                cp.wait_send()
            for j, chip in enumerate(chips):
                self._copy(t, 4 + j, (*chip, c), sibling).wait_send()
            self._mine(t, me).wait()


class _Exchange(_CommJob):
    def __init__(self, arrs):
        super().__init__(arrs)
        self.out_shape = [jax.ShapeDtypeStruct(a.shape, a.dtype) for a in arrs]

    def _copies(self, t):
        x, y, c = _me()
        my_idx = 4 * x + 2 * y + c
        pairs = []
        for k in range(1, N_DEV):
            peer = (x ^ ((k >> 2) & 1), y ^ ((k >> 1) & 1), c ^ (k & 1))
            peer_idx = 4 * peer[0] + 2 * peer[1] + peer[2]
            sems = dict(send_sem=self.send_sems.at[t, k - 1], recv_sem=self.recv_sems.at[t, k - 1],
                        device_id=peer, device_id_type=MESH)
            pairs.append((pltpu.make_async_remote_copy(src_ref=self.ins[t].at[peer_idx],
                                                       dst_ref=self.outs[t].at[my_idx], **sems),
                          pltpu.make_async_remote_copy(src_ref=self.ins[t].at[peer_idx],
                                                       dst_ref=self.outs[t].at[peer_idx], **sems)))
        return pairs

    def _mine(self, t):
        x, y, c = _me()
        my_idx = 4 * x + 2 * y + c
        return pltpu.make_async_copy(self.ins[t].at[my_idx], self.outs[t].at[my_idx], self.local_sems.at[t])

    def start(self):
        for t in range(self.n):
            self._mine(t).start()
            for snd, _ in self._copies(t):
                snd.start()

    def finish(self):
        for t in range(self.n):
            pairs = self._copies(t)
            for _, rcv in pairs:
                rcv.wait_recv()
            for snd, _ in pairs:
                snd.wait_send()
            self._mine(t).wait()


def _comm(job, *, name):
    def body(*refs):
        job.bind(refs)
        job.start()
        job.middle()
        job.finish()

    any_spec = pl.BlockSpec(memory_space=pl.ANY)
    return pl.pallas_call(
        body, name=name,
        in_specs=[any_spec] * job.n, out_specs=[any_spec] * job.n,
        out_shape=job.out_shape, scratch_shapes=job.scratch,
        compiler_params=pltpu.CompilerParams(has_side_effects=True),
    )(*job.arrs)


def _ride(job, refs, step, total):
    job.bind(refs)

    @pl.when(step == 0)
    def _():
        job.start()

    @pl.when(step == (total * 3) // 5)
    def _():
        job.middle()

    @pl.when(step == total - 1)
    def _():
        job.finish()


SCALE_A = DH ** -0.5
SCALE_B = (MLA_NOPE + MLA_ROPE) ** -0.5
SCALE_C = DH ** -0.5
DQK_B = MLA_NOPE + MLA_ROPE


def _rope_tables(L, half, width):
    pos = (jnp.arange(L) - PAD).astype(F32)
    lane = jnp.arange(width)
    inv = ROPE_THETA ** (-(lane % half).astype(F32) / half)
    ang = pos[:, None] * inv[None, :]
    sign = jnp.where(lane % (2 * half) < half, -1.0, 1.0).astype(F32)
    return jnp.cos(ang), jnp.sin(ang) * sign[None, :]


def _rope_lanes(x, cos, sin, half):
    W = x.shape[1]
    lane = lax.broadcasted_iota(jnp.int32, x.shape, 1)
    first = (lane & (2 * half - 1)) < half
    partner = jnp.where(first, pltpu.roll(x, W - half, 1), pltpu.roll(x, half, 1))
    return x * cos + partner * sin


def _tile_lanes(t, width):
    return t if t.shape[1] == width else jnp.tile(t, (1, width // t.shape[1]))


def _split(x, H, d, dst):
    for h in range(H):
        dst[h] = x[:, d * h:d * (h + 1)].astype(dst.dtype)


def _join(src, H):
    return jnp.concatenate([src[h] for h in range(H)], axis=1)


def _head_spec(H, tm, d):
    return pl.BlockSpec((H, tm, d), lambda i: (0, i, 0))


def _split_values(x, H, dst):
    ones = jnp.ones((x.shape[0], DH), BF16)
    for h in range(H):
        dst[h] = jnp.concatenate([x[:, DH * h:DH * (h + 1)].astype(BF16), ones], axis=1)


def _prep_a(proj, *, name):
    L = proj.shape[0]
    tm = ROW_T

    def body(x_ref, qo, ko, vo):
        _split(x_ref[:, 0:512] * SCALE_A, HEADS, DH, qo)
        _split(x_ref[:, 512:1024], HEADS, DH, ko)
        _split_values(x_ref[:, 1024:1536], HEADS, vo)

    return pl.pallas_call(
        body, name=name, grid=(L // tm,),
        in_specs=[_col_spec(tm, 1536, QKVA)],
        out_specs=[_head_spec(HEADS, tm, DH)] * 2 + [_head_spec(HEADS, tm, 2 * DH)],
        out_shape=[jax.ShapeDtypeStruct((HEADS, L, DH), BF16)] * 2 + [jax.ShapeDtypeStruct((HEADS, L, 2 * DH), BF16)],
        compiler_params=_cparams(("parallel",)),
    )(proj)


def _unprep_a(dq, dk, dv, dproj, *, name):
    L = dq.shape[1]
    tm = ROW_T

    def body(dq_ref, dk_ref, dv_ref, _, dp_ref):
        dp_ref[:, 0:512] = (_join(dq_ref, HEADS) * SCALE_A).astype(BF16)
        dp_ref[:, 512:1024] = _join(dk_ref, HEADS).astype(BF16)
        dp_ref[:, 1024:1536] = _join(dv_ref, HEADS).astype(BF16)

    hs = _head_spec(HEADS, tm, DH)
    return pl.pallas_call(
        body, name=name, grid=(L // tm,),
        in_specs=[hs, hs, hs, pl.BlockSpec(memory_space=pl.ANY)],
        out_specs=_col_spec(tm, 1536, QKVA),
        out_shape=jax.ShapeDtypeStruct(dproj.shape, dproj.dtype),
        input_output_aliases={3: 0},
        compiler_params=_cparams(("parallel",)),
    )(dq, dk, dv, dproj)


def _prep_c(proj, tab, *, name):
    L = proj.shape[0]
    tm = ROW_T

    def body(x_ref, cos_ref, sin_ref, qo, ko, vo):
        cos, sin = cos_ref[...], sin_ref[...]
        q = _rope_lanes(x_ref[:, 0:512], _tile_lanes(cos, 512), _tile_lanes(sin, 512), DH // 2)
        _split(q * SCALE_C, HEADS, DH, qo)
        _split(_rope_lanes(x_ref[:, 512:640], cos, sin, DH // 2), SWA_KV_HEADS, DH, ko)
        _split_values(x_ref[:, 640:768], SWA_KV_HEADS, vo)

    t128 = pl.BlockSpec((tm, LANES), lambda i: (i, 0))
    return pl.pallas_call(
        body, name=name, grid=(L // tm,),
        in_specs=[_col_spec(tm, 768, QKVC), t128, t128],
        out_specs=[_head_spec(HEADS, tm, DH), _head_spec(SWA_KV_HEADS, tm, DH),
                   _head_spec(SWA_KV_HEADS, tm, 2 * DH)],
        out_shape=[jax.ShapeDtypeStruct((HEADS, L, DH), BF16), jax.ShapeDtypeStruct((SWA_KV_HEADS, L, DH), BF16),
                   jax.ShapeDtypeStruct((SWA_KV_HEADS, L, 2 * DH), BF16)],
        compiler_params=_cparams(("parallel",)),
    )(proj, *tab)


def _unprep_c(dq, dk, dv, tab, dproj, *, name):
    L = dq.shape[1]
    tm = ROW_T

    def body(dq_ref, dk_ref, dv_ref, cos_ref, sin_ref, _, dp_ref):
        cos, nsin = cos_ref[...], -sin_ref[...]
        dqv = _join(dq_ref, HEADS) * SCALE_C
        dp_ref[:, 0:512] = _rope_lanes(dqv, _tile_lanes(cos, 512), _tile_lanes(nsin, 512), DH // 2).astype(BF16)
        dp_ref[:, 512:640] = _rope_lanes(_join(dk_ref, SWA_KV_HEADS), cos, nsin, DH // 2).astype(BF16)
        dp_ref[:, 640:768] = _join(dv_ref, SWA_KV_HEADS).astype(BF16)

    hs = _head_spec(HEADS, tm, DH)
    hkv = _head_spec(SWA_KV_HEADS, tm, DH)
    t128 = pl.BlockSpec((tm, LANES), lambda i: (i, 0))
    return pl.pallas_call(
        body, name=name, grid=(L // tm,),
        in_specs=[hs, hkv, hkv, t128, t128, pl.BlockSpec(memory_space=pl.ANY)],
        out_specs=_col_spec(tm, 768, QKVC),
        out_shape=jax.ShapeDtypeStruct(dproj.shape, dproj.dtype),
        input_output_aliases={5: 0},
        compiler_params=_cparams(("parallel",)),
    )(dq, dk, dv, *tab, dproj)


def _q_tables(cos, sin, on_ref):
    on = on_ref[...] > 0.5
    width = on_ref.shape[1]
    return jnp.where(on, _tile_lanes(cos, width), 1.0), jnp.where(on, _tile_lanes(sin, width), 0.0)


def _prep_b(qbm, kvbm, proj, tab_k, q_rotary, *, name):
    L = proj.shape[0]
    tm = ROW_T

    def body(q_ref, kv_ref, kr_ref, ck_ref, sk_ref, on_ref, qo, ko, vo):
        cq, sq = _q_tables(ck_ref[...], sk_ref[...], on_ref)
        q = _rope_lanes(q_ref[...], cq, sq, MLA_ROPE // 2) * SCALE_B
        _split(q, HEADS, DQK_B, qo)
        kr = _rope_lanes(kr_ref[...], ck_ref[...], sk_ref[...], MLA_ROPE // 2)[:, :MLA_ROPE].astype(BF16)
        kv = kv_ref[...]
        ones = jnp.ones((tm, DH), BF16)
        for h in range(HEADS):
            ko[h] = jnp.concatenate([kv[:, 128 * h:128 * h + MLA_NOPE].astype(BF16), kr], axis=1)
            vo[h] = jnp.concatenate([kv[:, 128 * h + MLA_NOPE:128 * (h + 1)].astype(BF16), ones], axis=1)

    t128 = pl.BlockSpec((tm, LANES), lambda i: (i, 0))
    t768 = pl.BlockSpec((tm, 768), lambda i: (i, 0))
    return pl.pallas_call(
        body, name=name, grid=(L // tm,),
        in_specs=[t768, pl.BlockSpec((tm, 1024), lambda i: (i, 0)), _col_spec(tm, 128, KR), t128, t128,
                  pl.BlockSpec((1, 768), lambda i: (0, 0))],
        out_specs=[_head_spec(HEADS, tm, DQK_B), _head_spec(HEADS, tm, DQK_B), _head_spec(HEADS, tm, 2 * DH)],
        out_shape=[jax.ShapeDtypeStruct((HEADS, L, DQK_B), BF16), jax.ShapeDtypeStruct((HEADS, L, DQK_B), BF16),
                   jax.ShapeDtypeStruct((HEADS, L, 2 * DH), BF16)],
        compiler_params=_cparams(("parallel",)),
    )(qbm, kvbm, proj, *tab_k, q_rotary)


def _unprep_b(dq, dk, dv, tab_k, q_rotary, dproj, *, name):
    L = dq.shape[1]
    tm = ROW_T

    def body(dq_ref, dk_ref, dv_ref, ck_ref, sk_ref, on_ref, _, dp_kr, dqo, dkvo):
        cq, sq = _q_tables(ck_ref[...], sk_ref[...], on_ref)
        dqv = _join(dq_ref, HEADS) * SCALE_B
        dqo[...] = _rope_lanes(dqv, cq, -sq, MLA_ROPE // 2).astype(BF16)
        parts = []
        dkr = None
        for h in range(HEADS):
            dkh = dk_ref[h]
            parts += [dkh[:, :MLA_NOPE], dv_ref[h]]
            r = dkh[:, MLA_NOPE:]
            dkr = r if dkr is None else dkr + r
        dkvo[...] = jnp.concatenate(parts, axis=1).astype(BF16)
        dkr = jnp.concatenate([dkr, jnp.zeros((tm, LANES - MLA_ROPE), F32)], axis=1)
        dp_kr[...] = _rope_lanes(dkr, ck_ref[...], -sk_ref[...], MLA_ROPE // 2).astype(BF16)

    t128 = pl.BlockSpec((tm, LANES), lambda i: (i, 0))
    t768 = pl.BlockSpec((tm, 768), lambda i: (i, 0))
    hq = _head_spec(HEADS, tm, DQK_B)
    return pl.pallas_call(
        body, name=name, grid=(L // tm,),
        in_specs=[hq, hq, _head_spec(HEADS, tm, DH), t128, t128, pl.BlockSpec((1, 768), lambda i: (0, 0)),
                  pl.BlockSpec(memory_space=pl.ANY)],
        out_specs=[_col_spec(tm, 128, KR), t768, pl.BlockSpec((tm, 1024), lambda i: (i, 0))],
        out_shape=[jax.ShapeDtypeStruct(dproj.shape, dproj.dtype), jax.ShapeDtypeStruct((L, 768), BF16),
                   jax.ShapeDtypeStruct((L, 1024), BF16)],
        input_output_aliases={6: 0},
        compiler_params=_cparams(("parallel",)),
    )(dq, dk, dv, *tab_k, q_rotary, dproj)


def _gate(y, proj, zcol, *, name):
    L = proj.shape[0]
    tm = ROW_T

    def body(y_ref, z_ref, u_ref):
        z = z_ref[...]
        u_ref[...] = (_join(y_ref, HEADS) * (z * jax.nn.sigmoid(z))).astype(BF16)

    return pl.pallas_call(
        body, name=name, grid=(L // tm,),
        in_specs=[_head_spec(HEADS, tm, DH), _col_spec(tm, 512, zcol)],
        out_specs=pl.BlockSpec((tm, 512), lambda i: (i, 0)),
        out_shape=jax.ShapeDtypeStruct((L, 512), BF16),
        compiler_params=_cparams(("parallel",)),
    )(y, proj)


def _gate_bwd(du, y, proj, zcol, dproj, *, delta_rows=False, name):
    L = proj.shape[0]
    tm = ROW_T
    assert not delta_rows or tm == ATT_T

    def body(du_ref, y_ref, z_ref, _, dz_ref, dy_ref, *dl_ref):
        z = z_ref[...]
        duv = du_ref[...]
        sg = jax.nn.sigmoid(z)
        yv = _join(y_ref, HEADS)
        dyv = duv * (z * sg)
        _split(dyv, HEADS, DH, dy_ref)
        dz_ref[...] = (duv * yv * (sg * (1.0 + z * (1.0 - sg)))).astype(BF16)
        if delta_rows:
            prod = dyv * yv
            ones = jnp.ones((8, DH), F32)
            for h in range(HEADS):
                sums = lax.dot_general(ones, prod[:, DH * h:DH * (h + 1)], _NT, preferred_element_type=F32)
                dl_ref[0][h, 0] = sums[0:1, :]

    hs = _head_spec(HEADS, tm, DH)
    out_specs = [_col_spec(tm, 512, zcol), hs]
    out_shape = [jax.ShapeDtypeStruct(dproj.shape, dproj.dtype),
                 jax.ShapeDtypeStruct((HEADS, L, DH), BF16 if delta_rows else F32)]
    if delta_rows:
        out_specs.append(pl.BlockSpec((HEADS, 1, 1, tm), lambda i: (0, i, 0, 0)))
        out_shape.append(jax.ShapeDtypeStruct((HEADS, L // tm, 1, tm), F32))
    return pl.pallas_call(
        body, name=name, grid=(L // tm,),
        in_specs=[pl.BlockSpec((tm, 512), lambda i: (i, 0)), hs, _col_spec(tm, 512, zcol),
                  pl.BlockSpec(memory_space=pl.ANY)],
        out_specs=out_specs, out_shape=out_shape,
        input_output_aliases={3: 0},
        compiler_params=_cparams(("parallel",)),
    )(du, y, proj, dproj)


MERGE_T = 192


def _branch_merge(us, w_branch, proj, *, name):
    L = proj.shape[0]
    tm = ROW_T

    def body(u0, u1, u2, w_ref, g_ref, m_ref, pb_ref):
        acc = None
        for n, u_ref in enumerate((u0, u1, u2)):
            pb = jnp.dot(u_ref[...], w_ref[n], preferred_element_type=F32)
            pb_ref[n] = pb.astype(BF16)
            t = jax.nn.sigmoid(g_ref[:, n * D_MODEL:(n + 1) * D_MODEL]) * pb
            acc = t if acc is None else acc + t
        m_ref[...] = acc.astype(BF16)

    urow = pl.BlockSpec((tm, BRANCH_W), lambda i: (i, 0))
    return pl.pallas_call(
        body, name=name, grid=(L // tm,),
        in_specs=[urow] * N_BRANCH + [pl.BlockSpec((N_BRANCH, BRANCH_W, D_MODEL), lambda i: (0, 0, 0)),
                                      _col_spec(tm, N_BRANCH * D_MODEL, GATES)],
        out_specs=[pl.BlockSpec((tm, D_MODEL), lambda i: (i, 0)),
                   pl.BlockSpec((N_BRANCH, tm, D_MODEL), lambda i: (0, i, 0))],
        out_shape=[jax.ShapeDtypeStruct((L, D_MODEL), BF16), jax.ShapeDtypeStruct((N_BRANCH, L, D_MODEL), BF16)],
        compiler_params=_cparams(("parallel",)),
    )(*us, w_branch, proj)


def _merge_bwd(dmerged, proj, pb, dproj, *, name):
    L = proj.shape[0]
    tm = MERGE_T

    def body(dm_ref, g_ref, p_ref, _, dg_ref, dp0, dp1, dp2):
        dm = dm_ref[...]
        for n, dp_ref in enumerate((dp0, dp1, dp2)):
            cols = slice(n * D_MODEL, (n + 1) * D_MODEL)
            sg = jax.nn.sigmoid(g_ref[:, cols])
            dp_ref[...] = (dm * sg).astype(BF16)
            dg_ref[:, cols] = (dm * p_ref[n].astype(F32) * (sg * (1.0 - sg))).astype(BF16)

    row = pl.BlockSpec((tm, D_MODEL), lambda i: (i, 0))
    gates = _col_spec(tm, N_BRANCH * D_MODEL, GATES)
    outs = pl.pallas_call(
        body, name=name, grid=(L // tm,),
        in_specs=[row, gates, pl.BlockSpec((N_BRANCH, tm, D_MODEL), lambda i: (0, i, 0)),
                  pl.BlockSpec(memory_space=pl.ANY)],
        out_specs=[gates] + [row] * N_BRANCH,
        out_shape=[jax.ShapeDtypeStruct(dproj.shape, dproj.dtype)]
        + [jax.ShapeDtypeStruct((L, D_MODEL), BF16)] * N_BRANCH,
        input_output_aliases={3: 0},
        compiler_params=_cparams(("parallel",)),
    )(dmerged, proj, pb, dproj)
    return outs[0], outs[1:]


def _scan_rows(x, reverse):
    rows = lax.broadcasted_iota(jnp.int32, x.shape, 0)
    step = 1
    while step < BLK:
        if reverse:
            x = x + jnp.where(rows < BLK - step, pltpu.roll(x, BLK - step, 0), 0.0)
        else:
            x = x + jnp.where(rows >= step, pltpu.roll(x, step, 0), 0.0)
        step *= 2
    return x


def _forget_fwd(proj, b_f, *, name):
    L = proj.shape[0]
    nb = L // BLK

    def body(x_ref, b_ref, o_ref, carry):
        i = pl.program_id(0)

        @pl.when(i == 0)
        def _():
            carry[...] = jnp.zeros_like(carry)

        c = _scan_rows(jax.nn.log_sigmoid(x_ref[...] + b_ref[...]), False) + carry[...]
        carry[...] = c[BLK - 1:BLK, :]
        pos = i * BLK + lax.broadcasted_iota(jnp.int32, (HEADS, BLK), 1)
        o_ref[...] = c.T[:HEADS, :] + jnp.where(pos < PAD, BIG, 0.0)

    return pl.pallas_call(
        body, name=name, grid=(nb,),
        in_specs=[_col_spec(BLK, LANES, FA), pl.BlockSpec((1, LANES), lambda i: (0, 0))],
        out_specs=pl.BlockSpec((HEADS, BLK), lambda i: (0, i)),
        out_shape=jax.ShapeDtypeStruct((HEADS, L), F32),
        scratch_shapes=[pltpu.VMEM((1, LANES), F32)],
        compiler_params=_cparams(("arbitrary",)),
    )(proj, jnp.pad(b_f, (0, LANES - HEADS)).reshape(1, LANES))


def _forget_bwd(dct, proj, b_f, dproj, *, name):
    L = proj.shape[0]
    nb = L // BLK

    def body(d_ref, x_ref, b_ref, _, dp_ref, db_ref, carry):
        i = pl.program_id(0)

        @pl.when(i == 0)
        def _():
            carry[...] = jnp.zeros_like(carry)
            db_ref[...] = jnp.zeros_like(db_ref)

        d = jnp.concatenate([d_ref[...], jnp.zeros((BLK - HEADS, BLK), F32)], axis=0).T
        dlog = _scan_rows(d, True) + carry[...]
        carry[...] = dlog[0:1, :]
        lane = lax.broadcasted_iota(jnp.int32, (BLK, LANES), 1)
        daf = jnp.where(lane < HEADS, dlog * jax.nn.sigmoid(-(x_ref[...] + b_ref[...])), 0.0)
        db_ref[...] += jnp.sum(daf, axis=0, keepdims=True)
        dp_ref[...] = jnp.concatenate([daf, jnp.zeros((BLK, LANES), F32)], axis=1).astype(BF16)

    back = lambda i: nb - 1 - i
    dp, db = pl.pallas_call(
        body, name=name, grid=(nb,),
        in_specs=[pl.BlockSpec((HEADS, BLK), lambda i: (0, back(i))),
                  pl.BlockSpec((BLK, LANES), lambda i: (back(i), FA // LANES)),
                  pl.BlockSpec((1, LANES), lambda i: (0, 0)), pl.BlockSpec(memory_space=pl.ANY)],
        out_specs=[pl.BlockSpec((BLK, 2 * LANES), lambda i: (back(i), FA // (2 * LANES))),
                   pl.BlockSpec((1, LANES), lambda i: (0, 0))],
        out_shape=[jax.ShapeDtypeStruct(dproj.shape, dproj.dtype), jax.ShapeDtypeStruct((1, LANES), F32)],
        scratch_shapes=[pltpu.VMEM((1, LANES), F32)],
        input_output_aliases={3: 0},
        compiler_params=_cparams(("arbitrary",)),
    )(dct, proj, jnp.pad(b_f, (0, LANES - HEADS)).reshape(1, LANES), dproj)
    return dp, db[0, :HEADS]


def _key_bias(L, T, ct=None):
    padb = jnp.where(jnp.arange(L) < PAD, BIG, 0.0).astype(F32)[None]
    kb = padb if ct is None else ct + padb
    return kb.reshape(kb.shape[0], L // T, 1, T)


def _layer_fwd(h, w, tabs, l, side=None):
    tag = f"l{l}"
    L = h.shape[0]
    hn = _rms_fwd(h, w["norm_g"], name=f"{tag}_rms_in")
    if "late" in w:
        late_job, assemble = w["late"]
        proj, *late = _mm(hn, w["w_in"], side=late_job, name=f"{tag}_mm_in")
        w = {**{key: val for key, val in w.items() if key != "late"}, **assemble(*late)}
    else:
        proj = _mm(hn, w["w_in"], name=f"{tag}_mm_in")
    kb_a = _forget_fwd(proj, w["b_f"], name=f"{tag}_forget").reshape(HEADS, L // ATT_T, 1, ATT_T)
    ops_a = (*_prep_a(proj, name=f"{tag}_prep_a"), kb_a)
    ya, lsea, *side_out = _attn_fwd(*ops_a, T=ATT_T, side=side, name=f"{tag}_attn_a")
    cqn = _rms_fwd(proj, w["g_cq"], col=CQ, name=f"{tag}_rms_cq")
    ckvn = _rms_fwd(proj, w["g_ckv"], col=CKV, name=f"{tag}_rms_ckv")
    qbm = _mm(cqn, w["w_uq"], name=f"{tag}_mm_uq")
    kvbm = _mm(ckvn, w["w_ukv"], name=f"{tag}_mm_ukv")
    ops_b = (*_prep_b(qbm, kvbm, proj, tabs["bk"], tabs["bq"], name=f"{tag}_prep_b"), _key_bias(L, ATT_T))
    yb, lseb = _attn_fwd(*ops_b, T=ATT_T, name=f"{tag}_attn_b")
    ops_c = (*_prep_c(proj, tabs["c"], name=f"{tag}_prep_c"), _key_bias(L, WINDOW))
    sink = jnp.broadcast_to(w["sinks"][:, None, None], (HEADS, 1, LANES))
    yc, lsec = _attn_fwd(*ops_c, T=WINDOW, window=True, sink=sink, name=f"{tag}_attn_c")
    us = [_gate(y, proj, zcol, name=f"{tag}_gate{n}") for n, (y, zcol) in enumerate(((ya, ZA), (yb, ZB), (yc, ZC)))]
    merged, pbr = _branch_merge(us, w["w_branch"], proj, name=f"{tag}_branch_merge")
    out = _mm(merged, w["w_out"], add=h, name=f"{tag}_mm_out")
    saved = dict(h=h, hn=hn, proj=proj, ops_a=ops_a, ya=ya, lsea=lsea, cqn=cqn, ckvn=ckvn,
                 ops_b=ops_b, yb=yb, lseb=lseb, ops_c=ops_c, yc=yc, lsec=lsec, us=us, pbr=pbr, merged=merged)
    return out, saved, side_out, w


def _w_in_chunks(g, tag):
    return _w_in_to_shards(g["w_in"], name=f"{tag}_w_in_chunks")


def _grad_chunks(g, tag, with_w_in=True):
    br = _cut_shards(g["w_branch"].reshape(N_BRANCH * BRANCH_W, D_MODEL), name=f"{tag}_w_branch_chunks")
    rest = [_cut_shards(g["w_uq"], name=f"{tag}_w_uq_chunks"), _cut_shards(g["w_ukv"], name=f"{tag}_w_ukv_chunks"),
            br.reshape(N_DEV, N_BRANCH, BRANCH_W, D_MODEL // N_DEV),
            g["w_out"].reshape(N_DEV, D_MODEL // N_DEV, D_MODEL)]
    return ([_w_in_chunks(g, tag)] if with_w_in else []) + rest


def _layer_bwd(dout, s, w, tabs, l, side=None, own_exchange=False):
    tag = f"l{l}"
    L = dout.shape[0]
    proj = s["proj"]
    g = {}
    dproj = jnp.zeros((L, NP_IN), BF16)
    dmerged = _mm(dout, w["w_out"], tb=True, name=f"{tag}_mm_out_dx")
    g["w_out"] = _mm(s["merged"], dout, ta=True, out_dtype=BF16, name=f"{tag}_mm_out_dw")
    dproj, dpbr = _merge_bwd(dmerged, proj, s["pbr"], dproj, name=f"{tag}_merge_bwd")
    dus = [_mm(dpbr[n], w["w_branch"][n], tb=True, name=f"{tag}_mm_br{n}_dx") for n in range(N_BRANCH)]
    g["w_branch"] = jnp.stack([_mm(s["us"][n], dpbr[n], ta=True, out_dtype=BF16, name=f"{tag}_mm_br{n}_dw")
                               for n in range(N_BRANCH)])
    dproj, dya, dla = _gate_bwd(dus[0], s["ya"], proj, ZA, dproj, delta_rows=True, name=f"{tag}_gate0_bwd")
    dproj, dyb, dlb = _gate_bwd(dus[1], s["yb"], proj, ZB, dproj, delta_rows=True, name=f"{tag}_gate1_bwd")
    dproj, dyc = _gate_bwd(dus[2], s["yc"], proj, ZC, dproj, name=f"{tag}_gate2_bwd")

    def bwd_operands(ops, lse, delta, dy16):
        q16, k16, v16, kbias = ops
        return (q16, k16, v16, kbias.reshape(kbias.shape[0], L, 1), lse.reshape(HEADS, L // ATT_T, 1, ATT_T), delta,
                dy16)

    dqa, dka, dva, dcq, dck, *side_out = _attn_bwd_t(*bwd_operands(s["ops_a"], s["lsea"], dla, dya), T=ATT_T,
                                                     fox=True, side=side, name=f"{tag}_attn_a_bwd")
    dproj = _unprep_a(dqa, dka, dva, dproj, name=f"{tag}_unprep_a")
    dproj, g["b_f"] = _forget_bwd(dcq.reshape(HEADS, L) + dck[:, :, 0], proj, w["b_f"], dproj,
                                  name=f"{tag}_forget_bwd")
    dqb, dkb, dvb, _, _ = _attn_bwd_t(*bwd_operands(s["ops_b"], s["lseb"], dlb, dyb), T=ATT_T,
                                      name=f"{tag}_attn_b_bwd")
    dproj, dqbm, dkvbm = _unprep_b(dqb, dkb, dvb, tabs["bk"], tabs["bq"], dproj, name=f"{tag}_unprep_b")
    dcqn = _mm(dqbm, w["w_uq"], tb=True, name=f"{tag}_mm_uq_dx")
    g["w_uq"] = _mm(s["cqn"], dqbm, ta=True, out_dtype=BF16, name=f"{tag}_mm_uq_dw")
    dckvn = _mm(dkvbm, w["w_ukv"], tb=True, name=f"{tag}_mm_ukv_dx")
    g["w_ukv"] = _mm(s["ckvn"], dkvbm, ta=True, out_dtype=BF16, name=f"{tag}_mm_ukv_dw")
    dproj, g["g_cq"] = _rms_bwd(dcqn, proj, w["g_cq"], col=CQ, into=dproj, name=f"{tag}_rms_cq_bwd")
    dproj, g["g_ckv"] = _rms_bwd(dckvn, proj, w["g_ckv"], col=CKV, into=dproj, name=f"{tag}_rms_ckv_bwd")
    dqc, dkc, dvc = _attn_bwd_window(*s["ops_c"], s["yc"], s["lsec"], dyc, name=f"{tag}_attn_c_bwd")
    dproj = _unprep_c(dqc, dkc, dvc, tabs["c"], dproj, name=f"{tag}_unprep_c")
    delta_c = jnp.sum(dyc * s["yc"], axis=-1)
    g["sinks"] = -jnp.sum(jnp.exp(w["sinks"][:, None] - s["lsec"][:, :, 0]) * delta_c, axis=1)
    own_out = []
    if own_exchange:
        g["w_in"], *r_rest = _mm(s["hn"].T, dproj, out_dtype=BF16, side=_Exchange(_grad_chunks(g, tag, False)),
                                 name=f"{tag}_mm_in_dw")
        dhn, r_in = _mm(dproj, w["w_in"], tb=True, side=_Exchange([_w_in_chunks(g, tag)]), name=f"{tag}_mm_in_dx")
        own_out = [r_in, *r_rest]
    else:
        g["w_in"] = _mm(s["hn"].T, dproj, out_dtype=BF16, name=f"{tag}_mm_in_dw")
        dhn = _mm(dproj, w["w_in"], tb=True, name=f"{tag}_mm_in_dx")
    dh, g["norm_g"] = _rms_bwd(dhn, s["h"], w["norm_g"], add=dout, name=f"{tag}_rms_in_bwd")
    return dh, g, side_out, own_out


def _cols_from_shards(g):
    return jnp.moveaxis(g, 0, 1).reshape(g.shape[1], N_DEV * g.shape[2])


def _cols_to_shards(w):
    R = w.shape[0]
    return jnp.moveaxis(w.reshape(R, N_DEV, w.shape[1] // N_DEV), 1, 0)


def _join_shards(g, *, name):
    _, R, C = g.shape
    tr = _pick(R, (512, 384, 256))

    def body(g_ref, o_ref):
        for d in range(N_DEV):
            o_ref[:, C * d:C * (d + 1)] = g_ref[d]

    return pl.pallas_call(
        body, name=name, grid=(R // tr,),
        in_specs=[pl.BlockSpec((N_DEV, tr, C), lambda i: (0, i, 0))],
        out_specs=pl.BlockSpec((tr, N_DEV * C), lambda i: (i, 0)),
        out_shape=jax.ShapeDtypeStruct((R, N_DEV * C), g.dtype),
        compiler_params=_cparams(("parallel",)),
    )(g)


def _cut_shards(w, *, name):
    R = w.shape[0]
    C = w.shape[1] // N_DEV
    tr = _pick(R, (512, 384, 256))

    def body(w_ref, o_ref):
        for d in range(N_DEV):
            o_ref[d] = w_ref[:, C * d:C * (d + 1)]

    return pl.pallas_call(
        body, name=name, grid=(R // tr,),
        in_specs=[pl.BlockSpec((tr, N_DEV * C), lambda i: (i, 0))],
        out_specs=pl.BlockSpec((N_DEV, tr, C), lambda i: (0, i, 0)),
        out_shape=jax.ShapeDtypeStruct((N_DEV, R, C), w.dtype),
        compiler_params=_cparams(("parallel",)),
    )(w)


def _pad_in(w):
    parts, at = [], 0
    for lo, hi, dst in sorted(_RUNS, key=lambda r: r[2]):
        if dst > at:
            parts.append(jnp.zeros((w.shape[0], dst - at), w.dtype))
        parts.append(w[:, lo:hi])
        at = dst + hi - lo
    parts.append(jnp.zeros((w.shape[0], NP_IN - at), w.dtype))
    return jnp.concatenate(parts, axis=1)


def _unpad_in(w):
    return jnp.concatenate([w[:, dst:dst + hi - lo] for lo, hi, dst in _RUNS], axis=1)


W_IN_SHARD = N_IN // N_DEV


def _w_in_pieces():
    pieces = []
    for lo, hi, dst in _RUNS:
        for d in range(N_DEV):
            a, b = max(lo, d * W_IN_SHARD), min(hi, (d + 1) * W_IN_SHARD)
            if a < b:
                pieces.append((d, a - d * W_IN_SHARD, b - d * W_IN_SHARD, dst + a - lo))
    return pieces


def _w_in_from_shards(g, *, name):
    R = g.shape[1]
    tr = 256
    covered = sorted((dst, dst + b - a) for _, a, b, dst in _w_in_pieces())

    def body(g_ref, o_ref):
        at = 0
        for lo, hi in covered + [(NP_IN, NP_IN)]:
            if lo > at:
                o_ref[:, at:lo] = jnp.zeros((tr, lo - at), BF16)
            at = max(at, hi)
        for d, a, b, dst in _w_in_pieces():
            o_ref[:, dst:dst + b - a] = g_ref[d, :, a:b]

    return pl.pallas_call(
        body, name=name, grid=(R // tr,),
        in_specs=[pl.BlockSpec((N_DEV, tr, W_IN_SHARD), lambda i: (0, i, 0))],
        out_specs=pl.BlockSpec((tr, NP_IN), lambda i: (i, 0)),
        out_shape=jax.ShapeDtypeStruct((R, NP_IN), BF16),
        compiler_params=_cparams(("parallel",)),
    )(g)


def _w_in_to_shards(dw, *, name):
    R = dw.shape[0]
    tr = 256

    def body(w_ref, o_ref):
        for d, a, b, dst in _w_in_pieces():
            o_ref[d, :, a:b] = w_ref[:, dst:dst + b - a]

    return pl.pallas_call(
        body, name=name, grid=(R // tr,),
        in_specs=[pl.BlockSpec((tr, NP_IN), lambda i: (i, 0))],
        out_specs=pl.BlockSpec((N_DEV, tr, W_IN_SHARD), lambda i: (0, i, 0)),
        out_shape=jax.ShapeDtypeStruct((N_DEV, R, W_IN_SHARD), BF16),
        compiler_params=_cparams(("parallel",)),
    )(dw)


_SMALL = (("norm_g", DEPTH * D_MODEL), ("b_f", DEPTH * HEADS), ("g_cq", DEPTH * MLA_QLORA),
          ("g_ckv", DEPTH * MLA_KVLORA), ("sinks", DEPTH * HEADS), ("final_g", D_MODEL), ("loss", 1),
          ("meta", N_META * D_MODEL))
SMALL_ROWS = 168


def _pack_small(d):
    parts = []
    for name, size in _SMALL:
        padded = -(-size // 128) * 128
        v = d[name].reshape(-1).astype(F32) if name in d else jnp.zeros((size,), F32)
        parts.append(jnp.pad(v, (0, padded - size)))
    flat = jnp.concatenate(parts)
    return jnp.pad(flat, (0, SMALL_ROWS * 128 - flat.shape[0])).reshape(SMALL_ROWS, 128)


def _unpack_small(p, shapes):
    flat = p.reshape(-1)
    out, at = {}, 0
    for name, size in _SMALL:
        if name in shapes:
            out[name] = flat[at:at + size].reshape(shapes[name])
        at += -(-size // 128) * 128
    return out


def kernel(x, meta_tokens, norm_g, w_in, b_f, g_cq, g_ckv, w_uq, w_ukv, sinks, w_branch, w_out, final_g, loss_target, m_meta_tokens, m_norm_g, m_w_in, m_b_f, m_g_cq, m_g_ckv, m_w_uq, m_w_ukv, m_sinks, m_w_branch, m_w_out, m_final_g, v_meta_tokens, v_norm_g, v_w_in, v_b_f, v_g_cq, v_g_ckv, v_w_uq, v_w_ukv, v_sinks, v_w_branch, v_w_out, v_final_g):
    S = x.shape[1]
    L = BLK + S
    cx, cy, cc = _me()
    my_idx = 4 * cx + 2 * cy + cc

    def shards(l):
        return [t[l].astype(BF16) for t in (w_in, w_uq, w_ukv, w_branch, w_out)]

    def small_weights(l):
        return dict(norm_g=norm_g[l], b_f=b_f[l], g_cq=g_cq[l], g_ckv=g_ckv[l], sinks=sinks[l])

    def rest_weights(l):
        def assemble(gw_uq, gw_ukv, gw_br, gw_out):
            br = _join_shards(gw_br.reshape(N_DEV, N_BRANCH * BRANCH_W, D_MODEL // N_DEV), name=f"l{l}_w_branch_full")
            return dict(w_uq=_join_shards(gw_uq, name=f"l{l}_w_uq_full"),
                        w_ukv=_join_shards(gw_ukv, name=f"l{l}_w_ukv_full"),
                        w_branch=br.reshape(N_BRANCH, BRANCH_W, D_MODEL), w_out=gw_out.reshape(D_MODEL, D_MODEL))
        return assemble

    def layer_weights(l, gw_in, *gw_rest):
        return dict(small_weights(l), w_in=_w_in_from_shards(gw_in, name=f"l{l}_w_in_full"),
                    **rest_weights(l)(*gw_rest))

    gw_in0, g_meta = _comm(_Gather([shards(0)[0], meta_tokens]), name="gather_l0")
    layers = [dict(small_weights(0), w_in=_w_in_from_shards(gw_in0, name="l0_w_in_full"),
                   late=(_Gather(shards(0)[1:]), rest_weights(0))), None]
    meta_full = _cols_from_shards(g_meta)

    h = jnp.concatenate([jnp.zeros((PAD, D_MODEL), F32), meta_full, x[0]], axis=0)
    q_rotary = ((jnp.arange(HEADS * DQK_B) % DQK_B) >= MLA_NOPE).astype(F32)[None, :]
    tabs = dict(c=_rope_tables(L, DH // 2, LANES), bk=_rope_tables(L, MLA_ROPE // 2, LANES), bq=q_rotary)
    saved = [None] * DEPTH
    h, saved[0], gw1, layers[0] = _layer_fwd(h, layers[0], tabs, 0, side=_Gather(shards(1)))
    layers[1] = layer_weights(1, *gw1)
    h, saved[1], _, _ = _layer_fwd(h, layers[1], tabs, 1)
    loss_vec, dh, g_final = _loss_head(h, final_g, loss_target[0], name="loss_head")

    grads = [None] * DEPTH
    dh, grads[1], _, _ = _layer_bwd(dh, saved[1], layers[1], tabs, 1)
    dh, grads[0], recv1, recv0 = _layer_bwd(dh, saved[0], layers[0], tabs, 0, side=_Exchange(_grad_chunks(grads[1], "l1")),
                                            own_exchange=True)
    r_in, r_uq, r_ukv, r_br, r_out = (jnp.stack([a, b], axis=1) for a, b in zip(recv0, recv1))

    def stack(name):
        return jnp.stack([grads[l][name] for l in range(DEPTH)])

    small = _pack_small(dict(norm_g=stack("norm_g"), b_f=stack("b_f"), g_cq=stack("g_cq"), g_ckv=stack("g_ckv"),
                             sinks=stack("sinks"), final_g=g_final, loss=loss_vec[0, 0:1],
                             meta=dh[PAD:BLK]))
    (g_small,) = _comm(_Gather([small]), name="gather_small")

    def adam_big(w_, parts, m_, v_, name):
        shape = w_.shape
        C = shape[-1]
        R = math.prod(shape[:-1])
        outs = _adamw(w_.reshape(R, C), parts.reshape(parts.shape[0], R, C), m_.reshape(R, C), v_.reshape(R, C),
                      name=name)
        return [o.reshape(shape) for o in outs]

    res = {}
    res["w_in"] = adam_big(w_in, r_in, m_w_in, v_w_in, "adam_w_in")
    res["w_uq"] = adam_big(w_uq, r_uq, m_w_uq, v_w_uq, "adam_w_uq")
    res["w_ukv"] = adam_big(w_ukv, r_ukv, m_w_ukv, v_w_ukv, "adam_w_ukv")
    res["w_branch"] = adam_big(w_branch, r_br, m_w_branch, v_w_branch, "adam_w_branch")
    res["w_out"] = adam_big(w_out, r_out, m_w_out, v_w_out, "adam_w_out")

    small_w = dict(norm_g=norm_g, b_f=b_f, g_cq=g_cq, g_ckv=g_ckv, sinks=sinks, final_g=final_g)
    small_m = dict(norm_g=m_norm_g, b_f=m_b_f, g_cq=m_g_cq, g_ckv=m_g_ckv, sinks=m_sinks, final_g=m_final_g)
    small_v = dict(norm_g=v_norm_g, b_f=v_b_f, g_cq=v_g_cq, g_ckv=v_g_ckv, sinks=v_sinks, final_g=v_final_g)
    sm = _adamw(_pack_small(small_w), g_small, _pack_small(small_m), _pack_small(small_v), name="adam_small")
    shapes = {k: a.shape for k, a in small_w.items()}
    shapes_all = dict(shapes, loss=(), meta=(N_META, D_MODEL))
    sm_g = _unpack_small(sm[0], shapes_all)
    sm_d, sm_m, sm_v = (_unpack_small(t, shapes) for t in sm[1:])
    for k in shapes:
        res[k] = [sm_g[k], sm_d[k], sm_m[k], sm_v[k]]
    g_meta_mine = lax.dynamic_slice(sm_g["meta"], (0, my_idx * 128), (N_META, 128))
    res["meta_tokens"] = _adamw(meta_tokens, g_meta_mine[None], m_meta_tokens, v_meta_tokens, name="adam_meta")

    order = ["meta_tokens", "norm_g", "w_in", "b_f", "g_cq", "g_ckv", "w_uq", "w_ukv", "sinks", "w_branch", "w_out",
             "final_g"]
    grad_x = dh[BLK:][None]
    return (sm_g["loss"], grad_x, *[res[k][0] for k in order], *[res[k][1] for k in order],
            *[res[k][2] for k in order], *[res[k][3] for k in order])
```

```python
import math

import jax
import jax.numpy as jnp
from jax import lax
from jax.experimental import pallas as pl
from jax.experimental.pallas import tpu as pltpu

F32 = jnp.float32
BF16 = jnp.bfloat16

D_MODEL = 1024
DEPTH = 2
N_META = 16
BLK = 128
PAD = BLK - N_META
ROPE_THETA = 10000.0
EPS = 1e-6
NEG = -1e30
BIG = 1e30
HEADS = 8
DH = 64
MLA_NOPE = 64
MLA_ROPE = 32
MLA_QLORA = 384
MLA_KVLORA = 256
SWA_KV_HEADS = 2
WINDOW = 128
BRANCH_W = 512
N_BRANCH = 3
N_IN = 7592

ADAM_LR = 0.001
ADAM_B1 = 0.9
ADAM_B2 = 0.999
ADAM_EPS = 1e-08
ADAM_WD = 0.01
ADAM_STEP = 10

N_DEV = 8
MESH = pl.DeviceIdType.MESH

NP_IN = 8192
QKVA, ZA, FA = 0, 1536, 2048
CKV, KR, CQ = 2304, 2560, 2688
GATES = 3072
ZB = 6144
QKVC, ZC = 6912, 7680
_RUNS = ((0, 1536, QKVA), (1536, 1544, FA), (1544, 2056, ZA), (2056, 2440, CQ), (2440, 2696, CKV), (2696, 2728, KR),
         (2728, 3240, ZB), (3240, 4008, QKVC), (4008, 4520, ZC), (4520, 7592, GATES))

VMEM_LIMIT = 48 * 1024 * 1024
ATT_T = 384
ATT_HP = 1
ROW_T = 384
LANES = 128


def _pick(dim, prefs):
    for p in prefs:
        if dim % p == 0:
            return p
    return dim


def _cparams(sem):
    return pltpu.CompilerParams(dimension_semantics=sem, vmem_limit_bytes=VMEM_LIMIT)


def _side_cparams(side, sem):
    if side is None:
        return _cparams(sem)
    return pltpu.CompilerParams(dimension_semantics=("arbitrary",) * len(sem), vmem_limit_bytes=VMEM_LIMIT,
                                has_side_effects=True)


def _mm(a, b, *, ta=False, tb=False, add=None, out_dtype=F32, side=None, name):
    M = a.shape[1] if ta else a.shape[0]
    K = a.shape[0] if ta else a.shape[1]
    N = b.shape[0] if tb else b.shape[1]
    assert K == (b.shape[1] if tb else b.shape[0])
    tm = _pick(M, (704, 1024, 512, 384, 256, 128))
    tn = _pick(N, (1024, 768, 512, 384, 256, 128))
    tk = _pick(K, (4096, 1408, 1024, 768, 512, 384, 256, 128))
    nk = K // tk
    dims = (((0 if ta else 1,), (1 if tb else 0,)), ((), ()))

    sn = 0 if side is None else side.n
    n_in = 2 + (add is not None)
    grid = (M // tm, N // tn, nk)

    def body(*refs):
        a_ref, b_ref = refs[:2]
        c_ref = refs[2] if add is not None else None
        o_ref = refs[n_in + sn]
        scratch = refs[n_in + 2 * sn + 1:]
        if side is not None:
            step = (pl.program_id(0) * grid[1] + pl.program_id(1)) * nk + pl.program_id(2)
            _ride(side, [*refs[n_in:n_in + sn], *refs[n_in + sn + 1:n_in + 2 * sn + 1], *scratch[nk > 1:]],
                  step, grid[0] * grid[1] * nk)
        r = lax.dot_general(a_ref[...].astype(BF16), b_ref[...].astype(BF16), dims, preferred_element_type=F32)

        def finish(total):
            if c_ref is not None:
                total = total + c_ref[...]
            o_ref[...] = total.astype(out_dtype)

        if nk == 1:
            finish(r)
        else:
            acc = scratch[0]
            k = pl.program_id(2)

            @pl.when(k == 0)
            def _():
                acc[...] = r

            @pl.when(k > 0)
            def _():
                acc[...] += r

            @pl.when(k == nk - 1)
            def _():
                finish(acc[...])

    a_spec = pl.BlockSpec((tk, tm), lambda i, j, k: (k, i)) if ta else pl.BlockSpec((tm, tk), lambda i, j, k: (i, k))
    b_spec = pl.BlockSpec((tn, tk), lambda i, j, k: (j, k)) if tb else pl.BlockSpec((tk, tn), lambda i, j, k: (k, j))
    o_spec = pl.BlockSpec((tm, tn), lambda i, j, k: (i, j))
    any_spec = pl.BlockSpec(memory_space=pl.ANY)
    outs = pl.pallas_call(
        body, name=name,
        grid=grid,
        in_specs=[a_spec, b_spec] + ([o_spec] if add is not None else []) + [any_spec] * sn,
        out_specs=[o_spec] + [any_spec] * sn,
        out_shape=[jax.ShapeDtypeStruct((M, N), out_dtype)] + ([] if side is None else side.out_shape),
        scratch_shapes=([pltpu.VMEM((tm, tn), F32)] if nk > 1 else []) + ([] if side is None else side.scratch),
        compiler_params=_side_cparams(side, ("parallel", "parallel", "arbitrary")),
    )(*((a, b) if add is None else (a, b, add)), *([] if side is None else side.arrs))
    return outs[0] if side is None else outs


def _col_spec(tm, width, col):
    assert col % width == 0
    return pl.BlockSpec((tm, width), lambda i, _c=col // width: (i, _c))


def _rms_fwd(x, g, *, col=0, name):
    L = x.shape[0]
    D = g.shape[0]
    tm = ROW_T

    def body(x_ref, g_ref, y_ref):
        xv = x_ref[...]
        rstd = lax.rsqrt(jnp.mean(xv * xv, axis=-1, keepdims=True) + EPS)
        y_ref[...] = (xv * rstd * g_ref[...]).astype(BF16)

    return pl.pallas_call(
        body, name=name, grid=(L // tm,),
        in_specs=[_col_spec(tm, D, col), pl.BlockSpec((1, D), lambda i: (0, 0))],
        out_specs=pl.BlockSpec((tm, D), lambda i: (i, 0)),
        out_shape=jax.ShapeDtypeStruct((L, D), BF16),
        compiler_params=_cparams(("parallel",)),
    )(x, g.reshape(1, D))


def _rms_bwd(dy, x, g, *, col=0, add=None, into=None, name):
    L = x.shape[0]
    D = g.shape[0]
    tm = ROW_T

    def body(*refs):
        dy_ref, x_ref, g_ref = refs[:3]
        add_ref = refs[3] if add is not None else None
        dx_ref, dg_ref = refs[-2:]
        i = pl.program_id(0)
        xv = x_ref[...]
        dyv = dy_ref[...]
        rstd = lax.rsqrt(jnp.mean(xv * xv, axis=-1, keepdims=True) + EPS)
        xhat = xv * rstd
        part = jnp.sum(dyv * xhat, axis=0, keepdims=True)

        @pl.when(i == 0)
        def _():
            dg_ref[...] = part

        @pl.when(i > 0)
        def _():
            dg_ref[...] += part

        dxh = dyv * g_ref[...]
        dx = rstd * (dxh - xhat * jnp.mean(dxh * xhat, axis=-1, keepdims=True))
        if add_ref is not None:
            dx = dx + add_ref[...]
        dx_ref[...] = dx.astype(dx_ref.dtype)

    row = pl.BlockSpec((tm, D), lambda i: (i, 0))
    in_specs = [row, _col_spec(tm, D, col), pl.BlockSpec((1, D), lambda i: (0, 0))]
    args = [dy, x, g.reshape(1, D)]
    aliases = {}
    if add is not None:
        in_specs.append(row)
        args.append(add)
    if into is not None:
        in_specs.append(pl.BlockSpec(memory_space=pl.ANY))
        args.append(into)
        aliases = {len(args) - 1: 0}
        dx_spec, dx_shape = _col_spec(tm, D, col), jax.ShapeDtypeStruct(into.shape, into.dtype)
    else:
        dx_spec, dx_shape = row, jax.ShapeDtypeStruct((L, D), F32)
    dx, dg = pl.pallas_call(
        body, name=name, grid=(L // tm,),
        in_specs=in_specs,
        out_specs=[dx_spec, pl.BlockSpec((1, D), lambda i: (0, 0))],
        out_shape=[dx_shape, jax.ShapeDtypeStruct((1, D), F32)],
        input_output_aliases=aliases,
        compiler_params=_cparams(("arbitrary",)),
    )(*args)
    return dx, dg.reshape(D)


def _loss_head(h, g, target, *, name):
    L, D = h.shape
    nb = L // BLK

    def body(h_ref, g_ref, t_ref, loss_ref, dh_ref, dg_ref):
        i = pl.program_id(0)

        @pl.when(i == 0)
        def _():
            loss_ref[...] = jnp.zeros_like(loss_ref)
            dg_ref[...] = jnp.zeros_like(dg_ref)
            dh_ref[...] = jnp.zeros_like(dh_ref)

        @pl.when(i > 0)
        def _():
            xv = h_ref[...]
            gv = g_ref[...]
            rstd = lax.rsqrt(jnp.mean(xv * xv, axis=-1, keepdims=True) + EPS)
            xhat = xv * rstd
            err = xhat * gv - t_ref[...]
            row = jnp.mean(err * err, axis=-1, keepdims=True)
            loss_ref[...] += 0.5 * jnp.sum(row, axis=0, keepdims=True)
            dy = err * (1.0 / D)
            dg_ref[...] += jnp.sum(dy * xhat, axis=0, keepdims=True)
            dxh = dy * gv
            dh_ref[...] = rstd * (dxh - xhat * jnp.mean(dxh * xhat, axis=-1, keepdims=True))

    loss, dh, dg = pl.pallas_call(
        body, name=name, grid=(nb,),
        in_specs=[pl.BlockSpec((BLK, D), lambda i: (i, 0)), pl.BlockSpec((1, D), lambda i: (0, 0)),
                  pl.BlockSpec((BLK, D), lambda i: (jnp.maximum(i - 1, 0), 0))],
        out_specs=[pl.BlockSpec((1, 128), lambda i: (0, 0)), pl.BlockSpec((BLK, D), lambda i: (i, 0)),
                   pl.BlockSpec((1, D), lambda i: (0, 0))],
        out_shape=[jax.ShapeDtypeStruct((1, 128), F32), jax.ShapeDtypeStruct((L, D), F32),
                   jax.ShapeDtypeStruct((1, D), F32)],
        compiler_params=_cparams(("arbitrary",)),
    )(h, g.reshape(1, D), target)
    return loss, dh, dg.reshape(D)


_NT = (((1,), (1,)), ((), ()))
_TN = (((0,), (0,)), ((), ()))


def _attn_fwd(q, k, v, kbias, *, T, window=False, sink=None, side=None, name):
    sn = 0 if side is None else side.n
    H, L, dk = q.shape
    Hkv = k.shape[0]
    dvx = v.shape[2]
    dv = dvx // 2
    G = H // Hkv
    nt = L // T
    Hb = kbias.shape[0]
    reps = T // LANES
    assert not window or T == WINDOW
    HP = G if G > 1 else ATT_HP
    NS = 1 if G > 1 else HP
    R = HP * T // NS
    HKV = HP // G
    HB = HP if Hb > 1 else 1
    assert G == 1 or Hb == 1

    def body(*refs):
        q_ref, k_ref, v_ref, kb_ref = refs[:4]
        n = 4
        if sink is not None:
            sk_ref = refs[n]
            n += 1
        side_in = refs[n:n + sn]
        n += sn
        o_ref, lse_ref = refs[n:n + 2]
        side_out = refs[n + 2:n + 2 + sn]
        n += 2 + sn
        m_scs, acc_scs, buf_a, buf_b = (refs[n + t * NS:n + (t + 1) * NS] for t in range(4))
        i = pl.program_id(1)
        if side is not None:
            _ride(side, [*side_in, *side_out, *refs[n + 4 * NS:]], pl.program_id(0) * nt + i, (H // HP) * nt)
        for a in range(NS):
            if sink is not None:
                sk = [jnp.broadcast_to(sk_ref[b, :, 0:1], (T, LANES)) for b in range(HP)]
                m_scs[a][...] = jnp.concatenate(sk, axis=0) if G > 1 else sk[a]
                lane = lax.broadcasted_iota(jnp.int32, (R, dvx), 1)
                acc_scs[a][...] = jnp.where(lane >= dv, 1.0, 0.0)
            else:
                m_scs[a][...] = jnp.full((R, LANES), NEG, F32)
                acc_scs[a][...] = jnp.zeros((R, dvx), F32)
        row = lax.broadcasted_iota(jnp.int32, (R, T), 0) & (T - 1) if G > 1 else \
            lax.broadcasted_iota(jnp.int32, (R, T), 0)
        col = lax.broadcasted_iota(jnp.int32, (R, T), 1)

        def logits(a, j, biased=True):
            rows = pl.ds(pl.multiple_of(j * T, T), T)
            qv = q_ref[...].reshape(R, dk) if G > 1 else q_ref[a]
            s = lax.dot_general(qv, k_ref[a, rows, :], _NT, preferred_element_type=F32)
            return s - kb_ref[a if HB > 1 else 0, j] if biased else s

        def update(a, s, j, kind):
            rows = pl.ds(pl.multiple_of(j * T, T), T)
            m_sc, acc_sc = m_scs[a], acc_scs[a]
            if kind == "diag":
                s = jnp.where(row >= col, s, NEG)
            elif kind == "prev":
                s = jnp.where((col > row) & (i > 0), s, NEG)
            m_prev = m_sc[...]
            m_new = jnp.maximum(m_prev, jnp.max(s, axis=1, keepdims=True))
            alpha = jnp.exp(m_prev - m_new)
            p = jnp.exp(s - jnp.tile(m_new, (1, reps)))
            acc_sc[...] = alpha[:, :dvx] * acc_sc[...] + jnp.dot(p.astype(BF16), v_ref[a, rows, :],
                                                                 preferred_element_type=F32)
            m_sc[...] = m_new

        if window:
            before = jnp.maximum(i - 1, 0)
            s_prev = [logits(a, before) for a in range(NS)]
            s_diag = [logits(a, i) for a in range(NS)]
            for a in range(NS):
                update(a, s_prev[a], before, "prev")
                update(a, s_diag[a], i, "diag")
        else:
            later = Hb > 1

            def fill(buf, j, biased=True):
                for a in range(NS):
                    buf[a][...] = logits(a, j, biased)

            def drain(buf, j, kind):
                for a in range(NS):
                    update(a, buf[a][...], j, kind)

            fill(buf_a, 0)

            def pair(t, c):
                fill(buf_b, 2 * t + 1, later)
                drain(buf_a, 2 * t, "full")
                fill(buf_a, 2 * t + 2, later)
                drain(buf_b, 2 * t + 1, "full")
                return c

            lax.fori_loop(0, i // 2, pair, 0)

            @pl.when(i % 2 == 1)
            def _():
                fill(buf_b, i, later)
                drain(buf_a, i - 1, "full")
                drain(buf_b, i, "diag")

            @pl.when(i % 2 == 0)
            def _():
                drain(buf_a, i, "diag")

        for a in range(NS):
            acc = acc_scs[a][...]
            ov = acc[:, :dv] / acc[:, dv:]
            lsev = m_scs[a][:, 0:1] + jnp.log(acc[:, dv:dv + 1])
            if G > 1:
                o_ref[...] = ov.reshape(HP, T, dv)
                lse_ref[...] = lsev.reshape(HP, T, 1)
            else:
                o_ref[a] = ov
                lse_ref[a] = lsev

    in_specs = [pl.BlockSpec((HP, T, dk), lambda h, i: (h, i, 0)),
                pl.BlockSpec((HKV, L, dk), lambda h, i: (h, 0, 0)),
                pl.BlockSpec((HKV, L, dvx), lambda h, i: (h, 0, 0)),
                pl.BlockSpec((HB, nt, 1, T), lambda h, i: (h if Hb > 1 else 0, 0, 0, 0))]
    args = [q, k, v, kbias]
    if sink is not None:
        in_specs += [pl.BlockSpec((HP, 1, LANES), lambda h, i: (h, 0, 0))]
        args += [sink]
    any_spec = pl.BlockSpec(memory_space=pl.ANY)
    return pl.pallas_call(
        body, name=name, grid=(H // HP, nt),
        in_specs=in_specs + [any_spec] * sn,
        out_specs=[pl.BlockSpec((HP, T, dv), lambda h, i: (h, i, 0)),
                   pl.BlockSpec((HP, T, 1), lambda h, i: (h, i, 0))] + [any_spec] * sn,
        out_shape=[jax.ShapeDtypeStruct((H, L, dv), F32), jax.ShapeDtypeStruct((H, L, 1), F32)]
        + ([] if side is None else side.out_shape),
        scratch_shapes=[pltpu.VMEM((R, LANES), F32)] * NS + [pltpu.VMEM((R, dvx), F32)] * NS
        + [pltpu.VMEM((R, T), F32)] * (2 * NS) + ([] if side is None else side.scratch),
        compiler_params=_side_cparams(side, ("parallel", "arbitrary")),
    )(*args, *([] if side is None else side.arrs))


def _attn_bwd_t(q, k, v, kbias_col, lse_row, delta_row, do, *, T, fox=False, side=None, name):
    sn = 0 if side is None else side.n
    H, L, dk = q.shape
    dv = do.shape[2]
    nt = L // T
    Hb = kbias_col.shape[0]

    def body(*refs):
        q_ref, k_ref, v_ref, kb_ref, lse_ref, dl_ref, do_ref = refs[:7]
        side_in = refs[7:7 + sn]
        n = 7 + sn
        dq_ref, dk_ref, dv_ref = refs[n:n + 3]
        n += 3
        if fox:
            dcq_ref, dck_ref = refs[n:n + 2]
            n += 2
        side_out = refs[n:n + sn]
        n += sn
        buf_a, buf_b = refs[n:n + 2], refs[n + 2:n + 4]
        j = pl.program_id(1)
        if side is not None:
            _ride(side, [*side_in, *side_out, *refs[n + 4:]], pl.program_id(0) * nt + j, H * nt)

        @pl.when(j == 0)
        def _():
            dq_ref[...] = jnp.zeros_like(dq_ref)
            if fox:
                dcq_ref[...] = jnp.zeros_like(dcq_ref)

        dk_ref[...] = jnp.zeros_like(dk_ref)
        dv_ref[...] = jnp.zeros_like(dv_ref)
        if fox:
            dck_ref[...] = jnp.zeros_like(dck_ref)
        kb = k_ref[0]
        vb = v_ref[0][:, :dv]
        kbias_j = kb_ref[0]
        key = lax.broadcasted_iota(jnp.int32, (T, T), 0)
        qry = lax.broadcasted_iota(jnp.int32, (T, T), 1)

        def fill(buf, i):
            rows = pl.ds(pl.multiple_of(i * T, T), T)
            buf[0][...] = lax.dot_general(kb, q_ref[0, rows, :], _NT, preferred_element_type=F32) - kbias_j
            buf[1][...] = lax.dot_general(vb, do_ref[0, rows, :], _NT, preferred_element_type=F32)

        def drain(buf, i, kind):
            rows = pl.ds(pl.multiple_of(i * T, T), T)
            st = buf[0][...]
            if kind == "diag":
                st = jnp.where(key <= qry, st, NEG)
            pt = jnp.exp(st - lse_ref[0, i])
            dst = pt * (buf[1][...] - dl_ref[0, i])
            dsb = dst.astype(BF16)
            dv_ref[0] += jnp.dot(pt.astype(BF16), do_ref[0, rows, :], preferred_element_type=F32)
            dk_ref[0] += jnp.dot(dsb, q_ref[0, rows, :], preferred_element_type=F32)
            dq_ref[0, rows, :] += lax.dot_general(dsb, kb, _TN, preferred_element_type=F32)
            if fox:
                dcq_ref[0, i] += jnp.sum(dst, axis=0, keepdims=True)
                dck_ref[0] += -jnp.sum(dst, axis=1, keepdims=True)

        first = j + 1
        rest = nt - first
        fill(buf_a, j)
        fill(buf_b, jnp.minimum(first, nt - 1))
        drain(buf_a, j, "diag")

        def pair(t, c):
            i0 = first + 2 * t
            fill(buf_a, i0 + 1)
            drain(buf_b, i0, "full")
            fill(buf_b, jnp.minimum(i0 + 2, nt - 1))
            drain(buf_a, i0 + 1, "full")
            return c

        lax.fori_loop(0, rest // 2, pair, 0)

        @pl.when(rest % 2 == 1)
        def _():
            drain(buf_b, nt - 1, "full")

    rows_spec = pl.BlockSpec((1, nt, 1, T), lambda h, j: (h, 0, 0, 0))
    in_specs = [pl.BlockSpec((1, L, dk), lambda h, j: (h, 0, 0)),
                pl.BlockSpec((1, T, dk), lambda h, j: (h, j, 0)),
                pl.BlockSpec((1, T, v.shape[2]), lambda h, j: (h, j, 0)),
                pl.BlockSpec((1, T, 1), lambda h, j: (h if Hb > 1 else 0, j, 0)),
                rows_spec, rows_spec,
                pl.BlockSpec((1, L, dv), lambda h, j: (h, 0, 0))]
    out_specs = [pl.BlockSpec((1, L, dk), lambda h, j: (h, 0, 0)),
                 pl.BlockSpec((1, T, dk), lambda h, j: (h, j, 0)),
                 pl.BlockSpec((1, T, dv), lambda h, j: (h, j, 0))]
    out_shape = [jax.ShapeDtypeStruct((H, L, dk), F32), jax.ShapeDtypeStruct((H, L, dk), F32),
                 jax.ShapeDtypeStruct((H, L, dv), F32)]
    if fox:
        out_specs += [rows_spec, pl.BlockSpec((1, T, 1), lambda h, j: (h, j, 0))]
        out_shape += [jax.ShapeDtypeStruct((H, nt, 1, T), F32), jax.ShapeDtypeStruct((H, L, 1), F32)]
    any_spec = pl.BlockSpec(memory_space=pl.ANY)
    outs = pl.pallas_call(
        body, name=name, grid=(H, nt),
        in_specs=in_specs + [any_spec] * sn, out_specs=out_specs + [any_spec] * sn,
        out_shape=out_shape + ([] if side is None else side.out_shape),
        scratch_shapes=[pltpu.VMEM((T, T), F32)] * 4 + ([] if side is None else side.scratch),
        compiler_params=_side_cparams(side, ("parallel", "arbitrary")),
    )(q, k, v, kbias_col, lse_row, delta_row, do, *([] if side is None else side.arrs))
    main, rest = outs[:len(outs) - sn], outs[len(outs) - sn:]
    return (*(main if fox else (*main, None, None)), *rest)


def _attn_bwd_window(q, k, v, kbias, o, lse, do, *, name):
    H, L, dk = q.shape
    Hkv = k.shape[0]
    dv = do.shape[2]
    G = H // Hkv
    T = WINDOW
    nt = L // T
    R = G * T

    def body(q_ref, kc_ref, kp_ref, vc_ref, vp_ref, kb_ref, o_ref, lse_ref, do_ref, dq_ref, dk_ref, dv_ref,
             dk_sc, dv_sc):
        i = pl.program_id(1)
        row = lax.broadcasted_iota(jnp.int32, (R, T), 0) & (T - 1)
        col = lax.broadcasted_iota(jnp.int32, (R, T), 1)

        @pl.when(i == 0)
        def _():
            dk_sc[...] = jnp.zeros_like(dk_sc)
            dv_sc[...] = jnp.zeros_like(dv_sc)

        @pl.when(i == nt)
        def _():
            dk_ref[0] = dk_sc[...]
            dv_ref[0] = dv_sc[...]

        @pl.when(i < nt)
        def _():
            qb = q_ref[...].reshape(R, dk)
            dof = do_ref[...].reshape(R, dv)
            dob = dof.astype(BF16)
            lse_c = lse_ref[...].reshape(R, 1)
            delta = jnp.sum(dof * o_ref[...].reshape(R, dv), axis=1, keepdims=True)

            def grads(kt, vt, kbias_j, mask):
                s = lax.dot_general(qb, kt, _NT, preferred_element_type=F32) - kbias_j
                p = jnp.exp(jnp.where(mask, s, NEG) - lse_c)
                dp = lax.dot_general(dob, vt, _NT, preferred_element_type=F32)
                ds = (p * (dp - delta)).astype(BF16)
                return (jnp.dot(ds, kt, preferred_element_type=F32),
                        lax.dot_general(ds, qb, _TN, preferred_element_type=F32),
                        lax.dot_general(p.astype(BF16), dob, _TN, preferred_element_type=F32))

            ip = jnp.maximum(i - 1, 0)
            dq_p, dk_p, dv_p = grads(kp_ref[0], vp_ref[0][:, :dv], kb_ref[0, ip], (col > row) & (i > 0))
            dq_c, dk_c, dv_c = grads(kc_ref[0], vc_ref[0][:, :dv], kb_ref[0, i], row >= col)
            dq_ref[...] = (dq_p + dq_c).reshape(G, T, dk)
            dk_ref[0] = dk_sc[...] + dk_p
            dv_ref[0] = dv_sc[...] + dv_p
            dk_sc[...] = dk_c
            dv_sc[...] = dv_c

    def cur(i):
        return jnp.minimum(i, nt - 1)

    def prev(i):
        return jnp.maximum(jnp.minimum(i, nt - 1) - 1, 0)

    def written(i):
        return jnp.maximum(i - 1, 0)

    qs = lambda d: pl.BlockSpec((G, T, d), lambda h, i: (h, cur(i), 0))
    return pl.pallas_call(
        body, name=name, grid=(Hkv, nt + 1),
        in_specs=[qs(dk),
                  pl.BlockSpec((1, T, dk), lambda h, i: (h, cur(i), 0)),
                  pl.BlockSpec((1, T, dk), lambda h, i: (h, prev(i), 0)),
                  pl.BlockSpec((1, T, v.shape[2]), lambda h, i: (h, cur(i), 0)),
                  pl.BlockSpec((1, T, v.shape[2]), lambda h, i: (h, prev(i), 0)),
                  pl.BlockSpec((1, nt, 1, T), lambda h, i: (0, 0, 0, 0)),
                  qs(dv), qs(1), qs(dv)],
        out_specs=[qs(dk),
                   pl.BlockSpec((1, T, dk), lambda h, i: (h, written(i), 0)),
                   pl.BlockSpec((1, T, dv), lambda h, i: (h, written(i), 0))],
        out_shape=[jax.ShapeDtypeStruct((H, L, dk), F32), jax.ShapeDtypeStruct((Hkv, L, dk), F32),
                   jax.ShapeDtypeStruct((Hkv, L, dv), F32)],
        scratch_shapes=[pltpu.VMEM((T, dk), F32), pltpu.VMEM((T, dv), F32)],
        compiler_params=_cparams(("parallel", "arbitrary")),
    )(q, k, k, v, v, kbias, o, lse, do)


def _adamw(w, gparts, m, v, *, name):
    n, R, C = gparts.shape
    tr = _pick(R, (128, 64, 32, 16, 8))
    c1 = 1.0 - ADAM_B1 ** ADAM_STEP
    c2 = 1.0 - ADAM_B2 ** ADAM_STEP

    def body(w_ref, g_ref, m_ref, v_ref, go_ref, d_ref, mo_ref, vo_ref):
        g = g_ref[0].astype(F32)
        for t in range(1, n):
            g = g + g_ref[t].astype(F32)
        mn = ADAM_B1 * m_ref[...] + (1.0 - ADAM_B1) * g
        vn = ADAM_B2 * v_ref[...] + (1.0 - ADAM_B2) * (g * g)
        go_ref[...] = g
        mo_ref[...] = mn
        vo_ref[...] = vn
        d_ref[...] = -ADAM_LR * ((mn / c1) / (jnp.sqrt(vn / c2) + ADAM_EPS) + ADAM_WD * w_ref[...])

    spec = pl.BlockSpec((tr, C), lambda i: (i, 0))
    return pl.pallas_call(
        body, name=name, grid=(R // tr,),
        in_specs=[spec, pl.BlockSpec((n, tr, C), lambda i: (0, i, 0)), spec, spec],
        out_specs=[spec] * 4,
        out_shape=[jax.ShapeDtypeStruct((R, C), F32)] * 4,
        compiler_params=_cparams(("parallel",)),
    )(w, gparts, m, v)


def _me():
    return lax.axis_index("x"), lax.axis_index("y"), lax.axis_index("c")


class _CommJob:
    def __init__(self, arrs):
        self.arrs = list(arrs)
        self.n = len(arrs)
        self.scratch = [pltpu.SemaphoreType.DMA((self.n, 7)), pltpu.SemaphoreType.DMA((self.n, 7)),
                        pltpu.SemaphoreType.DMA((self.n,))]

    def bind(self, refs):
        n = self.n
        self.ins, self.outs = refs[:n], refs[n:2 * n]
        self.send_sems, self.recv_sems, self.local_sems = refs[2 * n:2 * n + 3]

    def middle(self):
        pass


class _Gather(_CommJob):
    def __init__(self, arrs):
        super().__init__(arrs)
        self.out_shape = [jax.ShapeDtypeStruct((N_DEV, *a.shape), a.dtype) for a in arrs]

    def _where(self):
        x, y, c = _me()
        return (x, y, c), (x, y, 1 - c), [(1 - x, y), (x, 1 - y), (1 - x, 1 - y)], c

    def _copy(self, t, k, block, to, src=None):
        dst = self.outs[t].at[4 * block[0] + 2 * block[1] + block[2]]
        return pltpu.make_async_remote_copy(
            src_ref=dst if src is None else src, dst_ref=dst,
            send_sem=self.send_sems.at[t, k], recv_sem=self.recv_sems.at[t, k],
            device_id=to, device_id_type=MESH)

    def _mine(self, t, me):
        return pltpu.make_async_copy(self.ins[t], self.outs[t].at[4 * me[0] + 2 * me[1] + me[2]],
                                     self.local_sems.at[t])

    def _first(self, t, me, sibling, chips, c):
        return [self._copy(t, 0, me, sibling, src=self.ins[t])] + \
               [self._copy(t, 1 + j, me, (*chip, c), src=self.ins[t]) for j, chip in enumerate(chips)]

    def start(self):
        me, sibling, chips, c = self._where()
        for t in range(self.n):
            self._mine(t, me).start()
            for cp in self._first(t, me, sibling, chips, c):
                cp.start()

    def middle(self):
        me, sibling, chips, c = self._where()
        for j, chip in enumerate(chips):
            for t in range(self.n):
                self._copy(t, 1 + j, (*chip, c), me).wait_recv()
                self._copy(t, 4 + j, (*chip, c), sibling).start()

    def finish(self):
        me, sibling, chips, c = self._where()
        for t in range(self.n):
            self._copy(t, 0, sibling, me).wait_recv()
            for j, chip in enumerate(chips):
                self._copy(t, 4 + j, (*chip, 1 - c), me).wait_recv()
        for t in range(self.n):
            for cp in self._first(t, me, sibling, chips, c):
                cp.wait_send()
            for j, chip in enumerate(chips):
                self._copy(t, 4 + j, (*chip, c), sibling).wait_send()
            self._mine(t, me).wait()


class _Exchange(_CommJob):
    def __init__(self, arrs):
        super().__init__(arrs)
        self.out_shape = [jax.ShapeDtypeStruct(a.shape, a.dtype) for a in arrs]

    def _copies(self, t):
        x, y, c = _me()
        my_idx = 4 * x + 2 * y + c
        pairs = []
        for k in range(1, N_DEV):
            peer = (x ^ ((k >> 2) & 1), y ^ ((k >> 1) & 1), c ^ (k & 1))
            peer_idx = 4 * peer[0] + 2 * peer[1] + peer[2]
            sems = dict(send_sem=self.send_sems.at[t, k - 1], recv_sem=self.recv_sems.at[t, k - 1],
                        device_id=peer, device_id_type=MESH)
            pairs.append((pltpu.make_async_remote_copy(src_ref=self.ins[t].at[peer_idx],
                                                       dst_ref=self.outs[t].at[my_idx], **sems),
                          pltpu.make_async_remote_copy(src_ref=self.ins[t].at[peer_idx],
                                                       dst_ref=self.outs[t].at[peer_idx], **sems)))
        return pairs

    def _mine(self, t):
        x, y, c = _me()
        my_idx = 4 * x + 2 * y + c
        return pltpu.make_async_copy(self.ins[t].at[my_idx], self.outs[t].at[my_idx], self.local_sems.at[t])

    def start(self):
        for t in range(self.n):
            self._mine(t).start()
            for snd, _ in self._copies(t):
                snd.start()

    def finish(self):
        for t in range(self.n):
            pairs = self._copies(t)
            for _, rcv in pairs:
                rcv.wait_recv()
            for snd, _ in pairs:
                snd.wait_send()
            self._mine(t).wait()


def _comm(job, *, name):
    def body(*refs):
        job.bind(refs)
        job.start()
        job.middle()
        job.finish()

    any_spec = pl.BlockSpec(memory_space=pl.ANY)
    return pl.pallas_call(
        body, name=name,
        in_specs=[any_spec] * job.n, out_specs=[any_spec] * job.n,
        out_shape=job.out_shape, scratch_shapes=job.scratch,
        compiler_params=pltpu.CompilerParams(has_side_effects=True),
    )(*job.arrs)


def _ride(job, refs, step, total):
    job.bind(refs)

    @pl.when(step == 0)
    def _():
        job.start()

    @pl.when(step == (total * 3) // 5)
    def _():
        job.middle()

    @pl.when(step == total - 1)
    def _():
        job.finish()


SCALE_A = DH ** -0.5
SCALE_B = (MLA_NOPE + MLA_ROPE) ** -0.5
SCALE_C = DH ** -0.5
DQK_B = MLA_NOPE + MLA_ROPE


def _rope_tables(L, half, width):
    pos = (jnp.arange(L) - PAD).astype(F32)
    lane = jnp.arange(width)
    inv = ROPE_THETA ** (-(lane % half).astype(F32) / half)
    ang = pos[:, None] * inv[None, :]
    sign = jnp.where(lane % (2 * half) < half, -1.0, 1.0).astype(F32)
    return jnp.cos(ang), jnp.sin(ang) * sign[None, :]


def _rope_lanes(x, cos, sin, half):
    W = x.shape[1]
    lane = lax.broadcasted_iota(jnp.int32, x.shape, 1)
    first = (lane & (2 * half - 1)) < half
    partner = jnp.where(first, pltpu.roll(x, W - half, 1), pltpu.roll(x, half, 1))
    return x * cos + partner * sin


def _tile_lanes(t, width):
    return t if t.shape[1] == width else jnp.tile(t, (1, width // t.shape[1]))


def _split(x, H, d, dst):
    for h in range(H):
        dst[h] = x[:, d * h:d * (h + 1)].astype(dst.dtype)


def _join(src, H):
    return jnp.concatenate([src[h] for h in range(H)], axis=1)


def _head_spec(H, tm, d):
    return pl.BlockSpec((H, tm, d), lambda i: (0, i, 0))


def _split_values(x, H, dst):
    ones = jnp.ones((x.shape[0], DH), BF16)
    for h in range(H):
        dst[h] = jnp.concatenate([x[:, DH * h:DH * (h + 1)].astype(BF16), ones], axis=1)


def _prep_a(proj, *, name):
    L = proj.shape[0]
    tm = ROW_T

    def body(x_ref, qo, ko, vo):
        _split(x_ref[:, 0:512] * SCALE_A, HEADS, DH, qo)
        _split(x_ref[:, 512:1024], HEADS, DH, ko)
        _split_values(x_ref[:, 1024:1536], HEADS, vo)

    return pl.pallas_call(
        body, name=name, grid=(L // tm,),
        in_specs=[_col_spec(tm, 1536, QKVA)],
        out_specs=[_head_spec(HEADS, tm, DH)] * 2 + [_head_spec(HEADS, tm, 2 * DH)],
        out_shape=[jax.ShapeDtypeStruct((HEADS, L, DH), BF16)] * 2 + [jax.ShapeDtypeStruct((HEADS, L, 2 * DH), BF16)],
        compiler_params=_cparams(("parallel",)),
    )(proj)


def _unprep_a(dq, dk, dv, dproj, *, name):
    L = dq.shape[1]
    tm = ROW_T

    def body(dq_ref, dk_ref, dv_ref, _, dp_ref):
        dp_ref[:, 0:512] = (_join(dq_ref, HEADS) * SCALE_A).astype(BF16)
        dp_ref[:, 512:1024] = _join(dk_ref, HEADS).astype(BF16)
        dp_ref[:, 1024:1536] = _join(dv_ref, HEADS).astype(BF16)

    hs = _head_spec(HEADS, tm, DH)
    return pl.pallas_call(
        body, name=name, grid=(L // tm,),
        in_specs=[hs, hs, hs, pl.BlockSpec(memory_space=pl.ANY)],
        out_specs=_col_spec(tm, 1536, QKVA),
        out_shape=jax.ShapeDtypeStruct(dproj.shape, dproj.dtype),
        input_output_aliases={3: 0},
        compiler_params=_cparams(("parallel",)),
    )(dq, dk, dv, dproj)


def _prep_c(proj, tab, *, name):
    L = proj.shape[0]
    tm = ROW_T

    def body(x_ref, cos_ref, sin_ref, qo, ko, vo):
        cos, sin = cos_ref[...], sin_ref[...]
        q = _rope_lanes(x_ref[:, 0:512], _tile_lanes(cos, 512), _tile_lanes(sin, 512), DH // 2)
        _split(q * SCALE_C, HEADS, DH, qo)
        _split(_rope_lanes(x_ref[:, 512:640], cos, sin, DH // 2), SWA_KV_HEADS, DH, ko)
        _split_values(x_ref[:, 640:768], SWA_KV_HEADS, vo)

    t128 = pl.BlockSpec((tm, LANES), lambda i: (i, 0))
    return pl.pallas_call(
        body, name=name, grid=(L // tm,),
        in_specs=[_col_spec(tm, 768, QKVC), t128, t128],
        out_specs=[_head_spec(HEADS, tm, DH), _head_spec(SWA_KV_HEADS, tm, DH),
                   _head_spec(SWA_KV_HEADS, tm, 2 * DH)],
        out_shape=[jax.ShapeDtypeStruct((HEADS, L, DH), BF16), jax.ShapeDtypeStruct((SWA_KV_HEADS, L, DH), BF16),
                   jax.ShapeDtypeStruct((SWA_KV_HEADS, L, 2 * DH), BF16)],
        compiler_params=_cparams(("parallel",)),
    )(proj, *tab)


def _unprep_c(dq, dk, dv, tab, dproj, *, name):
    L = dq.shape[1]
    tm = ROW_T

    def body(dq_ref, dk_ref, dv_ref, cos_ref, sin_ref, _, dp_ref):
        cos, nsin = cos_ref[...], -sin_ref[...]
        dqv = _join(dq_ref, HEADS) * SCALE_C
        dp_ref[:, 0:512] = _rope_lanes(dqv, _tile_lanes(cos, 512), _tile_lanes(nsin, 512), DH // 2).astype(BF16)
        dp_ref[:, 512:640] = _rope_lanes(_join(dk_ref, SWA_KV_HEADS), cos, nsin, DH // 2).astype(BF16)
        dp_ref[:, 640:768] = _join(dv_ref, SWA_KV_HEADS).astype(BF16)

    hs = _head_spec(HEADS, tm, DH)
    hkv = _head_spec(SWA_KV_HEADS, tm, DH)
    t128 = pl.BlockSpec((tm, LANES), lambda i: (i, 0))
    return pl.pallas_call(
        body, name=name, grid=(L // tm,),
        in_specs=[hs, hkv, hkv, t128, t128, pl.BlockSpec(memory_space=pl.ANY)],
        out_specs=_col_spec(tm, 768, QKVC),
        out_shape=jax.ShapeDtypeStruct(dproj.shape, dproj.dtype),
        input_output_aliases={5: 0},
        compiler_params=_cparams(("parallel",)),
    )(dq, dk, dv, *tab, dproj)


def _q_tables(cos, sin, on_ref):
    on = on_ref[...] > 0.5
    width = on_ref.shape[1]
    return jnp.where(on, _tile_lanes(cos, width), 1.0), jnp.where(on, _tile_lanes(sin, width), 0.0)


def _prep_b(qbm, kvbm, proj, tab_k, q_rotary, *, name):
    L = proj.shape[0]
    tm = ROW_T

    def body(q_ref, kv_ref, kr_ref, ck_ref, sk_ref, on_ref, qo, ko, vo):
        cq, sq = _q_tables(ck_ref[...], sk_ref[...], on_ref)
        q = _rope_lanes(q_ref[...], cq, sq, MLA_ROPE // 2) * SCALE_B
        _split(q, HEADS, DQK_B, qo)
        kr = _rope_lanes(kr_ref[...], ck_ref[...], sk_ref[...], MLA_ROPE // 2)[:, :MLA_ROPE].astype(BF16)
        kv = kv_ref[...]
        ones = jnp.ones((tm, DH), BF16)
        for h in range(HEADS):
            ko[h] = jnp.concatenate([kv[:, 128 * h:128 * h + MLA_NOPE].astype(BF16), kr], axis=1)
            vo[h] = jnp.concatenate([kv[:, 128 * h + MLA_NOPE:128 * (h + 1)].astype(BF16), ones], axis=1)

    t128 = pl.BlockSpec((tm, LANES), lambda i: (i, 0))
    t768 = pl.BlockSpec((tm, 768), lambda i: (i, 0))
    return pl.pallas_call(
        body, name=name, grid=(L // tm,),
        in_specs=[t768, pl.BlockSpec((tm, 1024), lambda i: (i, 0)), _col_spec(tm, 128, KR), t128, t128,
                  pl.BlockSpec((1, 768), lambda i: (0, 0))],
        out_specs=[_head_spec(HEADS, tm, DQK_B), _head_spec(HEADS, tm, DQK_B), _head_spec(HEADS, tm, 2 * DH)],
        out_shape=[jax.ShapeDtypeStruct((HEADS, L, DQK_B), BF16), jax.ShapeDtypeStruct((HEADS, L, DQK_B), BF16),
                   jax.ShapeDtypeStruct((HEADS, L, 2 * DH), BF16)],
        compiler_params=_cparams(("parallel",)),
    )(qbm, kvbm, proj, *tab_k, q_rotary)


def _unprep_b(dq, dk, dv, tab_k, q_rotary, dproj, *, name):
    L = dq.shape[1]
    tm = ROW_T

    def body(dq_ref, dk_ref, dv_ref, ck_ref, sk_ref, on_ref, _, dp_kr, dqo, dkvo):
        cq, sq = _q_tables(ck_ref[...], sk_ref[...], on_ref)
        dqv = _join(dq_ref, HEADS) * SCALE_B
        dqo[...] = _rope_lanes(dqv, cq, -sq, MLA_ROPE // 2).astype(BF16)
        parts = []
        dkr = None
        for h in range(HEADS):
            dkh = dk_ref[h]
            parts += [dkh[:, :MLA_NOPE], dv_ref[h]]
            r = dkh[:, MLA_NOPE:]
            dkr = r if dkr is None else dkr + r
        dkvo[...] = jnp.concatenate(parts, axis=1).astype(BF16)
        dkr = jnp.concatenate([dkr, jnp.zeros((tm, LANES - MLA_ROPE), F32)], axis=1)
        dp_kr[...] = _rope_lanes(dkr, ck_ref[...], -sk_ref[...], MLA_ROPE // 2).astype(BF16)

    t128 = pl.BlockSpec((tm, LANES), lambda i: (i, 0))
    t768 = pl.BlockSpec((tm, 768), lambda i: (i, 0))
    hq = _head_spec(HEADS, tm, DQK_B)
    return pl.pallas_call(
        body, name=name, grid=(L // tm,),
        in_specs=[hq, hq, _head_spec(HEADS, tm, DH), t128, t128, pl.BlockSpec((1, 768), lambda i: (0, 0)),
                  pl.BlockSpec(memory_space=pl.ANY)],
        out_specs=[_col_spec(tm, 128, KR), t768, pl.BlockSpec((tm, 1024), lambda i: (i, 0))],
        out_shape=[jax.ShapeDtypeStruct(dproj.shape, dproj.dtype), jax.ShapeDtypeStruct((L, 768), BF16),
                   jax.ShapeDtypeStruct((L, 1024), BF16)],
        input_output_aliases={6: 0},
        compiler_params=_cparams(("parallel",)),
    )(dq, dk, dv, *tab_k, q_rotary, dproj)


def _gate(y, proj, zcol, *, name):
    L = proj.shape[0]
    tm = ROW_T

    def body(y_ref, z_ref, u_ref):
        z = z_ref[...]
        u_ref[...] = (_join(y_ref, HEADS) * (z * jax.nn.sigmoid(z))).astype(BF16)

    return pl.pallas_call(
        body, name=name, grid=(L // tm,),
        in_specs=[_head_spec(HEADS, tm, DH), _col_spec(tm, 512, zcol)],
        out_specs=pl.BlockSpec((tm, 512), lambda i: (i, 0)),
        out_shape=jax.ShapeDtypeStruct((L, 512), BF16),
        compiler_params=_cparams(("parallel",)),
    )(y, proj)


def _gate_bwd(du, y, proj, zcol, dproj, *, delta_rows=False, name):
    L = proj.shape[0]
    tm = ROW_T
    assert not delta_rows or tm == ATT_T

    def body(du_ref, y_ref, z_ref, _, dz_ref, dy_ref, *dl_ref):
        z = z_ref[...]
        duv = du_ref[...]
        sg = jax.nn.sigmoid(z)
        yv = _join(y_ref, HEADS)
        dyv = duv * (z * sg)
        _split(dyv, HEADS, DH, dy_ref)
        dz_ref[...] = (duv * yv * (sg * (1.0 + z * (1.0 - sg)))).astype(BF16)
        if delta_rows:
            prod = dyv * yv
            ones = jnp.ones((8, DH), F32)
            for h in range(HEADS):
                sums = lax.dot_general(ones, prod[:, DH * h:DH * (h + 1)], _NT, preferred_element_type=F32)
                dl_ref[0][h, 0] = sums[0:1, :]

    hs = _head_spec(HEADS, tm, DH)
    out_specs = [_col_spec(tm, 512, zcol), hs]
    out_shape = [jax.ShapeDtypeStruct(dproj.shape, dproj.dtype),
                 jax.ShapeDtypeStruct((HEADS, L, DH), BF16 if delta_rows else F32)]
    if delta_rows:
        out_specs.append(pl.BlockSpec((HEADS, 1, 1, tm), lambda i: (0, i, 0, 0)))
        out_shape.append(jax.ShapeDtypeStruct((HEADS, L // tm, 1, tm), F32))
    return pl.pallas_call(
        body, name=name, grid=(L // tm,),
        in_specs=[pl.BlockSpec((tm, 512), lambda i: (i, 0)), hs, _col_spec(tm, 512, zcol),
                  pl.BlockSpec(memory_space=pl.ANY)],
        out_specs=out_specs, out_shape=out_shape,
        input_output_aliases={3: 0},
        compiler_params=_cparams(("parallel",)),
    )(du, y, proj, dproj)


MERGE_T = 192


def _branch_merge(us, w_branch, proj, *, name):
    L = proj.shape[0]
    tm = ROW_T

    def body(u0, u1, u2, w_ref, g_ref, m_ref, pb_ref):
        acc = None
        for n, u_ref in enumerate((u0, u1, u2)):
            pb = jnp.dot(u_ref[...], w_ref[n], preferred_element_type=F32)
            pb_ref[n] = pb.astype(BF16)
            t = jax.nn.sigmoid(g_ref[:, n * D_MODEL:(n + 1) * D_MODEL]) * pb
            acc = t if acc is None else acc + t
        m_ref[...] = acc.astype(BF16)

    urow = pl.BlockSpec((tm, BRANCH_W), lambda i: (i, 0))
    return pl.pallas_call(
        body, name=name, grid=(L // tm,),
        in_specs=[urow] * N_BRANCH + [pl.BlockSpec((N_BRANCH, BRANCH_W, D_MODEL), lambda i: (0, 0, 0)),
                                      _col_spec(tm, N_BRANCH * D_MODEL, GATES)],
        out_specs=[pl.BlockSpec((tm, D_MODEL), lambda i: (i, 0)),
                   pl.BlockSpec((N_BRANCH, tm, D_MODEL), lambda i: (0, i, 0))],
        out_shape=[jax.ShapeDtypeStruct((L, D_MODEL), BF16), jax.ShapeDtypeStruct((N_BRANCH, L, D_MODEL), BF16)],
        compiler_params=_cparams(("parallel",)),
    )(*us, w_branch, proj)


def _merge_bwd(dmerged, proj, pb, dproj, *, name):
    L = proj.shape[0]
    tm = MERGE_T

    def body(dm_ref, g_ref, p_ref, _, dg_ref, dp0, dp1, dp2):
        dm = dm_ref[...]
        for n, dp_ref in enumerate((dp0, dp1, dp2)):
            cols = slice(n * D_MODEL, (n + 1) * D_MODEL)
            sg = jax.nn.sigmoid(g_ref[:, cols])
            dp_ref[...] = (dm * sg).astype(BF16)
            dg_ref[:, cols] = (dm * p_ref[n].astype(F32) * (sg * (1.0 - sg))).astype(BF16)

    row = pl.BlockSpec((tm, D_MODEL), lambda i: (i, 0))
    gates = _col_spec(tm, N_BRANCH * D_MODEL, GATES)
    outs = pl.pallas_call(
        body, name=name, grid=(L // tm,),
        in_specs=[row, gates, pl.BlockSpec((N_BRANCH, tm, D_MODEL), lambda i: (0, i, 0)),
                  pl.BlockSpec(memory_space=pl.ANY)],
        out_specs=[gates] + [row] * N_BRANCH,
        out_shape=[jax.ShapeDtypeStruct(dproj.shape, dproj.dtype)]
        + [jax.ShapeDtypeStruct((L, D_MODEL), BF16)] * N_BRANCH,
        input_output_aliases={3: 0},
        compiler_params=_cparams(("parallel",)),
    )(dmerged, proj, pb, dproj)
    return outs[0], outs[1:]


def _scan_rows(x, reverse):
    rows = lax.broadcasted_iota(jnp.int32, x.shape, 0)
    step = 1
    while step < BLK:
        if reverse:
            x = x + jnp.where(rows < BLK - step, pltpu.roll(x, BLK - step, 0), 0.0)
        else:
            x = x + jnp.where(rows >= step, pltpu.roll(x, step, 0), 0.0)
        step *= 2
    return x


def _forget_fwd(proj, b_f, *, name):
    L = proj.shape[0]
    nb = L // BLK

    def body(x_ref, b_ref, o_ref, carry):
        i = pl.program_id(0)

        @pl.when(i == 0)
        def _():
            carry[...] = jnp.zeros_like(carry)

        c = _scan_rows(jax.nn.log_sigmoid(x_ref[...] + b_ref[...]), False) + carry[...]
        carry[...] = c[BLK - 1:BLK, :]
        pos = i * BLK + lax.broadcasted_iota(jnp.int32, (HEADS, BLK), 1)
        o_ref[...] = c.T[:HEADS, :] + jnp.where(pos < PAD, BIG, 0.0)

    return pl.pallas_call(
        body, name=name, grid=(nb,),
        in_specs=[_col_spec(BLK, LANES, FA), pl.BlockSpec((1, LANES), lambda i: (0, 0))],
        out_specs=pl.BlockSpec((HEADS, BLK), lambda i: (0, i)),
        out_shape=jax.ShapeDtypeStruct((HEADS, L), F32),
        scratch_shapes=[pltpu.VMEM((1, LANES), F32)],
        compiler_params=_cparams(("arbitrary",)),
    )(proj, jnp.pad(b_f, (0, LANES - HEADS)).reshape(1, LANES))


def _forget_bwd(dct, proj, b_f, dproj, *, name):
    L = proj.shape[0]
    nb = L // BLK

    def body(d_ref, x_ref, b_ref, _, dp_ref, db_ref, carry):
        i = pl.program_id(0)

        @pl.when(i == 0)
        def _():
            carry[...] = jnp.zeros_like(carry)
            db_ref[...] = jnp.zeros_like(db_ref)

        d = jnp.concatenate([d_ref[...], jnp.zeros((BLK - HEADS, BLK), F32)], axis=0).T
        dlog = _scan_rows(d, True) + carry[...]
        carry[...] = dlog[0:1, :]
        lane = lax.broadcasted_iota(jnp.int32, (BLK, LANES), 1)
        daf = jnp.where(lane < HEADS, dlog * jax.nn.sigmoid(-(x_ref[...] + b_ref[...])), 0.0)
        db_ref[...] += jnp.sum(daf, axis=0, keepdims=True)
        dp_ref[...] = jnp.concatenate([daf, jnp.zeros((BLK, LANES), F32)], axis=1).astype(BF16)

    back = lambda i: nb - 1 - i
    dp, db = pl.pallas_call(
        body, name=name, grid=(nb,),
        in_specs=[pl.BlockSpec((HEADS, BLK), lambda i: (0, back(i))),
                  pl.BlockSpec((BLK, LANES), lambda i: (back(i), FA // LANES)),
                  pl.BlockSpec((1, LANES), lambda i: (0, 0)), pl.BlockSpec(memory_space=pl.ANY)],
        out_specs=[pl.BlockSpec((BLK, 2 * LANES), lambda i: (back(i), FA // (2 * LANES))),
                   pl.BlockSpec((1, LANES), lambda i: (0, 0))],
        out_shape=[jax.ShapeDtypeStruct(dproj.shape, dproj.dtype), jax.ShapeDtypeStruct((1, LANES), F32)],
        scratch_shapes=[pltpu.VMEM((1, LANES), F32)],
        input_output_aliases={3: 0},
        compiler_params=_cparams(("arbitrary",)),
    )(dct, proj, jnp.pad(b_f, (0, LANES - HEADS)).reshape(1, LANES), dproj)
    return dp, db[0, :HEADS]


def _key_bias(L, T, ct=None):
    padb = jnp.where(jnp.arange(L) < PAD, BIG, 0.0).astype(F32)[None]
    kb = padb if ct is None else ct + padb
    return kb.reshape(kb.shape[0], L // T, 1, T)


def _layer_fwd(h, w, tabs, l, side=None):
    tag = f"l{l}"
    L = h.shape[0]
    hn = _rms_fwd(h, w["norm_g"], name=f"{tag}_rms_in")
    if "late" in w:
        late_job, assemble = w["late"]
        proj, *late = _mm(hn, w["w_in"], side=late_job, name=f"{tag}_mm_in")
        w = {**{key: val for key, val in w.items() if key != "late"}, **assemble(*late)}
    else:
        proj = _mm(hn, w["w_in"], name=f"{tag}_mm_in")
    kb_a = _forget_fwd(proj, w["b_f"], name=f"{tag}_forget").reshape(HEADS, L // ATT_T, 1, ATT_T)
    ops_a = (*_prep_a(proj, name=f"{tag}_prep_a"), kb_a)
    ya, lsea, *side_out = _attn_fwd(*ops_a, T=ATT_T, side=side, name=f"{tag}_attn_a")
    cqn = _rms_fwd(proj, w["g_cq"], col=CQ, name=f"{tag}_rms_cq")
    ckvn = _rms_fwd(proj, w["g_ckv"], col=CKV, name=f"{tag}_rms_ckv")
    qbm = _mm(cqn, w["w_uq"], name=f"{tag}_mm_uq")
    kvbm = _mm(ckvn, w["w_ukv"], name=f"{tag}_mm_ukv")
    ops_b = (*_prep_b(qbm, kvbm, proj, tabs["bk"], tabs["bq"], name=f"{tag}_prep_b"), _key_bias(L, ATT_T))
    yb, lseb = _attn_fwd(*ops_b, T=ATT_T, name=f"{tag}_attn_b")
    ops_c = (*_prep_c(proj, tabs["c"], name=f"{tag}_prep_c"), _key_bias(L, WINDOW))
    sink = jnp.broadcast_to(w["sinks"][:, None, None], (HEADS, 1, LANES))
    yc, lsec = _attn_fwd(*ops_c, T=WINDOW, window=True, sink=sink, name=f"{tag}_attn_c")
    us = [_gate(y, proj, zcol, name=f"{tag}_gate{n}") for n, (y, zcol) in enumerate(((ya, ZA), (yb, ZB), (yc, ZC)))]
    merged, pbr = _branch_merge(us, w["w_branch"], proj, name=f"{tag}_branch_merge")
    out = _mm(merged, w["w_out"], add=h, name=f"{tag}_mm_out")
    saved = dict(h=h, hn=hn, proj=proj, ops_a=ops_a, ya=ya, lsea=lsea, cqn=cqn, ckvn=ckvn,
                 ops_b=ops_b, yb=yb, lseb=lseb, ops_c=ops_c, yc=yc, lsec=lsec, us=us, pbr=pbr, merged=merged)
    return out, saved, side_out, w


def _w_in_chunks(g, tag):
    return _w_in_to_shards(g["w_in"], name=f"{tag}_w_in_chunks")


def _grad_chunks(g, tag, with_w_in=True):
    br = _cut_shards(g["w_branch"].reshape(N_BRANCH * BRANCH_W, D_MODEL), name=f"{tag}_w_branch_chunks")
    rest = [_cut_shards(g["w_uq"], name=f"{tag}_w_uq_chunks"), _cut_shards(g["w_ukv"], name=f"{tag}_w_ukv_chunks"),
            br.reshape(N_DEV, N_BRANCH, BRANCH_W, D_MODEL // N_DEV),
            g["w_out"].reshape(N_DEV, D_MODEL // N_DEV, D_MODEL)]
    return ([_w_in_chunks(g, tag)] if with_w_in else []) + rest


def _layer_bwd(dout, s, w, tabs, l, side=None, own_exchange=False):
    tag = f"l{l}"
    L = dout.shape[0]
    proj = s["proj"]
    g = {}
    dproj = jnp.zeros((L, NP_IN), BF16)
    dmerged = _mm(dout, w["w_out"], tb=True, name=f"{tag}_mm_out_dx")
    g["w_out"] = _mm(s["merged"], dout, ta=True, out_dtype=BF16, name=f"{tag}_mm_out_dw")
    dproj, dpbr = _merge_bwd(dmerged, proj, s["pbr"], dproj, name=f"{tag}_merge_bwd")
    dus = [_mm(dpbr[n], w["w_branch"][n], tb=True, name=f"{tag}_mm_br{n}_dx") for n in range(N_BRANCH)]
    g["w_branch"] = jnp.stack([_mm(s["us"][n], dpbr[n], ta=True, out_dtype=BF16, name=f"{tag}_mm_br{n}_dw")
                               for n in range(N_BRANCH)])
    dproj, dya, dla = _gate_bwd(dus[0], s["ya"], proj, ZA, dproj, delta_rows=True, name=f"{tag}_gate0_bwd")
    dproj, dyb, dlb = _gate_bwd(dus[1], s["yb"], proj, ZB, dproj, delta_rows=True, name=f"{tag}_gate1_bwd")
    dproj, dyc = _gate_bwd(dus[2], s["yc"], proj, ZC, dproj, name=f"{tag}_gate2_bwd")

    def bwd_operands(ops, lse, delta, dy16):
        q16, k16, v16, kbias = ops
        return (q16, k16, v16, kbias.reshape(kbias.shape[0], L, 1), lse.reshape(HEADS, L // ATT_T, 1, ATT_T), delta,
                dy16)

    dqa, dka, dva, dcq, dck, *side_out = _attn_bwd_t(*bwd_operands(s["ops_a"], s["lsea"], dla, dya), T=ATT_T,
                                                     fox=True, side=side, name=f"{tag}_attn_a_bwd")
    dproj = _unprep_a(dqa, dka, dva, dproj, name=f"{tag}_unprep_a")
    dproj, g["b_f"] = _forget_bwd(dcq.reshape(HEADS, L) + dck[:, :, 0], proj, w["b_f"], dproj,
                                  name=f"{tag}_forget_bwd")
    dqb, dkb, dvb, _, _ = _attn_bwd_t(*bwd_operands(s["ops_b"], s["lseb"], dlb, dyb), T=ATT_T,
                                      name=f"{tag}_attn_b_bwd")
    dproj, dqbm, dkvbm = _unprep_b(dqb, dkb, dvb, tabs["bk"], tabs["bq"], dproj, name=f"{tag}_unprep_b")
    dcqn = _mm(dqbm, w["w_uq"], tb=True, name=f"{tag}_mm_uq_dx")
    g["w_uq"] = _mm(s["cqn"], dqbm, ta=True, out_dtype=BF16, name=f"{tag}_mm_uq_dw")
    dckvn = _mm(dkvbm, w["w_ukv"], tb=True, name=f"{tag}_mm_ukv_dx")
    g["w_ukv"] = _mm(s["ckvn"], dkvbm, ta=True, out_dtype=BF16, name=f"{tag}_mm_ukv_dw")
    dproj, g["g_cq"] = _rms_bwd(dcqn, proj, w["g_cq"], col=CQ, into=dproj, name=f"{tag}_rms_cq_bwd")
    dproj, g["g_ckv"] = _rms_bwd(dckvn, proj, w["g_ckv"], col=CKV, into=dproj, name=f"{tag}_rms_ckv_bwd")
    dqc, dkc, dvc = _attn_bwd_window(*s["ops_c"], s["yc"], s["lsec"], dyc, name=f"{tag}_attn_c_bwd")
    dproj = _unprep_c(dqc, dkc, dvc, tabs["c"], dproj, name=f"{tag}_unprep_c")
    delta_c = jnp.sum(dyc * s["yc"], axis=-1)
    g["sinks"] = -jnp.sum(jnp.exp(w["sinks"][:, None] - s["lsec"][:, :, 0]) * delta_c, axis=1)
    own_out = []
    if own_exchange:
        g["w_in"], *r_rest = _mm(s["hn"].T, dproj, out_dtype=BF16, side=_Exchange(_grad_chunks(g, tag, False)),
                                 name=f"{tag}_mm_in_dw")
        dhn, r_in = _mm(dproj, w["w_in"], tb=True, side=_Exchange([_w_in_chunks(g, tag)]), name=f"{tag}_mm_in_dx")
        own_out = [r_in, *r_rest]
    else:
        g["w_in"] = _mm(s["hn"].T, dproj, out_dtype=BF16, name=f"{tag}_mm_in_dw")
        dhn = _mm(dproj, w["w_in"], tb=True, name=f"{tag}_mm_in_dx")
    dh, g["norm_g"] = _rms_bwd(dhn, s["h"], w["norm_g"], add=dout, name=f"{tag}_rms_in_bwd")
    return dh, g, side_out, own_out


def _cols_from_shards(g):
    return jnp.moveaxis(g, 0, 1).reshape(g.shape[1], N_DEV * g.shape[2])


def _join_shards(g, *, name):
    _, R, C = g.shape
    tr = _pick(R, (512, 384, 256))

    def body(g_ref, o_ref):
        for d in range(N_DEV):
            o_ref[:, C * d:C * (d + 1)] = g_ref[d]

    return pl.pallas_call(
        body, name=name, grid=(R // tr,),
        in_specs=[pl.BlockSpec((N_DEV, tr, C), lambda i: (0, i, 0))],
        out_specs=pl.BlockSpec((tr, N_DEV * C), lambda i: (i, 0)),
        out_shape=jax.ShapeDtypeStruct((R, N_DEV * C), g.dtype),
        compiler_params=_cparams(("parallel",)),
    )(g)


def _cut_shards(w, *, name):
    R = w.shape[0]
    C = w.shape[1] // N_DEV
    tr = _pick(R, (512, 384, 256))

    def body(w_ref, o_ref):
        for d in range(N_DEV):
            o_ref[d] = w_ref[:, C * d:C * (d + 1)]

    return pl.pallas_call(
        body, name=name, grid=(R // tr,),
        in_specs=[pl.BlockSpec((tr, N_DEV * C), lambda i: (i, 0))],
        out_specs=pl.BlockSpec((N_DEV, tr, C), lambda i: (0, i, 0)),
        out_shape=jax.ShapeDtypeStruct((N_DEV, R, C), w.dtype),
        compiler_params=_cparams(("parallel",)),
    )(w)


W_IN_SHARD = N_IN // N_DEV


def _w_in_pieces():
    pieces = []
    for lo, hi, dst in _RUNS:
        for d in range(N_DEV):
            a, b = max(lo, d * W_IN_SHARD), min(hi, (d + 1) * W_IN_SHARD)
            if a < b:
                pieces.append((d, a - d * W_IN_SHARD, b - d * W_IN_SHARD, dst + a - lo))
    return pieces


def _w_in_from_shards(g, *, name):
    R = g.shape[1]
    tr = 256
    covered = sorted((dst, dst + b - a) for _, a, b, dst in _w_in_pieces())

    def body(g_ref, o_ref):
        at = 0
        for lo, hi in covered + [(NP_IN, NP_IN)]:
            if lo > at:
                o_ref[:, at:lo] = jnp.zeros((tr, lo - at), BF16)
            at = max(at, hi)
        for d, a, b, dst in _w_in_pieces():
            o_ref[:, dst:dst + b - a] = g_ref[d, :, a:b]

    return pl.pallas_call(
        body, name=name, grid=(R // tr,),
        in_specs=[pl.BlockSpec((N_DEV, tr, W_IN_SHARD), lambda i: (0, i, 0))],
        out_specs=pl.BlockSpec((tr, NP_IN), lambda i: (i, 0)),
        out_shape=jax.ShapeDtypeStruct((R, NP_IN), BF16),
        compiler_params=_cparams(("parallel",)),
    )(g)


def _w_in_to_shards(dw, *, name):
    R = dw.shape[0]
    tr = 256

    def body(w_ref, o_ref):
        for d, a, b, dst in _w_in_pieces():
            o_ref[d, :, a:b] = w_ref[:, dst:dst + b - a]

    return pl.pallas_call(
        body, name=name, grid=(R // tr,),
        in_specs=[pl.BlockSpec((tr, NP_IN), lambda i: (i, 0))],
        out_specs=pl.BlockSpec((N_DEV, tr, W_IN_SHARD), lambda i: (0, i, 0)),
        out_shape=jax.ShapeDtypeStruct((N_DEV, R, W_IN_SHARD), BF16),
        compiler_params=_cparams(("parallel",)),
    )(dw)


_SMALL = (("norm_g", DEPTH * D_MODEL), ("b_f", DEPTH * HEADS), ("g_cq", DEPTH * MLA_QLORA),
          ("g_ckv", DEPTH * MLA_KVLORA), ("sinks", DEPTH * HEADS), ("final_g", D_MODEL), ("loss", 1),
          ("meta", N_META * D_MODEL))
SMALL_ROWS = 168


def _pack_small(d):
    parts = []
    for name, size in _SMALL:
        padded = -(-size // 128) * 128
        v = d[name].reshape(-1).astype(F32) if name in d else jnp.zeros((size,), F32)
        parts.append(jnp.pad(v, (0, padded - size)))
    flat = jnp.concatenate(parts)
    return jnp.pad(flat, (0, SMALL_ROWS * 128 - flat.shape[0])).reshape(SMALL_ROWS, 128)


def _unpack_small(p, shapes):
    flat = p.reshape(-1)
    out, at = {}, 0
    for name, size in _SMALL:
        if name in shapes:
            out[name] = flat[at:at + size].reshape(shapes[name])
        at += -(-size // 128) * 128
    return out


def kernel(x, meta_tokens, norm_g, w_in, b_f, g_cq, g_ckv, w_uq, w_ukv, sinks, w_branch, w_out, final_g, loss_target, m_meta_tokens, m_norm_g, m_w_in, m_b_f, m_g_cq, m_g_ckv, m_w_uq, m_w_ukv, m_sinks, m_w_branch, m_w_out, m_final_g, v_meta_tokens, v_norm_g, v_w_in, v_b_f, v_g_cq, v_g_ckv, v_w_uq, v_w_ukv, v_sinks, v_w_branch, v_w_out, v_final_g):
    S = x.shape[1]
    L = BLK + S
    cx, cy, cc = _me()
    my_idx = 4 * cx + 2 * cy + cc

    def shards(l):
        return [t[l].astype(BF16) for t in (w_in, w_uq, w_ukv, w_branch, w_out)]

    def small_weights(l):
        return dict(norm_g=norm_g[l], b_f=b_f[l], g_cq=g_cq[l], g_ckv=g_ckv[l], sinks=sinks[l])

    def rest_weights(l):
        def assemble(gw_uq, gw_ukv, gw_br, gw_out):
            br = _join_shards(gw_br.reshape(N_DEV, N_BRANCH * BRANCH_W, D_MODEL // N_DEV), name=f"l{l}_w_branch_full")
            return dict(w_uq=_join_shards(gw_uq, name=f"l{l}_w_uq_full"),
                        w_ukv=_join_shards(gw_ukv, name=f"l{l}_w_ukv_full"),
                        w_branch=br.reshape(N_BRANCH, BRANCH_W, D_MODEL), w_out=gw_out.reshape(D_MODEL, D_MODEL))
        return assemble

    def layer_weights(l, gw_in, *gw_rest):
        return dict(small_weights(l), w_in=_w_in_from_shards(gw_in, name=f"l{l}_w_in_full"),
                    **rest_weights(l)(*gw_rest))

    gw_in0, g_meta = _comm(_Gather([shards(0)[0], meta_tokens]), name="gather_l0")
    layers = [dict(small_weights(0), w_in=_w_in_from_shards(gw_in0, name="l0_w_in_full"),
                   late=(_Gather(shards(0)[1:]), rest_weights(0))), None]
    meta_full = _cols_from_shards(g_meta)

    h = jnp.concatenate([jnp.zeros((PAD, D_MODEL), F32), meta_full, x[0]], axis=0)
    q_rotary = ((jnp.arange(HEADS * DQK_B) % DQK_B) >= MLA_NOPE).astype(F32)[None, :]
    tabs = dict(c=_rope_tables(L, DH // 2, LANES), bk=_rope_tables(L, MLA_ROPE // 2, LANES), bq=q_rotary)
    saved = [None] * DEPTH
    h, saved[0], gw1, layers[0] = _layer_fwd(h, layers[0], tabs, 0, side=_Gather(shards(1)))
    layers[1] = layer_weights(1, *gw1)
    h, saved[1], _, _ = _layer_fwd(h, layers[1], tabs, 1)
    loss_vec, dh, g_final = _loss_head(h, final_g, loss_target[0], name="loss_head")

    grads = [None] * DEPTH
    dh, grads[1], _, _ = _layer_bwd(dh, saved[1], layers[1], tabs, 1)
    dh, grads[0], recv1, recv0 = _layer_bwd(dh, saved[0], layers[0], tabs, 0, side=_Exchange(_grad_chunks(grads[1], "l1")),
                                            own_exchange=True)
    r_in, r_uq, r_ukv, r_br, r_out = (jnp.stack([a, b], axis=1) for a, b in zip(recv0, recv1))

    def stack(name):
        return jnp.stack([grads[l][name] for l in range(DEPTH)])

    small = _pack_small(dict(norm_g=stack("norm_g"), b_f=stack("b_f"), g_cq=stack("g_cq"), g_ckv=stack("g_ckv"),
                             sinks=stack("sinks"), final_g=g_final, loss=loss_vec[0, 0:1],
                             meta=dh[PAD:BLK]))
    (g_small,) = _comm(_Gather([small]), name="gather_small")

    def adam_big(w_, parts, m_, v_, name):
        shape = w_.shape
        C = shape[-1]
        R = math.prod(shape[:-1])
        outs = _adamw(w_.reshape(R, C), parts.reshape(parts.shape[0], R, C), m_.reshape(R, C), v_.reshape(R, C),
                      name=name)
        return [o.reshape(shape) for o in outs]

    res = {}
    res["w_in"] = adam_big(w_in, r_in, m_w_in, v_w_in, "adam_w_in")
    res["w_uq"] = adam_big(w_uq, r_uq, m_w_uq, v_w_uq, "adam_w_uq")
    res["w_ukv"] = adam_big(w_ukv, r_ukv, m_w_ukv, v_w_ukv, "adam_w_ukv")
    res["w_branch"] = adam_big(w_branch, r_br, m_w_branch, v_w_branch, "adam_w_branch")
    res["w_out"] = adam_big(w_out, r_out, m_w_out, v_w_out, "adam_w_out")

    small_w = dict(norm_g=norm_g, b_f=b_f, g_cq=g_cq, g_ckv=g_ckv, sinks=sinks, final_g=final_g)
    small_m = dict(norm_g=m_norm_g, b_f=m_b_f, g_cq=m_g_cq, g_ckv=m_g_ckv, sinks=m_sinks, final_g=m_final_g)
    small_v = dict(norm_g=v_norm_g, b_f=v_b_f, g_cq=v_g_cq, g_ckv=v_g_ckv, sinks=v_sinks, final_g=v_final_g)
    sm = _adamw(_pack_small(small_w), g_small, _pack_small(small_m), _pack_small(small_v), name="adam_small")
    shapes = {k: a.shape for k, a in small_w.items()}
    shapes_all = dict(shapes, loss=(), meta=(N_META, D_MODEL))
    sm_g = _unpack_small(sm[0], shapes_all)
    sm_d, sm_m, sm_v = (_unpack_small(t, shapes) for t in sm[1:])
    for k in shapes:
        res[k] = [sm_g[k], sm_d[k], sm_m[k], sm_v[k]]
    g_meta_mine = lax.dynamic_slice(sm_g["meta"], (0, my_idx * 128), (N_META, 128))
    res["meta_tokens"] = _adamw(meta_tokens, g_meta_mine[None], m_meta_tokens, v_meta_tokens, name="adam_meta")

    order = ["meta_tokens", "norm_g", "w_in", "b_f", "g_cq", "g_ckv", "w_uq", "w_ukv", "sinks", "w_branch", "w_out",
             "final_g"]
    grad_x = dh[BLK:][None]
    return (sm_g["loss"], grad_x, *[res[k][0] for k in order], *[res[k][1] for k in order],
            *[res[k][2] for k in order], *[res[k][3] for k in order])
```

```python
import math

import jax
import jax.numpy as jnp
from jax import lax
from jax.experimental import pallas as pl
from jax.experimental.pallas import tpu as pltpu

F32 = jnp.float32
BF16 = jnp.bfloat16

D_MODEL = 1024
DEPTH = 2
N_META = 16
BLK = 128
PAD = BLK - N_META
ROPE_THETA = 10000.0
EPS = 1e-6
NEG = -1e30
BIG = 1e30
HEADS = 8
DH = 64
MLA_NOPE = 64
MLA_ROPE = 32
MLA_QLORA = 384
MLA_KVLORA = 256
SWA_KV_HEADS = 2
WINDOW = 128
BRANCH_W = 512
N_BRANCH = 3
N_IN = 7592

ADAM_LR = 0.001
ADAM_B1 = 0.9
ADAM_B2 = 0.999
ADAM_EPS = 1e-08
ADAM_WD = 0.01
ADAM_STEP = 10

N_DEV = 8
MESH = pl.DeviceIdType.MESH

NP_IN = 8192
QKVA, ZA, FA = 0, 1536, 2048
CKV, KR, CQ = 2304, 2560, 2688
GATES = 3072
ZB = 6144
QKVC, ZC = 6912, 7680
_RUNS = ((0, 1536, QKVA), (1536, 1544, FA), (1544, 2056, ZA), (2056, 2440, CQ), (2440, 2696, CKV), (2696, 2728, KR),
         (2728, 3240, ZB), (3240, 4008, QKVC), (4008, 4520, ZC), (4520, 7592, GATES))

VMEM_LIMIT = 48 * 1024 * 1024
ATT_T = 384
ATT_HP = 1
ROW_T = 384
LANES = 128


def _pick(dim, prefs):
    for p in prefs:
        if dim % p == 0:
            return p
    return dim


def _cparams(sem):
    return pltpu.CompilerParams(dimension_semantics=sem, vmem_limit_bytes=VMEM_LIMIT)


def _side_cparams(side, sem):
    if side is None:
        return _cparams(sem)
    return pltpu.CompilerParams(dimension_semantics=("arbitrary",) * len(sem), vmem_limit_bytes=VMEM_LIMIT,
                                has_side_effects=True)


def _mm(a, b, *, ta=False, tb=False, add=None, out_dtype=F32, side=None, name):
    M = a.shape[1] if ta else a.shape[0]
    K = a.shape[0] if ta else a.shape[1]
    N = b.shape[0] if tb else b.shape[1]
    assert K == (b.shape[1] if tb else b.shape[0])
    tm = _pick(M, (704, 1024, 512, 384, 256, 128))
    tn = _pick(N, (1024, 768, 512, 384, 256, 128))
    tk = _pick(K, (4096, 1408, 1024, 768, 512, 384, 256, 128))
    nk = K // tk
    dims = (((0 if ta else 1,), (1 if tb else 0,)), ((), ()))

    sn = 0 if side is None else side.n
    n_in = 2 + (add is not None)
    grid = (M // tm, N // tn, nk)

    def body(*refs):
        a_ref, b_ref = refs[:2]
        c_ref = refs[2] if add is not None else None
        o_ref = refs[n_in + sn]
        scratch = refs[n_in + 2 * sn + 1:]
        if side is not None:
            step = (pl.program_id(0) * grid[1] + pl.program_id(1)) * nk + pl.program_id(2)
            _ride(side, [*refs[n_in:n_in + sn], *refs[n_in + sn + 1:n_in + 2 * sn + 1], *scratch[nk > 1:]],
                  step, grid[0] * grid[1] * nk)
        r = lax.dot_general(a_ref[...].astype(BF16), b_ref[...].astype(BF16), dims, preferred_element_type=F32)

        def finish(total):
            if c_ref is not None:
                total = total + c_ref[...]
            o_ref[...] = total.astype(out_dtype)

        if nk == 1:
            finish(r)
        else:
            acc = scratch[0]
            k = pl.program_id(2)

            @pl.when(k == 0)
            def _():
                acc[...] = r

            @pl.when(k > 0)
            def _():
                acc[...] += r

            @pl.when(k == nk - 1)
            def _():
                finish(acc[...])

    a_spec = pl.BlockSpec((tk, tm), lambda i, j, k: (k, i)) if ta else pl.BlockSpec((tm, tk), lambda i, j, k: (i, k))
    b_spec = pl.BlockSpec((tn, tk), lambda i, j, k: (j, k)) if tb else pl.BlockSpec((tk, tn), lambda i, j, k: (k, j))
    o_spec = pl.BlockSpec((tm, tn), lambda i, j, k: (i, j))
    any_spec = pl.BlockSpec(memory_space=pl.ANY)
    outs = pl.pallas_call(
        body, name=name,
        grid=grid,
        in_specs=[a_spec, b_spec] + ([o_spec] if add is not None else []) + [any_spec] * sn,
        out_specs=[o_spec] + [any_spec] * sn,
        out_shape=[jax.ShapeDtypeStruct((M, N), out_dtype)] + ([] if side is None else side.out_shape),
        scratch_shapes=([pltpu.VMEM((tm, tn), F32)] if nk > 1 else []) + ([] if side is None else side.scratch),
        compiler_params=_side_cparams(side, ("parallel", "parallel", "arbitrary")),
    )(*((a, b) if add is None else (a, b, add)), *([] if side is None else side.arrs))
    return outs[0] if side is None else outs


def _col_spec(tm, width, col):
    assert col % width == 0
    return pl.BlockSpec((tm, width), lambda i, _c=col // width: (i, _c))


def _rms_fwd(x, g, *, col=0, name):
    L = x.shape[0]
    D = g.shape[0]
    tm = ROW_T

    def body(x_ref, g_ref, y_ref):
        xv = x_ref[...]
        rstd = lax.rsqrt(jnp.mean(xv * xv, axis=-1, keepdims=True) + EPS)
        y_ref[...] = (xv * rstd * g_ref[...]).astype(BF16)

    return pl.pallas_call(
        body, name=name, grid=(L // tm,),
        in_specs=[_col_spec(tm, D, col), pl.BlockSpec((1, D), lambda i: (0, 0))],
        out_specs=pl.BlockSpec((tm, D), lambda i: (i, 0)),
        out_shape=jax.ShapeDtypeStruct((L, D), BF16),
        compiler_params=_cparams(("parallel",)),
    )(x, g.reshape(1, D))


def _rms_bwd(dy, x, g, *, col=0, add=None, into=None, name):
    L = x.shape[0]
    D = g.shape[0]
    tm = ROW_T

    def body(*refs):
        dy_ref, x_ref, g_ref = refs[:3]
        add_ref = refs[3] if add is not None else None
        dx_ref, dg_ref = refs[-2:]
        i = pl.program_id(0)
        xv = x_ref[...]
        dyv = dy_ref[...]
        rstd = lax.rsqrt(jnp.mean(xv * xv, axis=-1, keepdims=True) + EPS)
        xhat = xv * rstd
        part = jnp.sum(dyv * xhat, axis=0, keepdims=True)

        @pl.when(i == 0)
        def _():
            dg_ref[...] = part

        @pl.when(i > 0)
        def _():
            dg_ref[...] += part

        dxh = dyv * g_ref[...]
        dx = rstd * (dxh - xhat * jnp.mean(dxh * xhat, axis=-1, keepdims=True))
        if add_ref is not None:
            dx = dx + add_ref[...]
        dx_ref[...] = dx.astype(dx_ref.dtype)

    row = pl.BlockSpec((tm, D), lambda i: (i, 0))
    in_specs = [row, _col_spec(tm, D, col), pl.BlockSpec((1, D), lambda i: (0, 0))]
    args = [dy, x, g.reshape(1, D)]
    aliases = {}
    if add is not None:
        in_specs.append(row)
        args.append(add)
    if into is not None:
        in_specs.append(pl.BlockSpec(memory_space=pl.ANY))
        args.append(into)
        aliases = {len(args) - 1: 0}
        dx_spec, dx_shape = _col_spec(tm, D, col), jax.ShapeDtypeStruct(into.shape, into.dtype)
    else:
        dx_spec, dx_shape = row, jax.ShapeDtypeStruct((L, D), F32)
    dx, dg = pl.pallas_call(
        body, name=name, grid=(L // tm,),
        in_specs=in_specs,
        out_specs=[dx_spec, pl.BlockSpec((1, D), lambda i: (0, 0))],
        out_shape=[dx_shape, jax.ShapeDtypeStruct((1, D), F32)],
        input_output_aliases=aliases,
        compiler_params=_cparams(("arbitrary",)),
    )(*args)
    return dx, dg.reshape(D)


def _loss_head(h, g, target, *, name):
    L, D = h.shape
    nb = L // BLK

    def body(h_ref, g_ref, t_ref, loss_ref, dh_ref, dg_ref):
        i = pl.program_id(0)

        @pl.when(i == 0)
        def _():
            loss_ref[...] = jnp.zeros_like(loss_ref)
            dg_ref[...] = jnp.zeros_like(dg_ref)
            dh_ref[...] = jnp.zeros_like(dh_ref)

        @pl.when(i > 0)
        def _():
            xv = h_ref[...]
            gv = g_ref[...]
            rstd = lax.rsqrt(jnp.mean(xv * xv, axis=-1, keepdims=True) + EPS)
            xhat = xv * rstd
            err = xhat * gv - t_ref[...]
            row = jnp.mean(err * err, axis=-1, keepdims=True)
            loss_ref[...] += 0.5 * jnp.sum(row, axis=0, keepdims=True)
            dy = err * (1.0 / D)
            dg_ref[...] += jnp.sum(dy * xhat, axis=0, keepdims=True)
            dxh = dy * gv
            dh_ref[...] = rstd * (dxh - xhat * jnp.mean(dxh * xhat, axis=-1, keepdims=True))

    loss, dh, dg = pl.pallas_call(
        body, name=name, grid=(nb,),
        in_specs=[pl.BlockSpec((BLK, D), lambda i: (i, 0)), pl.BlockSpec((1, D), lambda i: (0, 0)),
                  pl.BlockSpec((BLK, D), lambda i: (jnp.maximum(i - 1, 0), 0))],
        out_specs=[pl.BlockSpec((1, 128), lambda i: (0, 0)), pl.BlockSpec((BLK, D), lambda i: (i, 0)),
                   pl.BlockSpec((1, D), lambda i: (0, 0))],
        out_shape=[jax.ShapeDtypeStruct((1, 128), F32), jax.ShapeDtypeStruct((L, D), F32),
                   jax.ShapeDtypeStruct((1, D), F32)],
        compiler_params=_cparams(("arbitrary",)),
    )(h, g.reshape(1, D), target)
    return loss, dh, dg.reshape(D)


_NT = (((1,), (1,)), ((), ()))
_TN = (((0,), (0,)), ((), ()))


def _attn_fwd(q, k, v, kbias, *, T, window=False, sink=None, side=None, name):
    sn = 0 if side is None else side.n
    H, L, dk = q.shape
    Hkv = k.shape[0]
    dvx = v.shape[2]
    dv = dvx // 2
    G = H // Hkv
    nt = L // T
    Hb = kbias.shape[0]
    reps = T // LANES
    assert not window or T == WINDOW
    HP = G if G > 1 else ATT_HP
    NS = 1 if G > 1 else HP
    R = HP * T // NS
    HKV = HP // G
    HB = HP if Hb > 1 else 1
    assert G == 1 or Hb == 1

    def body(*refs):
        q_ref, k_ref, v_ref, kb_ref = refs[:4]
        n = 4
        if sink is not None:
            sk_ref = refs[n]
            n += 1
        side_in = refs[n:n + sn]
        n += sn
        o_ref, lse_ref = refs[n:n + 2]
        side_out = refs[n + 2:n + 2 + sn]
        n += 2 + sn
        m_scs, acc_scs, buf_a, buf_b = (refs[n + t * NS:n + (t + 1) * NS] for t in range(4))
        i = pl.program_id(1)
        if side is not None:
            _ride(side, [*side_in, *side_out, *refs[n + 4 * NS:]], pl.program_id(0) * nt + i, (H // HP) * nt)
        for a in range(NS):
            if sink is not None:
                sk = [jnp.broadcast_to(sk_ref[b, :, 0:1], (T, LANES)) for b in range(HP)]
                m_scs[a][...] = jnp.concatenate(sk, axis=0) if G > 1 else sk[a]
                lane = lax.broadcasted_iota(jnp.int32, (R, dvx), 1)
                acc_scs[a][...] = jnp.where(lane >= dv, 1.0, 0.0)
            else:
                m_scs[a][...] = jnp.full((R, LANES), NEG, F32)
                acc_scs[a][...] = jnp.zeros((R, dvx), F32)
        row = lax.broadcasted_iota(jnp.int32, (R, T), 0) & (T - 1) if G > 1 else \
            lax.broadcasted_iota(jnp.int32, (R, T), 0)
        col = lax.broadcasted_iota(jnp.int32, (R, T), 1)

        def logits(a, j):
            rows = pl.ds(pl.multiple_of(j * T, T), T)
            qv = q_ref[...].reshape(R, dk) if G > 1 else q_ref[a]
            s = lax.dot_general(qv, k_ref[a, rows, :], _NT, preferred_element_type=F32)
            return s - kb_ref[a if HB > 1 else 0, j]

        def update(a, s, j, kind):
            rows = pl.ds(pl.multiple_of(j * T, T), T)
            m_sc, acc_sc = m_scs[a], acc_scs[a]
            if kind == "diag":
                s = jnp.where(row >= col, s, NEG)
            elif kind == "prev":
                s = jnp.where((col > row) & (i > 0), s, NEG)
            m_prev = m_sc[...]
            m_new = jnp.maximum(m_prev, jnp.max(s, axis=1, keepdims=True))
            alpha = jnp.exp(m_prev - m_new)
            p = jnp.exp(s - jnp.tile(m_new, (1, reps)))
            acc_sc[...] = alpha[:, :dvx] * acc_sc[...] + jnp.dot(p.astype(BF16), v_ref[a, rows, :],
                                                                 preferred_element_type=F32)
            m_sc[...] = m_new

        if window:
            before = jnp.maximum(i - 1, 0)
            s_prev = [logits(a, before) for a in range(NS)]
            s_diag = [logits(a, i) for a in range(NS)]
            for a in range(NS):
                update(a, s_prev[a], before, "prev")
                update(a, s_diag[a], i, "diag")
        else:
            def fill(buf, j):
                for a in range(NS):
                    buf[a][...] = logits(a, j)

            def drain(buf, j, kind):
                for a in range(NS):
                    update(a, buf[a][...], j, kind)

            fill(buf_a, 0)

            def pair(t, c):
                fill(buf_b, 2 * t + 1)
                drain(buf_a, 2 * t, "full")
                fill(buf_a, 2 * t + 2)
                drain(buf_b, 2 * t + 1, "full")
                return c

            lax.fori_loop(0, i // 2, pair, 0)

            @pl.when(i % 2 == 1)
            def _():
                fill(buf_b, i)
                drain(buf_a, i - 1, "full")
                drain(buf_b, i, "diag")

            @pl.when(i % 2 == 0)
            def _():
                drain(buf_a, i, "diag")

        for a in range(NS):
            acc = acc_scs[a][...]
            ov = acc[:, :dv] / acc[:, dv:]
            lsev = m_scs[a][:, 0:1] + jnp.log(acc[:, dv:dv + 1])
            if G > 1:
                o_ref[...] = ov.reshape(HP, T, dv)
                lse_ref[...] = lsev.reshape(HP, T, 1)
            else:
                o_ref[a] = ov
                lse_ref[a] = lsev

    in_specs = [pl.BlockSpec((HP, T, dk), lambda h, i: (h, i, 0)),
                pl.BlockSpec((HKV, L, dk), lambda h, i: (h, 0, 0)),
                pl.BlockSpec((HKV, L, dvx), lambda h, i: (h, 0, 0)),
                pl.BlockSpec((HB, nt, 1, T), lambda h, i: (h if Hb > 1 else 0, 0, 0, 0))]
    args = [q, k, v, kbias]
    if sink is not None:
        in_specs += [pl.BlockSpec((HP, 1, LANES), lambda h, i: (h, 0, 0))]
        args += [sink]
    any_spec = pl.BlockSpec(memory_space=pl.ANY)
    return pl.pallas_call(
        body, name=name, grid=(H // HP, nt),
        in_specs=in_specs + [any_spec] * sn,
        out_specs=[pl.BlockSpec((HP, T, dv), lambda h, i: (h, i, 0)),
                   pl.BlockSpec((HP, T, 1), lambda h, i: (h, i, 0))] + [any_spec] * sn,
        out_shape=[jax.ShapeDtypeStruct((H, L, dv), F32), jax.ShapeDtypeStruct((H, L, 1), F32)]
        + ([] if side is None else side.out_shape),
        scratch_shapes=[pltpu.VMEM((R, LANES), F32)] * NS + [pltpu.VMEM((R, dvx), F32)] * NS
        + [pltpu.VMEM((R, T), F32)] * (2 * NS) + ([] if side is None else side.scratch),
        compiler_params=_side_cparams(side, ("parallel", "arbitrary")),
    )(*args, *([] if side is None else side.arrs))


def _attn_bwd_t(q, k, v, kbias_col, lse_row, delta_row, do, *, T, fox=False, side=None, name):
    sn = 0 if side is None else side.n
    H, L, dk = q.shape
    dv = do.shape[2]
    nt = L // T
    Hb = kbias_col.shape[0]

    def body(*refs):
        q_ref, k_ref, v_ref, kb_ref, lse_ref, dl_ref, do_ref = refs[:7]
        side_in = refs[7:7 + sn]
        n = 7 + sn
        dq_ref, dk_ref, dv_ref = refs[n:n + 3]
        n += 3
        if fox:
            dcq_ref, dck_ref = refs[n:n + 2]
            n += 2
        side_out = refs[n:n + sn]
        n += sn
        buf_a, buf_b = refs[n:n + 2], refs[n + 2:n + 4]
        j = pl.program_id(1)
        if side is not None:
            _ride(side, [*side_in, *side_out, *refs[n + 4:]], pl.program_id(0) * nt + j, H * nt)

        @pl.when(j == 0)
        def _():
            dq_ref[...] = jnp.zeros_like(dq_ref)
            if fox:
                dcq_ref[...] = jnp.zeros_like(dcq_ref)

        dk_ref[...] = jnp.zeros_like(dk_ref)
        dv_ref[...] = jnp.zeros_like(dv_ref)
        if fox:
            dck_ref[...] = jnp.zeros_like(dck_ref)
        kb = k_ref[0]
        vb = v_ref[0][:, :dv]
        kbias_j = kb_ref[0]
        key = lax.broadcasted_iota(jnp.int32, (T, T), 0)
        qry = lax.broadcasted_iota(jnp.int32, (T, T), 1)

        def fill(buf, i):
            rows = pl.ds(pl.multiple_of(i * T, T), T)
            buf[0][...] = lax.dot_general(kb, q_ref[0, rows, :], _NT, preferred_element_type=F32) - kbias_j
            buf[1][...] = lax.dot_general(vb, do_ref[0, rows, :], _NT, preferred_element_type=F32)

        def drain(buf, i, kind):
            rows = pl.ds(pl.multiple_of(i * T, T), T)
            st = buf[0][...]
            if kind == "diag":
                st = jnp.where(key <= qry, st, NEG)
            pt = jnp.exp(st - lse_ref[0, i])
            dst = pt * (buf[1][...] - dl_ref[0, i])
            dsb = dst.astype(BF16)
            dv_ref[0] += jnp.dot(pt.astype(BF16), do_ref[0, rows, :], preferred_element_type=F32)
            dk_ref[0] += jnp.dot(dsb, q_ref[0, rows, :], preferred_element_type=F32)
            dq_ref[0, rows, :] += lax.dot_general(dsb, kb, _TN, preferred_element_type=F32)
            if fox:
                dcq_ref[0, i] += jnp.sum(dst, axis=0, keepdims=True)
                dck_ref[0] += -jnp.sum(dst, axis=1, keepdims=True)

        first = j + 1
        rest = nt - first
        fill(buf_a, j)
        fill(buf_b, jnp.minimum(first, nt - 1))
        drain(buf_a, j, "diag")

        def pair(t, c):
            i0 = first + 2 * t
            fill(buf_a, i0 + 1)
            drain(buf_b, i0, "full")
            fill(buf_b, jnp.minimum(i0 + 2, nt - 1))
            drain(buf_a, i0 + 1, "full")
            return c

        lax.fori_loop(0, rest // 2, pair, 0)

        @pl.when(rest % 2 == 1)
        def _():
            drain(buf_b, nt - 1, "full")

    rows_spec = pl.BlockSpec((1, nt, 1, T), lambda h, j: (h, 0, 0, 0))
    in_specs = [pl.BlockSpec((1, L, dk), lambda h, j: (h, 0, 0)),
                pl.BlockSpec((1, T, dk), lambda h, j: (h, j, 0)),
                pl.BlockSpec((1, T, v.shape[2]), lambda h, j: (h, j, 0)),
                pl.BlockSpec((1, T, 1), lambda h, j: (h if Hb > 1 else 0, j, 0)),
                rows_spec, rows_spec,
                pl.BlockSpec((1, L, dv), lambda h, j: (h, 0, 0))]
    out_specs = [pl.BlockSpec((1, L, dk), lambda h, j: (h, 0, 0)),
                 pl.BlockSpec((1, T, dk), lambda h, j: (h, j, 0)),
                 pl.BlockSpec((1, T, dv), lambda h, j: (h, j, 0))]
    out_shape = [jax.ShapeDtypeStruct((H, L, dk), F32), jax.ShapeDtypeStruct((H, L, dk), F32),
                 jax.ShapeDtypeStruct((H, L, dv), F32)]
    if fox:
        out_specs += [rows_spec, pl.BlockSpec((1, T, 1), lambda h, j: (h, j, 0))]
        out_shape += [jax.ShapeDtypeStruct((H, nt, 1, T), F32), jax.ShapeDtypeStruct((H, L, 1), F32)]
    any_spec = pl.BlockSpec(memory_space=pl.ANY)
    outs = pl.pallas_call(
        body, name=name, grid=(H, nt),
        in_specs=in_specs + [any_spec] * sn, out_specs=out_specs + [any_spec] * sn,
        out_shape=out_shape + ([] if side is None else side.out_shape),
        scratch_shapes=[pltpu.VMEM((T, T), F32)] * 4 + ([] if side is None else side.scratch),
        compiler_params=_side_cparams(side, ("parallel", "arbitrary")),
    )(q, k, v, kbias_col, lse_row, delta_row, do, *([] if side is None else side.arrs))
    main, rest = outs[:len(outs) - sn], outs[len(outs) - sn:]
    return (*(main if fox else (*main, None, None)), *rest)


def _attn_bwd_window(q, k, v, kbias, o, lse, do, *, name):
    H, L, dk = q.shape
    Hkv = k.shape[0]
    dv = do.shape[2]
    G = H // Hkv
    T = WINDOW
    nt = L // T
    R = G * T

    def body(q_ref, kc_ref, kp_ref, vc_ref, vp_ref, kb_ref, o_ref, lse_ref, do_ref, dq_ref, dk_ref, dv_ref,
             dk_sc, dv_sc):
        i = pl.program_id(1)
        row = lax.broadcasted_iota(jnp.int32, (R, T), 0) & (T - 1)
        col = lax.broadcasted_iota(jnp.int32, (R, T), 1)

        @pl.when(i == 0)
        def _():
            dk_sc[...] = jnp.zeros_like(dk_sc)
            dv_sc[...] = jnp.zeros_like(dv_sc)

        @pl.when(i == nt)
        def _():
            dk_ref[0] = dk_sc[...]
            dv_ref[0] = dv_sc[...]

        @pl.when(i < nt)
        def _():
            qb = q_ref[...].reshape(R, dk)
            dof = do_ref[...].reshape(R, dv)
            dob = dof.astype(BF16)
            lse_c = lse_ref[...].reshape(R, 1)
            delta = jnp.sum(dof * o_ref[...].reshape(R, dv), axis=1, keepdims=True)

            def grads(kt, vt, kbias_j, mask):
                s = lax.dot_general(qb, kt, _NT, preferred_element_type=F32) - kbias_j
                p = jnp.exp(jnp.where(mask, s, NEG) - lse_c)
                dp = lax.dot_general(dob, vt, _NT, preferred_element_type=F32)
                ds = (p * (dp - delta)).astype(BF16)
                return (jnp.dot(ds, kt, preferred_element_type=F32),
                        lax.dot_general(ds, qb, _TN, preferred_element_type=F32),
                        lax.dot_general(p.astype(BF16), dob, _TN, preferred_element_type=F32))

            ip = jnp.maximum(i - 1, 0)
            dq_p, dk_p, dv_p = grads(kp_ref[0], vp_ref[0][:, :dv], kb_ref[0, ip], (col > row) & (i > 0))
            dq_c, dk_c, dv_c = grads(kc_ref[0], vc_ref[0][:, :dv], kb_ref[0, i], row >= col)
            dq_ref[...] = (dq_p + dq_c).reshape(G, T, dk)
            dk_ref[0] = dk_sc[...] + dk_p
            dv_ref[0] = dv_sc[...] + dv_p
            dk_sc[...] = dk_c
            dv_sc[...] = dv_c

    def cur(i):
        return jnp.minimum(i, nt - 1)

    def prev(i):
        return jnp.maximum(jnp.minimum(i, nt - 1) - 1, 0)

    def written(i):
        return jnp.maximum(i - 1, 0)

    qs = lambda d: pl.BlockSpec((G, T, d), lambda h, i: (h, cur(i), 0))
    return pl.pallas_call(
        body, name=name, grid=(Hkv, nt + 1),
        in_specs=[qs(dk),
                  pl.BlockSpec((1, T, dk), lambda h, i: (h, cur(i), 0)),
                  pl.BlockSpec((1, T, dk), lambda h, i: (h, prev(i), 0)),
                  pl.BlockSpec((1, T, v.shape[2]), lambda h, i: (h, cur(i), 0)),
                  pl.BlockSpec((1, T, v.shape[2]), lambda h, i: (h, prev(i), 0)),
                  pl.BlockSpec((1, nt, 1, T), lambda h, i: (0, 0, 0, 0)),
                  qs(dv), qs(1), qs(dv)],
        out_specs=[qs(dk),
                   pl.BlockSpec((1, T, dk), lambda h, i: (h, written(i), 0)),
                   pl.BlockSpec((1, T, dv), lambda h, i: (h, written(i), 0))],
        out_shape=[jax.ShapeDtypeStruct((H, L, dk), F32), jax.ShapeDtypeStruct((Hkv, L, dk), F32),
                   jax.ShapeDtypeStruct((Hkv, L, dv), F32)],
        scratch_shapes=[pltpu.VMEM((T, dk), F32), pltpu.VMEM((T, dv), F32)],
        compiler_params=_cparams(("parallel", "arbitrary")),
    )(q, k, k, v, v, kbias, o, lse, do)


def _adamw(w, gparts, m, v, *, name):
    n, R, C = gparts.shape
    tr = _pick(R, (128, 64, 32, 16, 8))
    c1 = 1.0 - ADAM_B1 ** ADAM_STEP
    c2 = 1.0 - ADAM_B2 ** ADAM_STEP

    def body(w_ref, g_ref, m_ref, v_ref, go_ref, d_ref, mo_ref, vo_ref):
        g = g_ref[0].astype(F32)
        for t in range(1, n):
            g = g + g_ref[t].astype(F32)
        mn = ADAM_B1 * m_ref[...] + (1.0 - ADAM_B1) * g
        vn = ADAM_B2 * v_ref[...] + (1.0 - ADAM_B2) * (g * g)
        go_ref[...] = g
        mo_ref[...] = mn
        vo_ref[...] = vn
        d_ref[...] = -ADAM_LR * ((mn / c1) / (jnp.sqrt(vn / c2) + ADAM_EPS) + ADAM_WD * w_ref[...])

    spec = pl.BlockSpec((tr, C), lambda i: (i, 0))
    return pl.pallas_call(
        body, name=name, grid=(R // tr,),
        in_specs=[spec, pl.BlockSpec((n, tr, C), lambda i: (0, i, 0)), spec, spec],
        out_specs=[spec] * 4,
        out_shape=[jax.ShapeDtypeStruct((R, C), F32)] * 4,
        compiler_params=_cparams(("parallel",)),
    )(w, gparts, m, v)


def _me():
    return lax.axis_index("x"), lax.axis_index("y"), lax.axis_index("c")


class _CommJob:
    def __init__(self, arrs):
        self.arrs = list(arrs)
        self.n = len(arrs)
        self.scratch = [pltpu.SemaphoreType.DMA((self.n, 7)), pltpu.SemaphoreType.DMA((self.n, 7)),
                        pltpu.SemaphoreType.DMA((self.n,))]

    def bind(self, refs):
        n = self.n
        self.ins, self.outs = refs[:n], refs[n:2 * n]
        self.send_sems, self.recv_sems, self.local_sems = refs[2 * n:2 * n + 3]

    def middle(self):
        pass


class _Gather(_CommJob):
    def __init__(self, arrs):
        super().__init__(arrs)
        self.out_shape = [jax.ShapeDtypeStruct((N_DEV, *a.shape), a.dtype) for a in arrs]

    def _where(self):
        x, y, c = _me()
        return (x, y, c), (x, y, 1 - c), [(1 - x, y), (x, 1 - y), (1 - x, 1 - y)], c

    def _copy(self, t, k, block, to, src=None):
        dst = self.outs[t].at[4 * block[0] + 2 * block[1] + block[2]]
        return pltpu.make_async_remote_copy(
            src_ref=dst if src is None else src, dst_ref=dst,
            send_sem=self.send_sems.at[t, k], recv_sem=self.recv_sems.at[t, k],
            device_id=to, device_id_type=MESH)

    def _mine(self, t, me):
        return pltpu.make_async_copy(self.ins[t], self.outs[t].at[4 * me[0] + 2 * me[1] + me[2]],
                                     self.local_sems.at[t])

    def _first(self, t, me, sibling, chips, c):
        return [self._copy(t, 0, me, sibling, src=self.ins[t])] + \
               [self._copy(t, 1 + j, me, (*chip, c), src=self.ins[t]) for j, chip in enumerate(chips)]

    def start(self):
        me, sibling, chips, c = self._where()
        for t in range(self.n):
            self._mine(t, me).start()
            for cp in self._first(t, me, sibling, chips, c):
                cp.start()

    def middle(self):
        me, sibling, chips, c = self._where()
        for j, chip in enumerate(chips):
            for t in range(self.n):
                self._copy(t, 1 + j, (*chip, c), me).wait_recv()
                self._copy(t, 4 + j, (*chip, c), sibling).start()

    def finish(self):
        me, sibling, chips, c = self._where()
        for t in range(self.n):
            self._copy(t, 0, sibling, me).wait_recv()
            for j, chip in enumerate(chips):
                self._copy(t, 4 + j, (*chip, 1 - c), me).wait_recv()
        for t in range(self.n):
            for cp in self._first(t, me, sibling, chips, c):
                cp.wait_send()
            for j, chip in enumerate(chips):
                self._copy(t, 4 + j, (*chip, c), sibling).wait_send()
            self._mine(t, me).wait()


class _Exchange(_CommJob):
    def __init__(self, arrs):
        super().__init__(arrs)
        self.out_shape = [jax.ShapeDtypeStruct(a.shape, a.dtype) for a in arrs]

    def _copies(self, t):
        x, y, c = _me()
        my_idx = 4 * x + 2 * y + c
        pairs = []
        for k in range(1, N_DEV):
            peer = (x ^ ((k >> 2) & 1), y ^ ((k >> 1) & 1), c ^ (k & 1))
            peer_idx = 4 * peer[0] + 2 * peer[1] + peer[2]
            sems = dict(send_sem=self.send_sems.at[t, k - 1], recv_sem=self.recv_sems.at[t, k - 1],
                        device_id=peer, device_id_type=MESH)
            pairs.append((pltpu.make_async_remote_copy(src_ref=self.ins[t].at[peer_idx],
                                                       dst_ref=self.outs[t].at[my_idx], **sems),
                          pltpu.make_async_remote_copy(src_ref=self.ins[t].at[peer_idx],
                                                       dst_ref=self.outs[t].at[peer_idx], **sems)))
        return pairs

    def _mine(self, t):
        x, y, c = _me()
        my_idx = 4 * x + 2 * y + c
        return pltpu.make_async_copy(self.ins[t].at[my_idx], self.outs[t].at[my_idx], self.local_sems.at[t])

    def start(self):
        for t in range(self.n):
            self._mine(t).start()
            for snd, _ in self._copies(t):
                snd.start()

    def finish(self):
        for t in range(self.n):
            pairs = self._copies(t)
            for _, rcv in pairs:
                rcv.wait_recv()
            for snd, _ in pairs:
                snd.wait_send()
            self._mine(t).wait()


def _comm(job, *, name):
    def body(*refs):
        job.bind(refs)
        job.start()
        job.middle()
        job.finish()

    any_spec = pl.BlockSpec(memory_space=pl.ANY)
    return pl.pallas_call(
        body, name=name,
        in_specs=[any_spec] * job.n, out_specs=[any_spec] * job.n,
        out_shape=job.out_shape, scratch_shapes=job.scratch,
        compiler_params=pltpu.CompilerParams(has_side_effects=True),
    )(*job.arrs)


def _ride(job, refs, step, total):
    job.bind(refs)

    @pl.when(step == 0)
    def _():
        job.start()

    @pl.when(step == (total * 3) // 5)
    def _():
        job.middle()

    @pl.when(step == total - 1)
    def _():
        job.finish()


SCALE_A = DH ** -0.5
SCALE_B = (MLA_NOPE + MLA_ROPE) ** -0.5
SCALE_C = DH ** -0.5
DQK_B = MLA_NOPE + MLA_ROPE


def _rope_tables(L, half, width):
    pos = (jnp.arange(L) - PAD).astype(F32)
    lane = jnp.arange(width)
    inv = ROPE_THETA ** (-(lane % half).astype(F32) / half)
    ang = pos[:, None] * inv[None, :]
    sign = jnp.where(lane % (2 * half) < half, -1.0, 1.0).astype(F32)
    return jnp.cos(ang), jnp.sin(ang) * sign[None, :]


def _rope_lanes(x, cos, sin, half):
    W = x.shape[1]
    lane = lax.broadcasted_iota(jnp.int32, x.shape, 1)
    first = (lane & (2 * half - 1)) < half
    partner = jnp.where(first, pltpu.roll(x, W - half, 1), pltpu.roll(x, half, 1))
    return x * cos + partner * sin


def _tile_lanes(t, width):
    return t if t.shape[1] == width else jnp.tile(t, (1, width // t.shape[1]))


def _split(x, H, d, dst):
    for h in range(H):
        dst[h] = x[:, d * h:d * (h + 1)].astype(dst.dtype)


def _join(src, H):
    return jnp.concatenate([src[h] for h in range(H)], axis=1)


def _head_spec(H, tm, d):
    return pl.BlockSpec((H, tm, d), lambda i: (0, i, 0))


def _split_values(x, H, dst):
    ones = jnp.ones((x.shape[0], DH), BF16)
    for h in range(H):
        dst[h] = jnp.concatenate([x[:, DH * h:DH * (h + 1)].astype(BF16), ones], axis=1)


def _prep_a(proj, *, name):
    L = proj.shape[0]
    tm = ROW_T

    def body(x_ref, qo, ko, vo):
        _split(x_ref[:, 0:512] * SCALE_A, HEADS, DH, qo)
        _split(x_ref[:, 512:1024], HEADS, DH, ko)
        _split_values(x_ref[:, 1024:1536], HEADS, vo)

    return pl.pallas_call(
        body, name=name, grid=(L // tm,),
        in_specs=[_col_spec(tm, 1536, QKVA)],
        out_specs=[_head_spec(HEADS, tm, DH)] * 2 + [_head_spec(HEADS, tm, 2 * DH)],
        out_shape=[jax.ShapeDtypeStruct((HEADS, L, DH), BF16)] * 2 + [jax.ShapeDtypeStruct((HEADS, L, 2 * DH), BF16)],
        compiler_params=_cparams(("parallel",)),
    )(proj)


def _unprep_a(dq, dk, dv, dproj, *, name):
    L = dq.shape[1]
    tm = ROW_T

    def body(dq_ref, dk_ref, dv_ref, _, dp_ref):
        dp_ref[:, 0:512] = (_join(dq_ref, HEADS) * SCALE_A).astype(BF16)
        dp_ref[:, 512:1024] = _join(dk_ref, HEADS).astype(BF16)
        dp_ref[:, 1024:1536] = _join(dv_ref, HEADS).astype(BF16)

    hs = _head_spec(HEADS, tm, DH)
    return pl.pallas_call(
        body, name=name, grid=(L // tm,),
        in_specs=[hs, hs, hs, pl.BlockSpec(memory_space=pl.ANY)],
        out_specs=_col_spec(tm, 1536, QKVA),
        out_shape=jax.ShapeDtypeStruct(dproj.shape, dproj.dtype),
        input_output_aliases={3: 0},
        compiler_params=_cparams(("parallel",)),
    )(dq, dk, dv, dproj)


def _prep_c(proj, tab, *, name):
    L = proj.shape[0]
    tm = ROW_T

    def body(x_ref, cos_ref, sin_ref, qo, ko, vo):
        cos, sin = cos_ref[...], sin_ref[...]
        q = _rope_lanes(x_ref[:, 0:512], _tile_lanes(cos, 512), _tile_lanes(sin, 512), DH // 2)
        _split(q * SCALE_C, HEADS, DH, qo)
        _split(_rope_lanes(x_ref[:, 512:640], cos, sin, DH // 2), SWA_KV_HEADS, DH, ko)
        _split_values(x_ref[:, 640:768], SWA_KV_HEADS, vo)

    t128 = pl.BlockSpec((tm, LANES), lambda i: (i, 0))
    return pl.pallas_call(
        body, name=name, grid=(L // tm,),
        in_specs=[_col_spec(tm, 768, QKVC), t128, t128],
        out_specs=[_head_spec(HEADS, tm, DH), _head_spec(SWA_KV_HEADS, tm, DH),
                   _head_spec(SWA_KV_HEADS, tm, 2 * DH)],
        out_shape=[jax.ShapeDtypeStruct((HEADS, L, DH), BF16), jax.ShapeDtypeStruct((SWA_KV_HEADS, L, DH), BF16),
                   jax.ShapeDtypeStruct((SWA_KV_HEADS, L, 2 * DH), BF16)],
        compiler_params=_cparams(("parallel",)),
    )(proj, *tab)


def _unprep_c(dq, dk, dv, tab, dproj, *, name):
    L = dq.shape[1]
    tm = ROW_T

    def body(dq_ref, dk_ref, dv_ref, cos_ref, sin_ref, _, dp_ref):
        cos, nsin = cos_ref[...], -sin_ref[...]
        dqv = _join(dq_ref, HEADS) * SCALE_C
        dp_ref[:, 0:512] = _rope_lanes(dqv, _tile_lanes(cos, 512), _tile_lanes(nsin, 512), DH // 2).astype(BF16)
        dp_ref[:, 512:640] = _rope_lanes(_join(dk_ref, SWA_KV_HEADS), cos, nsin, DH // 2).astype(BF16)
        dp_ref[:, 640:768] = _join(dv_ref, SWA_KV_HEADS).astype(BF16)

    hs = _head_spec(HEADS, tm, DH)
    hkv = _head_spec(SWA_KV_HEADS, tm, DH)
    t128 = pl.BlockSpec((tm, LANES), lambda i: (i, 0))
    return pl.pallas_call(
        body, name=name, grid=(L // tm,),
        in_specs=[hs, hkv, hkv, t128, t128, pl.BlockSpec(memory_space=pl.ANY)],
        out_specs=_col_spec(tm, 768, QKVC),
        out_shape=jax.ShapeDtypeStruct(dproj.shape, dproj.dtype),
        input_output_aliases={5: 0},
        compiler_params=_cparams(("parallel",)),
    )(dq, dk, dv, *tab, dproj)


def _q_tables(cos, sin, on_ref):
    on = on_ref[...] > 0.5
    width = on_ref.shape[1]
    return jnp.where(on, _tile_lanes(cos, width), 1.0), jnp.where(on, _tile_lanes(sin, width), 0.0)


def _prep_b(qbm, kvbm, proj, tab_k, q_rotary, *, name):
    L = proj.shape[0]
    tm = ROW_T

    def body(q_ref, kv_ref, kr_ref, ck_ref, sk_ref, on_ref, qo, ko, vo):
        cq, sq = _q_tables(ck_ref[...], sk_ref[...], on_ref)
        q = _rope_lanes(q_ref[...], cq, sq, MLA_ROPE // 2) * SCALE_B
        _split(q, HEADS, DQK_B, qo)
        kr = _rope_lanes(kr_ref[...], ck_ref[...], sk_ref[...], MLA_ROPE // 2)[:, :MLA_ROPE].astype(BF16)
        kv = kv_ref[...]
        ones = jnp.ones((tm, DH), BF16)
        for h in range(HEADS):
            ko[h] = jnp.concatenate([kv[:, 128 * h:128 * h + MLA_NOPE].astype(BF16), kr], axis=1)
            vo[h] = jnp.concatenate([kv[:, 128 * h + MLA_NOPE:128 * (h + 1)].astype(BF16), ones], axis=1)

    t128 = pl.BlockSpec((tm, LANES), lambda i: (i, 0))
    t768 = pl.BlockSpec((tm, 768), lambda i: (i, 0))
    return pl.pallas_call(
        body, name=name, grid=(L // tm,),
        in_specs=[t768, pl.BlockSpec((tm, 1024), lambda i: (i, 0)), _col_spec(tm, 128, KR), t128, t128,
                  pl.BlockSpec((1, 768), lambda i: (0, 0))],
        out_specs=[_head_spec(HEADS, tm, DQK_B), _head_spec(HEADS, tm, DQK_B), _head_spec(HEADS, tm, 2 * DH)],
        out_shape=[jax.ShapeDtypeStruct((HEADS, L, DQK_B), BF16), jax.ShapeDtypeStruct((HEADS, L, DQK_B), BF16),
                   jax.ShapeDtypeStruct((HEADS, L, 2 * DH), BF16)],
        compiler_params=_cparams(("parallel",)),
    )(qbm, kvbm, proj, *tab_k, q_rotary)


def _unprep_b(dq, dk, dv, tab_k, q_rotary, dproj, *, name):
    L = dq.shape[1]
    tm = ROW_T

    def body(dq_ref, dk_ref, dv_ref, ck_ref, sk_ref, on_ref, _, dp_kr, dqo, dkvo):
        cq, sq = _q_tables(ck_ref[...], sk_ref[...], on_ref)
        dqv = _join(dq_ref, HEADS) * SCALE_B
        dqo[...] = _rope_lanes(dqv, cq, -sq, MLA_ROPE // 2).astype(BF16)
        parts = []
        dkr = None
        for h in range(HEADS):
            dkh = dk_ref[h]
            parts += [dkh[:, :MLA_NOPE], dv_ref[h]]
            r = dkh[:, MLA_NOPE:]
            dkr = r if dkr is None else dkr + r
        dkvo[...] = jnp.concatenate(parts, axis=1).astype(BF16)
        dkr = jnp.concatenate([dkr, jnp.zeros((tm, LANES - MLA_ROPE), F32)], axis=1)
        dp_kr[...] = _rope_lanes(dkr, ck_ref[...], -sk_ref[...], MLA_ROPE // 2).astype(BF16)

    t128 = pl.BlockSpec((tm, LANES), lambda i: (i, 0))
    t768 = pl.BlockSpec((tm, 768), lambda i: (i, 0))
    hq = _head_spec(HEADS, tm, DQK_B)
    return pl.pallas_call(
        body, name=name, grid=(L // tm,),
        in_specs=[hq, hq, _head_spec(HEADS, tm, DH), t128, t128, pl.BlockSpec((1, 768), lambda i: (0, 0)),
                  pl.BlockSpec(memory_space=pl.ANY)],
        out_specs=[_col_spec(tm, 128, KR), t768, pl.BlockSpec((tm, 1024), lambda i: (i, 0))],
        out_shape=[jax.ShapeDtypeStruct(dproj.shape, dproj.dtype), jax.ShapeDtypeStruct((L, 768), BF16),
                   jax.ShapeDtypeStruct((L, 1024), BF16)],
        input_output_aliases={6: 0},
        compiler_params=_cparams(("parallel",)),
    )(dq, dk, dv, *tab_k, q_rotary, dproj)


def _gate(y, proj, zcol, *, name):
    L = proj.shape[0]
    tm = ROW_T

    def body(y_ref, z_ref, u_ref):
        z = z_ref[...]
        u_ref[...] = (_join(y_ref, HEADS) * (z * jax.nn.sigmoid(z))).astype(BF16)

    return pl.pallas_call(
        body, name=name, grid=(L // tm,),
        in_specs=[_head_spec(HEADS, tm, DH), _col_spec(tm, 512, zcol)],
        out_specs=pl.BlockSpec((tm, 512), lambda i: (i, 0)),
        out_shape=jax.ShapeDtypeStruct((L, 512), BF16),
        compiler_params=_cparams(("parallel",)),
    )(y, proj)


def _gate_bwd(du, y, proj, zcol, dproj, *, delta_rows=False, zero_tail=0, name):
    L = proj.shape[0]
    tm = ROW_T
    assert not delta_rows or tm == ATT_T

    def body(du_ref, y_ref, z_ref, _, dz_ref, dy_ref, *dl_ref):
        z = z_ref[...]
        duv = du_ref[...]
        sg = jax.nn.sigmoid(z)
        yv = _join(y_ref, HEADS)
        dyv = duv * (z * sg)
        _split(dyv, HEADS, DH, dy_ref)
        dz_ref[:, 0:512] = (duv * yv * (sg * (1.0 + z * (1.0 - sg)))).astype(BF16)
        if zero_tail:
            dz_ref[:, 512:] = jnp.zeros((tm, zero_tail), BF16)
        if delta_rows:
            prod = dyv * yv
            ones = jnp.ones((8, DH), F32)
            for h in range(HEADS):
                sums = lax.dot_general(ones, prod[:, DH * h:DH * (h + 1)], _NT, preferred_element_type=F32)
                dl_ref[0][h, 0] = sums[0:1, :]

    hs = _head_spec(HEADS, tm, DH)
    out_specs = [_col_spec(tm, 512 + zero_tail, zcol), hs]
    out_shape = [jax.ShapeDtypeStruct(dproj.shape, dproj.dtype),
                 jax.ShapeDtypeStruct((HEADS, L, DH), BF16 if delta_rows else F32)]
    if delta_rows:
        out_specs.append(pl.BlockSpec((HEADS, 1, 1, tm), lambda i: (0, i, 0, 0)))
        out_shape.append(jax.ShapeDtypeStruct((HEADS, L // tm, 1, tm), F32))
    return pl.pallas_call(
        body, name=name, grid=(L // tm,),
        in_specs=[pl.BlockSpec((tm, 512), lambda i: (i, 0)), hs, _col_spec(tm, 512, zcol),
                  pl.BlockSpec(memory_space=pl.ANY)],
        out_specs=out_specs, out_shape=out_shape,
        input_output_aliases={3: 0},
        compiler_params=_cparams(("parallel",)),
    )(du, y, proj, dproj)


MERGE_T = 192


def _branch_merge(us, w_branch, proj, *, name):
    L = proj.shape[0]
    tm = ROW_T

    def body(u0, u1, u2, w_ref, g_ref, m_ref, pb_ref):
        acc = None
        for n, u_ref in enumerate((u0, u1, u2)):
            pb = jnp.dot(u_ref[...], w_ref[n], preferred_element_type=F32)
            pb_ref[n] = pb.astype(BF16)
            t = jax.nn.sigmoid(g_ref[:, n * D_MODEL:(n + 1) * D_MODEL]) * pb
            acc = t if acc is None else acc + t
        m_ref[...] = acc.astype(BF16)

    urow = pl.BlockSpec((tm, BRANCH_W), lambda i: (i, 0))
    return pl.pallas_call(
        body, name=name, grid=(L // tm,),
        in_specs=[urow] * N_BRANCH + [pl.BlockSpec((N_BRANCH, BRANCH_W, D_MODEL), lambda i: (0, 0, 0)),
                                      _col_spec(tm, N_BRANCH * D_MODEL, GATES)],
        out_specs=[pl.BlockSpec((tm, D_MODEL), lambda i: (i, 0)),
                   pl.BlockSpec((N_BRANCH, tm, D_MODEL), lambda i: (0, i, 0))],
        out_shape=[jax.ShapeDtypeStruct((L, D_MODEL), BF16), jax.ShapeDtypeStruct((N_BRANCH, L, D_MODEL), BF16)],
        compiler_params=_cparams(("parallel",)),
    )(*us, w_branch, proj)


def _merge_bwd(dmerged, proj, pb, *, name):
    L = proj.shape[0]
    tm = MERGE_T

    def body(dm_ref, g_ref, p_ref, dg_ref, dp0, dp1, dp2):
        dm = dm_ref[...]
        for n, dp_ref in enumerate((dp0, dp1, dp2)):
            cols = slice(n * D_MODEL, (n + 1) * D_MODEL)
            sg = jax.nn.sigmoid(g_ref[:, cols])
            dp_ref[...] = (dm * sg).astype(BF16)
            dg_ref[:, cols] = (dm * p_ref[n].astype(F32) * (sg * (1.0 - sg))).astype(BF16)

    row = pl.BlockSpec((tm, D_MODEL), lambda i: (i, 0))
    gates = _col_spec(tm, N_BRANCH * D_MODEL, GATES)
    outs = pl.pallas_call(
        body, name=name, grid=(L // tm,),
        in_specs=[row, gates, pl.BlockSpec((N_BRANCH, tm, D_MODEL), lambda i: (0, i, 0))],
        out_specs=[gates] + [row] * N_BRANCH,
        out_shape=[jax.ShapeDtypeStruct((L, NP_IN), BF16)] + [jax.ShapeDtypeStruct((L, D_MODEL), BF16)] * N_BRANCH,
        compiler_params=_cparams(("parallel",)),
    )(dmerged, proj, pb)
    return outs[0], outs[1:]


def _scan_rows(x, reverse):
    rows = lax.broadcasted_iota(jnp.int32, x.shape, 0)
    step = 1
    while step < BLK:
        if reverse:
            x = x + jnp.where(rows < BLK - step, pltpu.roll(x, BLK - step, 0), 0.0)
        else:
            x = x + jnp.where(rows >= step, pltpu.roll(x, step, 0), 0.0)
        step *= 2
    return x


def _forget_fwd(proj, b_f, *, name):
    L = proj.shape[0]
    nb = L // BLK

    def body(x_ref, b_ref, o_ref, carry):
        i = pl.program_id(0)

        @pl.when(i == 0)
        def _():
            carry[...] = jnp.zeros_like(carry)

        c = _scan_rows(jax.nn.log_sigmoid(x_ref[...] + b_ref[...]), False) + carry[...]
        carry[...] = c[BLK - 1:BLK, :]
        pos = i * BLK + lax.broadcasted_iota(jnp.int32, (HEADS, BLK), 1)
        o_ref[...] = c.T[:HEADS, :] + jnp.where(pos < PAD, BIG, 0.0)

    return pl.pallas_call(
        body, name=name, grid=(nb,),
        in_specs=[_col_spec(BLK, LANES, FA), pl.BlockSpec((1, LANES), lambda i: (0, 0))],
        out_specs=pl.BlockSpec((HEADS, BLK), lambda i: (0, i)),
        out_shape=jax.ShapeDtypeStruct((HEADS, L), F32),
        scratch_shapes=[pltpu.VMEM((1, LANES), F32)],
        compiler_params=_cparams(("arbitrary",)),
    )(proj, jnp.pad(b_f, (0, LANES - HEADS)).reshape(1, LANES))


def _forget_bwd(dct, proj, b_f, dproj, *, name):
    L = proj.shape[0]
    nb = L // BLK

    def body(d_ref, x_ref, b_ref, _, dp_ref, db_ref, carry):
        i = pl.program_id(0)

        @pl.when(i == 0)
        def _():
            carry[...] = jnp.zeros_like(carry)
            db_ref[...] = jnp.zeros_like(db_ref)

        d = jnp.concatenate([d_ref[...], jnp.zeros((BLK - HEADS, BLK), F32)], axis=0).T
        dlog = _scan_rows(d, True) + carry[...]
        carry[...] = dlog[0:1, :]
        lane = lax.broadcasted_iota(jnp.int32, (BLK, LANES), 1)
        daf = jnp.where(lane < HEADS, dlog * jax.nn.sigmoid(-(x_ref[...] + b_ref[...])), 0.0)
        db_ref[...] += jnp.sum(daf, axis=0, keepdims=True)
        dp_ref[...] = jnp.concatenate([daf, jnp.zeros((BLK, LANES), F32)], axis=1).astype(BF16)

    back = lambda i: nb - 1 - i
    dp, db = pl.pallas_call(
        body, name=name, grid=(nb,),
        in_specs=[pl.BlockSpec((HEADS, BLK), lambda i: (0, back(i))),
                  pl.BlockSpec((BLK, LANES), lambda i: (back(i), FA // LANES)),
                  pl.BlockSpec((1, LANES), lambda i: (0, 0)), pl.BlockSpec(memory_space=pl.ANY)],
        out_specs=[pl.BlockSpec((BLK, 2 * LANES), lambda i: (back(i), FA // (2 * LANES))),
                   pl.BlockSpec((1, LANES), lambda i: (0, 0))],
        out_shape=[jax.ShapeDtypeStruct(dproj.shape, dproj.dtype), jax.ShapeDtypeStruct((1, LANES), F32)],
        scratch_shapes=[pltpu.VMEM((1, LANES), F32)],
        input_output_aliases={3: 0},
        compiler_params=_cparams(("arbitrary",)),
    )(dct, proj, jnp.pad(b_f, (0, LANES - HEADS)).reshape(1, LANES), dproj)
    return dp, db[0, :HEADS]


def _key_bias(L, T, ct=None):
    padb = jnp.where(jnp.arange(L) < PAD, BIG, 0.0).astype(F32)[None]
    kb = padb if ct is None else ct + padb
    return kb.reshape(kb.shape[0], L // T, 1, T)


def _layer_fwd(h, w, tabs, l, side=None):
    tag = f"l{l}"
    L = h.shape[0]
    hn = _rms_fwd(h, w["norm_g"], name=f"{tag}_rms_in")
    if "late" in w:
        late_job, assemble = w["late"]
        proj, *late = _mm(hn, w["w_in"], side=late_job, name=f"{tag}_mm_in")
        w = {**{key: val for key, val in w.items() if key != "late"}, **assemble(*late)}
    else:
        proj = _mm(hn, w["w_in"], name=f"{tag}_mm_in")
    kb_a = _forget_fwd(proj, w["b_f"], name=f"{tag}_forget").reshape(HEADS, L // ATT_T, 1, ATT_T)
    ops_a = (*_prep_a(proj, name=f"{tag}_prep_a"), kb_a)
    ya, lsea, *side_out = _attn_fwd(*ops_a, T=ATT_T, side=side, name=f"{tag}_attn_a")
    cqn = _rms_fwd(proj, w["g_cq"], col=CQ, name=f"{tag}_rms_cq")
    ckvn = _rms_fwd(proj, w["g_ckv"], col=CKV, name=f"{tag}_rms_ckv")
    qbm = _mm(cqn, w["w_uq"], name=f"{tag}_mm_uq")
    kvbm = _mm(ckvn, w["w_ukv"], name=f"{tag}_mm_ukv")
    ops_b = (*_prep_b(qbm, kvbm, proj, tabs["bk"], tabs["bq"], name=f"{tag}_prep_b"), _key_bias(L, ATT_T))
    yb, lseb = _attn_fwd(*ops_b, T=ATT_T, name=f"{tag}_attn_b")
    ops_c = (*_prep_c(proj, tabs["c"], name=f"{tag}_prep_c"), _key_bias(L, WINDOW))
    sink = jnp.broadcast_to(w["sinks"][:, None, None], (HEADS, 1, LANES))
    yc, lsec = _attn_fwd(*ops_c, T=WINDOW, window=True, sink=sink, name=f"{tag}_attn_c")
    us = [_gate(y, proj, zcol, name=f"{tag}_gate{n}") for n, (y, zcol) in enumerate(((ya, ZA), (yb, ZB), (yc, ZC)))]
    merged, pbr = _branch_merge(us, w["w_branch"], proj, name=f"{tag}_branch_merge")
    out = _mm(merged, w["w_out"], add=h, name=f"{tag}_mm_out")
    saved = dict(h=h, hn=hn, proj=proj, ops_a=ops_a, ya=ya, lsea=lsea, cqn=cqn, ckvn=ckvn,
                 ops_b=ops_b, yb=yb, lseb=lseb, ops_c=ops_c, yc=yc, lsec=lsec, us=us, pbr=pbr, merged=merged)
    return out, saved, side_out, w


def _w_in_chunks(g, tag):
    return _w_in_to_shards(g["w_in"], name=f"{tag}_w_in_chunks")


def _grad_chunks(g, tag, with_w_in=True):
    br = _cut_shards(g["w_branch"].reshape(N_BRANCH * BRANCH_W, D_MODEL), name=f"{tag}_w_branch_chunks")
    rest = [_cut_shards(g["w_uq"], name=f"{tag}_w_uq_chunks"), _cut_shards(g["w_ukv"], name=f"{tag}_w_ukv_chunks"),
            br.reshape(N_DEV, N_BRANCH, BRANCH_W, D_MODEL // N_DEV),
            g["w_out"].reshape(N_DEV, D_MODEL // N_DEV, D_MODEL)]
    return ([_w_in_chunks(g, tag)] if with_w_in else []) + rest


def _layer_bwd(dout, s, w, tabs, l, side=None, own_exchange=False):
    tag = f"l{l}"
    L = dout.shape[0]
    proj = s["proj"]
    g = {}
    dmerged = _mm(dout, w["w_out"], tb=True, name=f"{tag}_mm_out_dx")
    g["w_out"] = _mm(s["merged"], dout, ta=True, out_dtype=BF16, name=f"{tag}_mm_out_dw")
    dproj, dpbr = _merge_bwd(dmerged, proj, s["pbr"], name=f"{tag}_merge_bwd")
    dus = [_mm(dpbr[n], w["w_branch"][n], tb=True, name=f"{tag}_mm_br{n}_dx") for n in range(N_BRANCH)]
    g["w_branch"] = jnp.stack([_mm(s["us"][n], dpbr[n], ta=True, out_dtype=BF16, name=f"{tag}_mm_br{n}_dw")
                               for n in range(N_BRANCH)])
    dproj, dya, dla = _gate_bwd(dus[0], s["ya"], proj, ZA, dproj, delta_rows=True, name=f"{tag}_gate0_bwd")
    dproj, dyb, dlb = _gate_bwd(dus[1], s["yb"], proj, ZB, dproj, delta_rows=True, zero_tail=QKVC - ZB - BRANCH_W,
                                name=f"{tag}_gate1_bwd")
    dproj, dyc = _gate_bwd(dus[2], s["yc"], proj, ZC, dproj, name=f"{tag}_gate2_bwd")

    def bwd_operands(ops, lse, delta, dy16):
        q16, k16, v16, kbias = ops
        return (q16, k16, v16, kbias.reshape(kbias.shape[0], L, 1), lse.reshape(HEADS, L // ATT_T, 1, ATT_T), delta,
                dy16)

    dqa, dka, dva, dcq, dck, *side_out = _attn_bwd_t(*bwd_operands(s["ops_a"], s["lsea"], dla, dya), T=ATT_T,
                                                     fox=True, side=side, name=f"{tag}_attn_a_bwd")
    dproj = _unprep_a(dqa, dka, dva, dproj, name=f"{tag}_unprep_a")
    dproj, g["b_f"] = _forget_bwd(dcq.reshape(HEADS, L) + dck[:, :, 0], proj, w["b_f"], dproj,
                                  name=f"{tag}_forget_bwd")
    dqb, dkb, dvb, _, _ = _attn_bwd_t(*bwd_operands(s["ops_b"], s["lseb"], dlb, dyb), T=ATT_T,
                                      name=f"{tag}_attn_b_bwd")
    dproj, dqbm, dkvbm = _unprep_b(dqb, dkb, dvb, tabs["bk"], tabs["bq"], dproj, name=f"{tag}_unprep_b")
    dcqn = _mm(dqbm, w["w_uq"], tb=True, name=f"{tag}_mm_uq_dx")
    g["w_uq"] = _mm(s["cqn"], dqbm, ta=True, out_dtype=BF16, name=f"{tag}_mm_uq_dw")
    dckvn = _mm(dkvbm, w["w_ukv"], tb=True, name=f"{tag}_mm_ukv_dx")
    g["w_ukv"] = _mm(s["ckvn"], dkvbm, ta=True, out_dtype=BF16, name=f"{tag}_mm_ukv_dw")
    dproj, g["g_cq"] = _rms_bwd(dcqn, proj, w["g_cq"], col=CQ, into=dproj, name=f"{tag}_rms_cq_bwd")
    dproj, g["g_ckv"] = _rms_bwd(dckvn, proj, w["g_ckv"], col=CKV, into=dproj, name=f"{tag}_rms_ckv_bwd")
    dqc, dkc, dvc = _attn_bwd_window(*s["ops_c"], s["yc"], s["lsec"], dyc, name=f"{tag}_attn_c_bwd")
    dproj = _unprep_c(dqc, dkc, dvc, tabs["c"], dproj, name=f"{tag}_unprep_c")
    delta_c = jnp.sum(dyc * s["yc"], axis=-1)
    g["sinks"] = -jnp.sum(jnp.exp(w["sinks"][:, None] - s["lsec"][:, :, 0]) * delta_c, axis=1)
    own_out = []
    if own_exchange:
        g["w_in"], *r_rest = _mm(s["hn"].T, dproj, out_dtype=BF16, side=_Exchange(_grad_chunks(g, tag, False)),
                                 name=f"{tag}_mm_in_dw")
        dhn, r_in = _mm(dproj, w["w_in"], tb=True, side=_Exchange([_w_in_chunks(g, tag)]), name=f"{tag}_mm_in_dx")
        own_out = [r_in, *r_rest]
    else:
        g["w_in"] = _mm(s["hn"].T, dproj, out_dtype=BF16, name=f"{tag}_mm_in_dw")
        dhn = _mm(dproj, w["w_in"], tb=True, name=f"{tag}_mm_in_dx")
    dh, g["norm_g"] = _rms_bwd(dhn, s["h"], w["norm_g"], add=dout, name=f"{tag}_rms_in_bwd")
    return dh, g, side_out, own_out


def _cols_from_shards(g):
    return jnp.moveaxis(g, 0, 1).reshape(g.shape[1], N_DEV * g.shape[2])


def _join_shards(g, *, name):
    _, R, C = g.shape
    tr = _pick(R, (512, 384, 256))

    def body(g_ref, o_ref):
        for d in range(N_DEV):
            o_ref[:, C * d:C * (d + 1)] = g_ref[d]

    return pl.pallas_call(
        body, name=name, grid=(R // tr,),
        in_specs=[pl.BlockSpec((N_DEV, tr, C), lambda i: (0, i, 0))],
        out_specs=pl.BlockSpec((tr, N_DEV * C), lambda i: (i, 0)),
        out_shape=jax.ShapeDtypeStruct((R, N_DEV * C), g.dtype),
        compiler_params=_cparams(("parallel",)),
    )(g)


def _cut_shards(w, *, name):
    R = w.shape[0]
    C = w.shape[1] // N_DEV
    tr = _pick(R, (512, 384, 256))

    def body(w_ref, o_ref):
        for d in range(N_DEV):
            o_ref[d] = w_ref[:, C * d:C * (d + 1)]

    return pl.pallas_call(
        body, name=name, grid=(R // tr,),
        in_specs=[pl.BlockSpec((tr, N_DEV * C), lambda i: (i, 0))],
        out_specs=pl.BlockSpec((N_DEV, tr, C), lambda i: (0, i, 0)),
        out_shape=jax.ShapeDtypeStruct((N_DEV, R, C), w.dtype),
        compiler_params=_cparams(("parallel",)),
    )(w)


W_IN_SHARD = N_IN // N_DEV


def _w_in_pieces():
    pieces = []
    for lo, hi, dst in _RUNS:
        for d in range(N_DEV):
            a, b = max(lo, d * W_IN_SHARD), min(hi, (d + 1) * W_IN_SHARD)
            if a < b:
                pieces.append((d, a - d * W_IN_SHARD, b - d * W_IN_SHARD, dst + a - lo))
    return pieces


def _w_in_from_shards(g, *, name):
    R = g.shape[1]
    tr = 256
    covered = sorted((dst, dst + b - a) for _, a, b, dst in _w_in_pieces())

    def body(g_ref, o_ref):
        at = 0
        for lo, hi in covered + [(NP_IN, NP_IN)]:
            if lo > at:
                o_ref[:, at:lo] = jnp.zeros((tr, lo - at), BF16)
            at = max(at, hi)
        for d, a, b, dst in _w_in_pieces():
            o_ref[:, dst:dst + b - a] = g_ref[d, :, a:b]

    return pl.pallas_call(
        body, name=name, grid=(R // tr,),
        in_specs=[pl.BlockSpec((N_DEV, tr, W_IN_SHARD), lambda i: (0, i, 0))],
        out_specs=pl.BlockSpec((tr, NP_IN), lambda i: (i, 0)),
        out_shape=jax.ShapeDtypeStruct((R, NP_IN), BF16),
        compiler_params=_cparams(("parallel",)),
    )(g)


def _w_in_to_shards(dw, *, name):
    R = dw.shape[0]
    tr = 256

    def body(w_ref, o_ref):
        for d, a, b, dst in _w_in_pieces():
            o_ref[d, :, a:b] = w_ref[:, dst:dst + b - a]

    return pl.pallas_call(
        body, name=name, grid=(R // tr,),
        in_specs=[pl.BlockSpec((tr, NP_IN), lambda i: (i, 0))],
        out_specs=pl.BlockSpec((N_DEV, tr, W_IN_SHARD), lambda i: (0, i, 0)),
        out_shape=jax.ShapeDtypeStruct((N_DEV, R, W_IN_SHARD), BF16),
        compiler_params=_cparams(("parallel",)),
    )(dw)


_SMALL = (("norm_g", DEPTH * D_MODEL), ("b_f", DEPTH * HEADS), ("g_cq", DEPTH * MLA_QLORA),
          ("g_ckv", DEPTH * MLA_KVLORA), ("sinks", DEPTH * HEADS), ("final_g", D_MODEL), ("loss", 1),
          ("meta", N_META * D_MODEL))
SMALL_ROWS = 168


def _pack_small(d):
    parts = []
    for name, size in _SMALL:
        padded = -(-size // 128) * 128
        v = d[name].reshape(-1).astype(F32) if name in d else jnp.zeros((size,), F32)
        parts.append(jnp.pad(v, (0, padded - size)))
    flat = jnp.concatenate(parts)
    return jnp.pad(flat, (0, SMALL_ROWS * 128 - flat.shape[0])).reshape(SMALL_ROWS, 128)


def _unpack_small(p, shapes):
    flat = p.reshape(-1)
    out, at = {}, 0
    for name, size in _SMALL:
        if name in shapes:
            out[name] = flat[at:at + size].reshape(shapes[name])
        at += -(-size // 128) * 128
    return out


def kernel(x, meta_tokens, norm_g, w_in, b_f, g_cq, g_ckv, w_uq, w_ukv, sinks, w_branch, w_out, final_g, loss_target, m_meta_tokens, m_norm_g, m_w_in, m_b_f, m_g_cq, m_g_ckv, m_w_uq, m_w_ukv, m_sinks, m_w_branch, m_w_out, m_final_g, v_meta_tokens, v_norm_g, v_w_in, v_b_f, v_g_cq, v_g_ckv, v_w_uq, v_w_ukv, v_sinks, v_w_branch, v_w_out, v_final_g):
    S = x.shape[1]
    L = BLK + S
    cx, cy, cc = _me()
    my_idx = 4 * cx + 2 * cy + cc

    def shards(l):
        return [t[l].astype(BF16) for t in (w_in, w_uq, w_ukv, w_branch, w_out)]

    def small_weights(l):
        return dict(norm_g=norm_g[l], b_f=b_f[l], g_cq=g_cq[l], g_ckv=g_ckv[l], sinks=sinks[l])

    def rest_weights(l):
        def assemble(gw_uq, gw_ukv, gw_br, gw_out):
            br = _join_shards(gw_br.reshape(N_DEV, N_BRANCH * BRANCH_W, D_MODEL // N_DEV), name=f"l{l}_w_branch_full")
            return dict(w_uq=_join_shards(gw_uq, name=f"l{l}_w_uq_full"),
                        w_ukv=_join_shards(gw_ukv, name=f"l{l}_w_ukv_full"),
                        w_branch=br.reshape(N_BRANCH, BRANCH_W, D_MODEL), w_out=gw_out.reshape(D_MODEL, D_MODEL))
        return assemble

    def layer_weights(l, gw_in, *gw_rest):
        return dict(small_weights(l), w_in=_w_in_from_shards(gw_in, name=f"l{l}_w_in_full"),
                    **rest_weights(l)(*gw_rest))

    gw_in0, g_meta = _comm(_Gather([shards(0)[0], meta_tokens]), name="gather_l0")
    layers = [dict(small_weights(0), w_in=_w_in_from_shards(gw_in0, name="l0_w_in_full"),
                   late=(_Gather(shards(0)[1:]), rest_weights(0))), None]
    meta_full = _cols_from_shards(g_meta)

    h = jnp.concatenate([jnp.zeros((PAD, D_MODEL), F32), meta_full, x[0]], axis=0)
    q_rotary = ((jnp.arange(HEADS * DQK_B) % DQK_B) >= MLA_NOPE).astype(F32)[None, :]
    tabs = dict(c=_rope_tables(L, DH // 2, LANES), bk=_rope_tables(L, MLA_ROPE // 2, LANES), bq=q_rotary)
    saved = [None] * DEPTH
    h, saved[0], gw1, layers[0] = _layer_fwd(h, layers[0], tabs, 0, side=_Gather(shards(1)))
    layers[1] = layer_weights(1, *gw1)
    h, saved[1], _, _ = _layer_fwd(h, layers[1], tabs, 1)
    loss_vec, dh, g_final = _loss_head(h, final_g, loss_target[0], name="loss_head")

    grads = [None] * DEPTH
    dh, grads[1], _, _ = _layer_bwd(dh, saved[1], layers[1], tabs, 1)
    dh, grads[0], recv1, recv0 = _layer_bwd(dh, saved[0], layers[0], tabs, 0, side=_Exchange(_grad_chunks(grads[1], "l1")),
                                            own_exchange=True)
    r_in, r_uq, r_ukv, r_br, r_out = (jnp.stack([a, b], axis=1) for a, b in zip(recv0, recv1))

    def stack(name):
        return jnp.stack([grads[l][name] for l in range(DEPTH)])

    small = _pack_small(dict(norm_g=stack("norm_g"), b_f=stack("b_f"), g_cq=stack("g_cq"), g_ckv=stack("g_ckv"),
                             sinks=stack("sinks"), final_g=g_final, loss=loss_vec[0, 0:1],
                             meta=dh[PAD:BLK]))
    (g_small,) = _comm(_Gather([small]), name="gather_small")

    def adam_big(w_, parts, m_, v_, name):
        shape = w_.shape
        C = shape[-1]
        R = math.prod(shape[:-1])
        outs = _adamw(w_.reshape(R, C), parts.reshape(parts.shape[0], R, C), m_.reshape(R, C), v_.reshape(R, C),
                      name=name)
        return [o.reshape(shape) for o in outs]

    res = {}
    res["w_in"] = adam_big(w_in, r_in, m_w_in, v_w_in, "adam_w_in")
    res["w_uq"] = adam_big(w_uq, r_uq, m_w_uq, v_w_uq, "adam_w_uq")
    res["w_ukv"] = adam_big(w_ukv, r_ukv, m_w_ukv, v_w_ukv, "adam_w_ukv")
    res["w_branch"] = adam_big(w_branch, r_br, m_w_branch, v_w_branch, "adam_w_branch")
    res["w_out"] = adam_big(w_out, r_out, m_w_out, v_w_out, "adam_w_out")

    small_w = dict(norm_g=norm_g, b_f=b_f, g_cq=g_cq, g_ckv=g_ckv, sinks=sinks, final_g=final_g)
    small_m = dict(norm_g=m_norm_g, b_f=m_b_f, g_cq=m_g_cq, g_ckv=m_g_ckv, sinks=m_sinks, final_g=m_final_g)
    small_v = dict(norm_g=v_norm_g, b_f=v_b_f, g_cq=v_g_cq, g_ckv=v_g_ckv, sinks=v_sinks, final_g=v_final_g)
    sm = _adamw(_pack_small(small_w), g_small, _pack_small(small_m), _pack_small(small_v), name="adam_small")
    shapes = {k: a.shape for k, a in small_w.items()}
    shapes_all = dict(shapes, loss=(), meta=(N_META, D_MODEL))
    sm_g = _unpack_small(sm[0], shapes_all)
    sm_d, sm_m, sm_v = (_unpack_small(t, shapes) for t in sm[1:])
    for k in shapes:
        res[k] = [sm_g[k], sm_d[k], sm_m[k], sm_v[k]]
    g_meta_mine = lax.dynamic_slice(sm_g["meta"], (0, my_idx * 128), (N_META, 128))
    res["meta_tokens"] = _adamw(meta_tokens, g_meta_mine[None], m_meta_tokens, v_meta_tokens, name="adam_meta")

    order = ["meta_tokens", "norm_g", "w_in", "b_f", "g_cq", "g_ckv", "w_uq", "w_ukv", "sinks", "w_branch", "w_out",
             "final_g"]
    grad_x = dh[BLK:][None]
    return (sm_g["loss"], grad_x, *[res[k][0] for k in order], *[res[k][1] for k in order],
            *[res[k][2] for k in order], *[res[k][3] for k in order])
```

```python
import math

import jax
import jax.numpy as jnp
from jax import lax
from jax.experimental import pallas as pl
from jax.experimental.pallas import tpu as pltpu

F32 = jnp.float32
BF16 = jnp.bfloat16

D_MODEL = 1024
DEPTH = 2
N_META = 16
BLK = 128
PAD = BLK - N_META
ROPE_THETA = 10000.0
EPS = 1e-6
NEG = -1e30
BIG = 1e30
HEADS = 8
DH = 64
MLA_NOPE = 64
MLA_ROPE = 32
MLA_QLORA = 384
MLA_KVLORA = 256
SWA_KV_HEADS = 2
WINDOW = 128
BRANCH_W = 512
N_BRANCH = 3
N_IN = 7592

ADAM_LR = 0.001
ADAM_B1 = 0.9
ADAM_B2 = 0.999
ADAM_EPS = 1e-08
ADAM_WD = 0.01
ADAM_STEP = 10

N_DEV = 8
MESH = pl.DeviceIdType.MESH

NP_IN = 8192
QKVA, ZA, FA = 0, 1536, 2048
CKV, KR, CQ = 2304, 2560, 2688
GATES = 3072
ZB = 6144
QKVC, ZC = 6912, 7680
_RUNS = ((0, 1536, QKVA), (1536, 1544, FA), (1544, 2056, ZA), (2056, 2440, CQ), (2440, 2696, CKV), (2696, 2728, KR),
         (2728, 3240, ZB), (3240, 4008, QKVC), (4008, 4520, ZC), (4520, 7592, GATES))

VMEM_LIMIT = 48 * 1024 * 1024
ATT_T = 384
ATT_HP = 1
ROW_T = 384
LANES = 128


def _pick(dim, prefs):
    for p in prefs:
        if dim % p == 0:
            return p
    return dim


def _cparams(sem):
    return pltpu.CompilerParams(dimension_semantics=sem, vmem_limit_bytes=VMEM_LIMIT)


def _side_cparams(side, sem):
    if side is None:
        return _cparams(sem)
    return pltpu.CompilerParams(dimension_semantics=("arbitrary",) * len(sem), vmem_limit_bytes=VMEM_LIMIT,
                                has_side_effects=True)


def _mm(a, b, *, ta=False, tb=False, add=None, out_dtype=F32, side=None, name):
    M = a.shape[1] if ta else a.shape[0]
    K = a.shape[0] if ta else a.shape[1]
    N = b.shape[0] if tb else b.shape[1]
    assert K == (b.shape[1] if tb else b.shape[0])
    tm = _pick(M, (704, 1024, 512, 384, 256, 128))
    tn = _pick(N, (1024, 768, 512, 384, 256, 128))
    tk = _pick(K, (4096, 1408, 1024, 768, 512, 384, 256, 128))
    nk = K // tk
    dims = (((0 if ta else 1,), (1 if tb else 0,)), ((), ()))

    sn = 0 if side is None else side.n
    n_in = 2 + (add is not None)
    grid = (M // tm, N // tn, nk)

    def body(*refs):
        a_ref, b_ref = refs[:2]
        c_ref = refs[2] if add is not None else None
        o_ref = refs[n_in + sn]
        scratch = refs[n_in + 2 * sn + 1:]
        if side is not None:
            step = (pl.program_id(0) * grid[1] + pl.program_id(1)) * nk + pl.program_id(2)
            _ride(side, [*refs[n_in:n_in + sn], *refs[n_in + sn + 1:n_in + 2 * sn + 1], *scratch[nk > 1:]],
                  step, grid[0] * grid[1] * nk)
        r = lax.dot_general(a_ref[...].astype(BF16), b_ref[...].astype(BF16), dims, preferred_element_type=F32)

        def finish(total):
            if c_ref is not None:
                total = total + c_ref[...]
            o_ref[...] = total.astype(out_dtype)

        if nk == 1:
            finish(r)
        else:
            acc = scratch[0]
            k = pl.program_id(2)

            @pl.when(k == 0)
            def _():
                acc[...] = r

            @pl.when(k > 0)
            def _():
                acc[...] += r

            @pl.when(k == nk - 1)
            def _():
                finish(acc[...])

    a_spec = pl.BlockSpec((tk, tm), lambda i, j, k: (k, i)) if ta else pl.BlockSpec((tm, tk), lambda i, j, k: (i, k))
    b_spec = pl.BlockSpec((tn, tk), lambda i, j, k: (j, k)) if tb else pl.BlockSpec((tk, tn), lambda i, j, k: (k, j))
    o_spec = pl.BlockSpec((tm, tn), lambda i, j, k: (i, j))
    any_spec = pl.BlockSpec(memory_space=pl.ANY)
    outs = pl.pallas_call(
        body, name=name,
        grid=grid,
        in_specs=[a_spec, b_spec] + ([o_spec] if add is not None else []) + [any_spec] * sn,
        out_specs=[o_spec] + [any_spec] * sn,
        out_shape=[jax.ShapeDtypeStruct((M, N), out_dtype)] + ([] if side is None else side.out_shape),
        scratch_shapes=([pltpu.VMEM((tm, tn), F32)] if nk > 1 else []) + ([] if side is None else side.scratch),
        compiler_params=_side_cparams(side, ("parallel", "parallel", "arbitrary")),
    )(*((a, b) if add is None else (a, b, add)), *([] if side is None else side.arrs))
    return outs[0] if side is None else outs


def _col_spec(tm, width, col):
    assert col % width == 0
    return pl.BlockSpec((tm, width), lambda i, _c=col // width: (i, _c))


def _rms_fwd(x, g, *, col=0, name):
    L = x.shape[0]
    D = g.shape[0]
    tm = ROW_T

    def body(x_ref, g_ref, y_ref):
        xv = x_ref[...]
        rstd = lax.rsqrt(jnp.mean(xv * xv, axis=-1, keepdims=True) + EPS)
        y_ref[...] = (xv * rstd * g_ref[...]).astype(BF16)

    return pl.pallas_call(
        body, name=name, grid=(L // tm,),
        in_specs=[_col_spec(tm, D, col), pl.BlockSpec((1, D), lambda i: (0, 0))],
        out_specs=pl.BlockSpec((tm, D), lambda i: (i, 0)),
        out_shape=jax.ShapeDtypeStruct((L, D), BF16),
        compiler_params=_cparams(("parallel",)),
    )(x, g.reshape(1, D))


def _rms_bwd(dy, x, g, *, col=0, add=None, into=None, name):
    L = x.shape[0]
    D = g.shape[0]
    tm = ROW_T

    def body(*refs):
        dy_ref, x_ref, g_ref = refs[:3]
        add_ref = refs[3] if add is not None else None
        dx_ref, dg_ref = refs[-2:]
        i = pl.program_id(0)
        xv = x_ref[...]
        dyv = dy_ref[...]
        rstd = lax.rsqrt(jnp.mean(xv * xv, axis=-1, keepdims=True) + EPS)
        xhat = xv * rstd
        part = jnp.sum(dyv * xhat, axis=0, keepdims=True)

        @pl.when(i == 0)
        def _():
            dg_ref[...] = part

        @pl.when(i > 0)
        def _():
            dg_ref[...] += part

        dxh = dyv * g_ref[...]
        dx = rstd * (dxh - xhat * jnp.mean(dxh * xhat, axis=-1, keepdims=True))
        if add_ref is not None:
            dx = dx + add_ref[...]
        dx_ref[...] = dx.astype(dx_ref.dtype)

    row = pl.BlockSpec((tm, D), lambda i: (i, 0))
    in_specs = [row, _col_spec(tm, D, col), pl.BlockSpec((1, D), lambda i: (0, 0))]
    args = [dy, x, g.reshape(1, D)]
    aliases = {}
    if add is not None:
        in_specs.append(row)
        args.append(add)
    if into is not None:
        in_specs.append(pl.BlockSpec(memory_space=pl.ANY))
        args.append(into)
        aliases = {len(args) - 1: 0}
        dx_spec, dx_shape = _col_spec(tm, D, col), jax.ShapeDtypeStruct(into.shape, into.dtype)
    else:
        dx_spec, dx_shape = row, jax.ShapeDtypeStruct((L, D), F32)
    dx, dg = pl.pallas_call(
        body, name=name, grid=(L // tm,),
        in_specs=in_specs,
        out_specs=[dx_spec, pl.BlockSpec((1, D), lambda i: (0, 0))],
        out_shape=[dx_shape, jax.ShapeDtypeStruct((1, D), F32)],
        input_output_aliases=aliases,
        compiler_params=_cparams(("arbitrary",)),
    )(*args)
    return dx, dg.reshape(D)


def _loss_head(h, g, target, *, name):
    L, D = h.shape
    nb = L // BLK

    def body(h_ref, g_ref, t_ref, loss_ref, dh_ref, dg_ref):
        i = pl.program_id(0)

        @pl.when(i == 0)
        def _():
            loss_ref[...] = jnp.zeros_like(loss_ref)
            dg_ref[...] = jnp.zeros_like(dg_ref)
            dh_ref[...] = jnp.zeros_like(dh_ref)

        @pl.when(i > 0)
        def _():
            xv = h_ref[...]
            gv = g_ref[...]
            rstd = lax.rsqrt(jnp.mean(xv * xv, axis=-1, keepdims=True) + EPS)
            xhat = xv * rstd
            err = xhat * gv - t_ref[...]
            row = jnp.mean(err * err, axis=-1, keepdims=True)
            loss_ref[...] += 0.5 * jnp.sum(row, axis=0, keepdims=True)
            dy = err * (1.0 / D)
            dg_ref[...] += jnp.sum(dy * xhat, axis=0, keepdims=True)
            dxh = dy * gv
            dh_ref[...] = rstd * (dxh - xhat * jnp.mean(dxh * xhat, axis=-1, keepdims=True))

    loss, dh, dg = pl.pallas_call(
        body, name=name, grid=(nb,),
        in_specs=[pl.BlockSpec((BLK, D), lambda i: (i, 0)), pl.BlockSpec((1, D), lambda i: (0, 0)),
                  pl.BlockSpec((BLK, D), lambda i: (jnp.maximum(i - 1, 0), 0))],
        out_specs=[pl.BlockSpec((1, 128), lambda i: (0, 0)), pl.BlockSpec((BLK, D), lambda i: (i, 0)),
                   pl.BlockSpec((1, D), lambda i: (0, 0))],
        out_shape=[jax.ShapeDtypeStruct((1, 128), F32), jax.ShapeDtypeStruct((L, D), F32),
                   jax.ShapeDtypeStruct((1, D), F32)],
        compiler_params=_cparams(("arbitrary",)),
    )(h, g.reshape(1, D), target)
    return loss, dh, dg.reshape(D)


_NT = (((1,), (1,)), ((), ()))
_TN = (((0,), (0,)), ((), ()))


def _attn_fwd(q, k, v, kbias, *, T, window=False, sink=None, side=None, name):
    sn = 0 if side is None else side.n
    H, L, dk = q.shape
    Hkv = k.shape[0]
    dvx = v.shape[2]
    dv = dvx // 2
    G = H // Hkv
    nt = L // T
    Hb = kbias.shape[0]
    reps = T // LANES
    assert not window or T == WINDOW
    HP = G if G > 1 else ATT_HP
    NS = 1 if G > 1 else HP
    R = HP * T // NS
    HKV = HP // G
    HB = HP if Hb > 1 else 1
    assert G == 1 or Hb == 1

    def body(*refs):
        q_ref, k_ref, v_ref, kb_ref = refs[:4]
        n = 4
        if sink is not None:
            sk_ref = refs[n]
            n += 1
        side_in = refs[n:n + sn]
        n += sn
        o_ref, lse_ref = refs[n:n + 2]
        side_out = refs[n + 2:n + 2 + sn]
        n += 2 + sn
        m_scs, acc_scs, buf_a, buf_b = (refs[n + t * NS:n + (t + 1) * NS] for t in range(4))
        i = pl.program_id(1)
        if side is not None:
            _ride(side, [*side_in, *side_out, *refs[n + 4 * NS:]], pl.program_id(0) * nt + i, (H // HP) * nt)
        for a in range(NS):
            if sink is not None:
                sk = [jnp.broadcast_to(sk_ref[b, :, 0:1], (T, LANES)) for b in range(HP)]
                m_scs[a][...] = jnp.concatenate(sk, axis=0) if G > 1 else sk[a]
                lane = lax.broadcasted_iota(jnp.int32, (R, dvx), 1)
                acc_scs[a][...] = jnp.where(lane >= dv, 1.0, 0.0)
            else:
                m_scs[a][...] = jnp.full((R, LANES), NEG, F32)
                acc_scs[a][...] = jnp.zeros((R, dvx), F32)
        row = lax.broadcasted_iota(jnp.int32, (R, T), 0) & (T - 1) if G > 1 else \
            lax.broadcasted_iota(jnp.int32, (R, T), 0)
        col = lax.broadcasted_iota(jnp.int32, (R, T), 1)

        def logits(a, j):
            rows = pl.ds(pl.multiple_of(j * T, T), T)
            qv = q_ref[...].reshape(R, dk) if G > 1 else q_ref[a]
            s = lax.dot_general(qv, k_ref[a, rows, :], _NT, preferred_element_type=F32)
            return s - kb_ref[a if HB > 1 else 0, j]

        def update(a, s, j, kind):
            rows = pl.ds(pl.multiple_of(j * T, T), T)
            m_sc, acc_sc = m_scs[a], acc_scs[a]
            if kind == "diag":
                s = jnp.where(row >= col, s, NEG)
            elif kind == "prev":
                s = jnp.where((col > row) & (i > 0), s, NEG)
            m_prev = m_sc[...]
            m_new = jnp.maximum(m_prev, jnp.max(s, axis=1, keepdims=True))
            alpha = jnp.exp(m_prev - m_new)
            p = jnp.exp(s - jnp.tile(m_new, (1, reps)))
            acc_sc[...] = alpha[:, :dvx] * acc_sc[...] + jnp.dot(p.astype(BF16), v_ref[a, rows, :],
                                                                 preferred_element_type=F32)
            m_sc[...] = m_new

        if window:
            before = jnp.maximum(i - 1, 0)
            s_prev = [logits(a, before) for a in range(NS)]
            s_diag = [logits(a, i) for a in range(NS)]
            for a in range(NS):
                update(a, s_prev[a], before, "prev")
                update(a, s_diag[a], i, "diag")
        else:
            def fill(buf, j):
                for a in range(NS):
                    buf[a][...] = logits(a, j)

            def drain(buf, j, kind):
                for a in range(NS):
                    update(a, buf[a][...], j, kind)

            fill(buf_a, 0)

            def pair(t, c):
                fill(buf_b, 2 * t + 1)
                drain(buf_a, 2 * t, "full")
                fill(buf_a, 2 * t + 2)
                drain(buf_b, 2 * t + 1, "full")
                return c

            lax.fori_loop(0, i // 2, pair, 0)

            @pl.when(i % 2 == 1)
            def _():
                fill(buf_b, i)
                drain(buf_a, i - 1, "full")
                drain(buf_b, i, "diag")

            @pl.when(i % 2 == 0)
            def _():
                drain(buf_a, i, "diag")

        for a in range(NS):
            acc = acc_scs[a][...]
            ov = acc[:, :dv] / acc[:, dv:]
            lsev = m_scs[a][:, 0:1] + jnp.log(acc[:, dv:dv + 1])
            if G > 1:
                o_ref[...] = ov.reshape(HP, T, dv)
                lse_ref[...] = lsev.reshape(HP, T, 1)
            else:
                o_ref[a] = ov
                lse_ref[a] = lsev

    in_specs = [pl.BlockSpec((HP, T, dk), lambda h, i: (h, i, 0)),
                pl.BlockSpec((HKV, L, dk), lambda h, i: (h, 0, 0)),
                pl.BlockSpec((HKV, L, dvx), lambda h, i: (h, 0, 0)),
                pl.BlockSpec((HB, nt, 1, T), lambda h, i: (h if Hb > 1 else 0, 0, 0, 0))]
    args = [q, k, v, kbias]
    if sink is not None:
        in_specs += [pl.BlockSpec((HP, 1, LANES), lambda h, i: (h, 0, 0))]
        args += [sink]
    any_spec = pl.BlockSpec(memory_space=pl.ANY)
    return pl.pallas_call(
        body, name=name, grid=(H // HP, nt),
        in_specs=in_specs + [any_spec] * sn,
        out_specs=[pl.BlockSpec((HP, T, dv), lambda h, i: (h, i, 0)),
                   pl.BlockSpec((HP, T, 1), lambda h, i: (h, i, 0))] + [any_spec] * sn,
        out_shape=[jax.ShapeDtypeStruct((H, L, dv), F32), jax.ShapeDtypeStruct((H, L, 1), F32)]
        + ([] if side is None else side.out_shape),
        scratch_shapes=[pltpu.VMEM((R, LANES), F32)] * NS + [pltpu.VMEM((R, dvx), F32)] * NS
        + [pltpu.VMEM((R, T), F32)] * (2 * NS) + ([] if side is None else side.scratch),
        compiler_params=_side_cparams(side, ("parallel", "arbitrary")),
    )(*args, *([] if side is None else side.arrs))


def _attn_bwd_t(q, k, v, kbias_col, lse_row, delta_row, do, *, T, fox=False, side=None, name):
    sn = 0 if side is None else side.n
    H, L, dk = q.shape
    dv = do.shape[2]
    nt = L // T
    Hb = kbias_col.shape[0]

    def body(*refs):
        q_ref, k_ref, v_ref, kb_ref, lse_ref, dl_ref, do_ref = refs[:7]
        side_in = refs[7:7 + sn]
        n = 7 + sn
        dq_ref, dk_ref, dv_ref = refs[n:n + 3]
        n += 3
        if fox:
            dcq_ref, dck_ref = refs[n:n + 2]
            n += 2
        side_out = refs[n:n + sn]
        n += sn
        buf_a, buf_b = refs[n:n + 2], refs[n + 2:n + 4]
        j = pl.program_id(1)
        if side is not None:
            _ride(side, [*side_in, *side_out, *refs[n + 4:]], pl.program_id(0) * nt + j, H * nt)

        @pl.when(j == 0)
        def _():
            dq_ref[...] = jnp.zeros_like(dq_ref)
            if fox:
                dcq_ref[...] = jnp.zeros_like(dcq_ref)

        dk_ref[...] = jnp.zeros_like(dk_ref)
        dv_ref[...] = jnp.zeros_like(dv_ref)
        if fox:
            dck_ref[...] = jnp.zeros_like(dck_ref)
        kb = k_ref[0]
        vb = v_ref[0][:, :dv]
        kbias_j = kb_ref[0]
        key = lax.broadcasted_iota(jnp.int32, (T, T), 0)
        qry = lax.broadcasted_iota(jnp.int32, (T, T), 1)

        def fill(buf, i):
            rows = pl.ds(pl.multiple_of(i * T, T), T)
            buf[0][...] = lax.dot_general(kb, q_ref[0, rows, :], _NT, preferred_element_type=F32) - kbias_j
            buf[1][...] = lax.dot_general(vb, do_ref[0, rows, :], _NT, preferred_element_type=F32)

        def drain(buf, i, kind):
            rows = pl.ds(pl.multiple_of(i * T, T), T)
            st = buf[0][...]
            if kind == "diag":
                st = jnp.where(key <= qry, st, NEG)
            pt = jnp.exp(st - lse_ref[0, i])
            dst = pt * (buf[1][...] - dl_ref[0, i])
            dsb = dst.astype(BF16)
            dv_ref[0] += jnp.dot(pt.astype(BF16), do_ref[0, rows, :], preferred_element_type=F32)
            dk_ref[0] += jnp.dot(dsb, q_ref[0, rows, :], preferred_element_type=F32)
            dq_ref[0, rows, :] += lax.dot_general(dsb, kb, _TN, preferred_element_type=F32)
            if fox:
                dcq_ref[0, i] += jnp.sum(dst, axis=0, keepdims=True)
                dck_ref[0] += -jnp.sum(dst, axis=1, keepdims=True)

        first = j + 1
        rest = nt - first
        fill(buf_a, j)
        fill(buf_b, jnp.minimum(first, nt - 1))
        drain(buf_a, j, "diag")

        def pair(t, c):
            i0 = first + 2 * t
            fill(buf_a, i0 + 1)
            drain(buf_b, i0, "full")
            fill(buf_b, jnp.minimum(i0 + 2, nt - 1))
            drain(buf_a, i0 + 1, "full")
            return c

        lax.fori_loop(0, rest // 2, pair, 0)

        @pl.when(rest % 2 == 1)
        def _():
            drain(buf_b, nt - 1, "full")

    rows_spec = pl.BlockSpec((1, nt, 1, T), lambda h, j: (h, 0, 0, 0))
    in_specs = [pl.BlockSpec((1, L, dk), lambda h, j: (h, 0, 0)),
                pl.BlockSpec((1, T, dk), lambda h, j: (h, j, 0)),
                pl.BlockSpec((1, T, v.shape[2]), lambda h, j: (h, j, 0)),
                pl.BlockSpec((1, T, 1), lambda h, j: (h if Hb > 1 else 0, j, 0)),
                rows_spec, rows_spec,
                pl.BlockSpec((1, L, dv), lambda h, j: (h, 0, 0))]
    out_specs = [pl.BlockSpec((1, L, dk), lambda h, j: (h, 0, 0)),
                 pl.BlockSpec((1, T, dk), lambda h, j: (h, j, 0)),
                 pl.BlockSpec((1, T, dv), lambda h, j: (h, j, 0))]
    out_shape = [jax.ShapeDtypeStruct((H, L, dk), F32), jax.ShapeDtypeStruct((H, L, dk), F32),
                 jax.ShapeDtypeStruct((H, L, dv), F32)]
    if fox:
        out_specs += [rows_spec, pl.BlockSpec((1, T, 1), lambda h, j: (h, j, 0))]
        out_shape += [jax.ShapeDtypeStruct((H, nt, 1, T), F32), jax.ShapeDtypeStruct((H, L, 1), F32)]
    any_spec = pl.BlockSpec(memory_space=pl.ANY)
    outs = pl.pallas_call(
        body, name=name, grid=(H, nt),
        in_specs=in_specs + [any_spec] * sn, out_specs=out_specs + [any_spec] * sn,
        out_shape=out_shape + ([] if side is None else side.out_shape),
        scratch_shapes=[pltpu.VMEM((T, T), F32)] * 4 + ([] if side is None else side.scratch),
        compiler_params=_side_cparams(side, ("parallel", "arbitrary")),
    )(q, k, v, kbias_col, lse_row, delta_row, do, *([] if side is None else side.arrs))
    main, rest = outs[:len(outs) - sn], outs[len(outs) - sn:]
    return (*(main if fox else (*main, None, None)), *rest)


def _attn_bwd_window(q, k, v, kbias, o, lse, do, *, name):
    H, L, dk = q.shape
    Hkv = k.shape[0]
    dv = do.shape[2]
    G = H // Hkv
    T = WINDOW
    nt = L // T
    R = G * T

    def body(q_ref, kc_ref, kp_ref, vc_ref, vp_ref, kb_ref, o_ref, lse_ref, do_ref, dq_ref, dk_ref, dv_ref,
             dk_sc, dv_sc):
        i = pl.program_id(1)
        row = lax.broadcasted_iota(jnp.int32, (R, T), 0) & (T - 1)
        col = lax.broadcasted_iota(jnp.int32, (R, T), 1)

        @pl.when(i == 0)
        def _():
            dk_sc[...] = jnp.zeros_like(dk_sc)
            dv_sc[...] = jnp.zeros_like(dv_sc)

        @pl.when(i == nt)
        def _():
            dk_ref[0] = dk_sc[...]
            dv_ref[0] = dv_sc[...]

        @pl.when(i < nt)
        def _():
            qb = q_ref[...].reshape(R, dk)
            dof = do_ref[...].reshape(R, dv)
            dob = dof.astype(BF16)
            lse_c = lse_ref[...].reshape(R, 1)
            delta = jnp.sum(dof * o_ref[...].reshape(R, dv), axis=1, keepdims=True)

            def grads(kt, vt, kbias_j, mask):
                s = lax.dot_general(qb, kt, _NT, preferred_element_type=F32) - kbias_j
                p = jnp.exp(jnp.where(mask, s, NEG) - lse_c)
                dp = lax.dot_general(dob, vt, _NT, preferred_element_type=F32)
                ds = (p * (dp - delta)).astype(BF16)
                return (jnp.dot(ds, kt, preferred_element_type=F32),
                        lax.dot_general(ds, qb, _TN, preferred_element_type=F32),
                        lax.dot_general(p.astype(BF16), dob, _TN, preferred_element_type=F32))

            ip = jnp.maximum(i - 1, 0)
            dq_p, dk_p, dv_p = grads(kp_ref[0], vp_ref[0][:, :dv], kb_ref[0, ip], (col > row) & (i > 0))
            dq_c, dk_c, dv_c = grads(kc_ref[0], vc_ref[0][:, :dv], kb_ref[0, i], row >= col)
            dq_ref[...] = (dq_p + dq_c).reshape(G, T, dk)
            dk_ref[0] = dk_sc[...] + dk_p
            dv_ref[0] = dv_sc[...] + dv_p
            dk_sc[...] = dk_c
            dv_sc[...] = dv_c

    def cur(i):
        return jnp.minimum(i, nt - 1)

    def prev(i):
        return jnp.maximum(jnp.minimum(i, nt - 1) - 1, 0)

    def written(i):
        return jnp.maximum(i - 1, 0)

    qs = lambda d: pl.BlockSpec((G, T, d), lambda h, i: (h, cur(i), 0))
    return pl.pallas_call(
        body, name=name, grid=(Hkv, nt + 1),
        in_specs=[qs(dk),
                  pl.BlockSpec((1, T, dk), lambda h, i: (h, cur(i), 0)),
                  pl.BlockSpec((1, T, dk), lambda h, i: (h, prev(i), 0)),
                  pl.BlockSpec((1, T, v.shape[2]), lambda h, i: (h, cur(i), 0)),
                  pl.BlockSpec((1, T, v.shape[2]), lambda h, i: (h, prev(i), 0)),
                  pl.BlockSpec((1, nt, 1, T), lambda h, i: (0, 0, 0, 0)),
                  qs(dv), qs(1), qs(dv)],
        out_specs=[qs(dk),
                   pl.BlockSpec((1, T, dk), lambda h, i: (h, written(i), 0)),
                   pl.BlockSpec((1, T, dv), lambda h, i: (h, written(i), 0))],
        out_shape=[jax.ShapeDtypeStruct((H, L, dk), F32), jax.ShapeDtypeStruct((Hkv, L, dk), F32),
                   jax.ShapeDtypeStruct((Hkv, L, dv), F32)],
        scratch_shapes=[pltpu.VMEM((T, dk), F32), pltpu.VMEM((T, dv), F32)],
        compiler_params=_cparams(("parallel", "arbitrary")),
    )(q, k, k, v, v, kbias, o, lse, do)


def _adamw(w, gparts, m, v, *, name):
    n, R, C = gparts.shape
    tr = _pick(R, (128, 64, 32, 16, 8))
    c1 = 1.0 - ADAM_B1 ** ADAM_STEP
    c2 = 1.0 - ADAM_B2 ** ADAM_STEP

    def body(w_ref, g_ref, m_ref, v_ref, go_ref, d_ref, mo_ref, vo_ref):
        g = g_ref[0].astype(F32)
        for t in range(1, n):
            g = g + g_ref[t].astype(F32)
        mn = ADAM_B1 * m_ref[...] + (1.0 - ADAM_B1) * g
        vn = ADAM_B2 * v_ref[...] + (1.0 - ADAM_B2) * (g * g)
        go_ref[...] = g
        mo_ref[...] = mn
        vo_ref[...] = vn
        d_ref[...] = -ADAM_LR * ((mn / c1) / (jnp.sqrt(vn / c2) + ADAM_EPS) + ADAM_WD * w_ref[...])

    spec = pl.BlockSpec((tr, C), lambda i: (i, 0))
    return pl.pallas_call(
        body, name=name, grid=(R // tr,),
        in_specs=[spec, pl.BlockSpec((n, tr, C), lambda i: (0, i, 0)), spec, spec],
        out_specs=[spec] * 4,
        out_shape=[jax.ShapeDtypeStruct((R, C), F32)] * 4,
        compiler_params=_cparams(("parallel",)),
    )(w, gparts, m, v)


def _me():
    return lax.axis_index("x"), lax.axis_index("y"), lax.axis_index("c")


class _CommJob:
    def __init__(self, arrs):
        self.arrs = list(arrs)
        self.n = len(arrs)
        self.scratch = [pltpu.SemaphoreType.DMA((self.n, 7)), pltpu.SemaphoreType.DMA((self.n, 7)),
                        pltpu.SemaphoreType.DMA((self.n,))]

    def bind(self, refs):
        n = self.n
        self.ins, self.outs = refs[:n], refs[n:2 * n]
        self.send_sems, self.recv_sems, self.local_sems = refs[2 * n:2 * n + 3]

    def middle(self):
        pass


class _Gather(_CommJob):
    def __init__(self, arrs):
        super().__init__(arrs)
        self.out_shape = [jax.ShapeDtypeStruct((N_DEV, *a.shape), a.dtype) for a in arrs]

    def _where(self):
        x, y, c = _me()
        return (x, y, c), (x, y, 1 - c), [(1 - x, y), (x, 1 - y), (1 - x, 1 - y)], c

    def _copy(self, t, k, block, to, src=None):
        dst = self.outs[t].at[4 * block[0] + 2 * block[1] + block[2]]
        return pltpu.make_async_remote_copy(
            src_ref=dst if src is None else src, dst_ref=dst,
            send_sem=self.send_sems.at[t, k], recv_sem=self.recv_sems.at[t, k],
            device_id=to, device_id_type=MESH)

    def _mine(self, t, me):
        return pltpu.make_async_copy(self.ins[t], self.outs[t].at[4 * me[0] + 2 * me[1] + me[2]],
                                     self.local_sems.at[t])

    def _first(self, t, me, sibling, chips, c):
        return [self._copy(t, 0, me, sibling, src=self.ins[t])] + \
               [self._copy(t, 1 + j, me, (*chip, c), src=self.ins[t]) for j, chip in enumerate(chips)]

    def start(self):
        me, sibling, chips, c = self._where()
        for t in range(self.n):
            self._mine(t, me).start()
            for cp in self._first(t, me, sibling, chips, c):
                cp.start()

    def middle(self):
        me, sibling, chips, c = self._where()
        for j, chip in enumerate(chips):
            for t in range(self.n):
                self._copy(t, 1 + j, (*chip, c), me).wait_recv()
                self._copy(t, 4 + j, (*chip, c), sibling).start()

    def finish(self):
        me, sibling, chips, c = self._where()
        for t in range(self.n):
            self._copy(t, 0, sibling, me).wait_recv()
            for j, chip in enumerate(chips):
                self._copy(t, 4 + j, (*chip, 1 - c), me).wait_recv()
        for t in range(self.n):
            for cp in self._first(t, me, sibling, chips, c):
                cp.wait_send()
            for j, chip in enumerate(chips):
                self._copy(t, 4 + j, (*chip, c), sibling).wait_send()
            self._mine(t, me).wait()


class _Exchange(_CommJob):
    def __init__(self, arrs):
        super().__init__(arrs)
        self.out_shape = [jax.ShapeDtypeStruct(a.shape, a.dtype) for a in arrs]

    def _copies(self, t):
        x, y, c = _me()
        my_idx = 4 * x + 2 * y + c
        pairs = []
        for k in range(1, N_DEV):
            peer = (x ^ ((k >> 2) & 1), y ^ ((k >> 1) & 1), c ^ (k & 1))
            peer_idx = 4 * peer[0] + 2 * peer[1] + peer[2]
            sems = dict(send_sem=self.send_sems.at[t, k - 1], recv_sem=self.recv_sems.at[t, k - 1],
                        device_id=peer, device_id_type=MESH)
            pairs.append((pltpu.make_async_remote_copy(src_ref=self.ins[t].at[peer_idx],
                                                       dst_ref=self.outs[t].at[my_idx], **sems),
                          pltpu.make_async_remote_copy(src_ref=self.ins[t].at[peer_idx],
                                                       dst_ref=self.outs[t].at[peer_idx], **sems)))
        return pairs

    def _mine(self, t):
        x, y, c = _me()
        my_idx = 4 * x + 2 * y + c
        return pltpu.make_async_copy(self.ins[t].at[my_idx], self.outs[t].at[my_idx], self.local_sems.at[t])

    def start(self):
        for t in range(self.n):
            self._mine(t).start()
            for snd, _ in self._copies(t):
                snd.start()

    def finish(self):
        for t in range(self.n):
            pairs = self._copies(t)
            for _, rcv in pairs:
                rcv.wait_recv()
            for snd, _ in pairs:
                snd.wait_send()
            self._mine(t).wait()


def _comm(job, *, name):
    def body(*refs):
        job.bind(refs)
        job.start()
        job.middle()
        job.finish()

    any_spec = pl.BlockSpec(memory_space=pl.ANY)
    return pl.pallas_call(
        body, name=name,
        in_specs=[any_spec] * job.n, out_specs=[any_spec] * job.n,
        out_shape=job.out_shape, scratch_shapes=job.scratch,
        compiler_params=pltpu.CompilerParams(has_side_effects=True),
    )(*job.arrs)


def _ride(job, refs, step, total):
    job.bind(refs)

    @pl.when(step == 0)
    def _():
        job.start()

    @pl.when(step == (total * 3) // 5)
    def _():
        job.middle()

    @pl.when(step == total - 1)
    def _():
        job.finish()


SCALE_A = DH ** -0.5
SCALE_B = (MLA_NOPE + MLA_ROPE) ** -0.5
SCALE_C = DH ** -0.5
DQK_B = MLA_NOPE + MLA_ROPE


def _rope_tables(L, half, width):
    pos = (jnp.arange(L) - PAD).astype(F32)
    lane = jnp.arange(width)
    inv = ROPE_THETA ** (-(lane % half).astype(F32) / half)
    ang = pos[:, None] * inv[None, :]
    sign = jnp.where(lane % (2 * half) < half, -1.0, 1.0).astype(F32)
    return jnp.cos(ang), jnp.sin(ang) * sign[None, :]


def _rope_lanes(x, cos, sin, half):
    W = x.shape[1]
    lane = lax.broadcasted_iota(jnp.int32, x.shape, 1)
    first = (lane & (2 * half - 1)) < half
    partner = jnp.where(first, pltpu.roll(x, W - half, 1), pltpu.roll(x, half, 1))
    return x * cos + partner * sin


def _tile_lanes(t, width):
    return t if t.shape[1] == width else jnp.tile(t, (1, width // t.shape[1]))


def _split(x, H, d, dst):
    for h in range(H):
        dst[h] = x[:, d * h:d * (h + 1)].astype(dst.dtype)


def _join(src, H):
    return jnp.concatenate([src[h] for h in range(H)], axis=1)


def _head_spec(H, tm, d):
    return pl.BlockSpec((H, tm, d), lambda i: (0, i, 0))


def _split_values(x, H, dst):
    ones = jnp.ones((x.shape[0], DH), BF16)
    for h in range(H):
        dst[h] = jnp.concatenate([x[:, DH * h:DH * (h + 1)].astype(BF16), ones], axis=1)


def _prep_a(proj, *, name):
    L = proj.shape[0]
    tm = ROW_T

    def body(x_ref, qo, ko, vo):
        _split(x_ref[:, 0:512] * SCALE_A, HEADS, DH, qo)
        _split(x_ref[:, 512:1024], HEADS, DH, ko)
        _split_values(x_ref[:, 1024:1536], HEADS, vo)

    return pl.pallas_call(
        body, name=name, grid=(L // tm,),
        in_specs=[_col_spec(tm, 1536, QKVA)],
        out_specs=[_head_spec(HEADS, tm, DH)] * 2 + [_head_spec(HEADS, tm, 2 * DH)],
        out_shape=[jax.ShapeDtypeStruct((HEADS, L, DH), BF16)] * 2 + [jax.ShapeDtypeStruct((HEADS, L, 2 * DH), BF16)],
        compiler_params=_cparams(("parallel",)),
    )(proj)


def _unprep_a(dq, dk, dv, dproj, *, name):
    L = dq.shape[1]
    tm = ROW_T

    def body(dq_ref, dk_ref, dv_ref, _, dp_ref):
        dp_ref[:, 0:512] = (_join(dq_ref, HEADS) * SCALE_A).astype(BF16)
        dp_ref[:, 512:1024] = _join(dk_ref, HEADS).astype(BF16)
        dp_ref[:, 1024:1536] = _join(dv_ref, HEADS).astype(BF16)

    hs = _head_spec(HEADS, tm, DH)
    return pl.pallas_call(
        body, name=name, grid=(L // tm,),
        in_specs=[hs, hs, hs, pl.BlockSpec(memory_space=pl.ANY)],
        out_specs=_col_spec(tm, 1536, QKVA),
        out_shape=jax.ShapeDtypeStruct(dproj.shape, dproj.dtype),
        input_output_aliases={3: 0},
        compiler_params=_cparams(("parallel",)),
    )(dq, dk, dv, dproj)


def _prep_c(proj, tab, *, name):
    L = proj.shape[0]
    tm = ROW_T

    def body(x_ref, cos_ref, sin_ref, qo, ko, vo):
        cos, sin = cos_ref[...], sin_ref[...]
        q = _rope_lanes(x_ref[:, 0:512], _tile_lanes(cos, 512), _tile_lanes(sin, 512), DH // 2)
        _split(q * SCALE_C, HEADS, DH, qo)
        _split(_rope_lanes(x_ref[:, 512:640], cos, sin, DH // 2), SWA_KV_HEADS, DH, ko)
        _split_values(x_ref[:, 640:768], SWA_KV_HEADS, vo)

    t128 = pl.BlockSpec((tm, LANES), lambda i: (i, 0))
    return pl.pallas_call(
        body, name=name, grid=(L // tm,),
        in_specs=[_col_spec(tm, 768, QKVC), t128, t128],
        out_specs=[_head_spec(HEADS, tm, DH), _head_spec(SWA_KV_HEADS, tm, DH),
                   _head_spec(SWA_KV_HEADS, tm, 2 * DH)],
        out_shape=[jax.ShapeDtypeStruct((HEADS, L, DH), BF16), jax.ShapeDtypeStruct((SWA_KV_HEADS, L, DH), BF16),
                   jax.ShapeDtypeStruct((SWA_KV_HEADS, L, 2 * DH), BF16)],
        compiler_params=_cparams(("parallel",)),
    )(proj, *tab)


def _unprep_c(dq, dk, dv, tab, dproj, *, name):
    L = dq.shape[1]
    tm = ROW_T

    def body(dq_ref, dk_ref, dv_ref, cos_ref, sin_ref, _, dp_ref):
        cos, nsin = cos_ref[...], -sin_ref[...]
        dqv = _join(dq_ref, HEADS) * SCALE_C
        dp_ref[:, 0:512] = _rope_lanes(dqv, _tile_lanes(cos, 512), _tile_lanes(nsin, 512), DH // 2).astype(BF16)
        dp_ref[:, 512:640] = _rope_lanes(_join(dk_ref, SWA_KV_HEADS), cos, nsin, DH // 2).astype(BF16)
        dp_ref[:, 640:768] = _join(dv_ref, SWA_KV_HEADS).astype(BF16)

    hs = _head_spec(HEADS, tm, DH)
    hkv = _head_spec(SWA_KV_HEADS, tm, DH)
    t128 = pl.BlockSpec((tm, LANES), lambda i: (i, 0))
    return pl.pallas_call(
        body, name=name, grid=(L // tm,),
        in_specs=[hs, hkv, hkv, t128, t128, pl.BlockSpec(memory_space=pl.ANY)],
        out_specs=_col_spec(tm, 768, QKVC),
        out_shape=jax.ShapeDtypeStruct(dproj.shape, dproj.dtype),
        input_output_aliases={5: 0},
        compiler_params=_cparams(("parallel",)),
    )(dq, dk, dv, *tab, dproj)


def _q_tables(cos, sin, on_ref):
    on = on_ref[...] > 0.5
    width = on_ref.shape[1]
    return jnp.where(on, _tile_lanes(cos, width), 1.0), jnp.where(on, _tile_lanes(sin, width), 0.0)


def _prep_b(qbm, kvbm, proj, tab_k, q_rotary, *, name):
    L = proj.shape[0]
    tm = ROW_T

    def body(q_ref, kv_ref, kr_ref, ck_ref, sk_ref, on_ref, qo, ko, vo):
        cq, sq = _q_tables(ck_ref[...], sk_ref[...], on_ref)
        q = _rope_lanes(q_ref[...], cq, sq, MLA_ROPE // 2) * SCALE_B
        _split(q, HEADS, DQK_B, qo)
        kr = _rope_lanes(kr_ref[...], ck_ref[...], sk_ref[...], MLA_ROPE // 2)[:, :MLA_ROPE].astype(BF16)
        kv = kv_ref[...]
        ones = jnp.ones((tm, DH), BF16)
        for h in range(HEADS):
            ko[h] = jnp.concatenate([kv[:, 128 * h:128 * h + MLA_NOPE].astype(BF16), kr], axis=1)
            vo[h] = jnp.concatenate([kv[:, 128 * h + MLA_NOPE:128 * (h + 1)].astype(BF16), ones], axis=1)

    t128 = pl.BlockSpec((tm, LANES), lambda i: (i, 0))
    t768 = pl.BlockSpec((tm, 768), lambda i: (i, 0))
    return pl.pallas_call(
        body, name=name, grid=(L // tm,),
        in_specs=[t768, pl.BlockSpec((tm, 1024), lambda i: (i, 0)), _col_spec(tm, 128, KR), t128, t128,
                  pl.BlockSpec((1, 768), lambda i: (0, 0))],
        out_specs=[_head_spec(HEADS, tm, DQK_B), _head_spec(HEADS, tm, DQK_B), _head_spec(HEADS, tm, 2 * DH)],
        out_shape=[jax.ShapeDtypeStruct((HEADS, L, DQK_B), BF16), jax.ShapeDtypeStruct((HEADS, L, DQK_B), BF16),
                   jax.ShapeDtypeStruct((HEADS, L, 2 * DH), BF16)],
        compiler_params=_cparams(("parallel",)),
    )(qbm, kvbm, proj, *tab_k, q_rotary)


def _unprep_b(dq, dk, dv, tab_k, q_rotary, dproj, *, name):
    L = dq.shape[1]
    tm = ROW_T

    def body(dq_ref, dk_ref, dv_ref, ck_ref, sk_ref, on_ref, _, dp_kr, dqo, dkvo):
        cq, sq = _q_tables(ck_ref[...], sk_ref[...], on_ref)
        dqv = _join(dq_ref, HEADS) * SCALE_B
        dqo[...] = _rope_lanes(dqv, cq, -sq, MLA_ROPE // 2).astype(BF16)
        parts = []
        dkr = None
        for h in range(HEADS):
            dkh = dk_ref[h]
            parts += [dkh[:, :MLA_NOPE], dv_ref[h]]
            r = dkh[:, MLA_NOPE:]
            dkr = r if dkr is None else dkr + r
        dkvo[...] = jnp.concatenate(parts, axis=1).astype(BF16)
        dkr = jnp.concatenate([dkr, jnp.zeros((tm, LANES - MLA_ROPE), F32)], axis=1)
        dp_kr[...] = _rope_lanes(dkr, ck_ref[...], -sk_ref[...], MLA_ROPE // 2).astype(BF16)

    t128 = pl.BlockSpec((tm, LANES), lambda i: (i, 0))
    t768 = pl.BlockSpec((tm, 768), lambda i: (i, 0))
    hq = _head_spec(HEADS, tm, DQK_B)
    return pl.pallas_call(
        body, name=name, grid=(L // tm,),
        in_specs=[hq, hq, _head_spec(HEADS, tm, DH), t128, t128, pl.BlockSpec((1, 768), lambda i: (0, 0)),
                  pl.BlockSpec(memory_space=pl.ANY)],
        out_specs=[_col_spec(tm, 128, KR), t768, pl.BlockSpec((tm, 1024), lambda i: (i, 0))],
        out_shape=[jax.ShapeDtypeStruct(dproj.shape, dproj.dtype), jax.ShapeDtypeStruct((L, 768), BF16),
                   jax.ShapeDtypeStruct((L, 1024), BF16)],
        input_output_aliases={6: 0},
        compiler_params=_cparams(("parallel",)),
    )(dq, dk, dv, *tab_k, q_rotary, dproj)


def _gate(y, proj, zcol, *, name):
    L = proj.shape[0]
    tm = ROW_T

    def body(y_ref, z_ref, u_ref):
        z = z_ref[...]
        u_ref[...] = (_join(y_ref, HEADS) * (z * jax.nn.sigmoid(z))).astype(BF16)

    return pl.pallas_call(
        body, name=name, grid=(L // tm,),
        in_specs=[_head_spec(HEADS, tm, DH), _col_spec(tm, 512, zcol)],
        out_specs=pl.BlockSpec((tm, 512), lambda i: (i, 0)),
        out_shape=jax.ShapeDtypeStruct((L, 512), BF16),
        compiler_params=_cparams(("parallel",)),
    )(y, proj)


def _gate_bwd(du, y, proj, zcol, dproj, *, delta_rows=False, zero_tail=0, name):
    L = proj.shape[0]
    tm = ROW_T
    assert not delta_rows or tm == ATT_T

    def body(du_ref, y_ref, z_ref, _, dz_ref, dy_ref, *dl_ref):
        z = z_ref[...]
        duv = du_ref[...]
        sg = jax.nn.sigmoid(z)
        yv = _join(y_ref, HEADS)
        dyv = duv * (z * sg)
        _split(dyv, HEADS, DH, dy_ref)
        dz_ref[:, 0:512] = (duv * yv * (sg * (1.0 + z * (1.0 - sg)))).astype(BF16)
        if zero_tail:
            dz_ref[:, 512:] = jnp.zeros((tm, zero_tail), BF16)
        if delta_rows:
            prod = dyv * yv
            ones = jnp.ones((8, DH), F32)
            for h in range(HEADS):
                sums = lax.dot_general(ones, prod[:, DH * h:DH * (h + 1)], _NT, preferred_element_type=F32)
                dl_ref[0][h, 0] = sums[0:1, :]

    hs = _head_spec(HEADS, tm, DH)
    out_specs = [_col_spec(tm, 512 + zero_tail, zcol), hs]
    out_shape = [jax.ShapeDtypeStruct(dproj.shape, dproj.dtype),
                 jax.ShapeDtypeStruct((HEADS, L, DH), BF16 if delta_rows else F32)]
    if delta_rows:
        out_specs.append(pl.BlockSpec((HEADS, 1, 1, tm), lambda i: (0, i, 0, 0)))
        out_shape.append(jax.ShapeDtypeStruct((HEADS, L // tm, 1, tm), F32))
    return pl.pallas_call(
        body, name=name, grid=(L // tm,),
        in_specs=[pl.BlockSpec((tm, 512), lambda i: (i, 0)), hs, _col_spec(tm, 512, zcol),
                  pl.BlockSpec(memory_space=pl.ANY)],
        out_specs=out_specs, out_shape=out_shape,
        input_output_aliases={3: 0},
        compiler_params=_cparams(("parallel",)),
    )(du, y, proj, dproj)


MERGE_T = 192


def _branch_merge(us, w_branch, proj, *, name):
    L = proj.shape[0]
    tm = ROW_T

    def body(u0, u1, u2, w_ref, g_ref, m_ref, pb_ref):
        acc = None
        for n, u_ref in enumerate((u0, u1, u2)):
            pb = jnp.dot(u_ref[...], w_ref[n], preferred_element_type=F32)
            pb_ref[n] = pb.astype(BF16)
            t = jax.nn.sigmoid(g_ref[:, n * D_MODEL:(n + 1) * D_MODEL]) * pb
            acc = t if acc is None else acc + t
        m_ref[...] = acc.astype(BF16)

    urow = pl.BlockSpec((tm, BRANCH_W), lambda i: (i, 0))
    return pl.pallas_call(
        body, name=name, grid=(L // tm,),
        in_specs=[urow] * N_BRANCH + [pl.BlockSpec((N_BRANCH, BRANCH_W, D_MODEL), lambda i: (0, 0, 0)),
                                      _col_spec(tm, N_BRANCH * D_MODEL, GATES)],
        out_specs=[pl.BlockSpec((tm, D_MODEL), lambda i: (i, 0)),
                   pl.BlockSpec((N_BRANCH, tm, D_MODEL), lambda i: (0, i, 0))],
        out_shape=[jax.ShapeDtypeStruct((L, D_MODEL), BF16), jax.ShapeDtypeStruct((N_BRANCH, L, D_MODEL), BF16)],
        compiler_params=_cparams(("parallel",)),
    )(*us, w_branch, proj)


def _merge_bwd(dmerged, proj, pb, *, name):
    L = proj.shape[0]
    tm = MERGE_T

    def body(dm_ref, g_ref, p_ref, dg_ref, dp0, dp1, dp2):
        dm = dm_ref[...]
        for n, dp_ref in enumerate((dp0, dp1, dp2)):
            cols = slice(n * D_MODEL, (n + 1) * D_MODEL)
            sg = jax.nn.sigmoid(g_ref[:, cols])
            dp_ref[...] = (dm * sg).astype(BF16)
            dg_ref[:, cols] = (dm * p_ref[n].astype(F32) * (sg * (1.0 - sg))).astype(BF16)

    row = pl.BlockSpec((tm, D_MODEL), lambda i: (i, 0))
    gates = _col_spec(tm, N_BRANCH * D_MODEL, GATES)
    outs = pl.pallas_call(
        body, name=name, grid=(L // tm,),
        in_specs=[row, gates, pl.BlockSpec((N_BRANCH, tm, D_MODEL), lambda i: (0, i, 0))],
        out_specs=[gates] + [row] * N_BRANCH,
        out_shape=[jax.ShapeDtypeStruct((L, NP_IN), BF16)] + [jax.ShapeDtypeStruct((L, D_MODEL), BF16)] * N_BRANCH,
        compiler_params=_cparams(("parallel",)),
    )(dmerged, proj, pb)
    return outs[0], outs[1:]


def _scan_rows(x, reverse):
    rows = lax.broadcasted_iota(jnp.int32, x.shape, 0)
    step = 1
    while step < BLK:
        if reverse:
            x = x + jnp.where(rows < BLK - step, pltpu.roll(x, BLK - step, 0), 0.0)
        else:
            x = x + jnp.where(rows >= step, pltpu.roll(x, step, 0), 0.0)
        step *= 2
    return x


def _forget_fwd(proj, b_f, *, name):
    L = proj.shape[0]
    nb = L // BLK

    def body(x_ref, b_ref, o_ref, carry):
        i = pl.program_id(0)

        @pl.when(i == 0)
        def _():
            carry[...] = jnp.zeros_like(carry)

        c = _scan_rows(jax.nn.log_sigmoid(x_ref[...] + b_ref[...]), False) + carry[...]
        carry[...] = c[BLK - 1:BLK, :]
        pos = i * BLK + lax.broadcasted_iota(jnp.int32, (HEADS, BLK), 1)
        o_ref[...] = c.T[:HEADS, :] + jnp.where(pos < PAD, BIG, 0.0)

    return pl.pallas_call(
        body, name=name, grid=(nb,),
        in_specs=[_col_spec(BLK, LANES, FA), pl.BlockSpec((1, LANES), lambda i: (0, 0))],
        out_specs=pl.BlockSpec((HEADS, BLK), lambda i: (0, i)),
        out_shape=jax.ShapeDtypeStruct((HEADS, L), F32),
        scratch_shapes=[pltpu.VMEM((1, LANES), F32)],
        compiler_params=_cparams(("arbitrary",)),
    )(proj, jnp.pad(b_f, (0, LANES - HEADS)).reshape(1, LANES))


def _forget_bwd(dct, proj, b_f, dproj, *, name):
    L = proj.shape[0]
    nb = L // BLK

    def body(d_ref, x_ref, b_ref, _, dp_ref, db_ref, carry):
        i = pl.program_id(0)

        @pl.when(i == 0)
        def _():
            carry[...] = jnp.zeros_like(carry)
            db_ref[...] = jnp.zeros_like(db_ref)

        d = jnp.concatenate([d_ref[...], jnp.zeros((BLK - HEADS, BLK), F32)], axis=0).T
        dlog = _scan_rows(d, True) + carry[...]
        carry[...] = dlog[0:1, :]
        lane = lax.broadcasted_iota(jnp.int32, (BLK, LANES), 1)
        daf = jnp.where(lane < HEADS, dlog * jax.nn.sigmoid(-(x_ref[...] + b_ref[...])), 0.0)
        db_ref[...] += jnp.sum(daf, axis=0, keepdims=True)
        dp_ref[...] = jnp.concatenate([daf, jnp.zeros((BLK, LANES), F32)], axis=1).astype(BF16)

    back = lambda i: nb - 1 - i
    dp, db = pl.pallas_call(
        body, name=name, grid=(nb,),
        in_specs=[pl.BlockSpec((HEADS, BLK), lambda i: (0, back(i))),
                  pl.BlockSpec((BLK, LANES), lambda i: (back(i), FA // LANES)),
                  pl.BlockSpec((1, LANES), lambda i: (0, 0)), pl.BlockSpec(memory_space=pl.ANY)],
        out_specs=[pl.BlockSpec((BLK, 2 * LANES), lambda i: (back(i), FA // (2 * LANES))),
                   pl.BlockSpec((1, LANES), lambda i: (0, 0))],
        out_shape=[jax.ShapeDtypeStruct(dproj.shape, dproj.dtype), jax.ShapeDtypeStruct((1, LANES), F32)],
        scratch_shapes=[pltpu.VMEM((1, LANES), F32)],
        input_output_aliases={3: 0},
        compiler_params=_cparams(("arbitrary",)),
    )(dct, proj, jnp.pad(b_f, (0, LANES - HEADS)).reshape(1, LANES), dproj)
    return dp, db[0, :HEADS]


def _key_bias(L, T, ct=None):
    padb = jnp.where(jnp.arange(L) < PAD, BIG, 0.0).astype(F32)[None]
    kb = padb if ct is None else ct + padb
    return kb.reshape(kb.shape[0], L // T, 1, T)


def _layer_fwd(h, w, tabs, l, side=None):
    tag = f"l{l}"
    L = h.shape[0]
    hn = _rms_fwd(h, w["norm_g"], name=f"{tag}_rms_in")
    if "late" in w:
        late_job, assemble = w["late"]
        proj, *late = _mm(hn, w["w_in"], side=late_job, name=f"{tag}_mm_in")
        w = {**{key: val for key, val in w.items() if key != "late"}, **assemble(*late)}
    else:
        proj = _mm(hn, w["w_in"], name=f"{tag}_mm_in")
    kb_a = _forget_fwd(proj, w["b_f"], name=f"{tag}_forget").reshape(HEADS, L // ATT_T, 1, ATT_T)
    ops_a = (*_prep_a(proj, name=f"{tag}_prep_a"), kb_a)
    ya, lsea, *side_out = _attn_fwd(*ops_a, T=ATT_T, side=side, name=f"{tag}_attn_a")
    cqn = _rms_fwd(proj, w["g_cq"], col=CQ, name=f"{tag}_rms_cq")
    ckvn = _rms_fwd(proj, w["g_ckv"], col=CKV, name=f"{tag}_rms_ckv")
    qbm = _mm(cqn, w["w_uq"], name=f"{tag}_mm_uq")
    kvbm = _mm(ckvn, w["w_ukv"], name=f"{tag}_mm_ukv")
    ops_b = (*_prep_b(qbm, kvbm, proj, tabs["bk"], tabs["bq"], name=f"{tag}_prep_b"), _key_bias(L, ATT_T))
    yb, lseb = _attn_fwd(*ops_b, T=ATT_T, name=f"{tag}_attn_b")
    ops_c = (*_prep_c(proj, tabs["c"], name=f"{tag}_prep_c"), _key_bias(L, WINDOW))
    sink = jnp.broadcast_to(w["sinks"][:, None, None], (HEADS, 1, LANES))
    yc, lsec = _attn_fwd(*ops_c, T=WINDOW, window=True, sink=sink, name=f"{tag}_attn_c")
    us = [_gate(y, proj, zcol, name=f"{tag}_gate{n}") for n, (y, zcol) in enumerate(((ya, ZA), (yb, ZB), (yc, ZC)))]
    merged, pbr = _branch_merge(us, w["w_branch"], proj, name=f"{tag}_branch_merge")
    out = _mm(merged, w["w_out"], add=h, name=f"{tag}_mm_out")
    saved = dict(h=h, hn=hn, proj=proj, ops_a=ops_a, ya=ya, lsea=lsea, cqn=cqn, ckvn=ckvn,
                 ops_b=ops_b, yb=yb, lseb=lseb, ops_c=ops_c, yc=yc, lsec=lsec, us=us, pbr=pbr, merged=merged)
    return out, saved, side_out, w


def _w_in_chunks(g, tag):
    return _w_in_to_shards(g["w_in"], name=f"{tag}_w_in_chunks")


def _grad_chunks(g, tag, with_w_in=True):
    br = _cut_shards(g["w_branch"].reshape(N_BRANCH * BRANCH_W, D_MODEL), name=f"{tag}_w_branch_chunks")
    rest = [_cut_shards(g["w_uq"], name=f"{tag}_w_uq_chunks"), _cut_shards(g["w_ukv"], name=f"{tag}_w_ukv_chunks"),
            br.reshape(N_DEV, N_BRANCH, BRANCH_W, D_MODEL // N_DEV),
            g["w_out"].reshape(N_DEV, D_MODEL // N_DEV, D_MODEL)]
    return ([_w_in_chunks(g, tag)] if with_w_in else []) + rest


def _layer_bwd(dout, s, w, tabs, l, side=None, own_exchange=False):
    tag = f"l{l}"
    L = dout.shape[0]
    proj = s["proj"]
    g = {}
    dmerged = _mm(dout, w["w_out"], tb=True, name=f"{tag}_mm_out_dx")
    g["w_out"] = _mm(s["merged"], dout, ta=True, out_dtype=BF16, name=f"{tag}_mm_out_dw")
    dproj, dpbr = _merge_bwd(dmerged, proj, s["pbr"], name=f"{tag}_merge_bwd")
    dus = [_mm(dpbr[n], w["w_branch"][n], tb=True, name=f"{tag}_mm_br{n}_dx") for n in range(N_BRANCH)]
    g["w_branch"] = jnp.stack([_mm(s["us"][n], dpbr[n], ta=True, out_dtype=BF16, name=f"{tag}_mm_br{n}_dw")
                               for n in range(N_BRANCH)])
    dproj, dya, dla = _gate_bwd(dus[0], s["ya"], proj, ZA, dproj, delta_rows=True, name=f"{tag}_gate0_bwd")
    dproj, dyb, dlb = _gate_bwd(dus[1], s["yb"], proj, ZB, dproj, delta_rows=True, zero_tail=QKVC - ZB - BRANCH_W,
                                name=f"{tag}_gate1_bwd")
    dproj, dyc = _gate_bwd(dus[2], s["yc"], proj, ZC, dproj, name=f"{tag}_gate2_bwd")

    def bwd_operands(ops, lse, delta, dy16):
        q16, k16, v16, kbias = ops
        return (q16, k16, v16, kbias.reshape(kbias.shape[0], L, 1), lse.reshape(HEADS, L // ATT_T, 1, ATT_T), delta,
                dy16)

    dqa, dka, dva, dcq, dck, *side_out = _attn_bwd_t(*bwd_operands(s["ops_a"], s["lsea"], dla, dya), T=ATT_T,
                                                     fox=True, side=side, name=f"{tag}_attn_a_bwd")
    dproj = _unprep_a(dqa, dka, dva, dproj, name=f"{tag}_unprep_a")
    dproj, g["b_f"] = _forget_bwd(dcq.reshape(HEADS, L) + dck[:, :, 0], proj, w["b_f"], dproj,
                                  name=f"{tag}_forget_bwd")
    dqb, dkb, dvb, _, _ = _attn_bwd_t(*bwd_operands(s["ops_b"], s["lseb"], dlb, dyb), T=ATT_T,
                                      name=f"{tag}_attn_b_bwd")
    dproj, dqbm, dkvbm = _unprep_b(dqb, dkb, dvb, tabs["bk"], tabs["bq"], dproj, name=f"{tag}_unprep_b")
    dcqn = _mm(dqbm, w["w_uq"], tb=True, name=f"{tag}_mm_uq_dx")
    g["w_uq"] = _mm(s["cqn"], dqbm, ta=True, out_dtype=BF16, name=f"{tag}_mm_uq_dw")
    dckvn = _mm(dkvbm, w["w_ukv"], tb=True, name=f"{tag}_mm_ukv_dx")
    g["w_ukv"] = _mm(s["ckvn"], dkvbm, ta=True, out_dtype=BF16, name=f"{tag}_mm_ukv_dw")
    dproj, g["g_cq"] = _rms_bwd(dcqn, proj, w["g_cq"], col=CQ, into=dproj, name=f"{tag}_rms_cq_bwd")
    dproj, g["g_ckv"] = _rms_bwd(dckvn, proj, w["g_ckv"], col=CKV, into=dproj, name=f"{tag}_rms_ckv_bwd")
    dqc, dkc, dvc = _attn_bwd_window(*s["ops_c"], s["yc"], s["lsec"], dyc, name=f"{tag}_attn_c_bwd")
    dproj = _unprep_c(dqc, dkc, dvc, tabs["c"], dproj, name=f"{tag}_unprep_c")
    delta_c = jnp.sum(dyc * s["yc"], axis=-1)
    g["sinks"] = -jnp.sum(jnp.exp(w["sinks"][:, None] - s["lsec"][:, :, 0]) * delta_c, axis=1)
    own_out = []
    if own_exchange:
        g["w_in"], *r_rest = _mm(s["hn"], dproj, ta=True, out_dtype=BF16, side=_Exchange(_grad_chunks(g, tag, False)),
                                 name=f"{tag}_mm_in_dw")
        dhn, r_in = _mm(dproj, w["w_in"], tb=True, side=_Exchange([_w_in_chunks(g, tag)]), name=f"{tag}_mm_in_dx")
        own_out = [r_in, *r_rest]
    else:
        g["w_in"] = _mm(s["hn"], dproj, ta=True, out_dtype=BF16, name=f"{tag}_mm_in_dw")
        dhn = _mm(dproj, w["w_in"], tb=True, name=f"{tag}_mm_in_dx")
    dh, g["norm_g"] = _rms_bwd(dhn, s["h"], w["norm_g"], add=dout, name=f"{tag}_rms_in_bwd")
    return dh, g, side_out, own_out


def _cols_from_shards(g):
    return jnp.moveaxis(g, 0, 1).reshape(g.shape[1], N_DEV * g.shape[2])


def _join_shards(g, *, name):
    _, R, C = g.shape
    tr = _pick(R, (512, 384, 256))

    def body(g_ref, o_ref):
        for d in range(N_DEV):
            o_ref[:, C * d:C * (d + 1)] = g_ref[d]

    return pl.pallas_call(
        body, name=name, grid=(R // tr,),
        in_specs=[pl.BlockSpec((N_DEV, tr, C), lambda i: (0, i, 0))],
        out_specs=pl.BlockSpec((tr, N_DEV * C), lambda i: (i, 0)),
        out_shape=jax.ShapeDtypeStruct((R, N_DEV * C), g.dtype),
        compiler_params=_cparams(("parallel",)),
    )(g)


def _cut_shards(w, *, name):
    R = w.shape[0]
    C = w.shape[1] // N_DEV
    tr = _pick(R, (512, 384, 256))

    def body(w_ref, o_ref):
        for d in range(N_DEV):
            o_ref[d] = w_ref[:, C * d:C * (d + 1)]

    return pl.pallas_call(
        body, name=name, grid=(R // tr,),
        in_specs=[pl.BlockSpec((tr, N_DEV * C), lambda i: (i, 0))],
        out_specs=pl.BlockSpec((N_DEV, tr, C), lambda i: (0, i, 0)),
        out_shape=jax.ShapeDtypeStruct((N_DEV, R, C), w.dtype),
        compiler_params=_cparams(("parallel",)),
    )(w)


W_IN_SHARD = N_IN // N_DEV


def _w_in_pieces():
    pieces = []
    for lo, hi, dst in _RUNS:
        for d in range(N_DEV):
            a, b = max(lo, d * W_IN_SHARD), min(hi, (d + 1) * W_IN_SHARD)
            if a < b:
                pieces.append((d, a - d * W_IN_SHARD, b - d * W_IN_SHARD, dst + a - lo))
    return pieces


def _w_in_from_shards(g, *, name):
    R = g.shape[1]
    tr = 256
    covered = sorted((dst, dst + b - a) for _, a, b, dst in _w_in_pieces())

    def body(g_ref, o_ref):
        at = 0
        for lo, hi in covered + [(NP_IN, NP_IN)]:
            if lo > at:
                o_ref[:, at:lo] = jnp.zeros((tr, lo - at), BF16)
            at = max(at, hi)
        for d, a, b, dst in _w_in_pieces():
            o_ref[:, dst:dst + b - a] = g_ref[d, :, a:b]

    return pl.pallas_call(
        body, name=name, grid=(R // tr,),
        in_specs=[pl.BlockSpec((N_DEV, tr, W_IN_SHARD), lambda i: (0, i, 0))],
        out_specs=pl.BlockSpec((tr, NP_IN), lambda i: (i, 0)),
        out_shape=jax.ShapeDtypeStruct((R, NP_IN), BF16),
        compiler_params=_cparams(("parallel",)),
    )(g)


def _w_in_to_shards(dw, *, name):
    R = dw.shape[0]
    tr = 256

    def body(w_ref, o_ref):
        for d, a, b, dst in _w_in_pieces():
            o_ref[d, :, a:b] = w_ref[:, dst:dst + b - a]

    return pl.pallas_call(
        body, name=name, grid=(R // tr,),
        in_specs=[pl.BlockSpec((tr, NP_IN), lambda i: (i, 0))],
        out_specs=pl.BlockSpec((N_DEV, tr, W_IN_SHARD), lambda i: (0, i, 0)),
        out_shape=jax.ShapeDtypeStruct((N_DEV, R, W_IN_SHARD), BF16),
        compiler_params=_cparams(("parallel",)),
    )(dw)


_SMALL = (("norm_g", DEPTH * D_MODEL), ("b_f", DEPTH * HEADS), ("g_cq", DEPTH * MLA_QLORA),
          ("g_ckv", DEPTH * MLA_KVLORA), ("sinks", DEPTH * HEADS), ("final_g", D_MODEL), ("loss", 1),
          ("meta", N_META * D_MODEL))
SMALL_ROWS = 168


def _pack_small(d):
    parts = []
    for name, size in _SMALL:
        padded = -(-size // 128) * 128
        v = d[name].reshape(-1).astype(F32) if name in d else jnp.zeros((size,), F32)
        parts.append(jnp.pad(v, (0, padded - size)))
    flat = jnp.concatenate(parts)
    return jnp.pad(flat, (0, SMALL_ROWS * 128 - flat.shape[0])).reshape(SMALL_ROWS, 128)


def _unpack_small(p, shapes):
    flat = p.reshape(-1)
    out, at = {}, 0
    for name, size in _SMALL:
        if name in shapes:
            out[name] = flat[at:at + size].reshape(shapes[name])
        at += -(-size // 128) * 128
    return out


def kernel(x, meta_tokens, norm_g, w_in, b_f, g_cq, g_ckv, w_uq, w_ukv, sinks, w_branch, w_out, final_g, loss_target, m_meta_tokens, m_norm_g, m_w_in, m_b_f, m_g_cq, m_g_ckv, m_w_uq, m_w_ukv, m_sinks, m_w_branch, m_w_out, m_final_g, v_meta_tokens, v_norm_g, v_w_in, v_b_f, v_g_cq, v_g_ckv, v_w_uq, v_w_ukv, v_sinks, v_w_branch, v_w_out, v_final_g):
    S = x.shape[1]
    L = BLK + S
    cx, cy, cc = _me()
    my_idx = 4 * cx + 2 * cy + cc

    def shards(l):
        return [t[l].astype(BF16) for t in (w_in, w_uq, w_ukv, w_branch, w_out)]

    def small_weights(l):
        return dict(norm_g=norm_g[l], b_f=b_f[l], g_cq=g_cq[l], g_ckv=g_ckv[l], sinks=sinks[l])

    def rest_weights(l):
        def assemble(gw_uq, gw_ukv, gw_br, gw_out):
            br = _join_shards(gw_br.reshape(N_DEV, N_BRANCH * BRANCH_W, D_MODEL // N_DEV), name=f"l{l}_w_branch_full")
            return dict(w_uq=_join_shards(gw_uq, name=f"l{l}_w_uq_full"),
                        w_ukv=_join_shards(gw_ukv, name=f"l{l}_w_ukv_full"),
                        w_branch=br.reshape(N_BRANCH, BRANCH_W, D_MODEL), w_out=gw_out.reshape(D_MODEL, D_MODEL))
        return assemble

    def layer_weights(l, gw_in, *gw_rest):
        return dict(small_weights(l), w_in=_w_in_from_shards(gw_in, name=f"l{l}_w_in_full"),
                    **rest_weights(l)(*gw_rest))

    gw_in0, g_meta = _comm(_Gather([shards(0)[0], meta_tokens]), name="gather_l0")
    layers = [dict(small_weights(0), w_in=_w_in_from_shards(gw_in0, name="l0_w_in_full"),
                   late=(_Gather(shards(0)[1:]), rest_weights(0))), None]
    meta_full = _cols_from_shards(g_meta)

    h = jnp.concatenate([jnp.zeros((PAD, D_MODEL), F32), meta_full, x[0]], axis=0)
    q_rotary = ((jnp.arange(HEADS * DQK_B) % DQK_B) >= MLA_NOPE).astype(F32)[None, :]
    tabs = dict(c=_rope_tables(L, DH // 2, LANES), bk=_rope_tables(L, MLA_ROPE // 2, LANES), bq=q_rotary)
    saved = [None] * DEPTH
    h, saved[0], gw1, layers[0] = _layer_fwd(h, layers[0], tabs, 0, side=_Gather(shards(1)))
    layers[1] = layer_weights(1, *gw1)
    h, saved[1], _, _ = _layer_fwd(h, layers[1], tabs, 1)
    loss_vec, dh, g_final = _loss_head(h, final_g, loss_target[0], name="loss_head")

    grads = [None] * DEPTH
    dh, grads[1], _, _ = _layer_bwd(dh, saved[1], layers[1], tabs, 1)
    dh, grads[0], recv1, recv0 = _layer_bwd(dh, saved[0], layers[0], tabs, 0, side=_Exchange(_grad_chunks(grads[1], "l1")),
                                            own_exchange=True)
    r_in, r_uq, r_ukv, r_br, r_out = (jnp.stack([a, b], axis=1) for a, b in zip(recv0, recv1))

    def stack(name):
        return jnp.stack([grads[l][name] for l in range(DEPTH)])

    small = _pack_small(dict(norm_g=stack("norm_g"), b_f=stack("b_f"), g_cq=stack("g_cq"), g_ckv=stack("g_ckv"),
                             sinks=stack("sinks"), final_g=g_final, loss=loss_vec[0, 0:1],
                             meta=dh[PAD:BLK]))
    (g_small,) = _comm(_Gather([small]), name="gather_small")

    def adam_big(w_, parts, m_, v_, name):
        shape = w_.shape
        C = shape[-1]
        R = math.prod(shape[:-1])
        outs = _adamw(w_.reshape(R, C), parts.reshape(parts.shape[0], R, C), m_.reshape(R, C), v_.reshape(R, C),
                      name=name)
        return [o.reshape(shape) for o in outs]

    res = {}
    res["w_in"] = adam_big(w_in, r_in, m_w_in, v_w_in, "adam_w_in")
    res["w_uq"] = adam_big(w_uq, r_uq, m_w_uq, v_w_uq, "adam_w_uq")
    res["w_ukv"] = adam_big(w_ukv, r_ukv, m_w_ukv, v_w_ukv, "adam_w_ukv")
    res["w_branch"] = adam_big(w_branch, r_br, m_w_branch, v_w_branch, "adam_w_branch")
    res["w_out"] = adam_big(w_out, r_out, m_w_out, v_w_out, "adam_w_out")

    small_w = dict(norm_g=norm_g, b_f=b_f, g_cq=g_cq, g_ckv=g_ckv, sinks=sinks, final_g=final_g)
    small_m = dict(norm_g=m_norm_g, b_f=m_b_f, g_cq=m_g_cq, g_ckv=m_g_ckv, sinks=m_sinks, final_g=m_final_g)
    small_v = dict(norm_g=v_norm_g, b_f=v_b_f, g_cq=v_g_cq, g_ckv=v_g_ckv, sinks=v_sinks, final_g=v_final_g)
    sm = _adamw(_pack_small(small_w), g_small, _pack_small(small_m), _pack_small(small_v), name="adam_small")
    shapes = {k: a.shape for k, a in small_w.items()}
    shapes_all = dict(shapes, loss=(), meta=(N_META, D_MODEL))
    sm_g = _unpack_small(sm[0], shapes_all)
    sm_d, sm_m, sm_v = (_unpack_small(t, shapes) for t in sm[1:])
    for k in shapes:
        res[k] = [sm_g[k], sm_d[k], sm_m[k], sm_v[k]]
    g_meta_mine = lax.dynamic_slice(sm_g["meta"], (0, my_idx * 128), (N_META, 128))
    res["meta_tokens"] = _adamw(meta_tokens, g_meta_mine[None], m_meta_tokens, v_meta_tokens, name="adam_meta")

    order = ["meta_tokens", "norm_g", "w_in", "b_f", "g_cq", "g_ckv", "w_uq", "w_ukv", "sinks", "w_branch", "w_out",
             "final_g"]
    grad_x = dh[BLK:][None]
    return (sm_g["loss"], grad_x, *[res[k][0] for k in order], *[res[k][1] for k in order],
            *[res[k][2] for k in order], *[res[k][3] for k in order])
```
